```python
import math
import jax, jax.numpy as jnp
from jax import lax
import numpy as np

D_MODEL = 1024
BATCH = 8
SEQ = 8192
DEPTH = 1

D_MIX = D_MODEL
N_HEADS_A = 8
HEAD_DIM_A = 64
DIL_CONFIGS = ((128, 1), (512, 4), (2048, 16))
Q_BLOCK = 128
N_HEADS_B = 8
QK_NOPE = 64
QK_ROPE = 32
V_DIM = 64
Q_LORA = 384
KV_LORA = 256
ROPE_BASE = 10000.0
D_FF = 2816
CONV_WIDTH = 3
EPS = 1e-6
NEG = -1e30

WIDTH_A = N_HEADS_A * HEAD_DIM_A
WIDTH_B = N_HEADS_B * V_DIM
IN_SIZES = (WIDTH_A, WIDTH_A, WIDTH_A, Q_LORA, KV_LORA, QK_ROPE)
D_IN = sum(IN_SIZES)
SPLIT_POINTS = tuple(int(v) for v in np.cumsum(IN_SIZES)[:-1])

kernel_name = "hybrid_dilated_swa_mla_convffn_sandwich"


def _rmsnorm(x, g):
    xf = x.astype(jnp.float32)
    y = xf * lax.rsqrt(jnp.mean(xf * xf, axis=-1, keepdims=True) + EPS)
    return (y * g.astype(jnp.float32)).astype(x.dtype)


def _rope(x, cos, sin):
    xf = x.astype(jnp.float32)
    half = xf.shape[-1] // 2
    x1, x2 = xf[..., :half], xf[..., half:]
    out = jnp.concatenate([x1 * cos - x2 * sin, x2 * cos + x1 * sin], axis=-1)
    return out.astype(x.dtype)


def _dilated_branch(q, k, v, slopes, window, dilation):
    b, h, s, d = q.shape
    r = dilation
    half = window // (2 * dilation)
    L = s // r
    qb = min(Q_BLOCK, L)
    nblk = -(-L // qb)
    Lp = nblk * qb
    slab = qb + 2 * half

    def to_dilated(t):
        return t.reshape(b, h, L, r, d).transpose(0, 1, 3, 2, 4)

    qd, kd, vd = to_dilated(q), to_dilated(k), to_dilated(v)
    qd = jnp.pad(qd, ((0, 0), (0, 0), (0, 0), (0, Lp - L), (0, 0)))
    pad_kv = ((0, 0), (0, 0), (0, 0), (half, half + Lp - L), (0, 0))
    kd = jnp.pad(kd, pad_kv)
    vd = jnp.pad(vd, pad_kv)
    idx = jnp.arange(nblk)[:, None] * qb + jnp.arange(slab)[None, :]
    k_s = kd[:, :, :, idx, :].astype(jnp.float32)
    v_s = vd[:, :, :, idx, :].astype(jnp.float32)
    q_s = qd.reshape(b, h, r, nblk, qb, d).astype(jnp.float32)

    scores = jnp.einsum('bhcnqd,bhcnkd->bhcnqk', q_s, k_s) * (d ** -0.5)
    off = jnp.arange(slab)[None, :] - half - jnp.arange(qb)[:, None]
    key_pos = idx - half
    valid = (jnp.abs(off) <= half)[None, :, :] & ((key_pos >= 0) & (key_pos < L))[:, None, :]
    dist = (jnp.abs(off) * r).astype(jnp.float32)
    alibi = -slopes[:, None, None] * dist[None]
    scores = scores + alibi[None, :, None, None]
    scores = jnp.where(valid[None, None, None], scores, NEG)
    lse = jax.nn.logsumexp(scores, axis=-1)
    p = jnp.exp(scores - lse[..., None])
    o = jnp.einsum('bhcnqk,bhcnkd->bhcnqd', p, v_s)

    o = o.reshape(b, h, r, Lp, d)[:, :, :, :L].transpose(0, 1, 3, 2, 4).reshape(b, h, s, d)
    lse = lse.reshape(b, h, r, Lp)[:, :, :, :L].transpose(0, 1, 3, 2).reshape(b, h, s)
    return o, lse


def _dilated_attention(qa, ka, va):
    b, s, _ = qa.shape
    def heads(t):
        return t.reshape(b, s, N_HEADS_A, HEAD_DIM_A).transpose(0, 2, 1, 3)
    q, k, v = heads(qa), heads(ka), heads(va)
    slopes = jnp.exp2(-8.0 * jnp.arange(1, N_HEADS_A + 1, dtype=jnp.float32) / N_HEADS_A)
    outs, lses = [], []
    for window, dilation in DIL_CONFIGS:
        o, lse = _dilated_branch(q, k, v, slopes, window, dilation)
        outs.append(o)
        lses.append(lse)
    w = jax.nn.softmax(jnp.stack(lses, axis=0), axis=0)
    o = jnp.einsum('gbhs,gbhsd->bhsd', w, jnp.stack(outs, axis=0))
    return o.transpose(0, 2, 1, 3).reshape(b, s, WIDTH_A).astype(qa.dtype)


def _mla(c_q, c_kv, k_r, q_lat_norm, w_uq, kv_lat_norm, w_ukv):
    b, s, _ = c_q.shape
    pos = jnp.arange(s, dtype=jnp.float32)
    inv_freq = jnp.exp(-math.log(ROPE_BASE) * jnp.arange(0, QK_ROPE, 2, dtype=jnp.float32) / QK_ROPE)
    ang = pos[:, None] * inv_freq[None, :]
    cos, sin = jnp.cos(ang), jnp.sin(ang)

    q = (_rmsnorm(c_q, q_lat_norm) @ w_uq).reshape(b, s, N_HEADS_B, QK_NOPE + QK_ROPE)
    q_nope = q[..., :QK_NOPE]
    q_rope = _rope(q[..., QK_NOPE:], cos[:, None, :], sin[:, None, :])
    kv = (_rmsnorm(c_kv, kv_lat_norm) @ w_ukv).reshape(b, s, N_HEADS_B, QK_NOPE + V_DIM)
    k_nope = kv[..., :QK_NOPE].transpose(0, 2, 1, 3).astype(jnp.float32)
    v = kv[..., QK_NOPE:].transpose(0, 2, 1, 3).astype(jnp.float32)
    k_rope = _rope(k_r, cos, sin).astype(jnp.float32)
    scale = (QK_NOPE + QK_ROPE) ** -0.5

    nq = s // Q_BLOCK
    qn_blk = q_nope.reshape(b, nq, Q_BLOCK, N_HEADS_B, QK_NOPE).transpose(1, 0, 3, 2, 4)
    qr_blk = q_rope.reshape(b, nq, Q_BLOCK, N_HEADS_B, QK_ROPE).transpose(1, 0, 3, 2, 4)

    def block(args):
        qn, qr = args
        sc = (jnp.einsum('bhqd,bhkd->bhqk', qn.astype(jnp.float32), k_nope)
              + jnp.einsum('bhqr,bkr->bhqk', qr.astype(jnp.float32), k_rope)) * scale
        p = jax.nn.softmax(sc, axis=-1)
        return jnp.einsum('bhqk,bhkd->bhqd', p, v)

    o = lax.map(block, (qn_blk, qr_blk))
    return o.transpose(1, 0, 3, 2, 4).reshape(b, s, WIDTH_B).astype(c_q.dtype)


def _dwconv(u, w, bias):
    c = u.shape[-1]
    y = lax.conv_general_dilated(
        u, w[:, None, :].astype(u.dtype), window_strides=(1,),
        padding=((CONV_WIDTH // 2, CONV_WIDTH // 2),),
        dimension_numbers=('NWC', 'WIO', 'NWC'), feature_group_count=c)
    return y + bias.astype(u.dtype)


def _fwd_setup_inputs(seed: int = 0) -> dict:
    key = jax.random.key(seed)
    ks = jax.random.split(key, 17)
    f32 = jnp.float32

    def gain(k, n):
        return 1.0 + 0.1 * jax.random.normal(k, (DEPTH, n), f32)

    def dense(k, fan_in, fan_out):
        return jax.random.normal(k, (DEPTH, fan_in, fan_out), f32) * fan_in ** -0.5

    return {
        "x": jax.random.normal(ks[0], (BATCH, SEQ, D_MODEL), f32),
        "norm_mix_pre": gain(ks[1], D_MODEL),
        "w_in": dense(ks[2], D_MODEL, D_IN),
        "q_lat_norm": gain(ks[3], Q_LORA),
        "w_uq": dense(ks[4], Q_LORA, N_HEADS_B * (QK_NOPE + QK_ROPE)),
        "kv_lat_norm": gain(ks[5], KV_LORA),
        "w_ukv": dense(ks[6], KV_LORA, N_HEADS_B * (QK_NOPE + V_DIM)),
        "out_norm_a": gain(ks[7], WIDTH_A),
        "out_norm_b": gain(ks[8], WIDTH_B),
        "w_o": dense(ks[9], D_MIX, D_MODEL),
        "norm_mix_post": gain(ks[10], D_MODEL),
        "norm_ffn_pre": gain(ks[11], D_MODEL),
        "w_up": dense(ks[12], D_MODEL, 2 * D_FF),
        "conv_w": jax.random.normal(ks[13], (DEPTH, CONV_WIDTH, 2 * D_FF), f32) * CONV_WIDTH ** -0.5,
        "conv_b": 0.02 * jax.random.normal(ks[14], (DEPTH, 2 * D_FF), f32),
        "w_down": dense(ks[15], D_FF, D_MODEL),
        "norm_ffn_post": gain(ks[16], D_MODEL),
    }


def _fwd_reference(x, norm_mix_pre, w_in, q_lat_norm, w_uq, kv_lat_norm, w_ukv, out_norm_a,
              out_norm_b, w_o, norm_mix_post, norm_ffn_pre, w_up, conv_w, conv_b, w_down,
              norm_ffn_post):
    for l in range(DEPTH):
        h = _rmsnorm(x, norm_mix_pre[l])
        proj = h @ w_in[l]
        qa, ka, va, c_q, c_kv, k_r = jnp.split(proj, SPLIT_POINTS, axis=-1)
        ya = _dilated_attention(qa, ka, va)
        yb = _mla(c_q, c_kv, k_r, q_lat_norm[l], w_uq[l], kv_lat_norm[l], w_ukv[l])
        y = jnp.concatenate([_rmsnorm(ya, out_norm_a[l]), _rmsnorm(yb, out_norm_b[l])], axis=-1)
        y = y @ w_o[l]
        x = x + _rmsnorm(y, norm_mix_post[l])
        h = _rmsnorm(x, norm_ffn_pre[l])
        u = _dwconv(h @ w_up[l], conv_w[l], conv_b[l])
        g, v = u[..., :D_FF], u[..., D_FF:]
        y = (jax.nn.gelu(g, approximate=True) * v) @ w_down[l]
        x = x + _rmsnorm(y, norm_ffn_post[l])
    return x


import jax as _jax
import jax.numpy as _jnp

TWIN_FORMAT = 'train_step'
FWD_PARAMS = ['x', 'norm_mix_pre', 'w_in', 'q_lat_norm', 'w_uq', 'kv_lat_norm', 'w_ukv', 'out_norm_a', 'out_norm_b', 'w_o', 'norm_mix_post', 'norm_ffn_pre', 'w_up', 'conv_w', 'conv_b', 'w_down', 'norm_ffn_post']
TWIN_WEIGHTS = ['norm_mix_pre', 'w_in', 'q_lat_norm', 'w_uq', 'kv_lat_norm', 'w_ukv', 'out_norm_a', 'out_norm_b', 'w_o', 'norm_mix_post', 'norm_ffn_pre', 'w_up', 'conv_w', 'conv_b', 'w_down', 'norm_ffn_post']
TWIN_DIFF_INPUT = 'x'
TWIN_INPUTS = ['x', 'norm_mix_pre', 'w_in', 'q_lat_norm', 'w_uq', 'kv_lat_norm', 'w_ukv', 'out_norm_a', 'out_norm_b', 'w_o', 'norm_mix_post', 'norm_ffn_pre', 'w_up', 'conv_w', 'conv_b', 'w_down', 'norm_ffn_post', 'loss_target', 'm_norm_mix_pre', 'm_w_in', 'm_q_lat_norm', 'm_w_uq', 'm_kv_lat_norm', 'm_w_ukv', 'm_out_norm_a', 'm_out_norm_b', 'm_w_o', 'm_norm_mix_post', 'm_norm_ffn_pre', 'm_w_up', 'm_conv_w', 'm_conv_b', 'm_w_down', 'm_norm_ffn_post', 'v_norm_mix_pre', 'v_w_in', 'v_q_lat_norm', 'v_w_uq', 'v_kv_lat_norm', 'v_w_ukv', 'v_out_norm_a', 'v_out_norm_b', 'v_w_o', 'v_norm_mix_post', 'v_norm_ffn_pre', 'v_w_up', 'v_conv_w', 'v_conv_b', 'v_w_down', 'v_norm_ffn_post']
TWIN_OUTPUTS = ['loss', 'grad_x', 'grad_norm_mix_pre', 'grad_w_in', 'grad_q_lat_norm', 'grad_w_uq', 'grad_kv_lat_norm', 'grad_w_ukv', 'grad_out_norm_a', 'grad_out_norm_b', 'grad_w_o', 'grad_norm_mix_post', 'grad_norm_ffn_pre', 'grad_w_up', 'grad_conv_w', 'grad_conv_b', 'grad_w_down', 'grad_norm_ffn_post', 'delta_norm_mix_pre', 'delta_w_in', 'delta_q_lat_norm', 'delta_w_uq', 'delta_kv_lat_norm', 'delta_w_ukv', 'delta_out_norm_a', 'delta_out_norm_b', 'delta_w_o', 'delta_norm_mix_post', 'delta_norm_ffn_pre', 'delta_w_up', 'delta_conv_w', 'delta_conv_b', 'delta_w_down', 'delta_norm_ffn_post', 'new_m_norm_mix_pre', 'new_m_w_in', 'new_m_q_lat_norm', 'new_m_w_uq', 'new_m_kv_lat_norm', 'new_m_w_ukv', 'new_m_out_norm_a', 'new_m_out_norm_b', 'new_m_w_o', 'new_m_norm_mix_post', 'new_m_norm_ffn_pre', 'new_m_w_up', 'new_m_conv_w', 'new_m_conv_b', 'new_m_w_down', 'new_m_norm_ffn_post', 'new_v_norm_mix_pre', 'new_v_w_in', 'new_v_q_lat_norm', 'new_v_w_uq', 'new_v_kv_lat_norm', 'new_v_w_ukv', 'new_v_out_norm_a', 'new_v_out_norm_b', 'new_v_w_o', 'new_v_norm_mix_post', 'new_v_norm_ffn_pre', 'new_v_w_up', 'new_v_conv_w', 'new_v_conv_b', 'new_v_w_down', 'new_v_norm_ffn_post']
TWIN_LEAF_KINDS = {'loss': 'loss', 'grad_x': 'grad_x', 'grad_norm_mix_pre': 'grad_w', 'grad_w_in': 'grad_w', 'grad_q_lat_norm': 'grad_w', 'grad_w_uq': 'grad_w', 'grad_kv_lat_norm': 'grad_w', 'grad_w_ukv': 'grad_w', 'grad_out_norm_a': 'grad_w', 'grad_out_norm_b': 'grad_w', 'grad_w_o': 'grad_w', 'grad_norm_mix_post': 'grad_w', 'grad_norm_ffn_pre': 'grad_w', 'grad_w_up': 'grad_w', 'grad_conv_w': 'grad_w', 'grad_conv_b': 'grad_w', 'grad_w_down': 'grad_w', 'grad_norm_ffn_post': 'grad_w', 'delta_norm_mix_pre': 'delta_w', 'delta_w_in': 'delta_w', 'delta_q_lat_norm': 'delta_w', 'delta_w_uq': 'delta_w', 'delta_kv_lat_norm': 'delta_w', 'delta_w_ukv': 'delta_w', 'delta_out_norm_a': 'delta_w', 'delta_out_norm_b': 'delta_w', 'delta_w_o': 'delta_w', 'delta_norm_mix_post': 'delta_w', 'delta_norm_ffn_pre': 'delta_w', 'delta_w_up': 'delta_w', 'delta_conv_w': 'delta_w', 'delta_conv_b': 'delta_w', 'delta_w_down': 'delta_w', 'delta_norm_ffn_post': 'delta_w', 'new_m_norm_mix_pre': 'new_m', 'new_m_w_in': 'new_m', 'new_m_q_lat_norm': 'new_m', 'new_m_w_uq': 'new_m', 'new_m_kv_lat_norm': 'new_m', 'new_m_w_ukv': 'new_m', 'new_m_out_norm_a': 'new_m', 'new_m_out_norm_b': 'new_m', 'new_m_w_o': 'new_m', 'new_m_norm_mix_post': 'new_m', 'new_m_norm_ffn_pre': 'new_m', 'new_m_w_up': 'new_m', 'new_m_conv_w': 'new_m', 'new_m_conv_b': 'new_m', 'new_m_w_down': 'new_m', 'new_m_norm_ffn_post': 'new_m', 'new_v_norm_mix_pre': 'new_v', 'new_v_w_in': 'new_v', 'new_v_q_lat_norm': 'new_v', 'new_v_w_uq': 'new_v', 'new_v_kv_lat_norm': 'new_v', 'new_v_w_ukv': 'new_v', 'new_v_out_norm_a': 'new_v', 'new_v_out_norm_b': 'new_v', 'new_v_w_o': 'new_v', 'new_v_norm_mix_post': 'new_v', 'new_v_norm_ffn_pre': 'new_v', 'new_v_w_up': 'new_v', 'new_v_conv_w': 'new_v', 'new_v_conv_b': 'new_v', 'new_v_w_down': 'new_v', 'new_v_norm_ffn_post': 'new_v'}


def _forward(args):
    return _fwd_reference(*[args[k] for k in FWD_PARAMS])


def _output_shape():
    def fwd():
        inp = _fwd_setup_inputs(0)
        return _fwd_reference(*[inp[k] for k in FWD_PARAMS])
    out = _jax.eval_shape(fwd)
    return out.shape, out.dtype

N_MICROBATCH = 1
ADAM_LR = 0.001
ADAM_B1 = 0.9
ADAM_B2 = 0.999
ADAM_EPS = 1e-08
ADAM_WD = 0.01
ADAM_STEP = 10
PER_EXAMPLE_BATCH_AXIS = {'x': 0, 'loss_target': 0}
SHARED_INPUTS = []
_WEIGHT_DTYPES = {'norm_mix_pre': _jnp.float32, 'w_in': _jnp.float32, 'q_lat_norm': _jnp.float32, 'w_uq': _jnp.float32, 'kv_lat_norm': _jnp.float32, 'w_ukv': _jnp.float32, 'out_norm_a': _jnp.float32, 'out_norm_b': _jnp.float32, 'w_o': _jnp.float32, 'norm_mix_post': _jnp.float32, 'norm_ffn_pre': _jnp.float32, 'w_up': _jnp.float32, 'conv_w': _jnp.float32, 'conv_b': _jnp.float32, 'w_down': _jnp.float32, 'norm_ffn_post': _jnp.float32}
MOMENT_SCALE = {'norm_mix_pre': 1.520927e+00, 'w_in': 9.152531e-01, 'q_lat_norm': 2.035050e+00, 'w_uq': 1.123846e+00, 'kv_lat_norm': 4.616520e+00, 'w_ukv': 1.375804e+00, 'out_norm_a': 1.439061e+00, 'out_norm_b': 1.365481e+00, 'w_o': 1.340465e+00, 'norm_mix_post': 6.418381e+01, 'norm_ffn_pre': 8.850507e-01, 'w_up': 3.918351e-01, 'conv_w': 4.851456e-01, 'conv_b': 1.111107e+00, 'w_down': 8.648561e-01, 'norm_ffn_post': 6.445398e+01}


def _to_microbatches(a, axis):
    t = _jnp.moveaxis(a, axis, 0)
    t = t.reshape((N_MICROBATCH, t.shape[0] // N_MICROBATCH) + t.shape[1:])
    return _jnp.moveaxis(t, 1, axis + 1)


def setup_inputs(seed: int = 0) -> dict:
    inp = _fwd_setup_inputs(seed)
    key = _jax.random.fold_in(_jax.random.key(seed), 7919)
    shape, _ = _output_shape()
    out = dict(inp)
    out["loss_target"] = _jax.random.normal(_jax.random.fold_in(key, 0), shape, _jnp.float32)
    for i, name in enumerate(TWIN_WEIGHTS):
        w = inp[name].astype(_jnp.float32)
        if MOMENT_SCALE is None:
            s = _jnp.sqrt(_jnp.mean(_jnp.square(w)) + 1e-30)
        else:
            s = MOMENT_SCALE[name]
        km, kv = _jax.random.split(_jax.random.fold_in(key, i + 1))
        out[name] = w
        out["m_" + name] = s * _jax.random.normal(km, w.shape, _jnp.float32)
        out["v_" + name] = (s * s) * _jax.random.uniform(kv, w.shape, _jnp.float32, 0.5, 1.5)
    if N_MICROBATCH > 1:
        for name, axis in PER_EXAMPLE_BATCH_AXIS.items():
            out[name] = _to_microbatches(out[name], axis)
    return {'x': out['x'], 'norm_mix_pre': out['norm_mix_pre'], 'w_in': out['w_in'], 'q_lat_norm': out['q_lat_norm'], 'w_uq': out['w_uq'], 'kv_lat_norm': out['kv_lat_norm'], 'w_ukv': out['w_ukv'], 'out_norm_a': out['out_norm_a'], 'out_norm_b': out['out_norm_b'], 'w_o': out['w_o'], 'norm_mix_post': out['norm_mix_post'], 'norm_ffn_pre': out['norm_ffn_pre'], 'w_up': out['w_up'], 'conv_w': out['conv_w'], 'conv_b': out['conv_b'], 'w_down': out['w_down'], 'norm_ffn_post': out['norm_ffn_post'], 'loss_target': out['loss_target'], 'm_norm_mix_pre': out['m_norm_mix_pre'], 'm_w_in': out['m_w_in'], 'm_q_lat_norm': out['m_q_lat_norm'], 'm_w_uq': out['m_w_uq'], 'm_kv_lat_norm': out['m_kv_lat_norm'], 'm_w_ukv': out['m_w_ukv'], 'm_out_norm_a': out['m_out_norm_a'], 'm_out_norm_b': out['m_out_norm_b'], 'm_w_o': out['m_w_o'], 'm_norm_mix_post': out['m_norm_mix_post'], 'm_norm_ffn_pre': out['m_norm_ffn_pre'], 'm_w_up': out['m_w_up'], 'm_conv_w': out['m_conv_w'], 'm_conv_b': out['m_conv_b'], 'm_w_down': out['m_w_down'], 'm_norm_ffn_post': out['m_norm_ffn_post'], 'v_norm_mix_pre': out['v_norm_mix_pre'], 'v_w_in': out['v_w_in'], 'v_q_lat_norm': out['v_q_lat_norm'], 'v_w_uq': out['v_w_uq'], 'v_kv_lat_norm': out['v_kv_lat_norm'], 'v_w_ukv': out['v_w_ukv'], 'v_out_norm_a': out['v_out_norm_a'], 'v_out_norm_b': out['v_out_norm_b'], 'v_w_o': out['v_w_o'], 'v_norm_mix_post': out['v_norm_mix_post'], 'v_norm_ffn_pre': out['v_norm_ffn_pre'], 'v_w_up': out['v_w_up'], 'v_conv_w': out['v_conv_w'], 'v_conv_b': out['v_conv_b'], 'v_w_down': out['v_w_down'], 'v_norm_ffn_post': out['v_norm_ffn_post']}


def _loss(weights, diff, rest, loss_target):
    with _jax.named_scope("forward"):
        args = {**rest, TWIN_DIFF_INPUT: diff, **{k: w.astype(_WEIGHT_DTYPES[k]) for k, w in weights.items()}}
        y = _forward(args)
    with _jax.named_scope("loss_head"):
        err = _jnp.square(y.astype(_jnp.float32) - loss_target)
        return 0.5 * _jnp.sum(_jnp.mean(err, axis=-1)) if err.ndim else 0.5 * err


def _adamw(w, g, m, v):
    m = ADAM_B1 * m + (1.0 - ADAM_B1) * g
    v = ADAM_B2 * v + (1.0 - ADAM_B2) * _jnp.square(g)
    m_hat = m / (1.0 - ADAM_B1 ** ADAM_STEP)
    v_hat = v / (1.0 - ADAM_B2 ** ADAM_STEP)
    delta = -ADAM_LR * (m_hat / (_jnp.sqrt(v_hat) + ADAM_EPS) + ADAM_WD * w)
    return delta, m, v


def reference(x, norm_mix_pre, w_in, q_lat_norm, w_uq, kv_lat_norm, w_ukv, out_norm_a, out_norm_b, w_o, norm_mix_post, norm_ffn_pre, w_up, conv_w, conv_b, w_down, norm_ffn_post, loss_target, m_norm_mix_pre, m_w_in, m_q_lat_norm, m_w_uq, m_kv_lat_norm, m_w_ukv, m_out_norm_a, m_out_norm_b, m_w_o, m_norm_mix_post, m_norm_ffn_pre, m_w_up, m_conv_w, m_conv_b, m_w_down, m_norm_ffn_post, v_norm_mix_pre, v_w_in, v_q_lat_norm, v_w_uq, v_kv_lat_norm, v_w_ukv, v_out_norm_a, v_out_norm_b, v_w_o, v_norm_mix_post, v_norm_ffn_pre, v_w_up, v_conv_w, v_conv_b, v_w_down, v_norm_ffn_post):
    given = dict(x=x, norm_mix_pre=norm_mix_pre, w_in=w_in, q_lat_norm=q_lat_norm, w_uq=w_uq, kv_lat_norm=kv_lat_norm, w_ukv=w_ukv, out_norm_a=out_norm_a, out_norm_b=out_norm_b, w_o=w_o, norm_mix_post=norm_mix_post, norm_ffn_pre=norm_ffn_pre, w_up=w_up, conv_w=conv_w, conv_b=conv_b, w_down=w_down, norm_ffn_post=norm_ffn_post, loss_target=loss_target, m_norm_mix_pre=m_norm_mix_pre, m_w_in=m_w_in, m_q_lat_norm=m_q_lat_norm, m_w_uq=m_w_uq, m_kv_lat_norm=m_kv_lat_norm, m_w_ukv=m_w_ukv, m_out_norm_a=m_out_norm_a, m_out_norm_b=m_out_norm_b, m_w_o=m_w_o, m_norm_mix_post=m_norm_mix_post, m_norm_ffn_pre=m_norm_ffn_pre, m_w_up=m_w_up, m_conv_w=m_conv_w, m_conv_b=m_conv_b, m_w_down=m_w_down, m_norm_ffn_post=m_norm_ffn_post, v_norm_mix_pre=v_norm_mix_pre, v_w_in=v_w_in, v_q_lat_norm=v_q_lat_norm, v_w_uq=v_w_uq, v_kv_lat_norm=v_kv_lat_norm, v_w_ukv=v_w_ukv, v_out_norm_a=v_out_norm_a, v_out_norm_b=v_out_norm_b, v_w_o=v_w_o, v_norm_mix_post=v_norm_mix_post, v_norm_ffn_pre=v_norm_ffn_pre, v_w_up=v_w_up, v_conv_w=v_conv_w, v_conv_b=v_conv_b, v_w_down=v_w_down, v_norm_ffn_post=v_norm_ffn_post)
    weights = {n: given[n] for n in TWIN_WEIGHTS}
    shared = {n: given[n] for n in SHARED_INPUTS}
    per_example = {n: given[n] for n in ['x']}
    grad_fn = _jax.value_and_grad(_loss, argnums=(0, 1))

    def one_microbatch(ex, loss_target):
        ex = dict(ex)
        diff = ex.pop(TWIN_DIFF_INPUT)
        return grad_fn(weights, diff, {**shared, **ex}, loss_target)

    if N_MICROBATCH == 1:
        loss, (grad_w, grad_x) = one_microbatch(per_example, given["loss_target"])
    else:
        def body(carry, xs):
            loss_sum, grad_sum = carry
            l_k, (gw_k, gx_k) = one_microbatch(xs[0], xs[1])
            with _jax.named_scope("update"):
                return (loss_sum + l_k, _jax.tree.map(_jnp.add, grad_sum, gw_k)), gx_k

        init = (_jnp.zeros((), _jnp.float32), _jax.tree.map(_jnp.zeros_like, weights))
        (loss, grad_w), grad_x = _jax.lax.scan(body, init, (per_example, given["loss_target"]))
    with _jax.named_scope("update"):
        delta_w, new_m, new_v = {}, {}, {}
        for n in TWIN_WEIGHTS:
            delta_w[n], new_m[n], new_v[n] = _adamw(weights[n], grad_w[n], given["m_" + n], given["v_" + n])
    return (loss, grad_x, *[grad_w[n] for n in TWIN_WEIGHTS], *[delta_w[n] for n in TWIN_WEIGHTS],
            *[new_m[n] for n in TWIN_WEIGHTS], *[new_v[n] for n in TWIN_WEIGHTS])
```

```python
import functools
import math

import jax
import jax.numpy as jnp
import numpy as np
from jax import lax
from jax.experimental import pallas as pl
from jax.experimental.pallas import tpu as pltpu

F32 = jnp.float32
BF16 = jnp.bfloat16

LANES = 128
D_MODEL = 1024
N_HEADS = 8
HEAD_DIM = 64
QK_ROPE = 32
Q_LORA = 384
KV_LORA = 256
D_FF = 2816
WIDTH = N_HEADS * HEAD_DIM
WIDTH_P = N_HEADS * LANES
IN_SIZES = (WIDTH, WIDTH, WIDTH, Q_LORA, KV_LORA, QK_ROPE)
D_IN = sum(IN_SIZES)
TAIL_P = Q_LORA + KV_LORA + LANES
EPS = 1e-6
ROPE_BASE = 10000.0
MASKED = -2e30
M_INIT = -1e30
BAND_BLOCK = 256
BAND_W = 4
MLA_SCALE = (HEAD_DIM + QK_ROPE) ** -0.5
DIL_SCALE = HEAD_DIM ** -0.5

ADAM_LR = 0.001
ADAM_B1 = 0.9
ADAM_B2 = 0.999
ADAM_EPS = 1e-08
ADAM_WD = 0.01
ADAM_STEP = 10

VMEM_LIMIT = 56 * 1024 * 1024

N_CHIPS = 4

WEIGHTS = ['norm_mix_pre', 'w_in', 'q_lat_norm', 'w_uq', 'kv_lat_norm', 'w_ukv', 'out_norm_a', 'out_norm_b',
           'w_o', 'norm_mix_post', 'norm_ffn_pre', 'w_up', 'conv_w', 'conv_b', 'w_down', 'norm_ffn_post']
SHARD_AXIS = {'norm_mix_pre': None, 'w_in': 1, 'q_lat_norm': None, 'w_uq': 1, 'kv_lat_norm': None, 'w_ukv': 1,
              'out_norm_a': None, 'out_norm_b': None, 'w_o': 0, 'norm_mix_post': None, 'norm_ffn_pre': None,
              'w_up': 1, 'conv_w': 1, 'conv_b': None, 'w_down': 0, 'norm_ffn_post': None}
BIG = ['w_in', 'w_uq', 'w_ukv', 'w_o', 'w_up', 'w_down']
SMALL = [n for n in WEIGHTS if SHARD_AXIS[n] is None]


def _tile(dim, target):
    best = None
    t = LANES
    while t <= min(dim, target):
        if dim % t == 0:
            best = t
        t += LANES
    return best if best is not None else dim


def _cparams(sem=None):
    return pltpu.CompilerParams(dimension_semantics=sem, vmem_limit_bytes=VMEM_LIMIT)


def _mm(a, b, mode, out_dtype, name, add=None, tm=1024, tn=1024, tk=1024):
    if mode == 'nn':
        (M, K), (K2, N) = a.shape, b.shape
        dims = (((1,), (0,)), ((), ()))
    elif mode == 'nt':
        (M, K), (N, K2) = a.shape, b.shape
        dims = (((1,), (1,)), ((), ()))
    else:
        (K, M), (K2, N) = a.shape, b.shape
        dims = (((0,), (0,)), ((), ()))
    assert K == K2, (a.shape, b.shape, mode)
    tm, tn, tk = _tile(M, tm), _tile(N, tn), _tile(K, tk)
    nk = K // tk
    if mode == 'nn':
        a_spec = pl.BlockSpec((tm, tk), lambda i, j, k: (i, k))
        b_spec = pl.BlockSpec((tk, tn), lambda i, j, k: (k, j))
    elif mode == 'nt':
        a_spec = pl.BlockSpec((tm, tk), lambda i, j, k: (i, k))
        b_spec = pl.BlockSpec((tn, tk), lambda i, j, k: (j, k))
    else:
        a_spec = pl.BlockSpec((tk, tm), lambda i, j, k: (k, i))
        b_spec = pl.BlockSpec((tk, tn), lambda i, j, k: (k, j))
    o_spec = pl.BlockSpec((tm, tn), lambda i, j, k: (i, j))
    has_add = add is not None

    def body(*refs):
        if has_add:
            a_ref, b_ref, add_ref, o_ref, acc_ref = refs
        else:
            a_ref, b_ref, o_ref, acc_ref = refs
        k = pl.program_id(2)

        @pl.when(k == 0)
        def _():
            acc_ref[...] = jnp.zeros_like(acc_ref)

        acc_ref[...] += lax.dot_general(a_ref[...].astype(BF16), b_ref[...].astype(BF16), dims,
                                        preferred_element_type=F32)

        @pl.when(k == nk - 1)
        def _():
            r = acc_ref[...]
            if has_add:
                r = r + add_ref[...]
            o_ref[...] = r.astype(o_ref.dtype)

    ins = [a, b] + ([add] if has_add else [])
    in_specs = [a_spec, b_spec] + ([o_spec] if has_add else [])
    return pl.pallas_call(
        body, name=name, grid=(M // tm, N // tn, nk), in_specs=in_specs, out_specs=o_spec,
        out_shape=jax.ShapeDtypeStruct((M, N), out_dtype),
        scratch_shapes=[pltpu.VMEM((tm, tn), F32)],
        compiler_params=_cparams(("parallel", "parallel", "arbitrary")),
    )(*ins)


def _rows(body, name, S, ts, row_ins, full_ins, row_outs, acc_outs=()):
    in_specs = [pl.BlockSpec((ts, a.shape[1]), lambda i: (i, 0)) for a in row_ins]
    in_specs += [pl.BlockSpec(a.shape, lambda i, nd=a.ndim: (0,) * nd) for a in full_ins]
    out_specs = [pl.BlockSpec((ts, w), lambda i: (i, 0)) for (w, _) in row_outs]
    out_specs += [pl.BlockSpec(shape, lambda i, nd=len(shape): (0,) * nd) for (shape, _) in acc_outs]
    out_shape = [jax.ShapeDtypeStruct((S, w), dt) for (w, dt) in row_outs]
    out_shape += [jax.ShapeDtypeStruct(shape, dt) for (shape, dt) in acc_outs]

    def kbody(*refs):
        body(pl.program_id(0), *refs)

    return pl.pallas_call(
        kbody, name=name, grid=(S // ts,), in_specs=in_specs, out_specs=out_specs, out_shape=out_shape,
        compiler_params=_cparams(("arbitrary",)),
    )(*row_ins, *full_ins)


def _acc_add(step, ref, val):
    @pl.when(step == 0)
    def _():
        ref[...] = val

    @pl.when(step != 0)
    def _():
        ref[...] += val


def _rms_fwd(x, g, name):
    S, W = x.shape

    def body(step, x_ref, g_ref, h_ref):
        xv = x_ref[...]
        r = lax.rsqrt(jnp.mean(xv * xv, axis=-1, keepdims=True) + EPS)
        h_ref[...] = (xv * r * g_ref[...]).astype(BF16)

    return _rows(body, name, S, 512, [x], [g], [(W, BF16)])[0]


def _rms_bwd_math(xv, g, dy, width):
    r = lax.rsqrt(jnp.sum(xv * xv, axis=-1, keepdims=True) * (1.0 / width) + EPS)
    xn = xv * r
    dyg = dy * g
    dx = r * (dyg - xn * (jnp.sum(dyg * xn, axis=-1, keepdims=True) * (1.0 / width)))
    return dx, dy * xn


def _rms_bwd(x, g, dys, resid, out_dtype, name):
    S, W = x.shape
    nd = len(dys)
    has_res = resid is not None

    def body(step, *refs):
        x_ref = refs[0]
        dy_refs = refs[1:1 + nd]
        pos = 1 + nd
        res_ref = refs[pos] if has_res else None
        pos += int(has_res)
        g_ref, dx_ref, dg_ref = refs[pos], refs[pos + 1], refs[pos + 2]
        dy = dy_refs[0][...].astype(F32)
        for r_ in dy_refs[1:]:
            dy = dy + r_[...].astype(F32)
        dx, dgr = _rms_bwd_math(x_ref[...], g_ref[...], dy, W)
        if has_res:
            dx = dx + res_ref[...]
        dx_ref[...] = dx.astype(dx_ref.dtype)
        _acc_add(step, dg_ref, jnp.sum(dgr, axis=0, keepdims=True))

    row_ins = [x] + list(dys) + ([resid] if has_res else [])
    dx, dg = _rows(body, name, S, 256, row_ins, [g], [(W, out_dtype)], [((1, W), F32)])
    return dx, dg


def _rope_apply(xv, c, sa, sb):
    return xv * c + pltpu.roll(xv, 16, 1) * sa + pltpu.roll(xv, LANES - 16, 1) * sb


def _rope_transpose(dy, c, sa, sb):
    return dy * c + pltpu.roll(dy * sa, LANES - 16, 1) + pltpu.roll(dy * sb, 16, 1)


def _rope_tables(S):
    pos = jnp.arange(S, dtype=F32)
    inv_freq = jnp.exp(-math.log(ROPE_BASE) * jnp.arange(0, QK_ROPE, 2, dtype=F32) / QK_ROPE)
    ang = pos[:, None] * inv_freq[None, :]
    cos, sin = jnp.cos(ang), jnp.sin(ang)
    ones, zeros = jnp.ones((S, HEAD_DIM), F32), jnp.zeros((S, HEAD_DIM), F32)
    z16, z32 = jnp.zeros((S, 16), F32), jnp.zeros((S, 32), F32)
    c = jnp.concatenate([ones, cos, cos, z32], axis=1)
    sa = jnp.concatenate([zeros, z16, sin, z32], axis=1)
    sb = jnp.concatenate([zeros, -sin, z16, z32], axis=1)
    return c, sa, sb


def _mla_prep(proj_b, g_q, g_kv, tabs):
    S = proj_b.shape[0]

    def body(step, p_ref, c_ref, sa_ref, sb_ref, gq_ref, gkv_ref, cq_ref, ckv_ref, kr_ref):
        cq = p_ref[:, 0:Q_LORA]
        ckv = p_ref[:, Q_LORA:Q_LORA + KV_LORA]
        kr = p_ref[:, Q_LORA + KV_LORA:TAIL_P]
        rq = lax.rsqrt(jnp.mean(cq * cq, axis=-1, keepdims=True) + EPS)
        cq_ref[...] = (cq * rq * gq_ref[...]).astype(BF16)
        rk = lax.rsqrt(jnp.mean(ckv * ckv, axis=-1, keepdims=True) + EPS)
        ckv_ref[...] = (ckv * rk * gkv_ref[...]).astype(BF16)
        kr_ref[...] = _rope_apply(kr, c_ref[...], sa_ref[...], sb_ref[...])

    return _rows(body, "mla_prep", S, 512, [proj_b, *tabs], [g_q, g_kv],
                 [(Q_LORA, BF16), (KV_LORA, BF16), (LANES, F32)])


def _mla_qkv(q, kv, kr, tabs):
    S = q.shape[0]

    def body(step, q_ref, kv_ref, kr_ref, c_ref, sa_ref, sb_ref, qb_ref, kb_ref, vb_ref):
        c, sa, sb = c_ref[...], sa_ref[...], sb_ref[...]
        krv = kr_ref[...]
        for h in range(N_HEADS):
            blk = slice(h * LANES, (h + 1) * LANES)
            qb_ref[:, blk] = _rope_apply(q_ref[:, blk], c, sa, sb).astype(BF16)
            kb_ref[:, blk] = (kv_ref[:, blk] + krv).astype(BF16)
        vb_ref[...] = kv_ref[:, WIDTH_P:2 * WIDTH_P].astype(BF16)

    return _rows(body, "mla_qkv", S, 256, [q, kv, kr, *tabs], [],
                 [(WIDTH_P, BF16), (WIDTH_P, BF16), (WIDTH_P, BF16)])


def _outnorm_fwd(oa, ob, ga, gb):
    S = oa.shape[0]

    def body(step, oa_ref, ob_ref, ga_ref, gb_ref, cat_ref):
        for o_ref, g_ref, off in ((oa_ref, ga_ref, 0), (ob_ref, gb_ref, WIDTH_P)):
            o = o_ref[...]
            r = lax.rsqrt(jnp.sum(o * o, axis=-1, keepdims=True) * (1.0 / WIDTH) + EPS)
            cat_ref[:, off:off + WIDTH_P] = (o * r * g_ref[...]).astype(BF16)

    return _rows(body, "outnorm_fwd", S, 256, [oa, ob], [ga, gb], [(2 * WIDTH_P, BF16)])[0]


def _outnorm_bwd(oa, ob, ga, gb, dcat):
    S = oa.shape[0]

    def body(step, oa_ref, ob_ref, dcat_ref, ga_ref, gb_ref, doa_ref, dob_ref, dla_ref, dlb_ref, dga_ref, dgb_ref):
        for o_ref, g_ref, off, do_ref, dl_ref, dg_ref in (
                (oa_ref, ga_ref, 0, doa_ref, dla_ref, dga_ref), (ob_ref, gb_ref, WIDTH_P, dob_ref, dlb_ref, dgb_ref)):
            o = o_ref[...]
            do, dgr = _rms_bwd_math(o, g_ref[...], dcat_ref[:, off:off + WIDTH_P], WIDTH)
            do_ref[...] = do.astype(BF16)
            _acc_add(step, dg_ref, jnp.sum(dgr, axis=0, keepdims=True))
            prod = do.astype(BF16).astype(F32) * o
            for h in range(N_HEADS):
                blk = slice(h * LANES, (h + 1) * LANES)
                d = jnp.sum(prod[:, blk], axis=-1, keepdims=True)
                dl_ref[:, blk] = jnp.broadcast_to(d, (d.shape[0], LANES))

    return _rows(body, "outnorm_bwd", S, 256, [oa, ob, dcat], [ga, gb],
                 [(WIDTH_P, BF16), (WIDTH_P, BF16), (WIDTH_P, F32), (WIDTH_P, F32)],
                 [((1, WIDTH_P), F32), ((1, WIDTH_P), F32)])


def _post_mix(x, y, g_post, g_pre):
    S, W = x.shape

    def body(step, x_ref, y_ref, gp_ref, gq_ref, x1_ref, h_ref):
        yv = y_ref[...]
        r = lax.rsqrt(jnp.mean(yv * yv, axis=-1, keepdims=True) + EPS)
        x1 = x_ref[...] + yv * r * gp_ref[...]
        x1_ref[...] = x1
        r1 = lax.rsqrt(jnp.mean(x1 * x1, axis=-1, keepdims=True) + EPS)
        h_ref[...] = (x1 * r1 * gq_ref[...]).astype(BF16)

    return _rows(body, "post_mix", S, 512, [x, y], [g_post, g_pre], [(W, F32), (W, BF16)])


def _final(x1, y2, g, target):
    S, W = x1.shape
    nsteps = S // 256

    def body(step, x1_ref, y_ref, t_ref, g_ref, dx2_ref, dy_ref, dg_ref, sq_ref, loss_ref):
        yv = y_ref[...]
        gv = g_ref[...]
        r = lax.rsqrt(jnp.mean(yv * yv, axis=-1, keepdims=True) + EPS)
        yn = yv * r
        err = (x1_ref[...] + yn * gv) - t_ref[...]
        dx2 = err * (1.0 / W)
        dx2_ref[...] = dx2
        dyg = dx2 * gv
        dy = r * (dyg - yn * jnp.mean(dyg * yn, axis=-1, keepdims=True))
        dy_ref[...] = dy.astype(BF16)
        _acc_add(step, dg_ref, jnp.sum(dx2 * yn, axis=0, keepdims=True))
        _acc_add(step, sq_ref, jnp.sum(err * err, axis=0, keepdims=True))

        @pl.when(step == nsteps - 1)
        def _():
            tot = jnp.sum(sq_ref[...], axis=-1, keepdims=True) * (0.5 / W)
            loss_ref[...] = jnp.broadcast_to(tot, (1, LANES))

    return _rows(body, "final_loss", S, 256, [x1, y2, target], [g], [(W, F32), (W, BF16)],
                 [((1, W), F32), ((1, W), F32), ((1, LANES), F32)])


_GELU_C = math.sqrt(2.0 / math.pi)
_CONV_CHUNK = 512
_HALO = 8


def _gelu(g):
    t = jnp.tanh(_GELU_C * (g + 0.044715 * (g * g * g)))
    return g * (0.5 * (1.0 + t)), t


def _fill_padded(pad_ref, src_ref, S):
    zeros = jnp.zeros((_HALO, LANES), F32)
    pad_ref[0:_HALO, :] = zeros
    pad_ref[_HALO + S:2 * _HALO + S, :] = zeros
    for r0 in range(0, S, _CONV_CHUNK):
        pad_ref[_HALO + r0:_HALO + r0 + _CONV_CHUNK, :] = src_ref[r0:r0 + _CONV_CHUNK, :].astype(F32)


def _conv_fwd(u0, conv_w, conv_b):
    S, C2 = u0.shape
    nb = D_FF // LANES

    def body(u0g_ref, u0v_ref, wg_ref, wv_ref, bg_ref, bv_ref, ug_ref, uv_ref, a_ref, pg_ref, pv_ref):
        _fill_padded(pg_ref, u0g_ref, S)
        _fill_padded(pv_ref, u0v_ref, S)
        wg, wv = wg_ref[...], wv_ref[...]
        for r0 in range(0, S, _CONV_CHUNK):
            def conv(p_ref, w, b_ref):
                base = _HALO + r0
                return (p_ref[base - 1:base - 1 + _CONV_CHUNK, :] * w[0:1, :]
                        + p_ref[base:base + _CONV_CHUNK, :] * w[1:2, :]
                        + p_ref[base + 1:base + 1 + _CONV_CHUNK, :] * w[2:3, :] + b_ref[...])
            g = conv(pg_ref, wg, bg_ref)
            v = conv(pv_ref, wv, bv_ref)
            rows = slice(r0, r0 + _CONV_CHUNK)
            ug_ref[rows, :] = g
            uv_ref[rows, :] = v
            a_ref[rows, :] = (_gelu(g)[0] * v).astype(BF16)

    col = lambda off: pl.BlockSpec((S, LANES), lambda j: (0, j + off))
    wcol = lambda off: pl.BlockSpec((3, LANES), lambda j: (0, j + off))
    bcol = lambda off: pl.BlockSpec((1, LANES), lambda j: (0, j + off))
    ug, uv, a = pl.pallas_call(
        body, name="conv_gelu_fwd", grid=(nb,),
        in_specs=[col(0), col(nb), wcol(0), wcol(nb), bcol(0), bcol(nb)],
        out_specs=[col(0), col(0), col(0)],
        out_shape=[jax.ShapeDtypeStruct((S, D_FF), F32), jax.ShapeDtypeStruct((S, D_FF), F32),
                   jax.ShapeDtypeStruct((S, D_FF), BF16)],
        scratch_shapes=[pltpu.VMEM((S + 2 * _HALO, LANES), F32), pltpu.VMEM((S + 2 * _HALO, LANES), F32)],
        compiler_params=_cparams(("arbitrary",)),
    )(u0, u0, conv_w, conv_w, conv_b, conv_b)
    return ug, uv, a


def _conv_bwd(u0, ug, uv, da, conv_w):
    S = u0.shape[0]
    nb = D_FF // LANES

    def body(u0_ref, ug_ref, uv_ref, da_ref, w_ref, du0_ref, dw_ref, db_ref, pu_ref, pd_ref):
        is_g = pl.program_id(0) < nb
        _fill_padded(pu_ref, u0_ref, S)
        zeros = jnp.zeros((_HALO, LANES), F32)
        pd_ref[0:_HALO, :] = zeros
        pd_ref[_HALO + S:2 * _HALO + S, :] = zeros
        sel = jnp.where(is_g, 1.0, 0.0).astype(F32)
        for r0 in range(0, S, _CONV_CHUNK):
            rows = slice(r0, r0 + _CONV_CHUNK)
            g, v, d = ug_ref[rows, :], uv_ref[rows, :], da_ref[rows, :]
            gel, t = _gelu(g)
            dgel = 0.5 * (1.0 + t) + (0.5 * g) * (1.0 - t * t) * (_GELU_C * (1.0 + 3.0 * 0.044715 * (g * g)))
            du = d * (sel * (v * dgel) + (1.0 - sel) * gel)
            pd_ref[_HALO + r0:_HALO + r0 + _CONV_CHUNK, :] = du
        w = w_ref[...]
        acc_b = jnp.zeros((1, LANES), F32)
        acc_w = [jnp.zeros((1, LANES), F32) for _ in range(3)]
        for r0 in range(0, S, _CONV_CHUNK):
            base = _HALO + r0
            du_m = pd_ref[base - 1:base - 1 + _CONV_CHUNK, :]
            du_c = pd_ref[base:base + _CONV_CHUNK, :]
            du_p = pd_ref[base + 1:base + 1 + _CONV_CHUNK, :]
            du0_ref[r0:r0 + _CONV_CHUNK, :] = (du_p * w[0:1, :] + du_c * w[1:2, :] + du_m * w[2:3, :]).astype(BF16)
            acc_b = acc_b + jnp.sum(du_c, axis=0, keepdims=True)
            for k in range(3):
                acc_w[k] = acc_w[k] + jnp.sum(du_c * pu_ref[base + k - 1:base + k - 1 + _CONV_CHUNK, :],
                                              axis=0, keepdims=True)
        db_ref[...] = acc_b
        for k in range(3):
            dw_ref[k:k + 1, :] = acc_w[k]

    own = pl.BlockSpec((S, LANES), lambda j: (0, j))
    half = lambda off: pl.BlockSpec((S, LANES), lambda j: (0, j % nb))
    du0, dw, db = pl.pallas_call(
        body, name="conv_gelu_bwd", grid=(2 * nb,),
        in_specs=[own, half(0), half(0), half(0), pl.BlockSpec((3, LANES), lambda j: (0, j))],
        out_specs=[own, pl.BlockSpec((3, LANES), lambda j: (0, j)), pl.BlockSpec((1, LANES), lambda j: (0, j))],
        out_shape=[jax.ShapeDtypeStruct((S, 2 * D_FF), BF16), jax.ShapeDtypeStruct((3, 2 * D_FF), F32),
                   jax.ShapeDtypeStruct((1, 2 * D_FF), F32)],
        scratch_shapes=[pltpu.VMEM((S + 2 * _HALO, LANES), F32), pltpu.VMEM((S + 2 * _HALO, LANES), F32)],
        compiler_params=_cparams(("arbitrary",)),
    )(u0, ug, uv, da, conv_w)
    return du0, dw, db


def _band_bias():
    T = BAND_BLOCK
    delta = (jnp.arange(2 * BAND_W + 1, dtype=jnp.int32) - BAND_W)[:, None, None] * T
    d = delta + jnp.arange(T, dtype=jnp.int32)[None, None, :] - jnp.arange(T, dtype=jnp.int32)[None, :, None]
    ad = jnp.abs(d)
    mult = ((ad <= 64).astype(F32) + ((ad <= 256) & (ad % 4 == 0)).astype(F32)
            + ((ad <= 1024) & (ad % 16 == 0)).astype(F32))
    slopes = jnp.exp2(-8.0 * jnp.arange(1, N_HEADS + 1, dtype=F32) / N_HEADS)
    bias = -slopes[:, None, None, None] * ad.astype(F32)[None] + jnp.log(jnp.maximum(mult, 1.0))[None]
    return jnp.where(mult[None] > 0, bias, MASKED)


def _attn_fwd(q, k, v, qoff, koff, voff, scale, bias, tq, tk, name):
    S = q.shape[0]
    nq, nkb = S // tq, S // tk
    banded = bias is not None
    nj = 2 * BAND_W + 1 if banded else nkb

    def kblock(i, j):
        return jnp.clip(i + j - BAND_W, 0, nkb - 1) if banded else j

    def body(*refs):
        if banded:
            q_ref, k_ref, v_ref, b_ref, o_ref, lse_ref, m_ref, l_ref, acc_ref = refs
        else:
            q_ref, k_ref, v_ref, o_ref, lse_ref, m_ref, l_ref, acc_ref = refs
        i, j = pl.program_id(1), pl.program_id(2)

        @pl.when(j == 0)
        def _():
            m_ref[...] = jnp.full_like(m_ref, M_INIT)
            l_ref[...] = jnp.zeros_like(l_ref)
            acc_ref[...] = jnp.zeros_like(acc_ref)

        def step():
            s = lax.dot_general(q_ref[...].astype(BF16), k_ref[...].astype(BF16), (((1,), (1,)), ((), ())),
                                preferred_element_type=F32) * scale
            if banded:
                s = s + b_ref[0, j]
            m_prev = m_ref[...]
            m_new = jnp.maximum(m_prev, jnp.max(s, axis=-1, keepdims=True))
            p = jnp.exp(s - m_new)
            alpha = jnp.exp(m_prev - m_new)
            l_ref[...] = alpha * l_ref[...] + jnp.sum(p, axis=-1, keepdims=True)
            acc_ref[...] = alpha * acc_ref[...] + jnp.dot(p.astype(BF16), v_ref[...].astype(BF16),
                                                          preferred_element_type=F32)
            m_ref[...] = m_new

        if banded:
            kb = i + j - BAND_W
            pl.when((kb >= 0) & (kb < nkb))(step)
        else:
            step()

        @pl.when(j == nj - 1)
        def _():
            l = l_ref[...]
            o_ref[...] = acc_ref[...] / l
            lse_ref[...] = jnp.broadcast_to(m_ref[...] + jnp.log(l), lse_ref.shape)

    in_specs = [pl.BlockSpec((tq, LANES), lambda h, i, j: (i, qoff + h)),
                pl.BlockSpec((tk, LANES), lambda h, i, j: (kblock(i, j), koff + h)),
                pl.BlockSpec((tk, LANES), lambda h, i, j: (kblock(i, j), voff + h))]
    ins = [q, k, v]
    if banded:
        in_specs.append(pl.BlockSpec((1, nj, tq, tk), lambda h, i, j: (h, 0, 0, 0)))
        ins.append(bias)
    o_spec = pl.BlockSpec((tq, LANES), lambda h, i, j: (i, h))
    return pl.pallas_call(
        body, name=name, grid=(N_HEADS, nq, nj), in_specs=in_specs, out_specs=[o_spec, o_spec],
        out_shape=[jax.ShapeDtypeStruct((S, WIDTH_P), F32), jax.ShapeDtypeStruct((S, WIDTH_P), F32)],
        scratch_shapes=[pltpu.VMEM((tq, 1), F32), pltpu.VMEM((tq, 1), F32), pltpu.VMEM((tq, LANES), F32)],
        compiler_params=_cparams(("parallel", "parallel", "arbitrary")),
    )(*ins)


def _attn_bwd(q, k, v, do, lse, delta, qoff, koff, voff, scale, bias, tq, tk, name):
    S = q.shape[0]
    nq, nkb = S // tq, S // tk
    banded = bias is not None
    nj = 2 * BAND_W + 1 if banded else nq

    def qblock(i, j):
        return jnp.clip(i + j - BAND_W, 0, nq - 1) if banded else j

    def body(*refs):
        if banded:
            q_ref, k_ref, v_ref, do_ref, lse_ref, dl_ref, b_ref, dq_ref, dk_ref, dv_ref, dk_acc, dv_acc = refs
        else:
            q_ref, k_ref, v_ref, do_ref, lse_ref, dl_ref, dq_ref, dk_ref, dv_ref, dk_acc, dv_acc = refs
        i, j = pl.program_id(1), pl.program_id(2)

        @pl.when((i == 0) & (j == 0))
        def _():
            dq_ref[...] = jnp.zeros_like(dq_ref)

        @pl.when(j == 0)
        def _():
            dk_acc[...] = jnp.zeros_like(dk_acc)
            dv_acc[...] = jnp.zeros_like(dv_acc)

        def step():
            qv, kv_, vv, dov = (q_ref[...].astype(BF16), k_ref[...].astype(BF16), v_ref[...].astype(BF16),
                                do_ref[...].astype(BF16))
            s = lax.dot_general(qv, kv_, (((1,), (1,)), ((), ())), preferred_element_type=F32) * scale
            if banded:
                s = s + b_ref[0, 2 * BAND_W - j]
            p = jnp.exp(s - lse_ref[:, 0:1])
            pb = p.astype(BF16)
            dv_acc[...] += lax.dot_general(pb, dov, (((0,), (0,)), ((), ())), preferred_element_type=F32)
            dp = lax.dot_general(dov, vv, (((1,), (1,)), ((), ())), preferred_element_type=F32)
            ds = (p * (dp - dl_ref[:, 0:1]) * scale).astype(BF16)
            qi = qblock(i, j)
            rows = pl.ds(pl.multiple_of(qi * tq, tq), tq)
            dq_ref[rows, :] += jnp.dot(ds, kv_, preferred_element_type=F32)
            dk_acc[...] += lax.dot_general(ds, qv, (((0,), (0,)), ((), ())), preferred_element_type=F32)

        if banded:
            qb = i + j - BAND_W
            pl.when((qb >= 0) & (qb < nq))(step)
        else:
            step()

        @pl.when(j == nj - 1)
        def _():
            dk_ref[...] = dk_acc[...]
            dv_ref[...] = dv_acc[...]

    qspec = lambda off: pl.BlockSpec((tq, LANES), lambda h, i, j: (qblock(i, j), off + h))
    in_specs = [qspec(qoff),
                pl.BlockSpec((tk, LANES), lambda h, i, j: (i, koff + h)),
                pl.BlockSpec((tk, LANES), lambda h, i, j: (i, voff + h)),
                qspec(0), qspec(0), qspec(0)]
    ins = [q, k, v, do, lse, delta]
    if banded:
        in_specs.append(pl.BlockSpec((1, nj, tq, tk), lambda h, i, j: (h, 0, 0, 0)))
        ins.append(bias)
    kspec = pl.BlockSpec((tk, LANES), lambda h, i, j: (i, h))
    out = jax.ShapeDtypeStruct((S, WIDTH_P), F32)
    return pl.pallas_call(
        body, name=name, grid=(N_HEADS, nkb, nj), in_specs=in_specs,
        out_specs=[pl.BlockSpec((S, LANES), lambda h, i, j: (0, h)), kspec, kspec],
        out_shape=[out, out, out],
        scratch_shapes=[pltpu.VMEM((tk, LANES), F32), pltpu.VMEM((tk, LANES), F32)],
        compiler_params=_cparams(("arbitrary", "arbitrary", "arbitrary")),
    )(*ins)


def _mla_bwd_prep(dq, dk, dv, tabs):
    S = dq.shape[0]

    def body(step, dq_ref, dk_ref, dv_ref, c_ref, sa_ref, sb_ref, dqp_ref, dkv_ref, dkr_ref):
        c, sa, sb = c_ref[...], sa_ref[...], sb_ref[...]
        dksum = jnp.zeros((dq_ref.shape[0], LANES), F32)
        for h in range(N_HEADS):
            blk = slice(h * LANES, (h + 1) * LANES)
            dqp_ref[:, blk] = _rope_transpose(dq_ref[:, blk], c, sa, sb).astype(BF16)
            dksum = dksum + dk_ref[:, blk]
        dkv_ref[:, 0:WIDTH_P] = dk_ref[...].astype(BF16)
        dkv_ref[:, WIDTH_P:2 * WIDTH_P] = dv_ref[...].astype(BF16)
        lane = lax.broadcasted_iota(jnp.int32, dksum.shape, 1)
        live = (lane >= HEAD_DIM) & (lane < HEAD_DIM + QK_ROPE)
        dkr_ref[...] = jnp.where(live, _rope_transpose(dksum, c, sa, sb), 0.0)

    return _rows(body, "mla_bwd_prep", S, 256, [dq, dk, dv, *tabs], [],
                 [(WIDTH_P, BF16), (2 * WIDTH_P, BF16), (LANES, F32)])


def _mla_norm_bwd(proj_b, dcq_n, dckv_n, dkr, g_q, g_kv):
    S = proj_b.shape[0]

    def body(step, p_ref, dcq_ref, dckv_ref, dkr_ref, gq_ref, gkv_ref, dp_ref, dgq_ref, dgkv_ref):
        dcq, dgq = _rms_bwd_math(p_ref[:, 0:Q_LORA], gq_ref[...], dcq_ref[...], Q_LORA)
        dckv, dgkv = _rms_bwd_math(p_ref[:, Q_LORA:Q_LORA + KV_LORA], gkv_ref[...], dckv_ref[...], KV_LORA)
        dp_ref[:, 0:Q_LORA] = dcq.astype(BF16)
        dp_ref[:, Q_LORA:Q_LORA + KV_LORA] = dckv.astype(BF16)
        dp_ref[:, Q_LORA + KV_LORA:TAIL_P] = dkr_ref[...].astype(BF16)
        _acc_add(step, dgq_ref, jnp.sum(dgq, axis=0, keepdims=True))
        _acc_add(step, dgkv_ref, jnp.sum(dgkv, axis=0, keepdims=True))

    return _rows(body, "mla_norm_bwd", S, 512, [proj_b, dcq_n, dckv_n, dkr], [g_q, g_kv], [(TAIL_P, BF16)],
                 [((1, Q_LORA), F32), ((1, KV_LORA), F32)])


def _pad_cols(w, d):
    lead = w.shape[:-1]
    w = w.reshape(lead + (N_HEADS, d))
    w = jnp.pad(w, [(0, 0)] * len(lead) + [(0, 0), (0, LANES - d)])
    return w.reshape(lead + (N_HEADS * LANES,))


def _unpad_cols(w, d):
    lead = w.shape[:-1]
    return w.reshape(lead + (N_HEADS, LANES))[..., :d].reshape(lead + (N_HEADS * d,))


def _pad_weights(w):
    w_in = w['w_in']
    zeros = lambda n: jnp.zeros((D_MODEL, n), w_in.dtype)
    p = {}
    p['w_in_a'] = jnp.concatenate([_pad_cols(w_in[:, i * WIDTH:(i + 1) * WIDTH], HEAD_DIM) for i in range(3)], axis=1)
    p['w_in_b'] = jnp.concatenate([w_in[:, 3 * WIDTH:3 * WIDTH + Q_LORA + KV_LORA], zeros(HEAD_DIM),
                                   w_in[:, D_IN - QK_ROPE:], zeros(LANES - HEAD_DIM - QK_ROPE)], axis=1)
    p['w_uq'] = _pad_cols(w['w_uq'], HEAD_DIM + QK_ROPE)
    kv = w['w_ukv'].reshape(KV_LORA, N_HEADS, 2 * HEAD_DIM)
    p['w_ukv'] = jnp.concatenate([_pad_cols(kv[:, :, :HEAD_DIM].reshape(KV_LORA, WIDTH), HEAD_DIM),
                                  _pad_cols(kv[:, :, HEAD_DIM:].reshape(KV_LORA, WIDTH), HEAD_DIM)], axis=1)
    p['w_o'] = jnp.concatenate(
        [_pad_cols(w['w_o'][i * WIDTH:(i + 1) * WIDTH].T, HEAD_DIM).T for i in range(2)], axis=0)
    p['g_a'] = _pad_cols(w['out_norm_a'], HEAD_DIM)
    p['g_b'] = _pad_cols(w['out_norm_b'], HEAD_DIM)
    return p


def _unpad_grads(d):
    g = {}
    dwa = d['w_in_a']
    tail = d['w_in_b']
    g['w_in'] = jnp.concatenate(
        [_unpad_cols(dwa[:, i * WIDTH_P:(i + 1) * WIDTH_P], HEAD_DIM) for i in range(3)]
        + [tail[:, :Q_LORA + KV_LORA], tail[:, Q_LORA + KV_LORA + HEAD_DIM:Q_LORA + KV_LORA + HEAD_DIM + QK_ROPE]],
        axis=1)
    g['w_uq'] = _unpad_cols(d['w_uq'], HEAD_DIM + QK_ROPE)
    dk = _unpad_cols(d['w_ukv'][:, :WIDTH_P], HEAD_DIM).reshape(KV_LORA, N_HEADS, HEAD_DIM)
    dv = _unpad_cols(d['w_ukv'][:, WIDTH_P:], HEAD_DIM).reshape(KV_LORA, N_HEADS, HEAD_DIM)
    g['w_ukv'] = jnp.concatenate([dk, dv], axis=2).reshape(KV_LORA, 2 * WIDTH)
    g['w_o'] = jnp.concatenate(
        [_unpad_cols(d['w_o'][i * WIDTH_P:(i + 1) * WIDTH_P].T, HEAD_DIM).T for i in range(2)], axis=0)
    g['out_norm_a'] = _unpad_cols(d['g_a'], HEAD_DIM)
    g['out_norm_b'] = _unpad_cols(d['g_b'], HEAD_DIM)
    return g


def _local_step(x, target, w):
    S = x.shape[0]
    p = _pad_weights(w)
    tabs = _rope_tables(S)
    bias = _band_bias()
    T = BAND_BLOCK
    tq_mla = _tile(S, 512)

    h1 = _rms_fwd(x, w['norm_mix_pre'], "rms_mix_pre")
    proj_a = _mm(h1, p['w_in_a'], 'nn', BF16, "mm_in_a")
    proj_b = _mm(h1, p['w_in_b'], 'nn', F32, "mm_in_b")
    oa, lse_a = _attn_fwd(proj_a, proj_a, proj_a, 0, N_HEADS, 2 * N_HEADS, DIL_SCALE, bias, T, T, "dil_fwd")
    cq_n, ckv_n, kr = _mla_prep(proj_b, w['q_lat_norm'], w['kv_lat_norm'], tabs)
    q_lin = _mm(cq_n, p['w_uq'], 'nn', F32, "mm_uq")
    kv_lin = _mm(ckv_n, p['w_ukv'], 'nn', F32, "mm_ukv")
    qb, kb, vb = _mla_qkv(q_lin, kv_lin, kr, tabs)
    ob, lse_b = _attn_fwd(qb, kb, vb, 0, 0, 0, MLA_SCALE, None, tq_mla, tq_mla, "mla_fwd")
    cat = _outnorm_fwd(oa, ob, p['g_a'], p['g_b'])
    y = _mm(cat, p['w_o'], 'nn', F32, "mm_o")
    x1, h2 = _post_mix(x, y, w['norm_mix_post'], w['norm_ffn_pre'])
    u0 = _mm(h2, w['w_up'], 'nn', F32, "mm_up")
    ug, uv, a = _conv_fwd(u0, w['conv_w'], w['conv_b'])
    y2 = _mm(a, w['w_down'], 'nn', F32, "mm_down")
    dx2, dy2, dg_ffn_post, _, loss = _final(x1, y2, w['norm_ffn_post'], target)

    g = {'norm_ffn_post': dg_ffn_post}
    da = _mm(dy2, w['w_down'], 'nt', F32, "mm_down_dx")
    g['w_down'] = _mm(a, dy2, 'tn', F32, "mm_down_dw")
    du0, g['conv_w'], g['conv_b'] = _conv_bwd(u0, ug, uv, da, w['conv_w'])
    dh2 = _mm(du0, w['w_up'], 'nt', F32, "mm_up_dx")
    g['w_up'] = _mm(h2, du0, 'tn', F32, "mm_up_dw")
    dx1, g['norm_ffn_pre'] = _rms_bwd(x1, w['norm_ffn_pre'], [dh2], dx2, F32, "rms_ffn_pre_bwd")
    dy, g['norm_mix_post'] = _rms_bwd(y, w['norm_mix_post'], [dx1], None, BF16, "rms_mix_post_bwd")
    dcat = _mm(dy, p['w_o'], 'nt', F32, "mm_o_dx")
    dpad = {'w_o': _mm(cat, dy, 'tn', F32, "mm_o_dw")}
    do_a, do_b, dl_a, dl_b, dpad['g_a'], dpad['g_b'] = _outnorm_bwd(oa, ob, p['g_a'], p['g_b'], dcat)

    dq_b, dk_b, dv_b = _attn_bwd(qb, kb, vb, do_b, lse_b, dl_b, 0, 0, 0, MLA_SCALE, None, tq_mla, tq_mla, "mla_bwd")
    dq_pre, dkv, dkr = _mla_bwd_prep(dq_b, dk_b, dv_b, tabs)
    dcq_n = _mm(dq_pre, p['w_uq'], 'nt', F32, "mm_uq_dx")
    dpad['w_uq'] = _mm(cq_n, dq_pre, 'tn', F32, "mm_uq_dw")
    dckv_n = _mm(dkv, p['w_ukv'], 'nt', F32, "mm_ukv_dx")
    dpad['w_ukv'] = _mm(ckv_n, dkv, 'tn', F32, "mm_ukv_dw")
    dproj_b, g['q_lat_norm'], g['kv_lat_norm'] = _mla_norm_bwd(proj_b, dcq_n, dckv_n, dkr,
                                                               w['q_lat_norm'], w['kv_lat_norm'])

    dq_a, dk_a, dv_a = _attn_bwd(proj_a, proj_a, proj_a, do_a, lse_a, dl_a, 0, N_HEADS, 2 * N_HEADS, DIL_SCALE,
                                 bias, T, T, "dil_bwd")
    dh1 = _mm(dproj_b, p['w_in_b'], 'nt', F32, "mm_in_b_dx")
    dw_parts = []
    for i, dpart in enumerate((dq_a, dk_a, dv_a)):
        w_part = p['w_in_a'][:, i * WIDTH_P:(i + 1) * WIDTH_P]
        dh1 = _mm(dpart, w_part, 'nt', F32, "mm_in_a_dx%d" % i, add=dh1)
        dw_parts.append(_mm(h1, dpart, 'tn', F32, "mm_in_a_dw%d" % i))
    dpad['w_in_a'] = jnp.concatenate(dw_parts, axis=1)
    dpad['w_in_b'] = _mm(h1, dproj_b, 'tn', F32, "mm_in_b_dw")
    grad_x, g['norm_mix_pre'] = _rms_bwd(x, w['norm_mix_pre'], [dh1], dx1, F32, "rms_mix_pre_bwd")
    g.update(_unpad_grads(dpad))
    return loss, grad_x, g


MESH = pl.DeviceIdType.MESH
ANY = pl.BlockSpec(memory_space=pl.ANY)


def _place():
    x, y, c = lax.axis_index("x"), lax.axis_index("y"), lax.axis_index("c")
    chips = [(1 - x, y), (x, 1 - y), (1 - x, 1 - y)]
    return x, y, c, chips


def _all_gather(bufs):
    n = len(bufs)

    def body(*refs):
        in_refs, out_refs = refs[:n], refs[n:2 * n]
        send_sems, recv_sems, local_sems = refs[2 * n:]
        x, y, c, chips = _place()
        me = 2 * x + y
        local = [pltpu.make_async_copy(in_refs[b], out_refs[b].at[me], local_sems.at[b]) for b in range(n)]
        for cp in local:
            cp.start()
        sends = []
        for j, (px, py) in enumerate(chips):
            for b in range(n):
                sends.append(pltpu.make_async_remote_copy(
                    src_ref=in_refs[b], dst_ref=out_refs[b].at[me], send_sem=send_sems.at[j * n + b],
                    recv_sem=recv_sems.at[j * n + b], device_id=(px, py, c), device_id_type=MESH))
        for cp in sends:
            cp.start()
        for j, (px, py) in enumerate(chips):
            for b in range(n):
                pltpu.make_async_remote_copy(
                    src_ref=in_refs[b], dst_ref=out_refs[b].at[2 * px + py], send_sem=send_sems.at[j * n + b],
                    recv_sem=recv_sems.at[j * n + b], device_id=(px, py, c), device_id_type=MESH).wait_recv()
        for cp in sends:
            cp.wait_send()
        for cp in local:
            cp.wait()

    return pl.pallas_call(
        body, name="gather_weights", in_specs=[ANY] * n, out_specs=[ANY] * n,
        out_shape=[jax.ShapeDtypeStruct((N_CHIPS,) + b.shape, b.dtype) for b in bufs],
        scratch_shapes=[pltpu.SemaphoreType.DMA((3 * n,)), pltpu.SemaphoreType.DMA((3 * n,)),
                        pltpu.SemaphoreType.DMA((n,))],
    )(*bufs)


def _scatter_grads(gflat):
    _, R, L = gflat.shape

    def body(g_ref, o_ref, send_sems, recv_sems, local_sem):
        x, y, c, chips = _place()
        me = 2 * x + y
        local = pltpu.make_async_copy(g_ref.at[me], o_ref.at[me], local_sem)
        local.start()
        sends = [pltpu.make_async_remote_copy(
            src_ref=g_ref.at[2 * px + py], dst_ref=o_ref.at[me], send_sem=send_sems.at[j], recv_sem=recv_sems.at[j],
            device_id=(px, py, c), device_id_type=MESH) for j, (px, py) in enumerate(chips)]
        for cp in sends:
            cp.start()
        for j, (px, py) in enumerate(chips):
            pltpu.make_async_remote_copy(
                src_ref=g_ref.at[me], dst_ref=o_ref.at[2 * px + py], send_sem=send_sems.at[j],
                recv_sem=recv_sems.at[j], device_id=(px, py, c), device_id_type=MESH).wait_recv()
        for cp in sends:
            cp.wait_send()
        local.wait()

    return pl.pallas_call(
        body, name="scatter_grads", in_specs=[ANY], out_specs=ANY,
        out_shape=jax.ShapeDtypeStruct(gflat.shape, gflat.dtype),
        scratch_shapes=[pltpu.SemaphoreType.DMA((3,)), pltpu.SemaphoreType.DMA((3,)), pltpu.SemaphoreType.DMA],
    )(gflat)


def _sum_slots(recv):
    _, R, L = recv.shape
    tr = _tile(R, 2048)

    def body(r_ref, o_ref):
        o_ref[...] = ((r_ref[0] + r_ref[1]) + r_ref[2]) + r_ref[3]

    return pl.pallas_call(
        body, name="sum_slots", grid=(R // tr,),
        in_specs=[pl.BlockSpec((N_CHIPS, tr, L), lambda i: (0, i, 0))],
        out_specs=pl.BlockSpec((tr, L), lambda i: (i, 0)),
        out_shape=jax.ShapeDtypeStruct((R, L), recv.dtype),
        compiler_params=_cparams(("parallel",)),
    )(recv)


def _swap_sibling(part):
    def body(p_ref, o_ref, send_sem, recv_sem):
        x, y, c, _ = _place()
        cp = pltpu.make_async_remote_copy(src_ref=p_ref, dst_ref=o_ref, send_sem=send_sem, recv_sem=recv_sem,
                                          device_id=(x, y, 1 - c), device_id_type=MESH)
        cp.start()
        cp.wait()

    return pl.pallas_call(
        body, name="swap_sibling", in_specs=[ANY], out_specs=ANY,
        out_shape=jax.ShapeDtypeStruct(part.shape, part.dtype),
        scratch_shapes=[pltpu.SemaphoreType.DMA, pltpu.SemaphoreType.DMA],
    )(part)


def _adamw(g0, g1, w, m, v):
    R, L = w.shape
    tr = _tile(R, 2048)
    bc1 = 1.0 - ADAM_B1 ** ADAM_STEP
    bc2 = 1.0 - ADAM_B2 ** ADAM_STEP

    def body(g0_ref, g1_ref, w_ref, m_ref, v_ref, g_ref, d_ref, nm_ref, nv_ref):
        g = g0_ref[...] + g1_ref[...]
        g_ref[...] = g
        nm = ADAM_B1 * m_ref[...] + (1.0 - ADAM_B1) * g
        nv = ADAM_B2 * v_ref[...] + (1.0 - ADAM_B2) * (g * g)
        nm_ref[...] = nm
        nv_ref[...] = nv
        d_ref[...] = -ADAM_LR * ((nm / bc1) / (jnp.sqrt(nv / bc2) + ADAM_EPS) + ADAM_WD * w_ref[...])

    spec = pl.BlockSpec((tr, L), lambda i: (i, 0))
    out = jax.ShapeDtypeStruct((R, L), F32)
    return pl.pallas_call(
        body, name="adamw", grid=(R // tr,), in_specs=[spec] * 5, out_specs=[spec] * 4, out_shape=[out] * 4,
        compiler_params=_cparams(("parallel",)),
    )(g0, g1, w, m, v)


FLAT_ALIGN = 16 * LANES


def _flatten(parts, dtype):
    flat = jnp.concatenate([p.reshape(-1).astype(dtype) for p in parts])
    n = flat.shape[0]
    npad = -(-n // (FLAT_ALIGN * 16)) * (FLAT_ALIGN * 16)
    return jnp.pad(flat, (0, npad - n)).reshape(npad // LANES, LANES)


def _unflatten(flat, shapes):
    flat = flat.reshape(-1)
    out, pos = [], 0
    for s in shapes:
        n = int(np.prod(s))
        out.append(flat[pos:pos + n].reshape(s))
        pos += n
    return out


def _shard(a, axis, i):
    n = a.shape[axis] // N_CHIPS
    return lax.slice_in_dim(a, i * n, (i + 1) * n, axis=axis)


def kernel(x, norm_mix_pre, w_in, q_lat_norm, w_uq, kv_lat_norm, w_ukv, out_norm_a, out_norm_b, w_o, norm_mix_post, norm_ffn_pre, w_up, conv_w, conv_b, w_down, norm_ffn_post, loss_target, m_norm_mix_pre, m_w_in, m_q_lat_norm, m_w_uq, m_kv_lat_norm, m_w_ukv, m_out_norm_a, m_out_norm_b, m_w_o, m_norm_mix_post, m_norm_ffn_pre, m_w_up, m_conv_w, m_conv_b, m_w_down, m_norm_ffn_post, v_norm_mix_pre, v_w_in, v_q_lat_norm, v_w_uq, v_kv_lat_norm, v_w_ukv, v_out_norm_a, v_out_norm_b, v_w_o, v_norm_mix_post, v_norm_ffn_pre, v_w_up, v_conv_w, v_conv_b, v_w_down, v_norm_ffn_post):
    args = dict(locals())
    strip = lambda a: a[0] if a.ndim == 3 else a
    wl = {n: strip(args[n]) for n in WEIGHTS}
    ml = {n: strip(args['m_' + n]) for n in WEIGHTS}
    vl = {n: strip(args['v_' + n]) for n in WEIGHTS}

    big_shapes = [wl[n].shape for n in BIG]
    gathered_big, gathered_cw = _all_gather([_flatten([wl[n] for n in BIG], BF16), _flatten([wl['conv_w']], F32)])
    full = {n: wl[n] for n in SMALL}
    per_chip = [_unflatten(gathered_big[i], big_shapes) for i in range(N_CHIPS)]
    for k, n in enumerate(BIG):
        full[n] = jnp.concatenate([per_chip[i][k] for i in range(N_CHIPS)], axis=SHARD_AXIS[n])
    full['conv_w'] = jnp.concatenate(
        [_unflatten(gathered_cw[i], [wl['conv_w'].shape])[0] for i in range(N_CHIPS)], axis=1)

    loss_b, grad_x, g = _local_step(x[0], loss_target[0], full)

    sharded = [n for n in WEIGHTS if SHARD_AXIS[n] is not None]
    slots = []
    for i in range(N_CHIPS):
        slots.append(_flatten([_shard(g[n], SHARD_AXIS[n], i) for n in sharded] + [g[n] for n in SMALL], F32))
    recv = _scatter_grads(jnp.stack(slots))
    part = _sum_slots(recv)
    other = _swap_sibling(part)

    order = sharded + SMALL
    shapes = [wl[n].shape for n in order]
    g_f, d_f, nm_f, nv_f = _adamw(part, other, _flatten([wl[n] for n in order], F32),
                                  _flatten([ml[n] for n in order], F32), _flatten([vl[n] for n in order], F32))
    outs = {}
    for tag, flat in (('grad', g_f), ('delta', d_f), ('new_m', nm_f), ('new_v', nv_f)):
        for n, a in zip(order, _unflatten(flat, shapes)):
            outs[tag + '_' + n] = a.reshape(args[n].shape)

    loss = lax.psum(loss_b[0, 0], ("x", "y", "c"))
    return (loss, grad_x[None], *[outs['grad_' + n] for n in WEIGHTS], *[outs['delta_' + n] for n in WEIGHTS],
            *[outs['new_m_' + n] for n in WEIGHTS], *[outs['new_v_' + n] for n in WEIGHTS])
```

```python
import functools
import math

import jax
import jax.numpy as jnp
import numpy as np
from jax import lax
from jax.experimental import pallas as pl
from jax.experimental.pallas import tpu as pltpu

F32 = jnp.float32
BF16 = jnp.bfloat16

LANES = 128
D_MODEL = 1024
N_HEADS = 8
HEAD_DIM = 64
QK_ROPE = 32
Q_LORA = 384
KV_LORA = 256
D_FF = 2816
WIDTH = N_HEADS * HEAD_DIM
WIDTH_P = N_HEADS * LANES
IN_SIZES = (WIDTH, WIDTH, WIDTH, Q_LORA, KV_LORA, QK_ROPE)
D_IN = sum(IN_SIZES)
TAIL_P = Q_LORA + KV_LORA + LANES
EPS = 1e-6
ROPE_BASE = 10000.0
MASKED = -2e30
M_INIT = -1e30
BAND_BLOCK = 256
BAND_W = 4
MLA_TQ = 2048
MLA_TK = 256
MLA_SCALE = (HEAD_DIM + QK_ROPE) ** -0.5
DIL_SCALE = HEAD_DIM ** -0.5

ADAM_LR = 0.001
ADAM_B1 = 0.9
ADAM_B2 = 0.999
ADAM_EPS = 1e-08
ADAM_WD = 0.01
ADAM_STEP = 10

VMEM_LIMIT = 56 * 1024 * 1024

N_CHIPS = 4

WEIGHTS = ['norm_mix_pre', 'w_in', 'q_lat_norm', 'w_uq', 'kv_lat_norm', 'w_ukv', 'out_norm_a', 'out_norm_b',
           'w_o', 'norm_mix_post', 'norm_ffn_pre', 'w_up', 'conv_w', 'conv_b', 'w_down', 'norm_ffn_post']
SHARD_AXIS = {'norm_mix_pre': None, 'w_in': 1, 'q_lat_norm': None, 'w_uq': 1, 'kv_lat_norm': None, 'w_ukv': 1,
              'out_norm_a': None, 'out_norm_b': None, 'w_o': 0, 'norm_mix_post': None, 'norm_ffn_pre': None,
              'w_up': 1, 'conv_w': 1, 'conv_b': None, 'w_down': 0, 'norm_ffn_post': None}
BIG = ['w_in', 'w_uq', 'w_ukv', 'w_o', 'w_up', 'w_down']
SMALL = [n for n in WEIGHTS if SHARD_AXIS[n] is None]


def _tile(dim, target):
    best = None
    t = LANES
    while t <= min(dim, target):
        if dim % t == 0:
            best = t
        t += LANES
    return best if best is not None else dim


def _cparams(sem=None):
    return pltpu.CompilerParams(dimension_semantics=sem, vmem_limit_bytes=VMEM_LIMIT)


def _mm(a, b, mode, out_dtype, name, add=None, tm=1024, tn=1024, tk=1024):
    if mode == 'nn':
        (M, K), (K2, N) = a.shape, b.shape
        dims = (((1,), (0,)), ((), ()))
    elif mode == 'nt':
        (M, K), (N, K2) = a.shape, b.shape
        dims = (((1,), (1,)), ((), ()))
    else:
        (K, M), (K2, N) = a.shape, b.shape
        dims = (((0,), (0,)), ((), ()))
    assert K == K2, (a.shape, b.shape, mode)
    tm, tn, tk = _tile(M, tm), _tile(N, tn), _tile(K, tk)
    nk = K // tk
    if mode == 'nn':
        a_spec = pl.BlockSpec((tm, tk), lambda i, j, k: (i, k))
        b_spec = pl.BlockSpec((tk, tn), lambda i, j, k: (k, j))
    elif mode == 'nt':
        a_spec = pl.BlockSpec((tm, tk), lambda i, j, k: (i, k))
        b_spec = pl.BlockSpec((tn, tk), lambda i, j, k: (j, k))
    else:
        a_spec = pl.BlockSpec((tk, tm), lambda i, j, k: (k, i))
        b_spec = pl.BlockSpec((tk, tn), lambda i, j, k: (k, j))
    o_spec = pl.BlockSpec((tm, tn), lambda i, j, k: (i, j))
    has_add = add is not None

    def body(*refs):
        if has_add:
            a_ref, b_ref, add_ref, o_ref, acc_ref = refs
        else:
            a_ref, b_ref, o_ref, acc_ref = refs
        k = pl.program_id(2)

        @pl.when(k == 0)
        def _():
            acc_ref[...] = jnp.zeros_like(acc_ref)

        acc_ref[...] += lax.dot_general(a_ref[...].astype(BF16), b_ref[...].astype(BF16), dims,
                                        preferred_element_type=F32)

        @pl.when(k == nk - 1)
        def _():
            r = acc_ref[...]
            if has_add:
                r = r + add_ref[...]
            o_ref[...] = r.astype(o_ref.dtype)

    ins = [a, b] + ([add] if has_add else [])
    in_specs = [a_spec, b_spec] + ([o_spec] if has_add else [])
    return pl.pallas_call(
        body, name=name, grid=(M // tm, N // tn, nk), in_specs=in_specs, out_specs=o_spec,
        out_shape=jax.ShapeDtypeStruct((M, N), out_dtype),
        scratch_shapes=[pltpu.VMEM((tm, tn), F32)],
        compiler_params=_cparams(("parallel", "parallel", "arbitrary")),
    )(*ins)


def _rows(body, name, S, ts, row_ins, full_ins, row_outs, acc_outs=(), chunk_outs=()):
    in_specs = [pl.BlockSpec((ts, a.shape[1]), lambda i: (i, 0)) for a in row_ins]
    in_specs += [pl.BlockSpec(a.shape, lambda i, nd=a.ndim: (0,) * nd) for a in full_ins]
    out_specs = [pl.BlockSpec((ts, w), lambda i: (i, 0)) for (w, _) in row_outs]
    out_specs += [pl.BlockSpec(shape, lambda i, nd=len(shape): (0,) * nd) for (shape, _) in acc_outs]
    out_specs += [pl.BlockSpec((lead, 1, LANES, ts), lambda i: (0, i, 0, 0)) for (lead, _) in chunk_outs]
    out_shape = [jax.ShapeDtypeStruct((S, w), dt) for (w, dt) in row_outs]
    out_shape += [jax.ShapeDtypeStruct(shape, dt) for (shape, dt) in acc_outs]
    out_shape += [jax.ShapeDtypeStruct((lead, S // ts, LANES, ts), dt) for (lead, dt) in chunk_outs]

    def kbody(*refs):
        body(pl.program_id(0), *refs)

    return pl.pallas_call(
        kbody, name=name, grid=(S // ts,), in_specs=in_specs, out_specs=out_specs, out_shape=out_shape,
        compiler_params=_cparams(("arbitrary",)),
    )(*row_ins, *full_ins)


def _acc_add(step, ref, val):
    @pl.when(step == 0)
    def _():
        ref[...] = val

    @pl.when(step != 0)
    def _():
        ref[...] += val


def _rms_fwd(x, g, name):
    S, W = x.shape

    def body(step, x_ref, g_ref, h_ref):
        xv = x_ref[...]
        r = lax.rsqrt(jnp.mean(xv * xv, axis=-1, keepdims=True) + EPS)
        h_ref[...] = (xv * r * g_ref[...]).astype(BF16)

    return _rows(body, name, S, 512, [x], [g], [(W, BF16)])[0]


def _rms_bwd_math(xv, g, dy, width):
    r = lax.rsqrt(jnp.sum(xv * xv, axis=-1, keepdims=True) * (1.0 / width) + EPS)
    xn = xv * r
    dyg = dy * g
    dx = r * (dyg - xn * (jnp.sum(dyg * xn, axis=-1, keepdims=True) * (1.0 / width)))
    return dx, dy * xn


def _rms_bwd(x, g, dys, resid, out_dtype, name):
    S, W = x.shape
    nd = len(dys)
    has_res = resid is not None

    def body(step, *refs):
        x_ref = refs[0]
        dy_refs = refs[1:1 + nd]
        pos = 1 + nd
        res_ref = refs[pos] if has_res else None
        pos += int(has_res)
        g_ref, dx_ref, dg_ref = refs[pos], refs[pos + 1], refs[pos + 2]
        dy = dy_refs[0][...].astype(F32)
        for r_ in dy_refs[1:]:
            dy = dy + r_[...].astype(F32)
        dx, dgr = _rms_bwd_math(x_ref[...], g_ref[...], dy, W)
        if has_res:
            dx = dx + res_ref[...]
        dx_ref[...] = dx.astype(dx_ref.dtype)
        _acc_add(step, dg_ref, jnp.sum(dgr, axis=0, keepdims=True))

    row_ins = [x] + list(dys) + ([resid] if has_res else [])
    dx, dg = _rows(body, name, S, 256, row_ins, [g], [(W, out_dtype)], [((1, W), F32)])
    return dx, dg


def _rope_apply(xv, c, sa, sb):
    return xv * c + pltpu.roll(xv, 16, 1) * sa + pltpu.roll(xv, LANES - 16, 1) * sb


def _rope_transpose(dy, c, sa, sb):
    return dy * c + pltpu.roll(dy * sa, LANES - 16, 1) + pltpu.roll(dy * sb, 16, 1)


def _rope_tables(S):
    pos = jnp.arange(S, dtype=F32)
    inv_freq = jnp.exp(-math.log(ROPE_BASE) * jnp.arange(0, QK_ROPE, 2, dtype=F32) / QK_ROPE)
    ang = pos[:, None] * inv_freq[None, :]
    cos, sin = jnp.cos(ang), jnp.sin(ang)
    ones, zeros = jnp.ones((S, HEAD_DIM), F32), jnp.zeros((S, HEAD_DIM), F32)
    z16, z32 = jnp.zeros((S, 16), F32), jnp.zeros((S, 32), F32)
    c = jnp.concatenate([ones, cos, cos, z32], axis=1)
    sa = jnp.concatenate([zeros, z16, sin, z32], axis=1)
    sb = jnp.concatenate([zeros, -sin, z16, z32], axis=1)
    return c, sa, sb


def _mla_prep(proj_b, g_q, g_kv, tabs):
    S = proj_b.shape[0]

    def body(step, p_ref, c_ref, sa_ref, sb_ref, gq_ref, gkv_ref, cq_ref, ckv_ref, kr_ref):
        cq = p_ref[:, 0:Q_LORA]
        ckv = p_ref[:, Q_LORA:Q_LORA + KV_LORA]
        kr = p_ref[:, Q_LORA + KV_LORA:TAIL_P]
        rq = lax.rsqrt(jnp.mean(cq * cq, axis=-1, keepdims=True) + EPS)
        cq_ref[...] = (cq * rq * gq_ref[...]).astype(BF16)
        rk = lax.rsqrt(jnp.mean(ckv * ckv, axis=-1, keepdims=True) + EPS)
        ckv_ref[...] = (ckv * rk * gkv_ref[...]).astype(BF16)
        kr_ref[...] = _rope_apply(kr, c_ref[...], sa_ref[...], sb_ref[...])

    return _rows(body, "mla_prep", S, 512, [proj_b, *tabs], [g_q, g_kv],
                 [(Q_LORA, BF16), (KV_LORA, BF16), (LANES, F32)])


def _mla_qkv(q, kv, kr, tabs):
    S = q.shape[0]

    def body(step, q_ref, kv_ref, kr_ref, c_ref, sa_ref, sb_ref, qb_ref, kb_ref, vb_ref, kt_ref, vt_ref):
        c, sa, sb = c_ref[...], sa_ref[...], sb_ref[...]
        krv = kr_ref[...]
        row = lax.broadcasted_iota(jnp.int32, (LANES, MLA_TK), 0)
        for h in range(N_HEADS):
            blk = slice(h * LANES, (h + 1) * LANES)
            qb_ref[:, blk] = (_rope_apply(q_ref[:, blk], c, sa, sb) * MLA_SCALE).astype(BF16)
            kh = kv_ref[:, blk] + krv
            kb_ref[:, blk] = kh.astype(BF16)
            kt_ref[h, 0] = kh.T.astype(BF16)
            vh = kv_ref[:, WIDTH_P + h * LANES:WIDTH_P + (h + 1) * LANES]
            vt_ref[h, 0] = jnp.where(row == HEAD_DIM, 1.0, vh.T).astype(BF16)
        vb_ref[...] = kv_ref[:, WIDTH_P:2 * WIDTH_P].astype(BF16)

    return _rows(body, "mla_qkv", S, MLA_TK, [q, kv, kr, *tabs], [],
                 [(WIDTH_P, BF16), (WIDTH_P, BF16), (WIDTH_P, BF16)],
                 chunk_outs=[(N_HEADS, BF16), (N_HEADS, BF16)])


def _outnorm_fwd(oa, ob, ga, gb):
    S = oa.shape[0]

    def body(step, oa_ref, ob_ref, ga_ref, gb_ref, cat_ref):
        for o_ref, g_ref, off in ((oa_ref, ga_ref, 0), (ob_ref, gb_ref, WIDTH_P)):
            o = o_ref[...]
            r = lax.rsqrt(jnp.sum(o * o, axis=-1, keepdims=True) * (1.0 / WIDTH) + EPS)
            cat_ref[:, off:off + WIDTH_P] = (o * r * g_ref[...]).astype(BF16)

    return _rows(body, "outnorm_fwd", S, 256, [oa, ob], [ga, gb], [(2 * WIDTH_P, BF16)])[0]


def _outnorm_bwd(oa, ob, ga, gb, dcat):
    S = oa.shape[0]

    def body(step, oa_ref, ob_ref, dcat_ref, ga_ref, gb_ref, doa_ref, dob_ref, dla_ref, dlb_ref, dga_ref, dgb_ref):
        for o_ref, g_ref, off, do_ref, dl_ref, dg_ref in (
                (oa_ref, ga_ref, 0, doa_ref, dla_ref, dga_ref), (ob_ref, gb_ref, WIDTH_P, dob_ref, dlb_ref, dgb_ref)):
            o = o_ref[...]
            do, dgr = _rms_bwd_math(o, g_ref[...], dcat_ref[:, off:off + WIDTH_P], WIDTH)
            do_ref[...] = do.astype(BF16)
            _acc_add(step, dg_ref, jnp.sum(dgr, axis=0, keepdims=True))
            prod = do.astype(BF16).astype(F32) * o
            for h in range(N_HEADS):
                blk = slice(h * LANES, (h + 1) * LANES)
                d = jnp.sum(prod[:, blk], axis=-1, keepdims=True)
                dl_ref[:, blk] = jnp.broadcast_to(d, (d.shape[0], LANES))

    return _rows(body, "outnorm_bwd", S, 256, [oa, ob, dcat], [ga, gb],
                 [(WIDTH_P, BF16), (WIDTH_P, BF16), (WIDTH_P, F32), (WIDTH_P, F32)],
                 [((1, WIDTH_P), F32), ((1, WIDTH_P), F32)])


def _post_mix(x, y, g_post, g_pre):
    S, W = x.shape

    def body(step, x_ref, y_ref, gp_ref, gq_ref, x1_ref, h_ref):
        yv = y_ref[...]
        r = lax.rsqrt(jnp.mean(yv * yv, axis=-1, keepdims=True) + EPS)
        x1 = x_ref[...] + yv * r * gp_ref[...]
        x1_ref[...] = x1
        r1 = lax.rsqrt(jnp.mean(x1 * x1, axis=-1, keepdims=True) + EPS)
        h_ref[...] = (x1 * r1 * gq_ref[...]).astype(BF16)

    return _rows(body, "post_mix", S, 512, [x, y], [g_post, g_pre], [(W, F32), (W, BF16)])


def _final(x1, y2, g, target):
    S, W = x1.shape
    nsteps = S // 256

    def body(step, x1_ref, y_ref, t_ref, g_ref, dx2_ref, dy_ref, dg_ref, sq_ref, loss_ref):
        yv = y_ref[...]
        gv = g_ref[...]
        r = lax.rsqrt(jnp.mean(yv * yv, axis=-1, keepdims=True) + EPS)
        yn = yv * r
        err = (x1_ref[...] + yn * gv) - t_ref[...]
        dx2 = err * (1.0 / W)
        dx2_ref[...] = dx2
        dyg = dx2 * gv
        dy = r * (dyg - yn * jnp.mean(dyg * yn, axis=-1, keepdims=True))
        dy_ref[...] = dy.astype(BF16)
        _acc_add(step, dg_ref, jnp.sum(dx2 * yn, axis=0, keepdims=True))
        _acc_add(step, sq_ref, jnp.sum(err * err, axis=0, keepdims=True))

        @pl.when(step == nsteps - 1)
        def _():
            tot = jnp.sum(sq_ref[...], axis=-1, keepdims=True) * (0.5 / W)
            loss_ref[...] = jnp.broadcast_to(tot, (1, LANES))

    return _rows(body, "final_loss", S, 256, [x1, y2, target], [g], [(W, F32), (W, BF16)],
                 [((1, W), F32), ((1, W), F32), ((1, LANES), F32)])


_GELU_C = math.sqrt(2.0 / math.pi)
_CONV_CHUNK = 512
_HALO = 8


def _gelu(g):
    t = jnp.tanh(_GELU_C * (g + 0.044715 * (g * g * g)))
    return g * (0.5 * (1.0 + t)), t


def _fill_padded(pad_ref, src_ref, S):
    zeros = jnp.zeros((_HALO, LANES), F32)
    pad_ref[0:_HALO, :] = zeros
    pad_ref[_HALO + S:2 * _HALO + S, :] = zeros
    for r0 in range(0, S, _CONV_CHUNK):
        pad_ref[_HALO + r0:_HALO + r0 + _CONV_CHUNK, :] = src_ref[r0:r0 + _CONV_CHUNK, :].astype(F32)


def _conv_fwd(u0, conv_w, conv_b):
    S, C2 = u0.shape
    nb = D_FF // LANES

    def body(u0g_ref, u0v_ref, wg_ref, wv_ref, bg_ref, bv_ref, ug_ref, uv_ref, a_ref, pg_ref, pv_ref):
        _fill_padded(pg_ref, u0g_ref, S)
        _fill_padded(pv_ref, u0v_ref, S)
        wg, wv = wg_ref[...], wv_ref[...]
        for r0 in range(0, S, _CONV_CHUNK):
            def conv(p_ref, w, b_ref):
                base = _HALO + r0
                return (p_ref[base - 1:base - 1 + _CONV_CHUNK, :] * w[0:1, :]
                        + p_ref[base:base + _CONV_CHUNK, :] * w[1:2, :]
                        + p_ref[base + 1:base + 1 + _CONV_CHUNK, :] * w[2:3, :] + b_ref[...])
            g = conv(pg_ref, wg, bg_ref)
            v = conv(pv_ref, wv, bv_ref)
            rows = slice(r0, r0 + _CONV_CHUNK)
            ug_ref[rows, :] = g
            uv_ref[rows, :] = v
            a_ref[rows, :] = (_gelu(g)[0] * v).astype(BF16)

    col = lambda off: pl.BlockSpec((S, LANES), lambda j: (0, j + off))
    wcol = lambda off: pl.BlockSpec((3, LANES), lambda j: (0, j + off))
    bcol = lambda off: pl.BlockSpec((1, LANES), lambda j: (0, j + off))
    ug, uv, a = pl.pallas_call(
        body, name="conv_gelu_fwd", grid=(nb,),
        in_specs=[col(0), col(nb), wcol(0), wcol(nb), bcol(0), bcol(nb)],
        out_specs=[col(0), col(0), col(0)],
        out_shape=[jax.ShapeDtypeStruct((S, D_FF), F32), jax.ShapeDtypeStruct((S, D_FF), F32),
                   jax.ShapeDtypeStruct((S, D_FF), BF16)],
        scratch_shapes=[pltpu.VMEM((S + 2 * _HALO, LANES), F32), pltpu.VMEM((S + 2 * _HALO, LANES), F32)],
        compiler_params=_cparams(("arbitrary",)),
    )(u0, u0, conv_w, conv_w, conv_b, conv_b)
    return ug, uv, a


def _conv_bwd(u0, ug, uv, da, conv_w):
    S = u0.shape[0]
    nb = D_FF // LANES

    def body(u0_ref, ug_ref, uv_ref, da_ref, w_ref, du0_ref, dw_ref, db_ref, pu_ref, pd_ref):
        is_g = pl.program_id(0) < nb
        _fill_padded(pu_ref, u0_ref, S)
        zeros = jnp.zeros((_HALO, LANES), F32)
        pd_ref[0:_HALO, :] = zeros
        pd_ref[_HALO + S:2 * _HALO + S, :] = zeros
        sel = jnp.where(is_g, 1.0, 0.0).astype(F32)
        for r0 in range(0, S, _CONV_CHUNK):
            rows = slice(r0, r0 + _CONV_CHUNK)
            g, v, d = ug_ref[rows, :], uv_ref[rows, :], da_ref[rows, :]
            gel, t = _gelu(g)
            dgel = 0.5 * (1.0 + t) + (0.5 * g) * (1.0 - t * t) * (_GELU_C * (1.0 + 3.0 * 0.044715 * (g * g)))
            du = d * (sel * (v * dgel) + (1.0 - sel) * gel)
            pd_ref[_HALO + r0:_HALO + r0 + _CONV_CHUNK, :] = du
        w = w_ref[...]
        acc_b = jnp.zeros((1, LANES), F32)
        acc_w = [jnp.zeros((1, LANES), F32) for _ in range(3)]
        for r0 in range(0, S, _CONV_CHUNK):
            base = _HALO + r0
            du_m = pd_ref[base - 1:base - 1 + _CONV_CHUNK, :]
            du_c = pd_ref[base:base + _CONV_CHUNK, :]
            du_p = pd_ref[base + 1:base + 1 + _CONV_CHUNK, :]
            du0_ref[r0:r0 + _CONV_CHUNK, :] = (du_p * w[0:1, :] + du_c * w[1:2, :] + du_m * w[2:3, :]).astype(BF16)
            acc_b = acc_b + jnp.sum(du_c, axis=0, keepdims=True)
            for k in range(3):
                acc_w[k] = acc_w[k] + jnp.sum(du_c * pu_ref[base + k - 1:base + k - 1 + _CONV_CHUNK, :],
                                              axis=0, keepdims=True)
        db_ref[...] = acc_b
        for k in range(3):
            dw_ref[k:k + 1, :] = acc_w[k]

    own = pl.BlockSpec((S, LANES), lambda j: (0, j))
    half = lambda off: pl.BlockSpec((S, LANES), lambda j: (0, j % nb))
    du0, dw, db = pl.pallas_call(
        body, name="conv_gelu_bwd", grid=(2 * nb,),
        in_specs=[own, half(0), half(0), half(0), pl.BlockSpec((3, LANES), lambda j: (0, j))],
        out_specs=[own, pl.BlockSpec((3, LANES), lambda j: (0, j)), pl.BlockSpec((1, LANES), lambda j: (0, j))],
        out_shape=[jax.ShapeDtypeStruct((S, 2 * D_FF), BF16), jax.ShapeDtypeStruct((3, 2 * D_FF), F32),
                   jax.ShapeDtypeStruct((1, 2 * D_FF), F32)],
        scratch_shapes=[pltpu.VMEM((S + 2 * _HALO, LANES), F32), pltpu.VMEM((S + 2 * _HALO, LANES), F32)],
        compiler_params=_cparams(("arbitrary",)),
    )(u0, ug, uv, da, conv_w)
    return du0, dw, db


def _band_bias():
    T = BAND_BLOCK
    delta = (jnp.arange(2 * BAND_W + 1, dtype=jnp.int32) - BAND_W)[:, None, None] * T
    d = delta + jnp.arange(T, dtype=jnp.int32)[None, None, :] - jnp.arange(T, dtype=jnp.int32)[None, :, None]
    ad = jnp.abs(d)
    mult = ((ad <= 64).astype(F32) + ((ad <= 256) & (ad % 4 == 0)).astype(F32)
            + ((ad <= 1024) & (ad % 16 == 0)).astype(F32))
    slopes = jnp.exp2(-8.0 * jnp.arange(1, N_HEADS + 1, dtype=F32) / N_HEADS)
    bias = -slopes[:, None, None, None] * ad.astype(F32)[None] + jnp.log(jnp.maximum(mult, 1.0))[None]
    return jnp.where(mult[None] > 0, bias, MASKED)


def _attn_fwd(q, k, v, qoff, koff, voff, scale, bias, tq, tk, name):
    S = q.shape[0]
    nq, nkb = S // tq, S // tk
    banded = bias is not None
    nj = 2 * BAND_W + 1 if banded else nkb

    def kblock(i, j):
        return jnp.clip(i + j - BAND_W, 0, nkb - 1) if banded else j

    def body(*refs):
        if banded:
            q_ref, k_ref, v_ref, b_ref, o_ref, lse_ref, m_ref, l_ref, acc_ref = refs
        else:
            q_ref, k_ref, v_ref, o_ref, lse_ref, m_ref, l_ref, acc_ref = refs
        i, j = pl.program_id(1), pl.program_id(2)

        @pl.when(j == 0)
        def _():
            m_ref[...] = jnp.full_like(m_ref, M_INIT)
            l_ref[...] = jnp.zeros_like(l_ref)
            acc_ref[...] = jnp.zeros_like(acc_ref)

        def step():
            s = lax.dot_general(q_ref[...].astype(BF16), k_ref[...].astype(BF16), (((1,), (1,)), ((), ())),
                                preferred_element_type=F32) * scale
            if banded:
                s = s + b_ref[0, j]
            m_prev = m_ref[...]
            m_new = jnp.maximum(m_prev, jnp.max(s, axis=-1, keepdims=True))
            p = jnp.exp(s - m_new)
            alpha = jnp.exp(m_prev - m_new)
            l_ref[...] = alpha * l_ref[...] + jnp.sum(p, axis=-1, keepdims=True)
            acc_ref[...] = alpha * acc_ref[...] + jnp.dot(p.astype(BF16), v_ref[...].astype(BF16),
                                                          preferred_element_type=F32)
            m_ref[...] = m_new

        if banded:
            kb = i + j - BAND_W
            pl.when((kb >= 0) & (kb < nkb))(step)
        else:
            step()

        @pl.when(j == nj - 1)
        def _():
            l = l_ref[...]
            o_ref[...] = acc_ref[...] / l
            lse_ref[...] = jnp.broadcast_to(m_ref[...] + jnp.log(l), lse_ref.shape)

    in_specs = [pl.BlockSpec((tq, LANES), lambda h, i, j: (i, qoff + h)),
                pl.BlockSpec((tk, LANES), lambda h, i, j: (kblock(i, j), koff + h)),
                pl.BlockSpec((tk, LANES), lambda h, i, j: (kblock(i, j), voff + h))]
    ins = [q, k, v]
    if banded:
        in_specs.append(pl.BlockSpec((1, nj, tq, tk), lambda h, i, j: (h, 0, 0, 0)))
        ins.append(bias)
    o_spec = pl.BlockSpec((tq, LANES), lambda h, i, j: (i, h))
    return pl.pallas_call(
        body, name=name, grid=(N_HEADS, nq, nj), in_specs=in_specs, out_specs=[o_spec, o_spec],
        out_shape=[jax.ShapeDtypeStruct((S, WIDTH_P), F32), jax.ShapeDtypeStruct((S, WIDTH_P), F32)],
        scratch_shapes=[pltpu.VMEM((tq, 1), F32), pltpu.VMEM((tq, 1), F32), pltpu.VMEM((tq, LANES), F32)],
        compiler_params=_cparams(("parallel", "parallel", "arbitrary")),
    )(*ins)


def _attn_bwd(q, k, v, do, lse, delta, qoff, koff, voff, scale, bias, tq, tk, name):
    S = q.shape[0]
    nq, nkb = S // tq, S // tk
    banded = bias is not None
    nj = 2 * BAND_W + 1 if banded else nq

    def qblock(i, j):
        return jnp.clip(i + j - BAND_W, 0, nq - 1) if banded else j

    def body(*refs):
        if banded:
            q_ref, k_ref, v_ref, do_ref, lse_ref, dl_ref, b_ref, dq_ref, dk_ref, dv_ref, dk_acc, dv_acc = refs
        else:
            q_ref, k_ref, v_ref, do_ref, lse_ref, dl_ref, dq_ref, dk_ref, dv_ref, dk_acc, dv_acc = refs
        i, j = pl.program_id(1), pl.program_id(2)

        @pl.when((i == 0) & (j == 0))
        def _():
            dq_ref[...] = jnp.zeros_like(dq_ref)

        @pl.when(j == 0)
        def _():
            dk_acc[...] = jnp.zeros_like(dk_acc)
            dv_acc[...] = jnp.zeros_like(dv_acc)

        def step():
            qv, kv_, vv, dov = (q_ref[...].astype(BF16), k_ref[...].astype(BF16), v_ref[...].astype(BF16),
                                do_ref[...].astype(BF16))
            s = lax.dot_general(qv, kv_, (((1,), (1,)), ((), ())), preferred_element_type=F32) * scale
            if banded:
                s = s + b_ref[0, 2 * BAND_W - j]
            p = jnp.exp(s - lse_ref[:, 0:1])
            pb = p.astype(BF16)
            dv_acc[...] += lax.dot_general(pb, dov, (((0,), (0,)), ((), ())), preferred_element_type=F32)
            dp = lax.dot_general(dov, vv, (((1,), (1,)), ((), ())), preferred_element_type=F32)
            ds = (p * (dp - dl_ref[:, 0:1]) * scale).astype(BF16)
            qi = qblock(i, j)
            rows = pl.ds(pl.multiple_of(qi * tq, tq), tq)
            dq_ref[rows, :] += jnp.dot(ds, kv_, preferred_element_type=F32)
            dk_acc[...] += lax.dot_general(ds, qv, (((0,), (0,)), ((), ())), preferred_element_type=F32)

        if banded:
            qb = i + j - BAND_W
            pl.when((qb >= 0) & (qb < nq))(step)
        else:
            step()

        @pl.when(j == nj - 1)
        def _():
            dk_ref[...] = dk_acc[...]
            dv_ref[...] = dv_acc[...]

    qspec = lambda off: pl.BlockSpec((tq, LANES), lambda h, i, j: (qblock(i, j), off + h))
    in_specs = [qspec(qoff),
                pl.BlockSpec((tk, LANES), lambda h, i, j: (i, koff + h)),
                pl.BlockSpec((tk, LANES), lambda h, i, j: (i, voff + h)),
                qspec(0), qspec(0), qspec(0)]
    ins = [q, k, v, do, lse, delta]
    if banded:
        in_specs.append(pl.BlockSpec((1, nj, tq, tk), lambda h, i, j: (h, 0, 0, 0)))
        ins.append(bias)
    kspec = pl.BlockSpec((tk, LANES), lambda h, i, j: (i, h))
    out = jax.ShapeDtypeStruct((S, WIDTH_P), F32)
    return pl.pallas_call(
        body, name=name, grid=(N_HEADS, nkb, nj), in_specs=in_specs,
        out_specs=[pl.BlockSpec((S, LANES), lambda h, i, j: (0, h)), kspec, kspec],
        out_shape=[out, out, out],
        scratch_shapes=[pltpu.VMEM((tk, LANES), F32), pltpu.VMEM((tk, LANES), F32)],
        compiler_params=_cparams(("arbitrary", "arbitrary", "arbitrary")),
    )(*ins)


_NT = (((1,), (1,)), ((), ()))


def _mla_fwd(q, k, vt):
    S = q.shape[0]
    tq, tk = MLA_TQ, MLA_TK
    nq, nk = S // tq, S // tk

    def body(q_ref, k_ref, vt_ref, o_ref, lse_ref, acc_ref):
        qv = q_ref[...]
        acc_ref[...] = jnp.zeros_like(acc_ref)

        def chunk(c, m):
            kc = k_ref[pl.ds(pl.multiple_of(c * tk, tk), tk), :]
            st = lax.dot_general(kc, qv, _NT, preferred_element_type=F32)
            m_new = jnp.maximum(m, jnp.max(st, axis=0, keepdims=True))
            pt = jnp.exp(st - m_new).astype(BF16)
            acc_ref[...] = jnp.exp(m - m_new) * acc_ref[...] + jnp.dot(vt_ref[0, c], pt,
                                                                        preferred_element_type=F32)
            return m_new

        m = lax.fori_loop(0, nk, chunk, jnp.full((1, tq), M_INIT, F32), unroll=2)
        acc = acc_ref[...]
        l = acc[HEAD_DIM:HEAD_DIM + 1, :]
        row = lax.broadcasted_iota(jnp.int32, acc.shape, 0)
        o_ref[...] = jnp.where(row < HEAD_DIM, acc / l, 0.0).T
        lse_ref[0] = m + jnp.log(l)

    return pl.pallas_call(
        body, name="mla_fwd", grid=(N_HEADS, nq),
        in_specs=[pl.BlockSpec((tq, LANES), lambda h, i: (i, h)),
                  pl.BlockSpec((S, LANES), lambda h, i: (0, h)),
                  pl.BlockSpec((1, nk, LANES, tk), lambda h, i: (h, 0, 0, 0))],
        out_specs=[pl.BlockSpec((tq, LANES), lambda h, i: (i, h)),
                   pl.BlockSpec((1, 1, tq), lambda h, i: (h, 0, i))],
        out_shape=[jax.ShapeDtypeStruct((S, WIDTH_P), F32), jax.ShapeDtypeStruct((N_HEADS, 1, S), F32)],
        scratch_shapes=[pltpu.VMEM((LANES, tq), F32)],
        compiler_params=_cparams(("parallel", "parallel")),
    )(q, k, vt)


def _mla_bwd(q, k, v, kt, do, o, lse):
    S = q.shape[0]
    tq, tk = MLA_TQ, MLA_TK
    nq, nk = S // tq, S // tk

    def body(q_ref, do_ref, o_ref, lse_ref, k_ref, v_ref, kt_ref, dq_ref, dk_ref, dv_ref, dqt_ref):
        @pl.when(pl.program_id(1) == 0)
        def _():
            dk_ref[...] = jnp.zeros_like(dk_ref)
            dv_ref[...] = jnp.zeros_like(dv_ref)

        qv, dov = q_ref[...], do_ref[...]
        delta = jnp.sum((dov.astype(F32) * o_ref[...]).T, axis=0, keepdims=True)
        lse = lse_ref[0]
        dqt_ref[...] = jnp.zeros_like(dqt_ref)

        def chunk(c, carry):
            rows = pl.ds(pl.multiple_of(c * tk, tk), tk)
            kc, vc = k_ref[rows, :], v_ref[rows, :]
            pt = jnp.exp(lax.dot_general(kc, qv, _NT, preferred_element_type=F32) - lse)
            dv_ref[rows, :] += jnp.dot(pt.astype(BF16), dov, preferred_element_type=F32)
            dpt = lax.dot_general(vc, dov, _NT, preferred_element_type=F32)
            dst = (pt * (dpt - delta)).astype(BF16)
            dk_ref[rows, :] += jnp.dot(dst, qv, preferred_element_type=F32)
            dqt_ref[...] += jnp.dot(kt_ref[0, c], dst, preferred_element_type=F32)
            return carry

        lax.fori_loop(0, nk, chunk, 0, unroll=2)
        dq_ref[...] = (dqt_ref[...] * MLA_SCALE).T

    qspec = pl.BlockSpec((tq, LANES), lambda h, i: (i, h))
    kspec = pl.BlockSpec((S, LANES), lambda h, i: (0, h))
    out = jax.ShapeDtypeStruct((S, WIDTH_P), F32)
    return pl.pallas_call(
        body, name="mla_bwd", grid=(N_HEADS, nq),
        in_specs=[qspec, qspec, qspec, pl.BlockSpec((1, 1, tq), lambda h, i: (h, 0, i)), kspec, kspec,
                  pl.BlockSpec((1, nk, LANES, tk), lambda h, i: (h, 0, 0, 0))],
        out_specs=[qspec, kspec, kspec], out_shape=[out, out, out],
        scratch_shapes=[pltpu.VMEM((LANES, tq), F32)],
        compiler_params=_cparams(("arbitrary", "arbitrary")),
    )(q, do, o, lse, k, v, kt)


def _mla_bwd_prep(dq, dk, dv, tabs):
    S = dq.shape[0]

    def body(step, dq_ref, dk_ref, dv_ref, c_ref, sa_ref, sb_ref, dqp_ref, dkv_ref, dkr_ref):
        c, sa, sb = c_ref[...], sa_ref[...], sb_ref[...]
        dksum = jnp.zeros((dq_ref.shape[0], LANES), F32)
        for h in range(N_HEADS):
            blk = slice(h * LANES, (h + 1) * LANES)
            dqp_ref[:, blk] = _rope_transpose(dq_ref[:, blk], c, sa, sb).astype(BF16)
            dksum = dksum + dk_ref[:, blk]
        dkv_ref[:, 0:WIDTH_P] = dk_ref[...].astype(BF16)
        dkv_ref[:, WIDTH_P:2 * WIDTH_P] = dv_ref[...].astype(BF16)
        lane = lax.broadcasted_iota(jnp.int32, dksum.shape, 1)
        live = (lane >= HEAD_DIM) & (lane < HEAD_DIM + QK_ROPE)
        dkr_ref[...] = jnp.where(live, _rope_transpose(dksum, c, sa, sb), 0.0)

    return _rows(body, "mla_bwd_prep", S, 256, [dq, dk, dv, *tabs], [],
                 [(WIDTH_P, BF16), (2 * WIDTH_P, BF16), (LANES, F32)])


def _mla_norm_bwd(proj_b, dcq_n, dckv_n, dkr, g_q, g_kv):
    S = proj_b.shape[0]

    def body(step, p_ref, dcq_ref, dckv_ref, dkr_ref, gq_ref, gkv_ref, dp_ref, dgq_ref, dgkv_ref):
        dcq, dgq = _rms_bwd_math(p_ref[:, 0:Q_LORA], gq_ref[...], dcq_ref[...], Q_LORA)
        dckv, dgkv = _rms_bwd_math(p_ref[:, Q_LORA:Q_LORA + KV_LORA], gkv_ref[...], dckv_ref[...], KV_LORA)
        dp_ref[:, 0:Q_LORA] = dcq.astype(BF16)
        dp_ref[:, Q_LORA:Q_LORA + KV_LORA] = dckv.astype(BF16)
        dp_ref[:, Q_LORA + KV_LORA:TAIL_P] = dkr_ref[...].astype(BF16)
        _acc_add(step, dgq_ref, jnp.sum(dgq, axis=0, keepdims=True))
        _acc_add(step, dgkv_ref, jnp.sum(dgkv, axis=0, keepdims=True))

    return _rows(body, "mla_norm_bwd", S, 512, [proj_b, dcq_n, dckv_n, dkr], [g_q, g_kv], [(TAIL_P, BF16)],
                 [((1, Q_LORA), F32), ((1, KV_LORA), F32)])


def _pad_cols(w, d):
    lead = w.shape[:-1]
    w = w.reshape(lead + (N_HEADS, d))
    w = jnp.pad(w, [(0, 0)] * len(lead) + [(0, 0), (0, LANES - d)])
    return w.reshape(lead + (N_HEADS * LANES,))


def _unpad_cols(w, d):
    lead = w.shape[:-1]
    return w.reshape(lead + (N_HEADS, LANES))[..., :d].reshape(lead + (N_HEADS * d,))


def _pad_weights(w):
    w_in = w['w_in']
    zeros = lambda n: jnp.zeros((D_MODEL, n), w_in.dtype)
    p = {}
    p['w_in_a'] = jnp.concatenate([_pad_cols(w_in[:, i * WIDTH:(i + 1) * WIDTH], HEAD_DIM) for i in range(3)], axis=1)
    p['w_in_b'] = jnp.concatenate([w_in[:, 3 * WIDTH:3 * WIDTH + Q_LORA + KV_LORA], zeros(HEAD_DIM),
                                   w_in[:, D_IN - QK_ROPE:], zeros(LANES - HEAD_DIM - QK_ROPE)], axis=1)
    p['w_uq'] = _pad_cols(w['w_uq'], HEAD_DIM + QK_ROPE)
    kv = w['w_ukv'].reshape(KV_LORA, N_HEADS, 2 * HEAD_DIM)
    p['w_ukv'] = jnp.concatenate([_pad_cols(kv[:, :, :HEAD_DIM].reshape(KV_LORA, WIDTH), HEAD_DIM),
                                  _pad_cols(kv[:, :, HEAD_DIM:].reshape(KV_LORA, WIDTH), HEAD_DIM)], axis=1)
    p['w_o'] = jnp.concatenate(
        [_pad_cols(w['w_o'][i * WIDTH:(i + 1) * WIDTH].T, HEAD_DIM).T for i in range(2)], axis=0)
    p['g_a'] = _pad_cols(w['out_norm_a'], HEAD_DIM)
    p['g_b'] = _pad_cols(w['out_norm_b'], HEAD_DIM)
    return p


def _unpad_grads(d):
    g = {}
    dwa = d['w_in_a']
    tail = d['w_in_b']
    g['w_in'] = jnp.concatenate(
        [_unpad_cols(dwa[:, i * WIDTH_P:(i + 1) * WIDTH_P], HEAD_DIM) for i in range(3)]
        + [tail[:, :Q_LORA + KV_LORA], tail[:, Q_LORA + KV_LORA + HEAD_DIM:Q_LORA + KV_LORA + HEAD_DIM + QK_ROPE]],
        axis=1)
    g['w_uq'] = _unpad_cols(d['w_uq'], HEAD_DIM + QK_ROPE)
    dk = _unpad_cols(d['w_ukv'][:, :WIDTH_P], HEAD_DIM).reshape(KV_LORA, N_HEADS, HEAD_DIM)
    dv = _unpad_cols(d['w_ukv'][:, WIDTH_P:], HEAD_DIM).reshape(KV_LORA, N_HEADS, HEAD_DIM)
    g['w_ukv'] = jnp.concatenate([dk, dv], axis=2).reshape(KV_LORA, 2 * WIDTH)
    g['w_o'] = jnp.concatenate(
        [_unpad_cols(d['w_o'][i * WIDTH_P:(i + 1) * WIDTH_P].T, HEAD_DIM).T for i in range(2)], axis=0)
    g['out_norm_a'] = _unpad_cols(d['g_a'], HEAD_DIM)
    g['out_norm_b'] = _unpad_cols(d['g_b'], HEAD_DIM)
    return g


def _local_step(x, target, w):
    S = x.shape[0]
    p = _pad_weights(w)
    tabs = _rope_tables(S)
    bias = _band_bias()
    T = BAND_BLOCK

    h1 = _rms_fwd(x, w['norm_mix_pre'], "rms_mix_pre")
    proj_a = _mm(h1, p['w_in_a'], 'nn', BF16, "mm_in_a")
    proj_b = _mm(h1, p['w_in_b'], 'nn', F32, "mm_in_b")
    oa, lse_a = _attn_fwd(proj_a, proj_a, proj_a, 0, N_HEADS, 2 * N_HEADS, DIL_SCALE, bias, T, T, "dil_fwd")
    cq_n, ckv_n, kr = _mla_prep(proj_b, w['q_lat_norm'], w['kv_lat_norm'], tabs)
    q_lin = _mm(cq_n, p['w_uq'], 'nn', F32, "mm_uq")
    kv_lin = _mm(ckv_n, p['w_ukv'], 'nn', F32, "mm_ukv")
    qb, kb, vb, kt, vt = _mla_qkv(q_lin, kv_lin, kr, tabs)
    ob, lse_b = _mla_fwd(qb, kb, vt)
    cat = _outnorm_fwd(oa, ob, p['g_a'], p['g_b'])
    y = _mm(cat, p['w_o'], 'nn', F32, "mm_o")
    x1, h2 = _post_mix(x, y, w['norm_mix_post'], w['norm_ffn_pre'])
    u0 = _mm(h2, w['w_up'], 'nn', F32, "mm_up")
    ug, uv, a = _conv_fwd(u0, w['conv_w'], w['conv_b'])
    y2 = _mm(a, w['w_down'], 'nn', F32, "mm_down")
    dx2, dy2, dg_ffn_post, _, loss = _final(x1, y2, w['norm_ffn_post'], target)

    g = {'norm_ffn_post': dg_ffn_post}
    da = _mm(dy2, w['w_down'], 'nt', F32, "mm_down_dx")
    g['w_down'] = _mm(a, dy2, 'tn', F32, "mm_down_dw")
    du0, g['conv_w'], g['conv_b'] = _conv_bwd(u0, ug, uv, da, w['conv_w'])
    dh2 = _mm(du0, w['w_up'], 'nt', F32, "mm_up_dx")
    g['w_up'] = _mm(h2, du0, 'tn', F32, "mm_up_dw")
    dx1, g['norm_ffn_pre'] = _rms_bwd(x1, w['norm_ffn_pre'], [dh2], dx2, F32, "rms_ffn_pre_bwd")
    dy, g['norm_mix_post'] = _rms_bwd(y, w['norm_mix_post'], [dx1], None, BF16, "rms_mix_post_bwd")
    dcat = _mm(dy, p['w_o'], 'nt', F32, "mm_o_dx")
    dpad = {'w_o': _mm(cat, dy, 'tn', F32, "mm_o_dw")}
    do_a, do_b, dl_a, dl_b, dpad['g_a'], dpad['g_b'] = _outnorm_bwd(oa, ob, p['g_a'], p['g_b'], dcat)

    dq_b, dk_b, dv_b = _mla_bwd(qb, kb, vb, kt, do_b, ob, lse_b)
    dq_pre, dkv, dkr = _mla_bwd_prep(dq_b, dk_b, dv_b, tabs)
    dcq_n = _mm(dq_pre, p['w_uq'], 'nt', F32, "mm_uq_dx")
    dpad['w_uq'] = _mm(cq_n, dq_pre, 'tn', F32, "mm_uq_dw")
    dckv_n = _mm(dkv, p['w_ukv'], 'nt', F32, "mm_ukv_dx")
    dpad['w_ukv'] = _mm(ckv_n, dkv, 'tn', F32, "mm_ukv_dw")
    dproj_b, g['q_lat_norm'], g['kv_lat_norm'] = _mla_norm_bwd(proj_b, dcq_n, dckv_n, dkr,
                                                               w['q_lat_norm'], w['kv_lat_norm'])

    dq_a, dk_a, dv_a = _attn_bwd(proj_a, proj_a, proj_a, do_a, lse_a, dl_a, 0, N_HEADS, 2 * N_HEADS, DIL_SCALE,
                                 bias, T, T, "dil_bwd")
    dh1 = _mm(dproj_b, p['w_in_b'], 'nt', F32, "mm_in_b_dx")
    dw_parts = []
    for i, dpart in enumerate((dq_a, dk_a, dv_a)):
        w_part = p['w_in_a'][:, i * WIDTH_P:(i + 1) * WIDTH_P]
        dh1 = _mm(dpart, w_part, 'nt', F32, "mm_in_a_dx%d" % i, add=dh1)
        dw_parts.append(_mm(h1, dpart, 'tn', F32, "mm_in_a_dw%d" % i))
    dpad['w_in_a'] = jnp.concatenate(dw_parts, axis=1)
    dpad['w_in_b'] = _mm(h1, dproj_b, 'tn', F32, "mm_in_b_dw")
    grad_x, g['norm_mix_pre'] = _rms_bwd(x, w['norm_mix_pre'], [dh1], dx1, F32, "rms_mix_pre_bwd")
    g.update(_unpad_grads(dpad))
    return loss, grad_x, g


MESH = pl.DeviceIdType.MESH
ANY = pl.BlockSpec(memory_space=pl.ANY)


def _place():
    x, y, c = lax.axis_index("x"), lax.axis_index("y"), lax.axis_index("c")
    chips = [(1 - x, y), (x, 1 - y), (1 - x, 1 - y)]
    return x, y, c, chips


def _all_gather(bufs):
    n = len(bufs)

    def body(*refs):
        in_refs, out_refs = refs[:n], refs[n:2 * n]
        send_sems, recv_sems, local_sems = refs[2 * n:]
        x, y, c, chips = _place()
        me = 2 * x + y
        local = [pltpu.make_async_copy(in_refs[b], out_refs[b].at[me], local_sems.at[b]) for b in range(n)]
        for cp in local:
            cp.start()
        sends = []
        for j, (px, py) in enumerate(chips):
            for b in range(n):
                sends.append(pltpu.make_async_remote_copy(
                    src_ref=in_refs[b], dst_ref=out_refs[b].at[me], send_sem=send_sems.at[j * n + b],
                    recv_sem=recv_sems.at[j * n + b], device_id=(px, py, c), device_id_type=MESH))
        for cp in sends:
            cp.start()
        for j, (px, py) in enumerate(chips):
            for b in range(n):
                pltpu.make_async_remote_copy(
                    src_ref=in_refs[b], dst_ref=out_refs[b].at[2 * px + py], send_sem=send_sems.at[j * n + b],
                    recv_sem=recv_sems.at[j * n + b], device_id=(px, py, c), device_id_type=MESH).wait_recv()
        for cp in sends:
            cp.wait_send()
        for cp in local:
            cp.wait()

    return pl.pallas_call(
        body, name="gather_weights", in_specs=[ANY] * n, out_specs=[ANY] * n,
        out_shape=[jax.ShapeDtypeStruct((N_CHIPS,) + b.shape, b.dtype) for b in bufs],
        scratch_shapes=[pltpu.SemaphoreType.DMA((3 * n,)), pltpu.SemaphoreType.DMA((3 * n,)),
                        pltpu.SemaphoreType.DMA((n,))],
    )(*bufs)


def _scatter_grads(gflat):
    _, R, L = gflat.shape

    def body(g_ref, o_ref, send_sems, recv_sems, local_sem):
        x, y, c, chips = _place()
        me = 2 * x + y
        local = pltpu.make_async_copy(g_ref.at[me], o_ref.at[me], local_sem)
        local.start()
        sends = [pltpu.make_async_remote_copy(
            src_ref=g_ref.at[2 * px + py], dst_ref=o_ref.at[me], send_sem=send_sems.at[j], recv_sem=recv_sems.at[j],
            device_id=(px, py, c), device_id_type=MESH) for j, (px, py) in enumerate(chips)]
        for cp in sends:
            cp.start()
        for j, (px, py) in enumerate(chips):
            pltpu.make_async_remote_copy(
                src_ref=g_ref.at[me], dst_ref=o_ref.at[2 * px + py], send_sem=send_sems.at[j],
                recv_sem=recv_sems.at[j], device_id=(px, py, c), device_id_type=MESH).wait_recv()
        for cp in sends:
            cp.wait_send()
        local.wait()

    return pl.pallas_call(
        body, name="scatter_grads", in_specs=[ANY], out_specs=ANY,
        out_shape=jax.ShapeDtypeStruct(gflat.shape, gflat.dtype),
        scratch_shapes=[pltpu.SemaphoreType.DMA((3,)), pltpu.SemaphoreType.DMA((3,)), pltpu.SemaphoreType.DMA],
    )(gflat)


def _sum_slots(recv):
    _, R, L = recv.shape
    tr = _tile(R, 2048)

    def body(r_ref, o_ref):
        o_ref[...] = ((r_ref[0] + r_ref[1]) + r_ref[2]) + r_ref[3]

    return pl.pallas_call(
        body, name="sum_slots", grid=(R // tr,),
        in_specs=[pl.BlockSpec((N_CHIPS, tr, L), lambda i: (0, i, 0))],
        out_specs=pl.BlockSpec((tr, L), lambda i: (i, 0)),
        out_shape=jax.ShapeDtypeStruct((R, L), recv.dtype),
        compiler_params=_cparams(("parallel",)),
    )(recv)


def _swap_sibling(part):
    def body(p_ref, o_ref, send_sem, recv_sem):
        x, y, c, _ = _place()
        cp = pltpu.make_async_remote_copy(src_ref=p_ref, dst_ref=o_ref, send_sem=send_sem, recv_sem=recv_sem,
                                          device_id=(x, y, 1 - c), device_id_type=MESH)
        cp.start()
        cp.wait()

    return pl.pallas_call(
        body, name="swap_sibling", in_specs=[ANY], out_specs=ANY,
        out_shape=jax.ShapeDtypeStruct(part.shape, part.dtype),
        scratch_shapes=[pltpu.SemaphoreType.DMA, pltpu.SemaphoreType.DMA],
    )(part)


def _adamw(g0, g1, w, m, v):
    R, L = w.shape
    tr = _tile(R, 2048)
    bc1 = 1.0 - ADAM_B1 ** ADAM_STEP
    bc2 = 1.0 - ADAM_B2 ** ADAM_STEP

    def body(g0_ref, g1_ref, w_ref, m_ref, v_ref, g_ref, d_ref, nm_ref, nv_ref):
        g = g0_ref[...] + g1_ref[...]
        g_ref[...] = g
        nm = ADAM_B1 * m_ref[...] + (1.0 - ADAM_B1) * g
        nv = ADAM_B2 * v_ref[...] + (1.0 - ADAM_B2) * (g * g)
        nm_ref[...] = nm
        nv_ref[...] = nv
        d_ref[...] = -ADAM_LR * ((nm / bc1) / (jnp.sqrt(nv / bc2) + ADAM_EPS) + ADAM_WD * w_ref[...])

    spec = pl.BlockSpec((tr, L), lambda i: (i, 0))
    out = jax.ShapeDtypeStruct((R, L), F32)
    return pl.pallas_call(
        body, name="adamw", grid=(R // tr,), in_specs=[spec] * 5, out_specs=[spec] * 4, out_shape=[out] * 4,
        compiler_params=_cparams(("parallel",)),
    )(g0, g1, w, m, v)


FLAT_ALIGN = 16 * LANES


def _flatten(parts, dtype):
    flat = jnp.concatenate([p.reshape(-1).astype(dtype) for p in parts])
    n = flat.shape[0]
    npad = -(-n // (FLAT_ALIGN * 16)) * (FLAT_ALIGN * 16)
    return jnp.pad(flat, (0, npad - n)).reshape(npad // LANES, LANES)


def _unflatten(flat, shapes):
    flat = flat.reshape(-1)
    out, pos = [], 0
    for s in shapes:
        n = int(np.prod(s))
        out.append(flat[pos:pos + n].reshape(s))
        pos += n
    return out


def _shard(a, axis, i):
    n = a.shape[axis] // N_CHIPS
    return lax.slice_in_dim(a, i * n, (i + 1) * n, axis=axis)


def kernel(x, norm_mix_pre, w_in, q_lat_norm, w_uq, kv_lat_norm, w_ukv, out_norm_a, out_norm_b, w_o, norm_mix_post, norm_ffn_pre, w_up, conv_w, conv_b, w_down, norm_ffn_post, loss_target, m_norm_mix_pre, m_w_in, m_q_lat_norm, m_w_uq, m_kv_lat_norm, m_w_ukv, m_out_norm_a, m_out_norm_b, m_w_o, m_norm_mix_post, m_norm_ffn_pre, m_w_up, m_conv_w, m_conv_b, m_w_down, m_norm_ffn_post, v_norm_mix_pre, v_w_in, v_q_lat_norm, v_w_uq, v_kv_lat_norm, v_w_ukv, v_out_norm_a, v_out_norm_b, v_w_o, v_norm_mix_post, v_norm_ffn_pre, v_w_up, v_conv_w, v_conv_b, v_w_down, v_norm_ffn_post):
    args = dict(locals())
    strip = lambda a: a[0] if a.ndim == 3 else a
    wl = {n: strip(args[n]) for n in WEIGHTS}
    ml = {n: strip(args['m_' + n]) for n in WEIGHTS}
    vl = {n: strip(args['v_' + n]) for n in WEIGHTS}

    big_shapes = [wl[n].shape for n in BIG]
    gathered_big, gathered_cw = _all_gather([_flatten([wl[n] for n in BIG], BF16), _flatten([wl['conv_w']], F32)])
    full = {n: wl[n] for n in SMALL}
    per_chip = [_unflatten(gathered_big[i], big_shapes) for i in range(N_CHIPS)]
    for k, n in enumerate(BIG):
        full[n] = jnp.concatenate([per_chip[i][k] for i in range(N_CHIPS)], axis=SHARD_AXIS[n])
    full['conv_w'] = jnp.concatenate(
        [_unflatten(gathered_cw[i], [wl['conv_w'].shape])[0] for i in range(N_CHIPS)], axis=1)

    loss_b, grad_x, g = _local_step(x[0], loss_target[0], full)

    sharded = [n for n in WEIGHTS if SHARD_AXIS[n] is not None]
    slots = []
    for i in range(N_CHIPS):
        slots.append(_flatten([_shard(g[n], SHARD_AXIS[n], i) for n in sharded] + [g[n] for n in SMALL], F32))
    recv = _scatter_grads(jnp.stack(slots))
    part = _sum_slots(recv)
    other = _swap_sibling(part)

    order = sharded + SMALL
    shapes = [wl[n].shape for n in order]
    g_f, d_f, nm_f, nv_f = _adamw(part, other, _flatten([wl[n] for n in order], F32),
                                  _flatten([ml[n] for n in order], F32), _flatten([vl[n] for n in order], F32))
    outs = {}
    for tag, flat in (('grad', g_f), ('delta', d_f), ('new_m', nm_f), ('new_v', nv_f)):
        for n, a in zip(order, _unflatten(flat, shapes)):
            outs[tag + '_' + n] = a.reshape(args[n].shape)

    loss = lax.psum(loss_b[0, 0], ("x", "y", "c"))
    return (loss, grad_x[None], *[outs['grad_' + n] for n in WEIGHTS], *[outs['delta_' + n] for n in WEIGHTS],
            *[outs['new_m_' + n] for n in WEIGHTS], *[outs['new_v_' + n] for n in WEIGHTS])
```

```python
import functools
import math

import jax
import jax.numpy as jnp
import numpy as np
from jax import lax
from jax.experimental import pallas as pl
from jax.experimental.pallas import tpu as pltpu

F32 = jnp.float32
BF16 = jnp.bfloat16

LANES = 128
D_MODEL = 1024
N_HEADS = 8
HEAD_DIM = 64
QK_ROPE = 32
Q_LORA = 384
KV_LORA = 256
D_FF = 2816
WIDTH = N_HEADS * HEAD_DIM
WIDTH_P = N_HEADS * LANES
IN_SIZES = (WIDTH, WIDTH, WIDTH, Q_LORA, KV_LORA, QK_ROPE)
D_IN = sum(IN_SIZES)
TAIL_P = Q_LORA + KV_LORA + LANES
EPS = 1e-6
ROPE_BASE = 10000.0
MASKED = -2e30
M_INIT = -1e30
MLA_TQ = 2048
MLA_TK = 256
MLA_SCALE = (HEAD_DIM + QK_ROPE) ** -0.5
DIL_SCALE = HEAD_DIM ** -0.5

ADAM_LR = 0.001
ADAM_B1 = 0.9
ADAM_B2 = 0.999
ADAM_EPS = 1e-08
ADAM_WD = 0.01
ADAM_STEP = 10

VMEM_LIMIT = 56 * 1024 * 1024

N_CHIPS = 4

WEIGHTS = ['norm_mix_pre', 'w_in', 'q_lat_norm', 'w_uq', 'kv_lat_norm', 'w_ukv', 'out_norm_a', 'out_norm_b',
           'w_o', 'norm_mix_post', 'norm_ffn_pre', 'w_up', 'conv_w', 'conv_b', 'w_down', 'norm_ffn_post']
SHARD_AXIS = {'norm_mix_pre': None, 'w_in': 1, 'q_lat_norm': None, 'w_uq': 1, 'kv_lat_norm': None, 'w_ukv': 1,
              'out_norm_a': None, 'out_norm_b': None, 'w_o': 0, 'norm_mix_post': None, 'norm_ffn_pre': None,
              'w_up': 1, 'conv_w': 1, 'conv_b': None, 'w_down': 0, 'norm_ffn_post': None}
BIG = ['w_in', 'w_uq', 'w_ukv', 'w_o', 'w_up', 'w_down']
SMALL = [n for n in WEIGHTS if SHARD_AXIS[n] is None]


def _tile(dim, target):
    best = None
    t = LANES
    while t <= min(dim, target):
        if dim % t == 0:
            best = t
        t += LANES
    return best if best is not None else dim


def _cparams(sem=None):
    return pltpu.CompilerParams(dimension_semantics=sem, vmem_limit_bytes=VMEM_LIMIT)


def _mm(a, b, mode, out_dtype, name, add=None, tm=1024, tn=1024, tk=1024):
    if mode == 'nn':
        (M, K), (K2, N) = a.shape, b.shape
        dims = (((1,), (0,)), ((), ()))
    elif mode == 'nt':
        (M, K), (N, K2) = a.shape, b.shape
        dims = (((1,), (1,)), ((), ()))
    else:
        (K, M), (K2, N) = a.shape, b.shape
        dims = (((0,), (0,)), ((), ()))
    assert K == K2, (a.shape, b.shape, mode)
    tm, tn, tk = _tile(M, tm), _tile(N, tn), _tile(K, tk)
    nk = K // tk
    if mode == 'nn':
        a_spec = pl.BlockSpec((tm, tk), lambda i, j, k: (i, k))
        b_spec = pl.BlockSpec((tk, tn), lambda i, j, k: (k, j))
    elif mode == 'nt':
        a_spec = pl.BlockSpec((tm, tk), lambda i, j, k: (i, k))
        b_spec = pl.BlockSpec((tn, tk), lambda i, j, k: (j, k))
    else:
        a_spec = pl.BlockSpec((tk, tm), lambda i, j, k: (k, i))
        b_spec = pl.BlockSpec((tk, tn), lambda i, j, k: (k, j))
    o_spec = pl.BlockSpec((tm, tn), lambda i, j, k: (i, j))
    has_add = add is not None

    def body(*refs):
        if has_add:
            a_ref, b_ref, add_ref, o_ref, acc_ref = refs
        else:
            a_ref, b_ref, o_ref, acc_ref = refs
        k = pl.program_id(2)

        @pl.when(k == 0)
        def _():
            acc_ref[...] = jnp.zeros_like(acc_ref)

        acc_ref[...] += lax.dot_general(a_ref[...].astype(BF16), b_ref[...].astype(BF16), dims,
                                        preferred_element_type=F32)

        @pl.when(k == nk - 1)
        def _():
            r = acc_ref[...]
            if has_add:
                r = r + add_ref[...]
            o_ref[...] = r.astype(o_ref.dtype)

    ins = [a, b] + ([add] if has_add else [])
    in_specs = [a_spec, b_spec] + ([o_spec] if has_add else [])
    return pl.pallas_call(
        body, name=name, grid=(M // tm, N // tn, nk), in_specs=in_specs, out_specs=o_spec,
        out_shape=jax.ShapeDtypeStruct((M, N), out_dtype),
        scratch_shapes=[pltpu.VMEM((tm, tn), F32)],
        compiler_params=_cparams(("parallel", "parallel", "arbitrary")),
    )(*ins)


def _rows(body, name, S, ts, row_ins, full_ins, row_outs, acc_outs=(), chunk_outs=()):
    in_specs = [pl.BlockSpec((ts, a.shape[1]), lambda i: (i, 0)) for a in row_ins]
    in_specs += [pl.BlockSpec(a.shape, lambda i, nd=a.ndim: (0,) * nd) for a in full_ins]
    out_specs = [pl.BlockSpec((ts, w), lambda i: (i, 0)) for (w, _) in row_outs]
    out_specs += [pl.BlockSpec(shape, lambda i, nd=len(shape): (0,) * nd) for (shape, _) in acc_outs]
    out_specs += [pl.BlockSpec((lead, 1, LANES, ts), lambda i: (0, i, 0, 0)) for (lead, _) in chunk_outs]
    out_shape = [jax.ShapeDtypeStruct((S, w), dt) for (w, dt) in row_outs]
    out_shape += [jax.ShapeDtypeStruct(shape, dt) for (shape, dt) in acc_outs]
    out_shape += [jax.ShapeDtypeStruct((lead, S // ts, LANES, ts), dt) for (lead, dt) in chunk_outs]

    def kbody(*refs):
        body(pl.program_id(0), *refs)

    return pl.pallas_call(
        kbody, name=name, grid=(S // ts,), in_specs=in_specs, out_specs=out_specs, out_shape=out_shape,
        compiler_params=_cparams(("arbitrary",)),
    )(*row_ins, *full_ins)


def _acc_add(step, ref, val):
    @pl.when(step == 0)
    def _():
        ref[...] = val

    @pl.when(step != 0)
    def _():
        ref[...] += val


def _rms_fwd(x, g, name):
    S, W = x.shape

    def body(step, x_ref, g_ref, h_ref):
        xv = x_ref[...]
        r = lax.rsqrt(jnp.mean(xv * xv, axis=-1, keepdims=True) + EPS)
        h_ref[...] = (xv * r * g_ref[...]).astype(BF16)

    return _rows(body, name, S, 512, [x], [g], [(W, BF16)])[0]


def _rms_bwd_math(xv, g, dy, width):
    r = lax.rsqrt(jnp.sum(xv * xv, axis=-1, keepdims=True) * (1.0 / width) + EPS)
    xn = xv * r
    dyg = dy * g
    dx = r * (dyg - xn * (jnp.sum(dyg * xn, axis=-1, keepdims=True) * (1.0 / width)))
    return dx, dy * xn


def _rms_bwd(x, g, dys, resid, out_dtype, name):
    S, W = x.shape
    nd = len(dys)
    has_res = resid is not None

    def body(step, *refs):
        x_ref = refs[0]
        dy_refs = refs[1:1 + nd]
        pos = 1 + nd
        res_ref = refs[pos] if has_res else None
        pos += int(has_res)
        g_ref, dx_ref, dg_ref = refs[pos], refs[pos + 1], refs[pos + 2]
        dy = dy_refs[0][...].astype(F32)
        for r_ in dy_refs[1:]:
            dy = dy + r_[...].astype(F32)
        dx, dgr = _rms_bwd_math(x_ref[...], g_ref[...], dy, W)
        if has_res:
            dx = dx + res_ref[...]
        dx_ref[...] = dx.astype(dx_ref.dtype)
        _acc_add(step, dg_ref, jnp.sum(dgr, axis=0, keepdims=True))

    row_ins = [x] + list(dys) + ([resid] if has_res else [])
    dx, dg = _rows(body, name, S, 256, row_ins, [g], [(W, out_dtype)], [((1, W), F32)])
    return dx, dg


def _rope_apply(xv, c, sa, sb):
    return xv * c + pltpu.roll(xv, 16, 1) * sa + pltpu.roll(xv, LANES - 16, 1) * sb


def _rope_transpose(dy, c, sa, sb):
    return dy * c + pltpu.roll(dy * sa, LANES - 16, 1) + pltpu.roll(dy * sb, 16, 1)


def _rope_tables(S):
    pos = jnp.arange(S, dtype=F32)
    inv_freq = jnp.exp(-math.log(ROPE_BASE) * jnp.arange(0, QK_ROPE, 2, dtype=F32) / QK_ROPE)
    ang = pos[:, None] * inv_freq[None, :]
    cos, sin = jnp.cos(ang), jnp.sin(ang)
    ones, zeros = jnp.ones((S, HEAD_DIM), F32), jnp.zeros((S, HEAD_DIM), F32)
    z16, z32 = jnp.zeros((S, 16), F32), jnp.zeros((S, 32), F32)
    c = jnp.concatenate([ones, cos, cos, z32], axis=1)
    sa = jnp.concatenate([zeros, z16, sin, z32], axis=1)
    sb = jnp.concatenate([zeros, -sin, z16, z32], axis=1)
    return c, sa, sb


def _mla_prep(proj_b, g_q, g_kv, tabs):
    S = proj_b.shape[0]

    def body(step, p_ref, c_ref, sa_ref, sb_ref, gq_ref, gkv_ref, cq_ref, ckv_ref, kr_ref):
        cq = p_ref[:, 0:Q_LORA]
        ckv = p_ref[:, Q_LORA:Q_LORA + KV_LORA]
        kr = p_ref[:, Q_LORA + KV_LORA:TAIL_P]
        rq = lax.rsqrt(jnp.mean(cq * cq, axis=-1, keepdims=True) + EPS)
        cq_ref[...] = (cq * rq * gq_ref[...]).astype(BF16)
        rk = lax.rsqrt(jnp.mean(ckv * ckv, axis=-1, keepdims=True) + EPS)
        ckv_ref[...] = (ckv * rk * gkv_ref[...]).astype(BF16)
        kr_ref[...] = _rope_apply(kr, c_ref[...], sa_ref[...], sb_ref[...])

    return _rows(body, "mla_prep", S, 512, [proj_b, *tabs], [g_q, g_kv],
                 [(Q_LORA, BF16), (KV_LORA, BF16), (LANES, F32)])


def _mla_qkv(q, kv, kr, tabs):
    S = q.shape[0]

    def body(step, q_ref, kv_ref, kr_ref, c_ref, sa_ref, sb_ref, qb_ref, kb_ref, vb_ref, kt_ref, vt_ref):
        c, sa, sb = c_ref[...], sa_ref[...], sb_ref[...]
        krv = kr_ref[...]
        row = lax.broadcasted_iota(jnp.int32, (LANES, MLA_TK), 0)
        for h in range(N_HEADS):
            blk = slice(h * LANES, (h + 1) * LANES)
            qb_ref[:, blk] = (_rope_apply(q_ref[:, blk], c, sa, sb) * MLA_SCALE).astype(BF16)
            kh = kv_ref[:, blk] + krv
            kb_ref[:, blk] = kh.astype(BF16)
            kt_ref[h, 0] = kh.T.astype(BF16)
            vh = kv_ref[:, WIDTH_P + h * LANES:WIDTH_P + (h + 1) * LANES]
            vt_ref[h, 0] = jnp.where(row == HEAD_DIM, 1.0, vh.T).astype(BF16)
        vb_ref[...] = kv_ref[:, WIDTH_P:2 * WIDTH_P].astype(BF16)

    return _rows(body, "mla_qkv", S, MLA_TK, [q, kv, kr, *tabs], [],
                 [(WIDTH_P, BF16), (WIDTH_P, BF16), (WIDTH_P, BF16)],
                 chunk_outs=[(N_HEADS, BF16), (N_HEADS, BF16)])


def _outnorm_fwd(oa, ob, ga, gb):
    S = oa.shape[0]

    def body(step, oa_ref, ob_ref, ga_ref, gb_ref, cat_ref):
        for o_ref, g_ref, off in ((oa_ref, ga_ref, 0), (ob_ref, gb_ref, WIDTH_P)):
            o = o_ref[...]
            r = lax.rsqrt(jnp.sum(o * o, axis=-1, keepdims=True) * (1.0 / WIDTH) + EPS)
            cat_ref[:, off:off + WIDTH_P] = (o * r * g_ref[...]).astype(BF16)

    return _rows(body, "outnorm_fwd", S, 256, [oa, ob], [ga, gb], [(2 * WIDTH_P, BF16)])[0]


def _outnorm_bwd(oa, ob, ga, gb, dcat):
    S = oa.shape[0]

    def body(step, oa_ref, ob_ref, dcat_ref, ga_ref, gb_ref, doa_ref, dob_ref, dla_ref, dlb_ref, dga_ref, dgb_ref):
        for o_ref, g_ref, off, do_ref, dl_ref, dg_ref in (
                (oa_ref, ga_ref, 0, doa_ref, dla_ref, dga_ref), (ob_ref, gb_ref, WIDTH_P, dob_ref, dlb_ref, dgb_ref)):
            o = o_ref[...]
            do, dgr = _rms_bwd_math(o, g_ref[...], dcat_ref[:, off:off + WIDTH_P], WIDTH)
            do_ref[...] = do.astype(BF16)
            _acc_add(step, dg_ref, jnp.sum(dgr, axis=0, keepdims=True))
            prod = do.astype(BF16).astype(F32) * o
            for h in range(N_HEADS):
                blk = slice(h * LANES, (h + 1) * LANES)
                d = jnp.sum(prod[:, blk], axis=-1, keepdims=True)
                dl_ref[:, blk] = jnp.broadcast_to(d, (d.shape[0], LANES))

    return _rows(body, "outnorm_bwd", S, 256, [oa, ob, dcat], [ga, gb],
                 [(WIDTH_P, BF16), (WIDTH_P, BF16), (WIDTH_P, F32), (WIDTH_P, F32)],
                 [((1, WIDTH_P), F32), ((1, WIDTH_P), F32)])


def _post_mix(x, y, g_post, g_pre):
    S, W = x.shape

    def body(step, x_ref, y_ref, gp_ref, gq_ref, x1_ref, h_ref):
        yv = y_ref[...]
        r = lax.rsqrt(jnp.mean(yv * yv, axis=-1, keepdims=True) + EPS)
        x1 = x_ref[...] + yv * r * gp_ref[...]
        x1_ref[...] = x1
        r1 = lax.rsqrt(jnp.mean(x1 * x1, axis=-1, keepdims=True) + EPS)
        h_ref[...] = (x1 * r1 * gq_ref[...]).astype(BF16)

    return _rows(body, "post_mix", S, 512, [x, y], [g_post, g_pre], [(W, F32), (W, BF16)])


def _final(x1, y2, g, target):
    S, W = x1.shape
    nsteps = S // 256

    def body(step, x1_ref, y_ref, t_ref, g_ref, dx2_ref, dy_ref, dg_ref, sq_ref, loss_ref):
        yv = y_ref[...]
        gv = g_ref[...]
        r = lax.rsqrt(jnp.mean(yv * yv, axis=-1, keepdims=True) + EPS)
        yn = yv * r
        err = (x1_ref[...] + yn * gv) - t_ref[...]
        dx2 = err * (1.0 / W)
        dx2_ref[...] = dx2
        dyg = dx2 * gv
        dy = r * (dyg - yn * jnp.mean(dyg * yn, axis=-1, keepdims=True))
        dy_ref[...] = dy.astype(BF16)
        _acc_add(step, dg_ref, jnp.sum(dx2 * yn, axis=0, keepdims=True))
        _acc_add(step, sq_ref, jnp.sum(err * err, axis=0, keepdims=True))

        @pl.when(step == nsteps - 1)
        def _():
            tot = jnp.sum(sq_ref[...], axis=-1, keepdims=True) * (0.5 / W)
            loss_ref[...] = jnp.broadcast_to(tot, (1, LANES))

    return _rows(body, "final_loss", S, 256, [x1, y2, target], [g], [(W, F32), (W, BF16)],
                 [((1, W), F32), ((1, W), F32), ((1, LANES), F32)])


_GELU_C = math.sqrt(2.0 / math.pi)
_CONV_CHUNK = 512
_HALO = 8


def _gelu(g):
    t = jnp.tanh(_GELU_C * (g + 0.044715 * (g * g * g)))
    return g * (0.5 * (1.0 + t)), t


def _fill_padded(pad_ref, src_ref, S):
    zeros = jnp.zeros((_HALO, LANES), F32)
    pad_ref[0:_HALO, :] = zeros
    pad_ref[_HALO + S:2 * _HALO + S, :] = zeros
    for r0 in range(0, S, _CONV_CHUNK):
        pad_ref[_HALO + r0:_HALO + r0 + _CONV_CHUNK, :] = src_ref[r0:r0 + _CONV_CHUNK, :].astype(F32)


def _conv_fwd(u0, conv_w, conv_b):
    S, C2 = u0.shape
    nb = D_FF // LANES

    def body(u0g_ref, u0v_ref, wg_ref, wv_ref, bg_ref, bv_ref, ug_ref, uv_ref, a_ref, pg_ref, pv_ref):
        _fill_padded(pg_ref, u0g_ref, S)
        _fill_padded(pv_ref, u0v_ref, S)
        wg, wv = wg_ref[...], wv_ref[...]
        for r0 in range(0, S, _CONV_CHUNK):
            def conv(p_ref, w, b_ref):
                base = _HALO + r0
                return (p_ref[base - 1:base - 1 + _CONV_CHUNK, :] * w[0:1, :]
                        + p_ref[base:base + _CONV_CHUNK, :] * w[1:2, :]
                        + p_ref[base + 1:base + 1 + _CONV_CHUNK, :] * w[2:3, :] + b_ref[...])
            g = conv(pg_ref, wg, bg_ref)
            v = conv(pv_ref, wv, bv_ref)
            rows = slice(r0, r0 + _CONV_CHUNK)
            ug_ref[rows, :] = g
            uv_ref[rows, :] = v
            a_ref[rows, :] = (_gelu(g)[0] * v).astype(BF16)

    col = lambda off: pl.BlockSpec((S, LANES), lambda j: (0, j + off))
    wcol = lambda off: pl.BlockSpec((3, LANES), lambda j: (0, j + off))
    bcol = lambda off: pl.BlockSpec((1, LANES), lambda j: (0, j + off))
    ug, uv, a = pl.pallas_call(
        body, name="conv_gelu_fwd", grid=(nb,),
        in_specs=[col(0), col(nb), wcol(0), wcol(nb), bcol(0), bcol(nb)],
        out_specs=[col(0), col(0), col(0)],
        out_shape=[jax.ShapeDtypeStruct((S, D_FF), F32), jax.ShapeDtypeStruct((S, D_FF), F32),
                   jax.ShapeDtypeStruct((S, D_FF), BF16)],
        scratch_shapes=[pltpu.VMEM((S + 2 * _HALO, LANES), F32), pltpu.VMEM((S + 2 * _HALO, LANES), F32)],
        compiler_params=_cparams(("arbitrary",)),
    )(u0, u0, conv_w, conv_w, conv_b, conv_b)
    return ug, uv, a


def _conv_bwd(u0, ug, uv, da, conv_w):
    S = u0.shape[0]
    nb = D_FF // LANES

    def body(u0_ref, ug_ref, uv_ref, da_ref, w_ref, du0_ref, dw_ref, db_ref, pu_ref, pd_ref):
        is_g = pl.program_id(0) < nb
        _fill_padded(pu_ref, u0_ref, S)
        zeros = jnp.zeros((_HALO, LANES), F32)
        pd_ref[0:_HALO, :] = zeros
        pd_ref[_HALO + S:2 * _HALO + S, :] = zeros
        sel = jnp.where(is_g, 1.0, 0.0).astype(F32)
        for r0 in range(0, S, _CONV_CHUNK):
            rows = slice(r0, r0 + _CONV_CHUNK)
            g, v, d = ug_ref[rows, :], uv_ref[rows, :], da_ref[rows, :]
            gel, t = _gelu(g)
            dgel = 0.5 * (1.0 + t) + (0.5 * g) * (1.0 - t * t) * (_GELU_C * (1.0 + 3.0 * 0.044715 * (g * g)))
            du = d * (sel * (v * dgel) + (1.0 - sel) * gel)
            pd_ref[_HALO + r0:_HALO + r0 + _CONV_CHUNK, :] = du
        w = w_ref[...]
        acc_b = jnp.zeros((1, LANES), F32)
        acc_w = [jnp.zeros((1, LANES), F32) for _ in range(3)]
        for r0 in range(0, S, _CONV_CHUNK):
            base = _HALO + r0
            du_m = pd_ref[base - 1:base - 1 + _CONV_CHUNK, :]
            du_c = pd_ref[base:base + _CONV_CHUNK, :]
            du_p = pd_ref[base + 1:base + 1 + _CONV_CHUNK, :]
            du0_ref[r0:r0 + _CONV_CHUNK, :] = (du_p * w[0:1, :] + du_c * w[1:2, :] + du_m * w[2:3, :]).astype(BF16)
            acc_b = acc_b + jnp.sum(du_c, axis=0, keepdims=True)
            for k in range(3):
                acc_w[k] = acc_w[k] + jnp.sum(du_c * pu_ref[base + k - 1:base + k - 1 + _CONV_CHUNK, :],
                                              axis=0, keepdims=True)
        db_ref[...] = acc_b
        for k in range(3):
            dw_ref[k:k + 1, :] = acc_w[k]

    own = pl.BlockSpec((S, LANES), lambda j: (0, j))
    half = lambda off: pl.BlockSpec((S, LANES), lambda j: (0, j % nb))
    du0, dw, db = pl.pallas_call(
        body, name="conv_gelu_bwd", grid=(2 * nb,),
        in_specs=[own, half(0), half(0), half(0), pl.BlockSpec((3, LANES), lambda j: (0, j))],
        out_specs=[own, pl.BlockSpec((3, LANES), lambda j: (0, j)), pl.BlockSpec((1, LANES), lambda j: (0, j))],
        out_shape=[jax.ShapeDtypeStruct((S, 2 * D_FF), BF16), jax.ShapeDtypeStruct((3, 2 * D_FF), F32),
                   jax.ShapeDtypeStruct((1, 2 * D_FF), F32)],
        scratch_shapes=[pltpu.VMEM((S + 2 * _HALO, LANES), F32), pltpu.VMEM((S + 2 * _HALO, LANES), F32)],
        compiler_params=_cparams(("arbitrary",)),
    )(u0, ug, uv, da, conv_w)
    return du0, dw, db


DIL_Q = 128
DIL_HALF = 64
DIL_SLAB = DIL_Q + 2 * DIL_HALF
DILATIONS = (1, 4, 16)
DIL_SEG = 2048


def _dil_bias(r):
    row = jnp.arange(DIL_Q, dtype=jnp.int32)[:, None]
    col = jnp.arange(DIL_SLAB, dtype=jnp.int32)[None, :]
    ad = jnp.abs(col - DIL_HALF - row)
    slopes = jnp.exp2(-8.0 * jnp.arange(1, N_HEADS + 1, dtype=F32) / N_HEADS)
    base = jnp.where(ad <= DIL_HALF, -slopes[:, None, None] * (ad * r).astype(F32)[None], MASKED)
    before = jnp.broadcast_to(col < DIL_HALF, (DIL_Q, DIL_SLAB))
    after = jnp.broadcast_to(col >= DIL_Q + DIL_HALF, (DIL_Q, DIL_SLAB))
    variants = [base, jnp.where(before, MASKED, base), jnp.where(after, MASKED, base),
                jnp.where(before | after, MASKED, base)]
    return jnp.stack(variants, axis=1)


def _lanes_hi_to_all(x):
    lane = lax.broadcasted_iota(jnp.int32, x.shape, 1)
    return jnp.where(lane < HEAD_DIM, pltpu.roll(x, HEAD_DIM, 1), x)


def _fill_kv(kp_ref, vp_ref, k_ref, v_ref, L, ones):
    zeros = jnp.zeros((DIL_HALF, LANES), BF16)
    for ref in (kp_ref, vp_ref):
        ref[0:DIL_HALF, :] = zeros
        ref[DIL_HALF + L:2 * DIL_HALF + L, :] = zeros
    step = min(L, 512)
    lane = lax.broadcasted_iota(jnp.int32, (step, LANES), 1)
    for r0 in range(0, L, step):
        kp_ref[DIL_HALF + r0:DIL_HALF + r0 + step, :] = k_ref[r0:r0 + step, :]
        vv = v_ref[r0:r0 + step, :]
        vp_ref[DIL_HALF + r0:DIL_HALF + r0 + step, :] = jnp.where(lane < HEAD_DIM, vv, 1.0).astype(BF16) if ones else vv


def _dil_fwd(proj_a, bias, r):
    S = proj_a.shape[0]
    L = S // r
    nblk = L // DIL_Q
    pv = proj_a.reshape(L, r * 3 * WIDTH_P)

    def body(q_ref, k_ref, v_ref, b_ref, o_ref, kp_ref, vp_ref):
        _fill_kv(kp_ref, vp_ref, k_ref, v_ref, L, True)
        lane = lax.broadcasted_iota(jnp.int32, (DIL_Q, LANES), 1)

        def block(i, carry):
            rows = pl.ds(pl.multiple_of(i * DIL_Q, DIL_Q), DIL_Q)
            slab = pl.ds(pl.multiple_of(i * DIL_Q, DIL_Q), DIL_SLAB)
            variant = jnp.where(i == 0, 1, 0) + jnp.where(i == nblk - 1, 2, 0)
            qv = q_ref[rows, :] * DIL_SCALE
            s = lax.dot_general(qv, kp_ref[slab, :], _NT, preferred_element_type=F32) + b_ref[0, variant]
            m = jnp.max(s, axis=-1, keepdims=True)
            acc = jnp.dot(jnp.exp(s - m).astype(BF16), vp_ref[slab, :], preferred_element_type=F32)
            l = _lanes_hi_to_all(acc)
            o_ref[rows, :] = jnp.where(lane < HEAD_DIM, acc / l, m + jnp.log(l))
            return carry

        lax.fori_loop(0, nblk, block, 0, unroll=min(4, nblk))

    col = lambda part: pl.BlockSpec((L, LANES), lambda c, h: (0, c * 3 * N_HEADS + part * N_HEADS + h))
    out = pl.pallas_call(
        body, name="dil_fwd_r%d" % r, grid=(r, N_HEADS),
        in_specs=[col(0), col(1), col(2), pl.BlockSpec((1, 4, DIL_Q, DIL_SLAB), lambda c, h: (h, 0, 0, 0))],
        out_specs=pl.BlockSpec((L, LANES), lambda c, h: (0, c * N_HEADS + h)),
        out_shape=jax.ShapeDtypeStruct((L, r * WIDTH_P), F32),
        scratch_shapes=[pltpu.VMEM((L + 2 * DIL_HALF, LANES), BF16), pltpu.VMEM((L + 2 * DIL_HALF, LANES), BF16)],
        compiler_params=_cparams(("parallel", "parallel")),
    )(pv, pv, pv, bias)
    return out.reshape(S, WIDTH_P)


def _dil_combine(branches):
    S = branches[0].shape[0]

    def body(step, *refs):
        o_ref, lse_ref = refs[-2], refs[-1]
        for h in range(N_HEADS):
            blk = slice(h * LANES, (h + 1) * LANES)
            xs = [r_[:, blk] for r_ in refs[:-2]]
            lses = [_lanes_hi_to_all(x) for x in xs]
            m = functools.reduce(jnp.maximum, lses)
            ws = [jnp.exp(l - m) for l in lses]
            tot = functools.reduce(jnp.add, ws)
            lane = lax.broadcasted_iota(jnp.int32, xs[0].shape, 1)
            o = functools.reduce(jnp.add, [w * x for w, x in zip(ws, xs)]) / tot
            o_ref[:, blk] = jnp.where(lane < HEAD_DIM, o, 0.0)
            lse_ref[:, blk] = m + jnp.log(tot)

    return _rows(body, "dil_combine", S, 256, list(branches), [], [(WIDTH_P, F32), (WIDTH_P, F32)])


def _dil_bwd(proj_a, do, lse, delta, bias, r):
    S = proj_a.shape[0]
    L = S // r
    seg = min(L, DIL_SEG)
    nseg, nblk, nblk_seg = L // seg, L // DIL_Q, seg // DIL_Q
    pv = proj_a.reshape(L, r * 3 * WIDTH_P)
    view = lambda a: a.reshape(L, r * WIDTH_P)
    _TN = (((0,), (0,)), ((), ()))

    def body(q_ref, k_ref, v_ref, do_ref, lse_ref, dl_ref, b_ref, dq_ref, dk_ref, dv_ref,
             kp_ref, vp_ref, dkp_ref, dvp_ref):
        sg = pl.program_id(2)

        @pl.when(sg == 0)
        def _():
            _fill_kv(kp_ref, vp_ref, k_ref, v_ref, L, False)
            dkp_ref[...] = jnp.zeros_like(dkp_ref)
            dvp_ref[...] = jnp.zeros_like(dvp_ref)

        def block(j, carry):
            i = sg * nblk_seg + j
            rows = pl.ds(pl.multiple_of(j * DIL_Q, DIL_Q), DIL_Q)
            slab = pl.ds(pl.multiple_of(i * DIL_Q, DIL_Q), DIL_SLAB)
            variant = jnp.where(i == 0, 1, 0) + jnp.where(i == nblk - 1, 2, 0)
            qv = q_ref[rows, :] * DIL_SCALE
            dov = do_ref[rows, :]
            ks, vs = kp_ref[slab, :], vp_ref[slab, :]
            two = lambda a: jnp.concatenate([a, a], axis=1)
            s = lax.dot_general(qv, ks, _NT, preferred_element_type=F32) + b_ref[0, variant]
            p = jnp.exp(s - two(lse_ref[rows, :]))
            dp = lax.dot_general(dov, vs, _NT, preferred_element_type=F32)
            ds = (p * (dp - two(dl_ref[rows, :]))).astype(BF16)
            dq_ref[rows, :] = jnp.dot(ds, ks, preferred_element_type=F32) * DIL_SCALE
            dkp_ref[slab, :] += lax.dot_general(ds, qv, _TN, preferred_element_type=F32)
            dvp_ref[slab, :] += lax.dot_general(p.astype(BF16), dov, _TN, preferred_element_type=F32)
            return carry

        lax.fori_loop(0, nblk_seg, block, 0, unroll=min(2, nblk_seg))

        @pl.when(sg == nseg - 1)
        def _():
            dk_ref[...] = dkp_ref[DIL_HALF:DIL_HALF + L, :]
            dv_ref[...] = dvp_ref[DIL_HALF:DIL_HALF + L, :]

    col = lambda part: pl.BlockSpec((L, LANES), lambda c, h, s: (0, c * 3 * N_HEADS + part * N_HEADS + h))
    segspec = pl.BlockSpec((seg, LANES), lambda c, h, s: (s, c * N_HEADS + h))
    fullspec = pl.BlockSpec((L, LANES), lambda c, h, s: (0, c * N_HEADS + h))
    out = jax.ShapeDtypeStruct((L, r * WIDTH_P), F32)
    dq, dk, dv = pl.pallas_call(
        body, name="dil_bwd_r%d" % r, grid=(r, N_HEADS, nseg),
        in_specs=[pl.BlockSpec((seg, LANES), lambda c, h, s: (s, c * 3 * N_HEADS + h)), col(1), col(2),
                  segspec, segspec, segspec, pl.BlockSpec((1, 4, DIL_Q, DIL_SLAB), lambda c, h, s: (h, 0, 0, 0))],
        out_specs=[segspec, fullspec, fullspec], out_shape=[out, out, out],
        scratch_shapes=[pltpu.VMEM((L + 2 * DIL_HALF, LANES), BF16), pltpu.VMEM((L + 2 * DIL_HALF, LANES), BF16),
                        pltpu.VMEM((L + 2 * DIL_HALF, LANES), F32), pltpu.VMEM((L + 2 * DIL_HALF, LANES), F32)],
        compiler_params=_cparams(("arbitrary", "arbitrary", "arbitrary")),
    )(pv, pv, pv, view(do), view(lse), view(delta), bias)
    return [a.reshape(S, WIDTH_P) for a in (dq, dk, dv)]


def _dil_sum(grads):
    S = grads[0][0].shape[0]
    nb = len(grads)

    def body(step, *refs):
        out_ref = refs[-1]
        for part in range(3):
            tot = refs[part][...]
            for b in range(1, nb):
                tot = tot + refs[3 * b + part][...]
            out_ref[:, part * WIDTH_P:(part + 1) * WIDTH_P] = tot.astype(BF16)

    return _rows(body, "dil_sum", S, 256, [a for g in grads for a in g], [], [(3 * WIDTH_P, BF16)])[0]


_NT = (((1,), (1,)), ((), ()))


def _mla_fwd(q, k, vt):
    S = q.shape[0]
    tq, tk = MLA_TQ, MLA_TK
    nq, nk = S // tq, S // tk

    def body(q_ref, k_ref, vt_ref, o_ref, lse_ref, acc_ref):
        qv = q_ref[...]
        acc_ref[...] = jnp.zeros_like(acc_ref)

        def chunk(c, m):
            kc = k_ref[pl.ds(pl.multiple_of(c * tk, tk), tk), :]
            st = lax.dot_general(kc, qv, _NT, preferred_element_type=F32)
            m_new = jnp.maximum(m, jnp.max(st, axis=0, keepdims=True))
            pt = jnp.exp(st - m_new).astype(BF16)
            acc_ref[...] = jnp.exp(m - m_new) * acc_ref[...] + jnp.dot(vt_ref[0, c], pt,
                                                                        preferred_element_type=F32)
            return m_new

        m = lax.fori_loop(0, nk, chunk, jnp.full((1, tq), M_INIT, F32), unroll=2)
        acc = acc_ref[...]
        l = acc[HEAD_DIM:HEAD_DIM + 1, :]
        row = lax.broadcasted_iota(jnp.int32, acc.shape, 0)
        o_ref[...] = jnp.where(row < HEAD_DIM, acc / l, 0.0).T
        lse_ref[0] = m + jnp.log(l)

    return pl.pallas_call(
        body, name="mla_fwd", grid=(N_HEADS, nq),
        in_specs=[pl.BlockSpec((tq, LANES), lambda h, i: (i, h)),
                  pl.BlockSpec((S, LANES), lambda h, i: (0, h)),
                  pl.BlockSpec((1, nk, LANES, tk), lambda h, i: (h, 0, 0, 0))],
        out_specs=[pl.BlockSpec((tq, LANES), lambda h, i: (i, h)),
                   pl.BlockSpec((1, 1, tq), lambda h, i: (h, 0, i))],
        out_shape=[jax.ShapeDtypeStruct((S, WIDTH_P), F32), jax.ShapeDtypeStruct((N_HEADS, 1, S), F32)],
        scratch_shapes=[pltpu.VMEM((LANES, tq), F32)],
        compiler_params=_cparams(("parallel", "parallel")),
    )(q, k, vt)


def _mla_bwd(q, k, v, kt, do, o, lse):
    S = q.shape[0]
    tq, tk = MLA_TQ, MLA_TK
    nq, nk = S // tq, S // tk

    def body(q_ref, do_ref, o_ref, lse_ref, k_ref, v_ref, kt_ref, dq_ref, dk_ref, dv_ref, dqt_ref):
        @pl.when(pl.program_id(1) == 0)
        def _():
            dk_ref[...] = jnp.zeros_like(dk_ref)
            dv_ref[...] = jnp.zeros_like(dv_ref)

        qv, dov = q_ref[...], do_ref[...]
        delta = jnp.sum((dov.astype(F32) * o_ref[...]).T, axis=0, keepdims=True)
        lse = lse_ref[0]
        dqt_ref[...] = jnp.zeros_like(dqt_ref)

        def chunk(c, carry):
            rows = pl.ds(pl.multiple_of(c * tk, tk), tk)
            kc, vc = k_ref[rows, :], v_ref[rows, :]
            pt = jnp.exp(lax.dot_general(kc, qv, _NT, preferred_element_type=F32) - lse)
            dv_ref[rows, :] += jnp.dot(pt.astype(BF16), dov, preferred_element_type=F32)
            dpt = lax.dot_general(vc, dov, _NT, preferred_element_type=F32)
            dst = (pt * (dpt - delta)).astype(BF16)
            dk_ref[rows, :] += jnp.dot(dst, qv, preferred_element_type=F32)
            dqt_ref[...] += jnp.dot(kt_ref[0, c], dst, preferred_element_type=F32)
            return carry

        lax.fori_loop(0, nk, chunk, 0, unroll=2)
        dq_ref[...] = (dqt_ref[...] * MLA_SCALE).T

    qspec = pl.BlockSpec((tq, LANES), lambda h, i: (i, h))
    kspec = pl.BlockSpec((S, LANES), lambda h, i: (0, h))
    out = jax.ShapeDtypeStruct((S, WIDTH_P), F32)
    return pl.pallas_call(
        body, name="mla_bwd", grid=(N_HEADS, nq),
        in_specs=[qspec, qspec, qspec, pl.BlockSpec((1, 1, tq), lambda h, i: (h, 0, i)), kspec, kspec,
                  pl.BlockSpec((1, nk, LANES, tk), lambda h, i: (h, 0, 0, 0))],
        out_specs=[qspec, kspec, kspec], out_shape=[out, out, out],
        scratch_shapes=[pltpu.VMEM((LANES, tq), F32)],
        compiler_params=_cparams(("arbitrary", "arbitrary")),
    )(q, do, o, lse, k, v, kt)


def _mla_bwd_prep(dq, dk, dv, tabs):
    S = dq.shape[0]

    def body(step, dq_ref, dk_ref, dv_ref, c_ref, sa_ref, sb_ref, dqp_ref, dkv_ref, dkr_ref):
        c, sa, sb = c_ref[...], sa_ref[...], sb_ref[...]
        dksum = jnp.zeros((dq_ref.shape[0], LANES), F32)
        for h in range(N_HEADS):
            blk = slice(h * LANES, (h + 1) * LANES)
            dqp_ref[:, blk] = _rope_transpose(dq_ref[:, blk], c, sa, sb).astype(BF16)
            dksum = dksum + dk_ref[:, blk]
        dkv_ref[:, 0:WIDTH_P] = dk_ref[...].astype(BF16)
        dkv_ref[:, WIDTH_P:2 * WIDTH_P] = dv_ref[...].astype(BF16)
        lane = lax.broadcasted_iota(jnp.int32, dksum.shape, 1)
        live = (lane >= HEAD_DIM) & (lane < HEAD_DIM + QK_ROPE)
        dkr_ref[...] = jnp.where(live, _rope_transpose(dksum, c, sa, sb), 0.0)

    return _rows(body, "mla_bwd_prep", S, 256, [dq, dk, dv, *tabs], [],
                 [(WIDTH_P, BF16), (2 * WIDTH_P, BF16), (LANES, F32)])


def _mla_norm_bwd(proj_b, dcq_n, dckv_n, dkr, g_q, g_kv):
    S = proj_b.shape[0]

    def body(step, p_ref, dcq_ref, dckv_ref, dkr_ref, gq_ref, gkv_ref, dp_ref, dgq_ref, dgkv_ref):
        dcq, dgq = _rms_bwd_math(p_ref[:, 0:Q_LORA], gq_ref[...], dcq_ref[...], Q_LORA)
        dckv, dgkv = _rms_bwd_math(p_ref[:, Q_LORA:Q_LORA + KV_LORA], gkv_ref[...], dckv_ref[...], KV_LORA)
        dp_ref[:, 0:Q_LORA] = dcq.astype(BF16)
        dp_ref[:, Q_LORA:Q_LORA + KV_LORA] = dckv.astype(BF16)
        dp_ref[:, Q_LORA + KV_LORA:TAIL_P] = dkr_ref[...].astype(BF16)
        _acc_add(step, dgq_ref, jnp.sum(dgq, axis=0, keepdims=True))
        _acc_add(step, dgkv_ref, jnp.sum(dgkv, axis=0, keepdims=True))

    return _rows(body, "mla_norm_bwd", S, 512, [proj_b, dcq_n, dckv_n, dkr], [g_q, g_kv], [(TAIL_P, BF16)],
                 [((1, Q_LORA), F32), ((1, KV_LORA), F32)])


def _pad_cols(w, d):
    lead = w.shape[:-1]
    w = w.reshape(lead + (N_HEADS, d))
    w = jnp.pad(w, [(0, 0)] * len(lead) + [(0, 0), (0, LANES - d)])
    return w.reshape(lead + (N_HEADS * LANES,))


def _unpad_cols(w, d):
    lead = w.shape[:-1]
    return w.reshape(lead + (N_HEADS, LANES))[..., :d].reshape(lead + (N_HEADS * d,))


def _pad_weights(w):
    w_in = w['w_in']
    zeros = lambda n: jnp.zeros((D_MODEL, n), w_in.dtype)
    p = {}
    p['w_in_a'] = jnp.concatenate([_pad_cols(w_in[:, i * WIDTH:(i + 1) * WIDTH], HEAD_DIM) for i in range(3)], axis=1)
    p['w_in_b'] = jnp.concatenate([w_in[:, 3 * WIDTH:3 * WIDTH + Q_LORA + KV_LORA], zeros(HEAD_DIM),
                                   w_in[:, D_IN - QK_ROPE:], zeros(LANES - HEAD_DIM - QK_ROPE)], axis=1)
    p['w_uq'] = _pad_cols(w['w_uq'], HEAD_DIM + QK_ROPE)
    kv = w['w_ukv'].reshape(KV_LORA, N_HEADS, 2 * HEAD_DIM)
    p['w_ukv'] = jnp.concatenate([_pad_cols(kv[:, :, :HEAD_DIM].reshape(KV_LORA, WIDTH), HEAD_DIM),
                                  _pad_cols(kv[:, :, HEAD_DIM:].reshape(KV_LORA, WIDTH), HEAD_DIM)], axis=1)
    p['w_o'] = jnp.concatenate(
        [_pad_cols(w['w_o'][i * WIDTH:(i + 1) * WIDTH].T, HEAD_DIM).T for i in range(2)], axis=0)
    p['g_a'] = _pad_cols(w['out_norm_a'], HEAD_DIM)
    p['g_b'] = _pad_cols(w['out_norm_b'], HEAD_DIM)
    return p


def _unpad_grads(d):
    g = {}
    dwa = d['w_in_a']
    tail = d['w_in_b']
    g['w_in'] = jnp.concatenate(
        [_unpad_cols(dwa[:, i * WIDTH_P:(i + 1) * WIDTH_P], HEAD_DIM) for i in range(3)]
        + [tail[:, :Q_LORA + KV_LORA], tail[:, Q_LORA + KV_LORA + HEAD_DIM:Q_LORA + KV_LORA + HEAD_DIM + QK_ROPE]],
        axis=1)
    g['w_uq'] = _unpad_cols(d['w_uq'], HEAD_DIM + QK_ROPE)
    dk = _unpad_cols(d['w_ukv'][:, :WIDTH_P], HEAD_DIM).reshape(KV_LORA, N_HEADS, HEAD_DIM)
    dv = _unpad_cols(d['w_ukv'][:, WIDTH_P:], HEAD_DIM).reshape(KV_LORA, N_HEADS, HEAD_DIM)
    g['w_ukv'] = jnp.concatenate([dk, dv], axis=2).reshape(KV_LORA, 2 * WIDTH)
    g['w_o'] = jnp.concatenate(
        [_unpad_cols(d['w_o'][i * WIDTH_P:(i + 1) * WIDTH_P].T, HEAD_DIM).T for i in range(2)], axis=0)
    g['out_norm_a'] = _unpad_cols(d['g_a'], HEAD_DIM)
    g['out_norm_b'] = _unpad_cols(d['g_b'], HEAD_DIM)
    return g


def _local_step(x, target, w):
    S = x.shape[0]
    p = _pad_weights(w)
    tabs = _rope_tables(S)
    biases = [_dil_bias(r) for r in DILATIONS]

    h1 = _rms_fwd(x, w['norm_mix_pre'], "rms_mix_pre")
    proj_a = _mm(h1, p['w_in_a'], 'nn', BF16, "mm_in_a")
    proj_b = _mm(h1, p['w_in_b'], 'nn', F32, "mm_in_b")
    oa, lse_a = _dil_combine([_dil_fwd(proj_a, b, r) for b, r in zip(biases, DILATIONS)])
    cq_n, ckv_n, kr = _mla_prep(proj_b, w['q_lat_norm'], w['kv_lat_norm'], tabs)
    q_lin = _mm(cq_n, p['w_uq'], 'nn', F32, "mm_uq")
    kv_lin = _mm(ckv_n, p['w_ukv'], 'nn', F32, "mm_ukv")
    qb, kb, vb, kt, vt = _mla_qkv(q_lin, kv_lin, kr, tabs)
    ob, lse_b = _mla_fwd(qb, kb, vt)
    cat = _outnorm_fwd(oa, ob, p['g_a'], p['g_b'])
    y = _mm(cat, p['w_o'], 'nn', F32, "mm_o")
    x1, h2 = _post_mix(x, y, w['norm_mix_post'], w['norm_ffn_pre'])
    u0 = _mm(h2, w['w_up'], 'nn', F32, "mm_up")
    ug, uv, a = _conv_fwd(u0, w['conv_w'], w['conv_b'])
    y2 = _mm(a, w['w_down'], 'nn', F32, "mm_down")
    dx2, dy2, dg_ffn_post, _, loss = _final(x1, y2, w['norm_ffn_post'], target)

    g = {'norm_ffn_post': dg_ffn_post}
    da = _mm(dy2, w['w_down'], 'nt', F32, "mm_down_dx")
    g['w_down'] = _mm(a, dy2, 'tn', F32, "mm_down_dw")
    du0, g['conv_w'], g['conv_b'] = _conv_bwd(u0, ug, uv, da, w['conv_w'])
    dh2 = _mm(du0, w['w_up'], 'nt', F32, "mm_up_dx")
    g['w_up'] = _mm(h2, du0, 'tn', F32, "mm_up_dw")
    dx1, g['norm_ffn_pre'] = _rms_bwd(x1, w['norm_ffn_pre'], [dh2], dx2, F32, "rms_ffn_pre_bwd")
    dy, g['norm_mix_post'] = _rms_bwd(y, w['norm_mix_post'], [dx1], None, BF16, "rms_mix_post_bwd")
    dcat = _mm(dy, p['w_o'], 'nt', F32, "mm_o_dx")
    dpad = {'w_o': _mm(cat, dy, 'tn', F32, "mm_o_dw")}
    do_a, do_b, dl_a, dl_b, dpad['g_a'], dpad['g_b'] = _outnorm_bwd(oa, ob, p['g_a'], p['g_b'], dcat)

    dq_b, dk_b, dv_b = _mla_bwd(qb, kb, vb, kt, do_b, ob, lse_b)
    dq_pre, dkv, dkr = _mla_bwd_prep(dq_b, dk_b, dv_b, tabs)
    dcq_n = _mm(dq_pre, p['w_uq'], 'nt', F32, "mm_uq_dx")
    dpad['w_uq'] = _mm(cq_n, dq_pre, 'tn', F32, "mm_uq_dw")
    dckv_n = _mm(dkv, p['w_ukv'], 'nt', F32, "mm_ukv_dx")
    dpad['w_ukv'] = _mm(ckv_n, dkv, 'tn', F32, "mm_ukv_dw")
    dproj_b, g['q_lat_norm'], g['kv_lat_norm'] = _mla_norm_bwd(proj_b, dcq_n, dckv_n, dkr,
                                                               w['q_lat_norm'], w['kv_lat_norm'])

    dproj_a = _dil_sum([_dil_bwd(proj_a, do_a, lse_a, dl_a, b, r) for b, r in zip(biases, DILATIONS)])
    dh1 = _mm(dproj_b, p['w_in_b'], 'nt', F32, "mm_in_b_dx")
    dh1 = _mm(dproj_a, p['w_in_a'], 'nt', F32, "mm_in_a_dx", add=dh1)
    dpad['w_in_a'] = _mm(h1, dproj_a, 'tn', F32, "mm_in_a_dw")
    dpad['w_in_b'] = _mm(h1, dproj_b, 'tn', F32, "mm_in_b_dw")
    grad_x, g['norm_mix_pre'] = _rms_bwd(x, w['norm_mix_pre'], [dh1], dx1, F32, "rms_mix_pre_bwd")
    g.update(_unpad_grads(dpad))
    return loss, grad_x, g


MESH = pl.DeviceIdType.MESH
ANY = pl.BlockSpec(memory_space=pl.ANY)


def _place():
    x, y, c = lax.axis_index("x"), lax.axis_index("y"), lax.axis_index("c")
    chips = [(1 - x, y), (x, 1 - y), (1 - x, 1 - y)]
    return x, y, c, chips


def _all_gather(bufs):
    n = len(bufs)

    def body(*refs):
        in_refs, out_refs = refs[:n], refs[n:2 * n]
        send_sems, recv_sems, local_sems = refs[2 * n:]
        x, y, c, chips = _place()
        me = 2 * x + y
        local = [pltpu.make_async_copy(in_refs[b], out_refs[b].at[me], local_sems.at[b]) for b in range(n)]
        for cp in local:
            cp.start()
        sends = []
        for j, (px, py) in enumerate(chips):
            for b in range(n):
                sends.append(pltpu.make_async_remote_copy(
                    src_ref=in_refs[b], dst_ref=out_refs[b].at[me], send_sem=send_sems.at[j * n + b],
                    recv_sem=recv_sems.at[j * n + b], device_id=(px, py, c), device_id_type=MESH))
        for cp in sends:
            cp.start()
        for j, (px, py) in enumerate(chips):
            for b in range(n):
                pltpu.make_async_remote_copy(
                    src_ref=in_refs[b], dst_ref=out_refs[b].at[2 * px + py], send_sem=send_sems.at[j * n + b],
                    recv_sem=recv_sems.at[j * n + b], device_id=(px, py, c), device_id_type=MESH).wait_recv()
        for cp in sends:
            cp.wait_send()
        for cp in local:
            cp.wait()

    return pl.pallas_call(
        body, name="gather_weights", in_specs=[ANY] * n, out_specs=[ANY] * n,
        out_shape=[jax.ShapeDtypeStruct((N_CHIPS,) + b.shape, b.dtype) for b in bufs],
        scratch_shapes=[pltpu.SemaphoreType.DMA((3 * n,)), pltpu.SemaphoreType.DMA((3 * n,)),
                        pltpu.SemaphoreType.DMA((n,))],
    )(*bufs)


def _scatter_grads(gflat):
    _, R, L = gflat.shape

    def body(g_ref, o_ref, send_sems, recv_sems, local_sem):
        x, y, c, chips = _place()
        me = 2 * x + y
        local = pltpu.make_async_copy(g_ref.at[me], o_ref.at[me], local_sem)
        local.start()
        sends = [pltpu.make_async_remote_copy(
            src_ref=g_ref.at[2 * px + py], dst_ref=o_ref.at[me], send_sem=send_sems.at[j], recv_sem=recv_sems.at[j],
            device_id=(px, py, c), device_id_type=MESH) for j, (px, py) in enumerate(chips)]
        for cp in sends:
            cp.start()
        for j, (px, py) in enumerate(chips):
            pltpu.make_async_remote_copy(
                src_ref=g_ref.at[me], dst_ref=o_ref.at[2 * px + py], send_sem=send_sems.at[j],
                recv_sem=recv_sems.at[j], device_id=(px, py, c), device_id_type=MESH).wait_recv()
        for cp in sends:
            cp.wait_send()
        local.wait()

    return pl.pallas_call(
        body, name="scatter_grads", in_specs=[ANY], out_specs=ANY,
        out_shape=jax.ShapeDtypeStruct(gflat.shape, gflat.dtype),
        scratch_shapes=[pltpu.SemaphoreType.DMA((3,)), pltpu.SemaphoreType.DMA((3,)), pltpu.SemaphoreType.DMA],
    )(gflat)


def _sum_slots(recv):
    _, R, L = recv.shape
    tr = _tile(R, 2048)

    def body(r_ref, o_ref):
        o_ref[...] = ((r_ref[0] + r_ref[1]) + r_ref[2]) + r_ref[3]

    return pl.pallas_call(
        body, name="sum_slots", grid=(R // tr,),
        in_specs=[pl.BlockSpec((N_CHIPS, tr, L), lambda i: (0, i, 0))],
        out_specs=pl.BlockSpec((tr, L), lambda i: (i, 0)),
        out_shape=jax.ShapeDtypeStruct((R, L), recv.dtype),
        compiler_params=_cparams(("parallel",)),
    )(recv)


def _swap_sibling(part):
    def body(p_ref, o_ref, send_sem, recv_sem):
        x, y, c, _ = _place()
        cp = pltpu.make_async_remote_copy(src_ref=p_ref, dst_ref=o_ref, send_sem=send_sem, recv_sem=recv_sem,
                                          device_id=(x, y, 1 - c), device_id_type=MESH)
        cp.start()
        cp.wait()

    return pl.pallas_call(
        body, name="swap_sibling", in_specs=[ANY], out_specs=ANY,
        out_shape=jax.ShapeDtypeStruct(part.shape, part.dtype),
        scratch_shapes=[pltpu.SemaphoreType.DMA, pltpu.SemaphoreType.DMA],
    )(part)


def _adamw(g0, g1, w, m, v):
    R, L = w.shape
    tr = _tile(R, 2048)
    bc1 = 1.0 - ADAM_B1 ** ADAM_STEP
    bc2 = 1.0 - ADAM_B2 ** ADAM_STEP

    def body(g0_ref, g1_ref, w_ref, m_ref, v_ref, g_ref, d_ref, nm_ref, nv_ref):
        g = g0_ref[...] + g1_ref[...]
        g_ref[...] = g
        nm = ADAM_B1 * m_ref[...] + (1.0 - ADAM_B1) * g
        nv = ADAM_B2 * v_ref[...] + (1.0 - ADAM_B2) * (g * g)
        nm_ref[...] = nm
        nv_ref[...] = nv
        d_ref[...] = -ADAM_LR * ((nm / bc1) / (jnp.sqrt(nv / bc2) + ADAM_EPS) + ADAM_WD * w_ref[...])

    spec = pl.BlockSpec((tr, L), lambda i: (i, 0))
    out = jax.ShapeDtypeStruct((R, L), F32)
    return pl.pallas_call(
        body, name="adamw", grid=(R // tr,), in_specs=[spec] * 5, out_specs=[spec] * 4, out_shape=[out] * 4,
        compiler_params=_cparams(("parallel",)),
    )(g0, g1, w, m, v)


FLAT_ALIGN = 16 * LANES


def _flatten(parts, dtype):
    flat = jnp.concatenate([p.reshape(-1).astype(dtype) for p in parts])
    n = flat.shape[0]
    npad = -(-n // (FLAT_ALIGN * 16)) * (FLAT_ALIGN * 16)
    return jnp.pad(flat, (0, npad - n)).reshape(npad // LANES, LANES)


def _unflatten(flat, shapes):
    flat = flat.reshape(-1)
    out, pos = [], 0
    for s in shapes:
        n = int(np.prod(s))
        out.append(flat[pos:pos + n].reshape(s))
        pos += n
    return out


def _shard(a, axis, i):
    n = a.shape[axis] // N_CHIPS
    return lax.slice_in_dim(a, i * n, (i + 1) * n, axis=axis)


def kernel(x, norm_mix_pre, w_in, q_lat_norm, w_uq, kv_lat_norm, w_ukv, out_norm_a, out_norm_b, w_o, norm_mix_post, norm_ffn_pre, w_up, conv_w, conv_b, w_down, norm_ffn_post, loss_target, m_norm_mix_pre, m_w_in, m_q_lat_norm, m_w_uq, m_kv_lat_norm, m_w_ukv, m_out_norm_a, m_out_norm_b, m_w_o, m_norm_mix_post, m_norm_ffn_pre, m_w_up, m_conv_w, m_conv_b, m_w_down, m_norm_ffn_post, v_norm_mix_pre, v_w_in, v_q_lat_norm, v_w_uq, v_kv_lat_norm, v_w_ukv, v_out_norm_a, v_out_norm_b, v_w_o, v_norm_mix_post, v_norm_ffn_pre, v_w_up, v_conv_w, v_conv_b, v_w_down, v_norm_ffn_post):
    args = dict(locals())
    strip = lambda a: a[0] if a.ndim == 3 else a
    wl = {n: strip(args[n]) for n in WEIGHTS}
    ml = {n: strip(args['m_' + n]) for n in WEIGHTS}
    vl = {n: strip(args['v_' + n]) for n in WEIGHTS}

    big_shapes = [wl[n].shape for n in BIG]
    gathered_big, gathered_cw = _all_gather([_flatten([wl[n] for n in BIG], BF16), _flatten([wl['conv_w']], F32)])
    full = {n: wl[n] for n in SMALL}
    per_chip = [_unflatten(gathered_big[i], big_shapes) for i in range(N_CHIPS)]
    for k, n in enumerate(BIG):
        full[n] = jnp.concatenate([per_chip[i][k] for i in range(N_CHIPS)], axis=SHARD_AXIS[n])
    full['conv_w'] = jnp.concatenate(
        [_unflatten(gathered_cw[i], [wl['conv_w'].shape])[0] for i in range(N_CHIPS)], axis=1)

    loss_b, grad_x, g = _local_step(x[0], loss_target[0], full)

    sharded = [n for n in WEIGHTS if SHARD_AXIS[n] is not None]
    slots = []
    for i in range(N_CHIPS):
        slots.append(_flatten([_shard(g[n], SHARD_AXIS[n], i) for n in sharded] + [g[n] for n in SMALL], F32))
    recv = _scatter_grads(jnp.stack(slots))
    part = _sum_slots(recv)
    other = _swap_sibling(part)

    order = sharded + SMALL
    shapes = [wl[n].shape for n in order]
    g_f, d_f, nm_f, nv_f = _adamw(part, other, _flatten([wl[n] for n in order], F32),
                                  _flatten([ml[n] for n in order], F32), _flatten([vl[n] for n in order], F32))
    outs = {}
    for tag, flat in (('grad', g_f), ('delta', d_f), ('new_m', nm_f), ('new_v', nv_f)):
        for n, a in zip(order, _unflatten(flat, shapes)):
            outs[tag + '_' + n] = a.reshape(args[n].shape)

    loss = lax.psum(loss_b[0, 0], ("x", "y", "c"))
    return (loss, grad_x[None], *[outs['grad_' + n] for n in WEIGHTS], *[outs['delta_' + n] for n in WEIGHTS],
            *[outs['new_m_' + n] for n in WEIGHTS], *[outs['new_v_' + n] for n in WEIGHTS])
```

```python
import functools
import math

import jax
import jax.numpy as jnp
import numpy as np
from jax import lax
from jax.experimental import pallas as pl
from jax.experimental.pallas import tpu as pltpu

F32 = jnp.float32
BF16 = jnp.bfloat16

LANES = 128
D_MODEL = 1024
N_HEADS = 8
HEAD_DIM = 64
QK_ROPE = 32
Q_LORA = 384
KV_LORA = 256
D_FF = 2816
WIDTH = N_HEADS * HEAD_DIM
WIDTH_P = N_HEADS * LANES
IN_SIZES = (WIDTH, WIDTH, WIDTH, Q_LORA, KV_LORA, QK_ROPE)
D_IN = sum(IN_SIZES)
TAIL_P = Q_LORA + KV_LORA + LANES
EPS = 1e-6
ROPE_BASE = 10000.0
MASKED = -2e30
M_INIT = -1e30
MLA_TQ = 2048
MLA_TK = 256
MLA_SCALE = (HEAD_DIM + QK_ROPE) ** -0.5
DIL_SCALE = HEAD_DIM ** -0.5

ADAM_LR = 0.001
ADAM_B1 = 0.9
ADAM_B2 = 0.999
ADAM_EPS = 1e-08
ADAM_WD = 0.01
ADAM_STEP = 10

VMEM_LIMIT = 56 * 1024 * 1024

N_CHIPS = 4

WEIGHTS = ['norm_mix_pre', 'w_in', 'q_lat_norm', 'w_uq', 'kv_lat_norm', 'w_ukv', 'out_norm_a', 'out_norm_b',
           'w_o', 'norm_mix_post', 'norm_ffn_pre', 'w_up', 'conv_w', 'conv_b', 'w_down', 'norm_ffn_post']
SHARD_AXIS = {'norm_mix_pre': None, 'w_in': 1, 'q_lat_norm': None, 'w_uq': 1, 'kv_lat_norm': None, 'w_ukv': 1,
              'out_norm_a': None, 'out_norm_b': None, 'w_o': 0, 'norm_mix_post': None, 'norm_ffn_pre': None,
              'w_up': 1, 'conv_w': 1, 'conv_b': None, 'w_down': 0, 'norm_ffn_post': None}
BIG = ['w_in', 'w_uq', 'w_ukv', 'w_o', 'w_up', 'w_down']
SMALL = [n for n in WEIGHTS if SHARD_AXIS[n] is None]


def _tile(dim, target):
    best = None
    t = LANES
    while t <= min(dim, target):
        if dim % t == 0:
            best = t
        t += LANES
    return best if best is not None else dim


def _cparams(sem=None):
    return pltpu.CompilerParams(dimension_semantics=sem, vmem_limit_bytes=VMEM_LIMIT)


def _mm(a, b, mode, out_dtype, name, add=None, tm=1024, tn=1024, tk=1024):
    if mode == 'nn':
        (M, K), (K2, N) = a.shape, b.shape
        dims = (((1,), (0,)), ((), ()))
    elif mode == 'nt':
        (M, K), (N, K2) = a.shape, b.shape
        dims = (((1,), (1,)), ((), ()))
    else:
        (K, M), (K2, N) = a.shape, b.shape
        dims = (((0,), (0,)), ((), ()))
    assert K == K2, (a.shape, b.shape, mode)
    tm, tn, tk = _tile(M, tm), _tile(N, tn), _tile(K, tk)
    nk = K // tk
    if mode == 'nn':
        a_spec = pl.BlockSpec((tm, tk), lambda i, j, k: (i, k))
        b_spec = pl.BlockSpec((tk, tn), lambda i, j, k: (k, j))
    elif mode == 'nt':
        a_spec = pl.BlockSpec((tm, tk), lambda i, j, k: (i, k))
        b_spec = pl.BlockSpec((tn, tk), lambda i, j, k: (j, k))
    else:
        a_spec = pl.BlockSpec((tk, tm), lambda i, j, k: (k, i))
        b_spec = pl.BlockSpec((tk, tn), lambda i, j, k: (k, j))
    o_spec = pl.BlockSpec((tm, tn), lambda i, j, k: (i, j))
    has_add = add is not None

    def body(*refs):
        if has_add:
            a_ref, b_ref, add_ref, o_ref, acc_ref = refs
        else:
            a_ref, b_ref, o_ref, acc_ref = refs
        k = pl.program_id(2)

        @pl.when(k == 0)
        def _():
            acc_ref[...] = jnp.zeros_like(acc_ref)

        acc_ref[...] += lax.dot_general(a_ref[...].astype(BF16), b_ref[...].astype(BF16), dims,
                                        preferred_element_type=F32)

        @pl.when(k == nk - 1)
        def _():
            r = acc_ref[...]
            if has_add:
                r = r + add_ref[...]
            o_ref[...] = r.astype(o_ref.dtype)

    ins = [a, b] + ([add] if has_add else [])
    in_specs = [a_spec, b_spec] + ([o_spec] if has_add else [])
    return pl.pallas_call(
        body, name=name, grid=(M // tm, N // tn, nk), in_specs=in_specs, out_specs=o_spec,
        out_shape=jax.ShapeDtypeStruct((M, N), out_dtype),
        scratch_shapes=[pltpu.VMEM((tm, tn), F32)],
        compiler_params=_cparams(("parallel", "parallel", "arbitrary")),
    )(*ins)


def _rows(body, name, S, ts, row_ins, full_ins, row_outs, acc_outs=(), chunk_outs=()):
    in_specs = [pl.BlockSpec((ts, a.shape[1]), lambda i: (i, 0)) for a in row_ins]
    in_specs += [pl.BlockSpec(a.shape, lambda i, nd=a.ndim: (0,) * nd) for a in full_ins]
    out_specs = [pl.BlockSpec((ts, w), lambda i: (i, 0)) for (w, _) in row_outs]
    out_specs += [pl.BlockSpec(shape, lambda i, nd=len(shape): (0,) * nd) for (shape, _) in acc_outs]
    out_specs += [pl.BlockSpec((lead, 1, LANES, ts), lambda i: (0, i, 0, 0)) for (lead, _) in chunk_outs]
    out_shape = [jax.ShapeDtypeStruct((S, w), dt) for (w, dt) in row_outs]
    out_shape += [jax.ShapeDtypeStruct(shape, dt) for (shape, dt) in acc_outs]
    out_shape += [jax.ShapeDtypeStruct((lead, S // ts, LANES, ts), dt) for (lead, dt) in chunk_outs]

    def kbody(*refs):
        body(pl.program_id(0), *refs)

    return pl.pallas_call(
        kbody, name=name, grid=(S // ts,), in_specs=in_specs, out_specs=out_specs, out_shape=out_shape,
        compiler_params=_cparams(("arbitrary",)),
    )(*row_ins, *full_ins)


def _acc_add(step, ref, val):
    @pl.when(step == 0)
    def _():
        ref[...] = val

    @pl.when(step != 0)
    def _():
        ref[...] += val


def _rms_fwd(x, g, name):
    S, W = x.shape

    def body(step, x_ref, g_ref, h_ref):
        xv = x_ref[...]
        r = lax.rsqrt(jnp.mean(xv * xv, axis=-1, keepdims=True) + EPS)
        h_ref[...] = (xv * r * g_ref[...]).astype(BF16)

    return _rows(body, name, S, 512, [x], [g], [(W, BF16)])[0]


def _rms_bwd_math(xv, g, dy, width):
    r = lax.rsqrt(jnp.sum(xv * xv, axis=-1, keepdims=True) * (1.0 / width) + EPS)
    xn = xv * r
    dyg = dy * g
    dx = r * (dyg - xn * (jnp.sum(dyg * xn, axis=-1, keepdims=True) * (1.0 / width)))
    return dx, dy * xn


def _rms_bwd(x, g, dys, resid, out_dtype, name):
    S, W = x.shape
    nd = len(dys)
    has_res = resid is not None

    def body(step, *refs):
        x_ref = refs[0]
        dy_refs = refs[1:1 + nd]
        pos = 1 + nd
        res_ref = refs[pos] if has_res else None
        pos += int(has_res)
        g_ref, dx_ref, dg_ref = refs[pos], refs[pos + 1], refs[pos + 2]
        dy = dy_refs[0][...].astype(F32)
        for r_ in dy_refs[1:]:
            dy = dy + r_[...].astype(F32)
        dx, dgr = _rms_bwd_math(x_ref[...], g_ref[...], dy, W)
        if has_res:
            dx = dx + res_ref[...]
        dx_ref[...] = dx.astype(dx_ref.dtype)
        _acc_add(step, dg_ref, jnp.sum(dgr, axis=0, keepdims=True))

    row_ins = [x] + list(dys) + ([resid] if has_res else [])
    dx, dg = _rows(body, name, S, 256, row_ins, [g], [(W, out_dtype)], [((1, W), F32)])
    return dx, dg


def _rope_apply(xv, c, sa, sb):
    return xv * c + pltpu.roll(xv, 16, 1) * sa + pltpu.roll(xv, LANES - 16, 1) * sb


def _rope_transpose(dy, c, sa, sb):
    return dy * c + pltpu.roll(dy * sa, LANES - 16, 1) + pltpu.roll(dy * sb, 16, 1)


def _rope_tables(S):
    pos = jnp.arange(S, dtype=F32)
    inv_freq = jnp.exp(-math.log(ROPE_BASE) * jnp.arange(0, QK_ROPE, 2, dtype=F32) / QK_ROPE)
    ang = pos[:, None] * inv_freq[None, :]
    cos, sin = jnp.cos(ang), jnp.sin(ang)
    ones, zeros = jnp.ones((S, HEAD_DIM), F32), jnp.zeros((S, HEAD_DIM), F32)
    z16, z32 = jnp.zeros((S, 16), F32), jnp.zeros((S, 32), F32)
    c = jnp.concatenate([ones, cos, cos, z32], axis=1)
    sa = jnp.concatenate([zeros, z16, sin, z32], axis=1)
    sb = jnp.concatenate([zeros, -sin, z16, z32], axis=1)
    return c, sa, sb


def _mla_prep(proj_b, g_q, g_kv, tabs):
    S = proj_b.shape[0]

    def body(step, p_ref, c_ref, sa_ref, sb_ref, gq_ref, gkv_ref, cq_ref, ckv_ref, kr_ref):
        cq = p_ref[:, 0:Q_LORA]
        ckv = p_ref[:, Q_LORA:Q_LORA + KV_LORA]
        kr = p_ref[:, Q_LORA + KV_LORA:TAIL_P]
        rq = lax.rsqrt(jnp.mean(cq * cq, axis=-1, keepdims=True) + EPS)
        cq_ref[...] = (cq * rq * gq_ref[...]).astype(BF16)
        rk = lax.rsqrt(jnp.mean(ckv * ckv, axis=-1, keepdims=True) + EPS)
        ckv_ref[...] = (ckv * rk * gkv_ref[...]).astype(BF16)
        kr_ref[...] = _rope_apply(kr, c_ref[...], sa_ref[...], sb_ref[...])

    return _rows(body, "mla_prep", S, 512, [proj_b, *tabs], [g_q, g_kv],
                 [(Q_LORA, BF16), (KV_LORA, BF16), (LANES, F32)])


def _mla_qkv(q, kv, kr, tabs):
    S = q.shape[0]

    def body(step, q_ref, kv_ref, kr_ref, c_ref, sa_ref, sb_ref, qb_ref, kb_ref, vb_ref, kt_ref, vt_ref):
        c, sa, sb = c_ref[...], sa_ref[...], sb_ref[...]
        krv = kr_ref[...]
        row = lax.broadcasted_iota(jnp.int32, (LANES, MLA_TK), 0)
        for h in range(N_HEADS):
            blk = slice(h * LANES, (h + 1) * LANES)
            qb_ref[:, blk] = (_rope_apply(q_ref[:, blk], c, sa, sb) * MLA_SCALE).astype(BF16)
            kh = kv_ref[:, blk] + krv
            kb_ref[:, blk] = kh.astype(BF16)
            kt_ref[h, 0] = kh.T.astype(BF16)
            vh = kv_ref[:, WIDTH_P + h * LANES:WIDTH_P + (h + 1) * LANES]
            vt_ref[h, 0] = jnp.where(row == HEAD_DIM, 1.0, vh.T).astype(BF16)
        vb_ref[...] = kv_ref[:, WIDTH_P:2 * WIDTH_P].astype(BF16)

    return _rows(body, "mla_qkv", S, MLA_TK, [q, kv, kr, *tabs], [],
                 [(WIDTH_P, BF16), (WIDTH_P, BF16), (WIDTH_P, BF16)],
                 chunk_outs=[(N_HEADS, BF16), (N_HEADS, BF16)])


def _outnorm_fwd(oa, ob, ga, gb):
    S = oa.shape[0]

    def body(step, oa_ref, ob_ref, ga_ref, gb_ref, cat_ref):
        live = lax.broadcasted_iota(jnp.int32, oa_ref.shape, 1) % LANES < HEAD_DIM
        for o_ref, g_ref, off in ((oa_ref, ga_ref, 0), (ob_ref, gb_ref, WIDTH_P)):
            o = jnp.where(live, o_ref[...], 0.0)
            r = lax.rsqrt(jnp.sum(o * o, axis=-1, keepdims=True) * (1.0 / WIDTH) + EPS)
            cat_ref[:, off:off + WIDTH_P] = (o * r * g_ref[...]).astype(BF16)

    return _rows(body, "outnorm_fwd", S, 256, [oa, ob], [ga, gb], [(2 * WIDTH_P, BF16)])[0]


def _outnorm_bwd(oa, ob, ga, gb, dcat):
    S = oa.shape[0]

    def body(step, oa_ref, ob_ref, dcat_ref, ga_ref, gb_ref, dpa_ref, dob_ref, dga_ref, dgb_ref):
        live = lax.broadcasted_iota(jnp.int32, oa_ref.shape, 1) % LANES < HEAD_DIM
        lane = lax.broadcasted_iota(jnp.int32, (oa_ref.shape[0], LANES), 1)
        packed = oa_ref[...]
        o = jnp.where(live, packed, 0.0)
        do, dgr = _rms_bwd_math(o, ga_ref[...], dcat_ref[:, 0:WIDTH_P], WIDTH)
        _acc_add(step, dga_ref, jnp.sum(dgr, axis=0, keepdims=True))
        prod = do.astype(BF16).astype(F32) * o
        for h in range(N_HEADS):
            blk = slice(h * LANES, (h + 1) * LANES)
            delta = jnp.sum(prod[:, blk], axis=-1, keepdims=True)
            lse = jnp.sum(jnp.where(lane == HEAD_DIM, packed[:, blk], 0.0), axis=-1, keepdims=True)
            out = do[:, blk]
            for k, piece in enumerate(_split3(-delta) + _split3(-lse)):
                out = jnp.where(lane == HEAD_DIM + k, piece, out)
            dpa_ref[:, blk] = out

        ov = ob_ref[...]
        do_b, dgr_b = _rms_bwd_math(ov, gb_ref[...], dcat_ref[:, WIDTH_P:2 * WIDTH_P], WIDTH)
        dob_ref[...] = do_b.astype(BF16)
        _acc_add(step, dgb_ref, jnp.sum(dgr_b, axis=0, keepdims=True))

    return _rows(body, "outnorm_bwd", S, 256, [oa, ob, dcat], [ga, gb],
                 [(WIDTH_P, F32), (WIDTH_P, BF16)], [((1, WIDTH_P), F32), ((1, WIDTH_P), F32)])


def _post_mix(x, y, g_post, g_pre):
    S, W = x.shape

    def body(step, x_ref, y_ref, gp_ref, gq_ref, x1_ref, h_ref):
        yv = y_ref[...]
        r = lax.rsqrt(jnp.mean(yv * yv, axis=-1, keepdims=True) + EPS)
        x1 = x_ref[...] + yv * r * gp_ref[...]
        x1_ref[...] = x1
        r1 = lax.rsqrt(jnp.mean(x1 * x1, axis=-1, keepdims=True) + EPS)
        h_ref[...] = (x1 * r1 * gq_ref[...]).astype(BF16)

    return _rows(body, "post_mix", S, 512, [x, y], [g_post, g_pre], [(W, F32), (W, BF16)])


def _final(x1, y2, g, target):
    S, W = x1.shape
    nsteps = S // 256

    def body(step, x1_ref, y_ref, t_ref, g_ref, dx2_ref, dy_ref, dg_ref, sq_ref, loss_ref):
        yv = y_ref[...]
        gv = g_ref[...]
        r = lax.rsqrt(jnp.mean(yv * yv, axis=-1, keepdims=True) + EPS)
        yn = yv * r
        err = (x1_ref[...] + yn * gv) - t_ref[...]
        dx2 = err * (1.0 / W)
        dx2_ref[...] = dx2
        dyg = dx2 * gv
        dy = r * (dyg - yn * jnp.mean(dyg * yn, axis=-1, keepdims=True))
        dy_ref[...] = dy.astype(BF16)
        _acc_add(step, dg_ref, jnp.sum(dx2 * yn, axis=0, keepdims=True))
        _acc_add(step, sq_ref, jnp.sum(err * err, axis=0, keepdims=True))

        @pl.when(step == nsteps - 1)
        def _():
            tot = jnp.sum(sq_ref[...], axis=-1, keepdims=True) * (0.5 / W)
            loss_ref[...] = jnp.broadcast_to(tot, (1, LANES))

    return _rows(body, "final_loss", S, 256, [x1, y2, target], [g], [(W, F32), (W, BF16)],
                 [((1, W), F32), ((1, W), F32), ((1, LANES), F32)])


_GELU_C = math.sqrt(2.0 / math.pi)
_CONV_CHUNK = 512
_HALO = 8


def _gelu(g):
    t = jnp.tanh(_GELU_C * (g + 0.044715 * (g * g * g)))
    return g * (0.5 * (1.0 + t)), t


def _fill_padded(pad_ref, src_ref, S):
    zeros = jnp.zeros((_HALO, LANES), F32)
    pad_ref[0:_HALO, :] = zeros
    pad_ref[_HALO + S:2 * _HALO + S, :] = zeros
    for r0 in range(0, S, _CONV_CHUNK):
        pad_ref[_HALO + r0:_HALO + r0 + _CONV_CHUNK, :] = src_ref[r0:r0 + _CONV_CHUNK, :].astype(F32)


def _conv_fwd(u0, conv_w, conv_b):
    S, C2 = u0.shape
    nb = D_FF // LANES

    def body(u0g_ref, u0v_ref, wg_ref, wv_ref, bg_ref, bv_ref, ug_ref, uv_ref, a_ref, pg_ref, pv_ref):
        _fill_padded(pg_ref, u0g_ref, S)
        _fill_padded(pv_ref, u0v_ref, S)
        wg, wv = wg_ref[...], wv_ref[...]
        for r0 in range(0, S, _CONV_CHUNK):
            def conv(p_ref, w, b_ref):
                base = _HALO + r0
                return (p_ref[base - 1:base - 1 + _CONV_CHUNK, :] * w[0:1, :]
                        + p_ref[base:base + _CONV_CHUNK, :] * w[1:2, :]
                        + p_ref[base + 1:base + 1 + _CONV_CHUNK, :] * w[2:3, :] + b_ref[...])
            g = conv(pg_ref, wg, bg_ref)
            v = conv(pv_ref, wv, bv_ref)
            rows = slice(r0, r0 + _CONV_CHUNK)
            ug_ref[rows, :] = g
            uv_ref[rows, :] = v
            a_ref[rows, :] = (_gelu(g)[0] * v).astype(BF16)

    col = lambda off: pl.BlockSpec((S, LANES), lambda j: (0, j + off))
    wcol = lambda off: pl.BlockSpec((3, LANES), lambda j: (0, j + off))
    bcol = lambda off: pl.BlockSpec((1, LANES), lambda j: (0, j + off))
    ug, uv, a = pl.pallas_call(
        body, name="conv_gelu_fwd", grid=(nb,),
        in_specs=[col(0), col(nb), wcol(0), wcol(nb), bcol(0), bcol(nb)],
        out_specs=[col(0), col(0), col(0)],
        out_shape=[jax.ShapeDtypeStruct((S, D_FF), F32), jax.ShapeDtypeStruct((S, D_FF), F32),
                   jax.ShapeDtypeStruct((S, D_FF), BF16)],
        scratch_shapes=[pltpu.VMEM((S + 2 * _HALO, LANES), F32), pltpu.VMEM((S + 2 * _HALO, LANES), F32)],
        compiler_params=_cparams(("arbitrary",)),
    )(u0, u0, conv_w, conv_w, conv_b, conv_b)
    return ug, uv, a


def _conv_bwd(u0, ug, uv, da, conv_w):
    S = u0.shape[0]
    nb = D_FF // LANES

    def body(u0_ref, ug_ref, uv_ref, da_ref, w_ref, du0_ref, dw_ref, db_ref, pu_ref, pd_ref):
        is_g = pl.program_id(0) < nb
        _fill_padded(pu_ref, u0_ref, S)
        zeros = jnp.zeros((_HALO, LANES), F32)
        pd_ref[0:_HALO, :] = zeros
        pd_ref[_HALO + S:2 * _HALO + S, :] = zeros
        sel = jnp.where(is_g, 1.0, 0.0).astype(F32)
        for r0 in range(0, S, _CONV_CHUNK):
            rows = slice(r0, r0 + _CONV_CHUNK)
            g, v, d = ug_ref[rows, :], uv_ref[rows, :], da_ref[rows, :]
            gel, t = _gelu(g)
            dgel = 0.5 * (1.0 + t) + (0.5 * g) * (1.0 - t * t) * (_GELU_C * (1.0 + 3.0 * 0.044715 * (g * g)))
            du = d * (sel * (v * dgel) + (1.0 - sel) * gel)
            pd_ref[_HALO + r0:_HALO + r0 + _CONV_CHUNK, :] = du
        w = w_ref[...]
        acc_b = jnp.zeros((1, LANES), F32)
        acc_w = [jnp.zeros((1, LANES), F32) for _ in range(3)]
        for r0 in range(0, S, _CONV_CHUNK):
            base = _HALO + r0
            du_m = pd_ref[base - 1:base - 1 + _CONV_CHUNK, :]
            du_c = pd_ref[base:base + _CONV_CHUNK, :]
            du_p = pd_ref[base + 1:base + 1 + _CONV_CHUNK, :]
            du0_ref[r0:r0 + _CONV_CHUNK, :] = (du_p * w[0:1, :] + du_c * w[1:2, :] + du_m * w[2:3, :]).astype(BF16)
            acc_b = acc_b + jnp.sum(du_c, axis=0, keepdims=True)
            for k in range(3):
                acc_w[k] = acc_w[k] + jnp.sum(du_c * pu_ref[base + k - 1:base + k - 1 + _CONV_CHUNK, :],
                                              axis=0, keepdims=True)
        db_ref[...] = acc_b
        for k in range(3):
            dw_ref[k:k + 1, :] = acc_w[k]

    own = pl.BlockSpec((S, LANES), lambda j: (0, j))
    half = lambda off: pl.BlockSpec((S, LANES), lambda j: (0, j % nb))
    du0, dw, db = pl.pallas_call(
        body, name="conv_gelu_bwd", grid=(2 * nb,),
        in_specs=[own, half(0), half(0), half(0), pl.BlockSpec((3, LANES), lambda j: (0, j))],
        out_specs=[own, pl.BlockSpec((3, LANES), lambda j: (0, j)), pl.BlockSpec((1, LANES), lambda j: (0, j))],
        out_shape=[jax.ShapeDtypeStruct((S, 2 * D_FF), BF16), jax.ShapeDtypeStruct((3, 2 * D_FF), F32),
                   jax.ShapeDtypeStruct((1, 2 * D_FF), F32)],
        scratch_shapes=[pltpu.VMEM((S + 2 * _HALO, LANES), F32), pltpu.VMEM((S + 2 * _HALO, LANES), F32)],
        compiler_params=_cparams(("arbitrary",)),
    )(u0, ug, uv, da, conv_w)
    return du0, dw, db


DIL_Q = 128
DIL_HALF = 64
DIL_SLAB = DIL_Q + 2 * DIL_HALF
DILATIONS = (1, 4, 16)
DIL_SEG = 2048


def _dil_bias(r):
    row = jnp.arange(DIL_Q, dtype=jnp.int32)[:, None]
    col = jnp.arange(DIL_SLAB, dtype=jnp.int32)[None, :]
    ad = jnp.abs(col - DIL_HALF - row)
    slopes = jnp.exp2(-8.0 * jnp.arange(1, N_HEADS + 1, dtype=F32) / N_HEADS)
    base = jnp.where(ad <= DIL_HALF, -slopes[:, None, None] * (ad * r).astype(F32)[None], MASKED)
    before = jnp.broadcast_to(col < DIL_HALF, (DIL_Q, DIL_SLAB))
    after = jnp.broadcast_to(col >= DIL_Q + DIL_HALF, (DIL_Q, DIL_SLAB))
    variants = [base, jnp.where(before, MASKED, base), jnp.where(after, MASKED, base),
                jnp.where(before | after, MASKED, base)]
    return jnp.stack(variants, axis=1)


def _lanes_hi_to_all(x):
    lane = lax.broadcasted_iota(jnp.int32, x.shape, 1)
    return jnp.where(lane < HEAD_DIM, pltpu.roll(x, HEAD_DIM, 1), x)


def _fill_kv(kp_ref, vp_ref, k_ref, v_ref, L, ones):
    zeros = jnp.zeros((DIL_HALF, LANES), BF16)
    for ref in (kp_ref, vp_ref):
        ref[0:DIL_HALF, :] = zeros
        ref[DIL_HALF + L:2 * DIL_HALF + L, :] = zeros
    step = min(L, 512)
    lane = lax.broadcasted_iota(jnp.int32, (step, LANES), 1)
    for r0 in range(0, L, step):
        kp_ref[DIL_HALF + r0:DIL_HALF + r0 + step, :] = k_ref[r0:r0 + step, :]
        vv = v_ref[r0:r0 + step, :]
        vp_ref[DIL_HALF + r0:DIL_HALF + r0 + step, :] = jnp.where(lane < HEAD_DIM, vv, 1.0).astype(BF16) if ones else vv


def _dil_fwd(proj_a, bias, r):
    S = proj_a.shape[0]
    L = S // r
    nblk = L // DIL_Q
    pv = proj_a.reshape(L, r * 3 * WIDTH_P)

    def body(q_ref, k_ref, v_ref, b_ref, o_ref, kp_ref, vp_ref):
        _fill_kv(kp_ref, vp_ref, k_ref, v_ref, L, True)
        lane = lax.broadcasted_iota(jnp.int32, (DIL_Q, LANES), 1)

        def block(i, carry):
            rows = pl.ds(pl.multiple_of(i * DIL_Q, DIL_Q), DIL_Q)
            slab = pl.ds(pl.multiple_of(i * DIL_Q, DIL_Q), DIL_SLAB)
            variant = jnp.where(i == 0, 1, 0) + jnp.where(i == nblk - 1, 2, 0)
            qv = q_ref[rows, :] * DIL_SCALE
            s = lax.dot_general(qv, kp_ref[slab, :], _NT, preferred_element_type=F32) + b_ref[0, variant]
            m = jnp.max(s, axis=-1, keepdims=True)
            acc = jnp.dot(jnp.exp(s - m).astype(BF16), vp_ref[slab, :], preferred_element_type=F32)
            l = _lanes_hi_to_all(acc)
            o_ref[rows, :] = jnp.where(lane < HEAD_DIM, acc / l, m + jnp.log(l))
            return carry

        lax.fori_loop(0, nblk, block, 0, unroll=min(4, nblk))

    col = lambda part: pl.BlockSpec((L, LANES), lambda c, h: (0, c * 3 * N_HEADS + part * N_HEADS + h))
    out = pl.pallas_call(
        body, name="dil_fwd_r%d" % r, grid=(r, N_HEADS),
        in_specs=[col(0), col(1), col(2), pl.BlockSpec((1, 4, DIL_Q, DIL_SLAB), lambda c, h: (h, 0, 0, 0))],
        out_specs=pl.BlockSpec((L, LANES), lambda c, h: (0, c * N_HEADS + h)),
        out_shape=jax.ShapeDtypeStruct((L, r * WIDTH_P), F32),
        scratch_shapes=[pltpu.VMEM((L + 2 * DIL_HALF, LANES), BF16), pltpu.VMEM((L + 2 * DIL_HALF, LANES), BF16)],
        compiler_params=_cparams(("parallel", "parallel")),
    )(pv, pv, pv, bias)
    return out.reshape(S, WIDTH_P)


def _dil_combine(branches):
    S = branches[0].shape[0]

    def body(step, *refs):
        o_ref, lse_ref = refs[-2], refs[-1]
        for h in range(N_HEADS):
            blk = slice(h * LANES, (h + 1) * LANES)
            xs = [r_[:, blk] for r_ in refs[:-2]]
            lses = [_lanes_hi_to_all(x) for x in xs]
            m = functools.reduce(jnp.maximum, lses)
            ws = [jnp.exp(l - m) for l in lses]
            tot = functools.reduce(jnp.add, ws)
            lane = lax.broadcasted_iota(jnp.int32, xs[0].shape, 1)
            o = functools.reduce(jnp.add, [w * x for w, x in zip(ws, xs)]) / tot
            o_ref[:, blk] = jnp.where(lane < HEAD_DIM, o, 0.0)
            lse_ref[:, blk] = m + jnp.log(tot)

    return _rows(body, "dil_combine", S, 256, list(branches), [], [(WIDTH_P, F32), (WIDTH_P, F32)])


def _dil_bwd(proj_a, do, lse, delta, bias, r):
    S = proj_a.shape[0]
    L = S // r
    seg = min(L, DIL_SEG)
    nseg, nblk, nblk_seg = L // seg, L // DIL_Q, seg // DIL_Q
    pv = proj_a.reshape(L, r * 3 * WIDTH_P)
    view = lambda a: a.reshape(L, r * WIDTH_P)
    _TN = (((0,), (0,)), ((), ()))

    def body(q_ref, k_ref, v_ref, do_ref, lse_ref, dl_ref, b_ref, dq_ref, dk_ref, dv_ref,
             kp_ref, vp_ref, dkp_ref, dvp_ref):
        sg = pl.program_id(2)

        @pl.when(sg == 0)
        def _():
            _fill_kv(kp_ref, vp_ref, k_ref, v_ref, L, False)
            dkp_ref[...] = jnp.zeros_like(dkp_ref)
            dvp_ref[...] = jnp.zeros_like(dvp_ref)

        def block(j, carry):
            i = sg * nblk_seg + j
            rows = pl.ds(pl.multiple_of(j * DIL_Q, DIL_Q), DIL_Q)
            slab = pl.ds(pl.multiple_of(i * DIL_Q, DIL_Q), DIL_SLAB)
            variant = jnp.where(i == 0, 1, 0) + jnp.where(i == nblk - 1, 2, 0)
            qv = q_ref[rows, :] * DIL_SCALE
            dov = do_ref[rows, :]
            ks, vs = kp_ref[slab, :], vp_ref[slab, :]
            two = lambda a: jnp.concatenate([a, a], axis=1)
            s = lax.dot_general(qv, ks, _NT, preferred_element_type=F32) + b_ref[0, variant]
            p = jnp.exp(s - two(lse_ref[rows, :]))
            dp = lax.dot_general(dov, vs, _NT, preferred_element_type=F32)
            ds = (p * (dp - two(dl_ref[rows, :]))).astype(BF16)
            dq_ref[rows, :] = jnp.dot(ds, ks, preferred_element_type=F32) * DIL_SCALE
            dkp_ref[slab, :] += lax.dot_general(ds, qv, _TN, preferred_element_type=F32)
            dvp_ref[slab, :] += lax.dot_general(p.astype(BF16), dov, _TN, preferred_element_type=F32)
            return carry

        lax.fori_loop(0, nblk_seg, block, 0, unroll=min(2, nblk_seg))

        @pl.when(sg == nseg - 1)
        def _():
            dk_ref[...] = dkp_ref[DIL_HALF:DIL_HALF + L, :]
            dv_ref[...] = dvp_ref[DIL_HALF:DIL_HALF + L, :]

    col = lambda part: pl.BlockSpec((L, LANES), lambda c, h, s: (0, c * 3 * N_HEADS + part * N_HEADS + h))
    segspec = pl.BlockSpec((seg, LANES), lambda c, h, s: (s, c * N_HEADS + h))
    fullspec = pl.BlockSpec((L, LANES), lambda c, h, s: (0, c * N_HEADS + h))
    out = jax.ShapeDtypeStruct((L, r * WIDTH_P), F32)
    dq, dk, dv = pl.pallas_call(
        body, name="dil_bwd_r%d" % r, grid=(r, N_HEADS, nseg),
        in_specs=[pl.BlockSpec((seg, LANES), lambda c, h, s: (s, c * 3 * N_HEADS + h)), col(1), col(2),
                  segspec, segspec, segspec, pl.BlockSpec((1, 4, DIL_Q, DIL_SLAB), lambda c, h, s: (h, 0, 0, 0))],
        out_specs=[segspec, fullspec, fullspec], out_shape=[out, out, out],
        scratch_shapes=[pltpu.VMEM((L + 2 * DIL_HALF, LANES), BF16), pltpu.VMEM((L + 2 * DIL_HALF, LANES), BF16),
                        pltpu.VMEM((L + 2 * DIL_HALF, LANES), F32), pltpu.VMEM((L + 2 * DIL_HALF, LANES), F32)],
        compiler_params=_cparams(("arbitrary", "arbitrary", "arbitrary")),
    )(pv, pv, pv, view(do), view(lse), view(delta), bias)
    return [a.reshape(S, WIDTH_P) for a in (dq, dk, dv)]


def _dil_sum(grads):
    S = grads[0][0].shape[0]
    nb = len(grads)

    def body(step, *refs):
        out_ref = refs[-1]
        for part in range(3):
            tot = refs[part][...]
            for b in range(1, nb):
                tot = tot + refs[3 * b + part][...]
            out_ref[:, part * WIDTH_P:(part + 1) * WIDTH_P] = tot.astype(BF16)

    return _rows(body, "dil_sum", S, 256, [a for g in grads for a in g], [], [(3 * WIDTH_P, BF16)])[0]


_NT = (((1,), (1,)), ((), ()))
_TN = (((0,), (0,)), ((), ()))
HEAD_COLS = 3 * LANES


def _slab_bias():
    row = jnp.arange(DIL_Q, dtype=jnp.int32)[:, None]
    col = jnp.arange(DIL_SLAB, dtype=jnp.int32)[None, :]
    slopes = jnp.exp2(-8.0 * jnp.arange(1, N_HEADS + 1, dtype=F32) / N_HEADS)
    out = []
    for r in DILATIONS:
        variants = []
        for shift in (DIL_HALF, 0, DIL_Q):
            ad = jnp.abs(col - shift - row)
            variants.append(jnp.where(ad <= DIL_HALF, -slopes[:, None, None] * (ad * r).astype(F32)[None], MASKED))
        out.append(jnp.stack(variants, axis=1))
    return jnp.stack(out, axis=0)


DIL_CHUNK = 512


def _block_geometry(i, nblk):
    first = pl.multiple_of(i * DIL_Q, DIL_Q)
    slab0 = pl.multiple_of(jnp.clip(i * DIL_Q - DIL_HALF, 0, (nblk - 2) * DIL_Q), DIL_HALF)
    variant = jnp.where(i == 0, 1, jnp.where(i == nblk - 1, 2, 0))
    return first, slab0, variant


def _class_rows(c, r, r0, n):
    return pl.ds(c + r0 * r, n, stride=r) if r > 1 else pl.ds(r0, n)


def _dila_fwd(proj_a, bias):
    S = proj_a.shape[0]
    lmax = S // DILATIONS[1]

    def body(qh_ref, kh_ref, vh_ref, b_ref, o_ref, q_s, k_s, v_s, cm_s):
        lane = lax.broadcasted_iota(jnp.int32, (DIL_Q, LANES), 1)
        lane_s = lax.broadcasted_iota(jnp.int32, (DIL_SLAB, LANES), 1)
        lane_c = lax.broadcasted_iota(jnp.int32, (DIL_CHUNK, LANES), 1)

        def run(g, nblk, load_q, load_k, load_v, store):
            def block(i, carry):
                first, slab0, variant = _block_geometry(i, nblk)
                qv, ks, vs = load_q(first), load_k(slab0), load_v(slab0)
                s = lax.dot_general(qv, ks, _NT, preferred_element_type=F32) + b_ref[g, 0, variant]
                m = jnp.max(s, axis=-1, keepdims=True)
                acc = jnp.dot(jnp.exp(s - m).astype(BF16), vs, preferred_element_type=F32)
                l = _lanes_hi_to_all(acc)
                store(first, jnp.where(lane < HEAD_DIM, acc / l, m + jnp.log(l)))
                return carry

            lax.fori_loop(0, nblk, block, 0, unroll=4)

        def direct_store(first, val):
            o_ref[pl.ds(first, DIL_Q), :] = val

        run(0, S // DIL_Q,
            lambda f: (qh_ref[pl.ds(f, DIL_Q), :] * DIL_SCALE).astype(BF16),
            lambda s0: kh_ref[pl.ds(s0, DIL_SLAB), :].astype(BF16),
            lambda s0: jnp.where(lane_s < HEAD_DIM, vh_ref[pl.ds(s0, DIL_SLAB), :], 1.0).astype(BF16),
            direct_store)

        def cm_store(first, val):
            cm_s[pl.ds(first, DIL_Q), :] = val

        for g, r in list(enumerate(DILATIONS))[1:]:
            L = S // r
            n = min(L, DIL_CHUNK)
            for c in range(r):
                for r0 in range(0, L, n):
                    src = _class_rows(c, r, r0, n)
                    q_s[r0:r0 + n, :] = (qh_ref[src, :] * DIL_SCALE).astype(BF16)
                    k_s[r0:r0 + n, :] = kh_ref[src, :].astype(BF16)
                    v_s[r0:r0 + n, :] = jnp.where(lane_c[:n] < HEAD_DIM, vh_ref[src, :], 1.0).astype(BF16)
                run(g, L // DIL_Q, lambda f: q_s[pl.ds(f, DIL_Q), :], lambda s0: k_s[pl.ds(s0, DIL_SLAB), :],
                    lambda s0: v_s[pl.ds(s0, DIL_SLAB), :], cm_store)
                for r0 in range(0, L, n):
                    dst = _class_rows(c, r, r0, n)
                    a, b = cm_s[r0:r0 + n, :], o_ref[dst, :]
                    la, lb = _lanes_hi_to_all(a), _lanes_hi_to_all(b)
                    m = jnp.maximum(la, lb)
                    wa, wb = jnp.exp(la - m), jnp.exp(lb - m)
                    tot = wa + wb
                    o_ref[dst, :] = jnp.where(lane_c[:n] < HEAD_DIM, (wa * a + wb * b) / tot, m + jnp.log(tot))

    return pl.pallas_call(
        body, name="dil_fwd", grid=(N_HEADS,),
        in_specs=[pl.BlockSpec((S, LANES), lambda h: (0, 3 * h)), pl.BlockSpec((S, LANES), lambda h: (0, 3 * h + 1)),
                  pl.BlockSpec((S, LANES), lambda h: (0, 3 * h + 2)),
                  pl.BlockSpec((len(DILATIONS), 1, 3, DIL_Q, DIL_SLAB), lambda h: (0, h, 0, 0, 0))],
        out_specs=pl.BlockSpec((S, LANES), lambda h: (0, h)),
        out_shape=jax.ShapeDtypeStruct((S, WIDTH_P), F32),
        scratch_shapes=[pltpu.VMEM((lmax, LANES), BF16), pltpu.VMEM((lmax, LANES), BF16),
                        pltpu.VMEM((lmax, LANES), BF16), pltpu.VMEM((lmax, LANES), F32)],
        compiler_params=_cparams(("parallel",)),
    )(proj_a, proj_a, proj_a, bias)


N_SPLIT = 3


def _split3(x):
    hi = x.astype(BF16).astype(F32)
    mid = (x - hi).astype(BF16).astype(F32)
    lo = (x - hi - mid).astype(BF16).astype(F32)
    return hi, mid, lo


def _dila_bwd(proj_a, dopack, bias):
    S = proj_a.shape[0]
    lmax = S // DILATIONS[1]

    def body(qh_ref, kh_ref, vh_ref, d_ref, b_ref, out_ref, dq_ref, dk_ref, dv_ref, q_s, k_s, v_s, do_s,
             dq_c, dk_c, dv_c):
        def scalar_lanes(shape):
            lane = lax.broadcasted_iota(jnp.int32, shape, 1)
            return lane, (lane >= HEAD_DIM) & (lane < HEAD_DIM + N_SPLIT)

        def q_side(q, x):
            lane, ones = scalar_lanes(x.shape)
            lse_parts = pltpu.roll(x, LANES - N_SPLIT, 1)
            qv = jnp.where(lane < HEAD_DIM, q * DIL_SCALE, jnp.where(ones, lse_parts, 0.0)).astype(BF16)
            return qv, jnp.where(lane < HEAD_DIM + N_SPLIT, x, 0.0).astype(BF16)

        def kv_side(k, v):
            _, ones = scalar_lanes(k.shape)
            return jnp.where(ones, 1.0, k).astype(BF16), jnp.where(ones, 1.0, v).astype(BF16)

        def run(g, nblk, load_q, load_kv, dq_o, dk_o, dv_o):
            def block(i, carry):
                first, slab0, variant = _block_geometry(i, nblk)
                rows, slab = pl.ds(first, DIL_Q), pl.ds(slab0, DIL_SLAB)
                (qv, dov), (ks, vs) = load_q(rows), load_kv(slab)
                p = jnp.exp(lax.dot_general(qv, ks, _NT, preferred_element_type=F32) + b_ref[g, 0, variant])
                ds = (p * lax.dot_general(dov, vs, _NT, preferred_element_type=F32)).astype(BF16)
                dq_o[rows, :] = jnp.dot(ds, ks, preferred_element_type=F32) * DIL_SCALE
                dk_o[slab, :] += lax.dot_general(ds, qv, _TN, preferred_element_type=F32)
                dv_o[slab, :] += lax.dot_general(p.astype(BF16), dov, _TN, preferred_element_type=F32)
                return carry

            lax.fori_loop(0, nblk, block, 0, unroll=2)

        dk_ref[...] = jnp.zeros_like(dk_ref)
        dv_ref[...] = jnp.zeros_like(dv_ref)
        run(0, S // DIL_Q,
            lambda rows: q_side(qh_ref[rows, :], d_ref[rows, :]),
            lambda slab: kv_side(kh_ref[slab, :], vh_ref[slab, :]),
            dq_ref, dk_ref, dv_ref)

        for g, r in list(enumerate(DILATIONS))[1:]:
            L = S // r
            n = min(L, DIL_CHUNK)
            for c in range(r):
                for r0 in range(0, L, n):
                    src = _class_rows(c, r, r0, n)
                    q_s[r0:r0 + n, :], do_s[r0:r0 + n, :] = q_side(qh_ref[src, :], d_ref[src, :])
                    k_s[r0:r0 + n, :], v_s[r0:r0 + n, :] = kv_side(kh_ref[src, :], vh_ref[src, :])
                    dk_c[r0:r0 + n, :] = jnp.zeros((n, LANES), F32)
                    dv_c[r0:r0 + n, :] = jnp.zeros((n, LANES), F32)
                run(g, L // DIL_Q, lambda rows: (q_s[rows, :], do_s[rows, :]),
                    lambda slab: (k_s[slab, :], v_s[slab, :]), dq_c, dk_c, dv_c)
                for r0 in range(0, L, n):
                    dst = _class_rows(c, r, r0, n)
                    for acc, cls in ((dq_ref, dq_c), (dk_ref, dk_c), (dv_ref, dv_c)):
                        acc[dst, :] += cls[r0:r0 + n, :]

        for r0 in range(0, S, DIL_CHUNK):
            for part, ref in enumerate((dq_ref, dk_ref, dv_ref)):
                out_ref[r0:r0 + DIL_CHUNK, part * LANES:(part + 1) * LANES] = ref[r0:r0 + DIL_CHUNK, :].astype(BF16)

    bf = lambda rows: pltpu.VMEM((rows, LANES), BF16)
    f32 = lambda rows: pltpu.VMEM((rows, LANES), F32)
    return pl.pallas_call(
        body, name="dil_bwd", grid=(N_HEADS,),
        in_specs=[pl.BlockSpec((S, LANES), lambda h: (0, 3 * h), pipeline_mode=pl.Buffered(1)),
                  pl.BlockSpec((S, LANES), lambda h: (0, 3 * h + 1), pipeline_mode=pl.Buffered(1)),
                  pl.BlockSpec((S, LANES), lambda h: (0, 3 * h + 2), pipeline_mode=pl.Buffered(1)),
                  pl.BlockSpec((S, LANES), lambda h: (0, h), pipeline_mode=pl.Buffered(1)),
                  pl.BlockSpec((len(DILATIONS), 1, 3, DIL_Q, DIL_SLAB), lambda h: (0, h, 0, 0, 0))],
        out_specs=pl.BlockSpec((S, HEAD_COLS), lambda h: (0, h)),
        out_shape=jax.ShapeDtypeStruct((S, N_HEADS * HEAD_COLS), BF16),
        scratch_shapes=[f32(S), f32(S), f32(S), bf(lmax), bf(lmax), bf(lmax), bf(lmax),
                        f32(lmax), f32(lmax), f32(lmax)],
        compiler_params=_cparams(("arbitrary",)),
    )(proj_a, proj_a, proj_a, dopack, bias)


def _mla_fwd(q, k, vt):
    S = q.shape[0]
    tq, tk = MLA_TQ, MLA_TK
    nq, nk = S // tq, S // tk

    def body(q_ref, k_ref, vt_ref, o_ref, lse_ref, acc_ref):
        qv = q_ref[...]
        acc_ref[...] = jnp.zeros_like(acc_ref)

        def chunk(c, m):
            kc = k_ref[pl.ds(pl.multiple_of(c * tk, tk), tk), :]
            st = lax.dot_general(kc, qv, _NT, preferred_element_type=F32)
            m_new = jnp.maximum(m, jnp.max(st, axis=0, keepdims=True))
            pt = jnp.exp(st - m_new).astype(BF16)
            acc_ref[...] = jnp.exp(m - m_new) * acc_ref[...] + jnp.dot(vt_ref[0, c], pt,
                                                                        preferred_element_type=F32)
            return m_new

        m = lax.fori_loop(0, nk, chunk, jnp.full((1, tq), M_INIT, F32), unroll=2)
        acc = acc_ref[...]
        l = acc[HEAD_DIM:HEAD_DIM + 1, :]
        row = lax.broadcasted_iota(jnp.int32, acc.shape, 0)
        o_ref[...] = jnp.where(row < HEAD_DIM, acc / l, 0.0).T
        lse_ref[0] = m + jnp.log(l)

    return pl.pallas_call(
        body, name="mla_fwd", grid=(N_HEADS, nq),
        in_specs=[pl.BlockSpec((tq, LANES), lambda h, i: (i, h)),
                  pl.BlockSpec((S, LANES), lambda h, i: (0, h)),
                  pl.BlockSpec((1, nk, LANES, tk), lambda h, i: (h, 0, 0, 0))],
        out_specs=[pl.BlockSpec((tq, LANES), lambda h, i: (i, h)),
                   pl.BlockSpec((1, 1, tq), lambda h, i: (h, 0, i))],
        out_shape=[jax.ShapeDtypeStruct((S, WIDTH_P), F32), jax.ShapeDtypeStruct((N_HEADS, 1, S), F32)],
        scratch_shapes=[pltpu.VMEM((LANES, tq), F32)],
        compiler_params=_cparams(("parallel", "parallel")),
    )(q, k, vt)


def _mla_bwd(q, k, v, kt, do, o, lse):
    S = q.shape[0]
    tq, tk = MLA_TQ, MLA_TK
    nq, nk = S // tq, S // tk

    def body(q_ref, do_ref, o_ref, lse_ref, k_ref, v_ref, kt_ref, dq_ref, dk_ref, dv_ref, dqt_ref):
        @pl.when(pl.program_id(1) == 0)
        def _():
            dk_ref[...] = jnp.zeros_like(dk_ref)
            dv_ref[...] = jnp.zeros_like(dv_ref)

        qv, dov = q_ref[...], do_ref[...]
        delta = jnp.sum((dov.astype(F32) * o_ref[...]).T, axis=0, keepdims=True)
        lse = lse_ref[0]
        dqt_ref[...] = jnp.zeros_like(dqt_ref)

        def chunk(c, carry):
            rows = pl.ds(pl.multiple_of(c * tk, tk), tk)
            kc, vc = k_ref[rows, :], v_ref[rows, :]
            pt = jnp.exp(lax.dot_general(kc, qv, _NT, preferred_element_type=F32) - lse)
            dv_ref[rows, :] += jnp.dot(pt.astype(BF16), dov, preferred_element_type=F32)
            dpt = lax.dot_general(vc, dov, _NT, preferred_element_type=F32)
            dst = (pt * (dpt - delta)).astype(BF16)
            dk_ref[rows, :] += jnp.dot(dst, qv, preferred_element_type=F32)
            dqt_ref[...] += jnp.dot(kt_ref[0, c], dst, preferred_element_type=F32)
            return carry

        lax.fori_loop(0, nk, chunk, 0, unroll=2)
        dq_ref[...] = (dqt_ref[...] * MLA_SCALE).T

    qspec = pl.BlockSpec((tq, LANES), lambda h, i: (i, h))
    kspec = pl.BlockSpec((S, LANES), lambda h, i: (0, h))
    out = jax.ShapeDtypeStruct((S, WIDTH_P), F32)
    return pl.pallas_call(
        body, name="mla_bwd", grid=(N_HEADS, nq),
        in_specs=[qspec, qspec, qspec, pl.BlockSpec((1, 1, tq), lambda h, i: (h, 0, i)), kspec, kspec,
                  pl.BlockSpec((1, nk, LANES, tk), lambda h, i: (h, 0, 0, 0))],
        out_specs=[qspec, kspec, kspec], out_shape=[out, out, out],
        scratch_shapes=[pltpu.VMEM((LANES, tq), F32)],
        compiler_params=_cparams(("arbitrary", "arbitrary")),
    )(q, do, o, lse, k, v, kt)


def _mla_bwd_prep(dq, dk, dv, tabs):
    S = dq.shape[0]

    def body(step, dq_ref, dk_ref, dv_ref, c_ref, sa_ref, sb_ref, dqp_ref, dkv_ref, dkr_ref):
        c, sa, sb = c_ref[...], sa_ref[...], sb_ref[...]
        dksum = jnp.zeros((dq_ref.shape[0], LANES), F32)
        for h in range(N_HEADS):
            blk = slice(h * LANES, (h + 1) * LANES)
            dqp_ref[:, blk] = _rope_transpose(dq_ref[:, blk], c, sa, sb).astype(BF16)
            dksum = dksum + dk_ref[:, blk]
        dkv_ref[:, 0:WIDTH_P] = dk_ref[...].astype(BF16)
        dkv_ref[:, WIDTH_P:2 * WIDTH_P] = dv_ref[...].astype(BF16)
        lane = lax.broadcasted_iota(jnp.int32, dksum.shape, 1)
        live = (lane >= HEAD_DIM) & (lane < HEAD_DIM + QK_ROPE)
        dkr_ref[...] = jnp.where(live, _rope_transpose(dksum, c, sa, sb), 0.0)

    return _rows(body, "mla_bwd_prep", S, 256, [dq, dk, dv, *tabs], [],
                 [(WIDTH_P, BF16), (2 * WIDTH_P, BF16), (LANES, F32)])


def _mla_norm_bwd(proj_b, dcq_n, dckv_n, dkr, g_q, g_kv):
    S = proj_b.shape[0]

    def body(step, p_ref, dcq_ref, dckv_ref, dkr_ref, gq_ref, gkv_ref, dp_ref, dgq_ref, dgkv_ref):
        dcq, dgq = _rms_bwd_math(p_ref[:, 0:Q_LORA], gq_ref[...], dcq_ref[...], Q_LORA)
        dckv, dgkv = _rms_bwd_math(p_ref[:, Q_LORA:Q_LORA + KV_LORA], gkv_ref[...], dckv_ref[...], KV_LORA)
        dp_ref[:, 0:Q_LORA] = dcq.astype(BF16)
        dp_ref[:, Q_LORA:Q_LORA + KV_LORA] = dckv.astype(BF16)
        dp_ref[:, Q_LORA + KV_LORA:TAIL_P] = dkr_ref[...].astype(BF16)
        _acc_add(step, dgq_ref, jnp.sum(dgq, axis=0, keepdims=True))
        _acc_add(step, dgkv_ref, jnp.sum(dgkv, axis=0, keepdims=True))

    return _rows(body, "mla_norm_bwd", S, 512, [proj_b, dcq_n, dckv_n, dkr], [g_q, g_kv], [(TAIL_P, BF16)],
                 [((1, Q_LORA), F32), ((1, KV_LORA), F32)])


def _pad_cols(w, d):
    lead = w.shape[:-1]
    w = w.reshape(lead + (N_HEADS, d))
    w = jnp.pad(w, [(0, 0)] * len(lead) + [(0, 0), (0, LANES - d)])
    return w.reshape(lead + (N_HEADS * LANES,))


def _unpad_cols(w, d):
    lead = w.shape[:-1]
    return w.reshape(lead + (N_HEADS, LANES))[..., :d].reshape(lead + (N_HEADS * d,))


def _pad_weights(w):
    w_in = w['w_in']
    zeros = lambda n: jnp.zeros((D_MODEL, n), w_in.dtype)
    p = {}
    parts = [_pad_cols(w_in[:, i * WIDTH:(i + 1) * WIDTH], HEAD_DIM).reshape(D_MODEL, N_HEADS, 1, LANES)
             for i in range(3)]
    p['w_in_a'] = jnp.concatenate(parts, axis=2).reshape(D_MODEL, N_HEADS * HEAD_COLS)
    p['w_in_b'] = jnp.concatenate([w_in[:, 3 * WIDTH:3 * WIDTH + Q_LORA + KV_LORA], zeros(HEAD_DIM),
                                   w_in[:, D_IN - QK_ROPE:], zeros(LANES - HEAD_DIM - QK_ROPE)], axis=1)
    p['w_uq'] = _pad_cols(w['w_uq'], HEAD_DIM + QK_ROPE)
    kv = w['w_ukv'].reshape(KV_LORA, N_HEADS, 2 * HEAD_DIM)
    p['w_ukv'] = jnp.concatenate([_pad_cols(kv[:, :, :HEAD_DIM].reshape(KV_LORA, WIDTH), HEAD_DIM),
                                  _pad_cols(kv[:, :, HEAD_DIM:].reshape(KV_LORA, WIDTH), HEAD_DIM)], axis=1)
    p['w_o'] = jnp.concatenate(
        [_pad_cols(w['w_o'][i * WIDTH:(i + 1) * WIDTH].T, HEAD_DIM).T for i in range(2)], axis=0)
    p['g_a'] = _pad_cols(w['out_norm_a'], HEAD_DIM)
    p['g_b'] = _pad_cols(w['out_norm_b'], HEAD_DIM)
    return p


def _unpad_grads(d):
    g = {}
    dwa = d['w_in_a'].reshape(D_MODEL, N_HEADS, 3, LANES)
    tail = d['w_in_b']
    g['w_in'] = jnp.concatenate(
        [dwa[:, :, i, :HEAD_DIM].reshape(D_MODEL, WIDTH) for i in range(3)]
        + [tail[:, :Q_LORA + KV_LORA], tail[:, Q_LORA + KV_LORA + HEAD_DIM:Q_LORA + KV_LORA + HEAD_DIM + QK_ROPE]],
        axis=1)
    g['w_uq'] = _unpad_cols(d['w_uq'], HEAD_DIM + QK_ROPE)
    dk = _unpad_cols(d['w_ukv'][:, :WIDTH_P], HEAD_DIM).reshape(KV_LORA, N_HEADS, HEAD_DIM)
    dv = _unpad_cols(d['w_ukv'][:, WIDTH_P:], HEAD_DIM).reshape(KV_LORA, N_HEADS, HEAD_DIM)
    g['w_ukv'] = jnp.concatenate([dk, dv], axis=2).reshape(KV_LORA, 2 * WIDTH)
    g['w_o'] = jnp.concatenate(
        [_unpad_cols(d['w_o'][i * WIDTH_P:(i + 1) * WIDTH_P].T, HEAD_DIM).T for i in range(2)], axis=0)
    g['out_norm_a'] = _unpad_cols(d['g_a'], HEAD_DIM)
    g['out_norm_b'] = _unpad_cols(d['g_b'], HEAD_DIM)
    return g


def _local_step(x, target, w):
    S = x.shape[0]
    p = _pad_weights(w)
    tabs = _rope_tables(S)
    bias = _slab_bias()

    h1 = _rms_fwd(x, w['norm_mix_pre'], "rms_mix_pre")
    proj_a = _mm(h1, p['w_in_a'], 'nn', F32, "mm_in_a")
    proj_b = _mm(h1, p['w_in_b'], 'nn', F32, "mm_in_b")
    oa = _dila_fwd(proj_a, bias)
    cq_n, ckv_n, kr = _mla_prep(proj_b, w['q_lat_norm'], w['kv_lat_norm'], tabs)
    q_lin = _mm(cq_n, p['w_uq'], 'nn', F32, "mm_uq")
    kv_lin = _mm(ckv_n, p['w_ukv'], 'nn', F32, "mm_ukv")
    qb, kb, vb, kt, vt = _mla_qkv(q_lin, kv_lin, kr, tabs)
    ob, lse_b = _mla_fwd(qb, kb, vt)
    cat = _outnorm_fwd(oa, ob, p['g_a'], p['g_b'])
    y = _mm(cat, p['w_o'], 'nn', F32, "mm_o")
    x1, h2 = _post_mix(x, y, w['norm_mix_post'], w['norm_ffn_pre'])
    u0 = _mm(h2, w['w_up'], 'nn', F32, "mm_up")
    ug, uv, a = _conv_fwd(u0, w['conv_w'], w['conv_b'])
    y2 = _mm(a, w['w_down'], 'nn', F32, "mm_down")
    dx2, dy2, dg_ffn_post, _, loss = _final(x1, y2, w['norm_ffn_post'], target)

    g = {'norm_ffn_post': dg_ffn_post}
    da = _mm(dy2, w['w_down'], 'nt', F32, "mm_down_dx")
    g['w_down'] = _mm(a, dy2, 'tn', F32, "mm_down_dw")
    du0, g['conv_w'], g['conv_b'] = _conv_bwd(u0, ug, uv, da, w['conv_w'])
    dh2 = _mm(du0, w['w_up'], 'nt', F32, "mm_up_dx")
    g['w_up'] = _mm(h2, du0, 'tn', F32, "mm_up_dw")
    dx1, g['norm_ffn_pre'] = _rms_bwd(x1, w['norm_ffn_pre'], [dh2], dx2, F32, "rms_ffn_pre_bwd")
    dy, g['norm_mix_post'] = _rms_bwd(y, w['norm_mix_post'], [dx1], None, BF16, "rms_mix_post_bwd")
    dcat = _mm(dy, p['w_o'], 'nt', F32, "mm_o_dx")
    dpad = {'w_o': _mm(cat, dy, 'tn', F32, "mm_o_dw")}
    dopack_a, do_b, dpad['g_a'], dpad['g_b'] = _outnorm_bwd(oa, ob, p['g_a'], p['g_b'], dcat)

    dq_b, dk_b, dv_b = _mla_bwd(qb, kb, vb, kt, do_b, ob, lse_b)
    dq_pre, dkv, dkr = _mla_bwd_prep(dq_b, dk_b, dv_b, tabs)
    dcq_n = _mm(dq_pre, p['w_uq'], 'nt', F32, "mm_uq_dx")
    dpad['w_uq'] = _mm(cq_n, dq_pre, 'tn', F32, "mm_uq_dw")
    dckv_n = _mm(dkv, p['w_ukv'], 'nt', F32, "mm_ukv_dx")
    dpad['w_ukv'] = _mm(ckv_n, dkv, 'tn', F32, "mm_ukv_dw")
    dproj_b, g['q_lat_norm'], g['kv_lat_norm'] = _mla_norm_bwd(proj_b, dcq_n, dckv_n, dkr,
                                                               w['q_lat_norm'], w['kv_lat_norm'])

    dproj_a = _dila_bwd(proj_a, dopack_a, bias)
    dh1 = _mm(dproj_b, p['w_in_b'], 'nt', F32, "mm_in_b_dx")
    dh1 = _mm(dproj_a, p['w_in_a'], 'nt', F32, "mm_in_a_dx", add=dh1)
    dpad['w_in_a'] = _mm(h1, dproj_a, 'tn', F32, "mm_in_a_dw")
    dpad['w_in_b'] = _mm(h1, dproj_b, 'tn', F32, "mm_in_b_dw")
    grad_x, g['norm_mix_pre'] = _rms_bwd(x, w['norm_mix_pre'], [dh1], dx1, F32, "rms_mix_pre_bwd")
    g.update(_unpad_grads(dpad))
    return loss, grad_x, g


MESH = pl.DeviceIdType.MESH
ANY = pl.BlockSpec(memory_space=pl.ANY)


def _place():
    x, y, c = lax.axis_index("x"), lax.axis_index("y"), lax.axis_index("c")
    chips = [(1 - x, y), (x, 1 - y), (1 - x, 1 - y)]
    return x, y, c, chips


def _all_gather(bufs):
    n = len(bufs)

    def body(*refs):
        in_refs, out_refs = refs[:n], refs[n:2 * n]
        send_sems, recv_sems, local_sems = refs[2 * n:]
        x, y, c, chips = _place()
        me = 2 * x + y
        local = [pltpu.make_async_copy(in_refs[b], out_refs[b].at[me], local_sems.at[b]) for b in range(n)]
        for cp in local:
            cp.start()
        sends = []
        for j, (px, py) in enumerate(chips):
            for b in range(n):
                sends.append(pltpu.make_async_remote_copy(
                    src_ref=in_refs[b], dst_ref=out_refs[b].at[me], send_sem=send_sems.at[j * n + b],
                    recv_sem=recv_sems.at[j * n + b], device_id=(px, py, c), device_id_type=MESH))
        for cp in sends:
            cp.start()
        for j, (px, py) in enumerate(chips):
            for b in range(n):
                pltpu.make_async_remote_copy(
                    src_ref=in_refs[b], dst_ref=out_refs[b].at[2 * px + py], send_sem=send_sems.at[j * n + b],
                    recv_sem=recv_sems.at[j * n + b], device_id=(px, py, c), device_id_type=MESH).wait_recv()
        for cp in sends:
            cp.wait_send()
        for cp in local:
            cp.wait()

    return pl.pallas_call(
        body, name="gather_weights", in_specs=[ANY] * n, out_specs=[ANY] * n,
        out_shape=[jax.ShapeDtypeStruct((N_CHIPS,) + b.shape, b.dtype) for b in bufs],
        scratch_shapes=[pltpu.SemaphoreType.DMA((3 * n,)), pltpu.SemaphoreType.DMA((3 * n,)),
                        pltpu.SemaphoreType.DMA((n,))],
    )(*bufs)


def _scatter_grads(gflat):
    _, R, L = gflat.shape

    def body(g_ref, o_ref, send_sems, recv_sems, local_sem):
        x, y, c, chips = _place()
        me = 2 * x + y
        local = pltpu.make_async_copy(g_ref.at[me], o_ref.at[me], local_sem)
        local.start()
        sends = [pltpu.make_async_remote_copy(
            src_ref=g_ref.at[2 * px + py], dst_ref=o_ref.at[me], send_sem=send_sems.at[j], recv_sem=recv_sems.at[j],
            device_id=(px, py, c), device_id_type=MESH) for j, (px, py) in enumerate(chips)]
        for cp in sends:
            cp.start()
        for j, (px, py) in enumerate(chips):
            pltpu.make_async_remote_copy(
                src_ref=g_ref.at[me], dst_ref=o_ref.at[2 * px + py], send_sem=send_sems.at[j],
                recv_sem=recv_sems.at[j], device_id=(px, py, c), device_id_type=MESH).wait_recv()
        for cp in sends:
            cp.wait_send()
        local.wait()

    return pl.pallas_call(
        body, name="scatter_grads", in_specs=[ANY], out_specs=ANY,
        out_shape=jax.ShapeDtypeStruct(gflat.shape, gflat.dtype),
        scratch_shapes=[pltpu.SemaphoreType.DMA((3,)), pltpu.SemaphoreType.DMA((3,)), pltpu.SemaphoreType.DMA],
    )(gflat)


def _sum_slots(recv):
    _, R, L = recv.shape
    tr = _tile(R, 2048)

    def body(r_ref, o_ref):
        o_ref[...] = ((r_ref[0] + r_ref[1]) + r_ref[2]) + r_ref[3]

    return pl.pallas_call(
        body, name="sum_slots", grid=(R // tr,),
        in_specs=[pl.BlockSpec((N_CHIPS, tr, L), lambda i: (0, i, 0))],
        out_specs=pl.BlockSpec((tr, L), lambda i: (i, 0)),
        out_shape=jax.ShapeDtypeStruct((R, L), recv.dtype),
        compiler_params=_cparams(("parallel",)),
    )(recv)


def _swap_sibling(part):
    def body(p_ref, o_ref, send_sem, recv_sem):
        x, y, c, _ = _place()
        cp = pltpu.make_async_remote_copy(src_ref=p_ref, dst_ref=o_ref, send_sem=send_sem, recv_sem=recv_sem,
                                          device_id=(x, y, 1 - c), device_id_type=MESH)
        cp.start()
        cp.wait()

    return pl.pallas_call(
        body, name="swap_sibling", in_specs=[ANY], out_specs=ANY,
        out_shape=jax.ShapeDtypeStruct(part.shape, part.dtype),
        scratch_shapes=[pltpu.SemaphoreType.DMA, pltpu.SemaphoreType.DMA],
    )(part)


def _adamw(g0, g1, w, m, v):
    R, L = w.shape
    tr = _tile(R, 2048)
    bc1 = 1.0 - ADAM_B1 ** ADAM_STEP
    bc2 = 1.0 - ADAM_B2 ** ADAM_STEP

    def body(g0_ref, g1_ref, w_ref, m_ref, v_ref, g_ref, d_ref, nm_ref, nv_ref):
        g = g0_ref[...] + g1_ref[...]
        g_ref[...] = g
        nm = ADAM_B1 * m_ref[...] + (1.0 - ADAM_B1) * g
        nv = ADAM_B2 * v_ref[...] + (1.0 - ADAM_B2) * (g * g)
        nm_ref[...] = nm
        nv_ref[...] = nv
        d_ref[...] = -ADAM_LR * ((nm / bc1) / (jnp.sqrt(nv / bc2) + ADAM_EPS) + ADAM_WD * w_ref[...])

    spec = pl.BlockSpec((tr, L), lambda i: (i, 0))
    out = jax.ShapeDtypeStruct((R, L), F32)
    return pl.pallas_call(
        body, name="adamw", grid=(R // tr,), in_specs=[spec] * 5, out_specs=[spec] * 4, out_shape=[out] * 4,
        compiler_params=_cparams(("parallel",)),
    )(g0, g1, w, m, v)


FLAT_ALIGN = 16 * LANES


def _flatten(parts, dtype):
    flat = jnp.concatenate([p.reshape(-1).astype(dtype) for p in parts])
    n = flat.shape[0]
    npad = -(-n // (FLAT_ALIGN * 16)) * (FLAT_ALIGN * 16)
    return jnp.pad(flat, (0, npad - n)).reshape(npad // LANES, LANES)


def _unflatten(flat, shapes):
    flat = flat.reshape(-1)
    out, pos = [], 0
    for s in shapes:
        n = int(np.prod(s))
        out.append(flat[pos:pos + n].reshape(s))
        pos += n
    return out


def _shard(a, axis, i):
    n = a.shape[axis] // N_CHIPS
    return lax.slice_in_dim(a, i * n, (i + 1) * n, axis=axis)


def kernel(x, norm_mix_pre, w_in, q_lat_norm, w_uq, kv_lat_norm, w_ukv, out_norm_a, out_norm_b, w_o, norm_mix_post, norm_ffn_pre, w_up, conv_w, conv_b, w_down, norm_ffn_post, loss_target, m_norm_mix_pre, m_w_in, m_q_lat_norm, m_w_uq, m_kv_lat_norm, m_w_ukv, m_out_norm_a, m_out_norm_b, m_w_o, m_norm_mix_post, m_norm_ffn_pre, m_w_up, m_conv_w, m_conv_b, m_w_down, m_norm_ffn_post, v_norm_mix_pre, v_w_in, v_q_lat_norm, v_w_uq, v_kv_lat_norm, v_w_ukv, v_out_norm_a, v_out_norm_b, v_w_o, v_norm_mix_post, v_norm_ffn_pre, v_w_up, v_conv_w, v_conv_b, v_w_down, v_norm_ffn_post):
    args = dict(locals())
    strip = lambda a: a[0] if a.ndim == 3 else a
    wl = {n: strip(args[n]) for n in WEIGHTS}
    ml = {n: strip(args['m_' + n]) for n in WEIGHTS}
    vl = {n: strip(args['v_' + n]) for n in WEIGHTS}

    big_shapes = [wl[n].shape for n in BIG]
    gathered_big, gathered_cw = _all_gather([_flatten([wl[n] for n in BIG], BF16), _flatten([wl['conv_w']], F32)])
    full = {n: wl[n] for n in SMALL}
    per_chip = [_unflatten(gathered_big[i], big_shapes) for i in range(N_CHIPS)]
    for k, n in enumerate(BIG):
        full[n] = jnp.concatenate([per_chip[i][k] for i in range(N_CHIPS)], axis=SHARD_AXIS[n])
    full['conv_w'] = jnp.concatenate(
        [_unflatten(gathered_cw[i], [wl['conv_w'].shape])[0] for i in range(N_CHIPS)], axis=1)

    loss_b, grad_x, g = _local_step(x[0], loss_target[0], full)

    sharded = [n for n in WEIGHTS if SHARD_AXIS[n] is not None]
    slots = []
    for i in range(N_CHIPS):
        slots.append(_flatten([_shard(g[n], SHARD_AXIS[n], i) for n in sharded] + [g[n] for n in SMALL], F32))
    recv = _scatter_grads(jnp.stack(slots))
    part = _sum_slots(recv)
    other = _swap_sibling(part)

    order = sharded + SMALL
    shapes = [wl[n].shape for n in order]
    g_f, d_f, nm_f, nv_f = _adamw(part, other, _flatten([wl[n] for n in order], F32),
                                  _flatten([ml[n] for n in order], F32), _flatten([vl[n] for n in order], F32))
    outs = {}
    for tag, flat in (('grad', g_f), ('delta', d_f), ('new_m', nm_f), ('new_v', nv_f)):
        for n, a in zip(order, _unflatten(flat, shapes)):
            outs[tag + '_' + n] = a.reshape(args[n].shape)

    loss = lax.psum(loss_b[0, 0], ("x", "y", "c"))
    return (loss, grad_x[None], *[outs['grad_' + n] for n in WEIGHTS], *[outs['delta_' + n] for n in WEIGHTS],
            *[outs['new_m_' + n] for n in WEIGHTS], *[outs['new_v_' + n] for n in WEIGHTS])
```

```python
import functools
import math

import jax
import jax.numpy as jnp
import numpy as np
from jax import lax
from jax.experimental import pallas as pl
from jax.experimental.pallas import tpu as pltpu

F32 = jnp.float32
BF16 = jnp.bfloat16

LANES = 128
D_MODEL = 1024
N_HEADS = 8
HEAD_DIM = 64
QK_ROPE = 32
Q_LORA = 384
KV_LORA = 256
D_FF = 2816
WIDTH = N_HEADS * HEAD_DIM
WIDTH_P = N_HEADS * LANES
IN_SIZES = (WIDTH, WIDTH, WIDTH, Q_LORA, KV_LORA, QK_ROPE)
D_IN = sum(IN_SIZES)
TAIL_P = Q_LORA + KV_LORA + LANES
EPS = 1e-6
ROPE_BASE = 10000.0
MASKED = -2e30
M_INIT = -1e30
MLA_TQ = 2048
MLA_TK = 256
MLA_SCALE = (HEAD_DIM + QK_ROPE) ** -0.5
DIL_SCALE = HEAD_DIM ** -0.5

ADAM_LR = 0.001
ADAM_B1 = 0.9
ADAM_B2 = 0.999
ADAM_EPS = 1e-08
ADAM_WD = 0.01
ADAM_STEP = 10

VMEM_LIMIT = 56 * 1024 * 1024

N_CHIPS = 4

WEIGHTS = ['norm_mix_pre', 'w_in', 'q_lat_norm', 'w_uq', 'kv_lat_norm', 'w_ukv', 'out_norm_a', 'out_norm_b',
           'w_o', 'norm_mix_post', 'norm_ffn_pre', 'w_up', 'conv_w', 'conv_b', 'w_down', 'norm_ffn_post']
SHARD_AXIS = {'norm_mix_pre': None, 'w_in': 1, 'q_lat_norm': None, 'w_uq': 1, 'kv_lat_norm': None, 'w_ukv': 1,
              'out_norm_a': None, 'out_norm_b': None, 'w_o': 0, 'norm_mix_post': None, 'norm_ffn_pre': None,
              'w_up': 1, 'conv_w': 1, 'conv_b': None, 'w_down': 0, 'norm_ffn_post': None}
BIG = ['w_in', 'w_uq', 'w_ukv', 'w_o', 'w_up', 'w_down']
SMALL = [n for n in WEIGHTS if SHARD_AXIS[n] is None]


def _tile(dim, target):
    best = None
    t = LANES
    while t <= min(dim, target):
        if dim % t == 0:
            best = t
        t += LANES
    return best if best is not None else dim


def _cparams(sem=None):
    return pltpu.CompilerParams(dimension_semantics=sem, vmem_limit_bytes=VMEM_LIMIT)


def _mm(a, b, mode, out_dtype, name, add=None, tm=1024, tn=1024, tk=1024, sharded=False):
    if mode == 'nn':
        (M, K), (K2, N) = a.shape, ((b.shape[1], N_CHIPS * b.shape[2]) if sharded else b.shape)
        dims = (((1,), (0,)), ((), ()))
    elif mode == 'nt':
        (M, K), (N, K2) = a.shape, ((b.shape[1], N_CHIPS * b.shape[2]) if sharded else b.shape)
        dims = (((1,), (1,)), ((), ()))
    else:
        (K, M), (K2, N) = a.shape, b.shape
        dims = (((0,), (0,)), ((), ()))
    assert K == K2, (a.shape, b.shape, mode)
    tm, tn, tk = _tile(M, tm), _tile(N, tn), _tile(K, tk)
    if sharded and mode == 'nt':
        tk = K // N_CHIPS
    elif sharded:
        tn = N // N_CHIPS
    nk = K // tk
    if mode == 'nn':
        a_spec = pl.BlockSpec((tm, tk), lambda i, j, k: (i, k))
        b_spec = (pl.BlockSpec((None, tk, tn), lambda i, j, k: (j, k, 0)) if sharded
                  else pl.BlockSpec((tk, tn), lambda i, j, k: (k, j)))
    elif mode == 'nt':
        a_spec = pl.BlockSpec((tm, tk), lambda i, j, k: (i, k))
        b_spec = (pl.BlockSpec((None, tn, tk), lambda i, j, k: (k, j, 0)) if sharded
                  else pl.BlockSpec((tn, tk), lambda i, j, k: (j, k)))
    else:
        a_spec = pl.BlockSpec((tk, tm), lambda i, j, k: (k, i))
        b_spec = pl.BlockSpec((tk, tn), lambda i, j, k: (k, j))
    o_spec = pl.BlockSpec((tm, tn), lambda i, j, k: (i, j))
    out_shape = jax.ShapeDtypeStruct((M, N), out_dtype)
    if sharded and mode == 'tn':
        o_spec = pl.BlockSpec((None, tm, tn), lambda i, j, k: (j, i, 0))
        out_shape = jax.ShapeDtypeStruct((N_CHIPS, M, tn), out_dtype)
    has_add = add is not None

    def body(*refs):
        if has_add:
            a_ref, b_ref, add_ref, o_ref, acc_ref = refs
        else:
            a_ref, b_ref, o_ref, acc_ref = refs
        k = pl.program_id(2)

        @pl.when(k == 0)
        def _():
            acc_ref[...] = jnp.zeros_like(acc_ref)

        acc_ref[...] += lax.dot_general(a_ref[...].astype(BF16), b_ref[...].astype(BF16), dims,
                                        preferred_element_type=F32)

        @pl.when(k == nk - 1)
        def _():
            r = acc_ref[...]
            if has_add:
                r = r + add_ref[...]
            o_ref[...] = r.astype(o_ref.dtype)

    ins = [a, b] + ([add] if has_add else [])
    in_specs = [a_spec, b_spec] + ([o_spec] if has_add else [])
    return pl.pallas_call(
        body, name=name, grid=(M // tm, N // tn, nk), in_specs=in_specs, out_specs=o_spec, out_shape=out_shape,
        scratch_shapes=[pltpu.VMEM((tm, tn), F32)],
        compiler_params=_cparams(("parallel", "parallel", "arbitrary")),
    )(*ins)


def _rows(body, name, S, ts, row_ins, full_ins, row_outs, acc_outs=(), chunk_outs=()):
    in_specs = [pl.BlockSpec((ts, a.shape[1]), lambda i: (i, 0)) for a in row_ins]
    in_specs += [pl.BlockSpec(a.shape, lambda i, nd=a.ndim: (0,) * nd) for a in full_ins]
    out_specs = [pl.BlockSpec((ts, w), lambda i: (i, 0)) for (w, _) in row_outs]
    out_specs += [pl.BlockSpec(shape, lambda i, nd=len(shape): (0,) * nd) for (shape, _) in acc_outs]
    out_specs += [pl.BlockSpec((lead, 1, LANES, ts), lambda i: (0, i, 0, 0)) for (lead, _) in chunk_outs]
    out_shape = [jax.ShapeDtypeStruct((S, w), dt) for (w, dt) in row_outs]
    out_shape += [jax.ShapeDtypeStruct(shape, dt) for (shape, dt) in acc_outs]
    out_shape += [jax.ShapeDtypeStruct((lead, S // ts, LANES, ts), dt) for (lead, dt) in chunk_outs]

    def kbody(*refs):
        body(pl.program_id(0), *refs)

    return pl.pallas_call(
        kbody, name=name, grid=(S // ts,), in_specs=in_specs, out_specs=out_specs, out_shape=out_shape,
        compiler_params=_cparams(("arbitrary",)),
    )(*row_ins, *full_ins)


def _acc_add(step, ref, val):
    @pl.when(step == 0)
    def _():
        ref[...] = val

    @pl.when(step != 0)
    def _():
        ref[...] += val


def _rms_fwd(x, g, name):
    S, W = x.shape

    def body(step, x_ref, g_ref, h_ref):
        xv = x_ref[...]
        r = lax.rsqrt(jnp.mean(xv * xv, axis=-1, keepdims=True) + EPS)
        h_ref[...] = (xv * r * g_ref[...]).astype(BF16)

    return _rows(body, name, S, 512, [x], [g], [(W, BF16)])[0]


def _rms_bwd_math(xv, g, dy, width):
    r = lax.rsqrt(jnp.sum(xv * xv, axis=-1, keepdims=True) * (1.0 / width) + EPS)
    xn = xv * r
    dyg = dy * g
    dx = r * (dyg - xn * (jnp.sum(dyg * xn, axis=-1, keepdims=True) * (1.0 / width)))
    return dx, dy * xn


def _rms_bwd(x, g, dys, resid, out_dtype, name):
    S, W = x.shape
    nd = len(dys)
    has_res = resid is not None

    def body(step, *refs):
        x_ref = refs[0]
        dy_refs = refs[1:1 + nd]
        pos = 1 + nd
        res_ref = refs[pos] if has_res else None
        pos += int(has_res)
        g_ref, dx_ref, dg_ref = refs[pos], refs[pos + 1], refs[pos + 2]
        dy = dy_refs[0][...].astype(F32)
        for r_ in dy_refs[1:]:
            dy = dy + r_[...].astype(F32)
        dx, dgr = _rms_bwd_math(x_ref[...], g_ref[...], dy, W)
        if has_res:
            dx = dx + res_ref[...]
        dx_ref[...] = dx.astype(dx_ref.dtype)
        _acc_add(step, dg_ref, jnp.sum(dgr, axis=0, keepdims=True))

    row_ins = [x] + list(dys) + ([resid] if has_res else [])
    dx, dg = _rows(body, name, S, 256, row_ins, [g], [(W, out_dtype)], [((1, W), F32)])
    return dx, dg


def _rope_apply(xv, c, sa, sb):
    return xv * c + pltpu.roll(xv, 16, 1) * sa + pltpu.roll(xv, LANES - 16, 1) * sb


def _rope_transpose(dy, c, sa, sb):
    return dy * c + pltpu.roll(dy * sa, LANES - 16, 1) + pltpu.roll(dy * sb, 16, 1)


def _rope_tables(S):
    pos = jnp.arange(S, dtype=F32)
    inv_freq = jnp.exp(-math.log(ROPE_BASE) * jnp.arange(0, QK_ROPE, 2, dtype=F32) / QK_ROPE)
    ang = pos[:, None] * inv_freq[None, :]
    cos, sin = jnp.cos(ang), jnp.sin(ang)
    ones, zeros = jnp.ones((S, HEAD_DIM), F32), jnp.zeros((S, HEAD_DIM), F32)
    z16, z32 = jnp.zeros((S, 16), F32), jnp.zeros((S, 32), F32)
    c = jnp.concatenate([ones, cos, cos, z32], axis=1)
    sa = jnp.concatenate([zeros, z16, sin, z32], axis=1)
    sb = jnp.concatenate([zeros, -sin, z16, z32], axis=1)
    return c, sa, sb


def _mla_prep(proj_b, g_q, g_kv, tabs):
    S = proj_b.shape[0]

    def body(step, p_ref, c_ref, sa_ref, sb_ref, gq_ref, gkv_ref, cq_ref, ckv_ref, kr_ref):
        cq = p_ref[:, 0:Q_LORA]
        ckv = p_ref[:, Q_LORA:Q_LORA + KV_LORA]
        kr = p_ref[:, Q_LORA + KV_LORA:TAIL_P]
        rq = lax.rsqrt(jnp.mean(cq * cq, axis=-1, keepdims=True) + EPS)
        cq_ref[...] = (cq * rq * gq_ref[...]).astype(BF16)
        rk = lax.rsqrt(jnp.mean(ckv * ckv, axis=-1, keepdims=True) + EPS)
        ckv_ref[...] = (ckv * rk * gkv_ref[...]).astype(BF16)
        kr_ref[...] = _rope_apply(kr, c_ref[...], sa_ref[...], sb_ref[...])

    return _rows(body, "mla_prep", S, 512, [proj_b, *tabs], [g_q, g_kv],
                 [(Q_LORA, BF16), (KV_LORA, BF16), (LANES, F32)])


def _mla_qkv(q, kv, kr, tabs):
    S = q.shape[0]

    def body(step, q_ref, kv_ref, kr_ref, c_ref, sa_ref, sb_ref, qb_ref, kb_ref, vb_ref, kt_ref, vt_ref):
        c, sa, sb = c_ref[...], sa_ref[...], sb_ref[...]
        krv = kr_ref[...]
        row = lax.broadcasted_iota(jnp.int32, (LANES, MLA_TK), 0)
        for h in range(N_HEADS):
            blk = slice(h * LANES, (h + 1) * LANES)
            qb_ref[:, blk] = (_rope_apply(q_ref[:, blk], c, sa, sb) * MLA_SCALE).astype(BF16)
            kh = kv_ref[:, blk] + krv
            kb_ref[:, blk] = kh.astype(BF16)
            kt_ref[h, 0] = kh.T.astype(BF16)
            vh = kv_ref[:, WIDTH_P + h * LANES:WIDTH_P + (h + 1) * LANES]
            vt_ref[h, 0] = jnp.where(row == HEAD_DIM, 1.0, vh.T).astype(BF16)
        vb_ref[...] = kv_ref[:, WIDTH_P:2 * WIDTH_P].astype(BF16)

    return _rows(body, "mla_qkv", S, MLA_TK, [q, kv, kr, *tabs], [],
                 [(WIDTH_P, BF16), (WIDTH_P, BF16), (WIDTH_P, BF16)],
                 chunk_outs=[(N_HEADS, BF16), (N_HEADS, BF16)])


def _outnorm_fwd(oa, ob, ga, gb):
    S = oa.shape[0]

    def body(step, oa_ref, ob_ref, ga_ref, gb_ref, cat_ref):
        live = lax.broadcasted_iota(jnp.int32, oa_ref.shape, 1) % LANES < HEAD_DIM
        for o_ref, g_ref, off in ((oa_ref, ga_ref, 0), (ob_ref, gb_ref, WIDTH_P)):
            o = jnp.where(live, o_ref[...], 0.0)
            r = lax.rsqrt(jnp.sum(o * o, axis=-1, keepdims=True) * (1.0 / WIDTH) + EPS)
            cat_ref[:, off:off + WIDTH_P] = (o * r * g_ref[...]).astype(BF16)

    return _rows(body, "outnorm_fwd", S, 256, [oa, ob], [ga, gb], [(2 * WIDTH_P, BF16)])[0]


def _outnorm_bwd(oa, ob, ga, gb, dcat):
    S = oa.shape[0]

    def body(step, oa_ref, ob_ref, dcat_ref, ga_ref, gb_ref, dpa_ref, dob_ref, dga_ref, dgb_ref):
        live = lax.broadcasted_iota(jnp.int32, oa_ref.shape, 1) % LANES < HEAD_DIM
        lane = lax.broadcasted_iota(jnp.int32, (oa_ref.shape[0], LANES), 1)
        packed = oa_ref[...]
        o = jnp.where(live, packed, 0.0)
        do, dgr = _rms_bwd_math(o, ga_ref[...], dcat_ref[:, 0:WIDTH_P], WIDTH)
        _acc_add(step, dga_ref, jnp.sum(dgr, axis=0, keepdims=True))
        prod = do.astype(BF16).astype(F32) * o
        for h in range(N_HEADS):
            blk = slice(h * LANES, (h + 1) * LANES)
            delta = jnp.sum(prod[:, blk], axis=-1, keepdims=True)
            lse = jnp.sum(jnp.where(lane == HEAD_DIM, packed[:, blk], 0.0), axis=-1, keepdims=True)
            out = do[:, blk]
            for k, piece in enumerate(_split3(-delta) + _split3(-lse)):
                out = jnp.where(lane == HEAD_DIM + k, piece, out)
            dpa_ref[:, blk] = out

        ov = ob_ref[...]
        do_b, dgr_b = _rms_bwd_math(ov, gb_ref[...], dcat_ref[:, WIDTH_P:2 * WIDTH_P], WIDTH)
        dob_ref[...] = do_b.astype(BF16)
        _acc_add(step, dgb_ref, jnp.sum(dgr_b, axis=0, keepdims=True))

    return _rows(body, "outnorm_bwd", S, 256, [oa, ob, dcat], [ga, gb],
                 [(WIDTH_P, F32), (WIDTH_P, BF16)], [((1, WIDTH_P), F32), ((1, WIDTH_P), F32)])


def _post_mix(x, y, g_post, g_pre):
    S, W = x.shape

    def body(step, x_ref, y_ref, gp_ref, gq_ref, x1_ref, h_ref):
        yv = y_ref[...]
        r = lax.rsqrt(jnp.mean(yv * yv, axis=-1, keepdims=True) + EPS)
        x1 = x_ref[...] + yv * r * gp_ref[...]
        x1_ref[...] = x1
        r1 = lax.rsqrt(jnp.mean(x1 * x1, axis=-1, keepdims=True) + EPS)
        h_ref[...] = (x1 * r1 * gq_ref[...]).astype(BF16)

    return _rows(body, "post_mix", S, 512, [x, y], [g_post, g_pre], [(W, F32), (W, BF16)])


def _final(x1, y2, g, target):
    S, W = x1.shape
    nsteps = S // 256

    def body(step, x1_ref, y_ref, t_ref, g_ref, dx2_ref, dy_ref, dg_ref, sq_ref, loss_ref):
        yv = y_ref[...]
        gv = g_ref[...]
        r = lax.rsqrt(jnp.mean(yv * yv, axis=-1, keepdims=True) + EPS)
        yn = yv * r
        err = (x1_ref[...] + yn * gv) - t_ref[...]
        dx2 = err * (1.0 / W)
        dx2_ref[...] = dx2
        dyg = dx2 * gv
        dy = r * (dyg - yn * jnp.mean(dyg * yn, axis=-1, keepdims=True))
        dy_ref[...] = dy.astype(BF16)
        _acc_add(step, dg_ref, jnp.sum(dx2 * yn, axis=0, keepdims=True))
        _acc_add(step, sq_ref, jnp.sum(err * err, axis=0, keepdims=True))

        @pl.when(step == nsteps - 1)
        def _():
            tot = jnp.sum(sq_ref[...], axis=-1, keepdims=True) * (0.5 / W)
            loss_ref[...] = jnp.broadcast_to(tot, (1, LANES))

    return _rows(body, "final_loss", S, 256, [x1, y2, target], [g], [(W, F32), (W, BF16)],
                 [((1, W), F32), ((1, W), F32), ((1, LANES), F32)])


_GELU_C = math.sqrt(2.0 / math.pi)
_CONV_CHUNK = 512
_HALO = 8


def _gelu(g):
    t = jnp.tanh(_GELU_C * (g + 0.044715 * (g * g * g)))
    return g * (0.5 * (1.0 + t)), t


def _fill_padded(pad_ref, src_ref, S):
    zeros = jnp.zeros((_HALO, LANES), F32)
    pad_ref[0:_HALO, :] = zeros
    pad_ref[_HALO + S:2 * _HALO + S, :] = zeros
    for r0 in range(0, S, _CONV_CHUNK):
        pad_ref[_HALO + r0:_HALO + r0 + _CONV_CHUNK, :] = src_ref[r0:r0 + _CONV_CHUNK, :].astype(F32)


def _conv_fwd(u0, conv_w, conv_b):
    S, C2 = u0.shape
    nb = D_FF // LANES

    def body(u0g_ref, u0v_ref, wg_ref, wv_ref, bg_ref, bv_ref, ug_ref, uv_ref, a_ref, pg_ref, pv_ref):
        _fill_padded(pg_ref, u0g_ref, S)
        _fill_padded(pv_ref, u0v_ref, S)
        wg, wv = wg_ref[...], wv_ref[...]
        for r0 in range(0, S, _CONV_CHUNK):
            def conv(p_ref, w, b_ref):
                base = _HALO + r0
                return (p_ref[base - 1:base - 1 + _CONV_CHUNK, :] * w[0:1, :]
                        + p_ref[base:base + _CONV_CHUNK, :] * w[1:2, :]
                        + p_ref[base + 1:base + 1 + _CONV_CHUNK, :] * w[2:3, :] + b_ref[...])
            g = conv(pg_ref, wg, bg_ref)
            v = conv(pv_ref, wv, bv_ref)
            rows = slice(r0, r0 + _CONV_CHUNK)
            ug_ref[rows, :] = g
            uv_ref[rows, :] = v
            a_ref[rows, :] = (_gelu(g)[0] * v).astype(BF16)

    col = lambda off: pl.BlockSpec((S, LANES), lambda j: (0, j + off))
    wcol = lambda off: pl.BlockSpec((3, LANES), lambda j: (0, j + off))
    bcol = lambda off: pl.BlockSpec((1, LANES), lambda j: (0, j + off))
    ug, uv, a = pl.pallas_call(
        body, name="conv_gelu_fwd", grid=(nb,),
        in_specs=[col(0), col(nb), wcol(0), wcol(nb), bcol(0), bcol(nb)],
        out_specs=[col(0), col(0), col(0)],
        out_shape=[jax.ShapeDtypeStruct((S, D_FF), F32), jax.ShapeDtypeStruct((S, D_FF), F32),
                   jax.ShapeDtypeStruct((S, D_FF), BF16)],
        scratch_shapes=[pltpu.VMEM((S + 2 * _HALO, LANES), F32), pltpu.VMEM((S + 2 * _HALO, LANES), F32)],
        compiler_params=_cparams(("arbitrary",)),
    )(u0, u0, conv_w, conv_w, conv_b, conv_b)
    return ug, uv, a


def _conv_bwd(u0, ug, uv, da, conv_w):
    S = u0.shape[0]
    nb = D_FF // LANES

    def body(u0_ref, ug_ref, uv_ref, da_ref, w_ref, du0_ref, dw_ref, db_ref, pu_ref, pd_ref):
        is_g = pl.program_id(0) < nb
        _fill_padded(pu_ref, u0_ref, S)
        zeros = jnp.zeros((_HALO, LANES), F32)
        pd_ref[0:_HALO, :] = zeros
        pd_ref[_HALO + S:2 * _HALO + S, :] = zeros
        sel = jnp.where(is_g, 1.0, 0.0).astype(F32)
        for r0 in range(0, S, _CONV_CHUNK):
            rows = slice(r0, r0 + _CONV_CHUNK)
            g, v, d = ug_ref[rows, :], uv_ref[rows, :], da_ref[rows, :]
            gel, t = _gelu(g)
            dgel = 0.5 * (1.0 + t) + (0.5 * g) * (1.0 - t * t) * (_GELU_C * (1.0 + 3.0 * 0.044715 * (g * g)))
            du = d * (sel * (v * dgel) + (1.0 - sel) * gel)
            pd_ref[_HALO + r0:_HALO + r0 + _CONV_CHUNK, :] = du
        w = w_ref[...]
        acc_b = jnp.zeros((1, LANES), F32)
        acc_w = [jnp.zeros((1, LANES), F32) for _ in range(3)]
        for r0 in range(0, S, _CONV_CHUNK):
            base = _HALO + r0
            du_m = pd_ref[base - 1:base - 1 + _CONV_CHUNK, :]
            du_c = pd_ref[base:base + _CONV_CHUNK, :]
            du_p = pd_ref[base + 1:base + 1 + _CONV_CHUNK, :]
            du0_ref[r0:r0 + _CONV_CHUNK, :] = (du_p * w[0:1, :] + du_c * w[1:2, :] + du_m * w[2:3, :]).astype(BF16)
            acc_b = acc_b + jnp.sum(du_c, axis=0, keepdims=True)
            for k in range(3):
                acc_w[k] = acc_w[k] + jnp.sum(du_c * pu_ref[base + k - 1:base + k - 1 + _CONV_CHUNK, :],
                                              axis=0, keepdims=True)
        db_ref[...] = acc_b
        for k in range(3):
            dw_ref[k:k + 1, :] = acc_w[k]

    own = pl.BlockSpec((S, LANES), lambda j: (0, j))
    half = lambda off: pl.BlockSpec((S, LANES), lambda j: (0, j % nb))
    du0, dw, db = pl.pallas_call(
        body, name="conv_gelu_bwd", grid=(2 * nb,),
        in_specs=[own, half(0), half(0), half(0), pl.BlockSpec((3, LANES), lambda j: (0, j))],
        out_specs=[own, pl.BlockSpec((3, LANES), lambda j: (0, j)), pl.BlockSpec((1, LANES), lambda j: (0, j))],
        out_shape=[jax.ShapeDtypeStruct((S, 2 * D_FF), BF16), jax.ShapeDtypeStruct((3, 2 * D_FF), F32),
                   jax.ShapeDtypeStruct((1, 2 * D_FF), F32)],
        scratch_shapes=[pltpu.VMEM((S + 2 * _HALO, LANES), F32), pltpu.VMEM((S + 2 * _HALO, LANES), F32)],
        compiler_params=_cparams(("arbitrary",)),
    )(u0, ug, uv, da, conv_w)
    return du0, dw, db


DIL_Q = 128
DIL_HALF = 64
DIL_SLAB = DIL_Q + 2 * DIL_HALF
DILATIONS = (1, 4, 16)
DIL_SEG = 2048


def _dil_bias(r):
    row = jnp.arange(DIL_Q, dtype=jnp.int32)[:, None]
    col = jnp.arange(DIL_SLAB, dtype=jnp.int32)[None, :]
    ad = jnp.abs(col - DIL_HALF - row)
    slopes = jnp.exp2(-8.0 * jnp.arange(1, N_HEADS + 1, dtype=F32) / N_HEADS)
    base = jnp.where(ad <= DIL_HALF, -slopes[:, None, None] * (ad * r).astype(F32)[None], MASKED)
    before = jnp.broadcast_to(col < DIL_HALF, (DIL_Q, DIL_SLAB))
    after = jnp.broadcast_to(col >= DIL_Q + DIL_HALF, (DIL_Q, DIL_SLAB))
    variants = [base, jnp.where(before, MASKED, base), jnp.where(after, MASKED, base),
                jnp.where(before | after, MASKED, base)]
    return jnp.stack(variants, axis=1)


def _lanes_hi_to_all(x):
    lane = lax.broadcasted_iota(jnp.int32, x.shape, 1)
    return jnp.where(lane < HEAD_DIM, pltpu.roll(x, HEAD_DIM, 1), x)


def _fill_kv(kp_ref, vp_ref, k_ref, v_ref, L, ones):
    zeros = jnp.zeros((DIL_HALF, LANES), BF16)
    for ref in (kp_ref, vp_ref):
        ref[0:DIL_HALF, :] = zeros
        ref[DIL_HALF + L:2 * DIL_HALF + L, :] = zeros
    step = min(L, 512)
    lane = lax.broadcasted_iota(jnp.int32, (step, LANES), 1)
    for r0 in range(0, L, step):
        kp_ref[DIL_HALF + r0:DIL_HALF + r0 + step, :] = k_ref[r0:r0 + step, :]
        vv = v_ref[r0:r0 + step, :]
        vp_ref[DIL_HALF + r0:DIL_HALF + r0 + step, :] = jnp.where(lane < HEAD_DIM, vv, 1.0).astype(BF16) if ones else vv


def _dil_fwd(proj_a, bias, r):
    S = proj_a.shape[0]
    L = S // r
    nblk = L // DIL_Q
    pv = proj_a.reshape(L, r * 3 * WIDTH_P)

    def body(q_ref, k_ref, v_ref, b_ref, o_ref, kp_ref, vp_ref):
        _fill_kv(kp_ref, vp_ref, k_ref, v_ref, L, True)
        lane = lax.broadcasted_iota(jnp.int32, (DIL_Q, LANES), 1)

        def block(i, carry):
            rows = pl.ds(pl.multiple_of(i * DIL_Q, DIL_Q), DIL_Q)
            slab = pl.ds(pl.multiple_of(i * DIL_Q, DIL_Q), DIL_SLAB)
            variant = jnp.where(i == 0, 1, 0) + jnp.where(i == nblk - 1, 2, 0)
            qv = q_ref[rows, :] * DIL_SCALE
            s = lax.dot_general(qv, kp_ref[slab, :], _NT, preferred_element_type=F32) + b_ref[0, variant]
            m = jnp.max(s, axis=-1, keepdims=True)
            acc = jnp.dot(jnp.exp(s - m).astype(BF16), vp_ref[slab, :], preferred_element_type=F32)
            l = _lanes_hi_to_all(acc)
            o_ref[rows, :] = jnp.where(lane < HEAD_DIM, acc / l, m + jnp.log(l))
            return carry

        lax.fori_loop(0, nblk, block, 0, unroll=min(4, nblk))

    col = lambda part: pl.BlockSpec((L, LANES), lambda c, h: (0, c * 3 * N_HEADS + part * N_HEADS + h))
    out = pl.pallas_call(
        body, name="dil_fwd_r%d" % r, grid=(r, N_HEADS),
        in_specs=[col(0), col(1), col(2), pl.BlockSpec((1, 4, DIL_Q, DIL_SLAB), lambda c, h: (h, 0, 0, 0))],
        out_specs=pl.BlockSpec((L, LANES), lambda c, h: (0, c * N_HEADS + h)),
        out_shape=jax.ShapeDtypeStruct((L, r * WIDTH_P), F32),
        scratch_shapes=[pltpu.VMEM((L + 2 * DIL_HALF, LANES), BF16), pltpu.VMEM((L + 2 * DIL_HALF, LANES), BF16)],
        compiler_params=_cparams(("parallel", "parallel")),
    )(pv, pv, pv, bias)
    return out.reshape(S, WIDTH_P)


def _dil_combine(branches):
    S = branches[0].shape[0]

    def body(step, *refs):
        o_ref, lse_ref = refs[-2], refs[-1]
        for h in range(N_HEADS):
            blk = slice(h * LANES, (h + 1) * LANES)
            xs = [r_[:, blk] for r_ in refs[:-2]]
            lses = [_lanes_hi_to_all(x) for x in xs]
            m = functools.reduce(jnp.maximum, lses)
            ws = [jnp.exp(l - m) for l in lses]
            tot = functools.reduce(jnp.add, ws)
            lane = lax.broadcasted_iota(jnp.int32, xs[0].shape, 1)
            o = functools.reduce(jnp.add, [w * x for w, x in zip(ws, xs)]) / tot
            o_ref[:, blk] = jnp.where(lane < HEAD_DIM, o, 0.0)
            lse_ref[:, blk] = m + jnp.log(tot)

    return _rows(body, "dil_combine", S, 256, list(branches), [], [(WIDTH_P, F32), (WIDTH_P, F32)])


def _dil_bwd(proj_a, do, lse, delta, bias, r):
    S = proj_a.shape[0]
    L = S // r
    seg = min(L, DIL_SEG)
    nseg, nblk, nblk_seg = L // seg, L // DIL_Q, seg // DIL_Q
    pv = proj_a.reshape(L, r * 3 * WIDTH_P)
    view = lambda a: a.reshape(L, r * WIDTH_P)
    _TN = (((0,), (0,)), ((), ()))

    def body(q_ref, k_ref, v_ref, do_ref, lse_ref, dl_ref, b_ref, dq_ref, dk_ref, dv_ref,
             kp_ref, vp_ref, dkp_ref, dvp_ref):
        sg = pl.program_id(2)

        @pl.when(sg == 0)
        def _():
            _fill_kv(kp_ref, vp_ref, k_ref, v_ref, L, False)
            dkp_ref[...] = jnp.zeros_like(dkp_ref)
            dvp_ref[...] = jnp.zeros_like(dvp_ref)

        def block(j, carry):
            i = sg * nblk_seg + j
            rows = pl.ds(pl.multiple_of(j * DIL_Q, DIL_Q), DIL_Q)
            slab = pl.ds(pl.multiple_of(i * DIL_Q, DIL_Q), DIL_SLAB)
            variant = jnp.where(i == 0, 1, 0) + jnp.where(i == nblk - 1, 2, 0)
            qv = q_ref[rows, :] * DIL_SCALE
            dov = do_ref[rows, :]
            ks, vs = kp_ref[slab, :], vp_ref[slab, :]
            two = lambda a: jnp.concatenate([a, a], axis=1)
            s = lax.dot_general(qv, ks, _NT, preferred_element_type=F32) + b_ref[0, variant]
            p = jnp.exp(s - two(lse_ref[rows, :]))
            dp = lax.dot_general(dov, vs, _NT, preferred_element_type=F32)
            ds = (p * (dp - two(dl_ref[rows, :]))).astype(BF16)
            dq_ref[rows, :] = jnp.dot(ds, ks, preferred_element_type=F32) * DIL_SCALE
            dkp_ref[slab, :] += lax.dot_general(ds, qv, _TN, preferred_element_type=F32)
            dvp_ref[slab, :] += lax.dot_general(p.astype(BF16), dov, _TN, preferred_element_type=F32)
            return carry

        lax.fori_loop(0, nblk_seg, block, 0, unroll=min(2, nblk_seg))

        @pl.when(sg == nseg - 1)
        def _():
            dk_ref[...] = dkp_ref[DIL_HALF:DIL_HALF + L, :]
            dv_ref[...] = dvp_ref[DIL_HALF:DIL_HALF + L, :]

    col = lambda part: pl.BlockSpec((L, LANES), lambda c, h, s: (0, c * 3 * N_HEADS + part * N_HEADS + h))
    segspec = pl.BlockSpec((seg, LANES), lambda c, h, s: (s, c * N_HEADS + h))
    fullspec = pl.BlockSpec((L, LANES), lambda c, h, s: (0, c * N_HEADS + h))
    out = jax.ShapeDtypeStruct((L, r * WIDTH_P), F32)
    dq, dk, dv = pl.pallas_call(
        body, name="dil_bwd_r%d" % r, grid=(r, N_HEADS, nseg),
        in_specs=[pl.BlockSpec((seg, LANES), lambda c, h, s: (s, c * 3 * N_HEADS + h)), col(1), col(2),
                  segspec, segspec, segspec, pl.BlockSpec((1, 4, DIL_Q, DIL_SLAB), lambda c, h, s: (h, 0, 0, 0))],
        out_specs=[segspec, fullspec, fullspec], out_shape=[out, out, out],
        scratch_shapes=[pltpu.VMEM((L + 2 * DIL_HALF, LANES), BF16), pltpu.VMEM((L + 2 * DIL_HALF, LANES), BF16),
                        pltpu.VMEM((L + 2 * DIL_HALF, LANES), F32), pltpu.VMEM((L + 2 * DIL_HALF, LANES), F32)],
        compiler_params=_cparams(("arbitrary", "arbitrary", "arbitrary")),
    )(pv, pv, pv, view(do), view(lse), view(delta), bias)
    return [a.reshape(S, WIDTH_P) for a in (dq, dk, dv)]


def _dil_sum(grads):
    S = grads[0][0].shape[0]
    nb = len(grads)

    def body(step, *refs):
        out_ref = refs[-1]
        for part in range(3):
            tot = refs[part][...]
            for b in range(1, nb):
                tot = tot + refs[3 * b + part][...]
            out_ref[:, part * WIDTH_P:(part + 1) * WIDTH_P] = tot.astype(BF16)

    return _rows(body, "dil_sum", S, 256, [a for g in grads for a in g], [], [(3 * WIDTH_P, BF16)])[0]


_NT = (((1,), (1,)), ((), ()))
_TN = (((0,), (0,)), ((), ()))
HEAD_COLS = 3 * LANES


def _slab_bias():
    row = jnp.arange(DIL_Q, dtype=jnp.int32)[:, None]
    col = jnp.arange(DIL_SLAB, dtype=jnp.int32)[None, :]
    slopes = jnp.exp2(-8.0 * jnp.arange(1, N_HEADS + 1, dtype=F32) / N_HEADS)
    out = []
    for r in DILATIONS:
        variants = []
        for shift in (DIL_HALF, 0, DIL_Q):
            ad = jnp.abs(col - shift - row)
            variants.append(jnp.where(ad <= DIL_HALF, -slopes[:, None, None] * (ad * r).astype(F32)[None], MASKED))
        out.append(jnp.stack(variants, axis=1))
    return jnp.stack(out, axis=0)


DIL_CHUNK = 512


def _block_geometry(i, nblk):
    first = pl.multiple_of(i * DIL_Q, DIL_Q)
    slab0 = pl.multiple_of(jnp.clip(i * DIL_Q - DIL_HALF, 0, (nblk - 2) * DIL_Q), DIL_HALF)
    variant = jnp.where(i == 0, 1, jnp.where(i == nblk - 1, 2, 0))
    return first, slab0, variant


def _class_rows(c, r, r0, n):
    return pl.ds(c + r0 * r, n, stride=r) if r > 1 else pl.ds(r0, n)


def _dila_fwd(proj_a, bias):
    S = proj_a.shape[0]
    lmax = S // DILATIONS[1]

    def body(qh_ref, kh_ref, vh_ref, b_ref, o_ref, q_s, k_s, v_s, cm_s):
        lane = lax.broadcasted_iota(jnp.int32, (DIL_Q, LANES), 1)
        lane_s = lax.broadcasted_iota(jnp.int32, (DIL_SLAB, LANES), 1)
        lane_c = lax.broadcasted_iota(jnp.int32, (DIL_CHUNK, LANES), 1)

        def run(g, nblk, load_q, load_k, load_v, store):
            def block(i, carry):
                first, slab0, variant = _block_geometry(i, nblk)
                qv, ks, vs = load_q(first), load_k(slab0), load_v(slab0)
                s = lax.dot_general(qv, ks, _NT, preferred_element_type=F32) + b_ref[g, 0, variant]
                m = jnp.max(s, axis=-1, keepdims=True)
                acc = jnp.dot(jnp.exp(s - m).astype(BF16), vs, preferred_element_type=F32)
                l = _lanes_hi_to_all(acc)
                store(first, jnp.where(lane < HEAD_DIM, acc / l, m + jnp.log(l)))
                return carry

            lax.fori_loop(0, nblk, block, 0, unroll=4)

        def direct_store(first, val):
            o_ref[pl.ds(first, DIL_Q), :] = val

        run(0, S // DIL_Q,
            lambda f: (qh_ref[pl.ds(f, DIL_Q), :] * DIL_SCALE).astype(BF16),
            lambda s0: kh_ref[pl.ds(s0, DIL_SLAB), :].astype(BF16),
            lambda s0: jnp.where(lane_s < HEAD_DIM, vh_ref[pl.ds(s0, DIL_SLAB), :], 1.0).astype(BF16),
            direct_store)

        def cm_store(first, val):
            cm_s[pl.ds(first, DIL_Q), :] = val

        for g, r in list(enumerate(DILATIONS))[1:]:
            L = S // r
            n = min(L, DIL_CHUNK)
            for c in range(r):
                for r0 in range(0, L, n):
                    src = _class_rows(c, r, r0, n)
                    q_s[r0:r0 + n, :] = (qh_ref[src, :] * DIL_SCALE).astype(BF16)
                    k_s[r0:r0 + n, :] = kh_ref[src, :].astype(BF16)
                    v_s[r0:r0 + n, :] = jnp.where(lane_c[:n] < HEAD_DIM, vh_ref[src, :], 1.0).astype(BF16)
                run(g, L // DIL_Q, lambda f: q_s[pl.ds(f, DIL_Q), :], lambda s0: k_s[pl.ds(s0, DIL_SLAB), :],
                    lambda s0: v_s[pl.ds(s0, DIL_SLAB), :], cm_store)
                for r0 in range(0, L, n):
                    dst = _class_rows(c, r, r0, n)
                    a, b = cm_s[r0:r0 + n, :], o_ref[dst, :]
                    la, lb = _lanes_hi_to_all(a), _lanes_hi_to_all(b)
                    m = jnp.maximum(la, lb)
                    wa, wb = jnp.exp(la - m), jnp.exp(lb - m)
                    tot = wa + wb
                    o_ref[dst, :] = jnp.where(lane_c[:n] < HEAD_DIM, (wa * a + wb * b) / tot, m + jnp.log(tot))

    return pl.pallas_call(
        body, name="dil_fwd", grid=(N_HEADS,),
        in_specs=[pl.BlockSpec((S, LANES), lambda h: (0, 3 * h)), pl.BlockSpec((S, LANES), lambda h: (0, 3 * h + 1)),
                  pl.BlockSpec((S, LANES), lambda h: (0, 3 * h + 2)),
                  pl.BlockSpec((len(DILATIONS), 1, 3, DIL_Q, DIL_SLAB), lambda h: (0, h, 0, 0, 0))],
        out_specs=pl.BlockSpec((S, LANES), lambda h: (0, h)),
        out_shape=jax.ShapeDtypeStruct((S, WIDTH_P), F32),
        scratch_shapes=[pltpu.VMEM((lmax, LANES), BF16), pltpu.VMEM((lmax, LANES), BF16),
                        pltpu.VMEM((lmax, LANES), BF16), pltpu.VMEM((lmax, LANES), F32)],
        compiler_params=_cparams(("parallel",)),
    )(proj_a, proj_a, proj_a, bias)


N_SPLIT = 3


def _split3(x):
    hi = x.astype(BF16).astype(F32)
    mid = (x - hi).astype(BF16).astype(F32)
    lo = (x - hi - mid).astype(BF16).astype(F32)
    return hi, mid, lo


def _dila_bwd(proj_a, dopack, bias):
    S = proj_a.shape[0]
    lmax = S // DILATIONS[1]

    def body(qh_ref, kh_ref, vh_ref, d_ref, b_ref, out_ref, dq_ref, dk_ref, dv_ref, q_s, k_s, v_s, do_s,
             dq_c, dk_c, dv_c):
        def scalar_lanes(shape):
            lane = lax.broadcasted_iota(jnp.int32, shape, 1)
            return lane, (lane >= HEAD_DIM) & (lane < HEAD_DIM + N_SPLIT)

        def q_side(q, x):
            lane, ones = scalar_lanes(x.shape)
            lse_parts = pltpu.roll(x, LANES - N_SPLIT, 1)
            qv = jnp.where(lane < HEAD_DIM, q * DIL_SCALE, jnp.where(ones, lse_parts, 0.0)).astype(BF16)
            return qv, jnp.where(lane < HEAD_DIM + N_SPLIT, x, 0.0).astype(BF16)

        def kv_side(k, v):
            _, ones = scalar_lanes(k.shape)
            return jnp.where(ones, 1.0, k).astype(BF16), jnp.where(ones, 1.0, v).astype(BF16)

        def run(g, nblk, load_q, load_kv, dq_o, dk_o, dv_o):
            def block(i, carry):
                first, slab0, variant = _block_geometry(i, nblk)
                rows, slab = pl.ds(first, DIL_Q), pl.ds(slab0, DIL_SLAB)
                (qv, dov), (ks, vs) = load_q(rows), load_kv(slab)
                p = jnp.exp(lax.dot_general(qv, ks, _NT, preferred_element_type=F32) + b_ref[g, 0, variant])
                ds = (p * lax.dot_general(dov, vs, _NT, preferred_element_type=F32)).astype(BF16)
                dq_o[rows, :] = jnp.dot(ds, ks, preferred_element_type=F32) * DIL_SCALE
                dk_o[slab, :] += lax.dot_general(ds, qv, _TN, preferred_element_type=F32)
                dv_o[slab, :] += lax.dot_general(p.astype(BF16), dov, _TN, preferred_element_type=F32)
                return carry

            lax.fori_loop(0, nblk, block, 0, unroll=2)

        dk_ref[...] = jnp.zeros_like(dk_ref)
        dv_ref[...] = jnp.zeros_like(dv_ref)
        run(0, S // DIL_Q,
            lambda rows: q_side(qh_ref[rows, :], d_ref[rows, :]),
            lambda slab: kv_side(kh_ref[slab, :], vh_ref[slab, :]),
            dq_ref, dk_ref, dv_ref)

        for g, r in list(enumerate(DILATIONS))[1:]:
            L = S // r
            n = min(L, DIL_CHUNK)
            for c in range(r):
                for r0 in range(0, L, n):
                    src = _class_rows(c, r, r0, n)
                    q_s[r0:r0 + n, :], do_s[r0:r0 + n, :] = q_side(qh_ref[src, :], d_ref[src, :])
                    k_s[r0:r0 + n, :], v_s[r0:r0 + n, :] = kv_side(kh_ref[src, :], vh_ref[src, :])
                    dk_c[r0:r0 + n, :] = jnp.zeros((n, LANES), F32)
                    dv_c[r0:r0 + n, :] = jnp.zeros((n, LANES), F32)
                run(g, L // DIL_Q, lambda rows: (q_s[rows, :], do_s[rows, :]),
                    lambda slab: (k_s[slab, :], v_s[slab, :]), dq_c, dk_c, dv_c)
                for r0 in range(0, L, n):
                    dst = _class_rows(c, r, r0, n)
                    for acc, cls in ((dq_ref, dq_c), (dk_ref, dk_c), (dv_ref, dv_c)):
                        acc[dst, :] += cls[r0:r0 + n, :]

        for r0 in range(0, S, DIL_CHUNK):
            for part, ref in enumerate((dq_ref, dk_ref, dv_ref)):
                out_ref[r0:r0 + DIL_CHUNK, part * LANES:(part + 1) * LANES] = ref[r0:r0 + DIL_CHUNK, :].astype(BF16)

    bf = lambda rows: pltpu.VMEM((rows, LANES), BF16)
    f32 = lambda rows: pltpu.VMEM((rows, LANES), F32)
    return pl.pallas_call(
        body, name="dil_bwd", grid=(N_HEADS,),
        in_specs=[pl.BlockSpec((S, LANES), lambda h: (0, 3 * h), pipeline_mode=pl.Buffered(1)),
                  pl.BlockSpec((S, LANES), lambda h: (0, 3 * h + 1), pipeline_mode=pl.Buffered(1)),
                  pl.BlockSpec((S, LANES), lambda h: (0, 3 * h + 2), pipeline_mode=pl.Buffered(1)),
                  pl.BlockSpec((S, LANES), lambda h: (0, h), pipeline_mode=pl.Buffered(1)),
                  pl.BlockSpec((len(DILATIONS), 1, 3, DIL_Q, DIL_SLAB), lambda h: (0, h, 0, 0, 0))],
        out_specs=pl.BlockSpec((S, HEAD_COLS), lambda h: (0, h)),
        out_shape=jax.ShapeDtypeStruct((S, N_HEADS * HEAD_COLS), BF16),
        scratch_shapes=[f32(S), f32(S), f32(S), bf(lmax), bf(lmax), bf(lmax), bf(lmax),
                        f32(lmax), f32(lmax), f32(lmax)],
        compiler_params=_cparams(("arbitrary",)),
    )(proj_a, proj_a, proj_a, dopack, bias)


def _mla_fwd(q, k, vt):
    S = q.shape[0]
    tq, tk = MLA_TQ, MLA_TK
    nq, nk = S // tq, S // tk

    def body(q_ref, k_ref, vt_ref, o_ref, lse_ref, acc_ref):
        qv = q_ref[...]
        acc_ref[...] = jnp.zeros_like(acc_ref)

        def chunk(c, m):
            kc = k_ref[pl.ds(pl.multiple_of(c * tk, tk), tk), :]
            st = lax.dot_general(kc, qv, _NT, preferred_element_type=F32)
            m_new = jnp.maximum(m, jnp.max(st, axis=0, keepdims=True))
            pt = jnp.exp(st - m_new).astype(BF16)
            acc_ref[...] = jnp.exp(m - m_new) * acc_ref[...] + jnp.dot(vt_ref[0, c], pt,
                                                                        preferred_element_type=F32)
            return m_new

        m = lax.fori_loop(0, nk, chunk, jnp.full((1, tq), M_INIT, F32), unroll=2)
        acc = acc_ref[...]
        l = acc[HEAD_DIM:HEAD_DIM + 1, :]
        row = lax.broadcasted_iota(jnp.int32, acc.shape, 0)
        o_ref[...] = jnp.where(row < HEAD_DIM, acc / l, 0.0).T
        lse_ref[0] = m + jnp.log(l)

    return pl.pallas_call(
        body, name="mla_fwd", grid=(N_HEADS, nq),
        in_specs=[pl.BlockSpec((tq, LANES), lambda h, i: (i, h)),
                  pl.BlockSpec((S, LANES), lambda h, i: (0, h)),
                  pl.BlockSpec((1, nk, LANES, tk), lambda h, i: (h, 0, 0, 0))],
        out_specs=[pl.BlockSpec((tq, LANES), lambda h, i: (i, h)),
                   pl.BlockSpec((1, 1, tq), lambda h, i: (h, 0, i))],
        out_shape=[jax.ShapeDtypeStruct((S, WIDTH_P), F32), jax.ShapeDtypeStruct((N_HEADS, 1, S), F32)],
        scratch_shapes=[pltpu.VMEM((LANES, tq), F32)],
        compiler_params=_cparams(("parallel", "parallel")),
    )(q, k, vt)


def _mla_bwd(q, k, v, kt, do, o, lse):
    S = q.shape[0]
    tq, tk = MLA_TQ, MLA_TK
    nq, nk = S // tq, S // tk

    def body(q_ref, do_ref, o_ref, lse_ref, k_ref, v_ref, kt_ref, dq_ref, dk_ref, dv_ref, dqt_ref):
        @pl.when(pl.program_id(1) == 0)
        def _():
            dk_ref[...] = jnp.zeros_like(dk_ref)
            dv_ref[...] = jnp.zeros_like(dv_ref)

        qv, dov = q_ref[...], do_ref[...]
        delta = jnp.sum((dov.astype(F32) * o_ref[...]).T, axis=0, keepdims=True)
        lse = lse_ref[0]
        dqt_ref[...] = jnp.zeros_like(dqt_ref)

        def chunk(c, carry):
            rows = pl.ds(pl.multiple_of(c * tk, tk), tk)
            kc, vc = k_ref[rows, :], v_ref[rows, :]
            pt = jnp.exp(lax.dot_general(kc, qv, _NT, preferred_element_type=F32) - lse)
            dv_ref[rows, :] += jnp.dot(pt.astype(BF16), dov, preferred_element_type=F32)
            dpt = lax.dot_general(vc, dov, _NT, preferred_element_type=F32)
            dst = (pt * (dpt - delta)).astype(BF16)
            dk_ref[rows, :] += jnp.dot(dst, qv, preferred_element_type=F32)
            dqt_ref[...] += jnp.dot(kt_ref[0, c], dst, preferred_element_type=F32)
            return carry

        lax.fori_loop(0, nk, chunk, 0, unroll=2)
        dq_ref[...] = (dqt_ref[...] * MLA_SCALE).T

    qspec = pl.BlockSpec((tq, LANES), lambda h, i: (i, h))
    kspec = pl.BlockSpec((S, LANES), lambda h, i: (0, h))
    out = jax.ShapeDtypeStruct((S, WIDTH_P), F32)
    return pl.pallas_call(
        body, name="mla_bwd", grid=(N_HEADS, nq),
        in_specs=[qspec, qspec, qspec, pl.BlockSpec((1, 1, tq), lambda h, i: (h, 0, i)), kspec, kspec,
                  pl.BlockSpec((1, nk, LANES, tk), lambda h, i: (h, 0, 0, 0))],
        out_specs=[qspec, kspec, kspec], out_shape=[out, out, out],
        scratch_shapes=[pltpu.VMEM((LANES, tq), F32)],
        compiler_params=_cparams(("arbitrary", "arbitrary")),
    )(q, do, o, lse, k, v, kt)


def _mla_bwd_prep(dq, dk, dv, tabs):
    S = dq.shape[0]

    def body(step, dq_ref, dk_ref, dv_ref, c_ref, sa_ref, sb_ref, dqp_ref, dkv_ref, dkr_ref):
        c, sa, sb = c_ref[...], sa_ref[...], sb_ref[...]
        dksum = jnp.zeros((dq_ref.shape[0], LANES), F32)
        for h in range(N_HEADS):
            blk = slice(h * LANES, (h + 1) * LANES)
            dqp_ref[:, blk] = _rope_transpose(dq_ref[:, blk], c, sa, sb).astype(BF16)
            dksum = dksum + dk_ref[:, blk]
        dkv_ref[:, 0:WIDTH_P] = dk_ref[...].astype(BF16)
        dkv_ref[:, WIDTH_P:2 * WIDTH_P] = dv_ref[...].astype(BF16)
        lane = lax.broadcasted_iota(jnp.int32, dksum.shape, 1)
        live = (lane >= HEAD_DIM) & (lane < HEAD_DIM + QK_ROPE)
        dkr_ref[...] = jnp.where(live, _rope_transpose(dksum, c, sa, sb), 0.0)

    return _rows(body, "mla_bwd_prep", S, 256, [dq, dk, dv, *tabs], [],
                 [(WIDTH_P, BF16), (2 * WIDTH_P, BF16), (LANES, F32)])


def _mla_norm_bwd(proj_b, dcq_n, dckv_n, dkr, g_q, g_kv):
    S = proj_b.shape[0]

    def body(step, p_ref, dcq_ref, dckv_ref, dkr_ref, gq_ref, gkv_ref, dp_ref, dgq_ref, dgkv_ref):
        dcq, dgq = _rms_bwd_math(p_ref[:, 0:Q_LORA], gq_ref[...], dcq_ref[...], Q_LORA)
        dckv, dgkv = _rms_bwd_math(p_ref[:, Q_LORA:Q_LORA + KV_LORA], gkv_ref[...], dckv_ref[...], KV_LORA)
        dp_ref[:, 0:Q_LORA] = dcq.astype(BF16)
        dp_ref[:, Q_LORA:Q_LORA + KV_LORA] = dckv.astype(BF16)
        dp_ref[:, Q_LORA + KV_LORA:TAIL_P] = dkr_ref[...].astype(BF16)
        _acc_add(step, dgq_ref, jnp.sum(dgq, axis=0, keepdims=True))
        _acc_add(step, dgkv_ref, jnp.sum(dgkv, axis=0, keepdims=True))

    return _rows(body, "mla_norm_bwd", S, 512, [proj_b, dcq_n, dckv_n, dkr], [g_q, g_kv], [(TAIL_P, BF16)],
                 [((1, Q_LORA), F32), ((1, KV_LORA), F32)])


def _pad_cols(w, d):
    lead = w.shape[:-1]
    w = w.reshape(lead + (N_HEADS, d))
    w = jnp.pad(w, [(0, 0)] * len(lead) + [(0, 0), (0, LANES - d)])
    return w.reshape(lead + (N_HEADS * LANES,))


def _unpad_cols(w, d):
    lead = w.shape[:-1]
    return w.reshape(lead + (N_HEADS, LANES))[..., :d].reshape(lead + (N_HEADS * d,))


def _pad_weights(w):
    w_in = w['w_in']
    zeros = lambda n: jnp.zeros((D_MODEL, n), w_in.dtype)
    p = {}
    parts = [_pad_cols(w_in[:, i * WIDTH:(i + 1) * WIDTH], HEAD_DIM).reshape(D_MODEL, N_HEADS, 1, LANES)
             for i in range(3)]
    p['w_in_a'] = jnp.concatenate(parts, axis=2).reshape(D_MODEL, N_HEADS * HEAD_COLS)
    p['w_in_b'] = jnp.concatenate([w_in[:, 3 * WIDTH:3 * WIDTH + Q_LORA + KV_LORA], zeros(HEAD_DIM),
                                   w_in[:, D_IN - QK_ROPE:], zeros(LANES - HEAD_DIM - QK_ROPE)], axis=1)
    p['w_uq'] = _pad_cols(w['w_uq'], HEAD_DIM + QK_ROPE)
    kv = w['w_ukv'].reshape(KV_LORA, N_HEADS, 2 * HEAD_DIM)
    p['w_ukv'] = jnp.concatenate([_pad_cols(kv[:, :, :HEAD_DIM].reshape(KV_LORA, WIDTH), HEAD_DIM),
                                  _pad_cols(kv[:, :, HEAD_DIM:].reshape(KV_LORA, WIDTH), HEAD_DIM)], axis=1)
    p['w_o'] = jnp.concatenate(
        [_pad_cols(w['w_o'][i * WIDTH:(i + 1) * WIDTH].T, HEAD_DIM).T for i in range(2)], axis=0)
    p['g_a'] = _pad_cols(w['out_norm_a'], HEAD_DIM)
    p['g_b'] = _pad_cols(w['out_norm_b'], HEAD_DIM)
    return p


def _unpad_grads(d):
    g = {}
    dwa = d['w_in_a'].reshape(D_MODEL, N_HEADS, 3, LANES)
    tail = d['w_in_b']
    g['w_in'] = jnp.concatenate(
        [dwa[:, :, i, :HEAD_DIM].reshape(D_MODEL, WIDTH) for i in range(3)]
        + [tail[:, :Q_LORA + KV_LORA], tail[:, Q_LORA + KV_LORA + HEAD_DIM:Q_LORA + KV_LORA + HEAD_DIM + QK_ROPE]],
        axis=1)
    g['w_uq'] = _unpad_cols(d['w_uq'], HEAD_DIM + QK_ROPE)
    dk = _unpad_cols(d['w_ukv'][:, :WIDTH_P], HEAD_DIM).reshape(KV_LORA, N_HEADS, HEAD_DIM)
    dv = _unpad_cols(d['w_ukv'][:, WIDTH_P:], HEAD_DIM).reshape(KV_LORA, N_HEADS, HEAD_DIM)
    g['w_ukv'] = jnp.concatenate([dk, dv], axis=2).reshape(KV_LORA, 2 * WIDTH)
    g['w_o'] = jnp.concatenate(
        [_unpad_cols(d['w_o'][i * WIDTH_P:(i + 1) * WIDTH_P].T, HEAD_DIM).T for i in range(2)], axis=0)
    g['out_norm_a'] = _unpad_cols(d['g_a'], HEAD_DIM)
    g['out_norm_b'] = _unpad_cols(d['g_b'], HEAD_DIM)
    return g


def _local_step(x, target, w):
    S = x.shape[0]
    p = _pad_weights(w)
    tabs = _rope_tables(S)
    bias = _slab_bias()

    h1 = _rms_fwd(x, w['norm_mix_pre'], "rms_mix_pre")
    proj_a = _mm(h1, p['w_in_a'], 'nn', F32, "mm_in_a")
    proj_b = _mm(h1, p['w_in_b'], 'nn', F32, "mm_in_b")
    oa = _dila_fwd(proj_a, bias)
    cq_n, ckv_n, kr = _mla_prep(proj_b, w['q_lat_norm'], w['kv_lat_norm'], tabs)
    q_lin = _mm(cq_n, p['w_uq'], 'nn', F32, "mm_uq")
    kv_lin = _mm(ckv_n, p['w_ukv'], 'nn', F32, "mm_ukv")
    qb, kb, vb, kt, vt = _mla_qkv(q_lin, kv_lin, kr, tabs)
    ob, lse_b = _mla_fwd(qb, kb, vt)
    cat = _outnorm_fwd(oa, ob, p['g_a'], p['g_b'])
    y = _mm(cat, p['w_o'], 'nn', F32, "mm_o")
    x1, h2 = _post_mix(x, y, w['norm_mix_post'], w['norm_ffn_pre'])
    u0 = _mm(h2, w['w_up'], 'nn', F32, "mm_up", sharded=True)
    ug, uv, a = _conv_fwd(u0, w['conv_w'], w['conv_b'])
    y2 = _mm(a, w['w_down'], 'nn', F32, "mm_down")
    dx2, dy2, dg_ffn_post, _, loss = _final(x1, y2, w['norm_ffn_post'], target)

    g = {'norm_ffn_post': dg_ffn_post}
    da = _mm(dy2, w['w_down'], 'nt', F32, "mm_down_dx")
    g['w_down'] = _mm(a, dy2, 'tn', F32, "mm_down_dw")
    du0, g['conv_w'], g['conv_b'] = _conv_bwd(u0, ug, uv, da, w['conv_w'])
    dh2 = _mm(du0, w['w_up'], 'nt', F32, "mm_up_dx", sharded=True)
    g['w_up'] = _mm(h2, du0, 'tn', F32, "mm_up_dw", sharded=True)
    dx1, g['norm_ffn_pre'] = _rms_bwd(x1, w['norm_ffn_pre'], [dh2], dx2, F32, "rms_ffn_pre_bwd")
    dy, g['norm_mix_post'] = _rms_bwd(y, w['norm_mix_post'], [dx1], None, BF16, "rms_mix_post_bwd")
    dcat = _mm(dy, p['w_o'], 'nt', F32, "mm_o_dx")
    dpad = {'w_o': _mm(cat, dy, 'tn', F32, "mm_o_dw")}
    dopack_a, do_b, dpad['g_a'], dpad['g_b'] = _outnorm_bwd(oa, ob, p['g_a'], p['g_b'], dcat)

    dq_b, dk_b, dv_b = _mla_bwd(qb, kb, vb, kt, do_b, ob, lse_b)
    dq_pre, dkv, dkr = _mla_bwd_prep(dq_b, dk_b, dv_b, tabs)
    dcq_n = _mm(dq_pre, p['w_uq'], 'nt', F32, "mm_uq_dx")
    dpad['w_uq'] = _mm(cq_n, dq_pre, 'tn', F32, "mm_uq_dw")
    dckv_n = _mm(dkv, p['w_ukv'], 'nt', F32, "mm_ukv_dx")
    dpad['w_ukv'] = _mm(ckv_n, dkv, 'tn', F32, "mm_ukv_dw")
    dproj_b, g['q_lat_norm'], g['kv_lat_norm'] = _mla_norm_bwd(proj_b, dcq_n, dckv_n, dkr,
                                                               w['q_lat_norm'], w['kv_lat_norm'])

    dproj_a = _dila_bwd(proj_a, dopack_a, bias)
    dh1 = _mm(dproj_b, p['w_in_b'], 'nt', F32, "mm_in_b_dx")
    dh1 = _mm(dproj_a, p['w_in_a'], 'nt', F32, "mm_in_a_dx", add=dh1)
    dpad['w_in_a'] = _mm(h1, dproj_a, 'tn', F32, "mm_in_a_dw")
    dpad['w_in_b'] = _mm(h1, dproj_b, 'tn', F32, "mm_in_b_dw")
    grad_x, g['norm_mix_pre'] = _rms_bwd(x, w['norm_mix_pre'], [dh1], dx1, F32, "rms_mix_pre_bwd")
    g.update(_unpad_grads(dpad))
    return loss, grad_x, g


MESH = pl.DeviceIdType.MESH
ANY = pl.BlockSpec(memory_space=pl.ANY)


def _place():
    x, y, c = lax.axis_index("x"), lax.axis_index("y"), lax.axis_index("c")
    chips = [(1 - x, y), (x, 1 - y), (1 - x, 1 - y)]
    return x, y, c, chips


def _all_gather(bufs):
    n = len(bufs)

    def body(*refs):
        in_refs, out_refs = refs[:n], refs[n:2 * n]
        send_sems, recv_sems, local_sems = refs[2 * n:]
        x, y, c, chips = _place()
        me = 2 * x + y
        local = [pltpu.make_async_copy(in_refs[b], out_refs[b].at[me], local_sems.at[b]) for b in range(n)]
        for cp in local:
            cp.start()
        sends = []
        for j, (px, py) in enumerate(chips):
            for b in range(n):
                sends.append(pltpu.make_async_remote_copy(
                    src_ref=in_refs[b], dst_ref=out_refs[b].at[me], send_sem=send_sems.at[j * n + b],
                    recv_sem=recv_sems.at[j * n + b], device_id=(px, py, c), device_id_type=MESH))
        for cp in sends:
            cp.start()
        for j, (px, py) in enumerate(chips):
            for b in range(n):
                pltpu.make_async_remote_copy(
                    src_ref=in_refs[b], dst_ref=out_refs[b].at[2 * px + py], send_sem=send_sems.at[j * n + b],
                    recv_sem=recv_sems.at[j * n + b], device_id=(px, py, c), device_id_type=MESH).wait_recv()
        for cp in sends:
            cp.wait_send()
        for cp in local:
            cp.wait()

    return pl.pallas_call(
        body, name="gather_weights", in_specs=[ANY] * n, out_specs=[ANY] * n,
        out_shape=[jax.ShapeDtypeStruct((N_CHIPS,) + b.shape, b.dtype) for b in bufs],
        scratch_shapes=[pltpu.SemaphoreType.DMA((3 * n,)), pltpu.SemaphoreType.DMA((3 * n,)),
                        pltpu.SemaphoreType.DMA((n,))],
    )(*bufs)


def _scatter_grads(slots):
    n = len(slots)

    def body(*refs):
        g_refs, o_refs = refs[:n], refs[n:2 * n]
        send_sems, recv_sems, local_sems = refs[2 * n:]
        x, y, c, chips = _place()
        me = 2 * x + y
        local = [pltpu.make_async_copy(g_refs[b].at[me], o_refs[b].at[me], local_sems.at[b]) for b in range(n)]
        for cp in local:
            cp.start()
        sends = [pltpu.make_async_remote_copy(
            src_ref=g_refs[b].at[2 * px + py], dst_ref=o_refs[b].at[me], send_sem=send_sems.at[j * n + b],
            recv_sem=recv_sems.at[j * n + b], device_id=(px, py, c), device_id_type=MESH)
            for j, (px, py) in enumerate(chips) for b in range(n)]
        for cp in sends:
            cp.start()
        for j, (px, py) in enumerate(chips):
            for b in range(n):
                pltpu.make_async_remote_copy(
                    src_ref=g_refs[b].at[me], dst_ref=o_refs[b].at[2 * px + py], send_sem=send_sems.at[j * n + b],
                    recv_sem=recv_sems.at[j * n + b], device_id=(px, py, c), device_id_type=MESH).wait_recv()
        for cp in sends:
            cp.wait_send()
        for cp in local:
            cp.wait()

    return pl.pallas_call(
        body, name="scatter_grads", in_specs=[ANY] * n, out_specs=[ANY] * n,
        out_shape=[jax.ShapeDtypeStruct(s.shape, s.dtype) for s in slots],
        scratch_shapes=[pltpu.SemaphoreType.DMA((3 * n,)), pltpu.SemaphoreType.DMA((3 * n,)),
                        pltpu.SemaphoreType.DMA((n,))],
    )(*slots)


ELEMENTWISE_BLOCK = 256 * 1024


def _row_tile(rows, cols):
    best = None
    for t in range(8, min(rows, max(8, ELEMENTWISE_BLOCK // cols)) + 1, 8):
        if rows % t == 0:
            best = t
    return best if best is not None else rows


def _sum_slots(recv, name):
    _, R, C = recv.shape
    tr = _row_tile(R, C)

    def body(r_ref, o_ref):
        o_ref[...] = ((r_ref[0] + r_ref[1]) + r_ref[2]) + r_ref[3]

    return pl.pallas_call(
        body, name="sum_" + name, grid=(R // tr,),
        in_specs=[pl.BlockSpec((N_CHIPS, tr, C), lambda i: (0, i, 0))],
        out_specs=pl.BlockSpec((tr, C), lambda i: (i, 0)),
        out_shape=jax.ShapeDtypeStruct((R, C), recv.dtype),
        compiler_params=_cparams(("parallel",)),
    )(recv)


def _swap_sibling(parts):
    n = len(parts)

    def body(*refs):
        p_refs, o_refs, send_sems, recv_sems = refs[:n], refs[n:2 * n], refs[2 * n], refs[2 * n + 1]
        x, y, c, _ = _place()
        cps = [pltpu.make_async_remote_copy(src_ref=p_refs[b], dst_ref=o_refs[b], send_sem=send_sems.at[b],
                                            recv_sem=recv_sems.at[b], device_id=(x, y, 1 - c), device_id_type=MESH)
               for b in range(n)]
        for cp in cps:
            cp.start()
        for cp in cps:
            cp.wait()

    return pl.pallas_call(
        body, name="swap_sibling", in_specs=[ANY] * n, out_specs=[ANY] * n,
        out_shape=[jax.ShapeDtypeStruct(p.shape, p.dtype) for p in parts],
        scratch_shapes=[pltpu.SemaphoreType.DMA((n,)), pltpu.SemaphoreType.DMA((n,))],
    )(*parts)


def _adamw(g0, g1, w, m, v, name, offset=0):
    R, C = w.shape
    tr = _row_tile(R, C)
    packed = g0.shape != w.shape
    bc1 = 1.0 - ADAM_B1 ** ADAM_STEP
    bc2 = 1.0 - ADAM_B2 ** ADAM_STEP

    def body(g0_ref, g1_ref, w_ref, m_ref, v_ref, g_ref, d_ref, nm_ref, nv_ref):
        if packed:
            g = g0_ref[:, offset:offset + C] + g1_ref[:, offset:offset + C]
        else:
            g = g0_ref[...] + g1_ref[...]
        g_ref[...] = g
        nm = ADAM_B1 * m_ref[...] + (1.0 - ADAM_B1) * g
        nv = ADAM_B2 * v_ref[...] + (1.0 - ADAM_B2) * (g * g)
        nm_ref[...] = nm
        nv_ref[...] = nv
        d_ref[...] = -ADAM_LR * ((nm / bc1) / (jnp.sqrt(nv / bc2) + ADAM_EPS) + ADAM_WD * w_ref[...])

    spec = pl.BlockSpec((tr, C), lambda i: (i, 0))
    gspec = pl.BlockSpec(g0.shape, lambda i: (0, 0)) if packed else spec
    out = jax.ShapeDtypeStruct((R, C), F32)
    return pl.pallas_call(
        body, name="adamw_" + name, grid=(R // tr,), in_specs=[gspec, gspec, spec, spec, spec],
        out_specs=[spec] * 4, out_shape=[out] * 4, compiler_params=_cparams(("parallel",)),
    )(g0, g1, w, m, v)


def kernel(x, norm_mix_pre, w_in, q_lat_norm, w_uq, kv_lat_norm, w_ukv, out_norm_a, out_norm_b, w_o, norm_mix_post, norm_ffn_pre, w_up, conv_w, conv_b, w_down, norm_ffn_post, loss_target, m_norm_mix_pre, m_w_in, m_q_lat_norm, m_w_uq, m_kv_lat_norm, m_w_ukv, m_out_norm_a, m_out_norm_b, m_w_o, m_norm_mix_post, m_norm_ffn_pre, m_w_up, m_conv_w, m_conv_b, m_w_down, m_norm_ffn_post, v_norm_mix_pre, v_w_in, v_q_lat_norm, v_w_uq, v_kv_lat_norm, v_w_ukv, v_out_norm_a, v_out_norm_b, v_w_o, v_norm_mix_post, v_norm_ffn_pre, v_w_up, v_conv_w, v_conv_b, v_w_down, v_norm_ffn_post):
    args = dict(locals())
    strip = lambda a: a[0] if a.ndim == 3 else a
    wl = {n: strip(args[n]) for n in WEIGHTS}
    ml = {n: strip(args['m_' + n]) for n in WEIGHTS}
    vl = {n: strip(args['v_' + n]) for n in WEIGHTS}

    gathered = dict(zip(BIG + ['conv_w'], _all_gather([wl[n].astype(BF16) for n in BIG] + [wl['conv_w']])))
    full = {n: wl[n] for n in SMALL}
    for n in ('w_in', 'w_uq', 'w_ukv', 'conv_w'):
        full[n] = jnp.concatenate([gathered[n][i] for i in range(N_CHIPS)], axis=1)
    full['w_o'] = gathered['w_o'].reshape(D_MODEL, D_MODEL)
    full['w_down'] = gathered['w_down'].reshape(D_FF, D_MODEL)
    full['w_up'] = gathered['w_up']

    loss_b, grad_x, g = _local_step(x[0], loss_target[0], full)

    sharded = [n for n in WEIGHTS if SHARD_AXIS[n] is not None]

    def slots_of(n):
        a = g[n]
        if n == 'w_up':
            return a
        if SHARD_AXIS[n] == 0:
            return a.reshape(N_CHIPS, a.shape[0] // N_CHIPS, a.shape[1])
        cols = a.shape[1] // N_CHIPS
        return jnp.stack([a[:, i * cols:(i + 1) * cols] for i in range(N_CHIPS)])

    small_pack = jnp.concatenate([g[n] for n in SMALL], axis=1)
    slots = [slots_of(n) for n in sharded] + [jnp.broadcast_to(small_pack[None], (N_CHIPS,) + small_pack.shape)]
    recv = _scatter_grads(slots)
    parts = [_sum_slots(r, n) for r, n in zip(recv, sharded + ['small'])]
    others = _swap_sibling(parts)

    outs = {}

    def record(n, results):
        for tag, a in zip(('grad', 'delta', 'new_m', 'new_v'), results):
            outs[tag + '_' + n] = a.reshape(args[n].shape)

    for n, p0, p1 in zip(sharded, parts, others):
        record(n, _adamw(p0, p1, wl[n], ml[n], vl[n], n))
    offset = 0
    for n in SMALL:
        record(n, _adamw(parts[-1], others[-1], wl[n], ml[n], vl[n], n, offset=offset))
        offset += wl[n].shape[1]

    loss = lax.psum(loss_b[0, 0], ("x", "y", "c"))
    return (loss, grad_x[None], *[outs['grad_' + n] for n in WEIGHTS], *[outs['delta_' + n] for n in WEIGHTS],
            *[outs['new_m_' + n] for n in WEIGHTS], *[outs['new_v_' + n] for n in WEIGHTS])
```

```python
import functools
import math

import jax
import jax.numpy as jnp
import numpy as np
from jax import lax
from jax.experimental import pallas as pl
from jax.experimental.pallas import tpu as pltpu

F32 = jnp.float32
BF16 = jnp.bfloat16

LANES = 128
D_MODEL = 1024
N_HEADS = 8
HEAD_DIM = 64
QK_ROPE = 32
Q_LORA = 384
KV_LORA = 256
D_FF = 2816
WIDTH = N_HEADS * HEAD_DIM
WIDTH_P = N_HEADS * LANES
IN_SIZES = (WIDTH, WIDTH, WIDTH, Q_LORA, KV_LORA, QK_ROPE)
D_IN = sum(IN_SIZES)
TAIL_P = Q_LORA + KV_LORA + LANES
EPS = 1e-6
ROPE_BASE = 10000.0
MASKED = -2e30
M_INIT = -1e30
MLA_TQ = 4096
MLA_TK = 256
MLA_SCALE = (HEAD_DIM + QK_ROPE) ** -0.5
DIL_SCALE = HEAD_DIM ** -0.5

ADAM_LR = 0.001
ADAM_B1 = 0.9
ADAM_B2 = 0.999
ADAM_EPS = 1e-08
ADAM_WD = 0.01
ADAM_STEP = 10

VMEM_LIMIT = 56 * 1024 * 1024

N_CHIPS = 4

WEIGHTS = ['norm_mix_pre', 'w_in', 'q_lat_norm', 'w_uq', 'kv_lat_norm', 'w_ukv', 'out_norm_a', 'out_norm_b',
           'w_o', 'norm_mix_post', 'norm_ffn_pre', 'w_up', 'conv_w', 'conv_b', 'w_down', 'norm_ffn_post']
SHARD_AXIS = {'norm_mix_pre': None, 'w_in': 1, 'q_lat_norm': None, 'w_uq': 1, 'kv_lat_norm': None, 'w_ukv': 1,
              'out_norm_a': None, 'out_norm_b': None, 'w_o': 0, 'norm_mix_post': None, 'norm_ffn_pre': None,
              'w_up': 1, 'conv_w': 1, 'conv_b': None, 'w_down': 0, 'norm_ffn_post': None}
BIG = ['w_in', 'w_uq', 'w_ukv', 'w_o', 'w_up', 'w_down']
SMALL = [n for n in WEIGHTS if SHARD_AXIS[n] is None]


def _tile(dim, target):
    best = None
    t = LANES
    while t <= min(dim, target):
        if dim % t == 0:
            best = t
        t += LANES
    return best if best is not None else dim


def _cparams(sem=None):
    return pltpu.CompilerParams(dimension_semantics=sem, vmem_limit_bytes=VMEM_LIMIT)


def _mm(a, b, mode, out_dtype, name, add=None, tm=1024, tn=1024, tk=1024, sharded=False):
    if mode == 'nn':
        (M, K), (K2, N) = a.shape, ((b.shape[1], N_CHIPS * b.shape[2]) if sharded else b.shape)
        dims = (((1,), (0,)), ((), ()))
    elif mode == 'nt':
        (M, K), (N, K2) = a.shape, ((b.shape[1], N_CHIPS * b.shape[2]) if sharded else b.shape)
        dims = (((1,), (1,)), ((), ()))
    else:
        (K, M), (K2, N) = a.shape, b.shape
        dims = (((0,), (0,)), ((), ()))
    assert K == K2, (a.shape, b.shape, mode)
    tm, tn, tk = _tile(M, tm), _tile(N, tn), _tile(K, tk)
    if sharded and mode == 'nt':
        tk = K // N_CHIPS
    elif sharded:
        tn = N // N_CHIPS
    nk = K // tk
    if mode == 'nn':
        a_spec = pl.BlockSpec((tm, tk), lambda i, j, k: (i, k))
        b_spec = (pl.BlockSpec((None, tk, tn), lambda i, j, k: (j, k, 0)) if sharded
                  else pl.BlockSpec((tk, tn), lambda i, j, k: (k, j)))
    elif mode == 'nt':
        a_spec = pl.BlockSpec((tm, tk), lambda i, j, k: (i, k))
        b_spec = (pl.BlockSpec((None, tn, tk), lambda i, j, k: (k, j, 0)) if sharded
                  else pl.BlockSpec((tn, tk), lambda i, j, k: (j, k)))
    else:
        a_spec = pl.BlockSpec((tk, tm), lambda i, j, k: (k, i))
        b_spec = pl.BlockSpec((tk, tn), lambda i, j, k: (k, j))
    o_spec = pl.BlockSpec((tm, tn), lambda i, j, k: (i, j))
    out_shape = jax.ShapeDtypeStruct((M, N), out_dtype)
    if sharded and mode == 'tn':
        o_spec = pl.BlockSpec((None, tm, tn), lambda i, j, k: (j, i, 0))
        out_shape = jax.ShapeDtypeStruct((N_CHIPS, M, tn), out_dtype)
    has_add = add is not None

    def body(*refs):
        if has_add:
            a_ref, b_ref, add_ref, o_ref, acc_ref = refs
        else:
            a_ref, b_ref, o_ref, acc_ref = refs
        k = pl.program_id(2)

        @pl.when(k == 0)
        def _():
            acc_ref[...] = jnp.zeros_like(acc_ref)

        acc_ref[...] += lax.dot_general(a_ref[...].astype(BF16), b_ref[...].astype(BF16), dims,
                                        preferred_element_type=F32)

        @pl.when(k == nk - 1)
        def _():
            r = acc_ref[...]
            if has_add:
                r = r + add_ref[...]
            o_ref[...] = r.astype(o_ref.dtype)

    ins = [a, b] + ([add] if has_add else [])
    in_specs = [a_spec, b_spec] + ([o_spec] if has_add else [])
    return pl.pallas_call(
        body, name=name, grid=(M // tm, N // tn, nk), in_specs=in_specs, out_specs=o_spec, out_shape=out_shape,
        scratch_shapes=[pltpu.VMEM((tm, tn), F32)],
        compiler_params=_cparams(("parallel", "parallel", "arbitrary")),
    )(*ins)


def _rows(body, name, S, ts, row_ins, full_ins, row_outs, acc_outs=(), chunk_outs=()):
    in_specs = [pl.BlockSpec((ts, a.shape[1]), lambda i: (i, 0)) for a in row_ins]
    in_specs += [pl.BlockSpec(a.shape, lambda i, nd=a.ndim: (0,) * nd) for a in full_ins]
    out_specs = [pl.BlockSpec((ts, w), lambda i: (i, 0)) for (w, _) in row_outs]
    out_specs += [pl.BlockSpec(shape, lambda i, nd=len(shape): (0,) * nd) for (shape, _) in acc_outs]
    out_specs += [pl.BlockSpec((lead, 1, LANES, ts), lambda i: (0, i, 0, 0)) for (lead, _) in chunk_outs]
    out_shape = [jax.ShapeDtypeStruct((S, w), dt) for (w, dt) in row_outs]
    out_shape += [jax.ShapeDtypeStruct(shape, dt) for (shape, dt) in acc_outs]
    out_shape += [jax.ShapeDtypeStruct((lead, S // ts, LANES, ts), dt) for (lead, dt) in chunk_outs]

    def kbody(*refs):
        body(pl.program_id(0), *refs)

    return pl.pallas_call(
        kbody, name=name, grid=(S // ts,), in_specs=in_specs, out_specs=out_specs, out_shape=out_shape,
        compiler_params=_cparams(("arbitrary",)),
    )(*row_ins, *full_ins)


def _acc_add(step, ref, val):
    @pl.when(step == 0)
    def _():
        ref[...] = val

    @pl.when(step != 0)
    def _():
        ref[...] += val


def _rms_fwd(x, g, name):
    S, W = x.shape

    def body(step, x_ref, g_ref, h_ref):
        xv = x_ref[...]
        r = lax.rsqrt(jnp.mean(xv * xv, axis=-1, keepdims=True) + EPS)
        h_ref[...] = (xv * r * g_ref[...]).astype(BF16)

    return _rows(body, name, S, 512, [x], [g], [(W, BF16)])[0]


def _rms_bwd_math(xv, g, dy, width):
    r = lax.rsqrt(jnp.sum(xv * xv, axis=-1, keepdims=True) * (1.0 / width) + EPS)
    xn = xv * r
    dyg = dy * g
    dx = r * (dyg - xn * (jnp.sum(dyg * xn, axis=-1, keepdims=True) * (1.0 / width)))
    return dx, dy * xn


def _rms_bwd(x, g, dys, resid, out_dtype, name):
    S, W = x.shape
    nd = len(dys)
    has_res = resid is not None

    def body(step, *refs):
        x_ref = refs[0]
        dy_refs = refs[1:1 + nd]
        pos = 1 + nd
        res_ref = refs[pos] if has_res else None
        pos += int(has_res)
        g_ref, dx_ref, dg_ref = refs[pos], refs[pos + 1], refs[pos + 2]
        dy = dy_refs[0][...].astype(F32)
        for r_ in dy_refs[1:]:
            dy = dy + r_[...].astype(F32)
        dx, dgr = _rms_bwd_math(x_ref[...], g_ref[...], dy, W)
        if has_res:
            dx = dx + res_ref[...]
        dx_ref[...] = dx.astype(dx_ref.dtype)
        _acc_add(step, dg_ref, jnp.sum(dgr, axis=0, keepdims=True))

    row_ins = [x] + list(dys) + ([resid] if has_res else [])
    dx, dg = _rows(body, name, S, 256, row_ins, [g], [(W, out_dtype)], [((1, W), F32)])
    return dx, dg


def _rope_apply(xv, c, sa, sb):
    return xv * c + pltpu.roll(xv, 16, 1) * sa + pltpu.roll(xv, LANES - 16, 1) * sb


def _rope_transpose(dy, c, sa, sb):
    return dy * c + pltpu.roll(dy * sa, LANES - 16, 1) + pltpu.roll(dy * sb, 16, 1)


def _rope_tables(S):
    pos = jnp.arange(S, dtype=F32)
    inv_freq = jnp.exp(-math.log(ROPE_BASE) * jnp.arange(0, QK_ROPE, 2, dtype=F32) / QK_ROPE)
    ang = pos[:, None] * inv_freq[None, :]
    cos, sin = jnp.cos(ang), jnp.sin(ang)
    ones, zeros = jnp.ones((S, HEAD_DIM), F32), jnp.zeros((S, HEAD_DIM), F32)
    z16, z32 = jnp.zeros((S, 16), F32), jnp.zeros((S, 32), F32)
    c = jnp.concatenate([ones, cos, cos, z32], axis=1)
    sa = jnp.concatenate([zeros, z16, sin, z32], axis=1)
    sb = jnp.concatenate([zeros, -sin, z16, z32], axis=1)
    return c, sa, sb


def _mla_prep(proj_b, g_q, g_kv, tabs):
    S = proj_b.shape[0]

    def body(step, p_ref, c_ref, sa_ref, sb_ref, gq_ref, gkv_ref, cq_ref, ckv_ref, kr_ref):
        cq = p_ref[:, 0:Q_LORA]
        ckv = p_ref[:, Q_LORA:Q_LORA + KV_LORA]
        kr = p_ref[:, Q_LORA + KV_LORA:TAIL_P]
        rq = lax.rsqrt(jnp.mean(cq * cq, axis=-1, keepdims=True) + EPS)
        cq_ref[...] = (cq * rq * gq_ref[...]).astype(BF16)
        rk = lax.rsqrt(jnp.mean(ckv * ckv, axis=-1, keepdims=True) + EPS)
        ckv_ref[...] = (ckv * rk * gkv_ref[...]).astype(BF16)
        kr_ref[...] = _rope_apply(kr, c_ref[...], sa_ref[...], sb_ref[...])

    return _rows(body, "mla_prep", S, 512, [proj_b, *tabs], [g_q, g_kv],
                 [(Q_LORA, BF16), (KV_LORA, BF16), (LANES, F32)])


def _mla_qkv(q, kv, kr, tabs):
    S = q.shape[0]

    def body(step, q_ref, kv_ref, kr_ref, c_ref, sa_ref, sb_ref, qb_ref, kb_ref, vb_ref, kt_ref, vt_ref):
        c, sa, sb = c_ref[...], sa_ref[...], sb_ref[...]
        krv = kr_ref[...]
        row = lax.broadcasted_iota(jnp.int32, (LANES, MLA_TK), 0)
        for h in range(N_HEADS):
            blk = slice(h * LANES, (h + 1) * LANES)
            qb_ref[:, blk] = (_rope_apply(q_ref[:, blk], c, sa, sb) * MLA_SCALE).astype(BF16)
            kh = kv_ref[:, blk] + krv
            kb_ref[:, blk] = kh.astype(BF16)
            kt_ref[h, 0] = kh.T.astype(BF16)
            vh = kv_ref[:, WIDTH_P + h * LANES:WIDTH_P + (h + 1) * LANES]
            vt_ref[h, 0] = jnp.where(row == HEAD_DIM, 1.0, vh.T).astype(BF16)
        vb_ref[...] = kv_ref[:, WIDTH_P:2 * WIDTH_P].astype(BF16)

    return _rows(body, "mla_qkv", S, MLA_TK, [q, kv, kr, *tabs], [],
                 [(WIDTH_P, BF16), (WIDTH_P, BF16), (WIDTH_P, BF16)],
                 chunk_outs=[(N_HEADS, BF16), (N_HEADS, BF16)])


def _outnorm_fwd(oa, ob, ga, gb):
    S = oa.shape[0]

    def body(step, oa_ref, ob_ref, ga_ref, gb_ref, cat_ref):
        live = lax.broadcasted_iota(jnp.int32, oa_ref.shape, 1) % LANES < HEAD_DIM
        for o_ref, g_ref, off in ((oa_ref, ga_ref, 0), (ob_ref, gb_ref, WIDTH_P)):
            o = jnp.where(live, o_ref[...], 0.0)
            r = lax.rsqrt(jnp.sum(o * o, axis=-1, keepdims=True) * (1.0 / WIDTH) + EPS)
            cat_ref[:, off:off + WIDTH_P] = (o * r * g_ref[...]).astype(BF16)

    return _rows(body, "outnorm_fwd", S, 256, [oa, ob], [ga, gb], [(2 * WIDTH_P, BF16)])[0]


def _outnorm_bwd(oa, ob, ga, gb, dcat):
    S = oa.shape[0]

    def body(step, oa_ref, ob_ref, dcat_ref, ga_ref, gb_ref, dpa_ref, dob_ref, dga_ref, dgb_ref):
        live = lax.broadcasted_iota(jnp.int32, oa_ref.shape, 1) % LANES < HEAD_DIM
        lane = lax.broadcasted_iota(jnp.int32, (oa_ref.shape[0], LANES), 1)
        packed = oa_ref[...]
        o = jnp.where(live, packed, 0.0)
        do, dgr = _rms_bwd_math(o, ga_ref[...], dcat_ref[:, 0:WIDTH_P], WIDTH)
        _acc_add(step, dga_ref, jnp.sum(dgr, axis=0, keepdims=True))
        prod = do.astype(BF16).astype(F32) * o
        for h in range(N_HEADS):
            blk = slice(h * LANES, (h + 1) * LANES)
            delta = jnp.sum(prod[:, blk], axis=-1, keepdims=True)
            lse = jnp.sum(jnp.where(lane == HEAD_DIM, packed[:, blk], 0.0), axis=-1, keepdims=True)
            out = do[:, blk]
            for k, piece in enumerate(_split3(-delta) + _split3(-lse)):
                out = jnp.where(lane == HEAD_DIM + k, piece, out)
            dpa_ref[:, blk] = out

        ov = ob_ref[...]
        do_b, dgr_b = _rms_bwd_math(ov, gb_ref[...], dcat_ref[:, WIDTH_P:2 * WIDTH_P], WIDTH)
        dob_ref[...] = do_b.astype(BF16)
        _acc_add(step, dgb_ref, jnp.sum(dgr_b, axis=0, keepdims=True))

    return _rows(body, "outnorm_bwd", S, 256, [oa, ob, dcat], [ga, gb],
                 [(WIDTH_P, F32), (WIDTH_P, BF16)], [((1, WIDTH_P), F32), ((1, WIDTH_P), F32)])


def _post_mix(x, y, g_post, g_pre):
    S, W = x.shape

    def body(step, x_ref, y_ref, gp_ref, gq_ref, x1_ref, h_ref):
        yv = y_ref[...]
        r = lax.rsqrt(jnp.mean(yv * yv, axis=-1, keepdims=True) + EPS)
        x1 = x_ref[...] + yv * r * gp_ref[...]
        x1_ref[...] = x1
        r1 = lax.rsqrt(jnp.mean(x1 * x1, axis=-1, keepdims=True) + EPS)
        h_ref[...] = (x1 * r1 * gq_ref[...]).astype(BF16)

    return _rows(body, "post_mix", S, 512, [x, y], [g_post, g_pre], [(W, F32), (W, BF16)])


def _final(x1, y2, g, target):
    S, W = x1.shape
    nsteps = S // 256

    def body(step, x1_ref, y_ref, t_ref, g_ref, dx2_ref, dy_ref, dg_ref, sq_ref, loss_ref):
        yv = y_ref[...]
        gv = g_ref[...]
        r = lax.rsqrt(jnp.mean(yv * yv, axis=-1, keepdims=True) + EPS)
        yn = yv * r
        err = (x1_ref[...] + yn * gv) - t_ref[...]
        dx2 = err * (1.0 / W)
        dx2_ref[...] = dx2
        dyg = dx2 * gv
        dy = r * (dyg - yn * jnp.mean(dyg * yn, axis=-1, keepdims=True))
        dy_ref[...] = dy.astype(BF16)
        _acc_add(step, dg_ref, jnp.sum(dx2 * yn, axis=0, keepdims=True))
        _acc_add(step, sq_ref, jnp.sum(err * err, axis=0, keepdims=True))

        @pl.when(step == nsteps - 1)
        def _():
            tot = jnp.sum(sq_ref[...], axis=-1, keepdims=True) * (0.5 / W)
            loss_ref[...] = jnp.broadcast_to(tot, (1, LANES))

    return _rows(body, "final_loss", S, 256, [x1, y2, target], [g], [(W, F32), (W, BF16)],
                 [((1, W), F32), ((1, W), F32), ((1, LANES), F32)])


_GELU_C = math.sqrt(2.0 / math.pi)
_CONV_CHUNK = 512
_HALO = 8


def _gelu(g):
    t = jnp.tanh(_GELU_C * (g + 0.044715 * (g * g * g)))
    return g * (0.5 * (1.0 + t)), t


def _fill_padded(pad_ref, src_ref, S):
    zeros = jnp.zeros((_HALO, LANES), F32)
    pad_ref[0:_HALO, :] = zeros
    pad_ref[_HALO + S:2 * _HALO + S, :] = zeros
    for r0 in range(0, S, _CONV_CHUNK):
        pad_ref[_HALO + r0:_HALO + r0 + _CONV_CHUNK, :] = src_ref[r0:r0 + _CONV_CHUNK, :].astype(F32)


def _conv_fwd(u0, conv_w, conv_b):
    S, C2 = u0.shape
    nb = D_FF // LANES

    def body(u0g_ref, u0v_ref, wg_ref, wv_ref, bg_ref, bv_ref, ug_ref, uv_ref, a_ref, pg_ref, pv_ref):
        _fill_padded(pg_ref, u0g_ref, S)
        _fill_padded(pv_ref, u0v_ref, S)
        wg, wv = wg_ref[...], wv_ref[...]
        for r0 in range(0, S, _CONV_CHUNK):
            def conv(p_ref, w, b_ref):
                base = _HALO + r0
                return (p_ref[base - 1:base - 1 + _CONV_CHUNK, :] * w[0:1, :]
                        + p_ref[base:base + _CONV_CHUNK, :] * w[1:2, :]
                        + p_ref[base + 1:base + 1 + _CONV_CHUNK, :] * w[2:3, :] + b_ref[...])
            g = conv(pg_ref, wg, bg_ref)
            v = conv(pv_ref, wv, bv_ref)
            rows = slice(r0, r0 + _CONV_CHUNK)
            ug_ref[rows, :] = g
            uv_ref[rows, :] = v
            a_ref[rows, :] = (_gelu(g)[0] * v).astype(BF16)

    col = lambda off: pl.BlockSpec((S, LANES), lambda j: (0, j + off))
    wcol = lambda off: pl.BlockSpec((3, LANES), lambda j: (0, j + off))
    bcol = lambda off: pl.BlockSpec((1, LANES), lambda j: (0, j + off))
    ug, uv, a = pl.pallas_call(
        body, name="conv_gelu_fwd", grid=(nb,),
        in_specs=[col(0), col(nb), wcol(0), wcol(nb), bcol(0), bcol(nb)],
        out_specs=[col(0), col(0), col(0)],
        out_shape=[jax.ShapeDtypeStruct((S, D_FF), F32), jax.ShapeDtypeStruct((S, D_FF), F32),
                   jax.ShapeDtypeStruct((S, D_FF), BF16)],
        scratch_shapes=[pltpu.VMEM((S + 2 * _HALO, LANES), F32), pltpu.VMEM((S + 2 * _HALO, LANES), F32)],
        compiler_params=_cparams(("arbitrary",)),
    )(u0, u0, conv_w, conv_w, conv_b, conv_b)
    return ug, uv, a


def _conv_bwd(u0, ug, uv, da, conv_w):
    S = u0.shape[0]
    nb = D_FF // LANES

    def body(u0_ref, ug_ref, uv_ref, da_ref, w_ref, du0_ref, dw_ref, db_ref, pu_ref, pd_ref):
        is_g = pl.program_id(1) == 0
        _fill_padded(pu_ref, u0_ref, S)
        zeros = jnp.zeros((_HALO, LANES), F32)
        pd_ref[0:_HALO, :] = zeros
        pd_ref[_HALO + S:2 * _HALO + S, :] = zeros
        sel = jnp.where(is_g, 1.0, 0.0).astype(F32)
        for r0 in range(0, S, _CONV_CHUNK):
            rows = slice(r0, r0 + _CONV_CHUNK)
            g, v, d = ug_ref[rows, :], uv_ref[rows, :], da_ref[rows, :]
            gel, t = _gelu(g)
            dgel = 0.5 * (1.0 + t) + (0.5 * g) * (1.0 - t * t) * (_GELU_C * (1.0 + 3.0 * 0.044715 * (g * g)))
            du = d * (sel * (v * dgel) + (1.0 - sel) * gel)
            pd_ref[_HALO + r0:_HALO + r0 + _CONV_CHUNK, :] = du
        w = w_ref[...]
        acc_b = jnp.zeros((1, LANES), F32)
        acc_w = [jnp.zeros((1, LANES), F32) for _ in range(3)]
        for r0 in range(0, S, _CONV_CHUNK):
            base = _HALO + r0
            du_m = pd_ref[base - 1:base - 1 + _CONV_CHUNK, :]
            du_c = pd_ref[base:base + _CONV_CHUNK, :]
            du_p = pd_ref[base + 1:base + 1 + _CONV_CHUNK, :]
            du0_ref[r0:r0 + _CONV_CHUNK, :] = (du_p * w[0:1, :] + du_c * w[1:2, :] + du_m * w[2:3, :]).astype(BF16)
            acc_b = acc_b + jnp.sum(du_c, axis=0, keepdims=True)
            for k in range(3):
                acc_w[k] = acc_w[k] + jnp.sum(du_c * pu_ref[base + k - 1:base + k - 1 + _CONV_CHUNK, :],
                                              axis=0, keepdims=True)
        db_ref[...] = acc_b
        for k in range(3):
            dw_ref[k:k + 1, :] = acc_w[k]

    own = pl.BlockSpec((S, LANES), lambda j, half: (0, half * nb + j))
    shared = pl.BlockSpec((S, LANES), lambda j, half: (0, j))
    du0, dw, db = pl.pallas_call(
        body, name="conv_gelu_bwd", grid=(nb, 2),
        in_specs=[own, shared, shared, shared, pl.BlockSpec((3, LANES), lambda j, half: (0, half * nb + j))],
        out_specs=[own, pl.BlockSpec((3, LANES), lambda j, half: (0, half * nb + j)),
                   pl.BlockSpec((1, LANES), lambda j, half: (0, half * nb + j))],
        out_shape=[jax.ShapeDtypeStruct((S, 2 * D_FF), BF16), jax.ShapeDtypeStruct((3, 2 * D_FF), F32),
                   jax.ShapeDtypeStruct((1, 2 * D_FF), F32)],
        scratch_shapes=[pltpu.VMEM((S + 2 * _HALO, LANES), F32), pltpu.VMEM((S + 2 * _HALO, LANES), F32)],
        compiler_params=_cparams(("arbitrary", "arbitrary")),
    )(u0, ug, uv, da, conv_w)
    return du0, dw, db


DIL_Q = 128
DIL_HALF = 64
DIL_SLAB = DIL_Q + 2 * DIL_HALF
DILATIONS = (1, 4, 16)
DIL_SEG = 2048


def _dil_bias(r):
    row = jnp.arange(DIL_Q, dtype=jnp.int32)[:, None]
    col = jnp.arange(DIL_SLAB, dtype=jnp.int32)[None, :]
    ad = jnp.abs(col - DIL_HALF - row)
    slopes = jnp.exp2(-8.0 * jnp.arange(1, N_HEADS + 1, dtype=F32) / N_HEADS)
    base = jnp.where(ad <= DIL_HALF, -slopes[:, None, None] * (ad * r).astype(F32)[None], MASKED)
    before = jnp.broadcast_to(col < DIL_HALF, (DIL_Q, DIL_SLAB))
    after = jnp.broadcast_to(col >= DIL_Q + DIL_HALF, (DIL_Q, DIL_SLAB))
    variants = [base, jnp.where(before, MASKED, base), jnp.where(after, MASKED, base),
                jnp.where(before | after, MASKED, base)]
    return jnp.stack(variants, axis=1)


def _lanes_hi_to_all(x):
    lane = lax.broadcasted_iota(jnp.int32, x.shape, 1)
    return jnp.where(lane < HEAD_DIM, pltpu.roll(x, HEAD_DIM, 1), x)


def _fill_kv(kp_ref, vp_ref, k_ref, v_ref, L, ones):
    zeros = jnp.zeros((DIL_HALF, LANES), BF16)
    for ref in (kp_ref, vp_ref):
        ref[0:DIL_HALF, :] = zeros
        ref[DIL_HALF + L:2 * DIL_HALF + L, :] = zeros
    step = min(L, 512)
    lane = lax.broadcasted_iota(jnp.int32, (step, LANES), 1)
    for r0 in range(0, L, step):
        kp_ref[DIL_HALF + r0:DIL_HALF + r0 + step, :] = k_ref[r0:r0 + step, :]
        vv = v_ref[r0:r0 + step, :]
        vp_ref[DIL_HALF + r0:DIL_HALF + r0 + step, :] = jnp.where(lane < HEAD_DIM, vv, 1.0).astype(BF16) if ones else vv


def _dil_fwd(proj_a, bias, r):
    S = proj_a.shape[0]
    L = S // r
    nblk = L // DIL_Q
    pv = proj_a.reshape(L, r * 3 * WIDTH_P)

    def body(q_ref, k_ref, v_ref, b_ref, o_ref, kp_ref, vp_ref):
        _fill_kv(kp_ref, vp_ref, k_ref, v_ref, L, True)
        lane = lax.broadcasted_iota(jnp.int32, (DIL_Q, LANES), 1)

        def block(i, carry):
            rows = pl.ds(pl.multiple_of(i * DIL_Q, DIL_Q), DIL_Q)
            slab = pl.ds(pl.multiple_of(i * DIL_Q, DIL_Q), DIL_SLAB)
            variant = jnp.where(i == 0, 1, 0) + jnp.where(i == nblk - 1, 2, 0)
            qv = q_ref[rows, :] * DIL_SCALE
            s = lax.dot_general(qv, kp_ref[slab, :], _NT, preferred_element_type=F32) + b_ref[0, variant]
            m = jnp.max(s, axis=-1, keepdims=True)
            acc = jnp.dot(jnp.exp(s - m).astype(BF16), vp_ref[slab, :], preferred_element_type=F32)
            l = _lanes_hi_to_all(acc)
            o_ref[rows, :] = jnp.where(lane < HEAD_DIM, acc / l, m + jnp.log(l))
            return carry

        lax.fori_loop(0, nblk, block, 0, unroll=min(4, nblk))

    col = lambda part: pl.BlockSpec((L, LANES), lambda c, h: (0, c * 3 * N_HEADS + part * N_HEADS + h))
    out = pl.pallas_call(
        body, name="dil_fwd_r%d" % r, grid=(r, N_HEADS),
        in_specs=[col(0), col(1), col(2), pl.BlockSpec((1, 4, DIL_Q, DIL_SLAB), lambda c, h: (h, 0, 0, 0))],
        out_specs=pl.BlockSpec((L, LANES), lambda c, h: (0, c * N_HEADS + h)),
        out_shape=jax.ShapeDtypeStruct((L, r * WIDTH_P), F32),
        scratch_shapes=[pltpu.VMEM((L + 2 * DIL_HALF, LANES), BF16), pltpu.VMEM((L + 2 * DIL_HALF, LANES), BF16)],
        compiler_params=_cparams(("parallel", "parallel")),
    )(pv, pv, pv, bias)
    return out.reshape(S, WIDTH_P)


def _dil_combine(branches):
    S = branches[0].shape[0]

    def body(step, *refs):
        o_ref, lse_ref = refs[-2], refs[-1]
        for h in range(N_HEADS):
            blk = slice(h * LANES, (h + 1) * LANES)
            xs = [r_[:, blk] for r_ in refs[:-2]]
            lses = [_lanes_hi_to_all(x) for x in xs]
            m = functools.reduce(jnp.maximum, lses)
            ws = [jnp.exp(l - m) for l in lses]
            tot = functools.reduce(jnp.add, ws)
            lane = lax.broadcasted_iota(jnp.int32, xs[0].shape, 1)
            o = functools.reduce(jnp.add, [w * x for w, x in zip(ws, xs)]) / tot
            o_ref[:, blk] = jnp.where(lane < HEAD_DIM, o, 0.0)
            lse_ref[:, blk] = m + jnp.log(tot)

    return _rows(body, "dil_combine", S, 256, list(branches), [], [(WIDTH_P, F32), (WIDTH_P, F32)])


def _dil_bwd(proj_a, do, lse, delta, bias, r):
    S = proj_a.shape[0]
    L = S // r
    seg = min(L, DIL_SEG)
    nseg, nblk, nblk_seg = L // seg, L // DIL_Q, seg // DIL_Q
    pv = proj_a.reshape(L, r * 3 * WIDTH_P)
    view = lambda a: a.reshape(L, r * WIDTH_P)
    _TN = (((0,), (0,)), ((), ()))

    def body(q_ref, k_ref, v_ref, do_ref, lse_ref, dl_ref, b_ref, dq_ref, dk_ref, dv_ref,
             kp_ref, vp_ref, dkp_ref, dvp_ref):
        sg = pl.program_id(2)

        @pl.when(sg == 0)
        def _():
            _fill_kv(kp_ref, vp_ref, k_ref, v_ref, L, False)
            dkp_ref[...] = jnp.zeros_like(dkp_ref)
            dvp_ref[...] = jnp.zeros_like(dvp_ref)

        def block(j, carry):
            i = sg * nblk_seg + j
            rows = pl.ds(pl.multiple_of(j * DIL_Q, DIL_Q), DIL_Q)
            slab = pl.ds(pl.multiple_of(i * DIL_Q, DIL_Q), DIL_SLAB)
            variant = jnp.where(i == 0, 1, 0) + jnp.where(i == nblk - 1, 2, 0)
            qv = q_ref[rows, :] * DIL_SCALE
            dov = do_ref[rows, :]
            ks, vs = kp_ref[slab, :], vp_ref[slab, :]
            two = lambda a: jnp.concatenate([a, a], axis=1)
            s = lax.dot_general(qv, ks, _NT, preferred_element_type=F32) + b_ref[0, variant]
            p = jnp.exp(s - two(lse_ref[rows, :]))
            dp = lax.dot_general(dov, vs, _NT, preferred_element_type=F32)
            ds = (p * (dp - two(dl_ref[rows, :]))).astype(BF16)
            dq_ref[rows, :] = jnp.dot(ds, ks, preferred_element_type=F32) * DIL_SCALE
            dkp_ref[slab, :] += lax.dot_general(ds, qv, _TN, preferred_element_type=F32)
            dvp_ref[slab, :] += lax.dot_general(p.astype(BF16), dov, _TN, preferred_element_type=F32)
            return carry

        lax.fori_loop(0, nblk_seg, block, 0, unroll=min(2, nblk_seg))

        @pl.when(sg == nseg - 1)
        def _():
            dk_ref[...] = dkp_ref[DIL_HALF:DIL_HALF + L, :]
            dv_ref[...] = dvp_ref[DIL_HALF:DIL_HALF + L, :]

    col = lambda part: pl.BlockSpec((L, LANES), lambda c, h, s: (0, c * 3 * N_HEADS + part * N_HEADS + h))
    segspec = pl.BlockSpec((seg, LANES), lambda c, h, s: (s, c * N_HEADS + h))
    fullspec = pl.BlockSpec((L, LANES), lambda c, h, s: (0, c * N_HEADS + h))
    out = jax.ShapeDtypeStruct((L, r * WIDTH_P), F32)
    dq, dk, dv = pl.pallas_call(
        body, name="dil_bwd_r%d" % r, grid=(r, N_HEADS, nseg),
        in_specs=[pl.BlockSpec((seg, LANES), lambda c, h, s: (s, c * 3 * N_HEADS + h)), col(1), col(2),
                  segspec, segspec, segspec, pl.BlockSpec((1, 4, DIL_Q, DIL_SLAB), lambda c, h, s: (h, 0, 0, 0))],
        out_specs=[segspec, fullspec, fullspec], out_shape=[out, out, out],
        scratch_shapes=[pltpu.VMEM((L + 2 * DIL_HALF, LANES), BF16), pltpu.VMEM((L + 2 * DIL_HALF, LANES), BF16),
                        pltpu.VMEM((L + 2 * DIL_HALF, LANES), F32), pltpu.VMEM((L + 2 * DIL_HALF, LANES), F32)],
        compiler_params=_cparams(("arbitrary", "arbitrary", "arbitrary")),
    )(pv, pv, pv, view(do), view(lse), view(delta), bias)
    return [a.reshape(S, WIDTH_P) for a in (dq, dk, dv)]


def _dil_sum(grads):
    S = grads[0][0].shape[0]
    nb = len(grads)

    def body(step, *refs):
        out_ref = refs[-1]
        for part in range(3):
            tot = refs[part][...]
            for b in range(1, nb):
                tot = tot + refs[3 * b + part][...]
            out_ref[:, part * WIDTH_P:(part + 1) * WIDTH_P] = tot.astype(BF16)

    return _rows(body, "dil_sum", S, 256, [a for g in grads for a in g], [], [(3 * WIDTH_P, BF16)])[0]


_NT = (((1,), (1,)), ((), ()))
_TN = (((0,), (0,)), ((), ()))
HEAD_COLS = 3 * LANES


def _slab_bias():
    row = jnp.arange(DIL_Q, dtype=jnp.int32)[:, None]
    col = jnp.arange(DIL_SLAB, dtype=jnp.int32)[None, :]
    slopes = jnp.exp2(-8.0 * jnp.arange(1, N_HEADS + 1, dtype=F32) / N_HEADS)
    out = []
    for r in DILATIONS:
        variants = []
        for shift in (DIL_HALF, 0, DIL_Q):
            ad = jnp.abs(col - shift - row)
            variants.append(jnp.where(ad <= DIL_HALF, -slopes[:, None, None] * (ad * r).astype(F32)[None], MASKED))
        out.append(jnp.stack(variants, axis=1))
    return jnp.stack(out, axis=0)


DIL_CHUNK = 512


def _block_geometry(i, nblk):
    first = pl.multiple_of(i * DIL_Q, DIL_Q)
    slab0 = pl.multiple_of(jnp.clip(i * DIL_Q - DIL_HALF, 0, (nblk - 2) * DIL_Q), DIL_HALF)
    variant = jnp.where(i == 0, 1, jnp.where(i == nblk - 1, 2, 0))
    return first, slab0, variant


def _class_rows(c, r, r0, n):
    return pl.ds(c + r0 * r, n, stride=r) if r > 1 else pl.ds(r0, n)


def _dila_fwd(proj_a, bias):
    S = proj_a.shape[0]
    lmax = S // DILATIONS[1]

    def body(qh_ref, kh_ref, vh_ref, b_ref, o_ref, q_s, k_s, v_s, cm_s):
        lane = lax.broadcasted_iota(jnp.int32, (DIL_Q, LANES), 1)
        lane_s = lax.broadcasted_iota(jnp.int32, (DIL_SLAB, LANES), 1)
        lane_c = lax.broadcasted_iota(jnp.int32, (DIL_CHUNK, LANES), 1)

        def run(g, nblk, load_q, load_k, load_v, store):
            def block(i, carry):
                first, slab0, variant = _block_geometry(i, nblk)
                qv, ks, vs = load_q(first), load_k(slab0), load_v(slab0)
                s = lax.dot_general(qv, ks, _NT, preferred_element_type=F32) + b_ref[g, 0, variant]
                m = jnp.max(s, axis=-1, keepdims=True)
                acc = jnp.dot(jnp.exp(s - m).astype(BF16), vs, preferred_element_type=F32)
                l = _lanes_hi_to_all(acc)
                store(first, jnp.where(lane < HEAD_DIM, acc / l, m + jnp.log(l)))
                return carry

            lax.fori_loop(0, nblk, block, 0, unroll=4)

        def direct_store(first, val):
            o_ref[pl.ds(first, DIL_Q), :] = val

        run(0, S // DIL_Q,
            lambda f: (qh_ref[pl.ds(f, DIL_Q), :] * DIL_SCALE).astype(BF16),
            lambda s0: kh_ref[pl.ds(s0, DIL_SLAB), :].astype(BF16),
            lambda s0: jnp.where(lane_s < HEAD_DIM, vh_ref[pl.ds(s0, DIL_SLAB), :], 1.0).astype(BF16),
            direct_store)

        def cm_store(first, val):
            cm_s[pl.ds(first, DIL_Q), :] = val

        for g, r in list(enumerate(DILATIONS))[1:]:
            L = S // r
            n = min(L, DIL_CHUNK)
            for c in range(r):
                for r0 in range(0, L, n):
                    src = _class_rows(c, r, r0, n)
                    q_s[r0:r0 + n, :] = (qh_ref[src, :] * DIL_SCALE).astype(BF16)
                    k_s[r0:r0 + n, :] = kh_ref[src, :].astype(BF16)
                    v_s[r0:r0 + n, :] = jnp.where(lane_c[:n] < HEAD_DIM, vh_ref[src, :], 1.0).astype(BF16)
                run(g, L // DIL_Q, lambda f: q_s[pl.ds(f, DIL_Q), :], lambda s0: k_s[pl.ds(s0, DIL_SLAB), :],
                    lambda s0: v_s[pl.ds(s0, DIL_SLAB), :], cm_store)
                for r0 in range(0, L, n):
                    dst = _class_rows(c, r, r0, n)
                    a, b = cm_s[r0:r0 + n, :], o_ref[dst, :]
                    la, lb = _lanes_hi_to_all(a), _lanes_hi_to_all(b)
                    m = jnp.maximum(la, lb)
                    wa, wb = jnp.exp(la - m), jnp.exp(lb - m)
                    tot = wa + wb
                    o_ref[dst, :] = jnp.where(lane_c[:n] < HEAD_DIM, (wa * a + wb * b) / tot, m + jnp.log(tot))

    return pl.pallas_call(
        body, name="dil_fwd", grid=(N_HEADS,),
        in_specs=[pl.BlockSpec((S, LANES), lambda h: (0, 3 * h)), pl.BlockSpec((S, LANES), lambda h: (0, 3 * h + 1)),
                  pl.BlockSpec((S, LANES), lambda h: (0, 3 * h + 2)),
                  pl.BlockSpec((len(DILATIONS), 1, 3, DIL_Q, DIL_SLAB), lambda h: (0, h, 0, 0, 0))],
        out_specs=pl.BlockSpec((S, LANES), lambda h: (0, h)),
        out_shape=jax.ShapeDtypeStruct((S, WIDTH_P), F32),
        scratch_shapes=[pltpu.VMEM((lmax, LANES), BF16), pltpu.VMEM((lmax, LANES), BF16),
                        pltpu.VMEM((lmax, LANES), BF16), pltpu.VMEM((lmax, LANES), F32)],
        compiler_params=_cparams(("parallel",)),
    )(proj_a, proj_a, proj_a, bias)


N_SPLIT = 3


def _split3(x):
    hi = x.astype(BF16).astype(F32)
    mid = (x - hi).astype(BF16).astype(F32)
    lo = (x - hi - mid).astype(BF16).astype(F32)
    return hi, mid, lo


def _dila_bwd(proj_a, dopack, bias):
    S = proj_a.shape[0]
    lmax = S // DILATIONS[1]

    def body(qh_ref, kh_ref, vh_ref, d_ref, b_ref, out_ref, dq_ref, dk_ref, dv_ref, q_s, k_s, v_s, do_s,
             dq_c, dk_c, dv_c):
        def scalar_lanes(shape):
            lane = lax.broadcasted_iota(jnp.int32, shape, 1)
            return lane, (lane >= HEAD_DIM) & (lane < HEAD_DIM + N_SPLIT)

        def q_side(q, x):
            lane, ones = scalar_lanes(x.shape)
            lse_parts = pltpu.roll(x, LANES - N_SPLIT, 1)
            qv = jnp.where(lane < HEAD_DIM, q * DIL_SCALE, jnp.where(ones, lse_parts, 0.0)).astype(BF16)
            return qv, jnp.where(lane < HEAD_DIM + N_SPLIT, x, 0.0).astype(BF16)

        def kv_side(k, v):
            _, ones = scalar_lanes(k.shape)
            return jnp.where(ones, 1.0, k).astype(BF16), jnp.where(ones, 1.0, v).astype(BF16)

        def run(g, nblk, load_q, load_kv, dq_o, dk_o, dv_o):
            def block(i, carry):
                first, slab0, variant = _block_geometry(i, nblk)
                rows, slab = pl.ds(first, DIL_Q), pl.ds(slab0, DIL_SLAB)
                (qv, dov), (ks, vs) = load_q(rows), load_kv(slab)
                p = jnp.exp(lax.dot_general(qv, ks, _NT, preferred_element_type=F32) + b_ref[g, 0, variant])
                ds = (p * lax.dot_general(dov, vs, _NT, preferred_element_type=F32)).astype(BF16)
                dq_o[rows, :] = jnp.dot(ds, ks, preferred_element_type=F32) * DIL_SCALE
                dk_o[slab, :] += lax.dot_general(ds, qv, _TN, preferred_element_type=F32)
                dv_o[slab, :] += lax.dot_general(p.astype(BF16), dov, _TN, preferred_element_type=F32)
                return carry

            lax.fori_loop(0, nblk, block, 0, unroll=2)

        dk_ref[...] = jnp.zeros_like(dk_ref)
        dv_ref[...] = jnp.zeros_like(dv_ref)
        run(0, S // DIL_Q,
            lambda rows: q_side(qh_ref[rows, :], d_ref[rows, :]),
            lambda slab: kv_side(kh_ref[slab, :], vh_ref[slab, :]),
            dq_ref, dk_ref, dv_ref)

        for g, r in list(enumerate(DILATIONS))[1:]:
            L = S // r
            n = min(L, DIL_CHUNK)
            for c in range(r):
                for r0 in range(0, L, n):
                    src = _class_rows(c, r, r0, n)
                    q_s[r0:r0 + n, :], do_s[r0:r0 + n, :] = q_side(qh_ref[src, :], d_ref[src, :])
                    k_s[r0:r0 + n, :], v_s[r0:r0 + n, :] = kv_side(kh_ref[src, :], vh_ref[src, :])
                    dk_c[r0:r0 + n, :] = jnp.zeros((n, LANES), F32)
                    dv_c[r0:r0 + n, :] = jnp.zeros((n, LANES), F32)
                run(g, L // DIL_Q, lambda rows: (q_s[rows, :], do_s[rows, :]),
                    lambda slab: (k_s[slab, :], v_s[slab, :]), dq_c, dk_c, dv_c)
                for r0 in range(0, L, n):
                    dst = _class_rows(c, r, r0, n)
                    for acc, cls in ((dq_ref, dq_c), (dk_ref, dk_c), (dv_ref, dv_c)):
                        acc[dst, :] += cls[r0:r0 + n, :]

        for r0 in range(0, S, DIL_CHUNK):
            for part, ref in enumerate((dq_ref, dk_ref, dv_ref)):
                out_ref[r0:r0 + DIL_CHUNK, part * LANES:(part + 1) * LANES] = ref[r0:r0 + DIL_CHUNK, :].astype(BF16)

    bf = lambda rows: pltpu.VMEM((rows, LANES), BF16)
    f32 = lambda rows: pltpu.VMEM((rows, LANES), F32)
    return pl.pallas_call(
        body, name="dil_bwd", grid=(N_HEADS,),
        in_specs=[pl.BlockSpec((S, LANES), lambda h: (0, 3 * h), pipeline_mode=pl.Buffered(1)),
                  pl.BlockSpec((S, LANES), lambda h: (0, 3 * h + 1), pipeline_mode=pl.Buffered(1)),
                  pl.BlockSpec((S, LANES), lambda h: (0, 3 * h + 2), pipeline_mode=pl.Buffered(1)),
                  pl.BlockSpec((S, LANES), lambda h: (0, h), pipeline_mode=pl.Buffered(1)),
                  pl.BlockSpec((len(DILATIONS), 1, 3, DIL_Q, DIL_SLAB), lambda h: (0, h, 0, 0, 0))],
        out_specs=pl.BlockSpec((S, HEAD_COLS), lambda h: (0, h)),
        out_shape=jax.ShapeDtypeStruct((S, N_HEADS * HEAD_COLS), BF16),
        scratch_shapes=[f32(S), f32(S), f32(S), bf(lmax), bf(lmax), bf(lmax), bf(lmax),
                        f32(lmax), f32(lmax), f32(lmax)],
        compiler_params=_cparams(("arbitrary",)),
    )(proj_a, proj_a, proj_a, dopack, bias)


def _mla_fwd(q, k, vt):
    S = q.shape[0]
    tq, tk = MLA_TQ, MLA_TK
    nq, nk = S // tq, S // tk

    def body(q_ref, k_ref, vt_ref, o_ref, lse_ref, acc_ref):
        qv = q_ref[...]
        acc_ref[...] = jnp.zeros_like(acc_ref)

        def chunk(c, m):
            kc = k_ref[pl.ds(pl.multiple_of(c * tk, tk), tk), :]
            st = lax.dot_general(kc, qv, _NT, preferred_element_type=F32)
            m_new = jnp.maximum(m, jnp.max(st, axis=0, keepdims=True))
            pt = jnp.exp(st - m_new).astype(BF16)
            acc_ref[...] = jnp.exp(m - m_new) * acc_ref[...] + jnp.dot(vt_ref[0, c], pt,
                                                                        preferred_element_type=F32)
            return m_new

        m = lax.fori_loop(0, nk, chunk, jnp.full((1, tq), M_INIT, F32), unroll=2)
        acc = acc_ref[...]
        l = acc[HEAD_DIM:HEAD_DIM + 1, :]
        row = lax.broadcasted_iota(jnp.int32, acc.shape, 0)
        o_ref[...] = jnp.where(row < HEAD_DIM, acc / l, 0.0).T
        lse_ref[0] = m + jnp.log(l)

    return pl.pallas_call(
        body, name="mla_fwd", grid=(N_HEADS, nq),
        in_specs=[pl.BlockSpec((tq, LANES), lambda h, i: (i, h)),
                  pl.BlockSpec((S, LANES), lambda h, i: (0, h)),
                  pl.BlockSpec((1, nk, LANES, tk), lambda h, i: (h, 0, 0, 0))],
        out_specs=[pl.BlockSpec((tq, LANES), lambda h, i: (i, h)),
                   pl.BlockSpec((1, 1, tq), lambda h, i: (h, 0, i))],
        out_shape=[jax.ShapeDtypeStruct((S, WIDTH_P), F32), jax.ShapeDtypeStruct((N_HEADS, 1, S), F32)],
        scratch_shapes=[pltpu.VMEM((LANES, tq), F32)],
        compiler_params=_cparams(("parallel", "parallel")),
    )(q, k, vt)


def _mla_bwd(q, k, v, kt, do, o, lse):
    S = q.shape[0]
    tq, tk = MLA_TQ, MLA_TK
    nq, nk = S // tq, S // tk

    def body(q_ref, do_ref, o_ref, lse_ref, k_ref, v_ref, kt_ref, dq_ref, dk_ref, dv_ref, dqt_ref):
        @pl.when(pl.program_id(1) == 0)
        def _():
            dk_ref[...] = jnp.zeros_like(dk_ref)
            dv_ref[...] = jnp.zeros_like(dv_ref)

        qv, dov = q_ref[...], do_ref[...]
        delta = jnp.sum((dov.astype(F32) * o_ref[...]).T, axis=0, keepdims=True)
        lse = lse_ref[0]
        dqt_ref[...] = jnp.zeros_like(dqt_ref)

        def chunk(c, carry):
            rows = pl.ds(pl.multiple_of(c * tk, tk), tk)
            kc, vc = k_ref[rows, :], v_ref[rows, :]
            pt = jnp.exp(lax.dot_general(kc, qv, _NT, preferred_element_type=F32) - lse)
            dv_ref[rows, :] += jnp.dot(pt.astype(BF16), dov, preferred_element_type=F32)
            dpt = lax.dot_general(vc, dov, _NT, preferred_element_type=F32)
            dst = (pt * (dpt - delta)).astype(BF16)
            dk_ref[rows, :] += jnp.dot(dst, qv, preferred_element_type=F32)
            dqt_ref[...] += jnp.dot(kt_ref[0, c], dst, preferred_element_type=F32)
            return carry

        lax.fori_loop(0, nk, chunk, 0, unroll=2)
        dq_ref[...] = (dqt_ref[...] * MLA_SCALE).T

    qspec = pl.BlockSpec((tq, LANES), lambda h, i: (i, h))
    kspec = pl.BlockSpec((S, LANES), lambda h, i: (0, h))
    out = jax.ShapeDtypeStruct((S, WIDTH_P), F32)
    return pl.pallas_call(
        body, name="mla_bwd", grid=(N_HEADS, nq),
        in_specs=[qspec, qspec, qspec, pl.BlockSpec((1, 1, tq), lambda h, i: (h, 0, i)), kspec, kspec,
                  pl.BlockSpec((1, nk, LANES, tk), lambda h, i: (h, 0, 0, 0))],
        out_specs=[qspec, kspec, kspec], out_shape=[out, out, out],
        scratch_shapes=[pltpu.VMEM((LANES, tq), F32)],
        compiler_params=_cparams(("arbitrary", "arbitrary")),
    )(q, do, o, lse, k, v, kt)


def _mla_bwd_prep(dq, dk, dv, tabs):
    S = dq.shape[0]

    def body(step, dq_ref, dk_ref, dv_ref, c_ref, sa_ref, sb_ref, dqp_ref, dkv_ref, dkr_ref):
        c, sa, sb = c_ref[...], sa_ref[...], sb_ref[...]
        dksum = jnp.zeros((dq_ref.shape[0], LANES), F32)
        for h in range(N_HEADS):
            blk = slice(h * LANES, (h + 1) * LANES)
            dqp_ref[:, blk] = _rope_transpose(dq_ref[:, blk], c, sa, sb).astype(BF16)
            dksum = dksum + dk_ref[:, blk]
        dkv_ref[:, 0:WIDTH_P] = dk_ref[...].astype(BF16)
        dkv_ref[:, WIDTH_P:2 * WIDTH_P] = dv_ref[...].astype(BF16)
        lane = lax.broadcasted_iota(jnp.int32, dksum.shape, 1)
        live = (lane >= HEAD_DIM) & (lane < HEAD_DIM + QK_ROPE)
        dkr_ref[...] = jnp.where(live, _rope_transpose(dksum, c, sa, sb), 0.0)

    return _rows(body, "mla_bwd_prep", S, 256, [dq, dk, dv, *tabs], [],
                 [(WIDTH_P, BF16), (2 * WIDTH_P, BF16), (LANES, F32)])


def _mla_norm_bwd(proj_b, dcq_n, dckv_n, dkr, g_q, g_kv):
    S = proj_b.shape[0]

    def body(step, p_ref, dcq_ref, dckv_ref, dkr_ref, gq_ref, gkv_ref, dp_ref, dgq_ref, dgkv_ref):
        dcq, dgq = _rms_bwd_math(p_ref[:, 0:Q_LORA], gq_ref[...], dcq_ref[...], Q_LORA)
        dckv, dgkv = _rms_bwd_math(p_ref[:, Q_LORA:Q_LORA + KV_LORA], gkv_ref[...], dckv_ref[...], KV_LORA)
        dp_ref[:, 0:Q_LORA] = dcq.astype(BF16)
        dp_ref[:, Q_LORA:Q_LORA + KV_LORA] = dckv.astype(BF16)
        dp_ref[:, Q_LORA + KV_LORA:TAIL_P] = dkr_ref[...].astype(BF16)
        _acc_add(step, dgq_ref, jnp.sum(dgq, axis=0, keepdims=True))
        _acc_add(step, dgkv_ref, jnp.sum(dgkv, axis=0, keepdims=True))

    return _rows(body, "mla_norm_bwd", S, 512, [proj_b, dcq_n, dckv_n, dkr], [g_q, g_kv], [(TAIL_P, BF16)],
                 [((1, Q_LORA), F32), ((1, KV_LORA), F32)])


def _pad_cols(w, d):
    lead = w.shape[:-1]
    w = w.reshape(lead + (N_HEADS, d))
    w = jnp.pad(w, [(0, 0)] * len(lead) + [(0, 0), (0, LANES - d)])
    return w.reshape(lead + (N_HEADS * LANES,))


def _unpad_cols(w, d):
    lead = w.shape[:-1]
    return w.reshape(lead + (N_HEADS, LANES))[..., :d].reshape(lead + (N_HEADS * d,))


def _pad_weights(w):
    w_in = w['w_in']
    zeros = lambda n: jnp.zeros((D_MODEL, n), w_in.dtype)
    p = {}
    parts = [_pad_cols(w_in[:, i * WIDTH:(i + 1) * WIDTH], HEAD_DIM).reshape(D_MODEL, N_HEADS, 1, LANES)
             for i in range(3)]
    p['w_in_a'] = jnp.concatenate(parts, axis=2).reshape(D_MODEL, N_HEADS * HEAD_COLS)
    p['w_in_b'] = jnp.concatenate([w_in[:, 3 * WIDTH:3 * WIDTH + Q_LORA + KV_LORA], zeros(HEAD_DIM),
                                   w_in[:, D_IN - QK_ROPE:], zeros(LANES - HEAD_DIM - QK_ROPE)], axis=1)
    p['w_uq'] = _pad_cols(w['w_uq'], HEAD_DIM + QK_ROPE)
    kv = w['w_ukv'].reshape(KV_LORA, N_HEADS, 2 * HEAD_DIM)
    p['w_ukv'] = jnp.concatenate([_pad_cols(kv[:, :, :HEAD_DIM].reshape(KV_LORA, WIDTH), HEAD_DIM),
                                  _pad_cols(kv[:, :, HEAD_DIM:].reshape(KV_LORA, WIDTH), HEAD_DIM)], axis=1)
    p['w_o'] = jnp.concatenate(
        [_pad_cols(w['w_o'][i * WIDTH:(i + 1) * WIDTH].T, HEAD_DIM).T for i in range(2)], axis=0)
    p['g_a'] = _pad_cols(w['out_norm_a'], HEAD_DIM)
    p['g_b'] = _pad_cols(w['out_norm_b'], HEAD_DIM)
    return p


def _unpad_grads(d):
    g = {}
    dwa = d['w_in_a'].reshape(D_MODEL, N_HEADS, 3, LANES)
    tail = d['w_in_b']
    g['w_in'] = jnp.concatenate(
        [dwa[:, :, i, :HEAD_DIM].reshape(D_MODEL, WIDTH) for i in range(3)]
        + [tail[:, :Q_LORA + KV_LORA], tail[:, Q_LORA + KV_LORA + HEAD_DIM:Q_LORA + KV_LORA + HEAD_DIM + QK_ROPE]],
        axis=1)
    g['w_uq'] = _unpad_cols(d['w_uq'], HEAD_DIM + QK_ROPE)
    dk = _unpad_cols(d['w_ukv'][:, :WIDTH_P], HEAD_DIM).reshape(KV_LORA, N_HEADS, HEAD_DIM)
    dv = _unpad_cols(d['w_ukv'][:, WIDTH_P:], HEAD_DIM).reshape(KV_LORA, N_HEADS, HEAD_DIM)
    g['w_ukv'] = jnp.concatenate([dk, dv], axis=2).reshape(KV_LORA, 2 * WIDTH)
    g['w_o'] = jnp.concatenate(
        [_unpad_cols(d['w_o'][i * WIDTH_P:(i + 1) * WIDTH_P].T, HEAD_DIM).T for i in range(2)], axis=0)
    g['out_norm_a'] = _unpad_cols(d['g_a'], HEAD_DIM)
    g['out_norm_b'] = _unpad_cols(d['g_b'], HEAD_DIM)
    return g


def _local_step(x, target, w):
    S = x.shape[0]
    p = _pad_weights(w)
    tabs = _rope_tables(S)
    bias = _slab_bias()

    h1 = _rms_fwd(x, w['norm_mix_pre'], "rms_mix_pre")
    proj_a = _mm(h1, p['w_in_a'], 'nn', F32, "mm_in_a")
    proj_b = _mm(h1, p['w_in_b'], 'nn', F32, "mm_in_b")
    oa = _dila_fwd(proj_a, bias)
    cq_n, ckv_n, kr = _mla_prep(proj_b, w['q_lat_norm'], w['kv_lat_norm'], tabs)
    q_lin = _mm(cq_n, p['w_uq'], 'nn', F32, "mm_uq")
    kv_lin = _mm(ckv_n, p['w_ukv'], 'nn', F32, "mm_ukv")
    qb, kb, vb, kt, vt = _mla_qkv(q_lin, kv_lin, kr, tabs)
    ob, lse_b = _mla_fwd(qb, kb, vt)
    cat = _outnorm_fwd(oa, ob, p['g_a'], p['g_b'])
    y = _mm(cat, p['w_o'], 'nn', F32, "mm_o")
    x1, h2 = _post_mix(x, y, w['norm_mix_post'], w['norm_ffn_pre'])
    u0 = _mm(h2, w['w_up'], 'nn', F32, "mm_up", sharded=True)
    ug, uv, a = _conv_fwd(u0, w['conv_w'], w['conv_b'])
    y2 = _mm(a, w['w_down'], 'nn', F32, "mm_down")
    dx2, dy2, dg_ffn_post, _, loss = _final(x1, y2, w['norm_ffn_post'], target)

    g = {'norm_ffn_post': dg_ffn_post}
    da = _mm(dy2, w['w_down'], 'nt', F32, "mm_down_dx")
    g['w_down'] = _mm(a, dy2, 'tn', BF16, "mm_down_dw")
    du0, g['conv_w'], g['conv_b'] = _conv_bwd(u0, ug, uv, da, w['conv_w'])
    dh2 = _mm(du0, w['w_up'], 'nt', F32, "mm_up_dx", sharded=True)
    g['w_up'] = _mm(h2, du0, 'tn', BF16, "mm_up_dw", sharded=True)
    dx1, g['norm_ffn_pre'] = _rms_bwd(x1, w['norm_ffn_pre'], [dh2], dx2, F32, "rms_ffn_pre_bwd")
    dy, g['norm_mix_post'] = _rms_bwd(y, w['norm_mix_post'], [dx1], None, BF16, "rms_mix_post_bwd")
    dcat = _mm(dy, p['w_o'], 'nt', F32, "mm_o_dx")
    dpad = {'w_o': _mm(cat, dy, 'tn', F32, "mm_o_dw")}
    dopack_a, do_b, dpad['g_a'], dpad['g_b'] = _outnorm_bwd(oa, ob, p['g_a'], p['g_b'], dcat)

    dq_b, dk_b, dv_b = _mla_bwd(qb, kb, vb, kt, do_b, ob, lse_b)
    dq_pre, dkv, dkr = _mla_bwd_prep(dq_b, dk_b, dv_b, tabs)
    dcq_n = _mm(dq_pre, p['w_uq'], 'nt', F32, "mm_uq_dx")
    dpad['w_uq'] = _mm(cq_n, dq_pre, 'tn', F32, "mm_uq_dw")
    dckv_n = _mm(dkv, p['w_ukv'], 'nt', F32, "mm_ukv_dx")
    dpad['w_ukv'] = _mm(ckv_n, dkv, 'tn', F32, "mm_ukv_dw")
    dproj_b, g['q_lat_norm'], g['kv_lat_norm'] = _mla_norm_bwd(proj_b, dcq_n, dckv_n, dkr,
                                                               w['q_lat_norm'], w['kv_lat_norm'])

    dproj_a = _dila_bwd(proj_a, dopack_a, bias)
    dh1 = _mm(dproj_b, p['w_in_b'], 'nt', F32, "mm_in_b_dx")
    dh1 = _mm(dproj_a, p['w_in_a'], 'nt', F32, "mm_in_a_dx", add=dh1)
    dpad['w_in_a'] = _mm(h1, dproj_a, 'tn', F32, "mm_in_a_dw")
    dpad['w_in_b'] = _mm(h1, dproj_b, 'tn', F32, "mm_in_b_dw")
    grad_x, g['norm_mix_pre'] = _rms_bwd(x, w['norm_mix_pre'], [dh1], dx1, F32, "rms_mix_pre_bwd")
    g.update(_unpad_grads(dpad))
    return loss, grad_x, g


MESH = pl.DeviceIdType.MESH
ANY = pl.BlockSpec(memory_space=pl.ANY)


def _place():
    x, y, c = lax.axis_index("x"), lax.axis_index("y"), lax.axis_index("c")
    chips = [(1 - x, y), (x, 1 - y), (1 - x, 1 - y)]
    return x, y, c, chips


def _all_gather(bufs):
    n = len(bufs)

    def body(*refs):
        in_refs, out_refs = refs[:n], refs[n:2 * n]
        send_sems, recv_sems, local_sems = refs[2 * n:]
        x, y, c, chips = _place()
        me = 2 * x + y
        local = [pltpu.make_async_copy(in_refs[b], out_refs[b].at[me], local_sems.at[b]) for b in range(n)]
        for cp in local:
            cp.start()
        sends = []
        for j, (px, py) in enumerate(chips):
            for b in range(n):
                sends.append(pltpu.make_async_remote_copy(
                    src_ref=in_refs[b], dst_ref=out_refs[b].at[me], send_sem=send_sems.at[j * n + b],
                    recv_sem=recv_sems.at[j * n + b], device_id=(px, py, c), device_id_type=MESH))
        for cp in sends:
            cp.start()
        for j, (px, py) in enumerate(chips):
            for b in range(n):
                pltpu.make_async_remote_copy(
                    src_ref=in_refs[b], dst_ref=out_refs[b].at[2 * px + py], send_sem=send_sems.at[j * n + b],
                    recv_sem=recv_sems.at[j * n + b], device_id=(px, py, c), device_id_type=MESH).wait_recv()
        for cp in sends:
            cp.wait_send()
        for cp in local:
            cp.wait()

    return pl.pallas_call(
        body, name="gather_weights", in_specs=[ANY] * n, out_specs=[ANY] * n,
        out_shape=[jax.ShapeDtypeStruct((N_CHIPS,) + b.shape, b.dtype) for b in bufs],
        scratch_shapes=[pltpu.SemaphoreType.DMA((3 * n,)), pltpu.SemaphoreType.DMA((3 * n,)),
                        pltpu.SemaphoreType.DMA((n,))],
    )(*bufs)


def _scatter_grads(slots):
    n = len(slots)

    def body(*refs):
        g_refs, o_refs = refs[:n], refs[n:2 * n]
        send_sems, recv_sems, local_sems = refs[2 * n:]
        x, y, c, chips = _place()
        me = 2 * x + y
        local = [pltpu.make_async_copy(g_refs[b].at[me], o_refs[b].at[me], local_sems.at[b]) for b in range(n)]
        for cp in local:
            cp.start()
        sends = [pltpu.make_async_remote_copy(
            src_ref=g_refs[b].at[2 * px + py], dst_ref=o_refs[b].at[me], send_sem=send_sems.at[j * n + b],
            recv_sem=recv_sems.at[j * n + b], device_id=(px, py, c), device_id_type=MESH)
            for j, (px, py) in enumerate(chips) for b in range(n)]
        for cp in sends:
            cp.start()
        for j, (px, py) in enumerate(chips):
            for b in range(n):
                pltpu.make_async_remote_copy(
                    src_ref=g_refs[b].at[me], dst_ref=o_refs[b].at[2 * px + py], send_sem=send_sems.at[j * n + b],
                    recv_sem=recv_sems.at[j * n + b], device_id=(px, py, c), device_id_type=MESH).wait_recv()
        for cp in sends:
            cp.wait_send()
        for cp in local:
            cp.wait()

    return pl.pallas_call(
        body, name="scatter_grads", in_specs=[ANY] * n, out_specs=[ANY] * n,
        out_shape=[jax.ShapeDtypeStruct(s.shape, s.dtype) for s in slots],
        scratch_shapes=[pltpu.SemaphoreType.DMA((3 * n,)), pltpu.SemaphoreType.DMA((3 * n,)),
                        pltpu.SemaphoreType.DMA((n,))],
    )(*slots)


ELEMENTWISE_BLOCK = 256 * 1024


def _row_tile(rows, cols):
    best = None
    for t in range(16, min(rows, max(16, ELEMENTWISE_BLOCK // cols)) + 1, 16):
        if rows % t == 0:
            best = t
    return best if best is not None else rows


def _sum_slots(recv, name):
    _, R, C = recv.shape
    tr = _row_tile(R, C)

    def body(r_ref, o_ref):
        f = lambda i: r_ref[i].astype(F32)
        o_ref[...] = ((f(0) + f(1)) + f(2)) + f(3)

    return pl.pallas_call(
        body, name="sum_" + name, grid=(R // tr,),
        in_specs=[pl.BlockSpec((N_CHIPS, tr, C), lambda i: (0, i, 0))],
        out_specs=pl.BlockSpec((tr, C), lambda i: (i, 0)),
        out_shape=jax.ShapeDtypeStruct((R, C), F32),
        compiler_params=_cparams(("parallel",)),
    )(recv)


def _swap_sibling(parts):
    n = len(parts)

    def body(*refs):
        p_refs, o_refs, send_sems, recv_sems = refs[:n], refs[n:2 * n], refs[2 * n], refs[2 * n + 1]
        x, y, c, _ = _place()
        cps = [pltpu.make_async_remote_copy(src_ref=p_refs[b], dst_ref=o_refs[b], send_sem=send_sems.at[b],
                                            recv_sem=recv_sems.at[b], device_id=(x, y, 1 - c), device_id_type=MESH)
               for b in range(n)]
        for cp in cps:
            cp.start()
        for cp in cps:
            cp.wait()

    return pl.pallas_call(
        body, name="swap_sibling", in_specs=[ANY] * n, out_specs=[ANY] * n,
        out_shape=[jax.ShapeDtypeStruct(p.shape, p.dtype) for p in parts],
        scratch_shapes=[pltpu.SemaphoreType.DMA((n,)), pltpu.SemaphoreType.DMA((n,))],
    )(*parts)


def _adamw(g0, g1, w, m, v, name, offset=0):
    R, C = w.shape
    tr = _row_tile(R, C)
    packed = g0.shape != w.shape
    bc1 = 1.0 - ADAM_B1 ** ADAM_STEP
    bc2 = 1.0 - ADAM_B2 ** ADAM_STEP

    def body(g0_ref, g1_ref, w_ref, m_ref, v_ref, g_ref, d_ref, nm_ref, nv_ref):
        if packed:
            g = g0_ref[:, offset:offset + C] + g1_ref[:, offset:offset + C]
        else:
            g = g0_ref[...] + g1_ref[...]
        g_ref[...] = g
        nm = ADAM_B1 * m_ref[...] + (1.0 - ADAM_B1) * g
        nv = ADAM_B2 * v_ref[...] + (1.0 - ADAM_B2) * (g * g)
        nm_ref[...] = nm
        nv_ref[...] = nv
        d_ref[...] = -ADAM_LR * ((nm / bc1) / (jnp.sqrt(nv / bc2) + ADAM_EPS) + ADAM_WD * w_ref[...])

    spec = pl.BlockSpec((tr, C), lambda i: (i, 0))
    gspec = pl.BlockSpec(g0.shape, lambda i: (0, 0)) if packed else spec
    out = jax.ShapeDtypeStruct((R, C), F32)
    return pl.pallas_call(
        body, name="adamw_" + name, grid=(R // tr,), in_specs=[gspec, gspec, spec, spec, spec],
        out_specs=[spec] * 4, out_shape=[out] * 4, compiler_params=_cparams(("parallel",)),
    )(g0, g1, w, m, v)


def kernel(x, norm_mix_pre, w_in, q_lat_norm, w_uq, kv_lat_norm, w_ukv, out_norm_a, out_norm_b, w_o, norm_mix_post, norm_ffn_pre, w_up, conv_w, conv_b, w_down, norm_ffn_post, loss_target, m_norm_mix_pre, m_w_in, m_q_lat_norm, m_w_uq, m_kv_lat_norm, m_w_ukv, m_out_norm_a, m_out_norm_b, m_w_o, m_norm_mix_post, m_norm_ffn_pre, m_w_up, m_conv_w, m_conv_b, m_w_down, m_norm_ffn_post, v_norm_mix_pre, v_w_in, v_q_lat_norm, v_w_uq, v_kv_lat_norm, v_w_ukv, v_out_norm_a, v_out_norm_b, v_w_o, v_norm_mix_post, v_norm_ffn_pre, v_w_up, v_conv_w, v_conv_b, v_w_down, v_norm_ffn_post):
    args = dict(locals())
    strip = lambda a: a[0] if a.ndim == 3 else a
    wl = {n: strip(args[n]) for n in WEIGHTS}
    ml = {n: strip(args['m_' + n]) for n in WEIGHTS}
    vl = {n: strip(args['v_' + n]) for n in WEIGHTS}

    gathered = dict(zip(BIG + ['conv_w'], _all_gather([wl[n].astype(BF16) for n in BIG] + [wl['conv_w']])))
    full = {n: wl[n] for n in SMALL}
    for n in ('w_in', 'w_uq', 'w_ukv', 'conv_w'):
        full[n] = jnp.concatenate([gathered[n][i] for i in range(N_CHIPS)], axis=1)
    full['w_o'] = gathered['w_o'].reshape(D_MODEL, D_MODEL)
    full['w_down'] = gathered['w_down'].reshape(D_FF, D_MODEL)
    full['w_up'] = gathered['w_up']

    loss_b, grad_x, g = _local_step(x[0], loss_target[0], full)

    sharded = [n for n in WEIGHTS if SHARD_AXIS[n] is not None]

    def slots_of(n):
        a = g[n].astype(BF16)
        if n == 'w_up':
            return a
        if SHARD_AXIS[n] == 0:
            return a.reshape(N_CHIPS, a.shape[0] // N_CHIPS, a.shape[1])
        cols = a.shape[1] // N_CHIPS
        return jnp.stack([a[:, i * cols:(i + 1) * cols] for i in range(N_CHIPS)])

    small_pack = jnp.concatenate([g[n] for n in SMALL], axis=1)
    slots = [slots_of(n) for n in sharded] + [jnp.broadcast_to(small_pack[None], (N_CHIPS,) + small_pack.shape)]
    recv = _scatter_grads(slots)
    parts = [_sum_slots(r, n) for r, n in zip(recv, sharded + ['small'])]
    others = _swap_sibling(parts)

    outs = {}

    def record(n, results):
        for tag, a in zip(('grad', 'delta', 'new_m', 'new_v'), results):
            outs[tag + '_' + n] = a.reshape(args[n].shape)

    for n, p0, p1 in zip(sharded, parts, others):
        record(n, _adamw(p0, p1, wl[n], ml[n], vl[n], n))
    offset = 0
    for n in SMALL:
        record(n, _adamw(parts[-1], others[-1], wl[n], ml[n], vl[n], n, offset=offset))
        offset += wl[n].shape[1]

    loss = lax.psum(loss_b[0, 0], ("x", "y", "c"))
    return (loss, grad_x[None], *[outs['grad_' + n] for n in WEIGHTS], *[outs['delta_' + n] for n in WEIGHTS],
            *[outs['new_m_' + n] for n in WEIGHTS], *[outs['new_v_' + n] for n in WEIGHTS])
```

```python
import functools
import math

import jax
import jax.numpy as jnp
import numpy as np
from jax import lax
from jax.experimental import pallas as pl
from jax.experimental.pallas import tpu as pltpu

F32 = jnp.float32
BF16 = jnp.bfloat16

LANES = 128
D_MODEL = 1024
N_HEADS = 8
HEAD_DIM = 64
QK_ROPE = 32
Q_LORA = 384
KV_LORA = 256
D_FF = 2816
WIDTH = N_HEADS * HEAD_DIM
WIDTH_P = N_HEADS * LANES
IN_SIZES = (WIDTH, WIDTH, WIDTH, Q_LORA, KV_LORA, QK_ROPE)
D_IN = sum(IN_SIZES)
TAIL_P = Q_LORA + KV_LORA + LANES
EPS = 1e-6
ROPE_BASE = 10000.0
MASKED = -2e30
M_INIT = -1e30
MLA_TQ = 4096
MLA_TK = 256
MLA_SCALE = (HEAD_DIM + QK_ROPE) ** -0.5
DIL_SCALE = HEAD_DIM ** -0.5

ADAM_LR = 0.001
ADAM_B1 = 0.9
ADAM_B2 = 0.999
ADAM_EPS = 1e-08
ADAM_WD = 0.01
ADAM_STEP = 10

VMEM_LIMIT = 56 * 1024 * 1024

N_CHIPS = 4

WEIGHTS = ['norm_mix_pre', 'w_in', 'q_lat_norm', 'w_uq', 'kv_lat_norm', 'w_ukv', 'out_norm_a', 'out_norm_b',
           'w_o', 'norm_mix_post', 'norm_ffn_pre', 'w_up', 'conv_w', 'conv_b', 'w_down', 'norm_ffn_post']
SHARD_AXIS = {'norm_mix_pre': None, 'w_in': 1, 'q_lat_norm': None, 'w_uq': 1, 'kv_lat_norm': None, 'w_ukv': 1,
              'out_norm_a': None, 'out_norm_b': None, 'w_o': 0, 'norm_mix_post': None, 'norm_ffn_pre': None,
              'w_up': 1, 'conv_w': 1, 'conv_b': None, 'w_down': 0, 'norm_ffn_post': None}
BIG = ['w_in', 'w_uq', 'w_ukv', 'w_o', 'w_up', 'w_down']
SMALL = [n for n in WEIGHTS if SHARD_AXIS[n] is None]


def _tile(dim, target):
    best = None
    t = LANES
    while t <= min(dim, target):
        if dim % t == 0:
            best = t
        t += LANES
    return best if best is not None else dim


def _cparams(sem=None):
    return pltpu.CompilerParams(dimension_semantics=sem, vmem_limit_bytes=VMEM_LIMIT)


def _mm(a, b, mode, out_dtype, name, add=None, tm=1024, tn=1024, tk=1024, sharded=False):
    if mode == 'nn':
        (M, K), (K2, N) = a.shape, ((b.shape[1], N_CHIPS * b.shape[2]) if sharded else b.shape)
        dims = (((1,), (0,)), ((), ()))
    elif mode == 'nt':
        (M, K), (N, K2) = a.shape, ((b.shape[1], N_CHIPS * b.shape[2]) if sharded else b.shape)
        dims = (((1,), (1,)), ((), ()))
    else:
        (K, M), (K2, N) = a.shape, b.shape
        dims = (((0,), (0,)), ((), ()))
    assert K == K2, (a.shape, b.shape, mode)
    tm, tn, tk = _tile(M, tm), _tile(N, tn), _tile(K, tk)
    if K == D_FF:
        tk = K
    if N == D_FF:
        tn, tm = N, min(tm, 512)
    if M == D_FF:
        tm = M
    if sharded and mode == 'nt':
        tk = K // N_CHIPS
    elif sharded:
        tn = N // N_CHIPS
    nk = K // tk
    if mode == 'nn':
        a_spec = pl.BlockSpec((tm, tk), lambda i, j, k: (i, k))
        b_spec = (pl.BlockSpec((None, tk, tn), lambda i, j, k: (j, k, 0)) if sharded
                  else pl.BlockSpec((tk, tn), lambda i, j, k: (k, j)))
    elif mode == 'nt':
        a_spec = pl.BlockSpec((tm, tk), lambda i, j, k: (i, k))
        b_spec = (pl.BlockSpec((None, tn, tk), lambda i, j, k: (k, j, 0)) if sharded
                  else pl.BlockSpec((tn, tk), lambda i, j, k: (j, k)))
    else:
        a_spec = pl.BlockSpec((tk, tm), lambda i, j, k: (k, i))
        b_spec = pl.BlockSpec((tk, tn), lambda i, j, k: (k, j))
    o_spec = pl.BlockSpec((tm, tn), lambda i, j, k: (i, j))
    out_shape = jax.ShapeDtypeStruct((M, N), out_dtype)
    if sharded and mode == 'tn':
        o_spec = pl.BlockSpec((None, tm, tn), lambda i, j, k: (j, i, 0))
        out_shape = jax.ShapeDtypeStruct((N_CHIPS, M, tn), out_dtype)
    has_add = add is not None

    def body(*refs):
        if has_add:
            a_ref, b_ref, add_ref, o_ref, acc_ref = refs
        else:
            a_ref, b_ref, o_ref, acc_ref = refs
        k = pl.program_id(2)

        @pl.when(k == 0)
        def _():
            acc_ref[...] = jnp.zeros_like(acc_ref)

        acc_ref[...] += lax.dot_general(a_ref[...].astype(BF16), b_ref[...].astype(BF16), dims,
                                        preferred_element_type=F32)

        @pl.when(k == nk - 1)
        def _():
            r = acc_ref[...]
            if has_add:
                r = r + add_ref[...]
            o_ref[...] = r.astype(o_ref.dtype)

    ins = [a, b] + ([add] if has_add else [])
    in_specs = [a_spec, b_spec] + ([o_spec] if has_add else [])
    return pl.pallas_call(
        body, name=name, grid=(M // tm, N // tn, nk), in_specs=in_specs, out_specs=o_spec, out_shape=out_shape,
        scratch_shapes=[pltpu.VMEM((tm, tn), F32)],
        compiler_params=_cparams(("parallel", "parallel", "arbitrary")),
    )(*ins)


def _rows(body, name, S, ts, row_ins, full_ins, row_outs, acc_outs=(), chunk_outs=()):
    in_specs = [pl.BlockSpec((ts, a.shape[1]), lambda i: (i, 0)) for a in row_ins]
    in_specs += [pl.BlockSpec(a.shape, lambda i, nd=a.ndim: (0,) * nd) for a in full_ins]
    out_specs = [pl.BlockSpec((ts, w), lambda i: (i, 0)) for (w, _) in row_outs]
    out_specs += [pl.BlockSpec(shape, lambda i, nd=len(shape): (0,) * nd) for (shape, _) in acc_outs]
    out_specs += [pl.BlockSpec((lead, 1, LANES, ts), lambda i: (0, i, 0, 0)) for (lead, _) in chunk_outs]
    out_shape = [jax.ShapeDtypeStruct((S, w), dt) for (w, dt) in row_outs]
    out_shape += [jax.ShapeDtypeStruct(shape, dt) for (shape, dt) in acc_outs]
    out_shape += [jax.ShapeDtypeStruct((lead, S // ts, LANES, ts), dt) for (lead, dt) in chunk_outs]

    def kbody(*refs):
        body(pl.program_id(0), *refs)

    return pl.pallas_call(
        kbody, name=name, grid=(S // ts,), in_specs=in_specs, out_specs=out_specs, out_shape=out_shape,
        compiler_params=_cparams(("arbitrary",)),
    )(*row_ins, *full_ins)


def _acc_add(step, ref, val):
    @pl.when(step == 0)
    def _():
        ref[...] = val

    @pl.when(step != 0)
    def _():
        ref[...] += val


def _rms_fwd(x, g, name):
    S, W = x.shape

    def body(step, x_ref, g_ref, h_ref):
        xv = x_ref[...]
        r = lax.rsqrt(jnp.mean(xv * xv, axis=-1, keepdims=True) + EPS)
        h_ref[...] = (xv * r * g_ref[...]).astype(BF16)

    return _rows(body, name, S, 512, [x], [g], [(W, BF16)])[0]


def _rms_bwd_math(xv, g, dy, width):
    r = lax.rsqrt(jnp.sum(xv * xv, axis=-1, keepdims=True) * (1.0 / width) + EPS)
    xn = xv * r
    dyg = dy * g
    dx = r * (dyg - xn * (jnp.sum(dyg * xn, axis=-1, keepdims=True) * (1.0 / width)))
    return dx, dy * xn


def _rms_bwd(x, g, dys, resid, out_dtype, name):
    S, W = x.shape
    nd = len(dys)
    has_res = resid is not None

    def body(step, *refs):
        x_ref = refs[0]
        dy_refs = refs[1:1 + nd]
        pos = 1 + nd
        res_ref = refs[pos] if has_res else None
        pos += int(has_res)
        g_ref, dx_ref, dg_ref = refs[pos], refs[pos + 1], refs[pos + 2]
        dy = dy_refs[0][...].astype(F32)
        for r_ in dy_refs[1:]:
            dy = dy + r_[...].astype(F32)
        dx, dgr = _rms_bwd_math(x_ref[...], g_ref[...], dy, W)
        if has_res:
            dx = dx + res_ref[...]
        dx_ref[...] = dx.astype(dx_ref.dtype)
        _acc_add(step, dg_ref, jnp.sum(dgr, axis=0, keepdims=True))

    row_ins = [x] + list(dys) + ([resid] if has_res else [])
    dx, dg = _rows(body, name, S, 256, row_ins, [g], [(W, out_dtype)], [((1, W), F32)])
    return dx, dg


def _rope_apply(xv, c, sa, sb):
    return xv * c + pltpu.roll(xv, 16, 1) * sa + pltpu.roll(xv, LANES - 16, 1) * sb


def _rope_transpose(dy, c, sa, sb):
    return dy * c + pltpu.roll(dy * sa, LANES - 16, 1) + pltpu.roll(dy * sb, 16, 1)


def _rope_tables(S):
    pos = jnp.arange(S, dtype=F32)
    inv_freq = jnp.exp(-math.log(ROPE_BASE) * jnp.arange(0, QK_ROPE, 2, dtype=F32) / QK_ROPE)
    ang = pos[:, None] * inv_freq[None, :]
    cos, sin = jnp.cos(ang), jnp.sin(ang)
    ones, zeros = jnp.ones((S, HEAD_DIM), F32), jnp.zeros((S, HEAD_DIM), F32)
    z16, z32 = jnp.zeros((S, 16), F32), jnp.zeros((S, 32), F32)
    c = jnp.concatenate([ones, cos, cos, z32], axis=1)
    sa = jnp.concatenate([zeros, z16, sin, z32], axis=1)
    sb = jnp.concatenate([zeros, -sin, z16, z32], axis=1)
    return c, sa, sb


def _mla_prep(proj_b, g_q, g_kv, tabs):
    S = proj_b.shape[0]

    def body(step, p_ref, c_ref, sa_ref, sb_ref, gq_ref, gkv_ref, cq_ref, ckv_ref, kr_ref):
        cq = p_ref[:, 0:Q_LORA]
        ckv = p_ref[:, Q_LORA:Q_LORA + KV_LORA]
        kr = p_ref[:, Q_LORA + KV_LORA:TAIL_P]
        rq = lax.rsqrt(jnp.mean(cq * cq, axis=-1, keepdims=True) + EPS)
        cq_ref[...] = (cq * rq * gq_ref[...]).astype(BF16)
        rk = lax.rsqrt(jnp.mean(ckv * ckv, axis=-1, keepdims=True) + EPS)
        ckv_ref[...] = (ckv * rk * gkv_ref[...]).astype(BF16)
        kr_ref[...] = _rope_apply(kr, c_ref[...], sa_ref[...], sb_ref[...])

    return _rows(body, "mla_prep", S, 512, [proj_b, *tabs], [g_q, g_kv],
                 [(Q_LORA, BF16), (KV_LORA, BF16), (LANES, F32)])


def _mla_qkv(q, kv, kr, tabs):
    S = q.shape[0]

    def body(step, q_ref, kv_ref, kr_ref, c_ref, sa_ref, sb_ref, qb_ref, kb_ref, vb_ref, kt_ref, vt_ref):
        c, sa, sb = c_ref[...], sa_ref[...], sb_ref[...]
        krv = kr_ref[...]
        row = lax.broadcasted_iota(jnp.int32, (LANES, MLA_TK), 0)
        for h in range(N_HEADS):
            blk = slice(h * LANES, (h + 1) * LANES)
            qb_ref[:, blk] = (_rope_apply(q_ref[:, blk], c, sa, sb) * MLA_SCALE).astype(BF16)
            kh = kv_ref[:, blk] + krv
            kb_ref[:, blk] = kh.astype(BF16)
            kt_ref[h, 0] = kh.T.astype(BF16)
            vh = kv_ref[:, WIDTH_P + h * LANES:WIDTH_P + (h + 1) * LANES]
            vt_ref[h, 0] = jnp.where(row == HEAD_DIM, 1.0, vh.T).astype(BF16)
        vb_ref[...] = kv_ref[:, WIDTH_P:2 * WIDTH_P].astype(BF16)

    return _rows(body, "mla_qkv", S, MLA_TK, [q, kv, kr, *tabs], [],
                 [(WIDTH_P, BF16), (WIDTH_P, BF16), (WIDTH_P, BF16)],
                 chunk_outs=[(N_HEADS, BF16), (N_HEADS, BF16)])


def _outnorm_fwd(oa, ob, ga, gb):
    S = oa.shape[0]

    def body(step, oa_ref, ob_ref, ga_ref, gb_ref, cat_ref):
        live = lax.broadcasted_iota(jnp.int32, oa_ref.shape, 1) % LANES < HEAD_DIM
        for o_ref, g_ref, off in ((oa_ref, ga_ref, 0), (ob_ref, gb_ref, WIDTH_P)):
            o = jnp.where(live, o_ref[...], 0.0)
            r = lax.rsqrt(jnp.sum(o * o, axis=-1, keepdims=True) * (1.0 / WIDTH) + EPS)
            cat_ref[:, off:off + WIDTH_P] = (o * r * g_ref[...]).astype(BF16)

    return _rows(body, "outnorm_fwd", S, 256, [oa, ob], [ga, gb], [(2 * WIDTH_P, BF16)])[0]


def _outnorm_bwd(oa, ob, ga, gb, dcat):
    S = oa.shape[0]

    def body(step, oa_ref, ob_ref, dcat_ref, ga_ref, gb_ref, dpa_ref, dob_ref, dga_ref, dgb_ref):
        live = lax.broadcasted_iota(jnp.int32, oa_ref.shape, 1) % LANES < HEAD_DIM
        lane = lax.broadcasted_iota(jnp.int32, (oa_ref.shape[0], LANES), 1)
        packed = oa_ref[...]
        o = jnp.where(live, packed, 0.0)
        do, dgr = _rms_bwd_math(o, ga_ref[...], dcat_ref[:, 0:WIDTH_P], WIDTH)
        _acc_add(step, dga_ref, jnp.sum(dgr, axis=0, keepdims=True))
        prod = do.astype(BF16).astype(F32) * o
        for h in range(N_HEADS):
            blk = slice(h * LANES, (h + 1) * LANES)
            delta = jnp.sum(prod[:, blk], axis=-1, keepdims=True)
            lse = jnp.sum(jnp.where(lane == HEAD_DIM, packed[:, blk], 0.0), axis=-1, keepdims=True)
            out = do[:, blk]
            for k, piece in enumerate(_split3(-delta) + _split3(-lse)):
                out = jnp.where(lane == HEAD_DIM + k, piece, out)
            dpa_ref[:, blk] = out

        ov = ob_ref[...]
        do_b, dgr_b = _rms_bwd_math(ov, gb_ref[...], dcat_ref[:, WIDTH_P:2 * WIDTH_P], WIDTH)
        dob_ref[...] = do_b.astype(BF16)
        _acc_add(step, dgb_ref, jnp.sum(dgr_b, axis=0, keepdims=True))

    return _rows(body, "outnorm_bwd", S, 256, [oa, ob, dcat], [ga, gb],
                 [(WIDTH_P, F32), (WIDTH_P, BF16)], [((1, WIDTH_P), F32), ((1, WIDTH_P), F32)])


def _post_mix(x, y, g_post, g_pre):
    S, W = x.shape

    def body(step, x_ref, y_ref, gp_ref, gq_ref, x1_ref, h_ref):
        yv = y_ref[...]
        r = lax.rsqrt(jnp.mean(yv * yv, axis=-1, keepdims=True) + EPS)
        x1 = x_ref[...] + yv * r * gp_ref[...]
        x1_ref[...] = x1
        r1 = lax.rsqrt(jnp.mean(x1 * x1, axis=-1, keepdims=True) + EPS)
        h_ref[...] = (x1 * r1 * gq_ref[...]).astype(BF16)

    return _rows(body, "post_mix", S, 512, [x, y], [g_post, g_pre], [(W, F32), (W, BF16)])


def _final(x1, y2, g, target):
    S, W = x1.shape
    nsteps = S // 256

    def body(step, x1_ref, y_ref, t_ref, g_ref, dx2_ref, dy_ref, dg_ref, sq_ref, loss_ref):
        yv = y_ref[...]
        gv = g_ref[...]
        r = lax.rsqrt(jnp.mean(yv * yv, axis=-1, keepdims=True) + EPS)
        yn = yv * r
        err = (x1_ref[...] + yn * gv) - t_ref[...]
        dx2 = err * (1.0 / W)
        dx2_ref[...] = dx2
        dyg = dx2 * gv
        dy = r * (dyg - yn * jnp.mean(dyg * yn, axis=-1, keepdims=True))
        dy_ref[...] = dy.astype(BF16)
        _acc_add(step, dg_ref, jnp.sum(dx2 * yn, axis=0, keepdims=True))
        _acc_add(step, sq_ref, jnp.sum(err * err, axis=0, keepdims=True))

        @pl.when(step == nsteps - 1)
        def _():
            tot = jnp.sum(sq_ref[...], axis=-1, keepdims=True) * (0.5 / W)
            loss_ref[...] = jnp.broadcast_to(tot, (1, LANES))

    return _rows(body, "final_loss", S, 256, [x1, y2, target], [g], [(W, F32), (W, BF16)],
                 [((1, W), F32), ((1, W), F32), ((1, LANES), F32)])


_GELU_C = math.sqrt(2.0 / math.pi)
_CONV_CHUNK = 128
_HALO = 8


def _gelu(g):
    t = jnp.tanh(_GELU_C * (g + 0.044715 * (g * g * g)))
    return g * (0.5 * (1.0 + t)), t


def _fill_padded(pad_ref, src_ref, S):
    zeros = jnp.zeros((_HALO, LANES), F32)
    pad_ref[0:_HALO, :] = zeros
    pad_ref[_HALO + S:2 * _HALO + S, :] = zeros
    for r0 in range(0, S, _CONV_CHUNK):
        pad_ref[_HALO + r0:_HALO + r0 + _CONV_CHUNK, :] = src_ref[r0:r0 + _CONV_CHUNK, :].astype(F32)


def _conv_fwd(u0, conv_w, conv_b):
    S, C2 = u0.shape
    nb = D_FF // LANES

    def body(u0g_ref, u0v_ref, wg_ref, wv_ref, bg_ref, bv_ref, ug_ref, uv_ref, a_ref, pg_ref, pv_ref):
        _fill_padded(pg_ref, u0g_ref, S)
        _fill_padded(pv_ref, u0v_ref, S)
        wg, wv = wg_ref[...], wv_ref[...]
        for r0 in range(0, S, _CONV_CHUNK):
            def conv(p_ref, w, b_ref):
                base = _HALO + r0
                return (p_ref[base - 1:base - 1 + _CONV_CHUNK, :] * w[0:1, :]
                        + p_ref[base:base + _CONV_CHUNK, :] * w[1:2, :]
                        + p_ref[base + 1:base + 1 + _CONV_CHUNK, :] * w[2:3, :] + b_ref[...])
            g = conv(pg_ref, wg, bg_ref)
            v = conv(pv_ref, wv, bv_ref)
            rows = slice(r0, r0 + _CONV_CHUNK)
            ug_ref[rows, :] = g
            uv_ref[rows, :] = v
            a_ref[rows, :] = (_gelu(g)[0] * v).astype(BF16)

    col = lambda off: pl.BlockSpec((S, LANES), lambda j: (0, j + off))
    wcol = lambda off: pl.BlockSpec((3, LANES), lambda j: (0, j + off))
    bcol = lambda off: pl.BlockSpec((1, LANES), lambda j: (0, j + off))
    ug, uv, a = pl.pallas_call(
        body, name="conv_gelu_fwd", grid=(nb,),
        in_specs=[col(0), col(nb), wcol(0), wcol(nb), bcol(0), bcol(nb)],
        out_specs=[col(0), col(0), col(0)],
        out_shape=[jax.ShapeDtypeStruct((S, D_FF), F32), jax.ShapeDtypeStruct((S, D_FF), F32),
                   jax.ShapeDtypeStruct((S, D_FF), BF16)],
        scratch_shapes=[pltpu.VMEM((S + 2 * _HALO, LANES), F32), pltpu.VMEM((S + 2 * _HALO, LANES), F32)],
        compiler_params=_cparams(("arbitrary",)),
    )(u0, u0, conv_w, conv_w, conv_b, conv_b)
    return ug, uv, a


def _conv_bwd(u0, ug, uv, da, conv_w):
    S = u0.shape[0]
    nb = D_FF // LANES

    def body(u0_ref, ug_ref, uv_ref, da_ref, w_ref, du0_ref, dw_ref, db_ref, pu_ref, pd_ref):
        is_g = pl.program_id(1) == 0
        _fill_padded(pu_ref, u0_ref, S)
        zeros = jnp.zeros((_HALO, LANES), F32)
        pd_ref[0:_HALO, :] = zeros
        pd_ref[_HALO + S:2 * _HALO + S, :] = zeros
        @pl.when(is_g)
        def _():
            for r0 in range(0, S, _CONV_CHUNK):
                rows = slice(r0, r0 + _CONV_CHUNK)
                g = ug_ref[rows, :]
                t = _gelu(g)[1]
                dgel = 0.5 * (1.0 + t) + (0.5 * g) * (1.0 - t * t) * (_GELU_C * (1.0 + 3.0 * 0.044715 * (g * g)))
                pd_ref[_HALO + r0:_HALO + r0 + _CONV_CHUNK, :] = da_ref[rows, :] * uv_ref[rows, :] * dgel

        @pl.when(jnp.logical_not(is_g))
        def _():
            for r0 in range(0, S, _CONV_CHUNK):
                rows = slice(r0, r0 + _CONV_CHUNK)
                pd_ref[_HALO + r0:_HALO + r0 + _CONV_CHUNK, :] = da_ref[rows, :] * _gelu(ug_ref[rows, :])[0]
        w = w_ref[...]
        acc_b = jnp.zeros((1, LANES), F32)
        acc_w = [jnp.zeros((1, LANES), F32) for _ in range(3)]
        for r0 in range(0, S, _CONV_CHUNK):
            base = _HALO + r0
            du_m = pd_ref[base - 1:base - 1 + _CONV_CHUNK, :]
            du_c = pd_ref[base:base + _CONV_CHUNK, :]
            du_p = pd_ref[base + 1:base + 1 + _CONV_CHUNK, :]
            du0_ref[r0:r0 + _CONV_CHUNK, :] = (du_p * w[0:1, :] + du_c * w[1:2, :] + du_m * w[2:3, :]).astype(BF16)
            acc_b = acc_b + jnp.sum(du_c, axis=0, keepdims=True)
            for k in range(3):
                acc_w[k] = acc_w[k] + jnp.sum(du_c * pu_ref[base + k - 1:base + k - 1 + _CONV_CHUNK, :],
                                              axis=0, keepdims=True)
        db_ref[...] = acc_b
        for k in range(3):
            dw_ref[k:k + 1, :] = acc_w[k]

    own = pl.BlockSpec((S, LANES), lambda j, half: (0, half * nb + j))
    shared = pl.BlockSpec((S, LANES), lambda j, half: (0, j))
    du0, dw, db = pl.pallas_call(
        body, name="conv_gelu_bwd", grid=(nb, 2),
        in_specs=[own, shared, shared, shared, pl.BlockSpec((3, LANES), lambda j, half: (0, half * nb + j))],
        out_specs=[own, pl.BlockSpec((3, LANES), lambda j, half: (0, half * nb + j)),
                   pl.BlockSpec((1, LANES), lambda j, half: (0, half * nb + j))],
        out_shape=[jax.ShapeDtypeStruct((S, 2 * D_FF), BF16), jax.ShapeDtypeStruct((3, 2 * D_FF), F32),
                   jax.ShapeDtypeStruct((1, 2 * D_FF), F32)],
        scratch_shapes=[pltpu.VMEM((S + 2 * _HALO, LANES), F32), pltpu.VMEM((S + 2 * _HALO, LANES), F32)],
        compiler_params=_cparams(("arbitrary", "arbitrary")),
    )(u0, ug, uv, da, conv_w)
    return du0, dw, db


DIL_Q = 128
DIL_HALF = 64
DIL_SLAB = DIL_Q + 2 * DIL_HALF
DILATIONS = (1, 4, 16)
DIL_SEG = 2048


def _dil_bias(r):
    row = jnp.arange(DIL_Q, dtype=jnp.int32)[:, None]
    col = jnp.arange(DIL_SLAB, dtype=jnp.int32)[None, :]
    ad = jnp.abs(col - DIL_HALF - row)
    slopes = jnp.exp2(-8.0 * jnp.arange(1, N_HEADS + 1, dtype=F32) / N_HEADS)
    base = jnp.where(ad <= DIL_HALF, -slopes[:, None, None] * (ad * r).astype(F32)[None], MASKED)
    before = jnp.broadcast_to(col < DIL_HALF, (DIL_Q, DIL_SLAB))
    after = jnp.broadcast_to(col >= DIL_Q + DIL_HALF, (DIL_Q, DIL_SLAB))
    variants = [base, jnp.where(before, MASKED, base), jnp.where(after, MASKED, base),
                jnp.where(before | after, MASKED, base)]
    return jnp.stack(variants, axis=1)


def _lanes_hi_to_all(x):
    lane = lax.broadcasted_iota(jnp.int32, x.shape, 1)
    return jnp.where(lane < HEAD_DIM, pltpu.roll(x, HEAD_DIM, 1), x)


def _fill_kv(kp_ref, vp_ref, k_ref, v_ref, L, ones):
    zeros = jnp.zeros((DIL_HALF, LANES), BF16)
    for ref in (kp_ref, vp_ref):
        ref[0:DIL_HALF, :] = zeros
        ref[DIL_HALF + L:2 * DIL_HALF + L, :] = zeros
    step = min(L, 512)
    lane = lax.broadcasted_iota(jnp.int32, (step, LANES), 1)
    for r0 in range(0, L, step):
        kp_ref[DIL_HALF + r0:DIL_HALF + r0 + step, :] = k_ref[r0:r0 + step, :]
        vv = v_ref[r0:r0 + step, :]
        vp_ref[DIL_HALF + r0:DIL_HALF + r0 + step, :] = jnp.where(lane < HEAD_DIM, vv, 1.0).astype(BF16) if ones else vv


def _dil_fwd(proj_a, bias, r):
    S = proj_a.shape[0]
    L = S // r
    nblk = L // DIL_Q
    pv = proj_a.reshape(L, r * 3 * WIDTH_P)

    def body(q_ref, k_ref, v_ref, b_ref, o_ref, kp_ref, vp_ref):
        _fill_kv(kp_ref, vp_ref, k_ref, v_ref, L, True)
        lane = lax.broadcasted_iota(jnp.int32, (DIL_Q, LANES), 1)

        def block(i, carry):
            rows = pl.ds(pl.multiple_of(i * DIL_Q, DIL_Q), DIL_Q)
            slab = pl.ds(pl.multiple_of(i * DIL_Q, DIL_Q), DIL_SLAB)
            variant = jnp.where(i == 0, 1, 0) + jnp.where(i == nblk - 1, 2, 0)
            qv = q_ref[rows, :] * DIL_SCALE
            s = lax.dot_general(qv, kp_ref[slab, :], _NT, preferred_element_type=F32) + b_ref[0, variant]
            m = jnp.max(s, axis=-1, keepdims=True)
            acc = jnp.dot(jnp.exp(s - m).astype(BF16), vp_ref[slab, :], preferred_element_type=F32)
            l = _lanes_hi_to_all(acc)
            o_ref[rows, :] = jnp.where(lane < HEAD_DIM, acc / l, m + jnp.log(l))
            return carry

        lax.fori_loop(0, nblk, block, 0, unroll=min(4, nblk))

    col = lambda part: pl.BlockSpec((L, LANES), lambda c, h: (0, c * 3 * N_HEADS + part * N_HEADS + h))
    out = pl.pallas_call(
        body, name="dil_fwd_r%d" % r, grid=(r, N_HEADS),
        in_specs=[col(0), col(1), col(2), pl.BlockSpec((1, 4, DIL_Q, DIL_SLAB), lambda c, h: (h, 0, 0, 0))],
        out_specs=pl.BlockSpec((L, LANES), lambda c, h: (0, c * N_HEADS + h)),
        out_shape=jax.ShapeDtypeStruct((L, r * WIDTH_P), F32),
        scratch_shapes=[pltpu.VMEM((L + 2 * DIL_HALF, LANES), BF16), pltpu.VMEM((L + 2 * DIL_HALF, LANES), BF16)],
        compiler_params=_cparams(("parallel", "parallel")),
    )(pv, pv, pv, bias)
    return out.reshape(S, WIDTH_P)


def _dil_combine(branches):
    S = branches[0].shape[0]

    def body(step, *refs):
        o_ref, lse_ref = refs[-2], refs[-1]
        for h in range(N_HEADS):
            blk = slice(h * LANES, (h + 1) * LANES)
            xs = [r_[:, blk] for r_ in refs[:-2]]
            lses = [_lanes_hi_to_all(x) for x in xs]
            m = functools.reduce(jnp.maximum, lses)
            ws = [jnp.exp(l - m) for l in lses]
            tot = functools.reduce(jnp.add, ws)
            lane = lax.broadcasted_iota(jnp.int32, xs[0].shape, 1)
            o = functools.reduce(jnp.add, [w * x for w, x in zip(ws, xs)]) / tot
            o_ref[:, blk] = jnp.where(lane < HEAD_DIM, o, 0.0)
            lse_ref[:, blk] = m + jnp.log(tot)

    return _rows(body, "dil_combine", S, 256, list(branches), [], [(WIDTH_P, F32), (WIDTH_P, F32)])


def _dil_bwd(proj_a, do, lse, delta, bias, r):
    S = proj_a.shape[0]
    L = S // r
    seg = min(L, DIL_SEG)
    nseg, nblk, nblk_seg = L // seg, L // DIL_Q, seg // DIL_Q
    pv = proj_a.reshape(L, r * 3 * WIDTH_P)
    view = lambda a: a.reshape(L, r * WIDTH_P)
    _TN = (((0,), (0,)), ((), ()))

    def body(q_ref, k_ref, v_ref, do_ref, lse_ref, dl_ref, b_ref, dq_ref, dk_ref, dv_ref,
             kp_ref, vp_ref, dkp_ref, dvp_ref):
        sg = pl.program_id(2)

        @pl.when(sg == 0)
        def _():
            _fill_kv(kp_ref, vp_ref, k_ref, v_ref, L, False)
            dkp_ref[...] = jnp.zeros_like(dkp_ref)
            dvp_ref[...] = jnp.zeros_like(dvp_ref)

        def block(j, carry):
            i = sg * nblk_seg + j
            rows = pl.ds(pl.multiple_of(j * DIL_Q, DIL_Q), DIL_Q)
            slab = pl.ds(pl.multiple_of(i * DIL_Q, DIL_Q), DIL_SLAB)
            variant = jnp.where(i == 0, 1, 0) + jnp.where(i == nblk - 1, 2, 0)
            qv = q_ref[rows, :] * DIL_SCALE
            dov = do_ref[rows, :]
            ks, vs = kp_ref[slab, :], vp_ref[slab, :]
            two = lambda a: jnp.concatenate([a, a], axis=1)
            s = lax.dot_general(qv, ks, _NT, preferred_element_type=F32) + b_ref[0, variant]
            p = jnp.exp(s - two(lse_ref[rows, :]))
            dp = lax.dot_general(dov, vs, _NT, preferred_element_type=F32)
            ds = (p * (dp - two(dl_ref[rows, :]))).astype(BF16)
            dq_ref[rows, :] = jnp.dot(ds, ks, preferred_element_type=F32) * DIL_SCALE
            dkp_ref[slab, :] += lax.dot_general(ds, qv, _TN, preferred_element_type=F32)
            dvp_ref[slab, :] += lax.dot_general(p.astype(BF16), dov, _TN, preferred_element_type=F32)
            return carry

        lax.fori_loop(0, nblk_seg, block, 0, unroll=min(2, nblk_seg))

        @pl.when(sg == nseg - 1)
        def _():
            dk_ref[...] = dkp_ref[DIL_HALF:DIL_HALF + L, :]
            dv_ref[...] = dvp_ref[DIL_HALF:DIL_HALF + L, :]

    col = lambda part: pl.BlockSpec((L, LANES), lambda c, h, s: (0, c * 3 * N_HEADS + part * N_HEADS + h))
    segspec = pl.BlockSpec((seg, LANES), lambda c, h, s: (s, c * N_HEADS + h))
    fullspec = pl.BlockSpec((L, LANES), lambda c, h, s: (0, c * N_HEADS + h))
    out = jax.ShapeDtypeStruct((L, r * WIDTH_P), F32)
    dq, dk, dv = pl.pallas_call(
        body, name="dil_bwd_r%d" % r, grid=(r, N_HEADS, nseg),
        in_specs=[pl.BlockSpec((seg, LANES), lambda c, h, s: (s, c * 3 * N_HEADS + h)), col(1), col(2),
                  segspec, segspec, segspec, pl.BlockSpec((1, 4, DIL_Q, DIL_SLAB), lambda c, h, s: (h, 0, 0, 0))],
        out_specs=[segspec, fullspec, fullspec], out_shape=[out, out, out],
        scratch_shapes=[pltpu.VMEM((L + 2 * DIL_HALF, LANES), BF16), pltpu.VMEM((L + 2 * DIL_HALF, LANES), BF16),
                        pltpu.VMEM((L + 2 * DIL_HALF, LANES), F32), pltpu.VMEM((L + 2 * DIL_HALF, LANES), F32)],
        compiler_params=_cparams(("arbitrary", "arbitrary", "arbitrary")),
    )(pv, pv, pv, view(do), view(lse), view(delta), bias)
    return [a.reshape(S, WIDTH_P) for a in (dq, dk, dv)]


def _dil_sum(grads):
    S = grads[0][0].shape[0]
    nb = len(grads)

    def body(step, *refs):
        out_ref = refs[-1]
        for part in range(3):
            tot = refs[part][...]
            for b in range(1, nb):
                tot = tot + refs[3 * b + part][...]
            out_ref[:, part * WIDTH_P:(part + 1) * WIDTH_P] = tot.astype(BF16)

    return _rows(body, "dil_sum", S, 256, [a for g in grads for a in g], [], [(3 * WIDTH_P, BF16)])[0]


_NT = (((1,), (1,)), ((), ()))
_TN = (((0,), (0,)), ((), ()))
HEAD_COLS = 3 * LANES


def _slab_bias():
    row = jnp.arange(DIL_Q, dtype=jnp.int32)[:, None]
    col = jnp.arange(DIL_SLAB, dtype=jnp.int32)[None, :]
    slopes = jnp.exp2(-8.0 * jnp.arange(1, N_HEADS + 1, dtype=F32) / N_HEADS)
    out = []
    for r in DILATIONS:
        variants = []
        for shift in (DIL_HALF, 0, DIL_Q):
            ad = jnp.abs(col - shift - row)
            variants.append(jnp.where(ad <= DIL_HALF, -slopes[:, None, None] * (ad * r).astype(F32)[None], MASKED))
        out.append(jnp.stack(variants, axis=1))
    return jnp.stack(out, axis=0)


DIL_CHUNK = 512


def _block_geometry(i, nblk):
    first = pl.multiple_of(i * DIL_Q, DIL_Q)
    slab0 = pl.multiple_of(jnp.clip(i * DIL_Q - DIL_HALF, 0, (nblk - 2) * DIL_Q), DIL_HALF)
    variant = jnp.where(i == 0, 1, jnp.where(i == nblk - 1, 2, 0))
    return first, slab0, variant


def _class_rows(c, r, r0, n):
    return pl.ds(c + r0 * r, n, stride=r) if r > 1 else pl.ds(r0, n)


def _dila_fwd(proj_a, bias):
    S = proj_a.shape[0]
    lmax = S // DILATIONS[1]

    def body(qh_ref, kh_ref, vh_ref, b_ref, o_ref, q_s, k_s, v_s, cm_s):
        lane = lax.broadcasted_iota(jnp.int32, (DIL_Q, LANES), 1)
        lane_s = lax.broadcasted_iota(jnp.int32, (DIL_SLAB, LANES), 1)
        lane_c = lax.broadcasted_iota(jnp.int32, (DIL_CHUNK, LANES), 1)

        def run(g, nblk, load_q, load_k, load_v, store):
            def block(i, carry):
                first, slab0, variant = _block_geometry(i, nblk)
                qv, ks, vs = load_q(first), load_k(slab0), load_v(slab0)
                s = lax.dot_general(qv, ks, _NT, preferred_element_type=F32) + b_ref[g, 0, variant]
                m = jnp.max(s, axis=-1, keepdims=True)
                acc = jnp.dot(jnp.exp(s - m).astype(BF16), vs, preferred_element_type=F32)
                l = _lanes_hi_to_all(acc)
                store(first, jnp.where(lane < HEAD_DIM, acc / l, m + jnp.log(l)))
                return carry

            lax.fori_loop(0, nblk, block, 0, unroll=4)

        def direct_store(first, val):
            o_ref[pl.ds(first, DIL_Q), :] = val

        run(0, S // DIL_Q,
            lambda f: (qh_ref[pl.ds(f, DIL_Q), :] * DIL_SCALE).astype(BF16),
            lambda s0: kh_ref[pl.ds(s0, DIL_SLAB), :].astype(BF16),
            lambda s0: jnp.where(lane_s < HEAD_DIM, vh_ref[pl.ds(s0, DIL_SLAB), :], 1.0).astype(BF16),
            direct_store)

        def cm_store(first, val):
            cm_s[pl.ds(first, DIL_Q), :] = val

        for g, r in list(enumerate(DILATIONS))[1:]:
            L = S // r
            n = min(L, DIL_CHUNK)
            for c in range(r):
                for r0 in range(0, L, n):
                    src = _class_rows(c, r, r0, n)
                    q_s[r0:r0 + n, :] = (qh_ref[src, :] * DIL_SCALE).astype(BF16)
                    k_s[r0:r0 + n, :] = kh_ref[src, :].astype(BF16)
                    v_s[r0:r0 + n, :] = jnp.where(lane_c[:n] < HEAD_DIM, vh_ref[src, :], 1.0).astype(BF16)
                run(g, L // DIL_Q, lambda f: q_s[pl.ds(f, DIL_Q), :], lambda s0: k_s[pl.ds(s0, DIL_SLAB), :],
                    lambda s0: v_s[pl.ds(s0, DIL_SLAB), :], cm_store)
                for r0 in range(0, L, n):
                    dst = _class_rows(c, r, r0, n)
                    a, b = cm_s[r0:r0 + n, :], o_ref[dst, :]
                    la, lb = _lanes_hi_to_all(a), _lanes_hi_to_all(b)
                    m = jnp.maximum(la, lb)
                    wa, wb = jnp.exp(la - m), jnp.exp(lb - m)
                    tot = wa + wb
                    o_ref[dst, :] = jnp.where(lane_c[:n] < HEAD_DIM, (wa * a + wb * b) / tot, m + jnp.log(tot))

    return pl.pallas_call(
        body, name="dil_fwd", grid=(N_HEADS,),
        in_specs=[pl.BlockSpec((S, LANES), lambda h: (0, 3 * h)), pl.BlockSpec((S, LANES), lambda h: (0, 3 * h + 1)),
                  pl.BlockSpec((S, LANES), lambda h: (0, 3 * h + 2)),
                  pl.BlockSpec((len(DILATIONS), 1, 3, DIL_Q, DIL_SLAB), lambda h: (0, h, 0, 0, 0))],
        out_specs=pl.BlockSpec((S, LANES), lambda h: (0, h)),
        out_shape=jax.ShapeDtypeStruct((S, WIDTH_P), F32),
        scratch_shapes=[pltpu.VMEM((lmax, LANES), BF16), pltpu.VMEM((lmax, LANES), BF16),
                        pltpu.VMEM((lmax, LANES), BF16), pltpu.VMEM((lmax, LANES), F32)],
        compiler_params=_cparams(("parallel",)),
    )(proj_a, proj_a, proj_a, bias)


N_SPLIT = 3


def _split3(x):
    hi = x.astype(BF16).astype(F32)
    mid = (x - hi).astype(BF16).astype(F32)
    lo = (x - hi - mid).astype(BF16).astype(F32)
    return hi, mid, lo


def _dila_bwd(proj_a, dopack, bias):
    S = proj_a.shape[0]
    lmax = S // DILATIONS[1]

    def body(qh_ref, kh_ref, vh_ref, d_ref, b_ref, out_ref, dq_ref, dk_ref, dv_ref, q_s, k_s, v_s, do_s,
             dq_c, dk_c, dv_c):
        def scalar_lanes(shape):
            lane = lax.broadcasted_iota(jnp.int32, shape, 1)
            return lane, (lane >= HEAD_DIM) & (lane < HEAD_DIM + N_SPLIT)

        def q_side(q, x):
            lane, ones = scalar_lanes(x.shape)
            lse_parts = pltpu.roll(x, LANES - N_SPLIT, 1)
            qv = jnp.where(lane < HEAD_DIM, q * DIL_SCALE, jnp.where(ones, lse_parts, 0.0)).astype(BF16)
            return qv, jnp.where(lane < HEAD_DIM + N_SPLIT, x, 0.0).astype(BF16)

        def kv_side(k, v):
            _, ones = scalar_lanes(k.shape)
            return jnp.where(ones, 1.0, k).astype(BF16), jnp.where(ones, 1.0, v).astype(BF16)

        def run(g, nblk, load_q, load_kv, dq_o, dk_o, dv_o):
            def block(i, carry):
                first, slab0, variant = _block_geometry(i, nblk)
                rows, slab = pl.ds(first, DIL_Q), pl.ds(slab0, DIL_SLAB)
                (qv, dov), (ks, vs) = load_q(rows), load_kv(slab)
                p = jnp.exp(lax.dot_general(qv, ks, _NT, preferred_element_type=F32) + b_ref[g, 0, variant])
                ds = (p * lax.dot_general(dov, vs, _NT, preferred_element_type=F32)).astype(BF16)
                dq_o[rows, :] = jnp.dot(ds, ks, preferred_element_type=F32) * DIL_SCALE
                dk_o[slab, :] += lax.dot_general(ds, qv, _TN, preferred_element_type=F32)
                dv_o[slab, :] += lax.dot_general(p.astype(BF16), dov, _TN, preferred_element_type=F32)
                return carry

            lax.fori_loop(0, nblk, block, 0, unroll=4)

        dk_ref[...] = jnp.zeros_like(dk_ref)
        dv_ref[...] = jnp.zeros_like(dv_ref)
        run(0, S // DIL_Q,
            lambda rows: q_side(qh_ref[rows, :], d_ref[rows, :]),
            lambda slab: kv_side(kh_ref[slab, :], vh_ref[slab, :]),
            dq_ref, dk_ref, dv_ref)

        for g, r in list(enumerate(DILATIONS))[1:]:
            L = S // r
            n = min(L, DIL_CHUNK)
            for c in range(r):
                for r0 in range(0, L, n):
                    src = _class_rows(c, r, r0, n)
                    q_s[r0:r0 + n, :], do_s[r0:r0 + n, :] = q_side(qh_ref[src, :], d_ref[src, :])
                    k_s[r0:r0 + n, :], v_s[r0:r0 + n, :] = kv_side(kh_ref[src, :], vh_ref[src, :])
                    dk_c[r0:r0 + n, :] = jnp.zeros((n, LANES), F32)
                    dv_c[r0:r0 + n, :] = jnp.zeros((n, LANES), F32)
                run(g, L // DIL_Q, lambda rows: (q_s[rows, :], do_s[rows, :]),
                    lambda slab: (k_s[slab, :], v_s[slab, :]), dq_c, dk_c, dv_c)
                for r0 in range(0, L, n):
                    dst = _class_rows(c, r, r0, n)
                    for acc, cls in ((dq_ref, dq_c), (dk_ref, dk_c), (dv_ref, dv_c)):
                        acc[dst, :] += cls[r0:r0 + n, :]

        for r0 in range(0, S, DIL_CHUNK):
            for part, ref in enumerate((dq_ref, dk_ref, dv_ref)):
                out_ref[r0:r0 + DIL_CHUNK, part * LANES:(part + 1) * LANES] = ref[r0:r0 + DIL_CHUNK, :].astype(BF16)

    bf = lambda rows: pltpu.VMEM((rows, LANES), BF16)
    f32 = lambda rows: pltpu.VMEM((rows, LANES), F32)
    return pl.pallas_call(
        body, name="dil_bwd", grid=(N_HEADS,),
        in_specs=[pl.BlockSpec((S, LANES), lambda h: (0, 3 * h), pipeline_mode=pl.Buffered(1)),
                  pl.BlockSpec((S, LANES), lambda h: (0, 3 * h + 1), pipeline_mode=pl.Buffered(1)),
                  pl.BlockSpec((S, LANES), lambda h: (0, 3 * h + 2), pipeline_mode=pl.Buffered(1)),
                  pl.BlockSpec((S, LANES), lambda h: (0, h), pipeline_mode=pl.Buffered(1)),
                  pl.BlockSpec((len(DILATIONS), 1, 3, DIL_Q, DIL_SLAB), lambda h: (0, h, 0, 0, 0))],
        out_specs=pl.BlockSpec((S, HEAD_COLS), lambda h: (0, h)),
        out_shape=jax.ShapeDtypeStruct((S, N_HEADS * HEAD_COLS), BF16),
        scratch_shapes=[f32(S), f32(S), f32(S), bf(lmax), bf(lmax), bf(lmax), bf(lmax),
                        f32(lmax), f32(lmax), f32(lmax)],
        compiler_params=_cparams(("arbitrary",)),
    )(proj_a, proj_a, proj_a, dopack, bias)


def _mla_fwd(q, k, vt):
    S = q.shape[0]
    tq, tk = MLA_TQ, MLA_TK
    nq, nk = S // tq, S // tk

    def body(q_ref, k_ref, vt_ref, o_ref, lse_ref, acc_ref):
        qv = q_ref[...]
        acc_ref[...] = jnp.zeros_like(acc_ref)

        def chunk(c, m):
            kc = k_ref[pl.ds(pl.multiple_of(c * tk, tk), tk), :]
            st = lax.dot_general(kc, qv, _NT, preferred_element_type=F32)
            m_new = jnp.maximum(m, jnp.max(st, axis=0, keepdims=True))
            pt = jnp.exp(st - m_new).astype(BF16)
            acc_ref[...] = jnp.exp(m - m_new) * acc_ref[...] + jnp.dot(vt_ref[0, c], pt,
                                                                        preferred_element_type=F32)
            return m_new

        m = lax.fori_loop(0, nk, chunk, jnp.full((1, tq), M_INIT, F32), unroll=2)
        acc = acc_ref[...]
        l = acc[HEAD_DIM:HEAD_DIM + 1, :]
        row = lax.broadcasted_iota(jnp.int32, acc.shape, 0)
        o_ref[...] = jnp.where(row < HEAD_DIM, acc / l, 0.0).T
        lse_ref[0] = m + jnp.log(l)

    return pl.pallas_call(
        body, name="mla_fwd", grid=(N_HEADS, nq),
        in_specs=[pl.BlockSpec((tq, LANES), lambda h, i: (i, h)),
                  pl.BlockSpec((S, LANES), lambda h, i: (0, h)),
                  pl.BlockSpec((1, nk, LANES, tk), lambda h, i: (h, 0, 0, 0))],
        out_specs=[pl.BlockSpec((tq, LANES), lambda h, i: (i, h)),
                   pl.BlockSpec((1, 1, tq), lambda h, i: (h, 0, i))],
        out_shape=[jax.ShapeDtypeStruct((S, WIDTH_P), F32), jax.ShapeDtypeStruct((N_HEADS, 1, S), F32)],
        scratch_shapes=[pltpu.VMEM((LANES, tq), F32)],
        compiler_params=_cparams(("parallel", "parallel")),
    )(q, k, vt)


def _mla_bwd(q, k, v, kt, do, o, lse):
    S = q.shape[0]
    tq, tk = MLA_TQ, MLA_TK
    nq, nk = S // tq, S // tk

    def body(q_ref, do_ref, o_ref, lse_ref, k_ref, v_ref, kt_ref, dq_ref, dk_ref, dv_ref, dqt_ref):
        @pl.when(pl.program_id(1) == 0)
        def _():
            dk_ref[...] = jnp.zeros_like(dk_ref)
            dv_ref[...] = jnp.zeros_like(dv_ref)

        qv, dov = q_ref[...], do_ref[...]
        delta = jnp.sum((dov.astype(F32) * o_ref[...]).T, axis=0, keepdims=True)
        lse = lse_ref[0]
        dqt_ref[...] = jnp.zeros_like(dqt_ref)

        def chunk(c, carry):
            rows = pl.ds(pl.multiple_of(c * tk, tk), tk)
            kc, vc = k_ref[rows, :], v_ref[rows, :]
            pt = jnp.exp(lax.dot_general(kc, qv, _NT, preferred_element_type=F32) - lse)
            dv_ref[rows, :] += jnp.dot(pt.astype(BF16), dov, preferred_element_type=F32)
            dpt = lax.dot_general(vc, dov, _NT, preferred_element_type=F32)
            dst = (pt * (dpt - delta)).astype(BF16)
            dk_ref[rows, :] += jnp.dot(dst, qv, preferred_element_type=F32)
            dqt_ref[...] += jnp.dot(kt_ref[0, c], dst, preferred_element_type=F32)
            return carry

        lax.fori_loop(0, nk, chunk, 0, unroll=2)
        dq_ref[...] = (dqt_ref[...] * MLA_SCALE).T

    qspec = pl.BlockSpec((tq, LANES), lambda h, i: (i, h))
    kspec = pl.BlockSpec((S, LANES), lambda h, i: (0, h))
    out = jax.ShapeDtypeStruct((S, WIDTH_P), F32)
    return pl.pallas_call(
        body, name="mla_bwd", grid=(N_HEADS, nq),
        in_specs=[qspec, qspec, qspec, pl.BlockSpec((1, 1, tq), lambda h, i: (h, 0, i)), kspec, kspec,
                  pl.BlockSpec((1, nk, LANES, tk), lambda h, i: (h, 0, 0, 0))],
        out_specs=[qspec, kspec, kspec], out_shape=[out, out, out],
        scratch_shapes=[pltpu.VMEM((LANES, tq), F32)],
        compiler_params=_cparams(("arbitrary", "arbitrary")),
    )(q, do, o, lse, k, v, kt)


def _mla_bwd_prep(dq, dk, dv, tabs):
    S = dq.shape[0]

    def body(step, dq_ref, dk_ref, dv_ref, c_ref, sa_ref, sb_ref, dqp_ref, dkv_ref, dkr_ref):
        c, sa, sb = c_ref[...], sa_ref[...], sb_ref[...]
        dksum = jnp.zeros((dq_ref.shape[0], LANES), F32)
        for h in range(N_HEADS):
            blk = slice(h * LANES, (h + 1) * LANES)
            dqp_ref[:, blk] = _rope_transpose(dq_ref[:, blk], c, sa, sb).astype(BF16)
            dksum = dksum + dk_ref[:, blk]
        dkv_ref[:, 0:WIDTH_P] = dk_ref[...].astype(BF16)
        dkv_ref[:, WIDTH_P:2 * WIDTH_P] = dv_ref[...].astype(BF16)
        lane = lax.broadcasted_iota(jnp.int32, dksum.shape, 1)
        live = (lane >= HEAD_DIM) & (lane < HEAD_DIM + QK_ROPE)
        dkr_ref[...] = jnp.where(live, _rope_transpose(dksum, c, sa, sb), 0.0)

    return _rows(body, "mla_bwd_prep", S, 256, [dq, dk, dv, *tabs], [],
                 [(WIDTH_P, BF16), (2 * WIDTH_P, BF16), (LANES, F32)])


def _mla_norm_bwd(proj_b, dcq_n, dckv_n, dkr, g_q, g_kv):
    S = proj_b.shape[0]

    def body(step, p_ref, dcq_ref, dckv_ref, dkr_ref, gq_ref, gkv_ref, dp_ref, dgq_ref, dgkv_ref):
        dcq, dgq = _rms_bwd_math(p_ref[:, 0:Q_LORA], gq_ref[...], dcq_ref[...], Q_LORA)
        dckv, dgkv = _rms_bwd_math(p_ref[:, Q_LORA:Q_LORA + KV_LORA], gkv_ref[...], dckv_ref[...], KV_LORA)
        dp_ref[:, 0:Q_LORA] = dcq.astype(BF16)
        dp_ref[:, Q_LORA:Q_LORA + KV_LORA] = dckv.astype(BF16)
        dp_ref[:, Q_LORA + KV_LORA:TAIL_P] = dkr_ref[...].astype(BF16)
        _acc_add(step, dgq_ref, jnp.sum(dgq, axis=0, keepdims=True))
        _acc_add(step, dgkv_ref, jnp.sum(dgkv, axis=0, keepdims=True))

    return _rows(body, "mla_norm_bwd", S, 512, [proj_b, dcq_n, dckv_n, dkr], [g_q, g_kv], [(TAIL_P, BF16)],
                 [((1, Q_LORA), F32), ((1, KV_LORA), F32)])


def _pad_cols(w, d):
    lead = w.shape[:-1]
    w = w.reshape(lead + (N_HEADS, d))
    w = jnp.pad(w, [(0, 0)] * len(lead) + [(0, 0), (0, LANES - d)])
    return w.reshape(lead + (N_HEADS * LANES,))


def _unpad_cols(w, d):
    lead = w.shape[:-1]
    return w.reshape(lead + (N_HEADS, LANES))[..., :d].reshape(lead + (N_HEADS * d,))


def _pad_weights(w):
    w_in = w['w_in']
    zeros = lambda n: jnp.zeros((D_MODEL, n), w_in.dtype)
    p = {}
    parts = [_pad_cols(w_in[:, i * WIDTH:(i + 1) * WIDTH], HEAD_DIM).reshape(D_MODEL, N_HEADS, 1, LANES)
             for i in range(3)]
    p['w_in_a'] = jnp.concatenate(parts, axis=2).reshape(D_MODEL, N_HEADS * HEAD_COLS)
    p['w_in_b'] = jnp.concatenate([w_in[:, 3 * WIDTH:3 * WIDTH + Q_LORA + KV_LORA], zeros(HEAD_DIM),
                                   w_in[:, D_IN - QK_ROPE:], zeros(LANES - HEAD_DIM - QK_ROPE)], axis=1)
    p['w_uq'] = _pad_cols(w['w_uq'], HEAD_DIM + QK_ROPE)
    kv = w['w_ukv'].reshape(KV_LORA, N_HEADS, 2 * HEAD_DIM)
    p['w_ukv'] = jnp.concatenate([_pad_cols(kv[:, :, :HEAD_DIM].reshape(KV_LORA, WIDTH), HEAD_DIM),
                                  _pad_cols(kv[:, :, HEAD_DIM:].reshape(KV_LORA, WIDTH), HEAD_DIM)], axis=1)
    p['w_o'] = jnp.concatenate(
        [_pad_cols(w['w_o'][i * WIDTH:(i + 1) * WIDTH].T, HEAD_DIM).T for i in range(2)], axis=0)
    p['g_a'] = _pad_cols(w['out_norm_a'], HEAD_DIM)
    p['g_b'] = _pad_cols(w['out_norm_b'], HEAD_DIM)
    return p


def _unpad_grads(d):
    g = {}
    dwa = d['w_in_a'].reshape(D_MODEL, N_HEADS, 3, LANES)
    tail = d['w_in_b']
    g['w_in'] = jnp.concatenate(
        [dwa[:, :, i, :HEAD_DIM].reshape(D_MODEL, WIDTH) for i in range(3)]
        + [tail[:, :Q_LORA + KV_LORA], tail[:, Q_LORA + KV_LORA + HEAD_DIM:Q_LORA + KV_LORA + HEAD_DIM + QK_ROPE]],
        axis=1)
    g['w_uq'] = _unpad_cols(d['w_uq'], HEAD_DIM + QK_ROPE)
    dk = _unpad_cols(d['w_ukv'][:, :WIDTH_P], HEAD_DIM).reshape(KV_LORA, N_HEADS, HEAD_DIM)
    dv = _unpad_cols(d['w_ukv'][:, WIDTH_P:], HEAD_DIM).reshape(KV_LORA, N_HEADS, HEAD_DIM)
    g['w_ukv'] = jnp.concatenate([dk, dv], axis=2).reshape(KV_LORA, 2 * WIDTH)
    g['w_o'] = jnp.concatenate(
        [_unpad_cols(d['w_o'][i * WIDTH_P:(i + 1) * WIDTH_P].T, HEAD_DIM).T for i in range(2)], axis=0)
    g['out_norm_a'] = _unpad_cols(d['g_a'], HEAD_DIM)
    g['out_norm_b'] = _unpad_cols(d['g_b'], HEAD_DIM)
    return g


def _local_step(x, target, w):
    S = x.shape[0]
    p = _pad_weights(w)
    tabs = _rope_tables(S)
    bias = _slab_bias()

    h1 = _rms_fwd(x, w['norm_mix_pre'], "rms_mix_pre")
    proj_a = _mm(h1, p['w_in_a'], 'nn', F32, "mm_in_a")
    proj_b = _mm(h1, p['w_in_b'], 'nn', F32, "mm_in_b")
    oa = _dila_fwd(proj_a, bias)
    cq_n, ckv_n, kr = _mla_prep(proj_b, w['q_lat_norm'], w['kv_lat_norm'], tabs)
    q_lin = _mm(cq_n, p['w_uq'], 'nn', F32, "mm_uq")
    kv_lin = _mm(ckv_n, p['w_ukv'], 'nn', F32, "mm_ukv")
    qb, kb, vb, kt, vt = _mla_qkv(q_lin, kv_lin, kr, tabs)
    ob, lse_b = _mla_fwd(qb, kb, vt)
    cat = _outnorm_fwd(oa, ob, p['g_a'], p['g_b'])
    y = _mm(cat, p['w_o'], 'nn', F32, "mm_o")
    x1, h2 = _post_mix(x, y, w['norm_mix_post'], w['norm_ffn_pre'])
    u0 = _mm(h2, w['w_up'], 'nn', F32, "mm_up", sharded=True)
    ug, uv, a = _conv_fwd(u0, w['conv_w'], w['conv_b'])
    y2 = _mm(a, w['w_down'], 'nn', F32, "mm_down")
    dx2, dy2, dg_ffn_post, _, loss = _final(x1, y2, w['norm_ffn_post'], target)

    g = {'norm_ffn_post': dg_ffn_post}
    da = _mm(dy2, w['w_down'], 'nt', F32, "mm_down_dx")
    g['w_down'] = _mm(a, dy2, 'tn', BF16, "mm_down_dw")
    du0, g['conv_w'], g['conv_b'] = _conv_bwd(u0, ug, uv, da, w['conv_w'])
    dh2 = _mm(du0, w['w_up'], 'nt', F32, "mm_up_dx", sharded=True)
    g['w_up'] = _mm(h2, du0, 'tn', BF16, "mm_up_dw", sharded=True)
    dx1, g['norm_ffn_pre'] = _rms_bwd(x1, w['norm_ffn_pre'], [dh2], dx2, F32, "rms_ffn_pre_bwd")
    dy, g['norm_mix_post'] = _rms_bwd(y, w['norm_mix_post'], [dx1], None, BF16, "rms_mix_post_bwd")
    dcat = _mm(dy, p['w_o'], 'nt', F32, "mm_o_dx")
    dpad = {'w_o': _mm(cat, dy, 'tn', F32, "mm_o_dw")}
    dopack_a, do_b, dpad['g_a'], dpad['g_b'] = _outnorm_bwd(oa, ob, p['g_a'], p['g_b'], dcat)

    dq_b, dk_b, dv_b = _mla_bwd(qb, kb, vb, kt, do_b, ob, lse_b)
    dq_pre, dkv, dkr = _mla_bwd_prep(dq_b, dk_b, dv_b, tabs)
    dcq_n = _mm(dq_pre, p['w_uq'], 'nt', F32, "mm_uq_dx")
    dpad['w_uq'] = _mm(cq_n, dq_pre, 'tn', F32, "mm_uq_dw")
    dckv_n = _mm(dkv, p['w_ukv'], 'nt', F32, "mm_ukv_dx")
    dpad['w_ukv'] = _mm(ckv_n, dkv, 'tn', F32, "mm_ukv_dw")
    dproj_b, g['q_lat_norm'], g['kv_lat_norm'] = _mla_norm_bwd(proj_b, dcq_n, dckv_n, dkr,
                                                               w['q_lat_norm'], w['kv_lat_norm'])

    dproj_a = _dila_bwd(proj_a, dopack_a, bias)
    dh1 = _mm(dproj_b, p['w_in_b'], 'nt', F32, "mm_in_b_dx")
    dh1 = _mm(dproj_a, p['w_in_a'], 'nt', F32, "mm_in_a_dx", add=dh1)
    dpad['w_in_a'] = _mm(h1, dproj_a, 'tn', F32, "mm_in_a_dw")
    dpad['w_in_b'] = _mm(h1, dproj_b, 'tn', F32, "mm_in_b_dw")
    grad_x, g['norm_mix_pre'] = _rms_bwd(x, w['norm_mix_pre'], [dh1], dx1, F32, "rms_mix_pre_bwd")
    g.update(_unpad_grads(dpad))
    return loss, grad_x, g


MESH = pl.DeviceIdType.MESH
ANY = pl.BlockSpec(memory_space=pl.ANY)


def _place():
    x, y, c = lax.axis_index("x"), lax.axis_index("y"), lax.axis_index("c")
    chips = [(1 - x, y), (x, 1 - y), (1 - x, 1 - y)]
    return x, y, c, chips


def _all_gather(bufs):
    n = len(bufs)

    def body(*refs):
        in_refs, out_refs = refs[:n], refs[n:2 * n]
        send_sems, recv_sems, local_sems = refs[2 * n:]
        x, y, c, chips = _place()
        me = 2 * x + y
        local = [pltpu.make_async_copy(in_refs[b], out_refs[b].at[me], local_sems.at[b]) for b in range(n)]
        for cp in local:
            cp.start()
        sends = []
        for j, (px, py) in enumerate(chips):
            for b in range(n):
                sends.append(pltpu.make_async_remote_copy(
                    src_ref=in_refs[b], dst_ref=out_refs[b].at[me], send_sem=send_sems.at[j * n + b],
                    recv_sem=recv_sems.at[j * n + b], device_id=(px, py, c), device_id_type=MESH))
        for cp in sends:
            cp.start()
        for j, (px, py) in enumerate(chips):
            for b in range(n):
                pltpu.make_async_remote_copy(
                    src_ref=in_refs[b], dst_ref=out_refs[b].at[2 * px + py], send_sem=send_sems.at[j * n + b],
                    recv_sem=recv_sems.at[j * n + b], device_id=(px, py, c), device_id_type=MESH).wait_recv()
        for cp in sends:
            cp.wait_send()
        for cp in local:
            cp.wait()

    return pl.pallas_call(
        body, name="gather_weights", in_specs=[ANY] * n, out_specs=[ANY] * n,
        out_shape=[jax.ShapeDtypeStruct((N_CHIPS,) + b.shape, b.dtype) for b in bufs],
        scratch_shapes=[pltpu.SemaphoreType.DMA((3 * n,)), pltpu.SemaphoreType.DMA((3 * n,)),
                        pltpu.SemaphoreType.DMA((n,))],
    )(*bufs)


def _scatter_grads(slots):
    n = len(slots)

    def body(*refs):
        g_refs, o_refs = refs[:n], refs[n:2 * n]
        send_sems, recv_sems, local_sems = refs[2 * n:]
        x, y, c, chips = _place()
        me = 2 * x + y
        local = [pltpu.make_async_copy(g_refs[b].at[me], o_refs[b].at[me], local_sems.at[b]) for b in range(n)]
        for cp in local:
            cp.start()
        sends = [pltpu.make_async_remote_copy(
            src_ref=g_refs[b].at[2 * px + py], dst_ref=o_refs[b].at[me], send_sem=send_sems.at[j * n + b],
            recv_sem=recv_sems.at[j * n + b], device_id=(px, py, c), device_id_type=MESH)
            for j, (px, py) in enumerate(chips) for b in range(n)]
        for cp in sends:
            cp.start()
        for j, (px, py) in enumerate(chips):
            for b in range(n):
                pltpu.make_async_remote_copy(
                    src_ref=g_refs[b].at[me], dst_ref=o_refs[b].at[2 * px + py], send_sem=send_sems.at[j * n + b],
                    recv_sem=recv_sems.at[j * n + b], device_id=(px, py, c), device_id_type=MESH).wait_recv()
        for cp in sends:
            cp.wait_send()
        for cp in local:
            cp.wait()

    return pl.pallas_call(
        body, name="scatter_grads", in_specs=[ANY] * n, out_specs=[ANY] * n,
        out_shape=[jax.ShapeDtypeStruct(s.shape, s.dtype) for s in slots],
        scratch_shapes=[pltpu.SemaphoreType.DMA((3 * n,)), pltpu.SemaphoreType.DMA((3 * n,)),
                        pltpu.SemaphoreType.DMA((n,))],
    )(*slots)


ELEMENTWISE_BLOCK = 256 * 1024


def _row_tile(rows, cols):
    best = None
    for t in range(16, min(rows, max(16, ELEMENTWISE_BLOCK // cols)) + 1, 16):
        if rows % t == 0:
            best = t
    return best if best is not None else rows


def _sum_slots(recv, name):
    _, R, C = recv.shape
    tr = _row_tile(R, C)

    def body(r_ref, o_ref):
        f = lambda i: r_ref[i].astype(F32)
        o_ref[...] = ((f(0) + f(1)) + f(2)) + f(3)

    return pl.pallas_call(
        body, name="sum_" + name, grid=(R // tr,),
        in_specs=[pl.BlockSpec((N_CHIPS, tr, C), lambda i: (0, i, 0))],
        out_specs=pl.BlockSpec((tr, C), lambda i: (i, 0)),
        out_shape=jax.ShapeDtypeStruct((R, C), F32),
        compiler_params=_cparams(("parallel",)),
    )(recv)


def _swap_sibling(parts):
    n = len(parts)

    def body(*refs):
        p_refs, o_refs, send_sems, recv_sems = refs[:n], refs[n:2 * n], refs[2 * n], refs[2 * n + 1]
        x, y, c, _ = _place()
        cps = [pltpu.make_async_remote_copy(src_ref=p_refs[b], dst_ref=o_refs[b], send_sem=send_sems.at[b],
                                            recv_sem=recv_sems.at[b], device_id=(x, y, 1 - c), device_id_type=MESH)
               for b in range(n)]
        for cp in cps:
            cp.start()
        for cp in cps:
            cp.wait()

    return pl.pallas_call(
        body, name="swap_sibling", in_specs=[ANY] * n, out_specs=[ANY] * n,
        out_shape=[jax.ShapeDtypeStruct(p.shape, p.dtype) for p in parts],
        scratch_shapes=[pltpu.SemaphoreType.DMA((n,)), pltpu.SemaphoreType.DMA((n,))],
    )(*parts)


def _adamw(g0, g1, w, m, v, name, offset=0):
    R, C = w.shape
    tr = _row_tile(R, C)
    packed = g0.shape != w.shape
    bc1 = 1.0 - ADAM_B1 ** ADAM_STEP
    bc2 = 1.0 - ADAM_B2 ** ADAM_STEP

    def body(g0_ref, g1_ref, w_ref, m_ref, v_ref, g_ref, d_ref, nm_ref, nv_ref):
        if packed:
            g = g0_ref[:, offset:offset + C] + g1_ref[:, offset:offset + C]
        else:
            g = g0_ref[...] + g1_ref[...]
        g_ref[...] = g
        nm = ADAM_B1 * m_ref[...] + (1.0 - ADAM_B1) * g
        nv = ADAM_B2 * v_ref[...] + (1.0 - ADAM_B2) * (g * g)
        nm_ref[...] = nm
        nv_ref[...] = nv
        d_ref[...] = -ADAM_LR * ((nm / bc1) / (jnp.sqrt(nv / bc2) + ADAM_EPS) + ADAM_WD * w_ref[...])

    spec = pl.BlockSpec((tr, C), lambda i: (i, 0))
    gspec = pl.BlockSpec(g0.shape, lambda i: (0, 0)) if packed else spec
    out = jax.ShapeDtypeStruct((R, C), F32)
    return pl.pallas_call(
        body, name="adamw_" + name, grid=(R // tr,), in_specs=[gspec, gspec, spec, spec, spec],
        out_specs=[spec] * 4, out_shape=[out] * 4, compiler_params=_cparams(("parallel",)),
    )(g0, g1, w, m, v)


def kernel(x, norm_mix_pre, w_in, q_lat_norm, w_uq, kv_lat_norm, w_ukv, out_norm_a, out_norm_b, w_o, norm_mix_post, norm_ffn_pre, w_up, conv_w, conv_b, w_down, norm_ffn_post, loss_target, m_norm_mix_pre, m_w_in, m_q_lat_norm, m_w_uq, m_kv_lat_norm, m_w_ukv, m_out_norm_a, m_out_norm_b, m_w_o, m_norm_mix_post, m_norm_ffn_pre, m_w_up, m_conv_w, m_conv_b, m_w_down, m_norm_ffn_post, v_norm_mix_pre, v_w_in, v_q_lat_norm, v_w_uq, v_kv_lat_norm, v_w_ukv, v_out_norm_a, v_out_norm_b, v_w_o, v_norm_mix_post, v_norm_ffn_pre, v_w_up, v_conv_w, v_conv_b, v_w_down, v_norm_ffn_post):
    args = dict(locals())
    strip = lambda a: a[0] if a.ndim == 3 else a
    wl = {n: strip(args[n]) for n in WEIGHTS}
    ml = {n: strip(args['m_' + n]) for n in WEIGHTS}
    vl = {n: strip(args['v_' + n]) for n in WEIGHTS}

    gathered = dict(zip(BIG + ['conv_w'], _all_gather([wl[n].astype(BF16) for n in BIG] + [wl['conv_w']])))
    full = {n: wl[n] for n in SMALL}
    for n in ('w_in', 'w_uq', 'w_ukv', 'conv_w'):
        full[n] = jnp.concatenate([gathered[n][i] for i in range(N_CHIPS)], axis=1)
    full['w_o'] = gathered['w_o'].reshape(D_MODEL, D_MODEL)
    full['w_down'] = gathered['w_down'].reshape(D_FF, D_MODEL)
    full['w_up'] = gathered['w_up']

    loss_b, grad_x, g = _local_step(x[0], loss_target[0], full)

    sharded = [n for n in WEIGHTS if SHARD_AXIS[n] is not None]

    def slots_of(n):
        a = g[n].astype(BF16)
        if n == 'w_up':
            return a
        if SHARD_AXIS[n] == 0:
            return a.reshape(N_CHIPS, a.shape[0] // N_CHIPS, a.shape[1])
        cols = a.shape[1] // N_CHIPS
        return jnp.stack([a[:, i * cols:(i + 1) * cols] for i in range(N_CHIPS)])

    small_pack = jnp.concatenate([g[n] for n in SMALL], axis=1)
    slots = [slots_of(n) for n in sharded] + [jnp.broadcast_to(small_pack[None], (N_CHIPS,) + small_pack.shape)]
    recv = _scatter_grads(slots)
    parts = [_sum_slots(r, n) for r, n in zip(recv, sharded + ['small'])]
    others = _swap_sibling(parts)

    outs = {}

    def record(n, results):
        for tag, a in zip(('grad', 'delta', 'new_m', 'new_v'), results):
            outs[tag + '_' + n] = a.reshape(args[n].shape)

    for n, p0, p1 in zip(sharded, parts, others):
        record(n, _adamw(p0, p1, wl[n], ml[n], vl[n], n))
    offset = 0
    for n in SMALL:
        record(n, _adamw(parts[-1], others[-1], wl[n], ml[n], vl[n], n, offset=offset))
        offset += wl[n].shape[1]

    loss = lax.psum(loss_b[0, 0], ("x", "y", "c"))
    return (loss, grad_x[None], *[outs['grad_' + n] for n in WEIGHTS], *[outs['delta_' + n] for n in WEIGHTS],
            *[outs['new_m_' + n] for n in WEIGHTS], *[outs['new_v_' + n] for n in WEIGHTS])
```

```python
import functools
import math

import jax
import jax.numpy as jnp
import numpy as np
from jax import lax
from jax.experimental import pallas as pl
from jax.experimental.pallas import tpu as pltpu

F32 = jnp.float32
BF16 = jnp.bfloat16

LANES = 128
D_MODEL = 1024
N_HEADS = 8
HEAD_DIM = 64
QK_ROPE = 32
Q_LORA = 384
KV_LORA = 256
D_FF = 2816
WIDTH = N_HEADS * HEAD_DIM
WIDTH_P = N_HEADS * LANES
IN_SIZES = (WIDTH, WIDTH, WIDTH, Q_LORA, KV_LORA, QK_ROPE)
D_IN = sum(IN_SIZES)
TAIL_P = Q_LORA + KV_LORA + LANES
EPS = 1e-6
ROPE_BASE = 10000.0
MASKED = -2e30
M_INIT = -1e30
MLA_TQ = 4096
MLA_TK = 256
MLA_SCALE = (HEAD_DIM + QK_ROPE) ** -0.5
DIL_SCALE = HEAD_DIM ** -0.5

ADAM_LR = 0.001
ADAM_B1 = 0.9
ADAM_B2 = 0.999
ADAM_EPS = 1e-08
ADAM_WD = 0.01
ADAM_STEP = 10

VMEM_LIMIT = 56 * 1024 * 1024

N_CHIPS = 4

WEIGHTS = ['norm_mix_pre', 'w_in', 'q_lat_norm', 'w_uq', 'kv_lat_norm', 'w_ukv', 'out_norm_a', 'out_norm_b',
           'w_o', 'norm_mix_post', 'norm_ffn_pre', 'w_up', 'conv_w', 'conv_b', 'w_down', 'norm_ffn_post']
SHARD_AXIS = {'norm_mix_pre': None, 'w_in': 1, 'q_lat_norm': None, 'w_uq': 1, 'kv_lat_norm': None, 'w_ukv': 1,
              'out_norm_a': None, 'out_norm_b': None, 'w_o': 0, 'norm_mix_post': None, 'norm_ffn_pre': None,
              'w_up': 1, 'conv_w': 1, 'conv_b': None, 'w_down': 0, 'norm_ffn_post': None}
BIG = ['w_in', 'w_uq', 'w_ukv', 'w_o', 'w_up', 'w_down']
SMALL = [n for n in WEIGHTS if SHARD_AXIS[n] is None]


def _tile(dim, target):
    best = None
    t = LANES
    while t <= min(dim, target):
        if dim % t == 0:
            best = t
        t += LANES
    return best if best is not None else dim


def _cparams(sem=None):
    return pltpu.CompilerParams(dimension_semantics=sem, vmem_limit_bytes=VMEM_LIMIT)


def _mm(a, b, mode, out_dtype, name, add=None, tm=1024, tn=1024, tk=1024, sharded=False):
    if mode == 'nn':
        (M, K), (K2, N) = a.shape, ((b.shape[1], N_CHIPS * b.shape[2]) if sharded else b.shape)
        dims = (((1,), (0,)), ((), ()))
    elif mode == 'nt':
        (M, K), (N, K2) = a.shape, ((b.shape[1], N_CHIPS * b.shape[2]) if sharded else b.shape)
        dims = (((1,), (1,)), ((), ()))
    else:
        (K, M), (K2, N) = a.shape, b.shape
        dims = (((0,), (0,)), ((), ()))
    assert K == K2, (a.shape, b.shape, mode)
    tm, tn, tk = _tile(M, tm), _tile(N, tn), _tile(K, tk)
    if K == D_FF:
        tk = K
    if N == D_FF:
        tn, tm = N, min(tm, 512)
    if M == D_FF:
        tm = M
    if sharded and mode == 'nt':
        tk = K // N_CHIPS
    elif sharded:
        tn = N // N_CHIPS
    nk = K // tk
    if mode == 'nn':
        a_spec = pl.BlockSpec((tm, tk), lambda i, j, k: (i, k))
        b_spec = (pl.BlockSpec((None, tk, tn), lambda i, j, k: (j, k, 0)) if sharded
                  else pl.BlockSpec((tk, tn), lambda i, j, k: (k, j)))
    elif mode == 'nt':
        a_spec = pl.BlockSpec((tm, tk), lambda i, j, k: (i, k))
        b_spec = (pl.BlockSpec((None, tn, tk), lambda i, j, k: (k, j, 0)) if sharded
                  else pl.BlockSpec((tn, tk), lambda i, j, k: (j, k)))
    else:
        a_spec = pl.BlockSpec((tk, tm), lambda i, j, k: (k, i))
        b_spec = pl.BlockSpec((tk, tn), lambda i, j, k: (k, j))
    o_spec = pl.BlockSpec((tm, tn), lambda i, j, k: (i, j))
    out_shape = jax.ShapeDtypeStruct((M, N), out_dtype)
    if sharded and mode == 'tn':
        o_spec = pl.BlockSpec((None, tm, tn), lambda i, j, k: (j, i, 0))
        out_shape = jax.ShapeDtypeStruct((N_CHIPS, M, tn), out_dtype)
    has_add = add is not None

    def body(*refs):
        if has_add:
            a_ref, b_ref, add_ref, o_ref, acc_ref = refs
        else:
            a_ref, b_ref, o_ref, acc_ref = refs
        k = pl.program_id(2)

        @pl.when(k == 0)
        def _():
            acc_ref[...] = jnp.zeros_like(acc_ref)

        acc_ref[...] += lax.dot_general(a_ref[...].astype(BF16), b_ref[...].astype(BF16), dims,
                                        preferred_element_type=F32)

        @pl.when(k == nk - 1)
        def _():
            r = acc_ref[...]
            if has_add:
                r = r + add_ref[...]
            o_ref[...] = r.astype(o_ref.dtype)

    ins = [a, b] + ([add] if has_add else [])
    in_specs = [a_spec, b_spec] + ([o_spec] if has_add else [])
    return pl.pallas_call(
        body, name=name, grid=(M // tm, N // tn, nk), in_specs=in_specs, out_specs=o_spec, out_shape=out_shape,
        scratch_shapes=[pltpu.VMEM((tm, tn), F32)],
        compiler_params=_cparams(("parallel", "parallel", "arbitrary")),
    )(*ins)


def _rows(body, name, S, ts, row_ins, full_ins, row_outs, acc_outs=(), chunk_outs=(), t_outs=(), chunk_ins=()):
    in_specs = [pl.BlockSpec((ts, a.shape[1]), lambda i: (i, 0)) for a in row_ins]
    in_specs += [pl.BlockSpec((a.shape[0], 1, LANES, ts), lambda i: (0, i, 0, 0)) for a in chunk_ins]
    in_specs += [pl.BlockSpec(a.shape, lambda i, nd=a.ndim: (0,) * nd) for a in full_ins]
    out_specs = [pl.BlockSpec((ts, w), lambda i: (i, 0)) for (w, _) in row_outs]
    out_specs += [pl.BlockSpec(shape, lambda i, nd=len(shape): (0,) * nd) for (shape, _) in acc_outs]
    out_specs += [pl.BlockSpec((lead, 1, LANES, ts), lambda i: (0, i, 0, 0)) for (lead, _) in chunk_outs]
    out_specs += [pl.BlockSpec((lead, LANES, ts), lambda i: (0, 0, i)) for (lead, _) in t_outs]
    out_shape = [jax.ShapeDtypeStruct((S, w), dt) for (w, dt) in row_outs]
    out_shape += [jax.ShapeDtypeStruct(shape, dt) for (shape, dt) in acc_outs]
    out_shape += [jax.ShapeDtypeStruct((lead, S // ts, LANES, ts), dt) for (lead, dt) in chunk_outs]
    out_shape += [jax.ShapeDtypeStruct((lead, LANES, S), dt) for (lead, dt) in t_outs]

    def kbody(*refs):
        body(pl.program_id(0), *refs)

    return pl.pallas_call(
        kbody, name=name, grid=(S // ts,), in_specs=in_specs, out_specs=out_specs, out_shape=out_shape,
        compiler_params=_cparams(("arbitrary",)),
    )(*row_ins, *chunk_ins, *full_ins)


def _acc_add(step, ref, val):
    @pl.when(step == 0)
    def _():
        ref[...] = val

    @pl.when(step != 0)
    def _():
        ref[...] += val


def _rms_fwd(x, g, name):
    S, W = x.shape

    def body(step, x_ref, g_ref, h_ref):
        xv = x_ref[...]
        r = lax.rsqrt(jnp.mean(xv * xv, axis=-1, keepdims=True) + EPS)
        h_ref[...] = (xv * r * g_ref[...]).astype(BF16)

    return _rows(body, name, S, 512, [x], [g], [(W, BF16)])[0]


def _rms_bwd_math(xv, g, dy, width):
    r = lax.rsqrt(jnp.sum(xv * xv, axis=-1, keepdims=True) * (1.0 / width) + EPS)
    xn = xv * r
    dyg = dy * g
    dx = r * (dyg - xn * (jnp.sum(dyg * xn, axis=-1, keepdims=True) * (1.0 / width)))
    return dx, dy * xn


def _rms_bwd(x, g, dys, resid, out_dtype, name):
    S, W = x.shape
    nd = len(dys)
    has_res = resid is not None

    def body(step, *refs):
        x_ref = refs[0]
        dy_refs = refs[1:1 + nd]
        pos = 1 + nd
        res_ref = refs[pos] if has_res else None
        pos += int(has_res)
        g_ref, dx_ref, dg_ref = refs[pos], refs[pos + 1], refs[pos + 2]
        dy = dy_refs[0][...].astype(F32)
        for r_ in dy_refs[1:]:
            dy = dy + r_[...].astype(F32)
        dx, dgr = _rms_bwd_math(x_ref[...], g_ref[...], dy, W)
        if has_res:
            dx = dx + res_ref[...]
        dx_ref[...] = dx.astype(dx_ref.dtype)
        _acc_add(step, dg_ref, jnp.sum(dgr, axis=0, keepdims=True))

    row_ins = [x] + list(dys) + ([resid] if has_res else [])
    dx, dg = _rows(body, name, S, 256, row_ins, [g], [(W, out_dtype)], [((1, W), F32)])
    return dx, dg


def _rope_apply(xv, c, sa, sb):
    return xv * c + pltpu.roll(xv, 16, 1) * sa + pltpu.roll(xv, LANES - 16, 1) * sb


def _rope_transpose(dy, c, sa, sb):
    return dy * c + pltpu.roll(dy * sa, LANES - 16, 1) + pltpu.roll(dy * sb, 16, 1)


def _rope_tables(S):
    pos = jnp.arange(S, dtype=F32)
    inv_freq = jnp.exp(-math.log(ROPE_BASE) * jnp.arange(0, QK_ROPE, 2, dtype=F32) / QK_ROPE)
    ang = pos[:, None] * inv_freq[None, :]
    cos, sin = jnp.cos(ang), jnp.sin(ang)
    ones, zeros = jnp.ones((S, HEAD_DIM), F32), jnp.zeros((S, HEAD_DIM), F32)
    z16, z32 = jnp.zeros((S, 16), F32), jnp.zeros((S, 32), F32)
    c = jnp.concatenate([ones, cos, cos, z32], axis=1)
    sa = jnp.concatenate([zeros, z16, sin, z32], axis=1)
    sb = jnp.concatenate([zeros, -sin, z16, z32], axis=1)
    return c, sa, sb


def _mla_prep(proj_b, g_q, g_kv, tabs):
    S = proj_b.shape[0]

    def body(step, p_ref, c_ref, sa_ref, sb_ref, gq_ref, gkv_ref, cq_ref, ckv_ref, kr_ref):
        cq = p_ref[:, 0:Q_LORA]
        ckv = p_ref[:, Q_LORA:Q_LORA + KV_LORA]
        kr = p_ref[:, Q_LORA + KV_LORA:TAIL_P]
        rq = lax.rsqrt(jnp.mean(cq * cq, axis=-1, keepdims=True) + EPS)
        cq_ref[...] = (cq * rq * gq_ref[...]).astype(BF16)
        rk = lax.rsqrt(jnp.mean(ckv * ckv, axis=-1, keepdims=True) + EPS)
        ckv_ref[...] = (ckv * rk * gkv_ref[...]).astype(BF16)
        kr_ref[...] = _rope_apply(kr, c_ref[...], sa_ref[...], sb_ref[...])

    return _rows(body, "mla_prep", S, 512, [proj_b, *tabs], [g_q, g_kv],
                 [(Q_LORA, BF16), (KV_LORA, BF16), (LANES, F32)])


def _mla_qkv(q, kv, kr, tabs):
    S = q.shape[0]

    def body(step, q_ref, kv_ref, kr_ref, c_ref, sa_ref, sb_ref, qb_ref, kb_ref, vb_ref, kt_ref, vt_ref):
        c, sa, sb = c_ref[...], sa_ref[...], sb_ref[...]
        krv = kr_ref[...]
        row = lax.broadcasted_iota(jnp.int32, (LANES, MLA_TK), 0)
        for h in range(N_HEADS):
            blk = slice(h * LANES, (h + 1) * LANES)
            qb_ref[:, blk] = (_rope_apply(q_ref[:, blk], c, sa, sb) * MLA_SCALE).astype(BF16)
            kh = kv_ref[:, blk] + krv
            kb_ref[:, blk] = kh.astype(BF16)
            kt_ref[h, 0] = kh.T.astype(BF16)
            vh = kv_ref[:, WIDTH_P + h * LANES:WIDTH_P + (h + 1) * LANES]
            vt_ref[h, 0] = jnp.where(row == HEAD_DIM, 1.0, vh.T).astype(BF16)
        vb_ref[...] = kv_ref[:, WIDTH_P:2 * WIDTH_P].astype(BF16)

    return _rows(body, "mla_qkv", S, MLA_TK, [q, kv, kr, *tabs], [],
                 [(WIDTH_P, BF16), (WIDTH_P, BF16), (WIDTH_P, BF16)],
                 chunk_outs=[(N_HEADS, BF16), (N_HEADS, BF16)])


def _outnorm_fwd(oa, ob, ga, gb):
    S = oa.shape[0]

    def body(step, oa_ref, ob_ref, ga_ref, gb_ref, cat_ref):
        live = lax.broadcasted_iota(jnp.int32, oa_ref.shape, 1) % LANES < HEAD_DIM
        for o_ref, g_ref, off in ((oa_ref, ga_ref, 0), (ob_ref, gb_ref, WIDTH_P)):
            o = jnp.where(live, o_ref[...], 0.0)
            r = lax.rsqrt(jnp.sum(o * o, axis=-1, keepdims=True) * (1.0 / WIDTH) + EPS)
            cat_ref[:, off:off + WIDTH_P] = (o * r * g_ref[...]).astype(BF16)

    return _rows(body, "outnorm_fwd", S, 256, [oa, ob], [ga, gb], [(2 * WIDTH_P, BF16)])[0]


def _outnorm_bwd(oa, ob, ga, gb, dcat):
    S = oa.shape[0]

    def body(step, oa_ref, ob_ref, dcat_ref, ga_ref, gb_ref, dpa_ref, dob_ref, dga_ref, dgb_ref):
        live = lax.broadcasted_iota(jnp.int32, oa_ref.shape, 1) % LANES < HEAD_DIM
        lane = lax.broadcasted_iota(jnp.int32, (oa_ref.shape[0], LANES), 1)
        packed = oa_ref[...]
        o = jnp.where(live, packed, 0.0)
        do, dgr = _rms_bwd_math(o, ga_ref[...], dcat_ref[:, 0:WIDTH_P], WIDTH)
        _acc_add(step, dga_ref, jnp.sum(dgr, axis=0, keepdims=True))
        prod = do.astype(BF16).astype(F32) * o
        for h in range(N_HEADS):
            blk = slice(h * LANES, (h + 1) * LANES)
            delta = jnp.sum(prod[:, blk], axis=-1, keepdims=True)
            lse = jnp.sum(jnp.where(lane == HEAD_DIM, packed[:, blk], 0.0), axis=-1, keepdims=True)
            out = do[:, blk]
            for k, piece in enumerate(_split3(-delta) + _split3(-lse)):
                out = jnp.where(lane == HEAD_DIM + k, piece, out)
            dpa_ref[:, blk] = out

        ov = ob_ref[...]
        do_b, dgr_b = _rms_bwd_math(ov, gb_ref[...], dcat_ref[:, WIDTH_P:2 * WIDTH_P], WIDTH)
        dob_ref[...] = do_b.astype(BF16)
        _acc_add(step, dgb_ref, jnp.sum(dgr_b, axis=0, keepdims=True))

    return _rows(body, "outnorm_bwd", S, 256, [oa, ob, dcat], [ga, gb],
                 [(WIDTH_P, F32), (WIDTH_P, BF16)], [((1, WIDTH_P), F32), ((1, WIDTH_P), F32)])


def _post_mix(x, y, g_post, g_pre):
    S, W = x.shape

    def body(step, x_ref, y_ref, gp_ref, gq_ref, x1_ref, h_ref):
        yv = y_ref[...]
        r = lax.rsqrt(jnp.mean(yv * yv, axis=-1, keepdims=True) + EPS)
        x1 = x_ref[...] + yv * r * gp_ref[...]
        x1_ref[...] = x1
        r1 = lax.rsqrt(jnp.mean(x1 * x1, axis=-1, keepdims=True) + EPS)
        h_ref[...] = (x1 * r1 * gq_ref[...]).astype(BF16)

    return _rows(body, "post_mix", S, 512, [x, y], [g_post, g_pre], [(W, F32), (W, BF16)])


def _final(x1, y2, g, target):
    S, W = x1.shape
    nsteps = S // 256

    def body(step, x1_ref, y_ref, t_ref, g_ref, dx2_ref, dy_ref, dg_ref, sq_ref, loss_ref):
        yv = y_ref[...]
        gv = g_ref[...]
        r = lax.rsqrt(jnp.mean(yv * yv, axis=-1, keepdims=True) + EPS)
        yn = yv * r
        err = (x1_ref[...] + yn * gv) - t_ref[...]
        dx2 = err * (1.0 / W)
        dx2_ref[...] = dx2
        dyg = dx2 * gv
        dy = r * (dyg - yn * jnp.mean(dyg * yn, axis=-1, keepdims=True))
        dy_ref[...] = dy.astype(BF16)
        _acc_add(step, dg_ref, jnp.sum(dx2 * yn, axis=0, keepdims=True))
        _acc_add(step, sq_ref, jnp.sum(err * err, axis=0, keepdims=True))

        @pl.when(step == nsteps - 1)
        def _():
            tot = jnp.sum(sq_ref[...], axis=-1, keepdims=True) * (0.5 / W)
            loss_ref[...] = jnp.broadcast_to(tot, (1, LANES))

    return _rows(body, "final_loss", S, 256, [x1, y2, target], [g], [(W, F32), (W, BF16)],
                 [((1, W), F32), ((1, W), F32), ((1, LANES), F32)])


_GELU_C = math.sqrt(2.0 / math.pi)
_CONV_CHUNK = 128
_HALO = 8


def _gelu(g):
    t = jnp.tanh(_GELU_C * (g + 0.044715 * (g * g * g)))
    return g * (0.5 * (1.0 + t)), t


def _fill_padded(pad_ref, src_ref, S):
    zeros = jnp.zeros((_HALO, LANES), F32)
    pad_ref[0:_HALO, :] = zeros
    pad_ref[_HALO + S:2 * _HALO + S, :] = zeros
    for r0 in range(0, S, _CONV_CHUNK):
        pad_ref[_HALO + r0:_HALO + r0 + _CONV_CHUNK, :] = src_ref[r0:r0 + _CONV_CHUNK, :].astype(F32)


def _conv_fwd(u0, conv_w, conv_b):
    S, C2 = u0.shape
    nb = D_FF // LANES

    def body(u0g_ref, u0v_ref, wg_ref, wv_ref, bg_ref, bv_ref, ug_ref, uv_ref, a_ref, pg_ref, pv_ref):
        _fill_padded(pg_ref, u0g_ref, S)
        _fill_padded(pv_ref, u0v_ref, S)
        wg, wv = wg_ref[...], wv_ref[...]
        for r0 in range(0, S, _CONV_CHUNK):
            def conv(p_ref, w, b_ref):
                base = _HALO + r0
                return (p_ref[base - 1:base - 1 + _CONV_CHUNK, :] * w[0:1, :]
                        + p_ref[base:base + _CONV_CHUNK, :] * w[1:2, :]
                        + p_ref[base + 1:base + 1 + _CONV_CHUNK, :] * w[2:3, :] + b_ref[...])
            g = conv(pg_ref, wg, bg_ref)
            v = conv(pv_ref, wv, bv_ref)
            rows = slice(r0, r0 + _CONV_CHUNK)
            ug_ref[rows, :] = g
            uv_ref[rows, :] = v
            a_ref[rows, :] = (_gelu(g)[0] * v).astype(BF16)

    col = lambda off: pl.BlockSpec((S, LANES), lambda j: (0, j + off))
    wcol = lambda off: pl.BlockSpec((3, LANES), lambda j: (0, j + off))
    bcol = lambda off: pl.BlockSpec((1, LANES), lambda j: (0, j + off))
    ug, uv, a = pl.pallas_call(
        body, name="conv_gelu_fwd", grid=(nb,),
        in_specs=[col(0), col(nb), wcol(0), wcol(nb), bcol(0), bcol(nb)],
        out_specs=[col(0), col(0), col(0)],
        out_shape=[jax.ShapeDtypeStruct((S, D_FF), F32), jax.ShapeDtypeStruct((S, D_FF), F32),
                   jax.ShapeDtypeStruct((S, D_FF), BF16)],
        scratch_shapes=[pltpu.VMEM((S + 2 * _HALO, LANES), F32), pltpu.VMEM((S + 2 * _HALO, LANES), F32)],
        compiler_params=_cparams(("arbitrary",)),
    )(u0, u0, conv_w, conv_w, conv_b, conv_b)
    return ug, uv, a


def _conv_bwd(u0, ug, uv, da, conv_w):
    S = u0.shape[0]
    nb = D_FF // LANES

    def body(u0_ref, ug_ref, uv_ref, da_ref, w_ref, du0_ref, dw_ref, db_ref, pu_ref, pd_ref):
        is_g = pl.program_id(1) == 0
        _fill_padded(pu_ref, u0_ref, S)
        zeros = jnp.zeros((_HALO, LANES), F32)
        pd_ref[0:_HALO, :] = zeros
        pd_ref[_HALO + S:2 * _HALO + S, :] = zeros
        @pl.when(is_g)
        def _():
            for r0 in range(0, S, _CONV_CHUNK):
                rows = slice(r0, r0 + _CONV_CHUNK)
                g = ug_ref[rows, :]
                t = _gelu(g)[1]
                dgel = 0.5 * (1.0 + t) + (0.5 * g) * (1.0 - t * t) * (_GELU_C * (1.0 + 3.0 * 0.044715 * (g * g)))
                pd_ref[_HALO + r0:_HALO + r0 + _CONV_CHUNK, :] = da_ref[rows, :] * uv_ref[rows, :] * dgel

        @pl.when(jnp.logical_not(is_g))
        def _():
            for r0 in range(0, S, _CONV_CHUNK):
                rows = slice(r0, r0 + _CONV_CHUNK)
                pd_ref[_HALO + r0:_HALO + r0 + _CONV_CHUNK, :] = da_ref[rows, :] * _gelu(ug_ref[rows, :])[0]
        w = w_ref[...]
        acc_b = jnp.zeros((1, LANES), F32)
        acc_w = [jnp.zeros((1, LANES), F32) for _ in range(3)]
        for r0 in range(0, S, _CONV_CHUNK):
            base = _HALO + r0
            du_m = pd_ref[base - 1:base - 1 + _CONV_CHUNK, :]
            du_c = pd_ref[base:base + _CONV_CHUNK, :]
            du_p = pd_ref[base + 1:base + 1 + _CONV_CHUNK, :]
            du0_ref[r0:r0 + _CONV_CHUNK, :] = (du_p * w[0:1, :] + du_c * w[1:2, :] + du_m * w[2:3, :]).astype(BF16)
            acc_b = acc_b + jnp.sum(du_c, axis=0, keepdims=True)
            for k in range(3):
                acc_w[k] = acc_w[k] + jnp.sum(du_c * pu_ref[base + k - 1:base + k - 1 + _CONV_CHUNK, :],
                                              axis=0, keepdims=True)
        db_ref[...] = acc_b
        for k in range(3):
            dw_ref[k:k + 1, :] = acc_w[k]

    own = pl.BlockSpec((S, LANES), lambda j, half: (0, half * nb + j))
    shared = pl.BlockSpec((S, LANES), lambda j, half: (0, j))
    du0, dw, db = pl.pallas_call(
        body, name="conv_gelu_bwd", grid=(nb, 2),
        in_specs=[own, shared, shared, shared, pl.BlockSpec((3, LANES), lambda j, half: (0, half * nb + j))],
        out_specs=[own, pl.BlockSpec((3, LANES), lambda j, half: (0, half * nb + j)),
                   pl.BlockSpec((1, LANES), lambda j, half: (0, half * nb + j))],
        out_shape=[jax.ShapeDtypeStruct((S, 2 * D_FF), BF16), jax.ShapeDtypeStruct((3, 2 * D_FF), F32),
                   jax.ShapeDtypeStruct((1, 2 * D_FF), F32)],
        scratch_shapes=[pltpu.VMEM((S + 2 * _HALO, LANES), F32), pltpu.VMEM((S + 2 * _HALO, LANES), F32)],
        compiler_params=_cparams(("arbitrary", "arbitrary")),
    )(u0, ug, uv, da, conv_w)
    return du0, dw, db


DIL_Q = 128
DIL_HALF = 64
DIL_SLAB = DIL_Q + 2 * DIL_HALF
DILATIONS = (1, 4, 16)
DIL_SEG = 2048


def _dil_bias(r):
    row = jnp.arange(DIL_Q, dtype=jnp.int32)[:, None]
    col = jnp.arange(DIL_SLAB, dtype=jnp.int32)[None, :]
    ad = jnp.abs(col - DIL_HALF - row)
    slopes = jnp.exp2(-8.0 * jnp.arange(1, N_HEADS + 1, dtype=F32) / N_HEADS)
    base = jnp.where(ad <= DIL_HALF, -slopes[:, None, None] * (ad * r).astype(F32)[None], MASKED)
    before = jnp.broadcast_to(col < DIL_HALF, (DIL_Q, DIL_SLAB))
    after = jnp.broadcast_to(col >= DIL_Q + DIL_HALF, (DIL_Q, DIL_SLAB))
    variants = [base, jnp.where(before, MASKED, base), jnp.where(after, MASKED, base),
                jnp.where(before | after, MASKED, base)]
    return jnp.stack(variants, axis=1)


def _lanes_hi_to_all(x):
    lane = lax.broadcasted_iota(jnp.int32, x.shape, 1)
    return jnp.where(lane < HEAD_DIM, pltpu.roll(x, HEAD_DIM, 1), x)


def _fill_kv(kp_ref, vp_ref, k_ref, v_ref, L, ones):
    zeros = jnp.zeros((DIL_HALF, LANES), BF16)
    for ref in (kp_ref, vp_ref):
        ref[0:DIL_HALF, :] = zeros
        ref[DIL_HALF + L:2 * DIL_HALF + L, :] = zeros
    step = min(L, 512)
    lane = lax.broadcasted_iota(jnp.int32, (step, LANES), 1)
    for r0 in range(0, L, step):
        kp_ref[DIL_HALF + r0:DIL_HALF + r0 + step, :] = k_ref[r0:r0 + step, :]
        vv = v_ref[r0:r0 + step, :]
        vp_ref[DIL_HALF + r0:DIL_HALF + r0 + step, :] = jnp.where(lane < HEAD_DIM, vv, 1.0).astype(BF16) if ones else vv


def _dil_fwd(proj_a, bias, r):
    S = proj_a.shape[0]
    L = S // r
    nblk = L // DIL_Q
    pv = proj_a.reshape(L, r * 3 * WIDTH_P)

    def body(q_ref, k_ref, v_ref, b_ref, o_ref, kp_ref, vp_ref):
        _fill_kv(kp_ref, vp_ref, k_ref, v_ref, L, True)
        lane = lax.broadcasted_iota(jnp.int32, (DIL_Q, LANES), 1)

        def block(i, carry):
            rows = pl.ds(pl.multiple_of(i * DIL_Q, DIL_Q), DIL_Q)
            slab = pl.ds(pl.multiple_of(i * DIL_Q, DIL_Q), DIL_SLAB)
            variant = jnp.where(i == 0, 1, 0) + jnp.where(i == nblk - 1, 2, 0)
            qv = q_ref[rows, :] * DIL_SCALE
            s = lax.dot_general(qv, kp_ref[slab, :], _NT, preferred_element_type=F32) + b_ref[0, variant]
            m = jnp.max(s, axis=-1, keepdims=True)
            acc = jnp.dot(jnp.exp(s - m).astype(BF16), vp_ref[slab, :], preferred_element_type=F32)
            l = _lanes_hi_to_all(acc)
            o_ref[rows, :] = jnp.where(lane < HEAD_DIM, acc / l, m + jnp.log(l))
            return carry

        lax.fori_loop(0, nblk, block, 0, unroll=min(4, nblk))

    col = lambda part: pl.BlockSpec((L, LANES), lambda c, h: (0, c * 3 * N_HEADS + part * N_HEADS + h))
    out = pl.pallas_call(
        body, name="dil_fwd_r%d" % r, grid=(r, N_HEADS),
        in_specs=[col(0), col(1), col(2), pl.BlockSpec((1, 4, DIL_Q, DIL_SLAB), lambda c, h: (h, 0, 0, 0))],
        out_specs=pl.BlockSpec((L, LANES), lambda c, h: (0, c * N_HEADS + h)),
        out_shape=jax.ShapeDtypeStruct((L, r * WIDTH_P), F32),
        scratch_shapes=[pltpu.VMEM((L + 2 * DIL_HALF, LANES), BF16), pltpu.VMEM((L + 2 * DIL_HALF, LANES), BF16)],
        compiler_params=_cparams(("parallel", "parallel")),
    )(pv, pv, pv, bias)
    return out.reshape(S, WIDTH_P)


def _dil_combine(branches):
    S = branches[0].shape[0]

    def body(step, *refs):
        o_ref, lse_ref = refs[-2], refs[-1]
        for h in range(N_HEADS):
            blk = slice(h * LANES, (h + 1) * LANES)
            xs = [r_[:, blk] for r_ in refs[:-2]]
            lses = [_lanes_hi_to_all(x) for x in xs]
            m = functools.reduce(jnp.maximum, lses)
            ws = [jnp.exp(l - m) for l in lses]
            tot = functools.reduce(jnp.add, ws)
            lane = lax.broadcasted_iota(jnp.int32, xs[0].shape, 1)
            o = functools.reduce(jnp.add, [w * x for w, x in zip(ws, xs)]) / tot
            o_ref[:, blk] = jnp.where(lane < HEAD_DIM, o, 0.0)
            lse_ref[:, blk] = m + jnp.log(tot)

    return _rows(body, "dil_combine", S, 256, list(branches), [], [(WIDTH_P, F32), (WIDTH_P, F32)])


def _dil_bwd(proj_a, do, lse, delta, bias, r):
    S = proj_a.shape[0]
    L = S // r
    seg = min(L, DIL_SEG)
    nseg, nblk, nblk_seg = L // seg, L // DIL_Q, seg // DIL_Q
    pv = proj_a.reshape(L, r * 3 * WIDTH_P)
    view = lambda a: a.reshape(L, r * WIDTH_P)
    _TN = (((0,), (0,)), ((), ()))

    def body(q_ref, k_ref, v_ref, do_ref, lse_ref, dl_ref, b_ref, dq_ref, dk_ref, dv_ref,
             kp_ref, vp_ref, dkp_ref, dvp_ref):
        sg = pl.program_id(2)

        @pl.when(sg == 0)
        def _():
            _fill_kv(kp_ref, vp_ref, k_ref, v_ref, L, False)
            dkp_ref[...] = jnp.zeros_like(dkp_ref)
            dvp_ref[...] = jnp.zeros_like(dvp_ref)

        def block(j, carry):
            i = sg * nblk_seg + j
            rows = pl.ds(pl.multiple_of(j * DIL_Q, DIL_Q), DIL_Q)
            slab = pl.ds(pl.multiple_of(i * DIL_Q, DIL_Q), DIL_SLAB)
            variant = jnp.where(i == 0, 1, 0) + jnp.where(i == nblk - 1, 2, 0)
            qv = q_ref[rows, :] * DIL_SCALE
            dov = do_ref[rows, :]
            ks, vs = kp_ref[slab, :], vp_ref[slab, :]
            two = lambda a: jnp.concatenate([a, a], axis=1)
            s = lax.dot_general(qv, ks, _NT, preferred_element_type=F32) + b_ref[0, variant]
            p = jnp.exp(s - two(lse_ref[rows, :]))
            dp = lax.dot_general(dov, vs, _NT, preferred_element_type=F32)
            ds = (p * (dp - two(dl_ref[rows, :]))).astype(BF16)
            dq_ref[rows, :] = jnp.dot(ds, ks, preferred_element_type=F32) * DIL_SCALE
            dkp_ref[slab, :] += lax.dot_general(ds, qv, _TN, preferred_element_type=F32)
            dvp_ref[slab, :] += lax.dot_general(p.astype(BF16), dov, _TN, preferred_element_type=F32)
            return carry

        lax.fori_loop(0, nblk_seg, block, 0, unroll=min(2, nblk_seg))

        @pl.when(sg == nseg - 1)
        def _():
            dk_ref[...] = dkp_ref[DIL_HALF:DIL_HALF + L, :]
            dv_ref[...] = dvp_ref[DIL_HALF:DIL_HALF + L, :]

    col = lambda part: pl.BlockSpec((L, LANES), lambda c, h, s: (0, c * 3 * N_HEADS + part * N_HEADS + h))
    segspec = pl.BlockSpec((seg, LANES), lambda c, h, s: (s, c * N_HEADS + h))
    fullspec = pl.BlockSpec((L, LANES), lambda c, h, s: (0, c * N_HEADS + h))
    out = jax.ShapeDtypeStruct((L, r * WIDTH_P), F32)
    dq, dk, dv = pl.pallas_call(
        body, name="dil_bwd_r%d" % r, grid=(r, N_HEADS, nseg),
        in_specs=[pl.BlockSpec((seg, LANES), lambda c, h, s: (s, c * 3 * N_HEADS + h)), col(1), col(2),
                  segspec, segspec, segspec, pl.BlockSpec((1, 4, DIL_Q, DIL_SLAB), lambda c, h, s: (h, 0, 0, 0))],
        out_specs=[segspec, fullspec, fullspec], out_shape=[out, out, out],
        scratch_shapes=[pltpu.VMEM((L + 2 * DIL_HALF, LANES), BF16), pltpu.VMEM((L + 2 * DIL_HALF, LANES), BF16),
                        pltpu.VMEM((L + 2 * DIL_HALF, LANES), F32), pltpu.VMEM((L + 2 * DIL_HALF, LANES), F32)],
        compiler_params=_cparams(("arbitrary", "arbitrary", "arbitrary")),
    )(pv, pv, pv, view(do), view(lse), view(delta), bias)
    return [a.reshape(S, WIDTH_P) for a in (dq, dk, dv)]


def _dil_sum(grads):
    S = grads[0][0].shape[0]
    nb = len(grads)

    def body(step, *refs):
        out_ref = refs[-1]
        for part in range(3):
            tot = refs[part][...]
            for b in range(1, nb):
                tot = tot + refs[3 * b + part][...]
            out_ref[:, part * WIDTH_P:(part + 1) * WIDTH_P] = tot.astype(BF16)

    return _rows(body, "dil_sum", S, 256, [a for g in grads for a in g], [], [(3 * WIDTH_P, BF16)])[0]


_NT = (((1,), (1,)), ((), ()))
_TN = (((0,), (0,)), ((), ()))
HEAD_COLS = 3 * LANES


def _slab_bias():
    row = jnp.arange(DIL_Q, dtype=jnp.int32)[:, None]
    col = jnp.arange(DIL_SLAB, dtype=jnp.int32)[None, :]
    slopes = jnp.exp2(-8.0 * jnp.arange(1, N_HEADS + 1, dtype=F32) / N_HEADS)
    out = []
    for r in DILATIONS:
        variants = []
        for shift in (DIL_HALF, 0, DIL_Q):
            ad = jnp.abs(col - shift - row)
            variants.append(jnp.where(ad <= DIL_HALF, -slopes[:, None, None] * (ad * r).astype(F32)[None], MASKED))
        out.append(jnp.stack(variants, axis=1))
    return jnp.stack(out, axis=0)


DIL_CHUNK = 512


def _block_geometry(i, nblk):
    first = pl.multiple_of(i * DIL_Q, DIL_Q)
    slab0 = pl.multiple_of(jnp.clip(i * DIL_Q - DIL_HALF, 0, (nblk - 2) * DIL_Q), DIL_HALF)
    variant = jnp.where(i == 0, 1, jnp.where(i == nblk - 1, 2, 0))
    return first, slab0, variant


def _class_rows(c, r, r0, n):
    return pl.ds(c + r0 * r, n, stride=r) if r > 1 else pl.ds(r0, n)


def _dila_fwd(proj_a, bias, exchange=None):
    S = proj_a.shape[0]
    lmax = S // DILATIONS[1]

    def compute(qh_ref, kh_ref, vh_ref, b_ref, o_ref, q_s, k_s, v_s, cm_s):
        lane = lax.broadcasted_iota(jnp.int32, (DIL_Q, LANES), 1)
        lane_s = lax.broadcasted_iota(jnp.int32, (DIL_SLAB, LANES), 1)
        lane_c = lax.broadcasted_iota(jnp.int32, (DIL_CHUNK, LANES), 1)

        def run(g, nblk, load_q, load_k, load_v, store):
            def block(i, carry):
                first, slab0, variant = _block_geometry(i, nblk)
                qv, ks, vs = load_q(first), load_k(slab0), load_v(slab0)
                s = lax.dot_general(qv, ks, _NT, preferred_element_type=F32) + b_ref[g, 0, variant]
                m = jnp.max(s, axis=-1, keepdims=True)
                acc = jnp.dot(jnp.exp(s - m).astype(BF16), vs, preferred_element_type=F32)
                l = _lanes_hi_to_all(acc)
                store(first, jnp.where(lane < HEAD_DIM, acc / l, m + jnp.log(l)))
                return carry

            lax.fori_loop(0, nblk, block, 0, unroll=4)

        def direct_store(first, val):
            o_ref[pl.ds(first, DIL_Q), :] = val

        run(0, S // DIL_Q,
            lambda f: (qh_ref[pl.ds(f, DIL_Q), :] * DIL_SCALE).astype(BF16),
            lambda s0: kh_ref[pl.ds(s0, DIL_SLAB), :].astype(BF16),
            lambda s0: jnp.where(lane_s < HEAD_DIM, vh_ref[pl.ds(s0, DIL_SLAB), :], 1.0).astype(BF16),
            direct_store)

        def cm_store(first, val):
            cm_s[pl.ds(first, DIL_Q), :] = val

        for g, r in list(enumerate(DILATIONS))[1:]:
            L = S // r
            n = min(L, DIL_CHUNK)
            for c in range(r):
                for r0 in range(0, L, n):
                    src = _class_rows(c, r, r0, n)
                    q_s[r0:r0 + n, :] = (qh_ref[src, :] * DIL_SCALE).astype(BF16)
                    k_s[r0:r0 + n, :] = kh_ref[src, :].astype(BF16)
                    v_s[r0:r0 + n, :] = jnp.where(lane_c[:n] < HEAD_DIM, vh_ref[src, :], 1.0).astype(BF16)
                run(g, L // DIL_Q, lambda f: q_s[pl.ds(f, DIL_Q), :], lambda s0: k_s[pl.ds(s0, DIL_SLAB), :],
                    lambda s0: v_s[pl.ds(s0, DIL_SLAB), :], cm_store)
                for r0 in range(0, L, n):
                    dst = _class_rows(c, r, r0, n)
                    a, b = cm_s[r0:r0 + n, :], o_ref[dst, :]
                    la, lb = _lanes_hi_to_all(a), _lanes_hi_to_all(b)
                    m = jnp.maximum(la, lb)
                    wa, wb = jnp.exp(la - m), jnp.exp(lb - m)
                    tot = wa + wb
                    o_ref[dst, :] = jnp.where(lane_c[:n] < HEAD_DIM, (wa * a + wb * b) / tot, m + jnp.log(tot))

    ex = exchange
    n = ex.n if ex else 0

    def body(*refs):
        ins, ex_in, o_ref, ex_out = refs[:4], refs[4:4 + n], refs[4 + n], refs[5 + n:5 + 2 * n]
        scratch, sems = refs[5 + 2 * n:9 + 2 * n], refs[9 + 2 * n:]
        if ex:
            pl.when(pl.program_id(0) == 0)(lambda: ex.start(ex_in, ex_out, sems))
        compute(*ins, o_ref, *scratch)
        if ex:
            pl.when(pl.program_id(0) == N_HEADS - 1)(lambda: ex.finish(ex_in, ex_out, sems))

    out = pl.pallas_call(
        body, name="dil_fwd", grid=(N_HEADS,),
        in_specs=[pl.BlockSpec((S, LANES), lambda h: (0, 3 * h)), pl.BlockSpec((S, LANES), lambda h: (0, 3 * h + 1)),
                  pl.BlockSpec((S, LANES), lambda h: (0, 3 * h + 2)),
                  pl.BlockSpec((len(DILATIONS), 1, 3, DIL_Q, DIL_SLAB), lambda h: (0, h, 0, 0, 0))] + [ANY] * n,
        out_specs=[pl.BlockSpec((S, LANES), lambda h: (0, h))] + [ANY] * n,
        out_shape=[jax.ShapeDtypeStruct((S, WIDTH_P), F32)] + (ex.out_shapes() if ex else []),
        scratch_shapes=[pltpu.VMEM((lmax, LANES), BF16), pltpu.VMEM((lmax, LANES), BF16),
                        pltpu.VMEM((lmax, LANES), BF16), pltpu.VMEM((lmax, LANES), F32)] + (ex.scratch() if ex else []),
        compiler_params=_cparams(("arbitrary",)),
    )(proj_a, proj_a, proj_a, bias, *(ex.arrays if ex else []))
    return out[0], list(out[1:])


N_SPLIT = 3


def _split3(x):
    hi = x.astype(BF16).astype(F32)
    mid = (x - hi).astype(BF16).astype(F32)
    lo = (x - hi - mid).astype(BF16).astype(F32)
    return hi, mid, lo


def _dila_bwd(proj_a, dopack, bias):
    S = proj_a.shape[0]
    lmax = S // DILATIONS[1]

    def body(qh_ref, kh_ref, vh_ref, d_ref, b_ref, out_ref, dq_ref, dk_ref, dv_ref, q_s, k_s, v_s, do_s,
             dq_c, dk_c, dv_c):
        def scalar_lanes(shape):
            lane = lax.broadcasted_iota(jnp.int32, shape, 1)
            return lane, (lane >= HEAD_DIM) & (lane < HEAD_DIM + N_SPLIT)

        def q_side(q, x):
            lane, ones = scalar_lanes(x.shape)
            lse_parts = pltpu.roll(x, LANES - N_SPLIT, 1)
            qv = jnp.where(lane < HEAD_DIM, q * DIL_SCALE, jnp.where(ones, lse_parts, 0.0)).astype(BF16)
            return qv, jnp.where(lane < HEAD_DIM + N_SPLIT, x, 0.0).astype(BF16)

        def kv_side(k, v):
            _, ones = scalar_lanes(k.shape)
            return jnp.where(ones, 1.0, k).astype(BF16), jnp.where(ones, 1.0, v).astype(BF16)

        def run(g, nblk, load_q, load_kv, dq_o, dk_o, dv_o):
            def block(i, carry):
                first, slab0, variant = _block_geometry(i, nblk)
                rows, slab = pl.ds(first, DIL_Q), pl.ds(slab0, DIL_SLAB)
                (qv, dov), (ks, vs) = load_q(rows), load_kv(slab)
                p = jnp.exp(lax.dot_general(qv, ks, _NT, preferred_element_type=F32) + b_ref[g, 0, variant])
                ds = (p * lax.dot_general(dov, vs, _NT, preferred_element_type=F32)).astype(BF16)
                dq_o[rows, :] = jnp.dot(ds, ks, preferred_element_type=F32) * DIL_SCALE
                dk_o[slab, :] += lax.dot_general(ds, qv, _TN, preferred_element_type=F32)
                dv_o[slab, :] += lax.dot_general(p.astype(BF16), dov, _TN, preferred_element_type=F32)
                return carry

            lax.fori_loop(0, nblk, block, 0, unroll=4)

        dk_ref[...] = jnp.zeros_like(dk_ref)
        dv_ref[...] = jnp.zeros_like(dv_ref)
        run(0, S // DIL_Q,
            lambda rows: q_side(qh_ref[rows, :], d_ref[rows, :]),
            lambda slab: kv_side(kh_ref[slab, :], vh_ref[slab, :]),
            dq_ref, dk_ref, dv_ref)

        for g, r in list(enumerate(DILATIONS))[1:]:
            L = S // r
            n = min(L, DIL_CHUNK)
            for c in range(r):
                for r0 in range(0, L, n):
                    src = _class_rows(c, r, r0, n)
                    q_s[r0:r0 + n, :], do_s[r0:r0 + n, :] = q_side(qh_ref[src, :], d_ref[src, :])
                    k_s[r0:r0 + n, :], v_s[r0:r0 + n, :] = kv_side(kh_ref[src, :], vh_ref[src, :])
                    dk_c[r0:r0 + n, :] = jnp.zeros((n, LANES), F32)
                    dv_c[r0:r0 + n, :] = jnp.zeros((n, LANES), F32)
                run(g, L // DIL_Q, lambda rows: (q_s[rows, :], do_s[rows, :]),
                    lambda slab: (k_s[slab, :], v_s[slab, :]), dq_c, dk_c, dv_c)
                for r0 in range(0, L, n):
                    dst = _class_rows(c, r, r0, n)
                    for acc, cls in ((dq_ref, dq_c), (dk_ref, dk_c), (dv_ref, dv_c)):
                        acc[dst, :] += cls[r0:r0 + n, :]

        for r0 in range(0, S, DIL_CHUNK):
            for part, ref in enumerate((dq_ref, dk_ref, dv_ref)):
                out_ref[r0:r0 + DIL_CHUNK, part * LANES:(part + 1) * LANES] = ref[r0:r0 + DIL_CHUNK, :].astype(BF16)

    bf = lambda rows: pltpu.VMEM((rows, LANES), BF16)
    f32 = lambda rows: pltpu.VMEM((rows, LANES), F32)
    return pl.pallas_call(
        body, name="dil_bwd", grid=(N_HEADS,),
        in_specs=[pl.BlockSpec((S, LANES), lambda h: (0, 3 * h), pipeline_mode=pl.Buffered(1)),
                  pl.BlockSpec((S, LANES), lambda h: (0, 3 * h + 1), pipeline_mode=pl.Buffered(1)),
                  pl.BlockSpec((S, LANES), lambda h: (0, 3 * h + 2), pipeline_mode=pl.Buffered(1)),
                  pl.BlockSpec((S, LANES), lambda h: (0, h), pipeline_mode=pl.Buffered(1)),
                  pl.BlockSpec((len(DILATIONS), 1, 3, DIL_Q, DIL_SLAB), lambda h: (0, h, 0, 0, 0))],
        out_specs=pl.BlockSpec((S, HEAD_COLS), lambda h: (0, h)),
        out_shape=jax.ShapeDtypeStruct((S, N_HEADS * HEAD_COLS), BF16),
        scratch_shapes=[f32(S), f32(S), f32(S), bf(lmax), bf(lmax), bf(lmax), bf(lmax),
                        f32(lmax), f32(lmax), f32(lmax)],
        compiler_params=_cparams(("arbitrary",)),
    )(proj_a, proj_a, proj_a, dopack, bias)


def _mla_fwd(q, k, vt):
    S = q.shape[0]
    tq, tk = MLA_TQ, MLA_TK
    nq, nk = S // tq, S // tk

    def body(q_ref, k_ref, vt_ref, o_ref, lse_ref, acc_ref):
        qv = q_ref[...]
        acc_ref[...] = jnp.zeros_like(acc_ref)

        def chunk(c, m):
            kc = k_ref[pl.ds(pl.multiple_of(c * tk, tk), tk), :]
            st = lax.dot_general(kc, qv, _NT, preferred_element_type=F32)
            m_new = jnp.maximum(m, jnp.max(st, axis=0, keepdims=True))
            pt = jnp.exp(st - m_new).astype(BF16)
            acc_ref[...] = jnp.exp(m - m_new) * acc_ref[...] + jnp.dot(vt_ref[0, c], pt,
                                                                        preferred_element_type=F32)
            return m_new

        m = lax.fori_loop(0, nk, chunk, jnp.full((1, tq), M_INIT, F32), unroll=2)
        acc = acc_ref[...]
        l = acc[HEAD_DIM:HEAD_DIM + 1, :]
        row = lax.broadcasted_iota(jnp.int32, acc.shape, 0)
        o_ref[...] = jnp.where(row < HEAD_DIM, acc / l, 0.0).T
        lse_ref[0] = m + jnp.log(l)

    return pl.pallas_call(
        body, name="mla_fwd", grid=(N_HEADS, nq),
        in_specs=[pl.BlockSpec((tq, LANES), lambda h, i: (i, h)),
                  pl.BlockSpec((S, LANES), lambda h, i: (0, h)),
                  pl.BlockSpec((1, nk, LANES, tk), lambda h, i: (h, 0, 0, 0))],
        out_specs=[pl.BlockSpec((tq, LANES), lambda h, i: (i, h)),
                   pl.BlockSpec((1, 1, tq), lambda h, i: (h, 0, i))],
        out_shape=[jax.ShapeDtypeStruct((S, WIDTH_P), F32), jax.ShapeDtypeStruct((N_HEADS, 1, S), F32)],
        scratch_shapes=[pltpu.VMEM((LANES, tq), F32)],
        compiler_params=_cparams(("parallel", "parallel")),
    )(q, k, vt)


def _mla_bwd(q, k, v, kt, do, o, lse, exchange=None):
    S = q.shape[0]
    tq, tk = MLA_TQ, MLA_TK
    nq, nk = S // tq, S // tk

    def compute(q_ref, do_ref, o_ref, lse_ref, k_ref, v_ref, kt_ref, dq_ref, dk_ref, dv_ref, dqt_ref):
        @pl.when(pl.program_id(1) == 0)
        def _():
            dk_ref[...] = jnp.zeros_like(dk_ref)
            dv_ref[...] = jnp.zeros_like(dv_ref)

        qv, dov = q_ref[...], do_ref[...]
        delta = jnp.sum((dov.astype(F32) * o_ref[...]).T, axis=0, keepdims=True)
        lse = lse_ref[0]
        dqt_ref[...] = jnp.zeros_like(dqt_ref)

        def chunk(c, carry):
            rows = pl.ds(pl.multiple_of(c * tk, tk), tk)
            kc, vc = k_ref[rows, :], v_ref[rows, :]
            pt = jnp.exp(lax.dot_general(kc, qv, _NT, preferred_element_type=F32) - lse)
            dv_ref[rows, :] += jnp.dot(pt.astype(BF16), dov, preferred_element_type=F32)
            dpt = lax.dot_general(vc, dov, _NT, preferred_element_type=F32)
            dst = (pt * (dpt - delta)).astype(BF16)
            dk_ref[rows, :] += jnp.dot(dst, qv, preferred_element_type=F32)
            dqt_ref[...] += jnp.dot(kt_ref[0, c], dst, preferred_element_type=F32)
            return carry

        lax.fori_loop(0, nk, chunk, 0, unroll=2)
        dq_ref[...] = (dqt_ref[...] * MLA_SCALE).T

    ex = exchange
    n = ex.n if ex else 0

    def body(*refs):
        ins, ex_in, outs, ex_out = refs[:7], refs[7:7 + n], refs[7 + n:10 + n], refs[10 + n:10 + 2 * n]
        dqt_ref, sems = refs[10 + 2 * n], refs[11 + 2 * n:]
        first = (pl.program_id(0) == 0) & (pl.program_id(1) == 0)
        last = (pl.program_id(0) == N_HEADS - 1) & (pl.program_id(1) == nq - 1)
        if ex:
            pl.when(first)(lambda: ex.start(ex_in, ex_out, sems))
        compute(*ins, *outs, dqt_ref)
        if ex:
            pl.when(last)(lambda: ex.finish(ex_in, ex_out, sems))

    qspec = pl.BlockSpec((tq, LANES), lambda h, i: (i, h))
    kspec = pl.BlockSpec((S, LANES), lambda h, i: (0, h))
    out = jax.ShapeDtypeStruct((S, WIDTH_P), F32)
    res = pl.pallas_call(
        body, name="mla_bwd", grid=(N_HEADS, nq),
        in_specs=[qspec, qspec, qspec, pl.BlockSpec((1, 1, tq), lambda h, i: (h, 0, i)), kspec, kspec,
                  pl.BlockSpec((1, nk, LANES, tk), lambda h, i: (h, 0, 0, 0))] + [ANY] * n,
        out_specs=[qspec, kspec, kspec] + [ANY] * n, out_shape=[out, out, out] + (ex.out_shapes() if ex else []),
        scratch_shapes=[pltpu.VMEM((LANES, tq), F32)] + (ex.scratch() if ex else []),
        compiler_params=_cparams(("arbitrary", "arbitrary")),
    )(q, do, o, lse, k, v, kt, *(ex.arrays if ex else []))
    return res[0], res[1], res[2], list(res[3:])


def _mla_bwd_prep(dq, dk, dv, tabs):
    S = dq.shape[0]

    def body(step, dq_ref, dk_ref, dv_ref, c_ref, sa_ref, sb_ref, dqp_ref, dkv_ref, dkr_ref):
        c, sa, sb = c_ref[...], sa_ref[...], sb_ref[...]
        dksum = jnp.zeros((dq_ref.shape[0], LANES), F32)
        for h in range(N_HEADS):
            blk = slice(h * LANES, (h + 1) * LANES)
            dqp_ref[:, blk] = _rope_transpose(dq_ref[:, blk], c, sa, sb).astype(BF16)
            dksum = dksum + dk_ref[:, blk]
        dkv_ref[:, 0:WIDTH_P] = dk_ref[...].astype(BF16)
        dkv_ref[:, WIDTH_P:2 * WIDTH_P] = dv_ref[...].astype(BF16)
        lane = lax.broadcasted_iota(jnp.int32, dksum.shape, 1)
        live = (lane >= HEAD_DIM) & (lane < HEAD_DIM + QK_ROPE)
        dkr_ref[...] = jnp.where(live, _rope_transpose(dksum, c, sa, sb), 0.0)

    return _rows(body, "mla_bwd_prep", S, 256, [dq, dk, dv, *tabs], [],
                 [(WIDTH_P, BF16), (2 * WIDTH_P, BF16), (LANES, F32)])


def _mla_norm_bwd(proj_b, dcq_n, dckv_n, dkr, g_q, g_kv):
    S = proj_b.shape[0]

    def body(step, p_ref, dcq_ref, dckv_ref, dkr_ref, gq_ref, gkv_ref, dp_ref, dgq_ref, dgkv_ref):
        dcq, dgq = _rms_bwd_math(p_ref[:, 0:Q_LORA], gq_ref[...], dcq_ref[...], Q_LORA)
        dckv, dgkv = _rms_bwd_math(p_ref[:, Q_LORA:Q_LORA + KV_LORA], gkv_ref[...], dckv_ref[...], KV_LORA)
        dp_ref[:, 0:Q_LORA] = dcq.astype(BF16)
        dp_ref[:, Q_LORA:Q_LORA + KV_LORA] = dckv.astype(BF16)
        dp_ref[:, Q_LORA + KV_LORA:TAIL_P] = dkr_ref[...].astype(BF16)
        _acc_add(step, dgq_ref, jnp.sum(dgq, axis=0, keepdims=True))
        _acc_add(step, dgkv_ref, jnp.sum(dgkv, axis=0, keepdims=True))

    return _rows(body, "mla_norm_bwd", S, 512, [proj_b, dcq_n, dckv_n, dkr], [g_q, g_kv], [(TAIL_P, BF16)],
                 [((1, Q_LORA), F32), ((1, KV_LORA), F32)])


def _pad_cols(w, d):
    lead = w.shape[:-1]
    w = w.reshape(lead + (N_HEADS, d))
    w = jnp.pad(w, [(0, 0)] * len(lead) + [(0, 0), (0, LANES - d)])
    return w.reshape(lead + (N_HEADS * LANES,))


def _unpad_cols(w, d):
    lead = w.shape[:-1]
    return w.reshape(lead + (N_HEADS, LANES))[..., :d].reshape(lead + (N_HEADS * d,))


def _pad_w_in(w_in):
    zeros = lambda n: jnp.zeros((D_MODEL, n), w_in.dtype)
    p = {}
    parts = [_pad_cols(w_in[:, i * WIDTH:(i + 1) * WIDTH], HEAD_DIM).reshape(D_MODEL, N_HEADS, 1, LANES)
             for i in range(3)]
    p['w_in_a'] = jnp.concatenate(parts, axis=2).reshape(D_MODEL, N_HEADS * HEAD_COLS)
    p['w_in_b'] = jnp.concatenate([w_in[:, 3 * WIDTH:3 * WIDTH + Q_LORA + KV_LORA], zeros(HEAD_DIM),
                                   w_in[:, D_IN - QK_ROPE:], zeros(LANES - HEAD_DIM - QK_ROPE)], axis=1)
    return p


def _pad_weights(w):
    p = {}
    p['w_uq'] = _pad_cols(w['w_uq'], HEAD_DIM + QK_ROPE)
    kv = w['w_ukv'].reshape(KV_LORA, N_HEADS, 2 * HEAD_DIM)
    p['w_ukv'] = jnp.concatenate([_pad_cols(kv[:, :, :HEAD_DIM].reshape(KV_LORA, WIDTH), HEAD_DIM),
                                  _pad_cols(kv[:, :, HEAD_DIM:].reshape(KV_LORA, WIDTH), HEAD_DIM)], axis=1)
    p['w_o'] = jnp.concatenate(
        [_pad_cols(w['w_o'][i * WIDTH:(i + 1) * WIDTH].T, HEAD_DIM).T for i in range(2)], axis=0)
    p['g_a'] = _pad_cols(w['out_norm_a'], HEAD_DIM)
    p['g_b'] = _pad_cols(w['out_norm_b'], HEAD_DIM)
    return p


def _unpad_grads(d):
    g = {}
    dwa = d['w_in_a'].reshape(D_MODEL, N_HEADS, 3, LANES)
    tail = d['w_in_b']
    g['w_in'] = jnp.concatenate(
        [dwa[:, :, i, :HEAD_DIM].reshape(D_MODEL, WIDTH) for i in range(3)]
        + [tail[:, :Q_LORA + KV_LORA], tail[:, Q_LORA + KV_LORA + HEAD_DIM:Q_LORA + KV_LORA + HEAD_DIM + QK_ROPE]],
        axis=1)
    g['w_uq'] = _unpad_cols(d['w_uq'], HEAD_DIM + QK_ROPE)
    dk = _unpad_cols(d['w_ukv'][:, :WIDTH_P], HEAD_DIM).reshape(KV_LORA, N_HEADS, HEAD_DIM)
    dv = _unpad_cols(d['w_ukv'][:, WIDTH_P:], HEAD_DIM).reshape(KV_LORA, N_HEADS, HEAD_DIM)
    g['w_ukv'] = jnp.concatenate([dk, dv], axis=2).reshape(KV_LORA, 2 * WIDTH)
    g['w_o'] = jnp.concatenate(
        [_unpad_cols(d['w_o'][i * WIDTH_P:(i + 1) * WIDTH_P].T, HEAD_DIM).T for i in range(2)], axis=0)
    g['out_norm_a'] = _unpad_cols(d['g_a'], HEAD_DIM)
    g['out_norm_b'] = _unpad_cols(d['g_b'], HEAD_DIM)
    return g


LATE = ['w_uq', 'w_ukv', 'w_o', 'w_up', 'w_down']
EARLY_GRADS = ['w_up', 'w_down']


def _assemble_late(gathered):
    g = dict(zip(LATE, gathered))
    return {'w_uq': jnp.concatenate([g['w_uq'][i] for i in range(N_CHIPS)], axis=1),
            'w_ukv': jnp.concatenate([g['w_ukv'][i] for i in range(N_CHIPS)], axis=1),
            'w_o': g['w_o'].reshape(D_MODEL, D_MODEL),
            'w_down': g['w_down'].reshape(D_FF, D_MODEL),
            'w_up': g['w_up']}


def _local_step(x, target, w, late_shards=None):
    S = x.shape[0]
    w = dict(w)
    p = _pad_w_in(w['w_in'])
    tabs = _rope_tables(S)
    bias = _slab_bias()

    h1 = _rms_fwd(x, w['norm_mix_pre'], "rms_mix_pre")
    proj_a = _mm(h1, p['w_in_a'], 'nn', F32, "mm_in_a")
    proj_b = _mm(h1, p['w_in_b'], 'nn', F32, "mm_in_b")
    gather = _Exchange([late_shards[n] for n in LATE], False) if late_shards else None
    oa, gathered = _dila_fwd(proj_a, bias, gather)
    if late_shards:
        w.update(_assemble_late(gathered))
    p.update(_pad_weights(w))
    cq_n, ckv_n, kr = _mla_prep(proj_b, w['q_lat_norm'], w['kv_lat_norm'], tabs)
    q_lin = _mm(cq_n, p['w_uq'], 'nn', F32, "mm_uq")
    kv_lin = _mm(ckv_n, p['w_ukv'], 'nn', F32, "mm_ukv")
    qb, kb, vb, kt, vt = _mla_qkv(q_lin, kv_lin, kr, tabs)
    ob, lse_b = _mla_fwd(qb, kb, vt)
    cat = _outnorm_fwd(oa, ob, p['g_a'], p['g_b'])
    y = _mm(cat, p['w_o'], 'nn', F32, "mm_o")
    x1, h2 = _post_mix(x, y, w['norm_mix_post'], w['norm_ffn_pre'])
    u0 = _mm(h2, w['w_up'], 'nn', F32, "mm_up", sharded=True)
    ug, uv, a = _conv_fwd(u0, w['conv_w'], w['conv_b'])
    y2 = _mm(a, w['w_down'], 'nn', F32, "mm_down")
    dx2, dy2, dg_ffn_post, _, loss = _final(x1, y2, w['norm_ffn_post'], target)

    g = {'norm_ffn_post': dg_ffn_post}
    da = _mm(dy2, w['w_down'], 'nt', F32, "mm_down_dx")
    g['w_down'] = _mm(a, dy2, 'tn', BF16, "mm_down_dw")
    du0, g['conv_w'], g['conv_b'] = _conv_bwd(u0, ug, uv, da, w['conv_w'])
    dh2 = _mm(du0, w['w_up'], 'nt', F32, "mm_up_dx", sharded=True)
    g['w_up'] = _mm(h2, du0, 'tn', BF16, "mm_up_dw", sharded=True)
    dx1, g['norm_ffn_pre'] = _rms_bwd(x1, w['norm_ffn_pre'], [dh2], dx2, F32, "rms_ffn_pre_bwd")
    dy, g['norm_mix_post'] = _rms_bwd(y, w['norm_mix_post'], [dx1], None, BF16, "rms_mix_post_bwd")
    dcat = _mm(dy, p['w_o'], 'nt', F32, "mm_o_dx")
    dpad = {'w_o': _mm(cat, dy, 'tn', F32, "mm_o_dw")}
    dopack_a, do_b, dpad['g_a'], dpad['g_b'] = _outnorm_bwd(oa, ob, p['g_a'], p['g_b'], dcat)

    scatter = None
    if late_shards:
        scatter = _Exchange([g['w_up'], g['w_down'].reshape(N_CHIPS, D_FF // N_CHIPS, D_MODEL)], True)
    dq_b, dk_b, dv_b, received = _mla_bwd(qb, kb, vb, kt, do_b, ob, lse_b, scatter)
    if late_shards:
        g.update(zip(EARLY_GRADS, received))
    dq_pre, dkv, dkr = _mla_bwd_prep(dq_b, dk_b, dv_b, tabs)
    dcq_n = _mm(dq_pre, p['w_uq'], 'nt', F32, "mm_uq_dx")
    dpad['w_uq'] = _mm(cq_n, dq_pre, 'tn', F32, "mm_uq_dw")
    dckv_n = _mm(dkv, p['w_ukv'], 'nt', F32, "mm_ukv_dx")
    dpad['w_ukv'] = _mm(ckv_n, dkv, 'tn', F32, "mm_ukv_dw")
    dproj_b, g['q_lat_norm'], g['kv_lat_norm'] = _mla_norm_bwd(proj_b, dcq_n, dckv_n, dkr,
                                                               w['q_lat_norm'], w['kv_lat_norm'])

    dproj_a = _dila_bwd(proj_a, dopack_a, bias)
    dh1 = _mm(dproj_b, p['w_in_b'], 'nt', F32, "mm_in_b_dx")
    dh1 = _mm(dproj_a, p['w_in_a'], 'nt', F32, "mm_in_a_dx", add=dh1)
    dpad['w_in_a'] = _mm(h1, dproj_a, 'tn', F32, "mm_in_a_dw")
    dpad['w_in_b'] = _mm(h1, dproj_b, 'tn', F32, "mm_in_b_dw")
    grad_x, g['norm_mix_pre'] = _rms_bwd(x, w['norm_mix_pre'], [dh1], dx1, F32, "rms_mix_pre_bwd")
    g.update(_unpad_grads(dpad))
    return loss, grad_x, g


MESH = pl.DeviceIdType.MESH
ANY = pl.BlockSpec(memory_space=pl.ANY)


def _place():
    x, y, c = lax.axis_index("x"), lax.axis_index("y"), lax.axis_index("c")
    chips = [(1 - x, y), (x, 1 - y), (1 - x, 1 - y)]
    return x, y, c, chips


class _Exchange:
    def __init__(self, arrays, scatter):
        self.arrays, self.scatter, self.n = list(arrays), scatter, len(arrays)

    def out_shapes(self):
        return [jax.ShapeDtypeStruct(a.shape if self.scatter else (N_CHIPS,) + a.shape, a.dtype) for a in self.arrays]

    def scratch(self):
        return [pltpu.SemaphoreType.DMA((3 * self.n,)), pltpu.SemaphoreType.DMA((3 * self.n,)),
                pltpu.SemaphoreType.DMA((self.n,))]

    def _copies(self, in_refs, out_refs, sems, arrivals):
        send_sems, recv_sems, local_sems = sems
        x, y, c, chips = _place()
        me = 2 * x + y
        src = lambda b, chip: in_refs[b].at[chip] if self.scatter else in_refs[b]
        local = [pltpu.make_async_copy(src(b, me), out_refs[b].at[me], local_sems.at[b]) for b in range(self.n)]
        sends, recvs = [], []
        for j, (px, py) in enumerate(chips):
            for b in range(self.n):
                k = j * self.n + b
                common = dict(send_sem=send_sems.at[k], recv_sem=recv_sems.at[k], device_id=(px, py, c),
                              device_id_type=MESH)
                sends.append(pltpu.make_async_remote_copy(src_ref=src(b, 2 * px + py), dst_ref=out_refs[b].at[me],
                                                          **common))
                if arrivals:
                    recvs.append(pltpu.make_async_remote_copy(src_ref=src(b, me), dst_ref=out_refs[b].at[2 * px + py],
                                                              **common))
        return local, sends, recvs

    def start(self, in_refs, out_refs, sems):
        local, sends, _ = self._copies(in_refs, out_refs, sems, False)
        for cp in local + sends:
            cp.start()

    def finish(self, in_refs, out_refs, sems):
        local, sends, recvs = self._copies(in_refs, out_refs, sems, True)
        for cp in recvs:
            cp.wait_recv()
        for cp in sends:
            cp.wait_send()
        for cp in local:
            cp.wait()


def _exchange_call(arrays, scatter, name):
    ex = _Exchange(arrays, scatter)
    n = ex.n

    def body(*refs):
        in_refs, out_refs, sems = refs[:n], refs[n:2 * n], refs[2 * n:]
        ex.start(in_refs, out_refs, sems)
        ex.finish(in_refs, out_refs, sems)

    return pl.pallas_call(body, name=name, in_specs=[ANY] * n, out_specs=[ANY] * n, out_shape=ex.out_shapes(),
                          scratch_shapes=ex.scratch())(*ex.arrays)


def _all_gather(bufs, name="gather_weights"):
    return _exchange_call(bufs, False, name)


def _scatter_grads(slots, name="scatter_grads"):
    return _exchange_call(slots, True, name)


ELEMENTWISE_BLOCK = 256 * 1024


def _row_tile(rows, cols):
    best = None
    for t in range(16, min(rows, max(16, ELEMENTWISE_BLOCK // cols)) + 1, 16):
        if rows % t == 0:
            best = t
    return best if best is not None else rows


def _sum_slots(recv, name):
    _, R, C = recv.shape
    tr = _row_tile(R, C)

    def body(r_ref, o_ref):
        f = lambda i: r_ref[i].astype(F32)
        o_ref[...] = ((f(0) + f(1)) + f(2)) + f(3)

    return pl.pallas_call(
        body, name="sum_" + name, grid=(R // tr,),
        in_specs=[pl.BlockSpec((N_CHIPS, tr, C), lambda i: (0, i, 0))],
        out_specs=pl.BlockSpec((tr, C), lambda i: (i, 0)),
        out_shape=jax.ShapeDtypeStruct((R, C), F32),
        compiler_params=_cparams(("parallel",)),
    )(recv)


def _swap_sibling(parts):
    n = len(parts)

    def body(*refs):
        p_refs, o_refs, send_sems, recv_sems = refs[:n], refs[n:2 * n], refs[2 * n], refs[2 * n + 1]
        x, y, c, _ = _place()
        cps = [pltpu.make_async_remote_copy(src_ref=p_refs[b], dst_ref=o_refs[b], send_sem=send_sems.at[b],
                                            recv_sem=recv_sems.at[b], device_id=(x, y, 1 - c), device_id_type=MESH)
               for b in range(n)]
        for cp in cps:
            cp.start()
        for cp in cps:
            cp.wait()

    return pl.pallas_call(
        body, name="swap_sibling", in_specs=[ANY] * n, out_specs=[ANY] * n,
        out_shape=[jax.ShapeDtypeStruct(p.shape, p.dtype) for p in parts],
        scratch_shapes=[pltpu.SemaphoreType.DMA((n,)), pltpu.SemaphoreType.DMA((n,))],
    )(*parts)


def _adamw(g0, g1, w, m, v, name, offset=0):
    R, C = w.shape
    tr = _row_tile(R, C)
    packed = g0.shape != w.shape
    bc1 = 1.0 - ADAM_B1 ** ADAM_STEP
    bc2 = 1.0 - ADAM_B2 ** ADAM_STEP

    def body(g0_ref, g1_ref, w_ref, m_ref, v_ref, g_ref, d_ref, nm_ref, nv_ref):
        if packed:
            g = g0_ref[:, offset:offset + C] + g1_ref[:, offset:offset + C]
        else:
            g = g0_ref[...] + g1_ref[...]
        g_ref[...] = g
        nm = ADAM_B1 * m_ref[...] + (1.0 - ADAM_B1) * g
        nv = ADAM_B2 * v_ref[...] + (1.0 - ADAM_B2) * (g * g)
        nm_ref[...] = nm
        nv_ref[...] = nv
        d_ref[...] = -ADAM_LR * ((nm / bc1) / (jnp.sqrt(nv / bc2) + ADAM_EPS) + ADAM_WD * w_ref[...])

    spec = pl.BlockSpec((tr, C), lambda i: (i, 0))
    gspec = pl.BlockSpec(g0.shape, lambda i: (0, 0)) if packed else spec
    out = jax.ShapeDtypeStruct((R, C), F32)
    return pl.pallas_call(
        body, name="adamw_" + name, grid=(R // tr,), in_specs=[gspec, gspec, spec, spec, spec],
        out_specs=[spec] * 4, out_shape=[out] * 4, compiler_params=_cparams(("parallel",)),
    )(g0, g1, w, m, v)


def kernel(x, norm_mix_pre, w_in, q_lat_norm, w_uq, kv_lat_norm, w_ukv, out_norm_a, out_norm_b, w_o, norm_mix_post, norm_ffn_pre, w_up, conv_w, conv_b, w_down, norm_ffn_post, loss_target, m_norm_mix_pre, m_w_in, m_q_lat_norm, m_w_uq, m_kv_lat_norm, m_w_ukv, m_out_norm_a, m_out_norm_b, m_w_o, m_norm_mix_post, m_norm_ffn_pre, m_w_up, m_conv_w, m_conv_b, m_w_down, m_norm_ffn_post, v_norm_mix_pre, v_w_in, v_q_lat_norm, v_w_uq, v_kv_lat_norm, v_w_ukv, v_out_norm_a, v_out_norm_b, v_w_o, v_norm_mix_post, v_norm_ffn_pre, v_w_up, v_conv_w, v_conv_b, v_w_down, v_norm_ffn_post):
    args = dict(locals())
    strip = lambda a: a[0] if a.ndim == 3 else a
    wl = {n: strip(args[n]) for n in WEIGHTS}
    ml = {n: strip(args['m_' + n]) for n in WEIGHTS}
    vl = {n: strip(args['v_' + n]) for n in WEIGHTS}

    gathered = _all_gather([wl['w_in'].astype(BF16), wl['conv_w']])
    full = {n: wl[n] for n in SMALL}
    for n, a in zip(('w_in', 'conv_w'), gathered):
        full[n] = jnp.concatenate([a[i] for i in range(N_CHIPS)], axis=1)

    loss_b, grad_x, g = _local_step(x[0], loss_target[0], full, {n: wl[n].astype(BF16) for n in LATE})

    sharded = [n for n in WEIGHTS if SHARD_AXIS[n] is not None]
    late = [n for n in sharded if n not in EARLY_GRADS]

    def slots_of(n):
        a = g[n].astype(BF16)
        if SHARD_AXIS[n] == 0:
            return a.reshape(N_CHIPS, a.shape[0] // N_CHIPS, a.shape[1])
        cols = a.shape[1] // N_CHIPS
        return jnp.stack([a[:, i * cols:(i + 1) * cols] for i in range(N_CHIPS)])

    small_pack = jnp.concatenate([g[n] for n in SMALL], axis=1)
    slots = [slots_of(n) for n in late] + [jnp.broadcast_to(small_pack[None], (N_CHIPS,) + small_pack.shape)]
    recv = dict(zip(late + ['small'], _scatter_grads(slots)))
    recv.update({n: g[n] for n in EARLY_GRADS})
    parts = [_sum_slots(recv[n], n) for n in sharded + ['small']]
    others = _swap_sibling(parts)

    outs = {}

    def record(n, results):
        for tag, a in zip(('grad', 'delta', 'new_m', 'new_v'), results):
            outs[tag + '_' + n] = a.reshape(args[n].shape)

    for n, p0, p1 in zip(sharded, parts, others):
        record(n, _adamw(p0, p1, wl[n], ml[n], vl[n], n))
    offset = 0
    for n in SMALL:
        record(n, _adamw(parts[-1], others[-1], wl[n], ml[n], vl[n], n, offset=offset))
        offset += wl[n].shape[1]

    loss = lax.psum(loss_b[0, 0], ("x", "y", "c"))
    return (loss, grad_x[None], *[outs['grad_' + n] for n in WEIGHTS], *[outs['delta_' + n] for n in WEIGHTS],
            *[outs['new_m_' + n] for n in WEIGHTS], *[outs['new_v_' + n] for n in WEIGHTS])
```

```python
import math

import jax
import jax.numpy as jnp
from jax import lax
from jax.experimental import pallas as pl
from jax.experimental.pallas import tpu as pltpu

F32 = jnp.float32
BF16 = jnp.bfloat16

LANES = 128
D_MODEL = 1024
N_HEADS = 8
HEAD_DIM = 64
QK_ROPE = 32
Q_LORA = 384
KV_LORA = 256
D_FF = 2816
WIDTH = N_HEADS * HEAD_DIM
WIDTH_P = N_HEADS * LANES
IN_SIZES = (WIDTH, WIDTH, WIDTH, Q_LORA, KV_LORA, QK_ROPE)
D_IN = sum(IN_SIZES)
TAIL_P = Q_LORA + KV_LORA + LANES
EPS = 1e-6
ROPE_BASE = 10000.0
MASKED = -2e30
M_INIT = -1e30
MLA_TQ = 4096
MLA_TK = 256
MLA_SCALE = (HEAD_DIM + QK_ROPE) ** -0.5
DIL_SCALE = HEAD_DIM ** -0.5

ADAM_LR = 0.001
ADAM_B1 = 0.9
ADAM_B2 = 0.999
ADAM_EPS = 1e-08
ADAM_WD = 0.01
ADAM_STEP = 10

VMEM_LIMIT = 56 * 1024 * 1024

N_CHIPS = 4

WEIGHTS = ['norm_mix_pre', 'w_in', 'q_lat_norm', 'w_uq', 'kv_lat_norm', 'w_ukv', 'out_norm_a', 'out_norm_b',
           'w_o', 'norm_mix_post', 'norm_ffn_pre', 'w_up', 'conv_w', 'conv_b', 'w_down', 'norm_ffn_post']
SHARD_AXIS = {'norm_mix_pre': None, 'w_in': 1, 'q_lat_norm': None, 'w_uq': 1, 'kv_lat_norm': None, 'w_ukv': 1,
              'out_norm_a': None, 'out_norm_b': None, 'w_o': 0, 'norm_mix_post': None, 'norm_ffn_pre': None,
              'w_up': 1, 'conv_w': 1, 'conv_b': None, 'w_down': 0, 'norm_ffn_post': None}
SMALL = [n for n in WEIGHTS if SHARD_AXIS[n] is None]


def _tile(dim, target):
    best = None
    t = LANES
    while t <= min(dim, target):
        if dim % t == 0:
            best = t
        t += LANES
    return best if best is not None else dim


def _cparams(sem=None):
    return pltpu.CompilerParams(dimension_semantics=sem, vmem_limit_bytes=VMEM_LIMIT)


def _mm(a, b, mode, out_dtype, name, add=None, tm=1024, tn=1024, tk=1024, sharded=False):
    if mode == 'nn':
        (M, K), (K2, N) = a.shape, ((b.shape[1], N_CHIPS * b.shape[2]) if sharded else b.shape)
        dims = (((1,), (0,)), ((), ()))
    elif mode == 'nt':
        (M, K), (N, K2) = a.shape, ((b.shape[1], N_CHIPS * b.shape[2]) if sharded else b.shape)
        dims = (((1,), (1,)), ((), ()))
    else:
        (K, M), (K2, N) = a.shape, b.shape
        dims = (((0,), (0,)), ((), ()))
    assert K == K2, (a.shape, b.shape, mode)
    tm, tn, tk = _tile(M, tm), _tile(N, tn), _tile(K, tk)
    if K == D_FF:
        tk = K
    if N == D_FF:
        tn, tm = N, min(tm, 512)
    if M == D_FF:
        tm = M
    if sharded and mode == 'nt':
        tk = K // N_CHIPS
    elif sharded:
        tn = N // N_CHIPS
    nk = K // tk
    if mode == 'nn':
        a_spec = pl.BlockSpec((tm, tk), lambda i, j, k: (i, k))
        b_spec = (pl.BlockSpec((None, tk, tn), lambda i, j, k: (j, k, 0)) if sharded
                  else pl.BlockSpec((tk, tn), lambda i, j, k: (k, j)))
    elif mode == 'nt':
        a_spec = pl.BlockSpec((tm, tk), lambda i, j, k: (i, k))
        b_spec = (pl.BlockSpec((None, tn, tk), lambda i, j, k: (k, j, 0)) if sharded
                  else pl.BlockSpec((tn, tk), lambda i, j, k: (j, k)))
    else:
        a_spec = pl.BlockSpec((tk, tm), lambda i, j, k: (k, i))
        b_spec = pl.BlockSpec((tk, tn), lambda i, j, k: (k, j))
    o_spec = pl.BlockSpec((tm, tn), lambda i, j, k: (i, j))
    out_shape = jax.ShapeDtypeStruct((M, N), out_dtype)
    if sharded and mode == 'tn':
        o_spec = pl.BlockSpec((None, tm, tn), lambda i, j, k: (j, i, 0))
        out_shape = jax.ShapeDtypeStruct((N_CHIPS, M, tn), out_dtype)
    has_add = add is not None

    def body(*refs):
        if has_add:
            a_ref, b_ref, add_ref, o_ref, acc_ref = refs
        else:
            a_ref, b_ref, o_ref, acc_ref = refs
        k = pl.program_id(2)

        @pl.when(k == 0)
        def _():
            acc_ref[...] = jnp.zeros_like(acc_ref)

        acc_ref[...] += lax.dot_general(a_ref[...].astype(BF16), b_ref[...].astype(BF16), dims,
                                        preferred_element_type=F32)

        @pl.when(k == nk - 1)
        def _():
            r = acc_ref[...]
            if has_add:
                r = r + add_ref[...]
            o_ref[...] = r.astype(o_ref.dtype)

    ins = [a, b] + ([add] if has_add else [])
    in_specs = [a_spec, b_spec] + ([o_spec] if has_add else [])
    return pl.pallas_call(
        body, name=name, grid=(M // tm, N // tn, nk), in_specs=in_specs, out_specs=o_spec, out_shape=out_shape,
        scratch_shapes=[pltpu.VMEM((tm, tn), F32)],
        compiler_params=_cparams(("parallel", "parallel", "arbitrary")),
    )(*ins)


def _rows(body, name, S, ts, row_ins, full_ins, row_outs, acc_outs=(), chunk_outs=()):
    in_specs = [pl.BlockSpec((ts, a.shape[1]), lambda i: (i, 0)) for a in row_ins]
    in_specs += [pl.BlockSpec(a.shape, lambda i, nd=a.ndim: (0,) * nd) for a in full_ins]
    out_specs = [pl.BlockSpec((ts, w), lambda i: (i, 0)) for (w, _) in row_outs]
    out_specs += [pl.BlockSpec(shape, lambda i, nd=len(shape): (0,) * nd) for (shape, _) in acc_outs]
    out_specs += [pl.BlockSpec((lead, 1, LANES, ts), lambda i: (0, i, 0, 0)) for (lead, _) in chunk_outs]
    out_shape = [jax.ShapeDtypeStruct((S, w), dt) for (w, dt) in row_outs]
    out_shape += [jax.ShapeDtypeStruct(shape, dt) for (shape, dt) in acc_outs]
    out_shape += [jax.ShapeDtypeStruct((lead, S // ts, LANES, ts), dt) for (lead, dt) in chunk_outs]

    def kbody(*refs):
        body(pl.program_id(0), *refs)

    return pl.pallas_call(
        kbody, name=name, grid=(S // ts,), in_specs=in_specs, out_specs=out_specs, out_shape=out_shape,
        compiler_params=_cparams(("arbitrary",)),
    )(*row_ins, *full_ins)


def _acc_add(step, ref, val):
    @pl.when(step == 0)
    def _():
        ref[...] = val

    @pl.when(step != 0)
    def _():
        ref[...] += val


def _rms_fwd(x, g, name):
    S, W = x.shape

    def body(step, x_ref, g_ref, h_ref):
        xv = x_ref[...]
        r = lax.rsqrt(jnp.mean(xv * xv, axis=-1, keepdims=True) + EPS)
        h_ref[...] = (xv * r * g_ref[...]).astype(BF16)

    return _rows(body, name, S, 512, [x], [g], [(W, BF16)])[0]


def _rms_bwd_math(xv, g, dy, width):
    r = lax.rsqrt(jnp.sum(xv * xv, axis=-1, keepdims=True) * (1.0 / width) + EPS)
    xn = xv * r
    dyg = dy * g
    dx = r * (dyg - xn * (jnp.sum(dyg * xn, axis=-1, keepdims=True) * (1.0 / width)))
    return dx, dy * xn


def _rms_bwd(x, g, dys, resid, out_dtype, name):
    S, W = x.shape
    nd = len(dys)
    has_res = resid is not None

    def body(step, *refs):
        x_ref = refs[0]
        dy_refs = refs[1:1 + nd]
        pos = 1 + nd
        res_ref = refs[pos] if has_res else None
        pos += int(has_res)
        g_ref, dx_ref, dg_ref = refs[pos], refs[pos + 1], refs[pos + 2]
        dy = dy_refs[0][...].astype(F32)
        for r_ in dy_refs[1:]:
            dy = dy + r_[...].astype(F32)
        dx, dgr = _rms_bwd_math(x_ref[...], g_ref[...], dy, W)
        if has_res:
            dx = dx + res_ref[...]
        dx_ref[...] = dx.astype(dx_ref.dtype)
        _acc_add(step, dg_ref, jnp.sum(dgr, axis=0, keepdims=True))

    row_ins = [x] + list(dys) + ([resid] if has_res else [])
    dx, dg = _rows(body, name, S, 256, row_ins, [g], [(W, out_dtype)], [((1, W), F32)])
    return dx, dg


def _rope_apply(xv, c, sa, sb):
    return xv * c + pltpu.roll(xv, 16, 1) * sa + pltpu.roll(xv, LANES - 16, 1) * sb


def _rope_transpose(dy, c, sa, sb):
    return dy * c + pltpu.roll(dy * sa, LANES - 16, 1) + pltpu.roll(dy * sb, 16, 1)


def _rope_tables(S):
    pos = jnp.arange(S, dtype=F32)
    inv_freq = jnp.exp(-math.log(ROPE_BASE) * jnp.arange(0, QK_ROPE, 2, dtype=F32) / QK_ROPE)
    ang = pos[:, None] * inv_freq[None, :]
    cos, sin = jnp.cos(ang), jnp.sin(ang)
    ones, zeros = jnp.ones((S, HEAD_DIM), F32), jnp.zeros((S, HEAD_DIM), F32)
    z16, z32 = jnp.zeros((S, 16), F32), jnp.zeros((S, 32), F32)
    c = jnp.concatenate([ones, cos, cos, z32], axis=1)
    sa = jnp.concatenate([zeros, z16, sin, z32], axis=1)
    sb = jnp.concatenate([zeros, -sin, z16, z32], axis=1)
    return c, sa, sb


def _mla_prep(proj_b, g_q, g_kv, tabs):
    S = proj_b.shape[0]

    def body(step, p_ref, c_ref, sa_ref, sb_ref, gq_ref, gkv_ref, cq_ref, ckv_ref, kr_ref):
        cq = p_ref[:, 0:Q_LORA]
        ckv = p_ref[:, Q_LORA:Q_LORA + KV_LORA]
        kr = p_ref[:, Q_LORA + KV_LORA:TAIL_P]
        rq = lax.rsqrt(jnp.mean(cq * cq, axis=-1, keepdims=True) + EPS)
        cq_ref[...] = (cq * rq * gq_ref[...]).astype(BF16)
        rk = lax.rsqrt(jnp.mean(ckv * ckv, axis=-1, keepdims=True) + EPS)
        ckv_ref[...] = (ckv * rk * gkv_ref[...]).astype(BF16)
        kr_ref[...] = _rope_apply(kr, c_ref[...], sa_ref[...], sb_ref[...])

    return _rows(body, "mla_prep", S, 512, [proj_b, *tabs], [g_q, g_kv],
                 [(Q_LORA, BF16), (KV_LORA, BF16), (LANES, F32)])


def _mla_qkv(q, kv, kr, tabs):
    S = q.shape[0]

    def body(step, q_ref, kv_ref, kr_ref, c_ref, sa_ref, sb_ref, qb_ref, kb_ref, vb_ref, kt_ref, vt_ref):
        c, sa, sb = c_ref[...], sa_ref[...], sb_ref[...]
        krv = kr_ref[...]
        row = lax.broadcasted_iota(jnp.int32, (LANES, MLA_TK), 0)
        for h in range(N_HEADS):
            blk = slice(h * LANES, (h + 1) * LANES)
            qb_ref[:, blk] = (_rope_apply(q_ref[:, blk], c, sa, sb) * MLA_SCALE).astype(BF16)
            kh = kv_ref[:, blk] + krv
            kb_ref[:, blk] = kh.astype(BF16)
            kt_ref[h, 0] = kh.T.astype(BF16)
            vh = kv_ref[:, WIDTH_P + h * LANES:WIDTH_P + (h + 1) * LANES]
            vt_ref[h, 0] = jnp.where(row == HEAD_DIM, 1.0, vh.T).astype(BF16)
        vb_ref[...] = kv_ref[:, WIDTH_P:2 * WIDTH_P].astype(BF16)

    return _rows(body, "mla_qkv", S, MLA_TK, [q, kv, kr, *tabs], [],
                 [(WIDTH_P, BF16), (WIDTH_P, BF16), (WIDTH_P, BF16)],
                 chunk_outs=[(N_HEADS, BF16), (N_HEADS, BF16)])


def _outnorm_fwd(oa, ob, ga, gb):
    S = oa.shape[0]

    def body(step, oa_ref, ob_ref, ga_ref, gb_ref, cat_ref):
        live = lax.broadcasted_iota(jnp.int32, oa_ref.shape, 1) % LANES < HEAD_DIM
        for o_ref, g_ref, off in ((oa_ref, ga_ref, 0), (ob_ref, gb_ref, WIDTH_P)):
            o = jnp.where(live, o_ref[...], 0.0)
            r = lax.rsqrt(jnp.sum(o * o, axis=-1, keepdims=True) * (1.0 / WIDTH) + EPS)
            cat_ref[:, off:off + WIDTH_P] = (o * r * g_ref[...]).astype(BF16)

    return _rows(body, "outnorm_fwd", S, 256, [oa, ob], [ga, gb], [(2 * WIDTH_P, BF16)])[0]


def _outnorm_bwd(oa, ob, ga, gb, dcat):
    S = oa.shape[0]

    def body(step, oa_ref, ob_ref, dcat_ref, ga_ref, gb_ref, dpa_ref, dob_ref, dga_ref, dgb_ref):
        live = lax.broadcasted_iota(jnp.int32, oa_ref.shape, 1) % LANES < HEAD_DIM
        lane = lax.broadcasted_iota(jnp.int32, (oa_ref.shape[0], LANES), 1)
        packed = oa_ref[...]
        o = jnp.where(live, packed, 0.0)
        do, dgr = _rms_bwd_math(o, ga_ref[...], dcat_ref[:, 0:WIDTH_P], WIDTH)
        _acc_add(step, dga_ref, jnp.sum(dgr, axis=0, keepdims=True))
        prod = do.astype(BF16).astype(F32) * o
        for h in range(N_HEADS):
            blk = slice(h * LANES, (h + 1) * LANES)
            delta = jnp.sum(prod[:, blk], axis=-1, keepdims=True)
            lse = jnp.sum(jnp.where(lane == HEAD_DIM, packed[:, blk], 0.0), axis=-1, keepdims=True)
            out = do[:, blk]
            for k, piece in enumerate(_split3(-delta) + _split3(-lse)):
                out = jnp.where(lane == HEAD_DIM + k, piece, out)
            dpa_ref[:, blk] = out

        ov = ob_ref[...]
        do_b, dgr_b = _rms_bwd_math(ov, gb_ref[...], dcat_ref[:, WIDTH_P:2 * WIDTH_P], WIDTH)
        dob_ref[...] = do_b.astype(BF16)
        _acc_add(step, dgb_ref, jnp.sum(dgr_b, axis=0, keepdims=True))

    return _rows(body, "outnorm_bwd", S, 256, [oa, ob, dcat], [ga, gb],
                 [(WIDTH_P, F32), (WIDTH_P, BF16)], [((1, WIDTH_P), F32), ((1, WIDTH_P), F32)])


def _post_mix(x, y, g_post, g_pre):
    S, W = x.shape

    def body(step, x_ref, y_ref, gp_ref, gq_ref, x1_ref, h_ref):
        yv = y_ref[...]
        r = lax.rsqrt(jnp.mean(yv * yv, axis=-1, keepdims=True) + EPS)
        x1 = x_ref[...] + yv * r * gp_ref[...]
        x1_ref[...] = x1
        r1 = lax.rsqrt(jnp.mean(x1 * x1, axis=-1, keepdims=True) + EPS)
        h_ref[...] = (x1 * r1 * gq_ref[...]).astype(BF16)

    return _rows(body, "post_mix", S, 512, [x, y], [g_post, g_pre], [(W, F32), (W, BF16)])


def _final(x1, y2, g, target):
    S, W = x1.shape
    nsteps = S // 256

    def body(step, x1_ref, y_ref, t_ref, g_ref, dx2_ref, dy_ref, dg_ref, sq_ref, loss_ref):
        yv = y_ref[...]
        gv = g_ref[...]
        r = lax.rsqrt(jnp.mean(yv * yv, axis=-1, keepdims=True) + EPS)
        yn = yv * r
        err = (x1_ref[...] + yn * gv) - t_ref[...]
        dx2 = err * (1.0 / W)
        dx2_ref[...] = dx2
        dyg = dx2 * gv
        dy = r * (dyg - yn * jnp.mean(dyg * yn, axis=-1, keepdims=True))
        dy_ref[...] = dy.astype(BF16)
        _acc_add(step, dg_ref, jnp.sum(dx2 * yn, axis=0, keepdims=True))
        _acc_add(step, sq_ref, jnp.sum(err * err, axis=0, keepdims=True))

        @pl.when(step == nsteps - 1)
        def _():
            tot = jnp.sum(sq_ref[...], axis=-1, keepdims=True) * (0.5 / W)
            loss_ref[...] = jnp.broadcast_to(tot, (1, LANES))

    return _rows(body, "final_loss", S, 256, [x1, y2, target], [g], [(W, F32), (W, BF16)],
                 [((1, W), F32), ((1, W), F32), ((1, LANES), F32)])


_GELU_C = math.sqrt(2.0 / math.pi)
_CONV_CHUNK = 128
_HALO = 8


def _gelu(g):
    t = jnp.tanh(_GELU_C * (g + 0.044715 * (g * g * g)))
    return g * (0.5 * (1.0 + t)), t


def _fill_padded(pad_ref, src_ref, S):
    zeros = jnp.zeros((_HALO, LANES), F32)
    pad_ref[0:_HALO, :] = zeros
    pad_ref[_HALO + S:2 * _HALO + S, :] = zeros
    for r0 in range(0, S, _CONV_CHUNK):
        pad_ref[_HALO + r0:_HALO + r0 + _CONV_CHUNK, :] = src_ref[r0:r0 + _CONV_CHUNK, :].astype(F32)


def _conv_fwd(u0, conv_w, conv_b):
    S, C2 = u0.shape
    nb = D_FF // LANES

    def body(u0g_ref, u0v_ref, wg_ref, wv_ref, bg_ref, bv_ref, ug_ref, uv_ref, a_ref, pg_ref, pv_ref):
        _fill_padded(pg_ref, u0g_ref, S)
        _fill_padded(pv_ref, u0v_ref, S)
        wg, wv = wg_ref[...], wv_ref[...]
        for r0 in range(0, S, _CONV_CHUNK):
            def conv(p_ref, w, b_ref):
                base = _HALO + r0
                return (p_ref[base - 1:base - 1 + _CONV_CHUNK, :] * w[0:1, :]
                        + p_ref[base:base + _CONV_CHUNK, :] * w[1:2, :]
                        + p_ref[base + 1:base + 1 + _CONV_CHUNK, :] * w[2:3, :] + b_ref[...])
            g = conv(pg_ref, wg, bg_ref)
            v = conv(pv_ref, wv, bv_ref)
            rows = slice(r0, r0 + _CONV_CHUNK)
            ug_ref[rows, :] = g
            uv_ref[rows, :] = v
            a_ref[rows, :] = (_gelu(g)[0] * v).astype(BF16)

    col = lambda off: pl.BlockSpec((S, LANES), lambda j: (0, j + off))
    wcol = lambda off: pl.BlockSpec((3, LANES), lambda j: (0, j + off))
    bcol = lambda off: pl.BlockSpec((1, LANES), lambda j: (0, j + off))
    ug, uv, a = pl.pallas_call(
        body, name="conv_gelu_fwd", grid=(nb,),
        in_specs=[col(0), col(nb), wcol(0), wcol(nb), bcol(0), bcol(nb)],
        out_specs=[col(0), col(0), col(0)],
        out_shape=[jax.ShapeDtypeStruct((S, D_FF), F32), jax.ShapeDtypeStruct((S, D_FF), F32),
                   jax.ShapeDtypeStruct((S, D_FF), BF16)],
        scratch_shapes=[pltpu.VMEM((S + 2 * _HALO, LANES), F32), pltpu.VMEM((S + 2 * _HALO, LANES), F32)],
        compiler_params=_cparams(("arbitrary",)),
    )(u0, u0, conv_w, conv_w, conv_b, conv_b)
    return ug, uv, a


def _conv_bwd(u0, ug, uv, da, conv_w):
    S = u0.shape[0]
    nb = D_FF // LANES

    def body(u0_ref, ug_ref, uv_ref, da_ref, w_ref, du0_ref, dw_ref, db_ref, pu_ref, pd_ref):
        is_g = pl.program_id(1) == 0
        _fill_padded(pu_ref, u0_ref, S)
        zeros = jnp.zeros((_HALO, LANES), F32)
        pd_ref[0:_HALO, :] = zeros
        pd_ref[_HALO + S:2 * _HALO + S, :] = zeros
        @pl.when(is_g)
        def _():
            for r0 in range(0, S, _CONV_CHUNK):
                rows = slice(r0, r0 + _CONV_CHUNK)
                g = ug_ref[rows, :]
                t = _gelu(g)[1]
                dgel = 0.5 * (1.0 + t) + (0.5 * g) * (1.0 - t * t) * (_GELU_C * (1.0 + 3.0 * 0.044715 * (g * g)))
                pd_ref[_HALO + r0:_HALO + r0 + _CONV_CHUNK, :] = da_ref[rows, :] * uv_ref[rows, :] * dgel

        @pl.when(jnp.logical_not(is_g))
        def _():
            for r0 in range(0, S, _CONV_CHUNK):
                rows = slice(r0, r0 + _CONV_CHUNK)
                pd_ref[_HALO + r0:_HALO + r0 + _CONV_CHUNK, :] = da_ref[rows, :] * _gelu(ug_ref[rows, :])[0]
        w = w_ref[...]
        acc_b = jnp.zeros((1, LANES), F32)
        acc_w = [jnp.zeros((1, LANES), F32) for _ in range(3)]
        for r0 in range(0, S, _CONV_CHUNK):
            base = _HALO + r0
            du_m = pd_ref[base - 1:base - 1 + _CONV_CHUNK, :]
            du_c = pd_ref[base:base + _CONV_CHUNK, :]
            du_p = pd_ref[base + 1:base + 1 + _CONV_CHUNK, :]
            du0_ref[r0:r0 + _CONV_CHUNK, :] = (du_p * w[0:1, :] + du_c * w[1:2, :] + du_m * w[2:3, :]).astype(BF16)
            acc_b = acc_b + jnp.sum(du_c, axis=0, keepdims=True)
            for k in range(3):
                acc_w[k] = acc_w[k] + jnp.sum(du_c * pu_ref[base + k - 1:base + k - 1 + _CONV_CHUNK, :],
                                              axis=0, keepdims=True)
        db_ref[...] = acc_b
        for k in range(3):
            dw_ref[k:k + 1, :] = acc_w[k]

    own = pl.BlockSpec((S, LANES), lambda j, half: (0, half * nb + j))
    shared = pl.BlockSpec((S, LANES), lambda j, half: (0, j))
    du0, dw, db = pl.pallas_call(
        body, name="conv_gelu_bwd", grid=(nb, 2),
        in_specs=[own, shared, shared, shared, pl.BlockSpec((3, LANES), lambda j, half: (0, half * nb + j))],
        out_specs=[own, pl.BlockSpec((3, LANES), lambda j, half: (0, half * nb + j)),
                   pl.BlockSpec((1, LANES), lambda j, half: (0, half * nb + j))],
        out_shape=[jax.ShapeDtypeStruct((S, 2 * D_FF), BF16), jax.ShapeDtypeStruct((3, 2 * D_FF), F32),
                   jax.ShapeDtypeStruct((1, 2 * D_FF), F32)],
        scratch_shapes=[pltpu.VMEM((S + 2 * _HALO, LANES), F32), pltpu.VMEM((S + 2 * _HALO, LANES), F32)],
        compiler_params=_cparams(("arbitrary", "arbitrary")),
    )(u0, ug, uv, da, conv_w)
    return du0, dw, db


DIL_Q = 128
DIL_HALF = 64
DIL_SLAB = DIL_Q + 2 * DIL_HALF
DILATIONS = (1, 4, 16)


def _lanes_hi_to_all(x):
    lane = lax.broadcasted_iota(jnp.int32, x.shape, 1)
    return jnp.where(lane < HEAD_DIM, pltpu.roll(x, HEAD_DIM, 1), x)


_NT = (((1,), (1,)), ((), ()))
_TN = (((0,), (0,)), ((), ()))
HEAD_COLS = 3 * LANES


def _slab_bias():
    row = jnp.arange(DIL_Q, dtype=jnp.int32)[:, None]
    col = jnp.arange(DIL_SLAB, dtype=jnp.int32)[None, :]
    slopes = jnp.exp2(-8.0 * jnp.arange(1, N_HEADS + 1, dtype=F32) / N_HEADS)
    out = []
    for r in DILATIONS:
        variants = []
        for shift in (DIL_HALF, 0, DIL_Q):
            ad = jnp.abs(col - shift - row)
            variants.append(jnp.where(ad <= DIL_HALF, -slopes[:, None, None] * (ad * r).astype(F32)[None], MASKED))
        out.append(jnp.stack(variants, axis=1))
    return jnp.stack(out, axis=0)


DIL_CHUNK = 512


def _block_geometry(i, nblk):
    first = pl.multiple_of(i * DIL_Q, DIL_Q)
    slab0 = pl.multiple_of(jnp.clip(i * DIL_Q - DIL_HALF, 0, (nblk - 2) * DIL_Q), DIL_HALF)
    variant = jnp.where(i == 0, 1, jnp.where(i == nblk - 1, 2, 0))
    return first, slab0, variant


def _class_rows(c, r, r0, n):
    return pl.ds(c + r0 * r, n, stride=r) if r > 1 else pl.ds(r0, n)


def _dila_fwd(proj_a, bias, exchange=None):
    S = proj_a.shape[0]
    lmax = S // DILATIONS[1]

    def compute(qh_ref, kh_ref, vh_ref, b_ref, o_ref, q_s, k_s, v_s, cm_s):
        lane = lax.broadcasted_iota(jnp.int32, (DIL_Q, LANES), 1)
        lane_s = lax.broadcasted_iota(jnp.int32, (DIL_SLAB, LANES), 1)
        lane_c = lax.broadcasted_iota(jnp.int32, (DIL_CHUNK, LANES), 1)

        def run(g, nblk, load_q, load_k, load_v, store):
            def block(i, carry):
                first, slab0, variant = _block_geometry(i, nblk)
                qv, ks, vs = load_q(first), load_k(slab0), load_v(slab0)
                s = lax.dot_general(qv, ks, _NT, preferred_element_type=F32) + b_ref[g, 0, variant]
                m = jnp.max(s, axis=-1, keepdims=True)
                acc = jnp.dot(jnp.exp(s - m).astype(BF16), vs, preferred_element_type=F32)
                l = _lanes_hi_to_all(acc)
                store(first, jnp.where(lane < HEAD_DIM, acc / l, m + jnp.log(l)))
                return carry

            lax.fori_loop(0, nblk, block, 0, unroll=min(8, nblk))

        def direct_store(first, val):
            o_ref[pl.ds(first, DIL_Q), :] = val

        run(0, S // DIL_Q,
            lambda f: (qh_ref[pl.ds(f, DIL_Q), :] * DIL_SCALE).astype(BF16),
            lambda s0: kh_ref[pl.ds(s0, DIL_SLAB), :].astype(BF16),
            lambda s0: jnp.where(lane_s < HEAD_DIM, vh_ref[pl.ds(s0, DIL_SLAB), :], 1.0).astype(BF16),
            direct_store)

        def cm_store(first, val):
            cm_s[pl.ds(first, DIL_Q), :] = val

        for g, r in list(enumerate(DILATIONS))[1:]:
            L = S // r
            n = min(L, DIL_CHUNK)
            for c in range(r):
                for r0 in range(0, L, n):
                    src = _class_rows(c, r, r0, n)
                    q_s[r0:r0 + n, :] = (qh_ref[src, :] * DIL_SCALE).astype(BF16)
                    k_s[r0:r0 + n, :] = kh_ref[src, :].astype(BF16)
                    v_s[r0:r0 + n, :] = jnp.where(lane_c[:n] < HEAD_DIM, vh_ref[src, :], 1.0).astype(BF16)
                run(g, L // DIL_Q, lambda f: q_s[pl.ds(f, DIL_Q), :], lambda s0: k_s[pl.ds(s0, DIL_SLAB), :],
                    lambda s0: v_s[pl.ds(s0, DIL_SLAB), :], cm_store)
                for r0 in range(0, L, n):
                    dst = _class_rows(c, r, r0, n)
                    a, b = cm_s[r0:r0 + n, :], o_ref[dst, :]
                    la, lb = _lanes_hi_to_all(a), _lanes_hi_to_all(b)
                    m = jnp.maximum(la, lb)
                    wa, wb = jnp.exp(la - m), jnp.exp(lb - m)
                    tot = wa + wb
                    o_ref[dst, :] = jnp.where(lane_c[:n] < HEAD_DIM, (wa * a + wb * b) / tot, m + jnp.log(tot))

    ex = exchange
    n = ex.n if ex else 0

    def body(*refs):
        ins, ex_in, o_ref, ex_out = refs[:4], refs[4:4 + n], refs[4 + n], refs[5 + n:5 + 2 * n]
        scratch, sems = refs[5 + 2 * n:9 + 2 * n], refs[9 + 2 * n:]
        if ex:
            pl.when(pl.program_id(0) == 0)(lambda: ex.start(ex_in, ex_out, sems))
        compute(*ins, o_ref, *scratch)
        if ex:
            pl.when(pl.program_id(0) == N_HEADS - 1)(lambda: ex.finish(ex_in, ex_out, sems))

    out = pl.pallas_call(
        body, name="dil_fwd", grid=(N_HEADS,),
        in_specs=[pl.BlockSpec((S, LANES), lambda h: (0, 3 * h)), pl.BlockSpec((S, LANES), lambda h: (0, 3 * h + 1)),
                  pl.BlockSpec((S, LANES), lambda h: (0, 3 * h + 2)),
                  pl.BlockSpec((len(DILATIONS), 1, 3, DIL_Q, DIL_SLAB), lambda h: (0, h, 0, 0, 0))] + [ANY] * n,
        out_specs=[pl.BlockSpec((S, LANES), lambda h: (0, h))] + [ANY] * n,
        out_shape=[jax.ShapeDtypeStruct((S, WIDTH_P), F32)] + (ex.out_shapes() if ex else []),
        scratch_shapes=[pltpu.VMEM((lmax, LANES), BF16), pltpu.VMEM((lmax, LANES), BF16),
                        pltpu.VMEM((lmax, LANES), BF16), pltpu.VMEM((lmax, LANES), F32)] + (ex.scratch() if ex else []),
        compiler_params=_cparams(("arbitrary",)),
    )(proj_a, proj_a, proj_a, bias, *(ex.arrays if ex else []))
    return out[0], list(out[1:])


N_SPLIT = 3


def _split3(x):
    hi = x.astype(BF16).astype(F32)
    mid = (x - hi).astype(BF16).astype(F32)
    lo = (x - hi - mid).astype(BF16).astype(F32)
    return hi, mid, lo


def _dila_bwd(proj_a, dopack, bias):
    S = proj_a.shape[0]
    lmax = S // DILATIONS[1]

    def body(qh_ref, kh_ref, vh_ref, d_ref, b_ref, out_ref, dq_ref, dk_ref, dv_ref, q_s, k_s, v_s, do_s,
             dq_c, dk_c, dv_c):
        def scalar_lanes(shape):
            lane = lax.broadcasted_iota(jnp.int32, shape, 1)
            return lane, (lane >= HEAD_DIM) & (lane < HEAD_DIM + N_SPLIT)

        def q_side(q, x):
            lane, ones = scalar_lanes(x.shape)
            lse_parts = pltpu.roll(x, LANES - N_SPLIT, 1)
            qv = jnp.where(lane < HEAD_DIM, q * DIL_SCALE, jnp.where(ones, lse_parts, 0.0)).astype(BF16)
            return qv, jnp.where(lane < HEAD_DIM + N_SPLIT, x, 0.0).astype(BF16)

        def kv_side(k, v):
            _, ones = scalar_lanes(k.shape)
            return jnp.where(ones, 1.0, k).astype(BF16), jnp.where(ones, 1.0, v).astype(BF16)

        def run(g, nblk, load_q, load_kv, dq_o, dk_o, dv_o):
            def block(i, carry):
                first, slab0, variant = _block_geometry(i, nblk)
                rows, slab = pl.ds(first, DIL_Q), pl.ds(slab0, DIL_SLAB)
                (qv, dov), (ks, vs) = load_q(rows), load_kv(slab)
                p = jnp.exp(lax.dot_general(qv, ks, _NT, preferred_element_type=F32) + b_ref[g, 0, variant])
                ds = (p * lax.dot_general(dov, vs, _NT, preferred_element_type=F32)).astype(BF16)
                dq_o[rows, :] = jnp.dot(ds, ks, preferred_element_type=F32) * DIL_SCALE
                dk_o[slab, :] += lax.dot_general(ds, qv, _TN, preferred_element_type=F32)
                dv_o[slab, :] += lax.dot_general(p.astype(BF16), dov, _TN, preferred_element_type=F32)
                return carry

            lax.fori_loop(0, nblk, block, 0, unroll=min(8, nblk))

        dk_ref[...] = jnp.zeros_like(dk_ref)
        dv_ref[...] = jnp.zeros_like(dv_ref)
        run(0, S // DIL_Q,
            lambda rows: q_side(qh_ref[rows, :], d_ref[rows, :]),
            lambda slab: kv_side(kh_ref[slab, :], vh_ref[slab, :]),
            dq_ref, dk_ref, dv_ref)

        for g, r in list(enumerate(DILATIONS))[1:]:
            L = S // r
            n = min(L, DIL_CHUNK)
            for c in range(r):
                for r0 in range(0, L, n):
                    src = _class_rows(c, r, r0, n)
                    q_s[r0:r0 + n, :], do_s[r0:r0 + n, :] = q_side(qh_ref[src, :], d_ref[src, :])
                    k_s[r0:r0 + n, :], v_s[r0:r0 + n, :] = kv_side(kh_ref[src, :], vh_ref[src, :])
                    dk_c[r0:r0 + n, :] = jnp.zeros((n, LANES), F32)
                    dv_c[r0:r0 + n, :] = jnp.zeros((n, LANES), F32)
                run(g, L // DIL_Q, lambda rows: (q_s[rows, :], do_s[rows, :]),
                    lambda slab: (k_s[slab, :], v_s[slab, :]), dq_c, dk_c, dv_c)
                for r0 in range(0, L, n):
                    dst = _class_rows(c, r, r0, n)
                    for acc, cls in ((dq_ref, dq_c), (dk_ref, dk_c), (dv_ref, dv_c)):
                        acc[dst, :] += cls[r0:r0 + n, :]

        for r0 in range(0, S, DIL_CHUNK):
            for part, ref in enumerate((dq_ref, dk_ref, dv_ref)):
                out_ref[r0:r0 + DIL_CHUNK, part * LANES:(part + 1) * LANES] = ref[r0:r0 + DIL_CHUNK, :].astype(BF16)

    bf = lambda rows: pltpu.VMEM((rows, LANES), BF16)
    f32 = lambda rows: pltpu.VMEM((rows, LANES), F32)
    return pl.pallas_call(
        body, name="dil_bwd", grid=(N_HEADS,),
        in_specs=[pl.BlockSpec((S, LANES), lambda h: (0, 3 * h), pipeline_mode=pl.Buffered(1)),
                  pl.BlockSpec((S, LANES), lambda h: (0, 3 * h + 1), pipeline_mode=pl.Buffered(1)),
                  pl.BlockSpec((S, LANES), lambda h: (0, 3 * h + 2), pipeline_mode=pl.Buffered(1)),
                  pl.BlockSpec((S, LANES), lambda h: (0, h), pipeline_mode=pl.Buffered(1)),
                  pl.BlockSpec((len(DILATIONS), 1, 3, DIL_Q, DIL_SLAB), lambda h: (0, h, 0, 0, 0))],
        out_specs=pl.BlockSpec((S, HEAD_COLS), lambda h: (0, h)),
        out_shape=jax.ShapeDtypeStruct((S, N_HEADS * HEAD_COLS), BF16),
        scratch_shapes=[f32(S), f32(S), f32(S), bf(lmax), bf(lmax), bf(lmax), bf(lmax),
                        f32(lmax), f32(lmax), f32(lmax)],
        compiler_params=_cparams(("arbitrary",)),
    )(proj_a, proj_a, proj_a, dopack, bias)


def _mla_fwd(q, k, vt):
    S = q.shape[0]
    tq, tk = MLA_TQ, MLA_TK
    nq, nk = S // tq, S // tk

    def body(q_ref, k_ref, vt_ref, o_ref, lse_ref, acc_ref):
        qv = q_ref[...]
        acc_ref[...] = jnp.zeros_like(acc_ref)

        def chunk(c, m):
            kc = k_ref[pl.ds(pl.multiple_of(c * tk, tk), tk), :]
            st = lax.dot_general(kc, qv, _NT, preferred_element_type=F32)
            m_new = jnp.maximum(m, jnp.max(st, axis=0, keepdims=True))
            pt = jnp.exp(st - m_new).astype(BF16)
            acc_ref[...] = jnp.exp(m - m_new) * acc_ref[...] + jnp.dot(vt_ref[0, c], pt,
                                                                        preferred_element_type=F32)
            return m_new

        m = lax.fori_loop(0, nk, chunk, jnp.full((1, tq), M_INIT, F32), unroll=2)
        acc = acc_ref[...]
        l = acc[HEAD_DIM:HEAD_DIM + 1, :]
        row = lax.broadcasted_iota(jnp.int32, acc.shape, 0)
        o_ref[...] = jnp.where(row < HEAD_DIM, acc / l, 0.0).T
        lse_ref[0] = m + jnp.log(l)

    return pl.pallas_call(
        body, name="mla_fwd", grid=(N_HEADS, nq),
        in_specs=[pl.BlockSpec((tq, LANES), lambda h, i: (i, h)),
                  pl.BlockSpec((S, LANES), lambda h, i: (0, h)),
                  pl.BlockSpec((1, nk, LANES, tk), lambda h, i: (h, 0, 0, 0))],
        out_specs=[pl.BlockSpec((tq, LANES), lambda h, i: (i, h)),
                   pl.BlockSpec((1, 1, tq), lambda h, i: (h, 0, i))],
        out_shape=[jax.ShapeDtypeStruct((S, WIDTH_P), F32), jax.ShapeDtypeStruct((N_HEADS, 1, S), F32)],
        scratch_shapes=[pltpu.VMEM((LANES, tq), F32)],
        compiler_params=_cparams(("parallel", "parallel")),
    )(q, k, vt)


def _mla_bwd(q, k, v, kt, do, o, lse, exchange=None):
    S = q.shape[0]
    tq, tk = MLA_TQ, MLA_TK
    nq, nk = S // tq, S // tk

    def compute(q_ref, do_ref, o_ref, lse_ref, k_ref, v_ref, kt_ref, dq_ref, dk_ref, dv_ref, dqt_ref):
        @pl.when(pl.program_id(1) == 0)
        def _():
            dk_ref[...] = jnp.zeros_like(dk_ref)
            dv_ref[...] = jnp.zeros_like(dv_ref)

        qv, dov = q_ref[...], do_ref[...]
        delta = jnp.sum((dov.astype(F32) * o_ref[...]).T, axis=0, keepdims=True)
        lse = lse_ref[0]
        dqt_ref[...] = jnp.zeros_like(dqt_ref)

        def chunk(c, carry):
            rows = pl.ds(pl.multiple_of(c * tk, tk), tk)
            kc, vc = k_ref[rows, :], v_ref[rows, :]
            pt = jnp.exp(lax.dot_general(kc, qv, _NT, preferred_element_type=F32) - lse)
            dv_ref[rows, :] += jnp.dot(pt.astype(BF16), dov, preferred_element_type=F32)
            dpt = lax.dot_general(vc, dov, _NT, preferred_element_type=F32)
            dst = (pt * (dpt - delta)).astype(BF16)
            dk_ref[rows, :] += jnp.dot(dst, qv, preferred_element_type=F32)
            dqt_ref[...] += jnp.dot(kt_ref[0, c], dst, preferred_element_type=F32)
            return carry

        lax.fori_loop(0, nk, chunk, 0, unroll=2)
        dq_ref[...] = (dqt_ref[...] * MLA_SCALE).T

    ex = exchange
    n = ex.n if ex else 0

    def body(*refs):
        ins, ex_in, outs, ex_out = refs[:7], refs[7:7 + n], refs[7 + n:10 + n], refs[10 + n:10 + 2 * n]
        dqt_ref, sems = refs[10 + 2 * n], refs[11 + 2 * n:]
        first = (pl.program_id(0) == 0) & (pl.program_id(1) == 0)
        last = (pl.program_id(0) == N_HEADS - 1) & (pl.program_id(1) == nq - 1)
        if ex:
            pl.when(first)(lambda: ex.start(ex_in, ex_out, sems))
        compute(*ins, *outs, dqt_ref)
        if ex:
            pl.when(last)(lambda: ex.finish(ex_in, ex_out, sems))

    qspec = pl.BlockSpec((tq, LANES), lambda h, i: (i, h))
    kspec = pl.BlockSpec((S, LANES), lambda h, i: (0, h))
    out = jax.ShapeDtypeStruct((S, WIDTH_P), F32)
    res = pl.pallas_call(
        body, name="mla_bwd", grid=(N_HEADS, nq),
        in_specs=[qspec, qspec, qspec, pl.BlockSpec((1, 1, tq), lambda h, i: (h, 0, i)), kspec, kspec,
                  pl.BlockSpec((1, nk, LANES, tk), lambda h, i: (h, 0, 0, 0))] + [ANY] * n,
        out_specs=[qspec, kspec, kspec] + [ANY] * n, out_shape=[out, out, out] + (ex.out_shapes() if ex else []),
        scratch_shapes=[pltpu.VMEM((LANES, tq), F32)] + (ex.scratch() if ex else []),
        compiler_params=_cparams(("arbitrary", "arbitrary")),
    )(q, do, o, lse, k, v, kt, *(ex.arrays if ex else []))
    return res[0], res[1], res[2], list(res[3:])


def _mla_bwd_prep(dq, dk, dv, tabs):
    S = dq.shape[0]

    def body(step, dq_ref, dk_ref, dv_ref, c_ref, sa_ref, sb_ref, dqp_ref, dkv_ref, dkr_ref):
        c, sa, sb = c_ref[...], sa_ref[...], sb_ref[...]
        dksum = jnp.zeros((dq_ref.shape[0], LANES), F32)
        for h in range(N_HEADS):
            blk = slice(h * LANES, (h + 1) * LANES)
            dqp_ref[:, blk] = _rope_transpose(dq_ref[:, blk], c, sa, sb).astype(BF16)
            dksum = dksum + dk_ref[:, blk]
        dkv_ref[:, 0:WIDTH_P] = dk_ref[...].astype(BF16)
        dkv_ref[:, WIDTH_P:2 * WIDTH_P] = dv_ref[...].astype(BF16)
        lane = lax.broadcasted_iota(jnp.int32, dksum.shape, 1)
        live = (lane >= HEAD_DIM) & (lane < HEAD_DIM + QK_ROPE)
        dkr_ref[...] = jnp.where(live, _rope_transpose(dksum, c, sa, sb), 0.0)

    return _rows(body, "mla_bwd_prep", S, 256, [dq, dk, dv, *tabs], [],
                 [(WIDTH_P, BF16), (2 * WIDTH_P, BF16), (LANES, F32)])


def _mla_norm_bwd(proj_b, dcq_n, dckv_n, dkr, g_q, g_kv):
    S = proj_b.shape[0]

    def body(step, p_ref, dcq_ref, dckv_ref, dkr_ref, gq_ref, gkv_ref, dp_ref, dgq_ref, dgkv_ref):
        dcq, dgq = _rms_bwd_math(p_ref[:, 0:Q_LORA], gq_ref[...], dcq_ref[...], Q_LORA)
        dckv, dgkv = _rms_bwd_math(p_ref[:, Q_LORA:Q_LORA + KV_LORA], gkv_ref[...], dckv_ref[...], KV_LORA)
        dp_ref[:, 0:Q_LORA] = dcq.astype(BF16)
        dp_ref[:, Q_LORA:Q_LORA + KV_LORA] = dckv.astype(BF16)
        dp_ref[:, Q_LORA + KV_LORA:TAIL_P] = dkr_ref[...].astype(BF16)
        _acc_add(step, dgq_ref, jnp.sum(dgq, axis=0, keepdims=True))
        _acc_add(step, dgkv_ref, jnp.sum(dgkv, axis=0, keepdims=True))

    return _rows(body, "mla_norm_bwd", S, 512, [proj_b, dcq_n, dckv_n, dkr], [g_q, g_kv], [(TAIL_P, BF16)],
                 [((1, Q_LORA), F32), ((1, KV_LORA), F32)])


def _pad_cols(w, d):
    lead = w.shape[:-1]
    w = w.reshape(lead + (N_HEADS, d))
    w = jnp.pad(w, [(0, 0)] * len(lead) + [(0, 0), (0, LANES - d)])
    return w.reshape(lead + (N_HEADS * LANES,))


def _unpad_cols(w, d):
    lead = w.shape[:-1]
    return w.reshape(lead + (N_HEADS, LANES))[..., :d].reshape(lead + (N_HEADS * d,))


def _pad_w_in(w_in):
    zeros = lambda n: jnp.zeros((D_MODEL, n), w_in.dtype)
    p = {}
    parts = [_pad_cols(w_in[:, i * WIDTH:(i + 1) * WIDTH], HEAD_DIM).reshape(D_MODEL, N_HEADS, 1, LANES)
             for i in range(3)]
    p['w_in_a'] = jnp.concatenate(parts, axis=2).reshape(D_MODEL, N_HEADS * HEAD_COLS)
    p['w_in_b'] = jnp.concatenate([w_in[:, 3 * WIDTH:3 * WIDTH + Q_LORA + KV_LORA], zeros(HEAD_DIM),
                                   w_in[:, D_IN - QK_ROPE:], zeros(LANES - HEAD_DIM - QK_ROPE)], axis=1)
    return p


def _pad_weights(w):
    p = {}
    p['w_uq'] = _pad_cols(w['w_uq'], HEAD_DIM + QK_ROPE)
    kv = w['w_ukv'].reshape(KV_LORA, N_HEADS, 2 * HEAD_DIM)
    p['w_ukv'] = jnp.concatenate([_pad_cols(kv[:, :, :HEAD_DIM].reshape(KV_LORA, WIDTH), HEAD_DIM),
                                  _pad_cols(kv[:, :, HEAD_DIM:].reshape(KV_LORA, WIDTH), HEAD_DIM)], axis=1)
    p['w_o'] = jnp.concatenate(
        [_pad_cols(w['w_o'][i * WIDTH:(i + 1) * WIDTH].T, HEAD_DIM).T for i in range(2)], axis=0)
    p['g_a'] = _pad_cols(w['out_norm_a'], HEAD_DIM)
    p['g_b'] = _pad_cols(w['out_norm_b'], HEAD_DIM)
    return p


def _unpad_grads(d):
    g = {}
    dwa = d['w_in_a'].reshape(D_MODEL, N_HEADS, 3, LANES)
    tail = d['w_in_b']
    g['w_in'] = jnp.concatenate(
        [dwa[:, :, i, :HEAD_DIM].reshape(D_MODEL, WIDTH) for i in range(3)]
        + [tail[:, :Q_LORA + KV_LORA], tail[:, Q_LORA + KV_LORA + HEAD_DIM:Q_LORA + KV_LORA + HEAD_DIM + QK_ROPE]],
        axis=1)
    g['w_uq'] = _unpad_cols(d['w_uq'], HEAD_DIM + QK_ROPE)
    dk = _unpad_cols(d['w_ukv'][:, :WIDTH_P], HEAD_DIM).reshape(KV_LORA, N_HEADS, HEAD_DIM)
    dv = _unpad_cols(d['w_ukv'][:, WIDTH_P:], HEAD_DIM).reshape(KV_LORA, N_HEADS, HEAD_DIM)
    g['w_ukv'] = jnp.concatenate([dk, dv], axis=2).reshape(KV_LORA, 2 * WIDTH)
    g['w_o'] = jnp.concatenate(
        [_unpad_cols(d['w_o'][i * WIDTH_P:(i + 1) * WIDTH_P].T, HEAD_DIM).T for i in range(2)], axis=0)
    g['out_norm_a'] = _unpad_cols(d['g_a'], HEAD_DIM)
    g['out_norm_b'] = _unpad_cols(d['g_b'], HEAD_DIM)
    return g


LATE = ['w_uq', 'w_ukv', 'w_o', 'w_up', 'w_down']
EARLY_GRADS = ['w_up', 'w_down']


def _assemble_late(gathered):
    g = dict(zip(LATE, gathered))
    return {'w_uq': jnp.concatenate([g['w_uq'][i] for i in range(N_CHIPS)], axis=1),
            'w_ukv': jnp.concatenate([g['w_ukv'][i] for i in range(N_CHIPS)], axis=1),
            'w_o': g['w_o'].reshape(D_MODEL, D_MODEL),
            'w_down': g['w_down'].reshape(D_FF, D_MODEL),
            'w_up': g['w_up']}


def _local_step(x, target, w, late_shards=None):
    S = x.shape[0]
    w = dict(w)
    p = _pad_w_in(w['w_in'])
    tabs = _rope_tables(S)
    bias = _slab_bias()

    h1 = _rms_fwd(x, w['norm_mix_pre'], "rms_mix_pre")
    proj_a = _mm(h1, p['w_in_a'], 'nn', F32, "mm_in_a")
    proj_b = _mm(h1, p['w_in_b'], 'nn', F32, "mm_in_b")
    gather = _Exchange([late_shards[n] for n in LATE], False) if late_shards else None
    oa, gathered = _dila_fwd(proj_a, bias, gather)
    if late_shards:
        w.update(_assemble_late(gathered))
    p.update(_pad_weights(w))
    cq_n, ckv_n, kr = _mla_prep(proj_b, w['q_lat_norm'], w['kv_lat_norm'], tabs)
    q_lin = _mm(cq_n, p['w_uq'], 'nn', F32, "mm_uq")
    kv_lin = _mm(ckv_n, p['w_ukv'], 'nn', F32, "mm_ukv")
    qb, kb, vb, kt, vt = _mla_qkv(q_lin, kv_lin, kr, tabs)
    ob, lse_b = _mla_fwd(qb, kb, vt)
    cat = _outnorm_fwd(oa, ob, p['g_a'], p['g_b'])
    y = _mm(cat, p['w_o'], 'nn', F32, "mm_o")
    x1, h2 = _post_mix(x, y, w['norm_mix_post'], w['norm_ffn_pre'])
    u0 = _mm(h2, w['w_up'], 'nn', F32, "mm_up", sharded=True)
    ug, uv, a = _conv_fwd(u0, w['conv_w'], w['conv_b'])
    y2 = _mm(a, w['w_down'], 'nn', F32, "mm_down")
    dx2, dy2, dg_ffn_post, _, loss = _final(x1, y2, w['norm_ffn_post'], target)

    g = {'norm_ffn_post': dg_ffn_post}
    da = _mm(dy2, w['w_down'], 'nt', F32, "mm_down_dx")
    g['w_down'] = _mm(a, dy2, 'tn', BF16, "mm_down_dw")
    du0, g['conv_w'], g['conv_b'] = _conv_bwd(u0, ug, uv, da, w['conv_w'])
    dh2 = _mm(du0, w['w_up'], 'nt', F32, "mm_up_dx", sharded=True)
    g['w_up'] = _mm(h2, du0, 'tn', BF16, "mm_up_dw", sharded=True)
    dx1, g['norm_ffn_pre'] = _rms_bwd(x1, w['norm_ffn_pre'], [dh2], dx2, F32, "rms_ffn_pre_bwd")
    dy, g['norm_mix_post'] = _rms_bwd(y, w['norm_mix_post'], [dx1], None, BF16, "rms_mix_post_bwd")
    dcat = _mm(dy, p['w_o'], 'nt', F32, "mm_o_dx")
    dpad = {'w_o': _mm(cat, dy, 'tn', F32, "mm_o_dw")}
    dopack_a, do_b, dpad['g_a'], dpad['g_b'] = _outnorm_bwd(oa, ob, p['g_a'], p['g_b'], dcat)

    scatter = None
    if late_shards:
        scatter = _Exchange([g['w_up'], g['w_down'].reshape(N_CHIPS, D_FF // N_CHIPS, D_MODEL)], True)
    dq_b, dk_b, dv_b, received = _mla_bwd(qb, kb, vb, kt, do_b, ob, lse_b, scatter)
    if late_shards:
        g.update(zip(EARLY_GRADS, received))
    dq_pre, dkv, dkr = _mla_bwd_prep(dq_b, dk_b, dv_b, tabs)
    dcq_n = _mm(dq_pre, p['w_uq'], 'nt', F32, "mm_uq_dx")
    dpad['w_uq'] = _mm(cq_n, dq_pre, 'tn', F32, "mm_uq_dw")
    dckv_n = _mm(dkv, p['w_ukv'], 'nt', F32, "mm_ukv_dx")
    dpad['w_ukv'] = _mm(ckv_n, dkv, 'tn', F32, "mm_ukv_dw")
    dproj_b, g['q_lat_norm'], g['kv_lat_norm'] = _mla_norm_bwd(proj_b, dcq_n, dckv_n, dkr,
                                                               w['q_lat_norm'], w['kv_lat_norm'])

    dproj_a = _dila_bwd(proj_a, dopack_a, bias)
    dh1 = _mm(dproj_b, p['w_in_b'], 'nt', F32, "mm_in_b_dx")
    dh1 = _mm(dproj_a, p['w_in_a'], 'nt', F32, "mm_in_a_dx", add=dh1)
    dpad['w_in_a'] = _mm(h1, dproj_a, 'tn', F32, "mm_in_a_dw")
    dpad['w_in_b'] = _mm(h1, dproj_b, 'tn', F32, "mm_in_b_dw")
    grad_x, g['norm_mix_pre'] = _rms_bwd(x, w['norm_mix_pre'], [dh1], dx1, F32, "rms_mix_pre_bwd")
    g.update(_unpad_grads(dpad))
    return loss, grad_x, g


MESH = pl.DeviceIdType.MESH
ANY = pl.BlockSpec(memory_space=pl.ANY)


def _place():
    x, y, c = lax.axis_index("x"), lax.axis_index("y"), lax.axis_index("c")
    chips = [(1 - x, y), (x, 1 - y), (1 - x, 1 - y)]
    return x, y, c, chips


class _Exchange:
    def __init__(self, arrays, scatter):
        self.arrays, self.scatter, self.n = list(arrays), scatter, len(arrays)

    def out_shapes(self):
        return [jax.ShapeDtypeStruct(a.shape if self.scatter else (N_CHIPS,) + a.shape, a.dtype) for a in self.arrays]

    def scratch(self):
        return [pltpu.SemaphoreType.DMA((3 * self.n,)), pltpu.SemaphoreType.DMA((3 * self.n,)),
                pltpu.SemaphoreType.DMA((self.n,))]

    def _copies(self, in_refs, out_refs, sems, arrivals):
        send_sems, recv_sems, local_sems = sems
        x, y, c, chips = _place()
        me = 2 * x + y
        src = lambda b, chip: in_refs[b].at[chip] if self.scatter else in_refs[b]
        local = [pltpu.make_async_copy(src(b, me), out_refs[b].at[me], local_sems.at[b]) for b in range(self.n)]
        sends, recvs = [], []
        for j, (px, py) in enumerate(chips):
            for b in range(self.n):
                k = j * self.n + b
                common = dict(send_sem=send_sems.at[k], recv_sem=recv_sems.at[k], device_id=(px, py, c),
                              device_id_type=MESH)
                sends.append(pltpu.make_async_remote_copy(src_ref=src(b, 2 * px + py), dst_ref=out_refs[b].at[me],
                                                          **common))
                if arrivals:
                    recvs.append(pltpu.make_async_remote_copy(src_ref=src(b, me), dst_ref=out_refs[b].at[2 * px + py],
                                                              **common))
        return local, sends, recvs

    def start(self, in_refs, out_refs, sems):
        local, sends, _ = self._copies(in_refs, out_refs, sems, False)
        for cp in local + sends:
            cp.start()

    def finish(self, in_refs, out_refs, sems):
        local, sends, recvs = self._copies(in_refs, out_refs, sems, True)
        for cp in recvs:
            cp.wait_recv()
        for cp in sends:
            cp.wait_send()
        for cp in local:
            cp.wait()


def _exchange_call(arrays, scatter, name):
    ex = _Exchange(arrays, scatter)
    n = ex.n

    def body(*refs):
        in_refs, out_refs, sems = refs[:n], refs[n:2 * n], refs[2 * n:]
        ex.start(in_refs, out_refs, sems)
        ex.finish(in_refs, out_refs, sems)

    return pl.pallas_call(body, name=name, in_specs=[ANY] * n, out_specs=[ANY] * n, out_shape=ex.out_shapes(),
                          scratch_shapes=ex.scratch())(*ex.arrays)


def _all_gather(bufs, name="gather_weights"):
    return _exchange_call(bufs, False, name)


def _scatter_grads(slots, name="scatter_grads"):
    return _exchange_call(slots, True, name)


ELEMENTWISE_BLOCK = 256 * 1024


def _row_tile(rows, cols):
    best = None
    for t in range(16, min(rows, max(16, ELEMENTWISE_BLOCK // cols)) + 1, 16):
        if rows % t == 0:
            best = t
    return best if best is not None else rows


def _sum_slots(recv, name):
    _, R, C = recv.shape
    tr = _row_tile(R, C)

    def body(r_ref, o_ref):
        f = lambda i: r_ref[i].astype(F32)
        o_ref[...] = ((f(0) + f(1)) + f(2)) + f(3)

    return pl.pallas_call(
        body, name="sum_" + name, grid=(R // tr,),
        in_specs=[pl.BlockSpec((N_CHIPS, tr, C), lambda i: (0, i, 0))],
        out_specs=pl.BlockSpec((tr, C), lambda i: (i, 0)),
        out_shape=jax.ShapeDtypeStruct((R, C), F32),
        compiler_params=_cparams(("parallel",)),
    )(recv)


def _swap_sibling(parts):
    n = len(parts)

    def body(*refs):
        p_refs, o_refs, send_sems, recv_sems = refs[:n], refs[n:2 * n], refs[2 * n], refs[2 * n + 1]
        x, y, c, _ = _place()
        cps = [pltpu.make_async_remote_copy(src_ref=p_refs[b], dst_ref=o_refs[b], send_sem=send_sems.at[b],
                                            recv_sem=recv_sems.at[b], device_id=(x, y, 1 - c), device_id_type=MESH)
               for b in range(n)]
        for cp in cps:
            cp.start()
        for cp in cps:
            cp.wait()

    return pl.pallas_call(
        body, name="swap_sibling", in_specs=[ANY] * n, out_specs=[ANY] * n,
        out_shape=[jax.ShapeDtypeStruct(p.shape, p.dtype) for p in parts],
        scratch_shapes=[pltpu.SemaphoreType.DMA((n,)), pltpu.SemaphoreType.DMA((n,))],
    )(*parts)


def _adamw(g0, g1, w, m, v, name, offset=0):
    R, C = w.shape
    tr = _row_tile(R, C)
    packed = g0.shape != w.shape
    bc1 = 1.0 - ADAM_B1 ** ADAM_STEP
    bc2 = 1.0 - ADAM_B2 ** ADAM_STEP

    def body(g0_ref, g1_ref, w_ref, m_ref, v_ref, g_ref, d_ref, nm_ref, nv_ref):
        if packed:
            g = g0_ref[:, offset:offset + C] + g1_ref[:, offset:offset + C]
        else:
            g = g0_ref[...] + g1_ref[...]
        g_ref[...] = g
        nm = ADAM_B1 * m_ref[...] + (1.0 - ADAM_B1) * g
        nv = ADAM_B2 * v_ref[...] + (1.0 - ADAM_B2) * (g * g)
        nm_ref[...] = nm
        nv_ref[...] = nv
        d_ref[...] = -ADAM_LR * ((nm / bc1) / (jnp.sqrt(nv / bc2) + ADAM_EPS) + ADAM_WD * w_ref[...])

    spec = pl.BlockSpec((tr, C), lambda i: (i, 0))
    gspec = pl.BlockSpec(g0.shape, lambda i: (0, 0)) if packed else spec
    out = jax.ShapeDtypeStruct((R, C), F32)
    return pl.pallas_call(
        body, name="adamw_" + name, grid=(R // tr,), in_specs=[gspec, gspec, spec, spec, spec],
        out_specs=[spec] * 4, out_shape=[out] * 4, compiler_params=_cparams(("parallel",)),
    )(g0, g1, w, m, v)


def kernel(x, norm_mix_pre, w_in, q_lat_norm, w_uq, kv_lat_norm, w_ukv, out_norm_a, out_norm_b, w_o, norm_mix_post, norm_ffn_pre, w_up, conv_w, conv_b, w_down, norm_ffn_post, loss_target, m_norm_mix_pre, m_w_in, m_q_lat_norm, m_w_uq, m_kv_lat_norm, m_w_ukv, m_out_norm_a, m_out_norm_b, m_w_o, m_norm_mix_post, m_norm_ffn_pre, m_w_up, m_conv_w, m_conv_b, m_w_down, m_norm_ffn_post, v_norm_mix_pre, v_w_in, v_q_lat_norm, v_w_uq, v_kv_lat_norm, v_w_ukv, v_out_norm_a, v_out_norm_b, v_w_o, v_norm_mix_post, v_norm_ffn_pre, v_w_up, v_conv_w, v_conv_b, v_w_down, v_norm_ffn_post):
    args = dict(locals())
    strip = lambda a: a[0] if a.ndim == 3 else a
    wl = {n: strip(args[n]) for n in WEIGHTS}
    ml = {n: strip(args['m_' + n]) for n in WEIGHTS}
    vl = {n: strip(args['v_' + n]) for n in WEIGHTS}

    gathered = _all_gather([wl['w_in'].astype(BF16), wl['conv_w']])
    full = {n: wl[n] for n in SMALL}
    for n, a in zip(('w_in', 'conv_w'), gathered):
        full[n] = jnp.concatenate([a[i] for i in range(N_CHIPS)], axis=1)

    loss_b, grad_x, g = _local_step(x[0], loss_target[0], full, {n: wl[n].astype(BF16) for n in LATE})

    sharded = [n for n in WEIGHTS if SHARD_AXIS[n] is not None]
    late = [n for n in sharded if n not in EARLY_GRADS]

    def slots_of(n):
        a = g[n].astype(BF16)
        if SHARD_AXIS[n] == 0:
            return a.reshape(N_CHIPS, a.shape[0] // N_CHIPS, a.shape[1])
        cols = a.shape[1] // N_CHIPS
        return jnp.stack([a[:, i * cols:(i + 1) * cols] for i in range(N_CHIPS)])

    small_pack = jnp.concatenate([g[n] for n in SMALL], axis=1)
    slots = [slots_of(n) for n in late] + [jnp.broadcast_to(small_pack[None], (N_CHIPS,) + small_pack.shape)]
    recv = dict(zip(late + ['small'], _scatter_grads(slots)))
    recv.update({n: g[n] for n in EARLY_GRADS})
    parts = [_sum_slots(recv[n], n) for n in sharded + ['small']]
    others = _swap_sibling(parts)

    outs = {}

    def record(n, results):
        for tag, a in zip(('grad', 'delta', 'new_m', 'new_v'), results):
            outs[tag + '_' + n] = a.reshape(args[n].shape)

    for n, p0, p1 in zip(sharded, parts, others):
        record(n, _adamw(p0, p1, wl[n], ml[n], vl[n], n))
    offset = 0
    for n in SMALL:
        record(n, _adamw(parts[-1], others[-1], wl[n], ml[n], vl[n], n, offset=offset))
        offset += wl[n].shape[1]

    loss = lax.psum(loss_b[0, 0], ("x", "y", "c"))
    return (loss, grad_x[None], *[outs['grad_' + n] for n in WEIGHTS], *[outs['delta_' + n] for n in WEIGHTS],
            *[outs['new_m_' + n] for n in WEIGHTS], *[outs['new_v_' + n] for n in WEIGHTS])
```

```python
import math

import jax
import jax.numpy as jnp
from jax import lax
from jax.experimental import pallas as pl
from jax.experimental.pallas import tpu as pltpu

F32 = jnp.float32
BF16 = jnp.bfloat16

LANES = 128
D_MODEL = 1024
N_HEADS = 8
HEAD_DIM = 64
QK_ROPE = 32
Q_LORA = 384
KV_LORA = 256
D_FF = 2816
WIDTH = N_HEADS * HEAD_DIM
WIDTH_P = N_HEADS * LANES
IN_SIZES = (WIDTH, WIDTH, WIDTH, Q_LORA, KV_LORA, QK_ROPE)
D_IN = sum(IN_SIZES)
TAIL_P = Q_LORA + KV_LORA + LANES
EPS = 1e-6
ROPE_BASE = 10000.0
MASKED = -2e30
M_INIT = -1e30
MLA_TQ = 4096
MLA_TK = 256
MLA_SCALE = (HEAD_DIM + QK_ROPE) ** -0.5
V_ROWS = HEAD_DIM + 16
DIL_SCALE = HEAD_DIM ** -0.5

ADAM_LR = 0.001
ADAM_B1 = 0.9
ADAM_B2 = 0.999
ADAM_EPS = 1e-08
ADAM_WD = 0.01
ADAM_STEP = 10

VMEM_LIMIT = 56 * 1024 * 1024

N_CHIPS = 4

WEIGHTS = ['norm_mix_pre', 'w_in', 'q_lat_norm', 'w_uq', 'kv_lat_norm', 'w_ukv', 'out_norm_a', 'out_norm_b',
           'w_o', 'norm_mix_post', 'norm_ffn_pre', 'w_up', 'conv_w', 'conv_b', 'w_down', 'norm_ffn_post']
SHARD_AXIS = {'norm_mix_pre': None, 'w_in': 1, 'q_lat_norm': None, 'w_uq': 1, 'kv_lat_norm': None, 'w_ukv': 1,
              'out_norm_a': None, 'out_norm_b': None, 'w_o': 0, 'norm_mix_post': None, 'norm_ffn_pre': None,
              'w_up': 1, 'conv_w': 1, 'conv_b': None, 'w_down': 0, 'norm_ffn_post': None}
SMALL = [n for n in WEIGHTS if SHARD_AXIS[n] is None]


def _tile(dim, target):
    best = None
    t = LANES
    while t <= min(dim, target):
        if dim % t == 0:
            best = t
        t += LANES
    return best if best is not None else dim


def _cparams(sem=None):
    return pltpu.CompilerParams(dimension_semantics=sem, vmem_limit_bytes=VMEM_LIMIT)


def _mm(a, b, mode, out_dtype, name, add=None, tm=1024, tn=1024, tk=1024, sharded=False):
    if mode == 'nn':
        (M, K), (K2, N) = a.shape, ((b.shape[1], N_CHIPS * b.shape[2]) if sharded else b.shape)
        dims = (((1,), (0,)), ((), ()))
    elif mode == 'nt':
        (M, K), (N, K2) = a.shape, ((b.shape[1], N_CHIPS * b.shape[2]) if sharded else b.shape)
        dims = (((1,), (1,)), ((), ()))
    else:
        (K, M), (K2, N) = a.shape, b.shape
        dims = (((0,), (0,)), ((), ()))
    assert K == K2, (a.shape, b.shape, mode)
    tm, tn, tk = _tile(M, tm), _tile(N, tn), _tile(K, tk)
    if K == D_FF:
        tk = K
    if N == D_FF:
        tn, tm = N, min(tm, 512)
    if M == D_FF:
        tm = M
    if sharded and mode == 'nt':
        tk = K // N_CHIPS
    elif sharded:
        tn = N // N_CHIPS
    nk = K // tk
    if mode == 'nn':
        a_spec = pl.BlockSpec((tm, tk), lambda i, j, k: (i, k))
        b_spec = (pl.BlockSpec((None, tk, tn), lambda i, j, k: (j, k, 0)) if sharded
                  else pl.BlockSpec((tk, tn), lambda i, j, k: (k, j)))
    elif mode == 'nt':
        a_spec = pl.BlockSpec((tm, tk), lambda i, j, k: (i, k))
        b_spec = (pl.BlockSpec((None, tn, tk), lambda i, j, k: (k, j, 0)) if sharded
                  else pl.BlockSpec((tn, tk), lambda i, j, k: (j, k)))
    else:
        a_spec = pl.BlockSpec((tk, tm), lambda i, j, k: (k, i))
        b_spec = pl.BlockSpec((tk, tn), lambda i, j, k: (k, j))
    o_spec = pl.BlockSpec((tm, tn), lambda i, j, k: (i, j))
    out_shape = jax.ShapeDtypeStruct((M, N), out_dtype)
    if sharded and mode == 'tn':
        o_spec = pl.BlockSpec((None, tm, tn), lambda i, j, k: (j, i, 0))
        out_shape = jax.ShapeDtypeStruct((N_CHIPS, M, tn), out_dtype)
    has_add = add is not None

    def body(*refs):
        if has_add:
            a_ref, b_ref, add_ref, o_ref, acc_ref = refs
        else:
            a_ref, b_ref, o_ref, acc_ref = refs
        k = pl.program_id(2)

        @pl.when(k == 0)
        def _():
            acc_ref[...] = jnp.zeros_like(acc_ref)

        acc_ref[...] += lax.dot_general(a_ref[...].astype(BF16), b_ref[...].astype(BF16), dims,
                                        preferred_element_type=F32)

        @pl.when(k == nk - 1)
        def _():
            r = acc_ref[...]
            if has_add:
                r = r + add_ref[...]
            o_ref[...] = r.astype(o_ref.dtype)

    ins = [a, b] + ([add] if has_add else [])
    in_specs = [a_spec, b_spec] + ([o_spec] if has_add else [])
    return pl.pallas_call(
        body, name=name, grid=(M // tm, N // tn, nk), in_specs=in_specs, out_specs=o_spec, out_shape=out_shape,
        scratch_shapes=[pltpu.VMEM((tm, tn), F32)],
        compiler_params=_cparams(("parallel", "parallel", "arbitrary")),
    )(*ins)


def _rows(body, name, S, ts, row_ins, full_ins, row_outs, acc_outs=(), chunk_outs=()):
    in_specs = [pl.BlockSpec((ts, a.shape[1]), lambda i: (i, 0)) for a in row_ins]
    in_specs += [pl.BlockSpec(a.shape, lambda i, nd=a.ndim: (0,) * nd) for a in full_ins]
    out_specs = [pl.BlockSpec((ts, w), lambda i: (i, 0)) for (w, _) in row_outs]
    out_specs += [pl.BlockSpec(shape, lambda i, nd=len(shape): (0,) * nd) for (shape, _) in acc_outs]
    out_specs += [pl.BlockSpec((lead, 1, LANES, ts), lambda i: (0, i, 0, 0)) for (lead, _) in chunk_outs]
    out_shape = [jax.ShapeDtypeStruct((S, w), dt) for (w, dt) in row_outs]
    out_shape += [jax.ShapeDtypeStruct(shape, dt) for (shape, dt) in acc_outs]
    out_shape += [jax.ShapeDtypeStruct((lead, S // ts, LANES, ts), dt) for (lead, dt) in chunk_outs]

    def kbody(*refs):
        body(pl.program_id(0), *refs)

    return pl.pallas_call(
        kbody, name=name, grid=(S // ts,), in_specs=in_specs, out_specs=out_specs, out_shape=out_shape,
        compiler_params=_cparams(("arbitrary",)),
    )(*row_ins, *full_ins)


def _acc_add(step, ref, val):
    @pl.when(step == 0)
    def _():
        ref[...] = val

    @pl.when(step != 0)
    def _():
        ref[...] += val


def _rms_fwd(x, g, name):
    S, W = x.shape

    def body(step, x_ref, g_ref, h_ref):
        xv = x_ref[...]
        r = lax.rsqrt(jnp.mean(xv * xv, axis=-1, keepdims=True) + EPS)
        h_ref[...] = (xv * r * g_ref[...]).astype(BF16)

    return _rows(body, name, S, 512, [x], [g], [(W, BF16)])[0]


def _rms_bwd_math(xv, g, dy, width):
    r = lax.rsqrt(jnp.sum(xv * xv, axis=-1, keepdims=True) * (1.0 / width) + EPS)
    xn = xv * r
    dyg = dy * g
    dx = r * (dyg - xn * (jnp.sum(dyg * xn, axis=-1, keepdims=True) * (1.0 / width)))
    return dx, dy * xn


def _rms_bwd(x, g, dys, resid, out_dtype, name):
    S, W = x.shape
    nd = len(dys)
    has_res = resid is not None

    def body(step, *refs):
        x_ref = refs[0]
        dy_refs = refs[1:1 + nd]
        pos = 1 + nd
        res_ref = refs[pos] if has_res else None
        pos += int(has_res)
        g_ref, dx_ref, dg_ref = refs[pos], refs[pos + 1], refs[pos + 2]
        dy = dy_refs[0][...].astype(F32)
        for r_ in dy_refs[1:]:
            dy = dy + r_[...].astype(F32)
        dx, dgr = _rms_bwd_math(x_ref[...], g_ref[...], dy, W)
        if has_res:
            dx = dx + res_ref[...]
        dx_ref[...] = dx.astype(dx_ref.dtype)
        _acc_add(step, dg_ref, jnp.sum(dgr, axis=0, keepdims=True))

    row_ins = [x] + list(dys) + ([resid] if has_res else [])
    dx, dg = _rows(body, name, S, 256, row_ins, [g], [(W, out_dtype)], [((1, W), F32)])
    return dx, dg


def _rope_apply(xv, c, sa, sb):
    return xv * c + pltpu.roll(xv, 16, 1) * sa + pltpu.roll(xv, LANES - 16, 1) * sb


def _rope_transpose(dy, c, sa, sb):
    return dy * c + pltpu.roll(dy * sa, LANES - 16, 1) + pltpu.roll(dy * sb, 16, 1)


def _rope_tables(S):
    pos = jnp.arange(S, dtype=F32)
    inv_freq = jnp.exp(-math.log(ROPE_BASE) * jnp.arange(0, QK_ROPE, 2, dtype=F32) / QK_ROPE)
    ang = pos[:, None] * inv_freq[None, :]
    cos, sin = jnp.cos(ang), jnp.sin(ang)
    ones, zeros = jnp.ones((S, HEAD_DIM), F32), jnp.zeros((S, HEAD_DIM), F32)
    z16, z32 = jnp.zeros((S, 16), F32), jnp.zeros((S, 32), F32)
    c = jnp.concatenate([ones, cos, cos, z32], axis=1)
    sa = jnp.concatenate([zeros, z16, sin, z32], axis=1)
    sb = jnp.concatenate([zeros, -sin, z16, z32], axis=1)
    return c, sa, sb


def _mla_prep(proj_b, g_q, g_kv, tabs):
    S = proj_b.shape[0]

    def body(step, p_ref, c_ref, sa_ref, sb_ref, gq_ref, gkv_ref, cq_ref, ckv_ref, kr_ref):
        cq = p_ref[:, 0:Q_LORA]
        ckv = p_ref[:, Q_LORA:Q_LORA + KV_LORA]
        kr = p_ref[:, Q_LORA + KV_LORA:TAIL_P]
        rq = lax.rsqrt(jnp.mean(cq * cq, axis=-1, keepdims=True) + EPS)
        cq_ref[...] = (cq * rq * gq_ref[...]).astype(BF16)
        rk = lax.rsqrt(jnp.mean(ckv * ckv, axis=-1, keepdims=True) + EPS)
        ckv_ref[...] = (ckv * rk * gkv_ref[...]).astype(BF16)
        kr_ref[...] = _rope_apply(kr, c_ref[...], sa_ref[...], sb_ref[...])

    return _rows(body, "mla_prep", S, 512, [proj_b, *tabs], [g_q, g_kv],
                 [(Q_LORA, BF16), (KV_LORA, BF16), (LANES, F32)])


def _mla_qkv(q, kv, kr, tabs):
    S = q.shape[0]

    def body(step, q_ref, kv_ref, kr_ref, c_ref, sa_ref, sb_ref, qb_ref, kb_ref, vb_ref, kt_ref, vt_ref):
        c, sa, sb = c_ref[...], sa_ref[...], sb_ref[...]
        krv = kr_ref[...]
        row = lax.broadcasted_iota(jnp.int32, (LANES, MLA_TK), 0)
        for h in range(N_HEADS):
            blk = slice(h * LANES, (h + 1) * LANES)
            qb_ref[:, blk] = (_rope_apply(q_ref[:, blk], c, sa, sb) * MLA_SCALE).astype(BF16)
            kh = kv_ref[:, blk] + krv
            kb_ref[:, blk] = kh.astype(BF16)
            kt_ref[h, 0] = kh.T.astype(BF16)
            vh = kv_ref[:, WIDTH_P + h * LANES:WIDTH_P + (h + 1) * LANES]
            vt_ref[h, 0] = jnp.where(row == HEAD_DIM, 1.0, vh.T).astype(BF16)
        vb_ref[...] = kv_ref[:, WIDTH_P:2 * WIDTH_P].astype(BF16)

    return _rows(body, "mla_qkv", S, MLA_TK, [q, kv, kr, *tabs], [],
                 [(WIDTH_P, BF16), (WIDTH_P, BF16), (WIDTH_P, BF16)],
                 chunk_outs=[(N_HEADS, BF16), (N_HEADS, BF16)])


def _outnorm_fwd(oa, ob, ga, gb):
    S = oa.shape[0]

    def body(step, oa_ref, ob_ref, ga_ref, gb_ref, cat_ref):
        live = lax.broadcasted_iota(jnp.int32, oa_ref.shape, 1) % LANES < HEAD_DIM
        for o_ref, g_ref, off in ((oa_ref, ga_ref, 0), (ob_ref, gb_ref, WIDTH_P)):
            o = jnp.where(live, o_ref[...], 0.0)
            r = lax.rsqrt(jnp.sum(o * o, axis=-1, keepdims=True) * (1.0 / WIDTH) + EPS)
            cat_ref[:, off:off + WIDTH_P] = (o * r * g_ref[...]).astype(BF16)

    return _rows(body, "outnorm_fwd", S, 256, [oa, ob], [ga, gb], [(2 * WIDTH_P, BF16)])[0]


def _outnorm_bwd(oa, ob, ga, gb, dcat):
    S = oa.shape[0]

    def body(step, oa_ref, ob_ref, dcat_ref, ga_ref, gb_ref, dpa_ref, dob_ref, dga_ref, dgb_ref):
        live = lax.broadcasted_iota(jnp.int32, oa_ref.shape, 1) % LANES < HEAD_DIM
        lane = lax.broadcasted_iota(jnp.int32, (oa_ref.shape[0], LANES), 1)
        packed = oa_ref[...]
        o = jnp.where(live, packed, 0.0)
        do, dgr = _rms_bwd_math(o, ga_ref[...], dcat_ref[:, 0:WIDTH_P], WIDTH)
        _acc_add(step, dga_ref, jnp.sum(dgr, axis=0, keepdims=True))
        prod = do.astype(BF16).astype(F32) * o
        for h in range(N_HEADS):
            blk = slice(h * LANES, (h + 1) * LANES)
            delta = jnp.sum(prod[:, blk], axis=-1, keepdims=True)
            lse = jnp.sum(jnp.where(lane == HEAD_DIM, packed[:, blk], 0.0), axis=-1, keepdims=True)
            out = do[:, blk]
            for k, piece in enumerate(_split3(-delta) + _split3(-lse)):
                out = jnp.where(lane == HEAD_DIM + k, piece, out)
            dpa_ref[:, blk] = out

        ov = ob_ref[...]
        do_b, dgr_b = _rms_bwd_math(ov, gb_ref[...], dcat_ref[:, WIDTH_P:2 * WIDTH_P], WIDTH)
        dob_ref[...] = do_b.astype(BF16)
        _acc_add(step, dgb_ref, jnp.sum(dgr_b, axis=0, keepdims=True))

    return _rows(body, "outnorm_bwd", S, 256, [oa, ob, dcat], [ga, gb],
                 [(WIDTH_P, F32), (WIDTH_P, BF16)], [((1, WIDTH_P), F32), ((1, WIDTH_P), F32)])


def _post_mix(x, y, g_post, g_pre):
    S, W = x.shape

    def body(step, x_ref, y_ref, gp_ref, gq_ref, x1_ref, h_ref):
        yv = y_ref[...]
        r = lax.rsqrt(jnp.mean(yv * yv, axis=-1, keepdims=True) + EPS)
        x1 = x_ref[...] + yv * r * gp_ref[...]
        x1_ref[...] = x1
        r1 = lax.rsqrt(jnp.mean(x1 * x1, axis=-1, keepdims=True) + EPS)
        h_ref[...] = (x1 * r1 * gq_ref[...]).astype(BF16)

    return _rows(body, "post_mix", S, 512, [x, y], [g_post, g_pre], [(W, F32), (W, BF16)])


def _final(x1, y2, g, target):
    S, W = x1.shape
    nsteps = S // 256

    def body(step, x1_ref, y_ref, t_ref, g_ref, dx2_ref, dy_ref, dg_ref, sq_ref, loss_ref):
        yv = y_ref[...]
        gv = g_ref[...]
        r = lax.rsqrt(jnp.mean(yv * yv, axis=-1, keepdims=True) + EPS)
        yn = yv * r
        err = (x1_ref[...] + yn * gv) - t_ref[...]
        dx2 = err * (1.0 / W)
        dx2_ref[...] = dx2
        dyg = dx2 * gv
        dy = r * (dyg - yn * jnp.mean(dyg * yn, axis=-1, keepdims=True))
        dy_ref[...] = dy.astype(BF16)
        _acc_add(step, dg_ref, jnp.sum(dx2 * yn, axis=0, keepdims=True))
        _acc_add(step, sq_ref, jnp.sum(err * err, axis=0, keepdims=True))

        @pl.when(step == nsteps - 1)
        def _():
            tot = jnp.sum(sq_ref[...], axis=-1, keepdims=True) * (0.5 / W)
            loss_ref[...] = jnp.broadcast_to(tot, (1, LANES))

    return _rows(body, "final_loss", S, 256, [x1, y2, target], [g], [(W, F32), (W, BF16)],
                 [((1, W), F32), ((1, W), F32), ((1, LANES), F32)])


_GELU_C = math.sqrt(2.0 / math.pi)
_CONV_CHUNK = 128
_HALO = 8


def _gelu(g):
    t = jnp.tanh(_GELU_C * (g + 0.044715 * (g * g * g)))
    return g * (0.5 * (1.0 + t)), t


def _fill_padded(pad_ref, src_ref, S):
    zeros = jnp.zeros((_HALO, LANES), F32)
    pad_ref[0:_HALO, :] = zeros
    pad_ref[_HALO + S:2 * _HALO + S, :] = zeros
    for r0 in range(0, S, _CONV_CHUNK):
        pad_ref[_HALO + r0:_HALO + r0 + _CONV_CHUNK, :] = src_ref[r0:r0 + _CONV_CHUNK, :].astype(F32)


def _conv_fwd(u0, conv_w, conv_b):
    S, C2 = u0.shape
    nb = D_FF // LANES

    def body(u0g_ref, u0v_ref, wg_ref, wv_ref, bg_ref, bv_ref, ug_ref, uv_ref, a_ref, pg_ref, pv_ref):
        _fill_padded(pg_ref, u0g_ref, S)
        _fill_padded(pv_ref, u0v_ref, S)
        wg, wv = wg_ref[...], wv_ref[...]
        for r0 in range(0, S, _CONV_CHUNK):
            def conv(p_ref, w, b_ref):
                base = _HALO + r0
                return (p_ref[base - 1:base - 1 + _CONV_CHUNK, :] * w[0:1, :]
                        + p_ref[base:base + _CONV_CHUNK, :] * w[1:2, :]
                        + p_ref[base + 1:base + 1 + _CONV_CHUNK, :] * w[2:3, :] + b_ref[...])
            g = conv(pg_ref, wg, bg_ref)
            v = conv(pv_ref, wv, bv_ref)
            rows = slice(r0, r0 + _CONV_CHUNK)
            ug_ref[rows, :] = g
            uv_ref[rows, :] = v
            a_ref[rows, :] = (_gelu(g)[0] * v).astype(BF16)

    col = lambda off: pl.BlockSpec((S, LANES), lambda j: (0, j + off))
    wcol = lambda off: pl.BlockSpec((3, LANES), lambda j: (0, j + off))
    bcol = lambda off: pl.BlockSpec((1, LANES), lambda j: (0, j + off))
    ug, uv, a = pl.pallas_call(
        body, name="conv_gelu_fwd", grid=(nb,),
        in_specs=[col(0), col(nb), wcol(0), wcol(nb), bcol(0), bcol(nb)],
        out_specs=[col(0), col(0), col(0)],
        out_shape=[jax.ShapeDtypeStruct((S, D_FF), F32), jax.ShapeDtypeStruct((S, D_FF), F32),
                   jax.ShapeDtypeStruct((S, D_FF), BF16)],
        scratch_shapes=[pltpu.VMEM((S + 2 * _HALO, LANES), F32), pltpu.VMEM((S + 2 * _HALO, LANES), F32)],
        compiler_params=_cparams(("arbitrary",)),
    )(u0, u0, conv_w, conv_w, conv_b, conv_b)
    return ug, uv, a


def _conv_bwd(u0, ug, uv, da, conv_w):
    S = u0.shape[0]
    nb = D_FF // LANES

    def body(u0_ref, ug_ref, uv_ref, da_ref, w_ref, du0_ref, dw_ref, db_ref, pu_ref, pd_ref):
        is_g = pl.program_id(1) == 0
        _fill_padded(pu_ref, u0_ref, S)
        zeros = jnp.zeros((_HALO, LANES), F32)
        pd_ref[0:_HALO, :] = zeros
        pd_ref[_HALO + S:2 * _HALO + S, :] = zeros
        @pl.when(is_g)
        def _():
            for r0 in range(0, S, _CONV_CHUNK):
                rows = slice(r0, r0 + _CONV_CHUNK)
                g = ug_ref[rows, :]
                t = _gelu(g)[1]
                dgel = 0.5 * (1.0 + t) + (0.5 * g) * (1.0 - t * t) * (_GELU_C * (1.0 + 3.0 * 0.044715 * (g * g)))
                pd_ref[_HALO + r0:_HALO + r0 + _CONV_CHUNK, :] = da_ref[rows, :] * uv_ref[rows, :] * dgel

        @pl.when(jnp.logical_not(is_g))
        def _():
            for r0 in range(0, S, _CONV_CHUNK):
                rows = slice(r0, r0 + _CONV_CHUNK)
                pd_ref[_HALO + r0:_HALO + r0 + _CONV_CHUNK, :] = da_ref[rows, :] * _gelu(ug_ref[rows, :])[0]
        w = w_ref[...]
        acc_b = jnp.zeros((1, LANES), F32)
        acc_w = [jnp.zeros((1, LANES), F32) for _ in range(3)]
        for r0 in range(0, S, _CONV_CHUNK):
            base = _HALO + r0
            du_m = pd_ref[base - 1:base - 1 + _CONV_CHUNK, :]
            du_c = pd_ref[base:base + _CONV_CHUNK, :]
            du_p = pd_ref[base + 1:base + 1 + _CONV_CHUNK, :]
            du0_ref[r0:r0 + _CONV_CHUNK, :] = (du_p * w[0:1, :] + du_c * w[1:2, :] + du_m * w[2:3, :]).astype(BF16)
            acc_b = acc_b + jnp.sum(du_c, axis=0, keepdims=True)
            for k in range(3):
                acc_w[k] = acc_w[k] + jnp.sum(du_c * pu_ref[base + k - 1:base + k - 1 + _CONV_CHUNK, :],
                                              axis=0, keepdims=True)
        db_ref[...] = acc_b
        for k in range(3):
            dw_ref[k:k + 1, :] = acc_w[k]

    own = pl.BlockSpec((S, LANES), lambda j, half: (0, half * nb + j))
    shared = pl.BlockSpec((S, LANES), lambda j, half: (0, j))
    du0, dw, db = pl.pallas_call(
        body, name="conv_gelu_bwd", grid=(nb, 2),
        in_specs=[own, shared, shared, shared, pl.BlockSpec((3, LANES), lambda j, half: (0, half * nb + j))],
        out_specs=[own, pl.BlockSpec((3, LANES), lambda j, half: (0, half * nb + j)),
                   pl.BlockSpec((1, LANES), lambda j, half: (0, half * nb + j))],
        out_shape=[jax.ShapeDtypeStruct((S, 2 * D_FF), BF16), jax.ShapeDtypeStruct((3, 2 * D_FF), F32),
                   jax.ShapeDtypeStruct((1, 2 * D_FF), F32)],
        scratch_shapes=[pltpu.VMEM((S + 2 * _HALO, LANES), F32), pltpu.VMEM((S + 2 * _HALO, LANES), F32)],
        compiler_params=_cparams(("arbitrary", "arbitrary")),
    )(u0, ug, uv, da, conv_w)
    return du0, dw, db


DIL_HALF = 64
DIL_Q_FWD = 128
DIL_Q_BWD = 256
DILATIONS = (1, 4, 16)


def _lanes_hi_to_all(x):
    lane = lax.broadcasted_iota(jnp.int32, x.shape, 1)
    return jnp.where(lane < HEAD_DIM, pltpu.roll(x, HEAD_DIM, 1), x)


_NT = (((1,), (1,)), ((), ()))
_TN = (((0,), (0,)), ((), ()))
HEAD_COLS = 3 * LANES


def _slab_bias(q):
    row = jnp.arange(q, dtype=jnp.int32)[:, None]
    col = jnp.arange(q + 2 * DIL_HALF, dtype=jnp.int32)[None, :]
    slopes = jnp.exp2(-8.0 * jnp.arange(1, N_HEADS + 1, dtype=F32) / N_HEADS)
    out = []
    for r in DILATIONS:
        variants = []
        for shift in (DIL_HALF, 0, 2 * DIL_HALF):
            ad = jnp.abs(col - shift - row)
            variants.append(jnp.where(ad <= DIL_HALF, -slopes[:, None, None] * (ad * r).astype(F32)[None], MASKED))
        out.append(jnp.stack(variants, axis=1))
    return jnp.stack(out, axis=0)


DIL_CHUNK = 512


def _block_geometry(i, nblk, q):
    first = pl.multiple_of(i * q, q)
    slab0 = pl.multiple_of(jnp.clip(i * q - DIL_HALF, 0, (nblk - 1) * q - 2 * DIL_HALF), DIL_HALF)
    variant = jnp.where(i == 0, 1, jnp.where(i == nblk - 1, 2, 0))
    return first, slab0, variant


def _class_rows(c, r, r0, n):
    return pl.ds(c + r0 * r, n, stride=r) if r > 1 else pl.ds(r0, n)


def _dila_fwd(proj_a, bias, exchange=None):
    S = proj_a.shape[0]
    lmax = S // DILATIONS[1]
    DIL_Q, DIL_SLAB = DIL_Q_FWD, DIL_Q_FWD + 2 * DIL_HALF

    def compute(qh_ref, kh_ref, vh_ref, b_ref, o_ref, q_s, k_s, v_s, cm_s):
        lane = lax.broadcasted_iota(jnp.int32, (DIL_Q, LANES), 1)
        lane_s = lax.broadcasted_iota(jnp.int32, (DIL_SLAB, LANES), 1)
        lane_c = lax.broadcasted_iota(jnp.int32, (DIL_CHUNK, LANES), 1)

        def run(g, nblk, load_q, load_k, load_v, store):
            def block(i, carry):
                first, slab0, variant = _block_geometry(i, nblk, DIL_Q)
                qv, ks, vs = load_q(first), load_k(slab0), load_v(slab0)
                s = lax.dot_general(qv, ks, _NT, preferred_element_type=F32) + b_ref[g, 0, variant]
                m = jnp.max(s, axis=-1, keepdims=True)
                acc = jnp.dot(jnp.exp(s - m).astype(BF16), vs, preferred_element_type=F32)
                l = _lanes_hi_to_all(acc)
                store(first, jnp.where(lane < HEAD_DIM, acc / l, m + jnp.log(l)))
                return carry

            lax.fori_loop(0, nblk, block, 0, unroll=min(8, nblk))

        def direct_store(first, val):
            o_ref[pl.ds(first, DIL_Q), :] = val

        run(0, S // DIL_Q,
            lambda f: (qh_ref[pl.ds(f, DIL_Q), :] * DIL_SCALE).astype(BF16),
            lambda s0: kh_ref[pl.ds(s0, DIL_SLAB), :].astype(BF16),
            lambda s0: jnp.where(lane_s < HEAD_DIM, vh_ref[pl.ds(s0, DIL_SLAB), :], 1.0).astype(BF16),
            direct_store)

        def cm_store(first, val):
            cm_s[pl.ds(first, DIL_Q), :] = val

        for g, r in list(enumerate(DILATIONS))[1:]:
            L = S // r
            n = min(L, DIL_CHUNK)
            for c in range(r):
                for r0 in range(0, L, n):
                    src = _class_rows(c, r, r0, n)
                    q_s[r0:r0 + n, :] = (qh_ref[src, :] * DIL_SCALE).astype(BF16)
                    k_s[r0:r0 + n, :] = kh_ref[src, :].astype(BF16)
                    v_s[r0:r0 + n, :] = jnp.where(lane_c[:n] < HEAD_DIM, vh_ref[src, :], 1.0).astype(BF16)
                run(g, L // DIL_Q, lambda f: q_s[pl.ds(f, DIL_Q), :], lambda s0: k_s[pl.ds(s0, DIL_SLAB), :],
                    lambda s0: v_s[pl.ds(s0, DIL_SLAB), :], cm_store)
                for r0 in range(0, L, n):
                    dst = _class_rows(c, r, r0, n)
                    a, b = cm_s[r0:r0 + n, :], o_ref[dst, :]
                    la, lb = _lanes_hi_to_all(a), _lanes_hi_to_all(b)
                    m = jnp.maximum(la, lb)
                    wa, wb = jnp.exp(la - m), jnp.exp(lb - m)
                    tot = wa + wb
                    o_ref[dst, :] = jnp.where(lane_c[:n] < HEAD_DIM, (wa * a + wb * b) / tot, m + jnp.log(tot))

    ex = exchange
    n = ex.n if ex else 0

    def body(*refs):
        ins, ex_in, o_ref, ex_out = refs[:4], refs[4:4 + n], refs[4 + n], refs[5 + n:5 + 2 * n]
        scratch, sems = refs[5 + 2 * n:9 + 2 * n], refs[9 + 2 * n:]
        if ex:
            pl.when(pl.program_id(0) == 0)(lambda: ex.start(ex_in, ex_out, sems))
        compute(*ins, o_ref, *scratch)
        if ex:
            pl.when(pl.program_id(0) == N_HEADS - 1)(lambda: ex.finish(ex_in, ex_out, sems))

    out = pl.pallas_call(
        body, name="dil_fwd", grid=(N_HEADS,),
        in_specs=[pl.BlockSpec((S, LANES), lambda h: (0, 3 * h)), pl.BlockSpec((S, LANES), lambda h: (0, 3 * h + 1)),
                  pl.BlockSpec((S, LANES), lambda h: (0, 3 * h + 2)),
                  pl.BlockSpec((len(DILATIONS), 1, 3, DIL_Q, DIL_SLAB), lambda h: (0, h, 0, 0, 0))] + [ANY] * n,
        out_specs=[pl.BlockSpec((S, LANES), lambda h: (0, h))] + [ANY] * n,
        out_shape=[jax.ShapeDtypeStruct((S, WIDTH_P), F32)] + (ex.out_shapes() if ex else []),
        scratch_shapes=[pltpu.VMEM((lmax, LANES), BF16), pltpu.VMEM((lmax, LANES), BF16),
                        pltpu.VMEM((lmax, LANES), BF16), pltpu.VMEM((lmax, LANES), F32)] + (ex.scratch() if ex else []),
        compiler_params=_cparams(("arbitrary",)),
    )(proj_a, proj_a, proj_a, bias, *(ex.arrays if ex else []))
    return out[0], list(out[1:])


N_SPLIT = 3


def _split3(x):
    hi = x.astype(BF16).astype(F32)
    mid = (x - hi).astype(BF16).astype(F32)
    lo = (x - hi - mid).astype(BF16).astype(F32)
    return hi, mid, lo


def _dila_bwd(proj_a, dopack, bias):
    S = proj_a.shape[0]
    lmax = S // DILATIONS[1]
    DIL_Q, DIL_SLAB = DIL_Q_BWD, DIL_Q_BWD + 2 * DIL_HALF

    def body(qh_ref, kh_ref, vh_ref, d_ref, b_ref, out_ref, dq_ref, dk_ref, dv_ref, q_s, k_s, v_s, do_s,
             dq_c, dk_c, dv_c):
        def scalar_lanes(shape):
            lane = lax.broadcasted_iota(jnp.int32, shape, 1)
            return lane, (lane >= HEAD_DIM) & (lane < HEAD_DIM + N_SPLIT)

        def q_side(q, x):
            lane, ones = scalar_lanes(x.shape)
            lse_parts = pltpu.roll(x, LANES - N_SPLIT, 1)
            qv = jnp.where(lane < HEAD_DIM, q * DIL_SCALE, jnp.where(ones, lse_parts, 0.0)).astype(BF16)
            return qv, jnp.where(lane < HEAD_DIM + N_SPLIT, x, 0.0).astype(BF16)

        def kv_side(k, v):
            _, ones = scalar_lanes(k.shape)
            return jnp.where(ones, 1.0, k).astype(BF16), jnp.where(ones, 1.0, v).astype(BF16)

        def run(g, nblk, load_q, load_kv, dq_o, dk_o, dv_o):
            def block(i, carry):
                first, slab0, variant = _block_geometry(i, nblk, DIL_Q)
                rows, slab = pl.ds(first, DIL_Q), pl.ds(slab0, DIL_SLAB)
                (qv, dov), (ks, vs) = load_q(rows), load_kv(slab)
                p = jnp.exp(lax.dot_general(qv, ks, _NT, preferred_element_type=F32) + b_ref[g, 0, variant])
                ds = (p * lax.dot_general(dov, vs, _NT, preferred_element_type=F32)).astype(BF16)
                dq_o[rows, :] = jnp.dot(ds, ks, preferred_element_type=F32) * DIL_SCALE
                dk_o[slab, :] += lax.dot_general(ds, qv, _TN, preferred_element_type=F32)
                dv_o[slab, :] += lax.dot_general(p.astype(BF16), dov, _TN, preferred_element_type=F32)
                return carry

            lax.fori_loop(0, nblk, block, 0, unroll=min(8, nblk))

        dk_ref[...] = jnp.zeros_like(dk_ref)
        dv_ref[...] = jnp.zeros_like(dv_ref)
        run(0, S // DIL_Q,
            lambda rows: q_side(qh_ref[rows, :], d_ref[rows, :]),
            lambda slab: kv_side(kh_ref[slab, :], vh_ref[slab, :]),
            dq_ref, dk_ref, dv_ref)

        for g, r in list(enumerate(DILATIONS))[1:]:
            L = S // r
            n = min(L, DIL_CHUNK)
            for c in range(r):
                for r0 in range(0, L, n):
                    src = _class_rows(c, r, r0, n)
                    q_s[r0:r0 + n, :], do_s[r0:r0 + n, :] = q_side(qh_ref[src, :], d_ref[src, :])
                    k_s[r0:r0 + n, :], v_s[r0:r0 + n, :] = kv_side(kh_ref[src, :], vh_ref[src, :])
                    dk_c[r0:r0 + n, :] = jnp.zeros((n, LANES), F32)
                    dv_c[r0:r0 + n, :] = jnp.zeros((n, LANES), F32)
                run(g, L // DIL_Q, lambda rows: (q_s[rows, :], do_s[rows, :]),
                    lambda slab: (k_s[slab, :], v_s[slab, :]), dq_c, dk_c, dv_c)
                for r0 in range(0, L, n):
                    dst = _class_rows(c, r, r0, n)
                    for acc, cls in ((dq_ref, dq_c), (dk_ref, dk_c), (dv_ref, dv_c)):
                        acc[dst, :] += cls[r0:r0 + n, :]

        for r0 in range(0, S, DIL_CHUNK):
            for part, ref in enumerate((dq_ref, dk_ref, dv_ref)):
                out_ref[r0:r0 + DIL_CHUNK, part * LANES:(part + 1) * LANES] = ref[r0:r0 + DIL_CHUNK, :].astype(BF16)

    bf = lambda rows: pltpu.VMEM((rows, LANES), BF16)
    f32 = lambda rows: pltpu.VMEM((rows, LANES), F32)
    return pl.pallas_call(
        body, name="dil_bwd", grid=(N_HEADS,),
        in_specs=[pl.BlockSpec((S, LANES), lambda h: (0, 3 * h), pipeline_mode=pl.Buffered(1)),
                  pl.BlockSpec((S, LANES), lambda h: (0, 3 * h + 1), pipeline_mode=pl.Buffered(1)),
                  pl.BlockSpec((S, LANES), lambda h: (0, 3 * h + 2), pipeline_mode=pl.Buffered(1)),
                  pl.BlockSpec((S, LANES), lambda h: (0, h), pipeline_mode=pl.Buffered(1)),
                  pl.BlockSpec((len(DILATIONS), 1, 3, DIL_Q, DIL_SLAB), lambda h: (0, h, 0, 0, 0))],
        out_specs=pl.BlockSpec((S, HEAD_COLS), lambda h: (0, h)),
        out_shape=jax.ShapeDtypeStruct((S, N_HEADS * HEAD_COLS), BF16),
        scratch_shapes=[f32(S), f32(S), f32(S), bf(lmax), bf(lmax), bf(lmax), bf(lmax),
                        f32(lmax), f32(lmax), f32(lmax)],
        compiler_params=_cparams(("arbitrary",)),
    )(proj_a, proj_a, proj_a, dopack, bias)


def _mla_fwd(q, k, vt):
    S = q.shape[0]
    tq, tk = MLA_TQ, MLA_TK
    nq, nk = S // tq, S // tk

    def body(q_ref, k_ref, vt_ref, o_ref, lse_ref, acc_ref):
        qv = q_ref[...]
        acc_ref[...] = jnp.zeros_like(acc_ref)

        def pair(j, m):
            sts = []
            for c in (2 * j, 2 * j + 1):
                kc = k_ref[pl.ds(pl.multiple_of(c * tk, tk), tk), :]
                sts.append(lax.dot_general(kc, qv, _NT, preferred_element_type=F32))
            m_new = jnp.maximum(m, jnp.maximum(jnp.max(sts[0], axis=0, keepdims=True),
                                               jnp.max(sts[1], axis=0, keepdims=True)))
            upd = [jnp.dot(vt_ref[0, c, 0:V_ROWS, :], jnp.exp(st - m_new).astype(BF16), preferred_element_type=F32)
                   for c, st in zip((2 * j, 2 * j + 1), sts)]
            acc_ref[...] = jnp.exp(m - m_new) * acc_ref[...] + (upd[0] + upd[1])
            return m_new

        m = lax.fori_loop(0, nk // 2, pair, jnp.full((1, tq), M_INIT, F32))
        acc = acc_ref[...]
        l = acc[HEAD_DIM:HEAD_DIM + 1, :]
        ot = jnp.concatenate([acc[0:HEAD_DIM, :] / l, jnp.zeros((LANES - HEAD_DIM, tq), F32)], axis=0)
        o_ref[...] = ot.T
        lse_ref[0] = m + jnp.log(l)

    return pl.pallas_call(
        body, name="mla_fwd", grid=(N_HEADS, nq),
        in_specs=[pl.BlockSpec((tq, LANES), lambda h, i: (i, h)),
                  pl.BlockSpec((S, LANES), lambda h, i: (0, h)),
                  pl.BlockSpec((1, nk, LANES, tk), lambda h, i: (h, 0, 0, 0))],
        out_specs=[pl.BlockSpec((tq, LANES), lambda h, i: (i, h)),
                   pl.BlockSpec((1, 1, tq), lambda h, i: (h, 0, i))],
        out_shape=[jax.ShapeDtypeStruct((S, WIDTH_P), F32), jax.ShapeDtypeStruct((N_HEADS, 1, S), F32)],
        scratch_shapes=[pltpu.VMEM((V_ROWS, tq), F32)],
        compiler_params=_cparams(("parallel", "parallel")),
    )(q, k, vt)


def _mla_bwd(q, k, v, kt, do, o, lse, exchange=None):
    S = q.shape[0]
    tq, tk = MLA_TQ, MLA_TK
    nq, nk = S // tq, S // tk

    def compute(q_ref, do_ref, o_ref, lse_ref, k_ref, v_ref, kt_ref, dq_ref, dk_ref, dv_ref, dqt_ref):
        @pl.when(pl.program_id(1) == 0)
        def _():
            dk_ref[...] = jnp.zeros_like(dk_ref)
            dv_ref[...] = jnp.zeros_like(dv_ref)

        qv, dov = q_ref[...], do_ref[...]
        delta = jnp.sum((dov.astype(F32) * o_ref[...]).T, axis=0, keepdims=True)
        lse = lse_ref[0]
        dqt_ref[...] = jnp.zeros_like(dqt_ref)

        def chunk(c, carry):
            rows = pl.ds(pl.multiple_of(c * tk, tk), tk)
            kc, vc = k_ref[rows, :], v_ref[rows, :]
            pt = jnp.exp(lax.dot_general(kc, qv, _NT, preferred_element_type=F32) - lse)
            dv_ref[rows, :] += jnp.dot(pt.astype(BF16), dov, preferred_element_type=F32)
            dpt = lax.dot_general(vc, dov, _NT, preferred_element_type=F32)
            dst = (pt * (dpt - delta)).astype(BF16)
            dk_ref[rows, :] += jnp.dot(dst, qv, preferred_element_type=F32)
            dqt_ref[...] += jnp.dot(kt_ref[0, c], dst, preferred_element_type=F32)
            return carry

        lax.fori_loop(0, nk, chunk, 0, unroll=2)
        dq_ref[...] = (dqt_ref[...] * MLA_SCALE).T

    ex = exchange
    n = ex.n if ex else 0

    def body(*refs):
        ins, ex_in, outs, ex_out = refs[:7], refs[7:7 + n], refs[7 + n:10 + n], refs[10 + n:10 + 2 * n]
        dqt_ref, sems = refs[10 + 2 * n], refs[11 + 2 * n:]
        first = (pl.program_id(0) == 0) & (pl.program_id(1) == 0)
        last = (pl.program_id(0) == N_HEADS - 1) & (pl.program_id(1) == nq - 1)
        if ex:
            pl.when(first)(lambda: ex.start(ex_in, ex_out, sems))
        compute(*ins, *outs, dqt_ref)
        if ex:
            pl.when(last)(lambda: ex.finish(ex_in, ex_out, sems))

    qspec = pl.BlockSpec((tq, LANES), lambda h, i: (i, h))
    kspec = pl.BlockSpec((S, LANES), lambda h, i: (0, h))
    out = jax.ShapeDtypeStruct((S, WIDTH_P), F32)
    res = pl.pallas_call(
        body, name="mla_bwd", grid=(N_HEADS, nq),
        in_specs=[qspec, qspec, qspec, pl.BlockSpec((1, 1, tq), lambda h, i: (h, 0, i)), kspec, kspec,
                  pl.BlockSpec((1, nk, LANES, tk), lambda h, i: (h, 0, 0, 0))] + [ANY] * n,
        out_specs=[qspec, kspec, kspec] + [ANY] * n, out_shape=[out, out, out] + (ex.out_shapes() if ex else []),
        scratch_shapes=[pltpu.VMEM((LANES, tq), F32)] + (ex.scratch() if ex else []),
        compiler_params=_cparams(("arbitrary", "arbitrary")),
    )(q, do, o, lse, k, v, kt, *(ex.arrays if ex else []))
    return res[0], res[1], res[2], list(res[3:])


def _mla_bwd_prep(dq, dk, dv, tabs):
    S = dq.shape[0]

    def body(step, dq_ref, dk_ref, dv_ref, c_ref, sa_ref, sb_ref, dqp_ref, dkv_ref, dkr_ref):
        c, sa, sb = c_ref[...], sa_ref[...], sb_ref[...]
        dksum = jnp.zeros((dq_ref.shape[0], LANES), F32)
        for h in range(N_HEADS):
            blk = slice(h * LANES, (h + 1) * LANES)
            dqp_ref[:, blk] = _rope_transpose(dq_ref[:, blk], c, sa, sb).astype(BF16)
            dksum = dksum + dk_ref[:, blk]
        dkv_ref[:, 0:WIDTH_P] = dk_ref[...].astype(BF16)
        dkv_ref[:, WIDTH_P:2 * WIDTH_P] = dv_ref[...].astype(BF16)
        lane = lax.broadcasted_iota(jnp.int32, dksum.shape, 1)
        live = (lane >= HEAD_DIM) & (lane < HEAD_DIM + QK_ROPE)
        dkr_ref[...] = jnp.where(live, _rope_transpose(dksum, c, sa, sb), 0.0)

    return _rows(body, "mla_bwd_prep", S, 256, [dq, dk, dv, *tabs], [],
                 [(WIDTH_P, BF16), (2 * WIDTH_P, BF16), (LANES, F32)])


def _mla_norm_bwd(proj_b, dcq_n, dckv_n, dkr, g_q, g_kv):
    S = proj_b.shape[0]

    def body(step, p_ref, dcq_ref, dckv_ref, dkr_ref, gq_ref, gkv_ref, dp_ref, dgq_ref, dgkv_ref):
        dcq, dgq = _rms_bwd_math(p_ref[:, 0:Q_LORA], gq_ref[...], dcq_ref[...], Q_LORA)
        dckv, dgkv = _rms_bwd_math(p_ref[:, Q_LORA:Q_LORA + KV_LORA], gkv_ref[...], dckv_ref[...], KV_LORA)
        dp_ref[:, 0:Q_LORA] = dcq.astype(BF16)
        dp_ref[:, Q_LORA:Q_LORA + KV_LORA] = dckv.astype(BF16)
        dp_ref[:, Q_LORA + KV_LORA:TAIL_P] = dkr_ref[...].astype(BF16)
        _acc_add(step, dgq_ref, jnp.sum(dgq, axis=0, keepdims=True))
        _acc_add(step, dgkv_ref, jnp.sum(dgkv, axis=0, keepdims=True))

    return _rows(body, "mla_norm_bwd", S, 512, [proj_b, dcq_n, dckv_n, dkr], [g_q, g_kv], [(TAIL_P, BF16)],
                 [((1, Q_LORA), F32), ((1, KV_LORA), F32)])


def _pad_cols(w, d):
    lead = w.shape[:-1]
    w = w.reshape(lead + (N_HEADS, d))
    w = jnp.pad(w, [(0, 0)] * len(lead) + [(0, 0), (0, LANES - d)])
    return w.reshape(lead + (N_HEADS * LANES,))


def _unpad_cols(w, d):
    lead = w.shape[:-1]
    return w.reshape(lead + (N_HEADS, LANES))[..., :d].reshape(lead + (N_HEADS * d,))


def _pad_w_in(w_in):
    zeros = lambda n: jnp.zeros((D_MODEL, n), w_in.dtype)
    p = {}
    parts = [_pad_cols(w_in[:, i * WIDTH:(i + 1) * WIDTH], HEAD_DIM).reshape(D_MODEL, N_HEADS, 1, LANES)
             for i in range(3)]
    p['w_in_a'] = jnp.concatenate(parts, axis=2).reshape(D_MODEL, N_HEADS * HEAD_COLS)
    p['w_in_b'] = jnp.concatenate([w_in[:, 3 * WIDTH:3 * WIDTH + Q_LORA + KV_LORA], zeros(HEAD_DIM),
                                   w_in[:, D_IN - QK_ROPE:], zeros(LANES - HEAD_DIM - QK_ROPE)], axis=1)
    return p


def _pad_weights(w):
    p = {}
    p['w_uq'] = _pad_cols(w['w_uq'], HEAD_DIM + QK_ROPE)
    kv = w['w_ukv'].reshape(KV_LORA, N_HEADS, 2 * HEAD_DIM)
    p['w_ukv'] = jnp.concatenate([_pad_cols(kv[:, :, :HEAD_DIM].reshape(KV_LORA, WIDTH), HEAD_DIM),
                                  _pad_cols(kv[:, :, HEAD_DIM:].reshape(KV_LORA, WIDTH), HEAD_DIM)], axis=1)
    p['w_o'] = jnp.concatenate(
        [_pad_cols(w['w_o'][i * WIDTH:(i + 1) * WIDTH].T, HEAD_DIM).T for i in range(2)], axis=0)
    p['g_a'] = _pad_cols(w['out_norm_a'], HEAD_DIM)
    p['g_b'] = _pad_cols(w['out_norm_b'], HEAD_DIM)
    return p


def _unpad_grads(d):
    g = {}
    dwa = d['w_in_a'].reshape(D_MODEL, N_HEADS, 3, LANES)
    tail = d['w_in_b']
    g['w_in'] = jnp.concatenate(
        [dwa[:, :, i, :HEAD_DIM].reshape(D_MODEL, WIDTH) for i in range(3)]
        + [tail[:, :Q_LORA + KV_LORA], tail[:, Q_LORA + KV_LORA + HEAD_DIM:Q_LORA + KV_LORA + HEAD_DIM + QK_ROPE]],
        axis=1)
    g['w_uq'] = _unpad_cols(d['w_uq'], HEAD_DIM + QK_ROPE)
    dk = _unpad_cols(d['w_ukv'][:, :WIDTH_P], HEAD_DIM).reshape(KV_LORA, N_HEADS, HEAD_DIM)
    dv = _unpad_cols(d['w_ukv'][:, WIDTH_P:], HEAD_DIM).reshape(KV_LORA, N_HEADS, HEAD_DIM)
    g['w_ukv'] = jnp.concatenate([dk, dv], axis=2).reshape(KV_LORA, 2 * WIDTH)
    g['w_o'] = jnp.concatenate(
        [_unpad_cols(d['w_o'][i * WIDTH_P:(i + 1) * WIDTH_P].T, HEAD_DIM).T for i in range(2)], axis=0)
    g['out_norm_a'] = _unpad_cols(d['g_a'], HEAD_DIM)
    g['out_norm_b'] = _unpad_cols(d['g_b'], HEAD_DIM)
    return g


LATE = ['w_uq', 'w_ukv', 'w_o', 'w_up', 'w_down']
EARLY_GRADS = ['w_up', 'w_down']


def _assemble_late(gathered):
    g = dict(zip(LATE, gathered))
    return {'w_uq': jnp.concatenate([g['w_uq'][i] for i in range(N_CHIPS)], axis=1),
            'w_ukv': jnp.concatenate([g['w_ukv'][i] for i in range(N_CHIPS)], axis=1),
            'w_o': g['w_o'].reshape(D_MODEL, D_MODEL),
            'w_down': g['w_down'].reshape(D_FF, D_MODEL),
            'w_up': g['w_up']}


def _local_step(x, target, w, late_shards=None):
    S = x.shape[0]
    w = dict(w)
    p = _pad_w_in(w['w_in'])
    tabs = _rope_tables(S)
    bias = _slab_bias(DIL_Q_FWD)

    h1 = _rms_fwd(x, w['norm_mix_pre'], "rms_mix_pre")
    proj_a = _mm(h1, p['w_in_a'], 'nn', F32, "mm_in_a")
    proj_b = _mm(h1, p['w_in_b'], 'nn', F32, "mm_in_b")
    gather = _Exchange([late_shards[n] for n in LATE], False) if late_shards else None
    oa, gathered = _dila_fwd(proj_a, bias, gather)
    if late_shards:
        w.update(_assemble_late(gathered))
    p.update(_pad_weights(w))
    cq_n, ckv_n, kr = _mla_prep(proj_b, w['q_lat_norm'], w['kv_lat_norm'], tabs)
    q_lin = _mm(cq_n, p['w_uq'], 'nn', F32, "mm_uq")
    kv_lin = _mm(ckv_n, p['w_ukv'], 'nn', F32, "mm_ukv")
    qb, kb, vb, kt, vt = _mla_qkv(q_lin, kv_lin, kr, tabs)
    ob, lse_b = _mla_fwd(qb, kb, vt)
    cat = _outnorm_fwd(oa, ob, p['g_a'], p['g_b'])
    y = _mm(cat, p['w_o'], 'nn', F32, "mm_o")
    x1, h2 = _post_mix(x, y, w['norm_mix_post'], w['norm_ffn_pre'])
    u0 = _mm(h2, w['w_up'], 'nn', F32, "mm_up", sharded=True)
    ug, uv, a = _conv_fwd(u0, w['conv_w'], w['conv_b'])
    y2 = _mm(a, w['w_down'], 'nn', F32, "mm_down")
    dx2, dy2, dg_ffn_post, _, loss = _final(x1, y2, w['norm_ffn_post'], target)

    g = {'norm_ffn_post': dg_ffn_post}
    da = _mm(dy2, w['w_down'], 'nt', F32, "mm_down_dx")
    g['w_down'] = _mm(a, dy2, 'tn', BF16, "mm_down_dw")
    du0, g['conv_w'], g['conv_b'] = _conv_bwd(u0, ug, uv, da, w['conv_w'])
    dh2 = _mm(du0, w['w_up'], 'nt', F32, "mm_up_dx", sharded=True)
    g['w_up'] = _mm(h2, du0, 'tn', BF16, "mm_up_dw", sharded=True)
    dx1, g['norm_ffn_pre'] = _rms_bwd(x1, w['norm_ffn_pre'], [dh2], dx2, F32, "rms_ffn_pre_bwd")
    dy, g['norm_mix_post'] = _rms_bwd(y, w['norm_mix_post'], [dx1], None, BF16, "rms_mix_post_bwd")
    dcat = _mm(dy, p['w_o'], 'nt', F32, "mm_o_dx")
    dpad = {'w_o': _mm(cat, dy, 'tn', F32, "mm_o_dw")}
    dopack_a, do_b, dpad['g_a'], dpad['g_b'] = _outnorm_bwd(oa, ob, p['g_a'], p['g_b'], dcat)

    scatter = None
    if late_shards:
        scatter = _Exchange([g['w_up'], g['w_down'].reshape(N_CHIPS, D_FF // N_CHIPS, D_MODEL)], True)
    dq_b, dk_b, dv_b, received = _mla_bwd(qb, kb, vb, kt, do_b, ob, lse_b, scatter)
    if late_shards:
        g.update(zip(EARLY_GRADS, received))
    dq_pre, dkv, dkr = _mla_bwd_prep(dq_b, dk_b, dv_b, tabs)
    dcq_n = _mm(dq_pre, p['w_uq'], 'nt', F32, "mm_uq_dx")
    dpad['w_uq'] = _mm(cq_n, dq_pre, 'tn', F32, "mm_uq_dw")
    dckv_n = _mm(dkv, p['w_ukv'], 'nt', F32, "mm_ukv_dx")
    dpad['w_ukv'] = _mm(ckv_n, dkv, 'tn', F32, "mm_ukv_dw")
    dproj_b, g['q_lat_norm'], g['kv_lat_norm'] = _mla_norm_bwd(proj_b, dcq_n, dckv_n, dkr,
                                                               w['q_lat_norm'], w['kv_lat_norm'])

    dproj_a = _dila_bwd(proj_a, dopack_a, _slab_bias(DIL_Q_BWD))
    dh1 = _mm(dproj_b, p['w_in_b'], 'nt', F32, "mm_in_b_dx")
    dh1 = _mm(dproj_a, p['w_in_a'], 'nt', F32, "mm_in_a_dx", add=dh1)
    dpad['w_in_a'] = _mm(h1, dproj_a, 'tn', F32, "mm_in_a_dw")
    dpad['w_in_b'] = _mm(h1, dproj_b, 'tn', F32, "mm_in_b_dw")
    grad_x, g['norm_mix_pre'] = _rms_bwd(x, w['norm_mix_pre'], [dh1], dx1, F32, "rms_mix_pre_bwd")
    g.update(_unpad_grads(dpad))
    return loss, grad_x, g


MESH = pl.DeviceIdType.MESH
ANY = pl.BlockSpec(memory_space=pl.ANY)


def _place():
    x, y, c = lax.axis_index("x"), lax.axis_index("y"), lax.axis_index("c")
    chips = [(1 - x, y), (x, 1 - y), (1 - x, 1 - y)]
    return x, y, c, chips


class _Exchange:
    def __init__(self, arrays, scatter):
        self.arrays, self.scatter, self.n = list(arrays), scatter, len(arrays)

    def out_shapes(self):
        return [jax.ShapeDtypeStruct(a.shape if self.scatter else (N_CHIPS,) + a.shape, a.dtype) for a in self.arrays]

    def scratch(self):
        return [pltpu.SemaphoreType.DMA((3 * self.n,)), pltpu.SemaphoreType.DMA((3 * self.n,)),
                pltpu.SemaphoreType.DMA((self.n,))]

    def _copies(self, in_refs, out_refs, sems, arrivals):
        send_sems, recv_sems, local_sems = sems
        x, y, c, chips = _place()
        me = 2 * x + y
        src = lambda b, chip: in_refs[b].at[chip] if self.scatter else in_refs[b]
        local = [pltpu.make_async_copy(src(b, me), out_refs[b].at[me], local_sems.at[b]) for b in range(self.n)]
        sends, recvs = [], []
        for j, (px, py) in enumerate(chips):
            for b in range(self.n):
                k = j * self.n + b
                common = dict(send_sem=send_sems.at[k], recv_sem=recv_sems.at[k], device_id=(px, py, c),
                              device_id_type=MESH)
                sends.append(pltpu.make_async_remote_copy(src_ref=src(b, 2 * px + py), dst_ref=out_refs[b].at[me],
                                                          **common))
                if arrivals:
                    recvs.append(pltpu.make_async_remote_copy(src_ref=src(b, me), dst_ref=out_refs[b].at[2 * px + py],
                                                              **common))
        return local, sends, recvs

    def start(self, in_refs, out_refs, sems):
        local, sends, _ = self._copies(in_refs, out_refs, sems, False)
        for cp in local + sends:
            cp.start()

    def finish(self, in_refs, out_refs, sems):
        local, sends, recvs = self._copies(in_refs, out_refs, sems, True)
        for cp in recvs:
            cp.wait_recv()
        for cp in sends:
            cp.wait_send()
        for cp in local:
            cp.wait()


def _exchange_call(arrays, scatter, name):
    ex = _Exchange(arrays, scatter)
    n = ex.n

    def body(*refs):
        in_refs, out_refs, sems = refs[:n], refs[n:2 * n], refs[2 * n:]
        ex.start(in_refs, out_refs, sems)
        ex.finish(in_refs, out_refs, sems)

    return pl.pallas_call(body, name=name, in_specs=[ANY] * n, out_specs=[ANY] * n, out_shape=ex.out_shapes(),
                          scratch_shapes=ex.scratch())(*ex.arrays)


def _all_gather(bufs, name="gather_weights"):
    return _exchange_call(bufs, False, name)


def _scatter_grads(slots, name="scatter_grads"):
    return _exchange_call(slots, True, name)


ELEMENTWISE_BLOCK = 256 * 1024


def _row_tile(rows, cols):
    best = None
    for t in range(16, min(rows, max(16, ELEMENTWISE_BLOCK // cols)) + 1, 16):
        if rows % t == 0:
            best = t
    return best if best is not None else rows


def _sum_slots(recv, name):
    _, R, C = recv.shape
    tr = _row_tile(R, C)

    def body(r_ref, o_ref):
        f = lambda i: r_ref[i].astype(F32)
        o_ref[...] = ((f(0) + f(1)) + f(2)) + f(3)

    return pl.pallas_call(
        body, name="sum_" + name, grid=(R // tr,),
        in_specs=[pl.BlockSpec((N_CHIPS, tr, C), lambda i: (0, i, 0))],
        out_specs=pl.BlockSpec((tr, C), lambda i: (i, 0)),
        out_shape=jax.ShapeDtypeStruct((R, C), F32),
        compiler_params=_cparams(("parallel",)),
    )(recv)


def _swap_sibling(parts):
    n = len(parts)

    def body(*refs):
        p_refs, o_refs, send_sems, recv_sems = refs[:n], refs[n:2 * n], refs[2 * n], refs[2 * n + 1]
        x, y, c, _ = _place()
        cps = [pltpu.make_async_remote_copy(src_ref=p_refs[b], dst_ref=o_refs[b], send_sem=send_sems.at[b],
                                            recv_sem=recv_sems.at[b], device_id=(x, y, 1 - c), device_id_type=MESH)
               for b in range(n)]
        for cp in cps:
            cp.start()
        for cp in cps:
            cp.wait()

    return pl.pallas_call(
        body, name="swap_sibling", in_specs=[ANY] * n, out_specs=[ANY] * n,
        out_shape=[jax.ShapeDtypeStruct(p.shape, p.dtype) for p in parts],
        scratch_shapes=[pltpu.SemaphoreType.DMA((n,)), pltpu.SemaphoreType.DMA((n,))],
    )(*parts)


def _adamw(g0, g1, w, m, v, name, offset=0):
    R, C = w.shape
    tr = _row_tile(R, C)
    packed = g0.shape != w.shape
    bc1 = 1.0 - ADAM_B1 ** ADAM_STEP
    bc2 = 1.0 - ADAM_B2 ** ADAM_STEP

    def body(g0_ref, g1_ref, w_ref, m_ref, v_ref, g_ref, d_ref, nm_ref, nv_ref):
        if packed:
            g = g0_ref[:, offset:offset + C] + g1_ref[:, offset:offset + C]
        else:
            g = g0_ref[...] + g1_ref[...]
        g_ref[...] = g
        nm = ADAM_B1 * m_ref[...] + (1.0 - ADAM_B1) * g
        nv = ADAM_B2 * v_ref[...] + (1.0 - ADAM_B2) * (g * g)
        nm_ref[...] = nm
        nv_ref[...] = nv
        d_ref[...] = -ADAM_LR * ((nm / bc1) / (jnp.sqrt(nv / bc2) + ADAM_EPS) + ADAM_WD * w_ref[...])

    spec = pl.BlockSpec((tr, C), lambda i: (i, 0))
    gspec = pl.BlockSpec(g0.shape, lambda i: (0, 0)) if packed else spec
    out = jax.ShapeDtypeStruct((R, C), F32)
    return pl.pallas_call(
        body, name="adamw_" + name, grid=(R // tr,), in_specs=[gspec, gspec, spec, spec, spec],
        out_specs=[spec] * 4, out_shape=[out] * 4, compiler_params=_cparams(("parallel",)),
    )(g0, g1, w, m, v)


def kernel(x, norm_mix_pre, w_in, q_lat_norm, w_uq, kv_lat_norm, w_ukv, out_norm_a, out_norm_b, w_o, norm_mix_post, norm_ffn_pre, w_up, conv_w, conv_b, w_down, norm_ffn_post, loss_target, m_norm_mix_pre, m_w_in, m_q_lat_norm, m_w_uq, m_kv_lat_norm, m_w_ukv, m_out_norm_a, m_out_norm_b, m_w_o, m_norm_mix_post, m_norm_ffn_pre, m_w_up, m_conv_w, m_conv_b, m_w_down, m_norm_ffn_post, v_norm_mix_pre, v_w_in, v_q_lat_norm, v_w_uq, v_kv_lat_norm, v_w_ukv, v_out_norm_a, v_out_norm_b, v_w_o, v_norm_mix_post, v_norm_ffn_pre, v_w_up, v_conv_w, v_conv_b, v_w_down, v_norm_ffn_post):
    args = dict(locals())
    strip = lambda a: a[0] if a.ndim == 3 else a
    wl = {n: strip(args[n]) for n in WEIGHTS}
    ml = {n: strip(args['m_' + n]) for n in WEIGHTS}
    vl = {n: strip(args['v_' + n]) for n in WEIGHTS}

    gathered = _all_gather([wl['w_in'].astype(BF16), wl['conv_w']])
    full = {n: wl[n] for n in SMALL}
    for n, a in zip(('w_in', 'conv_w'), gathered):
        full[n] = jnp.concatenate([a[i] for i in range(N_CHIPS)], axis=1)

    loss_b, grad_x, g = _local_step(x[0], loss_target[0], full, {n: wl[n].astype(BF16) for n in LATE})

    sharded = [n for n in WEIGHTS if SHARD_AXIS[n] is not None]
    late = [n for n in sharded if n not in EARLY_GRADS]

    def slots_of(n):
        a = g[n].astype(BF16)
        if SHARD_AXIS[n] == 0:
            return a.reshape(N_CHIPS, a.shape[0] // N_CHIPS, a.shape[1])
        cols = a.shape[1] // N_CHIPS
        return jnp.stack([a[:, i * cols:(i + 1) * cols] for i in range(N_CHIPS)])

    small_pack = jnp.concatenate([g[n] for n in SMALL], axis=1)
    slots = [slots_of(n) for n in late] + [jnp.broadcast_to(small_pack[None], (N_CHIPS,) + small_pack.shape)]
    recv = dict(zip(late + ['small'], _scatter_grads(slots)))
    recv.update({n: g[n] for n in EARLY_GRADS})
    parts = [_sum_slots(recv[n], n) for n in sharded + ['small']]
    others = _swap_sibling(parts)

    outs = {}

    def record(n, results):
        for tag, a in zip(('grad', 'delta', 'new_m', 'new_v'), results):
            outs[tag + '_' + n] = a.reshape(args[n].shape)

    for n, p0, p1 in zip(sharded, parts, others):
        record(n, _adamw(p0, p1, wl[n], ml[n], vl[n], n))
    offset = 0
    for n in SMALL:
        record(n, _adamw(parts[-1], others[-1], wl[n], ml[n], vl[n], n, offset=offset))
        offset += wl[n].shape[1]

    loss = lax.psum(loss_b[0, 0], ("x", "y", "c"))
    return (loss, grad_x[None], *[outs['grad_' + n] for n in WEIGHTS], *[outs['delta_' + n] for n in WEIGHTS],
            *[outs['new_m_' + n] for n in WEIGHTS], *[outs['new_v_' + n] for n in WEIGHTS])
```

```python
import math

import jax
import jax.numpy as jnp
from jax import lax
from jax.experimental import pallas as pl
from jax.experimental.pallas import tpu as pltpu

F32 = jnp.float32
BF16 = jnp.bfloat16

LANES = 128
D_MODEL = 1024
N_HEADS = 8
HEAD_DIM = 64
QK_ROPE = 32
Q_LORA = 384
KV_LORA = 256
D_FF = 2816
WIDTH = N_HEADS * HEAD_DIM
WIDTH_P = N_HEADS * LANES
IN_SIZES = (WIDTH, WIDTH, WIDTH, Q_LORA, KV_LORA, QK_ROPE)
D_IN = sum(IN_SIZES)
TAIL_P = Q_LORA + KV_LORA + LANES
EPS = 1e-6
ROPE_BASE = 10000.0
MASKED = -2e30
M_INIT = -1e30
MLA_TQ = 4096
MLA_TK = 256
MLA_GROUP = 2
MLA_SCALE = (HEAD_DIM + QK_ROPE) ** -0.5
V_ROWS = HEAD_DIM + 16
DIL_SCALE = HEAD_DIM ** -0.5

ADAM_LR = 0.001
ADAM_B1 = 0.9
ADAM_B2 = 0.999
ADAM_EPS = 1e-08
ADAM_WD = 0.01
ADAM_STEP = 10

VMEM_LIMIT = 56 * 1024 * 1024

N_CHIPS = 4

WEIGHTS = ['norm_mix_pre', 'w_in', 'q_lat_norm', 'w_uq', 'kv_lat_norm', 'w_ukv', 'out_norm_a', 'out_norm_b',
           'w_o', 'norm_mix_post', 'norm_ffn_pre', 'w_up', 'conv_w', 'conv_b', 'w_down', 'norm_ffn_post']
SHARD_AXIS = {'norm_mix_pre': None, 'w_in': 1, 'q_lat_norm': None, 'w_uq': 1, 'kv_lat_norm': None, 'w_ukv': 1,
              'out_norm_a': None, 'out_norm_b': None, 'w_o': 0, 'norm_mix_post': None, 'norm_ffn_pre': None,
              'w_up': 1, 'conv_w': 1, 'conv_b': None, 'w_down': 0, 'norm_ffn_post': None}
SMALL = [n for n in WEIGHTS if SHARD_AXIS[n] is None]


def _tile(dim, target):
    best = None
    t = LANES
    while t <= min(dim, target):
        if dim % t == 0:
            best = t
        t += LANES
    return best if best is not None else dim


def _cparams(sem=None):
    return pltpu.CompilerParams(dimension_semantics=sem, vmem_limit_bytes=VMEM_LIMIT)


def _mm(a, b, mode, out_dtype, name, add=None, tm=1024, tn=1024, tk=1024, sharded=False):
    if mode == 'nn':
        (M, K), (K2, N) = a.shape, ((b.shape[1], N_CHIPS * b.shape[2]) if sharded else b.shape)
        dims = (((1,), (0,)), ((), ()))
    elif mode == 'nt':
        (M, K), (N, K2) = a.shape, ((b.shape[1], N_CHIPS * b.shape[2]) if sharded else b.shape)
        dims = (((1,), (1,)), ((), ()))
    else:
        (K, M), (K2, N) = a.shape, b.shape
        dims = (((0,), (0,)), ((), ()))
    assert K == K2, (a.shape, b.shape, mode)
    tm, tn, tk = _tile(M, tm), _tile(N, tn), _tile(K, tk)
    if K == D_FF:
        tk = K
    if N == D_FF:
        tn, tm = N, min(tm, 512)
    if M == D_FF:
        tm = M
    if sharded and mode == 'nt':
        tk = K // N_CHIPS
    elif sharded:
        tn = N // N_CHIPS
    nk = K // tk
    if mode == 'nn':
        a_spec = pl.BlockSpec((tm, tk), lambda i, j, k: (i, k))
        b_spec = (pl.BlockSpec((None, tk, tn), lambda i, j, k: (j, k, 0)) if sharded
                  else pl.BlockSpec((tk, tn), lambda i, j, k: (k, j)))
    elif mode == 'nt':
        a_spec = pl.BlockSpec((tm, tk), lambda i, j, k: (i, k))
        b_spec = (pl.BlockSpec((None, tn, tk), lambda i, j, k: (k, j, 0)) if sharded
                  else pl.BlockSpec((tn, tk), lambda i, j, k: (j, k)))
    else:
        a_spec = pl.BlockSpec((tk, tm), lambda i, j, k: (k, i))
        b_spec = pl.BlockSpec((tk, tn), lambda i, j, k: (k, j))
    o_spec = pl.BlockSpec((tm, tn), lambda i, j, k: (i, j))
    out_shape = jax.ShapeDtypeStruct((M, N), out_dtype)
    if sharded and mode == 'tn':
        o_spec = pl.BlockSpec((None, tm, tn), lambda i, j, k: (j, i, 0))
        out_shape = jax.ShapeDtypeStruct((N_CHIPS, M, tn), out_dtype)
    has_add = add is not None

    def body(*refs):
        if has_add:
            a_ref, b_ref, add_ref, o_ref, acc_ref = refs
        else:
            a_ref, b_ref, o_ref, acc_ref = refs
        k = pl.program_id(2)

        @pl.when(k == 0)
        def _():
            acc_ref[...] = jnp.zeros_like(acc_ref)

        acc_ref[...] += lax.dot_general(a_ref[...].astype(BF16), b_ref[...].astype(BF16), dims,
                                        preferred_element_type=F32)

        @pl.when(k == nk - 1)
        def _():
            r = acc_ref[...]
            if has_add:
                r = r + add_ref[...]
            o_ref[...] = r.astype(o_ref.dtype)

    ins = [a, b] + ([add] if has_add else [])
    in_specs = [a_spec, b_spec] + ([o_spec] if has_add else [])
    return pl.pallas_call(
        body, name=name, grid=(M // tm, N // tn, nk), in_specs=in_specs, out_specs=o_spec, out_shape=out_shape,
        scratch_shapes=[pltpu.VMEM((tm, tn), F32)],
        compiler_params=_cparams(("parallel", "parallel", "arbitrary")),
    )(*ins)


def _rows(body, name, S, ts, row_ins, full_ins, row_outs, acc_outs=(), chunk_outs=()):
    in_specs = [pl.BlockSpec((ts, a.shape[1]), lambda i: (i, 0)) for a in row_ins]
    in_specs += [pl.BlockSpec(a.shape, lambda i, nd=a.ndim: (0,) * nd) for a in full_ins]
    out_specs = [pl.BlockSpec((ts, w), lambda i: (i, 0)) for (w, _) in row_outs]
    out_specs += [pl.BlockSpec(shape, lambda i, nd=len(shape): (0,) * nd) for (shape, _) in acc_outs]
    out_specs += [pl.BlockSpec((lead, 1, LANES, ts), lambda i: (0, i, 0, 0)) for (lead, _) in chunk_outs]
    out_shape = [jax.ShapeDtypeStruct((S, w), dt) for (w, dt) in row_outs]
    out_shape += [jax.ShapeDtypeStruct(shape, dt) for (shape, dt) in acc_outs]
    out_shape += [jax.ShapeDtypeStruct((lead, S // ts, LANES, ts), dt) for (lead, dt) in chunk_outs]

    def kbody(*refs):
        body(pl.program_id(0), *refs)

    return pl.pallas_call(
        kbody, name=name, grid=(S // ts,), in_specs=in_specs, out_specs=out_specs, out_shape=out_shape,
        compiler_params=_cparams(("arbitrary",)),
    )(*row_ins, *full_ins)


def _acc_add(step, ref, val):
    @pl.when(step == 0)
    def _():
        ref[...] = val

    @pl.when(step != 0)
    def _():
        ref[...] += val


def _rms_fwd(x, g, name):
    S, W = x.shape

    def body(step, x_ref, g_ref, h_ref):
        xv = x_ref[...]
        r = lax.rsqrt(jnp.mean(xv * xv, axis=-1, keepdims=True) + EPS)
        h_ref[...] = (xv * r * g_ref[...]).astype(BF16)

    return _rows(body, name, S, 512, [x], [g], [(W, BF16)])[0]


def _rms_bwd_math(xv, g, dy, width):
    r = lax.rsqrt(jnp.sum(xv * xv, axis=-1, keepdims=True) * (1.0 / width) + EPS)
    xn = xv * r
    dyg = dy * g
    dx = r * (dyg - xn * (jnp.sum(dyg * xn, axis=-1, keepdims=True) * (1.0 / width)))
    return dx, dy * xn


def _rms_bwd(x, g, dys, resid, out_dtype, name):
    S, W = x.shape
    nd = len(dys)
    has_res = resid is not None

    def body(step, *refs):
        x_ref = refs[0]
        dy_refs = refs[1:1 + nd]
        pos = 1 + nd
        res_ref = refs[pos] if has_res else None
        pos += int(has_res)
        g_ref, dx_ref, dg_ref = refs[pos], refs[pos + 1], refs[pos + 2]
        dy = dy_refs[0][...].astype(F32)
        for r_ in dy_refs[1:]:
            dy = dy + r_[...].astype(F32)
        dx, dgr = _rms_bwd_math(x_ref[...], g_ref[...], dy, W)
        if has_res:
            dx = dx + res_ref[...]
        dx_ref[...] = dx.astype(dx_ref.dtype)
        _acc_add(step, dg_ref, jnp.sum(dgr, axis=0, keepdims=True))

    row_ins = [x] + list(dys) + ([resid] if has_res else [])
    dx, dg = _rows(body, name, S, 256, row_ins, [g], [(W, out_dtype)], [((1, W), F32)])
    return dx, dg


def _rope_apply(xv, c, sa, sb):
    return xv * c + pltpu.roll(xv, 16, 1) * sa + pltpu.roll(xv, LANES - 16, 1) * sb


def _rope_transpose(dy, c, sa, sb):
    return dy * c + pltpu.roll(dy * sa, LANES - 16, 1) + pltpu.roll(dy * sb, 16, 1)


def _rope_tables(S):
    pos = jnp.arange(S, dtype=F32)
    inv_freq = jnp.exp(-math.log(ROPE_BASE) * jnp.arange(0, QK_ROPE, 2, dtype=F32) / QK_ROPE)
    ang = pos[:, None] * inv_freq[None, :]
    cos, sin = jnp.cos(ang), jnp.sin(ang)
    ones, zeros = jnp.ones((S, HEAD_DIM), F32), jnp.zeros((S, HEAD_DIM), F32)
    z16, z32 = jnp.zeros((S, 16), F32), jnp.zeros((S, 32), F32)
    c = jnp.concatenate([ones, cos, cos, z32], axis=1)
    sa = jnp.concatenate([zeros, z16, sin, z32], axis=1)
    sb = jnp.concatenate([zeros, -sin, z16, z32], axis=1)
    return c, sa, sb


def _mla_prep(proj_b, g_q, g_kv, tabs):
    S = proj_b.shape[0]

    def body(step, p_ref, c_ref, sa_ref, sb_ref, gq_ref, gkv_ref, cq_ref, ckv_ref, kr_ref):
        cq = p_ref[:, 0:Q_LORA]
        ckv = p_ref[:, Q_LORA:Q_LORA + KV_LORA]
        kr = p_ref[:, Q_LORA + KV_LORA:TAIL_P]
        rq = lax.rsqrt(jnp.mean(cq * cq, axis=-1, keepdims=True) + EPS)
        cq_ref[...] = (cq * rq * gq_ref[...]).astype(BF16)
        rk = lax.rsqrt(jnp.mean(ckv * ckv, axis=-1, keepdims=True) + EPS)
        ckv_ref[...] = (ckv * rk * gkv_ref[...]).astype(BF16)
        kr_ref[...] = _rope_apply(kr, c_ref[...], sa_ref[...], sb_ref[...])

    return _rows(body, "mla_prep", S, 512, [proj_b, *tabs], [g_q, g_kv],
                 [(Q_LORA, BF16), (KV_LORA, BF16), (LANES, F32)])


def _mla_qkv(q, kv, kr, tabs):
    S = q.shape[0]

    def body(step, q_ref, kv_ref, kr_ref, c_ref, sa_ref, sb_ref, qb_ref, kb_ref, vb_ref, kt_ref, vt_ref):
        c, sa, sb = c_ref[...], sa_ref[...], sb_ref[...]
        krv = kr_ref[...]
        row = lax.broadcasted_iota(jnp.int32, (LANES, MLA_TK), 0)
        for h in range(N_HEADS):
            blk = slice(h * LANES, (h + 1) * LANES)
            qb_ref[:, blk] = (_rope_apply(q_ref[:, blk], c, sa, sb) * MLA_SCALE).astype(BF16)
            kh = kv_ref[:, blk] + krv
            kb_ref[:, blk] = kh.astype(BF16)
            kt_ref[h, 0] = kh.T.astype(BF16)
            vh = kv_ref[:, WIDTH_P + h * LANES:WIDTH_P + (h + 1) * LANES]
            vt_ref[h, 0] = jnp.where(row == HEAD_DIM, 1.0, vh.T).astype(BF16)
        vb_ref[...] = kv_ref[:, WIDTH_P:2 * WIDTH_P].astype(BF16)

    return _rows(body, "mla_qkv", S, MLA_TK, [q, kv, kr, *tabs], [],
                 [(WIDTH_P, BF16), (WIDTH_P, BF16), (WIDTH_P, BF16)],
                 chunk_outs=[(N_HEADS, BF16), (N_HEADS, BF16)])


def _outnorm_fwd(oa, ob, ga, gb):
    S = oa.shape[0]

    def body(step, oa_ref, ob_ref, ga_ref, gb_ref, cat_ref):
        live = lax.broadcasted_iota(jnp.int32, oa_ref.shape, 1) % LANES < HEAD_DIM
        for o_ref, g_ref, off in ((oa_ref, ga_ref, 0), (ob_ref, gb_ref, WIDTH_P)):
            o = jnp.where(live, o_ref[...], 0.0)
            r = lax.rsqrt(jnp.sum(o * o, axis=-1, keepdims=True) * (1.0 / WIDTH) + EPS)
            cat_ref[:, off:off + WIDTH_P] = (o * r * g_ref[...]).astype(BF16)

    return _rows(body, "outnorm_fwd", S, 256, [oa, ob], [ga, gb], [(2 * WIDTH_P, BF16)])[0]


def _outnorm_bwd(oa, ob, ga, gb, dcat):
    S = oa.shape[0]

    def body(step, oa_ref, ob_ref, dcat_ref, ga_ref, gb_ref, dpa_ref, dob_ref, dga_ref, dgb_ref):
        live = lax.broadcasted_iota(jnp.int32, oa_ref.shape, 1) % LANES < HEAD_DIM
        lane = lax.broadcasted_iota(jnp.int32, (oa_ref.shape[0], LANES), 1)
        packed = oa_ref[...]
        o = jnp.where(live, packed, 0.0)
        do, dgr = _rms_bwd_math(o, ga_ref[...], dcat_ref[:, 0:WIDTH_P], WIDTH)
        _acc_add(step, dga_ref, jnp.sum(dgr, axis=0, keepdims=True))
        prod = do.astype(BF16).astype(F32) * o
        for h in range(N_HEADS):
            blk = slice(h * LANES, (h + 1) * LANES)
            delta = jnp.sum(prod[:, blk], axis=-1, keepdims=True)
            lse = jnp.sum(jnp.where(lane == HEAD_DIM, packed[:, blk], 0.0), axis=-1, keepdims=True)
            out = do[:, blk]
            for k, piece in enumerate(_split3(-delta) + _split3(-lse)):
                out = jnp.where(lane == HEAD_DIM + k, piece, out)
            dpa_ref[:, blk] = out

        ov = ob_ref[...]
        do_b, dgr_b = _rms_bwd_math(ov, gb_ref[...], dcat_ref[:, WIDTH_P:2 * WIDTH_P], WIDTH)
        dob_ref[...] = do_b.astype(BF16)
        _acc_add(step, dgb_ref, jnp.sum(dgr_b, axis=0, keepdims=True))

    return _rows(body, "outnorm_bwd", S, 256, [oa, ob, dcat], [ga, gb],
                 [(WIDTH_P, F32), (WIDTH_P, BF16)], [((1, WIDTH_P), F32), ((1, WIDTH_P), F32)])


def _post_mix(x, y, g_post, g_pre):
    S, W = x.shape

    def body(step, x_ref, y_ref, gp_ref, gq_ref, x1_ref, h_ref):
        yv = y_ref[...]
        r = lax.rsqrt(jnp.mean(yv * yv, axis=-1, keepdims=True) + EPS)
        x1 = x_ref[...] + yv * r * gp_ref[...]
        x1_ref[...] = x1
        r1 = lax.rsqrt(jnp.mean(x1 * x1, axis=-1, keepdims=True) + EPS)
        h_ref[...] = (x1 * r1 * gq_ref[...]).astype(BF16)

    return _rows(body, "post_mix", S, 512, [x, y], [g_post, g_pre], [(W, F32), (W, BF16)])


def _final(x1, y2, g, target):
    S, W = x1.shape
    nsteps = S // 256

    def body(step, x1_ref, y_ref, t_ref, g_ref, dx2_ref, dy_ref, dg_ref, sq_ref, loss_ref):
        yv = y_ref[...]
        gv = g_ref[...]
        r = lax.rsqrt(jnp.mean(yv * yv, axis=-1, keepdims=True) + EPS)
        yn = yv * r
        err = (x1_ref[...] + yn * gv) - t_ref[...]
        dx2 = err * (1.0 / W)
        dx2_ref[...] = dx2
        dyg = dx2 * gv
        dy = r * (dyg - yn * jnp.mean(dyg * yn, axis=-1, keepdims=True))
        dy_ref[...] = dy.astype(BF16)
        _acc_add(step, dg_ref, jnp.sum(dx2 * yn, axis=0, keepdims=True))
        _acc_add(step, sq_ref, jnp.sum(err * err, axis=0, keepdims=True))

        @pl.when(step == nsteps - 1)
        def _():
            tot = jnp.sum(sq_ref[...], axis=-1, keepdims=True) * (0.5 / W)
            loss_ref[...] = jnp.broadcast_to(tot, (1, LANES))

    return _rows(body, "final_loss", S, 256, [x1, y2, target], [g], [(W, F32), (W, BF16)],
                 [((1, W), F32), ((1, W), F32), ((1, LANES), F32)])


_GELU_C = math.sqrt(2.0 / math.pi)
_CONV_CHUNK = 128
_HALO = 8


def _gelu(g):
    t = jnp.tanh(_GELU_C * (g + 0.044715 * (g * g * g)))
    return g * (0.5 * (1.0 + t)), t


def _fill_padded(pad_ref, src_ref, S):
    zeros = jnp.zeros((_HALO, LANES), F32)
    pad_ref[0:_HALO, :] = zeros
    pad_ref[_HALO + S:2 * _HALO + S, :] = zeros
    for r0 in range(0, S, _CONV_CHUNK):
        pad_ref[_HALO + r0:_HALO + r0 + _CONV_CHUNK, :] = src_ref[r0:r0 + _CONV_CHUNK, :].astype(F32)


def _conv_fwd(u0, conv_w, conv_b):
    S, C2 = u0.shape
    nb = D_FF // LANES

    def body(u0g_ref, u0v_ref, wg_ref, wv_ref, bg_ref, bv_ref, ug_ref, uv_ref, a_ref, pg_ref, pv_ref):
        _fill_padded(pg_ref, u0g_ref, S)
        _fill_padded(pv_ref, u0v_ref, S)
        wg, wv = wg_ref[...], wv_ref[...]
        for r0 in range(0, S, _CONV_CHUNK):
            def conv(p_ref, w, b_ref):
                base = _HALO + r0
                return (p_ref[base - 1:base - 1 + _CONV_CHUNK, :] * w[0:1, :]
                        + p_ref[base:base + _CONV_CHUNK, :] * w[1:2, :]
                        + p_ref[base + 1:base + 1 + _CONV_CHUNK, :] * w[2:3, :] + b_ref[...])
            g = conv(pg_ref, wg, bg_ref)
            v = conv(pv_ref, wv, bv_ref)
            rows = slice(r0, r0 + _CONV_CHUNK)
            ug_ref[rows, :] = g
            uv_ref[rows, :] = v
            a_ref[rows, :] = (_gelu(g)[0] * v).astype(BF16)

    col = lambda off: pl.BlockSpec((S, LANES), lambda j: (0, j + off))
    wcol = lambda off: pl.BlockSpec((3, LANES), lambda j: (0, j + off))
    bcol = lambda off: pl.BlockSpec((1, LANES), lambda j: (0, j + off))
    ug, uv, a = pl.pallas_call(
        body, name="conv_gelu_fwd", grid=(nb,),
        in_specs=[col(0), col(nb), wcol(0), wcol(nb), bcol(0), bcol(nb)],
        out_specs=[col(0), col(0), col(0)],
        out_shape=[jax.ShapeDtypeStruct((S, D_FF), F32), jax.ShapeDtypeStruct((S, D_FF), F32),
                   jax.ShapeDtypeStruct((S, D_FF), BF16)],
        scratch_shapes=[pltpu.VMEM((S + 2 * _HALO, LANES), F32), pltpu.VMEM((S + 2 * _HALO, LANES), F32)],
        compiler_params=_cparams(("arbitrary",)),
    )(u0, u0, conv_w, conv_w, conv_b, conv_b)
    return ug, uv, a


def _conv_bwd(u0, ug, uv, da, conv_w):
    S = u0.shape[0]
    nb = D_FF // LANES

    def body(u0_ref, ug_ref, uv_ref, da_ref, w_ref, du0_ref, dw_ref, db_ref, pu_ref, pd_ref):
        is_g = pl.program_id(1) == 0
        _fill_padded(pu_ref, u0_ref, S)
        zeros = jnp.zeros((_HALO, LANES), F32)
        pd_ref[0:_HALO, :] = zeros
        pd_ref[_HALO + S:2 * _HALO + S, :] = zeros
        @pl.when(is_g)
        def _():
            for r0 in range(0, S, _CONV_CHUNK):
                rows = slice(r0, r0 + _CONV_CHUNK)
                g = ug_ref[rows, :]
                t = _gelu(g)[1]
                dgel = 0.5 * (1.0 + t) + (0.5 * g) * (1.0 - t * t) * (_GELU_C * (1.0 + 3.0 * 0.044715 * (g * g)))
                pd_ref[_HALO + r0:_HALO + r0 + _CONV_CHUNK, :] = da_ref[rows, :] * uv_ref[rows, :] * dgel

        @pl.when(jnp.logical_not(is_g))
        def _():
            for r0 in range(0, S, _CONV_CHUNK):
                rows = slice(r0, r0 + _CONV_CHUNK)
                pd_ref[_HALO + r0:_HALO + r0 + _CONV_CHUNK, :] = da_ref[rows, :] * _gelu(ug_ref[rows, :])[0]
        w = w_ref[...]
        acc_b = jnp.zeros((1, LANES), F32)
        acc_w = [jnp.zeros((1, LANES), F32) for _ in range(3)]
        for r0 in range(0, S, _CONV_CHUNK):
            base = _HALO + r0
            du_m = pd_ref[base - 1:base - 1 + _CONV_CHUNK, :]
            du_c = pd_ref[base:base + _CONV_CHUNK, :]
            du_p = pd_ref[base + 1:base + 1 + _CONV_CHUNK, :]
            du0_ref[r0:r0 + _CONV_CHUNK, :] = (du_p * w[0:1, :] + du_c * w[1:2, :] + du_m * w[2:3, :]).astype(BF16)
            acc_b = acc_b + jnp.sum(du_c, axis=0, keepdims=True)
            for k in range(3):
                acc_w[k] = acc_w[k] + jnp.sum(du_c * pu_ref[base + k - 1:base + k - 1 + _CONV_CHUNK, :],
                                              axis=0, keepdims=True)
        db_ref[...] = acc_b
        for k in range(3):
            dw_ref[k:k + 1, :] = acc_w[k]

    own = pl.BlockSpec((S, LANES), lambda j, half: (0, half * nb + j))
    shared = pl.BlockSpec((S, LANES), lambda j, half: (0, j))
    du0, dw, db = pl.pallas_call(
        body, name="conv_gelu_bwd", grid=(nb, 2),
        in_specs=[own, shared, shared, shared, pl.BlockSpec((3, LANES), lambda j, half: (0, half * nb + j))],
        out_specs=[own, pl.BlockSpec((3, LANES), lambda j, half: (0, half * nb + j)),
                   pl.BlockSpec((1, LANES), lambda j, half: (0, half * nb + j))],
        out_shape=[jax.ShapeDtypeStruct((S, 2 * D_FF), BF16), jax.ShapeDtypeStruct((3, 2 * D_FF), F32),
                   jax.ShapeDtypeStruct((1, 2 * D_FF), F32)],
        scratch_shapes=[pltpu.VMEM((S + 2 * _HALO, LANES), F32), pltpu.VMEM((S + 2 * _HALO, LANES), F32)],
        compiler_params=_cparams(("arbitrary", "arbitrary")),
    )(u0, ug, uv, da, conv_w)
    return du0, dw, db


DIL_HALF = 64
DIL_Q_FWD = 128
DIL_Q_BWD = 256
DILATIONS = (1, 4, 16)


def _lanes_hi_to_all(x):
    lane = lax.broadcasted_iota(jnp.int32, x.shape, 1)
    return jnp.where(lane < HEAD_DIM, pltpu.roll(x, HEAD_DIM, 1), x)


_NT = (((1,), (1,)), ((), ()))
_TN = (((0,), (0,)), ((), ()))
HEAD_COLS = 3 * LANES


def _slab_bias(q):
    row = jnp.arange(q, dtype=jnp.int32)[:, None]
    col = jnp.arange(q + 2 * DIL_HALF, dtype=jnp.int32)[None, :]
    slopes = jnp.exp2(-8.0 * jnp.arange(1, N_HEADS + 1, dtype=F32) / N_HEADS)
    out = []
    for r in DILATIONS:
        variants = []
        for shift in (DIL_HALF, 0, 2 * DIL_HALF):
            ad = jnp.abs(col - shift - row)
            variants.append(jnp.where(ad <= DIL_HALF, -slopes[:, None, None] * (ad * r).astype(F32)[None], MASKED))
        out.append(jnp.stack(variants, axis=1))
    return jnp.stack(out, axis=0)


DIL_CHUNK = 512


def _block_geometry(i, nblk, q):
    first = pl.multiple_of(i * q, q)
    slab0 = pl.multiple_of(jnp.clip(i * q - DIL_HALF, 0, (nblk - 1) * q - 2 * DIL_HALF), DIL_HALF)
    variant = jnp.where(i == 0, 1, jnp.where(i == nblk - 1, 2, 0))
    return first, slab0, variant


def _class_rows(c, r, r0, n):
    return pl.ds(c + r0 * r, n, stride=r) if r > 1 else pl.ds(r0, n)


def _dila_fwd(proj_a, bias, exchange=None):
    S = proj_a.shape[0]
    lmax = S // DILATIONS[1]
    DIL_Q, DIL_SLAB = DIL_Q_FWD, DIL_Q_FWD + 2 * DIL_HALF

    def compute(qh_ref, kh_ref, vh_ref, b_ref, o_ref, q_s, k_s, v_s, cm_s):
        lane = lax.broadcasted_iota(jnp.int32, (DIL_Q, LANES), 1)
        lane_s = lax.broadcasted_iota(jnp.int32, (DIL_SLAB, LANES), 1)
        lane_c = lax.broadcasted_iota(jnp.int32, (DIL_CHUNK, LANES), 1)

        def run(g, nblk, load_q, load_k, load_v, store):
            def block(i, carry):
                first, slab0, variant = _block_geometry(i, nblk, DIL_Q)
                qv, ks, vs = load_q(first), load_k(slab0), load_v(slab0)
                s = lax.dot_general(qv, ks, _NT, preferred_element_type=F32) + b_ref[g, 0, variant]
                m = jnp.max(s, axis=-1, keepdims=True)
                acc = jnp.dot(jnp.exp(s - m).astype(BF16), vs, preferred_element_type=F32)
                l = _lanes_hi_to_all(acc)
                store(first, jnp.where(lane < HEAD_DIM, acc / l, m + jnp.log(l)))
                return carry

            lax.fori_loop(0, nblk, block, 0, unroll=min(8, nblk))

        def direct_store(first, val):
            o_ref[pl.ds(first, DIL_Q), :] = val

        run(0, S // DIL_Q,
            lambda f: (qh_ref[pl.ds(f, DIL_Q), :] * DIL_SCALE).astype(BF16),
            lambda s0: kh_ref[pl.ds(s0, DIL_SLAB), :].astype(BF16),
            lambda s0: jnp.where(lane_s < HEAD_DIM, vh_ref[pl.ds(s0, DIL_SLAB), :], 1.0).astype(BF16),
            direct_store)

        def cm_store(first, val):
            cm_s[pl.ds(first, DIL_Q), :] = val

        for g, r in list(enumerate(DILATIONS))[1:]:
            L = S // r
            n = min(L, DIL_CHUNK)
            for c in range(r):
                for r0 in range(0, L, n):
                    src = _class_rows(c, r, r0, n)
                    q_s[r0:r0 + n, :] = (qh_ref[src, :] * DIL_SCALE).astype(BF16)
                    k_s[r0:r0 + n, :] = kh_ref[src, :].astype(BF16)
                    v_s[r0:r0 + n, :] = jnp.where(lane_c[:n] < HEAD_DIM, vh_ref[src, :], 1.0).astype(BF16)
                run(g, L // DIL_Q, lambda f: q_s[pl.ds(f, DIL_Q), :], lambda s0: k_s[pl.ds(s0, DIL_SLAB), :],
                    lambda s0: v_s[pl.ds(s0, DIL_SLAB), :], cm_store)
                for r0 in range(0, L, n):
                    dst = _class_rows(c, r, r0, n)
                    a, b = cm_s[r0:r0 + n, :], o_ref[dst, :]
                    la, lb = _lanes_hi_to_all(a), _lanes_hi_to_all(b)
                    m = jnp.maximum(la, lb)
                    wa, wb = jnp.exp(la - m), jnp.exp(lb - m)
                    tot = wa + wb
                    o_ref[dst, :] = jnp.where(lane_c[:n] < HEAD_DIM, (wa * a + wb * b) / tot, m + jnp.log(tot))

    ex = exchange
    n = ex.n if ex else 0

    def body(*refs):
        ins, ex_in, o_ref, ex_out = refs[:4], refs[4:4 + n], refs[4 + n], refs[5 + n:5 + 2 * n]
        scratch, sems = refs[5 + 2 * n:9 + 2 * n], refs[9 + 2 * n:]
        if ex:
            pl.when(pl.program_id(0) == 0)(lambda: ex.start(ex_in, ex_out, sems))
        compute(*ins, o_ref, *scratch)
        if ex:
            pl.when(pl.program_id(0) == N_HEADS - 1)(lambda: ex.finish(ex_in, ex_out, sems))

    out = pl.pallas_call(
        body, name="dil_fwd", grid=(N_HEADS,),
        in_specs=[pl.BlockSpec((S, LANES), lambda h: (0, 3 * h)), pl.BlockSpec((S, LANES), lambda h: (0, 3 * h + 1)),
                  pl.BlockSpec((S, LANES), lambda h: (0, 3 * h + 2)),
                  pl.BlockSpec((len(DILATIONS), 1, 3, DIL_Q, DIL_SLAB), lambda h: (0, h, 0, 0, 0))] + [ANY] * n,
        out_specs=[pl.BlockSpec((S, LANES), lambda h: (0, h))] + [ANY] * n,
        out_shape=[jax.ShapeDtypeStruct((S, WIDTH_P), F32)] + (ex.out_shapes() if ex else []),
        scratch_shapes=[pltpu.VMEM((lmax, LANES), BF16), pltpu.VMEM((lmax, LANES), BF16),
                        pltpu.VMEM((lmax, LANES), BF16), pltpu.VMEM((lmax, LANES), F32)] + (ex.scratch() if ex else []),
        compiler_params=_cparams(("arbitrary",)),
    )(proj_a, proj_a, proj_a, bias, *(ex.arrays if ex else []))
    return out[0], list(out[1:])


N_SPLIT = 3


def _split3(x):
    hi = x.astype(BF16).astype(F32)
    mid = (x - hi).astype(BF16).astype(F32)
    lo = (x - hi - mid).astype(BF16).astype(F32)
    return hi, mid, lo


def _dila_bwd(proj_a, dopack, bias):
    S = proj_a.shape[0]
    lmax = S // DILATIONS[1]
    DIL_Q, DIL_SLAB = DIL_Q_BWD, DIL_Q_BWD + 2 * DIL_HALF

    def body(qh_ref, kh_ref, vh_ref, d_ref, b_ref, out_ref, dq_ref, dk_ref, dv_ref, q_s, k_s, v_s, do_s,
             dq_c, dk_c, dv_c):
        def scalar_lanes(shape):
            lane = lax.broadcasted_iota(jnp.int32, shape, 1)
            return lane, (lane >= HEAD_DIM) & (lane < HEAD_DIM + N_SPLIT)

        def q_side(q, x):
            lane, ones = scalar_lanes(x.shape)
            lse_parts = pltpu.roll(x, LANES - N_SPLIT, 1)
            qv = jnp.where(lane < HEAD_DIM, q * DIL_SCALE, jnp.where(ones, lse_parts, 0.0)).astype(BF16)
            return qv, jnp.where(lane < HEAD_DIM + N_SPLIT, x, 0.0).astype(BF16)

        def kv_side(k, v):
            _, ones = scalar_lanes(k.shape)
            return jnp.where(ones, 1.0, k).astype(BF16), jnp.where(ones, 1.0, v).astype(BF16)

        def run(g, nblk, load_q, load_kv, dq_o, dk_o, dv_o):
            def block(i, carry):
                first, slab0, variant = _block_geometry(i, nblk, DIL_Q)
                rows, slab = pl.ds(first, DIL_Q), pl.ds(slab0, DIL_SLAB)
                (qv, dov), (ks, vs) = load_q(rows), load_kv(slab)
                p = jnp.exp(lax.dot_general(qv, ks, _NT, preferred_element_type=F32) + b_ref[g, 0, variant])
                ds = (p * lax.dot_general(dov, vs, _NT, preferred_element_type=F32)).astype(BF16)
                dq_o[rows, :] = jnp.dot(ds, ks, preferred_element_type=F32) * DIL_SCALE
                dk_o[slab, :] += lax.dot_general(ds, qv, _TN, preferred_element_type=F32)
                dv_o[slab, :] += lax.dot_general(p.astype(BF16), dov, _TN, preferred_element_type=F32)
                return carry

            lax.fori_loop(0, nblk, block, 0, unroll=min(8, nblk))

        dk_ref[...] = jnp.zeros_like(dk_ref)
        dv_ref[...] = jnp.zeros_like(dv_ref)
        run(0, S // DIL_Q,
            lambda rows: q_side(qh_ref[rows, :], d_ref[rows, :]),
            lambda slab: kv_side(kh_ref[slab, :], vh_ref[slab, :]),
            dq_ref, dk_ref, dv_ref)

        for g, r in list(enumerate(DILATIONS))[1:]:
            L = S // r
            n = min(L, DIL_CHUNK)
            for c in range(r):
                for r0 in range(0, L, n):
                    src = _class_rows(c, r, r0, n)
                    q_s[r0:r0 + n, :], do_s[r0:r0 + n, :] = q_side(qh_ref[src, :], d_ref[src, :])
                    k_s[r0:r0 + n, :], v_s[r0:r0 + n, :] = kv_side(kh_ref[src, :], vh_ref[src, :])
                    dk_c[r0:r0 + n, :] = jnp.zeros((n, LANES), F32)
                    dv_c[r0:r0 + n, :] = jnp.zeros((n, LANES), F32)
                run(g, L // DIL_Q, lambda rows: (q_s[rows, :], do_s[rows, :]),
                    lambda slab: (k_s[slab, :], v_s[slab, :]), dq_c, dk_c, dv_c)
                for r0 in range(0, L, n):
                    dst = _class_rows(c, r, r0, n)
                    for acc, cls in ((dq_ref, dq_c), (dk_ref, dk_c), (dv_ref, dv_c)):
                        acc[dst, :] += cls[r0:r0 + n, :]

        for r0 in range(0, S, DIL_CHUNK):
            for part, ref in enumerate((dq_ref, dk_ref, dv_ref)):
                out_ref[r0:r0 + DIL_CHUNK, part * LANES:(part + 1) * LANES] = ref[r0:r0 + DIL_CHUNK, :].astype(BF16)

    bf = lambda rows: pltpu.VMEM((rows, LANES), BF16)
    f32 = lambda rows: pltpu.VMEM((rows, LANES), F32)
    return pl.pallas_call(
        body, name="dil_bwd", grid=(N_HEADS,),
        in_specs=[pl.BlockSpec((S, LANES), lambda h: (0, 3 * h), pipeline_mode=pl.Buffered(1)),
                  pl.BlockSpec((S, LANES), lambda h: (0, 3 * h + 1), pipeline_mode=pl.Buffered(1)),
                  pl.BlockSpec((S, LANES), lambda h: (0, 3 * h + 2), pipeline_mode=pl.Buffered(1)),
                  pl.BlockSpec((S, LANES), lambda h: (0, h), pipeline_mode=pl.Buffered(1)),
                  pl.BlockSpec((len(DILATIONS), 1, 3, DIL_Q, DIL_SLAB), lambda h: (0, h, 0, 0, 0))],
        out_specs=pl.BlockSpec((S, HEAD_COLS), lambda h: (0, h)),
        out_shape=jax.ShapeDtypeStruct((S, N_HEADS * HEAD_COLS), BF16),
        scratch_shapes=[f32(S), f32(S), f32(S), bf(lmax), bf(lmax), bf(lmax), bf(lmax),
                        f32(lmax), f32(lmax), f32(lmax)],
        compiler_params=_cparams(("arbitrary",)),
    )(proj_a, proj_a, proj_a, dopack, bias)


def _mla_fwd(q, k, vt):
    S = q.shape[0]
    tq, tk = MLA_TQ, MLA_TK
    nq, nk = S // tq, S // tk

    def body(q_ref, k_ref, vt_ref, o_ref, lse_ref, acc_ref):
        qv = q_ref[...]
        acc_ref[...] = jnp.zeros_like(acc_ref)

        def group(j, m):
            chunks = [MLA_GROUP * j + u for u in range(MLA_GROUP)]
            sts = []
            for c in chunks:
                kc = k_ref[pl.ds(pl.multiple_of(c * tk, tk), tk), :]
                sts.append(lax.dot_general(kc, qv, _NT, preferred_element_type=F32))
            m_new = m
            for st in sts:
                m_new = jnp.maximum(m_new, jnp.max(st, axis=0, keepdims=True))
            upd = None
            for c, st in zip(chunks, sts):
                part = jnp.dot(vt_ref[0, c, 0:V_ROWS, :], jnp.exp(st - m_new).astype(BF16),
                               preferred_element_type=F32)
                upd = part if upd is None else upd + part
            acc_ref[...] = jnp.exp(m - m_new) * acc_ref[...] + upd
            return m_new

        m = lax.fori_loop(0, nk // MLA_GROUP, group, jnp.full((1, tq), M_INIT, F32))
        acc = acc_ref[...]
        l = acc[HEAD_DIM:HEAD_DIM + 1, :]
        ot = jnp.concatenate([acc[0:HEAD_DIM, :] / l, jnp.zeros((LANES - HEAD_DIM, tq), F32)], axis=0)
        o_ref[...] = ot.T
        lse_ref[0] = m + jnp.log(l)

    return pl.pallas_call(
        body, name="mla_fwd", grid=(N_HEADS, nq),
        in_specs=[pl.BlockSpec((tq, LANES), lambda h, i: (i, h)),
                  pl.BlockSpec((S, LANES), lambda h, i: (0, h)),
                  pl.BlockSpec((1, nk, LANES, tk), lambda h, i: (h, 0, 0, 0))],
        out_specs=[pl.BlockSpec((tq, LANES), lambda h, i: (i, h)),
                   pl.BlockSpec((1, 1, tq), lambda h, i: (h, 0, i))],
        out_shape=[jax.ShapeDtypeStruct((S, WIDTH_P), F32), jax.ShapeDtypeStruct((N_HEADS, 1, S), F32)],
        scratch_shapes=[pltpu.VMEM((V_ROWS, tq), F32)],
        compiler_params=_cparams(("parallel", "parallel")),
    )(q, k, vt)


def _mla_bwd(q, k, v, kt, do, o, lse, exchange=None):
    S = q.shape[0]
    tq, tk = MLA_TQ, MLA_TK
    nq, nk = S // tq, S // tk

    def compute(q_ref, do_ref, o_ref, lse_ref, k_ref, v_ref, kt_ref, dq_ref, dk_ref, dv_ref, dqt_ref):
        @pl.when(pl.program_id(1) == 0)
        def _():
            dk_ref[...] = jnp.zeros_like(dk_ref)
            dv_ref[...] = jnp.zeros_like(dv_ref)

        qv, dov = q_ref[...], do_ref[...]
        delta = jnp.sum((dov.astype(F32) * o_ref[...]).T, axis=0, keepdims=True)
        lse = lse_ref[0]
        dqt_ref[...] = jnp.zeros_like(dqt_ref)

        def chunk(c, carry):
            rows = pl.ds(pl.multiple_of(c * tk, tk), tk)
            kc, vc = k_ref[rows, :], v_ref[rows, :]
            pt = jnp.exp(lax.dot_general(kc, qv, _NT, preferred_element_type=F32) - lse)
            dv_ref[rows, :] += jnp.dot(pt.astype(BF16), dov, preferred_element_type=F32)
            dpt = lax.dot_general(vc, dov, _NT, preferred_element_type=F32)
            dst = (pt * (dpt - delta)).astype(BF16)
            dk_ref[rows, :] += jnp.dot(dst, qv, preferred_element_type=F32)
            dqt_ref[...] += jnp.dot(kt_ref[0, c], dst, preferred_element_type=F32)
            return carry

        lax.fori_loop(0, nk, chunk, 0, unroll=2)
        dq_ref[...] = (dqt_ref[...] * MLA_SCALE).T

    ex = exchange
    n = ex.n if ex else 0

    def body(*refs):
        ins, ex_in, outs, ex_out = refs[:7], refs[7:7 + n], refs[7 + n:10 + n], refs[10 + n:10 + 2 * n]
        dqt_ref, sems = refs[10 + 2 * n], refs[11 + 2 * n:]
        first = (pl.program_id(0) == 0) & (pl.program_id(1) == 0)
        last = (pl.program_id(0) == N_HEADS - 1) & (pl.program_id(1) == nq - 1)
        if ex:
            pl.when(first)(lambda: ex.start(ex_in, ex_out, sems))
        compute(*ins, *outs, dqt_ref)
        if ex:
            pl.when(last)(lambda: ex.finish(ex_in, ex_out, sems))

    qspec = pl.BlockSpec((tq, LANES), lambda h, i: (i, h))
    kspec = pl.BlockSpec((S, LANES), lambda h, i: (0, h))
    out = jax.ShapeDtypeStruct((S, WIDTH_P), F32)
    res = pl.pallas_call(
        body, name="mla_bwd", grid=(N_HEADS, nq),
        in_specs=[qspec, qspec, qspec, pl.BlockSpec((1, 1, tq), lambda h, i: (h, 0, i)), kspec, kspec,
                  pl.BlockSpec((1, nk, LANES, tk), lambda h, i: (h, 0, 0, 0))] + [ANY] * n,
        out_specs=[qspec, kspec, kspec] + [ANY] * n, out_shape=[out, out, out] + (ex.out_shapes() if ex else []),
        scratch_shapes=[pltpu.VMEM((LANES, tq), F32)] + (ex.scratch() if ex else []),
        compiler_params=_cparams(("arbitrary", "arbitrary")),
    )(q, do, o, lse, k, v, kt, *(ex.arrays if ex else []))
    return res[0], res[1], res[2], list(res[3:])


def _mla_bwd_prep(dq, dk, dv, tabs):
    S = dq.shape[0]

    def body(step, dq_ref, dk_ref, dv_ref, c_ref, sa_ref, sb_ref, dqp_ref, dkv_ref, dkr_ref):
        c, sa, sb = c_ref[...], sa_ref[...], sb_ref[...]
        dksum = jnp.zeros((dq_ref.shape[0], LANES), F32)
        for h in range(N_HEADS):
            blk = slice(h * LANES, (h + 1) * LANES)
            dqp_ref[:, blk] = _rope_transpose(dq_ref[:, blk], c, sa, sb).astype(BF16)
            dksum = dksum + dk_ref[:, blk]
        dkv_ref[:, 0:WIDTH_P] = dk_ref[...].astype(BF16)
        dkv_ref[:, WIDTH_P:2 * WIDTH_P] = dv_ref[...].astype(BF16)
        lane = lax.broadcasted_iota(jnp.int32, dksum.shape, 1)
        live = (lane >= HEAD_DIM) & (lane < HEAD_DIM + QK_ROPE)
        dkr_ref[...] = jnp.where(live, _rope_transpose(dksum, c, sa, sb), 0.0)

    return _rows(body, "mla_bwd_prep", S, 256, [dq, dk, dv, *tabs], [],
                 [(WIDTH_P, BF16), (2 * WIDTH_P, BF16), (LANES, F32)])


def _mla_norm_bwd(proj_b, dcq_n, dckv_n, dkr, g_q, g_kv):
    S = proj_b.shape[0]

    def body(step, p_ref, dcq_ref, dckv_ref, dkr_ref, gq_ref, gkv_ref, dp_ref, dgq_ref, dgkv_ref):
        dcq, dgq = _rms_bwd_math(p_ref[:, 0:Q_LORA], gq_ref[...], dcq_ref[...], Q_LORA)
        dckv, dgkv = _rms_bwd_math(p_ref[:, Q_LORA:Q_LORA + KV_LORA], gkv_ref[...], dckv_ref[...], KV_LORA)
        dp_ref[:, 0:Q_LORA] = dcq.astype(BF16)
        dp_ref[:, Q_LORA:Q_LORA + KV_LORA] = dckv.astype(BF16)
        dp_ref[:, Q_LORA + KV_LORA:TAIL_P] = dkr_ref[...].astype(BF16)
        _acc_add(step, dgq_ref, jnp.sum(dgq, axis=0, keepdims=True))
        _acc_add(step, dgkv_ref, jnp.sum(dgkv, axis=0, keepdims=True))

    return _rows(body, "mla_norm_bwd", S, 512, [proj_b, dcq_n, dckv_n, dkr], [g_q, g_kv], [(TAIL_P, BF16)],
                 [((1, Q_LORA), F32), ((1, KV_LORA), F32)])


def _pad_cols(w, d):
    lead = w.shape[:-1]
    w = w.reshape(lead + (N_HEADS, d))
    w = jnp.pad(w, [(0, 0)] * len(lead) + [(0, 0), (0, LANES - d)])
    return w.reshape(lead + (N_HEADS * LANES,))


def _unpad_cols(w, d):
    lead = w.shape[:-1]
    return w.reshape(lead + (N_HEADS, LANES))[..., :d].reshape(lead + (N_HEADS * d,))


def _pad_w_in(w_in):
    zeros = lambda n: jnp.zeros((D_MODEL, n), w_in.dtype)
    p = {}
    parts = [_pad_cols(w_in[:, i * WIDTH:(i + 1) * WIDTH], HEAD_DIM).reshape(D_MODEL, N_HEADS, 1, LANES)
             for i in range(3)]
    p['w_in_a'] = jnp.concatenate(parts, axis=2).reshape(D_MODEL, N_HEADS * HEAD_COLS)
    p['w_in_b'] = jnp.concatenate([w_in[:, 3 * WIDTH:3 * WIDTH + Q_LORA + KV_LORA], zeros(HEAD_DIM),
                                   w_in[:, D_IN - QK_ROPE:], zeros(LANES - HEAD_DIM - QK_ROPE)], axis=1)
    return p


def _pad_weights(w):
    p = {}
    p['w_uq'] = _pad_cols(w['w_uq'], HEAD_DIM + QK_ROPE)
    kv = w['w_ukv'].reshape(KV_LORA, N_HEADS, 2 * HEAD_DIM)
    p['w_ukv'] = jnp.concatenate([_pad_cols(kv[:, :, :HEAD_DIM].reshape(KV_LORA, WIDTH), HEAD_DIM),
                                  _pad_cols(kv[:, :, HEAD_DIM:].reshape(KV_LORA, WIDTH), HEAD_DIM)], axis=1)
    p['w_o'] = jnp.concatenate(
        [_pad_cols(w['w_o'][i * WIDTH:(i + 1) * WIDTH].T, HEAD_DIM).T for i in range(2)], axis=0)
    p['g_a'] = _pad_cols(w['out_norm_a'], HEAD_DIM)
    p['g_b'] = _pad_cols(w['out_norm_b'], HEAD_DIM)
    return p


def _unpad_w_o(dwo):
    return jnp.concatenate([_unpad_cols(dwo[i * WIDTH_P:(i + 1) * WIDTH_P].T, HEAD_DIM).T for i in range(2)], axis=0)


def _unpad_grads(d):
    g = {}
    dwa = d['w_in_a'].reshape(D_MODEL, N_HEADS, 3, LANES)
    tail = d['w_in_b']
    g['w_in'] = jnp.concatenate(
        [dwa[:, :, i, :HEAD_DIM].reshape(D_MODEL, WIDTH) for i in range(3)]
        + [tail[:, :Q_LORA + KV_LORA], tail[:, Q_LORA + KV_LORA + HEAD_DIM:Q_LORA + KV_LORA + HEAD_DIM + QK_ROPE]],
        axis=1)
    g['w_uq'] = _unpad_cols(d['w_uq'], HEAD_DIM + QK_ROPE)
    dk = _unpad_cols(d['w_ukv'][:, :WIDTH_P], HEAD_DIM).reshape(KV_LORA, N_HEADS, HEAD_DIM)
    dv = _unpad_cols(d['w_ukv'][:, WIDTH_P:], HEAD_DIM).reshape(KV_LORA, N_HEADS, HEAD_DIM)
    g['w_ukv'] = jnp.concatenate([dk, dv], axis=2).reshape(KV_LORA, 2 * WIDTH)
    g['out_norm_a'] = _unpad_cols(d['g_a'], HEAD_DIM)
    g['out_norm_b'] = _unpad_cols(d['g_b'], HEAD_DIM)
    return g


LATE = ['w_uq', 'w_ukv', 'w_o', 'w_up', 'w_down']
EARLY_GRADS = ['w_up', 'w_down', 'w_o']


def _assemble_late(gathered):
    g = dict(zip(LATE, gathered))
    return {'w_uq': jnp.concatenate([g['w_uq'][i] for i in range(N_CHIPS)], axis=1),
            'w_ukv': jnp.concatenate([g['w_ukv'][i] for i in range(N_CHIPS)], axis=1),
            'w_o': g['w_o'].reshape(D_MODEL, D_MODEL),
            'w_down': g['w_down'].reshape(D_FF, D_MODEL),
            'w_up': g['w_up']}


def _local_step(x, target, w, late_shards=None):
    S = x.shape[0]
    w = dict(w)
    p = _pad_w_in(w['w_in'])
    tabs = _rope_tables(S)
    bias = _slab_bias(DIL_Q_FWD)

    h1 = _rms_fwd(x, w['norm_mix_pre'], "rms_mix_pre")
    proj_a = _mm(h1, p['w_in_a'], 'nn', F32, "mm_in_a")
    proj_b = _mm(h1, p['w_in_b'], 'nn', F32, "mm_in_b")
    gather = _Exchange([late_shards[n] for n in LATE], False) if late_shards else None
    oa, gathered = _dila_fwd(proj_a, bias, gather)
    if late_shards:
        w.update(_assemble_late(gathered))
    p.update(_pad_weights(w))
    cq_n, ckv_n, kr = _mla_prep(proj_b, w['q_lat_norm'], w['kv_lat_norm'], tabs)
    q_lin = _mm(cq_n, p['w_uq'], 'nn', F32, "mm_uq")
    kv_lin = _mm(ckv_n, p['w_ukv'], 'nn', F32, "mm_ukv")
    qb, kb, vb, kt, vt = _mla_qkv(q_lin, kv_lin, kr, tabs)
    ob, lse_b = _mla_fwd(qb, kb, vt)
    cat = _outnorm_fwd(oa, ob, p['g_a'], p['g_b'])
    y = _mm(cat, p['w_o'], 'nn', F32, "mm_o")
    x1, h2 = _post_mix(x, y, w['norm_mix_post'], w['norm_ffn_pre'])
    u0 = _mm(h2, w['w_up'], 'nn', F32, "mm_up", sharded=True)
    ug, uv, a = _conv_fwd(u0, w['conv_w'], w['conv_b'])
    y2 = _mm(a, w['w_down'], 'nn', F32, "mm_down")
    dx2, dy2, dg_ffn_post, _, loss = _final(x1, y2, w['norm_ffn_post'], target)

    g = {'norm_ffn_post': dg_ffn_post}
    da = _mm(dy2, w['w_down'], 'nt', F32, "mm_down_dx")
    g['w_down'] = _mm(a, dy2, 'tn', BF16, "mm_down_dw")
    du0, g['conv_w'], g['conv_b'] = _conv_bwd(u0, ug, uv, da, w['conv_w'])
    dh2 = _mm(du0, w['w_up'], 'nt', F32, "mm_up_dx", sharded=True)
    g['w_up'] = _mm(h2, du0, 'tn', BF16, "mm_up_dw", sharded=True)
    dx1, g['norm_ffn_pre'] = _rms_bwd(x1, w['norm_ffn_pre'], [dh2], dx2, F32, "rms_ffn_pre_bwd")
    dy, g['norm_mix_post'] = _rms_bwd(y, w['norm_mix_post'], [dx1], None, BF16, "rms_mix_post_bwd")
    dcat = _mm(dy, p['w_o'], 'nt', F32, "mm_o_dx")
    g['w_o'] = _unpad_w_o(_mm(cat, dy, 'tn', F32, "mm_o_dw"))
    dpad = {}
    dopack_a, do_b, dpad['g_a'], dpad['g_b'] = _outnorm_bwd(oa, ob, p['g_a'], p['g_b'], dcat)

    scatter = None
    if late_shards:
        row_slots = lambda a: a.astype(BF16).reshape(N_CHIPS, a.shape[0] // N_CHIPS, a.shape[1])
        scatter = _Exchange([g['w_up'], row_slots(g['w_down']), row_slots(g['w_o'])], True)
    dq_b, dk_b, dv_b, received = _mla_bwd(qb, kb, vb, kt, do_b, ob, lse_b, scatter)
    if late_shards:
        g.update(zip(EARLY_GRADS, received))
    dq_pre, dkv, dkr = _mla_bwd_prep(dq_b, dk_b, dv_b, tabs)
    dcq_n = _mm(dq_pre, p['w_uq'], 'nt', F32, "mm_uq_dx")
    dpad['w_uq'] = _mm(cq_n, dq_pre, 'tn', F32, "mm_uq_dw")
    dckv_n = _mm(dkv, p['w_ukv'], 'nt', F32, "mm_ukv_dx")
    dpad['w_ukv'] = _mm(ckv_n, dkv, 'tn', F32, "mm_ukv_dw")
    dproj_b, g['q_lat_norm'], g['kv_lat_norm'] = _mla_norm_bwd(proj_b, dcq_n, dckv_n, dkr,
                                                               w['q_lat_norm'], w['kv_lat_norm'])

    dproj_a = _dila_bwd(proj_a, dopack_a, _slab_bias(DIL_Q_BWD))
    dh1 = _mm(dproj_b, p['w_in_b'], 'nt', F32, "mm_in_b_dx")
    dh1 = _mm(dproj_a, p['w_in_a'], 'nt', F32, "mm_in_a_dx", add=dh1)
    dpad['w_in_a'] = _mm(h1, dproj_a, 'tn', F32, "mm_in_a_dw")
    dpad['w_in_b'] = _mm(h1, dproj_b, 'tn', F32, "mm_in_b_dw")
    grad_x, g['norm_mix_pre'] = _rms_bwd(x, w['norm_mix_pre'], [dh1], dx1, F32, "rms_mix_pre_bwd")
    g.update(_unpad_grads(dpad))
    return loss, grad_x, g


MESH = pl.DeviceIdType.MESH
ANY = pl.BlockSpec(memory_space=pl.ANY)


def _place():
    x, y, c = lax.axis_index("x"), lax.axis_index("y"), lax.axis_index("c")
    chips = [(1 - x, y), (x, 1 - y), (1 - x, 1 - y)]
    return x, y, c, chips


class _Exchange:
    def __init__(self, arrays, scatter):
        self.arrays, self.scatter, self.n = list(arrays), scatter, len(arrays)

    def out_shapes(self):
        return [jax.ShapeDtypeStruct(a.shape if self.scatter else (N_CHIPS,) + a.shape, a.dtype) for a in self.arrays]

    def scratch(self):
        return [pltpu.SemaphoreType.DMA((3 * self.n,)), pltpu.SemaphoreType.DMA((3 * self.n,)),
                pltpu.SemaphoreType.DMA((self.n,))]

    def _copies(self, in_refs, out_refs, sems, arrivals):
        send_sems, recv_sems, local_sems = sems
        x, y, c, chips = _place()
        me = 2 * x + y
        src = lambda b, chip: in_refs[b].at[chip] if self.scatter else in_refs[b]
        local = [pltpu.make_async_copy(src(b, me), out_refs[b].at[me], local_sems.at[b]) for b in range(self.n)]
        sends, recvs = [], []
        for j, (px, py) in enumerate(chips):
            for b in range(self.n):
                k = j * self.n + b
                common = dict(send_sem=send_sems.at[k], recv_sem=recv_sems.at[k], device_id=(px, py, c),
                              device_id_type=MESH)
                sends.append(pltpu.make_async_remote_copy(src_ref=src(b, 2 * px + py), dst_ref=out_refs[b].at[me],
                                                          **common))
                if arrivals:
                    recvs.append(pltpu.make_async_remote_copy(src_ref=src(b, me), dst_ref=out_refs[b].at[2 * px + py],
                                                              **common))
        return local, sends, recvs

    def start(self, in_refs, out_refs, sems):
        local, sends, _ = self._copies(in_refs, out_refs, sems, False)
        for cp in local + sends:
            cp.start()

    def finish(self, in_refs, out_refs, sems):
        local, sends, recvs = self._copies(in_refs, out_refs, sems, True)
        for cp in recvs:
            cp.wait_recv()
        for cp in sends:
            cp.wait_send()
        for cp in local:
            cp.wait()


def _exchange_call(arrays, scatter, name):
    ex = _Exchange(arrays, scatter)
    n = ex.n

    def body(*refs):
        in_refs, out_refs, sems = refs[:n], refs[n:2 * n], refs[2 * n:]
        ex.start(in_refs, out_refs, sems)
        ex.finish(in_refs, out_refs, sems)

    return pl.pallas_call(body, name=name, in_specs=[ANY] * n, out_specs=[ANY] * n, out_shape=ex.out_shapes(),
                          scratch_shapes=ex.scratch())(*ex.arrays)


def _all_gather(bufs, name="gather_weights"):
    return _exchange_call(bufs, False, name)


def _scatter_grads(slots, name="scatter_grads"):
    return _exchange_call(slots, True, name)


ELEMENTWISE_BLOCK = 256 * 1024


def _row_tile(rows, cols):
    best = None
    for t in range(16, min(rows, max(16, ELEMENTWISE_BLOCK // cols)) + 1, 16):
        if rows % t == 0:
            best = t
    return best if best is not None else rows


def _sum_slots(recv, name):
    _, R, C = recv.shape
    tr = _row_tile(R, C)

    def body(r_ref, o_ref):
        f = lambda i: r_ref[i].astype(F32)
        o_ref[...] = ((f(0) + f(1)) + f(2)) + f(3)

    return pl.pallas_call(
        body, name="sum_" + name, grid=(R // tr,),
        in_specs=[pl.BlockSpec((N_CHIPS, tr, C), lambda i: (0, i, 0))],
        out_specs=pl.BlockSpec((tr, C), lambda i: (i, 0)),
        out_shape=jax.ShapeDtypeStruct((R, C), F32),
        compiler_params=_cparams(("parallel",)),
    )(recv)


def _swap_sibling(parts):
    n = len(parts)

    def body(*refs):
        p_refs, o_refs, send_sems, recv_sems = refs[:n], refs[n:2 * n], refs[2 * n], refs[2 * n + 1]
        x, y, c, _ = _place()
        cps = [pltpu.make_async_remote_copy(src_ref=p_refs[b], dst_ref=o_refs[b], send_sem=send_sems.at[b],
                                            recv_sem=recv_sems.at[b], device_id=(x, y, 1 - c), device_id_type=MESH)
               for b in range(n)]
        for cp in cps:
            cp.start()
        for cp in cps:
            cp.wait()

    return pl.pallas_call(
        body, name="swap_sibling", in_specs=[ANY] * n, out_specs=[ANY] * n,
        out_shape=[jax.ShapeDtypeStruct(p.shape, p.dtype) for p in parts],
        scratch_shapes=[pltpu.SemaphoreType.DMA((n,)), pltpu.SemaphoreType.DMA((n,))],
    )(*parts)


def _adamw(g0, g1, w, m, v, name, offset=0):
    R, C = w.shape
    tr = _row_tile(R, C)
    packed = g0.shape != w.shape
    bc1 = 1.0 - ADAM_B1 ** ADAM_STEP
    bc2 = 1.0 - ADAM_B2 ** ADAM_STEP

    def body(g0_ref, g1_ref, w_ref, m_ref, v_ref, g_ref, d_ref, nm_ref, nv_ref):
        if packed:
            g = g0_ref[:, offset:offset + C] + g1_ref[:, offset:offset + C]
        else:
            g = g0_ref[...] + g1_ref[...]
        g_ref[...] = g
        nm = ADAM_B1 * m_ref[...] + (1.0 - ADAM_B1) * g
        nv = ADAM_B2 * v_ref[...] + (1.0 - ADAM_B2) * (g * g)
        nm_ref[...] = nm
        nv_ref[...] = nv
        d_ref[...] = -ADAM_LR * ((nm / bc1) / (jnp.sqrt(nv / bc2) + ADAM_EPS) + ADAM_WD * w_ref[...])

    spec = pl.BlockSpec((tr, C), lambda i: (i, 0))
    gspec = pl.BlockSpec(g0.shape, lambda i: (0, 0)) if packed else spec
    out = jax.ShapeDtypeStruct((R, C), F32)
    return pl.pallas_call(
        body, name="adamw_" + name, grid=(R // tr,), in_specs=[gspec, gspec, spec, spec, spec],
        out_specs=[spec] * 4, out_shape=[out] * 4, compiler_params=_cparams(("parallel",)),
    )(g0, g1, w, m, v)


def kernel(x, norm_mix_pre, w_in, q_lat_norm, w_uq, kv_lat_norm, w_ukv, out_norm_a, out_norm_b, w_o, norm_mix_post, norm_ffn_pre, w_up, conv_w, conv_b, w_down, norm_ffn_post, loss_target, m_norm_mix_pre, m_w_in, m_q_lat_norm, m_w_uq, m_kv_lat_norm, m_w_ukv, m_out_norm_a, m_out_norm_b, m_w_o, m_norm_mix_post, m_norm_ffn_pre, m_w_up, m_conv_w, m_conv_b, m_w_down, m_norm_ffn_post, v_norm_mix_pre, v_w_in, v_q_lat_norm, v_w_uq, v_kv_lat_norm, v_w_ukv, v_out_norm_a, v_out_norm_b, v_w_o, v_norm_mix_post, v_norm_ffn_pre, v_w_up, v_conv_w, v_conv_b, v_w_down, v_norm_ffn_post):
    args = dict(locals())
    strip = lambda a: a[0] if a.ndim == 3 else a
    wl = {n: strip(args[n]) for n in WEIGHTS}
    ml = {n: strip(args['m_' + n]) for n in WEIGHTS}
    vl = {n: strip(args['v_' + n]) for n in WEIGHTS}

    gathered = _all_gather([wl['w_in'].astype(BF16), wl['conv_w']])
    full = {n: wl[n] for n in SMALL}
    for n, a in zip(('w_in', 'conv_w'), gathered):
        full[n] = jnp.concatenate([a[i] for i in range(N_CHIPS)], axis=1)

    loss_b, grad_x, g = _local_step(x[0], loss_target[0], full, {n: wl[n].astype(BF16) for n in LATE})

    sharded = [n for n in WEIGHTS if SHARD_AXIS[n] is not None]
    late = [n for n in sharded if n not in EARLY_GRADS]

    def slots_of(n):
        a = g[n].astype(BF16)
        if SHARD_AXIS[n] == 0:
            return a.reshape(N_CHIPS, a.shape[0] // N_CHIPS, a.shape[1])
        cols = a.shape[1] // N_CHIPS
        return jnp.stack([a[:, i * cols:(i + 1) * cols] for i in range(N_CHIPS)])

    small_pack = jnp.concatenate([g[n] for n in SMALL], axis=1)
    slots = [slots_of(n) for n in late] + [jnp.broadcast_to(small_pack[None], (N_CHIPS,) + small_pack.shape)]
    recv = dict(zip(late + ['small'], _scatter_grads(slots)))
    recv.update({n: g[n] for n in EARLY_GRADS})
    parts = [_sum_slots(recv[n], n) for n in sharded + ['small']]
    others = _swap_sibling(parts)

    outs = {}

    def record(n, results):
        for tag, a in zip(('grad', 'delta', 'new_m', 'new_v'), results):
            outs[tag + '_' + n] = a.reshape(args[n].shape)

    for n, p0, p1 in zip(sharded, parts, others):
        record(n, _adamw(p0, p1, wl[n], ml[n], vl[n], n))
    offset = 0
    for n in SMALL:
        record(n, _adamw(parts[-1], others[-1], wl[n], ml[n], vl[n], n, offset=offset))
        offset += wl[n].shape[1]

    loss = lax.psum(loss_b[0, 0], ("x", "y", "c"))
    return (loss, grad_x[None], *[outs['grad_' + n] for n in WEIGHTS], *[outs['delta_' + n] for n in WEIGHTS],
            *[outs['new_m_' + n] for n in WEIGHTS], *[outs['new_v_' + n] for n in WEIGHTS])
```

```python
import math

import jax
import jax.numpy as jnp
from jax import lax
from jax.experimental import pallas as pl
from jax.experimental.pallas import tpu as pltpu

F32 = jnp.float32
BF16 = jnp.bfloat16

LANES = 128
D_MODEL = 1024
N_HEADS = 8
HEAD_DIM = 64
QK_ROPE = 32
Q_LORA = 384
KV_LORA = 256
D_FF = 2816
WIDTH = N_HEADS * HEAD_DIM
WIDTH_P = N_HEADS * LANES
IN_SIZES = (WIDTH, WIDTH, WIDTH, Q_LORA, KV_LORA, QK_ROPE)
D_IN = sum(IN_SIZES)
TAIL_P = Q_LORA + KV_LORA + LANES
EPS = 1e-6
ROPE_BASE = 10000.0
MASKED = -2e30
M_INIT = -1e30
MLA_TQ = 4096
MLA_TK = 256
MLA_SCALE = (HEAD_DIM + QK_ROPE) ** -0.5
V_ROWS = HEAD_DIM + 16
DIL_SCALE = HEAD_DIM ** -0.5

ADAM_LR = 0.001
ADAM_B1 = 0.9
ADAM_B2 = 0.999
ADAM_EPS = 1e-08
ADAM_WD = 0.01
ADAM_STEP = 10

VMEM_LIMIT = 56 * 1024 * 1024

N_CHIPS = 4

WEIGHTS = ['norm_mix_pre', 'w_in', 'q_lat_norm', 'w_uq', 'kv_lat_norm', 'w_ukv', 'out_norm_a', 'out_norm_b',
           'w_o', 'norm_mix_post', 'norm_ffn_pre', 'w_up', 'conv_w', 'conv_b', 'w_down', 'norm_ffn_post']
SHARD_AXIS = {'norm_mix_pre': None, 'w_in': 1, 'q_lat_norm': None, 'w_uq': 1, 'kv_lat_norm': None, 'w_ukv': 1,
              'out_norm_a': None, 'out_norm_b': None, 'w_o': 0, 'norm_mix_post': None, 'norm_ffn_pre': None,
              'w_up': 1, 'conv_w': 1, 'conv_b': None, 'w_down': 0, 'norm_ffn_post': None}
SMALL = [n for n in WEIGHTS if SHARD_AXIS[n] is None]


def _tile(dim, target):
    best = None
    t = LANES
    while t <= min(dim, target):
        if dim % t == 0:
            best = t
        t += LANES
    return best if best is not None else dim


def _cparams(sem=None):
    return pltpu.CompilerParams(dimension_semantics=sem, vmem_limit_bytes=VMEM_LIMIT)


def _mm(a, b, mode, out_dtype, name, add=None, tm=1024, tn=1024, tk=1024, sharded=False):
    if mode == 'nn':
        (M, K), (K2, N) = a.shape, ((b.shape[1], N_CHIPS * b.shape[2]) if sharded else b.shape)
        dims = (((1,), (0,)), ((), ()))
    elif mode == 'nt':
        (M, K), (N, K2) = a.shape, ((b.shape[1], N_CHIPS * b.shape[2]) if sharded else b.shape)
        dims = (((1,), (1,)), ((), ()))
    else:
        (K, M), (K2, N) = a.shape, b.shape
        dims = (((0,), (0,)), ((), ()))
    assert K == K2, (a.shape, b.shape, mode)
    tm, tn, tk = _tile(M, tm), _tile(N, tn), _tile(K, tk)
    if K == D_FF:
        tk = K
    if N == D_FF:
        tn, tm = N, min(tm, 512)
    if M == D_FF:
        tm = M
    if sharded and mode == 'nt':
        tk = K // N_CHIPS
    elif sharded:
        tn = N // N_CHIPS
    nk = K // tk
    if mode == 'nn':
        a_spec = pl.BlockSpec((tm, tk), lambda i, j, k: (i, k))
        b_spec = (pl.BlockSpec((None, tk, tn), lambda i, j, k: (j, k, 0)) if sharded
                  else pl.BlockSpec((tk, tn), lambda i, j, k: (k, j)))
    elif mode == 'nt':
        a_spec = pl.BlockSpec((tm, tk), lambda i, j, k: (i, k))
        b_spec = (pl.BlockSpec((None, tn, tk), lambda i, j, k: (k, j, 0)) if sharded
                  else pl.BlockSpec((tn, tk), lambda i, j, k: (j, k)))
    else:
        a_spec = pl.BlockSpec((tk, tm), lambda i, j, k: (k, i))
        b_spec = pl.BlockSpec((tk, tn), lambda i, j, k: (k, j))
    o_spec = pl.BlockSpec((tm, tn), lambda i, j, k: (i, j))
    out_shape = jax.ShapeDtypeStruct((M, N), out_dtype)
    if sharded and mode == 'tn':
        o_spec = pl.BlockSpec((None, tm, tn), lambda i, j, k: (j, i, 0))
        out_shape = jax.ShapeDtypeStruct((N_CHIPS, M, tn), out_dtype)
    has_add = add is not None

    def body(*refs):
        if has_add:
            a_ref, b_ref, add_ref, o_ref, acc_ref = refs
        else:
            a_ref, b_ref, o_ref, acc_ref = refs
        k = pl.program_id(2)

        @pl.when(k == 0)
        def _():
            acc_ref[...] = jnp.zeros_like(acc_ref)

        acc_ref[...] += lax.dot_general(a_ref[...].astype(BF16), b_ref[...].astype(BF16), dims,
                                        preferred_element_type=F32)

        @pl.when(k == nk - 1)
        def _():
            r = acc_ref[...]
            if has_add:
                r = r + add_ref[...]
            o_ref[...] = r.astype(o_ref.dtype)

    ins = [a, b] + ([add] if has_add else [])
    in_specs = [a_spec, b_spec] + ([o_spec] if has_add else [])
    return pl.pallas_call(
        body, name=name, grid=(M // tm, N // tn, nk), in_specs=in_specs, out_specs=o_spec, out_shape=out_shape,
        scratch_shapes=[pltpu.VMEM((tm, tn), F32)],
        compiler_params=_cparams(("parallel", "parallel", "arbitrary")),
    )(*ins)


def _rows(body, name, S, ts, row_ins, full_ins, row_outs, acc_outs=(), chunk_outs=()):
    in_specs = [pl.BlockSpec((ts, a.shape[1]), lambda i: (i, 0)) for a in row_ins]
    in_specs += [pl.BlockSpec(a.shape, lambda i, nd=a.ndim: (0,) * nd) for a in full_ins]
    out_specs = [pl.BlockSpec((ts, w), lambda i: (i, 0)) for (w, _) in row_outs]
    out_specs += [pl.BlockSpec(shape, lambda i, nd=len(shape): (0,) * nd) for (shape, _) in acc_outs]
    out_specs += [pl.BlockSpec((lead, 1, LANES, ts), lambda i: (0, i, 0, 0)) for (lead, _) in chunk_outs]
    out_shape = [jax.ShapeDtypeStruct((S, w), dt) for (w, dt) in row_outs]
    out_shape += [jax.ShapeDtypeStruct(shape, dt) for (shape, dt) in acc_outs]
    out_shape += [jax.ShapeDtypeStruct((lead, S // ts, LANES, ts), dt) for (lead, dt) in chunk_outs]

    def kbody(*refs):
        body(pl.program_id(0), *refs)

    return pl.pallas_call(
        kbody, name=name, grid=(S // ts,), in_specs=in_specs, out_specs=out_specs, out_shape=out_shape,
        compiler_params=_cparams(("arbitrary",)),
    )(*row_ins, *full_ins)


def _acc_add(step, ref, val):
    @pl.when(step == 0)
    def _():
        ref[...] = val

    @pl.when(step != 0)
    def _():
        ref[...] += val


def _rms_fwd(x, g, name):
    S, W = x.shape

    def body(step, x_ref, g_ref, h_ref):
        xv = x_ref[...]
        r = lax.rsqrt(jnp.mean(xv * xv, axis=-1, keepdims=True) + EPS)
        h_ref[...] = (xv * r * g_ref[...]).astype(BF16)

    return _rows(body, name, S, 512, [x], [g], [(W, BF16)])[0]


def _rms_bwd_math(xv, g, dy, width):
    r = lax.rsqrt(jnp.sum(xv * xv, axis=-1, keepdims=True) * (1.0 / width) + EPS)
    xn = xv * r
    dyg = dy * g
    dx = r * (dyg - xn * (jnp.sum(dyg * xn, axis=-1, keepdims=True) * (1.0 / width)))
    return dx, dy * xn


def _rms_bwd(x, g, dys, resid, out_dtype, name):
    S, W = x.shape
    nd = len(dys)
    has_res = resid is not None

    def body(step, *refs):
        x_ref = refs[0]
        dy_refs = refs[1:1 + nd]
        pos = 1 + nd
        res_ref = refs[pos] if has_res else None
        pos += int(has_res)
        g_ref, dx_ref, dg_ref = refs[pos], refs[pos + 1], refs[pos + 2]
        dy = dy_refs[0][...].astype(F32)
        for r_ in dy_refs[1:]:
            dy = dy + r_[...].astype(F32)
        dx, dgr = _rms_bwd_math(x_ref[...], g_ref[...], dy, W)
        if has_res:
            dx = dx + res_ref[...]
        dx_ref[...] = dx.astype(dx_ref.dtype)
        _acc_add(step, dg_ref, jnp.sum(dgr, axis=0, keepdims=True))

    row_ins = [x] + list(dys) + ([resid] if has_res else [])
    dx, dg = _rows(body, name, S, 256, row_ins, [g], [(W, out_dtype)], [((1, W), F32)])
    return dx, dg


def _rope_apply(xv, c, sa, sb):
    return xv * c + pltpu.roll(xv, 16, 1) * sa + pltpu.roll(xv, LANES - 16, 1) * sb


def _rope_transpose(dy, c, sa, sb):
    return dy * c + pltpu.roll(dy * sa, LANES - 16, 1) + pltpu.roll(dy * sb, 16, 1)


def _rope_tables(S):
    pos = jnp.arange(S, dtype=F32)
    inv_freq = jnp.exp(-math.log(ROPE_BASE) * jnp.arange(0, QK_ROPE, 2, dtype=F32) / QK_ROPE)
    ang = pos[:, None] * inv_freq[None, :]
    cos, sin = jnp.cos(ang), jnp.sin(ang)
    ones, zeros = jnp.ones((S, HEAD_DIM), F32), jnp.zeros((S, HEAD_DIM), F32)
    z16, z32 = jnp.zeros((S, 16), F32), jnp.zeros((S, 32), F32)
    c = jnp.concatenate([ones, cos, cos, z32], axis=1)
    sa = jnp.concatenate([zeros, z16, sin, z32], axis=1)
    sb = jnp.concatenate([zeros, -sin, z16, z32], axis=1)
    return c, sa, sb


def _mla_prep(proj_b, g_q, g_kv, tabs):
    S = proj_b.shape[0]

    def body(step, p_ref, c_ref, sa_ref, sb_ref, gq_ref, gkv_ref, cq_ref, ckv_ref, kr_ref):
        cq = p_ref[:, 0:Q_LORA]
        ckv = p_ref[:, Q_LORA:Q_LORA + KV_LORA]
        kr = p_ref[:, Q_LORA + KV_LORA:TAIL_P]
        rq = lax.rsqrt(jnp.mean(cq * cq, axis=-1, keepdims=True) + EPS)
        cq_ref[...] = (cq * rq * gq_ref[...]).astype(BF16)
        rk = lax.rsqrt(jnp.mean(ckv * ckv, axis=-1, keepdims=True) + EPS)
        ckv_ref[...] = (ckv * rk * gkv_ref[...]).astype(BF16)
        kr_ref[...] = _rope_apply(kr, c_ref[...], sa_ref[...], sb_ref[...])

    return _rows(body, "mla_prep", S, 512, [proj_b, *tabs], [g_q, g_kv],
                 [(Q_LORA, BF16), (KV_LORA, BF16), (LANES, F32)])


def _mla_qkv(q, kv, kr, tabs):
    S = q.shape[0]

    def body(step, q_ref, kv_ref, kr_ref, c_ref, sa_ref, sb_ref, qb_ref, kb_ref, vb_ref, kt_ref, vt_ref):
        c, sa, sb = c_ref[...], sa_ref[...], sb_ref[...]
        krv = kr_ref[...]
        row = lax.broadcasted_iota(jnp.int32, (LANES, MLA_TK), 0)
        for h in range(N_HEADS):
            blk = slice(h * LANES, (h + 1) * LANES)
            qb_ref[:, blk] = (_rope_apply(q_ref[:, blk], c, sa, sb) * MLA_SCALE).astype(BF16)
            kh = kv_ref[:, blk] + krv
            kb_ref[:, blk] = kh.astype(BF16)
            kt_ref[h, 0] = kh.T.astype(BF16)
            vh = kv_ref[:, WIDTH_P + h * LANES:WIDTH_P + (h + 1) * LANES]
            vt_ref[h, 0] = jnp.where(row == HEAD_DIM, 1.0, vh.T).astype(BF16)
        vb_ref[...] = kv_ref[:, WIDTH_P:2 * WIDTH_P].astype(BF16)

    return _rows(body, "mla_qkv", S, MLA_TK, [q, kv, kr, *tabs], [],
                 [(WIDTH_P, BF16), (WIDTH_P, BF16), (WIDTH_P, BF16)],
                 chunk_outs=[(N_HEADS, BF16), (N_HEADS, BF16)])


def _outnorm_fwd(oa, ob, ga, gb):
    S = oa.shape[0]

    def body(step, oa_ref, ob_ref, ga_ref, gb_ref, cat_ref):
        live = lax.broadcasted_iota(jnp.int32, oa_ref.shape, 1) % LANES < HEAD_DIM
        for o_ref, g_ref, off in ((oa_ref, ga_ref, 0), (ob_ref, gb_ref, WIDTH_P)):
            o = jnp.where(live, o_ref[...], 0.0)
            r = lax.rsqrt(jnp.sum(o * o, axis=-1, keepdims=True) * (1.0 / WIDTH) + EPS)
            cat_ref[:, off:off + WIDTH_P] = (o * r * g_ref[...]).astype(BF16)

    return _rows(body, "outnorm_fwd", S, 256, [oa, ob], [ga, gb], [(2 * WIDTH_P, BF16)])[0]


def _outnorm_bwd(oa, ob, ga, gb, dcat):
    S = oa.shape[0]

    def body(step, oa_ref, ob_ref, dcat_ref, ga_ref, gb_ref, dpa_ref, dob_ref, dga_ref, dgb_ref):
        live = lax.broadcasted_iota(jnp.int32, oa_ref.shape, 1) % LANES < HEAD_DIM
        lane = lax.broadcasted_iota(jnp.int32, (oa_ref.shape[0], LANES), 1)
        packed = oa_ref[...]
        o = jnp.where(live, packed, 0.0)
        do, dgr = _rms_bwd_math(o, ga_ref[...], dcat_ref[:, 0:WIDTH_P], WIDTH)
        _acc_add(step, dga_ref, jnp.sum(dgr, axis=0, keepdims=True))
        prod = do.astype(BF16).astype(F32) * o
        for h in range(N_HEADS):
            blk = slice(h * LANES, (h + 1) * LANES)
            delta = jnp.sum(prod[:, blk], axis=-1, keepdims=True)
            lse = jnp.sum(jnp.where(lane == HEAD_DIM, packed[:, blk], 0.0), axis=-1, keepdims=True)
            out = do[:, blk]
            for k, piece in enumerate(_split3(-delta) + _split3(-lse)):
                out = jnp.where(lane == HEAD_DIM + k, piece, out)
            dpa_ref[:, blk] = out

        ov = ob_ref[...]
        do_b, dgr_b = _rms_bwd_math(ov, gb_ref[...], dcat_ref[:, WIDTH_P:2 * WIDTH_P], WIDTH)
        dob_ref[...] = do_b.astype(BF16)
        _acc_add(step, dgb_ref, jnp.sum(dgr_b, axis=0, keepdims=True))

    return _rows(body, "outnorm_bwd", S, 256, [oa, ob, dcat], [ga, gb],
                 [(WIDTH_P, F32), (WIDTH_P, BF16)], [((1, WIDTH_P), F32), ((1, WIDTH_P), F32)])


def _post_mix(x, y, g_post, g_pre):
    S, W = x.shape

    def body(step, x_ref, y_ref, gp_ref, gq_ref, x1_ref, h_ref):
        yv = y_ref[...]
        r = lax.rsqrt(jnp.mean(yv * yv, axis=-1, keepdims=True) + EPS)
        x1 = x_ref[...] + yv * r * gp_ref[...]
        x1_ref[...] = x1
        r1 = lax.rsqrt(jnp.mean(x1 * x1, axis=-1, keepdims=True) + EPS)
        h_ref[...] = (x1 * r1 * gq_ref[...]).astype(BF16)

    return _rows(body, "post_mix", S, 512, [x, y], [g_post, g_pre], [(W, F32), (W, BF16)])


def _final(x1, y2, g, target):
    S, W = x1.shape
    nsteps = S // 256

    def body(step, x1_ref, y_ref, t_ref, g_ref, dx2_ref, dy_ref, dg_ref, sq_ref, loss_ref):
        yv = y_ref[...]
        gv = g_ref[...]
        r = lax.rsqrt(jnp.mean(yv * yv, axis=-1, keepdims=True) + EPS)
        yn = yv * r
        err = (x1_ref[...] + yn * gv) - t_ref[...]
        dx2 = err * (1.0 / W)
        dx2_ref[...] = dx2
        dyg = dx2 * gv
        dy = r * (dyg - yn * jnp.mean(dyg * yn, axis=-1, keepdims=True))
        dy_ref[...] = dy.astype(BF16)
        _acc_add(step, dg_ref, jnp.sum(dx2 * yn, axis=0, keepdims=True))
        _acc_add(step, sq_ref, jnp.sum(err * err, axis=0, keepdims=True))

        @pl.when(step == nsteps - 1)
        def _():
            tot = jnp.sum(sq_ref[...], axis=-1, keepdims=True) * (0.5 / W)
            loss_ref[...] = jnp.broadcast_to(tot, (1, LANES))

    return _rows(body, "final_loss", S, 256, [x1, y2, target], [g], [(W, F32), (W, BF16)],
                 [((1, W), F32), ((1, W), F32), ((1, LANES), F32)])


_GELU_C = math.sqrt(2.0 / math.pi)
_CONV_CHUNK = 128
_HALO = 8


def _gelu(g):
    t = jnp.tanh(_GELU_C * (g + 0.044715 * (g * g * g)))
    return g * (0.5 * (1.0 + t)), t


def _fill_padded(pad_ref, src_ref, S):
    zeros = jnp.zeros((_HALO, LANES), F32)
    pad_ref[0:_HALO, :] = zeros
    pad_ref[_HALO + S:2 * _HALO + S, :] = zeros
    for r0 in range(0, S, _CONV_CHUNK):
        pad_ref[_HALO + r0:_HALO + r0 + _CONV_CHUNK, :] = src_ref[r0:r0 + _CONV_CHUNK, :].astype(F32)


def _conv_fwd(u0, conv_w, conv_b):
    S, C2 = u0.shape
    nb = D_FF // LANES

    def body(u0g_ref, u0v_ref, wg_ref, wv_ref, bg_ref, bv_ref, ug_ref, uv_ref, a_ref, pg_ref, pv_ref):
        _fill_padded(pg_ref, u0g_ref, S)
        _fill_padded(pv_ref, u0v_ref, S)
        wg, wv = wg_ref[...], wv_ref[...]
        for r0 in range(0, S, _CONV_CHUNK):
            def conv(p_ref, w, b_ref):
                base = _HALO + r0
                return (p_ref[base - 1:base - 1 + _CONV_CHUNK, :] * w[0:1, :]
                        + p_ref[base:base + _CONV_CHUNK, :] * w[1:2, :]
                        + p_ref[base + 1:base + 1 + _CONV_CHUNK, :] * w[2:3, :] + b_ref[...])
            g = conv(pg_ref, wg, bg_ref)
            v = conv(pv_ref, wv, bv_ref)
            rows = slice(r0, r0 + _CONV_CHUNK)
            ug_ref[rows, :] = g
            uv_ref[rows, :] = v
            a_ref[rows, :] = (_gelu(g)[0] * v).astype(BF16)

    col = lambda off: pl.BlockSpec((S, LANES), lambda j: (0, j + off))
    wcol = lambda off: pl.BlockSpec((3, LANES), lambda j: (0, j + off))
    bcol = lambda off: pl.BlockSpec((1, LANES), lambda j: (0, j + off))
    ug, uv, a = pl.pallas_call(
        body, name="conv_gelu_fwd", grid=(nb,),
        in_specs=[col(0), col(nb), wcol(0), wcol(nb), bcol(0), bcol(nb)],
        out_specs=[col(0), col(0), col(0)],
        out_shape=[jax.ShapeDtypeStruct((S, D_FF), F32), jax.ShapeDtypeStruct((S, D_FF), F32),
                   jax.ShapeDtypeStruct((S, D_FF), BF16)],
        scratch_shapes=[pltpu.VMEM((S + 2 * _HALO, LANES), F32), pltpu.VMEM((S + 2 * _HALO, LANES), F32)],
        compiler_params=_cparams(("arbitrary",)),
    )(u0, u0, conv_w, conv_w, conv_b, conv_b)
    return ug, uv, a


def _conv_bwd(u0, ug, uv, da, conv_w):
    S = u0.shape[0]
    nb = D_FF // LANES

    def body(u0_ref, ug_ref, uv_ref, da_ref, w_ref, du0_ref, dw_ref, db_ref, pu_ref, pd_ref):
        is_g = pl.program_id(1) == 0
        _fill_padded(pu_ref, u0_ref, S)
        zeros = jnp.zeros((_HALO, LANES), F32)
        pd_ref[0:_HALO, :] = zeros
        pd_ref[_HALO + S:2 * _HALO + S, :] = zeros
        @pl.when(is_g)
        def _():
            for r0 in range(0, S, _CONV_CHUNK):
                rows = slice(r0, r0 + _CONV_CHUNK)
                g = ug_ref[rows, :]
                t = _gelu(g)[1]
                dgel = 0.5 * (1.0 + t) + (0.5 * g) * (1.0 - t * t) * (_GELU_C * (1.0 + 3.0 * 0.044715 * (g * g)))
                pd_ref[_HALO + r0:_HALO + r0 + _CONV_CHUNK, :] = da_ref[rows, :] * uv_ref[rows, :] * dgel

        @pl.when(jnp.logical_not(is_g))
        def _():
            for r0 in range(0, S, _CONV_CHUNK):
                rows = slice(r0, r0 + _CONV_CHUNK)
                pd_ref[_HALO + r0:_HALO + r0 + _CONV_CHUNK, :] = da_ref[rows, :] * _gelu(ug_ref[rows, :])[0]
        w = w_ref[...]
        acc_b = jnp.zeros((1, LANES), F32)
        acc_w = [jnp.zeros((1, LANES), F32) for _ in range(3)]
        for r0 in range(0, S, _CONV_CHUNK):
            base = _HALO + r0
            du_m = pd_ref[base - 1:base - 1 + _CONV_CHUNK, :]
            du_c = pd_ref[base:base + _CONV_CHUNK, :]
            du_p = pd_ref[base + 1:base + 1 + _CONV_CHUNK, :]
            du0_ref[r0:r0 + _CONV_CHUNK, :] = (du_p * w[0:1, :] + du_c * w[1:2, :] + du_m * w[2:3, :]).astype(BF16)
            acc_b = acc_b + jnp.sum(du_c, axis=0, keepdims=True)
            for k in range(3):
                acc_w[k] = acc_w[k] + jnp.sum(du_c * pu_ref[base + k - 1:base + k - 1 + _CONV_CHUNK, :],
                                              axis=0, keepdims=True)
        db_ref[...] = acc_b
        for k in range(3):
            dw_ref[k:k + 1, :] = acc_w[k]

    own = pl.BlockSpec((S, LANES), lambda j, half: (0, half * nb + j))
    shared = pl.BlockSpec((S, LANES), lambda j, half: (0, j))
    du0, dw, db = pl.pallas_call(
        body, name="conv_gelu_bwd", grid=(nb, 2),
        in_specs=[own, shared, shared, shared, pl.BlockSpec((3, LANES), lambda j, half: (0, half * nb + j))],
        out_specs=[own, pl.BlockSpec((3, LANES), lambda j, half: (0, half * nb + j)),
                   pl.BlockSpec((1, LANES), lambda j, half: (0, half * nb + j))],
        out_shape=[jax.ShapeDtypeStruct((S, 2 * D_FF), BF16), jax.ShapeDtypeStruct((3, 2 * D_FF), F32),
                   jax.ShapeDtypeStruct((1, 2 * D_FF), F32)],
        scratch_shapes=[pltpu.VMEM((S + 2 * _HALO, LANES), F32), pltpu.VMEM((S + 2 * _HALO, LANES), F32)],
        compiler_params=_cparams(("arbitrary", "arbitrary")),
    )(u0, ug, uv, da, conv_w)
    return du0, dw, db


DIL_HALF = 64
DIL_Q_FWD = 128
DIL_Q_BWD = 256
DILATIONS = (1, 4, 16)


def _lanes_hi_to_all(x):
    lane = lax.broadcasted_iota(jnp.int32, x.shape, 1)
    return jnp.where(lane < HEAD_DIM, pltpu.roll(x, HEAD_DIM, 1), x)


_NT = (((1,), (1,)), ((), ()))
_TN = (((0,), (0,)), ((), ()))
HEAD_COLS = 3 * LANES


def _slab_bias(q):
    row = jnp.arange(q, dtype=jnp.int32)[:, None]
    col = jnp.arange(q + 2 * DIL_HALF, dtype=jnp.int32)[None, :]
    slopes = jnp.exp2(-8.0 * jnp.arange(1, N_HEADS + 1, dtype=F32) / N_HEADS)
    out = []
    for r in DILATIONS:
        variants = []
        for shift in (DIL_HALF, 0, 2 * DIL_HALF):
            ad = jnp.abs(col - shift - row)
            variants.append(jnp.where(ad <= DIL_HALF, -slopes[:, None, None] * (ad * r).astype(F32)[None], MASKED))
        out.append(jnp.stack(variants, axis=1))
    return jnp.stack(out, axis=0)


DIL_CHUNK = 512


def _block_geometry(i, nblk, q):
    first = pl.multiple_of(i * q, q)
    slab0 = pl.multiple_of(jnp.clip(i * q - DIL_HALF, 0, (nblk - 1) * q - 2 * DIL_HALF), DIL_HALF)
    variant = jnp.where(i == 0, 1, jnp.where(i == nblk - 1, 2, 0))
    return first, slab0, variant


def _class_rows(c, r, r0, n):
    return pl.ds(c + r0 * r, n, stride=r) if r > 1 else pl.ds(r0, n)


def _dila_fwd(proj_a, bias, exchange=None):
    S = proj_a.shape[0]
    lmax = S // DILATIONS[1]
    DIL_Q, DIL_SLAB = DIL_Q_FWD, DIL_Q_FWD + 2 * DIL_HALF

    def compute(qh_ref, kh_ref, vh_ref, b_ref, o_ref, q_s, k_s, v_s, cm_s):
        lane = lax.broadcasted_iota(jnp.int32, (DIL_Q, LANES), 1)
        lane_s = lax.broadcasted_iota(jnp.int32, (DIL_SLAB, LANES), 1)
        lane_c = lax.broadcasted_iota(jnp.int32, (DIL_CHUNK, LANES), 1)

        def run(g, nblk, load_q, load_k, load_v, store):
            def block(i, carry):
                first, slab0, variant = _block_geometry(i, nblk, DIL_Q)
                qv, ks, vs = load_q(first), load_k(slab0), load_v(slab0)
                s = lax.dot_general(qv, ks, _NT, preferred_element_type=F32) + b_ref[g, 0, variant]
                m = jnp.max(s, axis=-1, keepdims=True)
                acc = jnp.dot(jnp.exp(s - m).astype(BF16), vs, preferred_element_type=F32)
                l = _lanes_hi_to_all(acc)
                store(first, jnp.where(lane < HEAD_DIM, acc / l, m + jnp.log(l)))
                return carry

            lax.fori_loop(0, nblk, block, 0, unroll=min(8, nblk))

        def direct_store(first, val):
            o_ref[pl.ds(first, DIL_Q), :] = val

        run(0, S // DIL_Q,
            lambda f: (qh_ref[pl.ds(f, DIL_Q), :] * DIL_SCALE).astype(BF16),
            lambda s0: kh_ref[pl.ds(s0, DIL_SLAB), :].astype(BF16),
            lambda s0: jnp.where(lane_s < HEAD_DIM, vh_ref[pl.ds(s0, DIL_SLAB), :], 1.0).astype(BF16),
            direct_store)

        def cm_store(first, val):
            cm_s[pl.ds(first, DIL_Q), :] = val

        for g, r in list(enumerate(DILATIONS))[1:]:
            L = S // r
            n = min(L, DIL_CHUNK)
            for c in range(r):
                for r0 in range(0, L, n):
                    src = _class_rows(c, r, r0, n)
                    q_s[r0:r0 + n, :] = (qh_ref[src, :] * DIL_SCALE).astype(BF16)
                    k_s[r0:r0 + n, :] = kh_ref[src, :].astype(BF16)
                    v_s[r0:r0 + n, :] = jnp.where(lane_c[:n] < HEAD_DIM, vh_ref[src, :], 1.0).astype(BF16)
                run(g, L // DIL_Q, lambda f: q_s[pl.ds(f, DIL_Q), :], lambda s0: k_s[pl.ds(s0, DIL_SLAB), :],
                    lambda s0: v_s[pl.ds(s0, DIL_SLAB), :], cm_store)
                for r0 in range(0, L, n):
                    dst = _class_rows(c, r, r0, n)
                    a, b = cm_s[r0:r0 + n, :], o_ref[dst, :]
                    la, lb = _lanes_hi_to_all(a), _lanes_hi_to_all(b)
                    m = jnp.maximum(la, lb)
                    wa, wb = jnp.exp(la - m), jnp.exp(lb - m)
                    tot = wa + wb
                    o_ref[dst, :] = jnp.where(lane_c[:n] < HEAD_DIM, (wa * a + wb * b) / tot, m + jnp.log(tot))

    ex = exchange
    n = ex.n if ex else 0

    def body(*refs):
        ins, ex_in, o_ref, ex_out = refs[:4], refs[4:4 + n], refs[4 + n], refs[5 + n:5 + 2 * n]
        scratch, sems = refs[5 + 2 * n:9 + 2 * n], refs[9 + 2 * n:]
        if ex:
            pl.when(pl.program_id(0) == 0)(lambda: ex.start(ex_in, ex_out, sems))
        compute(*ins, o_ref, *scratch)
        if ex:
            pl.when(pl.program_id(0) == N_HEADS - 1)(lambda: ex.finish(ex_in, ex_out, sems))

    out = pl.pallas_call(
        body, name="dil_fwd", grid=(N_HEADS,),
        in_specs=[pl.BlockSpec((S, LANES), lambda h: (0, 3 * h)), pl.BlockSpec((S, LANES), lambda h: (0, 3 * h + 1)),
                  pl.BlockSpec((S, LANES), lambda h: (0, 3 * h + 2)),
                  pl.BlockSpec((len(DILATIONS), 1, 3, DIL_Q, DIL_SLAB), lambda h: (0, h, 0, 0, 0))] + [ANY] * n,
        out_specs=[pl.BlockSpec((S, LANES), lambda h: (0, h))] + [ANY] * n,
        out_shape=[jax.ShapeDtypeStruct((S, WIDTH_P), F32)] + (ex.out_shapes() if ex else []),
        scratch_shapes=[pltpu.VMEM((lmax, LANES), BF16), pltpu.VMEM((lmax, LANES), BF16),
                        pltpu.VMEM((lmax, LANES), BF16), pltpu.VMEM((lmax, LANES), F32)] + (ex.scratch() if ex else []),
        compiler_params=_cparams(("arbitrary",)),
    )(proj_a, proj_a, proj_a, bias, *(ex.arrays if ex else []))
    return out[0], list(out[1:])


N_SPLIT = 3


def _split3(x):
    hi = x.astype(BF16).astype(F32)
    mid = (x - hi).astype(BF16).astype(F32)
    lo = (x - hi - mid).astype(BF16).astype(F32)
    return hi, mid, lo


def _dila_bwd(proj_a, dopack, bias):
    S = proj_a.shape[0]
    lmax = S // DILATIONS[1]
    DIL_Q, DIL_SLAB = DIL_Q_BWD, DIL_Q_BWD + 2 * DIL_HALF

    def body(qh_ref, kh_ref, vh_ref, d_ref, b_ref, out_ref, dq_ref, dk_ref, dv_ref, q_s, k_s, v_s, do_s,
             dq_c, dk_c, dv_c):
        def scalar_lanes(shape):
            lane = lax.broadcasted_iota(jnp.int32, shape, 1)
            return lane, (lane >= HEAD_DIM) & (lane < HEAD_DIM + N_SPLIT)

        def q_side(q, x):
            lane, ones = scalar_lanes(x.shape)
            lse_parts = pltpu.roll(x, LANES - N_SPLIT, 1)
            qv = jnp.where(lane < HEAD_DIM, q * DIL_SCALE, jnp.where(ones, lse_parts, 0.0)).astype(BF16)
            return qv, jnp.where(lane < HEAD_DIM + N_SPLIT, x, 0.0).astype(BF16)

        def kv_side(k, v):
            _, ones = scalar_lanes(k.shape)
            return jnp.where(ones, 1.0, k).astype(BF16), jnp.where(ones, 1.0, v).astype(BF16)

        def run(g, nblk, load_q, load_kv, dq_o, dk_o, dv_o):
            def block(i, carry):
                first, slab0, variant = _block_geometry(i, nblk, DIL_Q)
                rows, slab = pl.ds(first, DIL_Q), pl.ds(slab0, DIL_SLAB)
                (qv, dov), (ks, vs) = load_q(rows), load_kv(slab)
                p = jnp.exp(lax.dot_general(qv, ks, _NT, preferred_element_type=F32) + b_ref[g, 0, variant])
                ds = (p * lax.dot_general(dov, vs, _NT, preferred_element_type=F32)).astype(BF16)
                dq_o[rows, :] = jnp.dot(ds, ks, preferred_element_type=F32) * DIL_SCALE
                dk_o[slab, :] += lax.dot_general(ds, qv, _TN, preferred_element_type=F32)
                dv_o[slab, :] += lax.dot_general(p.astype(BF16), dov, _TN, preferred_element_type=F32)
                return carry

            lax.fori_loop(0, nblk, block, 0, unroll=min(8, nblk))

        dk_ref[...] = jnp.zeros_like(dk_ref)
        dv_ref[...] = jnp.zeros_like(dv_ref)
        run(0, S // DIL_Q,
            lambda rows: q_side(qh_ref[rows, :], d_ref[rows, :]),
            lambda slab: kv_side(kh_ref[slab, :], vh_ref[slab, :]),
            dq_ref, dk_ref, dv_ref)

        for g, r in list(enumerate(DILATIONS))[1:]:
            L = S // r
            n = min(L, DIL_CHUNK)
            for c in range(r):
                for r0 in range(0, L, n):
                    src = _class_rows(c, r, r0, n)
                    q_s[r0:r0 + n, :], do_s[r0:r0 + n, :] = q_side(qh_ref[src, :], d_ref[src, :])
                    k_s[r0:r0 + n, :], v_s[r0:r0 + n, :] = kv_side(kh_ref[src, :], vh_ref[src, :])
                    dk_c[r0:r0 + n, :] = jnp.zeros((n, LANES), F32)
                    dv_c[r0:r0 + n, :] = jnp.zeros((n, LANES), F32)
                run(g, L // DIL_Q, lambda rows: (q_s[rows, :], do_s[rows, :]),
                    lambda slab: (k_s[slab, :], v_s[slab, :]), dq_c, dk_c, dv_c)
                for r0 in range(0, L, n):
                    dst = _class_rows(c, r, r0, n)
                    for acc, cls in ((dq_ref, dq_c), (dk_ref, dk_c), (dv_ref, dv_c)):
                        acc[dst, :] += cls[r0:r0 + n, :]

        for r0 in range(0, S, DIL_CHUNK):
            for part, ref in enumerate((dq_ref, dk_ref, dv_ref)):
                out_ref[r0:r0 + DIL_CHUNK, part * LANES:(part + 1) * LANES] = ref[r0:r0 + DIL_CHUNK, :].astype(BF16)

    bf = lambda rows: pltpu.VMEM((rows, LANES), BF16)
    f32 = lambda rows: pltpu.VMEM((rows, LANES), F32)
    return pl.pallas_call(
        body, name="dil_bwd", grid=(N_HEADS,),
        in_specs=[pl.BlockSpec((S, LANES), lambda h: (0, 3 * h), pipeline_mode=pl.Buffered(1)),
                  pl.BlockSpec((S, LANES), lambda h: (0, 3 * h + 1), pipeline_mode=pl.Buffered(1)),
                  pl.BlockSpec((S, LANES), lambda h: (0, 3 * h + 2), pipeline_mode=pl.Buffered(1)),
                  pl.BlockSpec((S, LANES), lambda h: (0, h), pipeline_mode=pl.Buffered(1)),
                  pl.BlockSpec((len(DILATIONS), 1, 3, DIL_Q, DIL_SLAB), lambda h: (0, h, 0, 0, 0))],
        out_specs=pl.BlockSpec((S, HEAD_COLS), lambda h: (0, h)),
        out_shape=jax.ShapeDtypeStruct((S, N_HEADS * HEAD_COLS), BF16),
        scratch_shapes=[f32(S), f32(S), f32(S), bf(lmax), bf(lmax), bf(lmax), bf(lmax),
                        f32(lmax), f32(lmax), f32(lmax)],
        compiler_params=_cparams(("arbitrary",)),
    )(proj_a, proj_a, proj_a, dopack, bias)


def _mla_fwd(q, k, vt):
    S = q.shape[0]
    tq, tk = MLA_TQ, MLA_TK
    nq, nk = S // tq, S // tk

    def body(q_ref, k_ref, vt_ref, o_ref, lse_ref, acc_ref):
        qv = q_ref[...]
        acc_ref[...] = jnp.zeros_like(acc_ref)

        def pair(j, m):
            sts = []
            for c in (2 * j, 2 * j + 1):
                kc = k_ref[pl.ds(pl.multiple_of(c * tk, tk), tk), :]
                sts.append(lax.dot_general(kc, qv, _NT, preferred_element_type=F32))
            m_new = jnp.maximum(m, jnp.maximum(jnp.max(sts[0], axis=0, keepdims=True),
                                               jnp.max(sts[1], axis=0, keepdims=True)))
            upd = [jnp.dot(vt_ref[0, c, 0:V_ROWS, :], jnp.exp(st - m_new).astype(BF16), preferred_element_type=F32)
                   for c, st in zip((2 * j, 2 * j + 1), sts)]
            acc_ref[...] = jnp.exp(m - m_new) * acc_ref[...] + (upd[0] + upd[1])
            return m_new

        m = lax.fori_loop(0, nk // 2, pair, jnp.full((1, tq), M_INIT, F32))
        acc = acc_ref[...]
        l = acc[HEAD_DIM:HEAD_DIM + 1, :]
        ot = jnp.concatenate([acc[0:HEAD_DIM, :] / l, jnp.zeros((LANES - HEAD_DIM, tq), F32)], axis=0)
        o_ref[...] = ot.T
        lse_ref[0] = m + jnp.log(l)

    return pl.pallas_call(
        body, name="mla_fwd", grid=(N_HEADS, nq),
        in_specs=[pl.BlockSpec((tq, LANES), lambda h, i: (i, h)),
                  pl.BlockSpec((S, LANES), lambda h, i: (0, h)),
                  pl.BlockSpec((1, nk, LANES, tk), lambda h, i: (h, 0, 0, 0))],
        out_specs=[pl.BlockSpec((tq, LANES), lambda h, i: (i, h)),
                   pl.BlockSpec((1, 1, tq), lambda h, i: (h, 0, i))],
        out_shape=[jax.ShapeDtypeStruct((S, WIDTH_P), F32), jax.ShapeDtypeStruct((N_HEADS, 1, S), F32)],
        scratch_shapes=[pltpu.VMEM((V_ROWS, tq), F32)],
        compiler_params=_cparams(("parallel", "parallel")),
    )(q, k, vt)


def _mla_bwd(q, k, v, kt, do, o, lse, exchange=None):
    S = q.shape[0]
    tq, tk = MLA_TQ, MLA_TK
    nq, nk = S // tq, S // tk

    def compute(q_ref, do_ref, o_ref, lse_ref, k_ref, v_ref, kt_ref, dq_ref, dk_ref, dv_ref, dqt_ref):
        @pl.when(pl.program_id(1) == 0)
        def _():
            dk_ref[...] = jnp.zeros_like(dk_ref)
            dv_ref[...] = jnp.zeros_like(dv_ref)

        qv, dov = q_ref[...], do_ref[...]
        delta = jnp.sum((dov.astype(F32) * o_ref[...]).T, axis=0, keepdims=True)
        lse = lse_ref[0]
        dqt_ref[...] = jnp.zeros_like(dqt_ref)

        def chunk(c, carry):
            rows = pl.ds(pl.multiple_of(c * tk, tk), tk)
            kc, vc = k_ref[rows, :], v_ref[rows, :]
            pt = jnp.exp(lax.dot_general(kc, qv, _NT, preferred_element_type=F32) - lse)
            dv_ref[rows, :] += jnp.dot(pt.astype(BF16), dov, preferred_element_type=F32)
            dpt = lax.dot_general(vc, dov, _NT, preferred_element_type=F32)
            dst = (pt * (dpt - delta)).astype(BF16)
            dk_ref[rows, :] += jnp.dot(dst, qv, preferred_element_type=F32)
            dqt_ref[...] += jnp.dot(kt_ref[0, c], dst, preferred_element_type=F32)
            return carry

        lax.fori_loop(0, nk, chunk, 0, unroll=2)
        dq_ref[...] = (dqt_ref[...] * MLA_SCALE).T

    ex = exchange
    n = ex.n if ex else 0

    def body(*refs):
        ins, ex_in, outs, ex_out = refs[:7], refs[7:7 + n], refs[7 + n:10 + n], refs[10 + n:10 + 2 * n]
        dqt_ref, sems = refs[10 + 2 * n], refs[11 + 2 * n:]
        first = (pl.program_id(0) == 0) & (pl.program_id(1) == 0)
        last = (pl.program_id(0) == N_HEADS - 1) & (pl.program_id(1) == nq - 1)
        if ex:
            pl.when(first)(lambda: ex.start(ex_in, ex_out, sems))
        compute(*ins, *outs, dqt_ref)
        if ex:
            pl.when(last)(lambda: ex.finish(ex_in, ex_out, sems))

    qspec = pl.BlockSpec((tq, LANES), lambda h, i: (i, h))
    kspec = pl.BlockSpec((S, LANES), lambda h, i: (0, h))
    out = jax.ShapeDtypeStruct((S, WIDTH_P), F32)
    res = pl.pallas_call(
        body, name="mla_bwd", grid=(N_HEADS, nq),
        in_specs=[qspec, qspec, qspec, pl.BlockSpec((1, 1, tq), lambda h, i: (h, 0, i)), kspec, kspec,
                  pl.BlockSpec((1, nk, LANES, tk), lambda h, i: (h, 0, 0, 0))] + [ANY] * n,
        out_specs=[qspec, kspec, kspec] + [ANY] * n, out_shape=[out, out, out] + (ex.out_shapes() if ex else []),
        scratch_shapes=[pltpu.VMEM((LANES, tq), F32)] + (ex.scratch() if ex else []),
        compiler_params=_cparams(("arbitrary", "arbitrary")),
    )(q, do, o, lse, k, v, kt, *(ex.arrays if ex else []))
    return res[0], res[1], res[2], list(res[3:])


def _mla_bwd_prep(dq, dk, dv, tabs):
    S = dq.shape[0]

    def body(step, dq_ref, dk_ref, dv_ref, c_ref, sa_ref, sb_ref, dqp_ref, dkv_ref, dkr_ref):
        c, sa, sb = c_ref[...], sa_ref[...], sb_ref[...]
        dksum = jnp.zeros((dq_ref.shape[0], LANES), F32)
        for h in range(N_HEADS):
            blk = slice(h * LANES, (h + 1) * LANES)
            dqp_ref[:, blk] = _rope_transpose(dq_ref[:, blk], c, sa, sb).astype(BF16)
            dksum = dksum + dk_ref[:, blk]
        dkv_ref[:, 0:WIDTH_P] = dk_ref[...].astype(BF16)
        dkv_ref[:, WIDTH_P:2 * WIDTH_P] = dv_ref[...].astype(BF16)
        lane = lax.broadcasted_iota(jnp.int32, dksum.shape, 1)
        live = (lane >= HEAD_DIM) & (lane < HEAD_DIM + QK_ROPE)
        dkr_ref[...] = jnp.where(live, _rope_transpose(dksum, c, sa, sb), 0.0)

    return _rows(body, "mla_bwd_prep", S, 256, [dq, dk, dv, *tabs], [],
                 [(WIDTH_P, BF16), (2 * WIDTH_P, BF16), (LANES, F32)])


def _mla_norm_bwd(proj_b, dcq_n, dckv_n, dkr, g_q, g_kv):
    S = proj_b.shape[0]

    def body(step, p_ref, dcq_ref, dckv_ref, dkr_ref, gq_ref, gkv_ref, dp_ref, dgq_ref, dgkv_ref):
        dcq, dgq = _rms_bwd_math(p_ref[:, 0:Q_LORA], gq_ref[...], dcq_ref[...], Q_LORA)
        dckv, dgkv = _rms_bwd_math(p_ref[:, Q_LORA:Q_LORA + KV_LORA], gkv_ref[...], dckv_ref[...], KV_LORA)
        dp_ref[:, 0:Q_LORA] = dcq.astype(BF16)
        dp_ref[:, Q_LORA:Q_LORA + KV_LORA] = dckv.astype(BF16)
        dp_ref[:, Q_LORA + KV_LORA:TAIL_P] = dkr_ref[...].astype(BF16)
        _acc_add(step, dgq_ref, jnp.sum(dgq, axis=0, keepdims=True))
        _acc_add(step, dgkv_ref, jnp.sum(dgkv, axis=0, keepdims=True))

    return _rows(body, "mla_norm_bwd", S, 512, [proj_b, dcq_n, dckv_n, dkr], [g_q, g_kv], [(TAIL_P, BF16)],
                 [((1, Q_LORA), F32), ((1, KV_LORA), F32)])


def _pad_cols(w, d):
    lead = w.shape[:-1]
    w = w.reshape(lead + (N_HEADS, d))
    w = jnp.pad(w, [(0, 0)] * len(lead) + [(0, 0), (0, LANES - d)])
    return w.reshape(lead + (N_HEADS * LANES,))


def _unpad_cols(w, d):
    lead = w.shape[:-1]
    return w.reshape(lead + (N_HEADS, LANES))[..., :d].reshape(lead + (N_HEADS * d,))


def _pad_w_in(w_in):
    zeros = lambda n: jnp.zeros((D_MODEL, n), w_in.dtype)
    p = {}
    parts = [_pad_cols(w_in[:, i * WIDTH:(i + 1) * WIDTH], HEAD_DIM).reshape(D_MODEL, N_HEADS, 1, LANES)
             for i in range(3)]
    p['w_in_a'] = jnp.concatenate(parts, axis=2).reshape(D_MODEL, N_HEADS * HEAD_COLS)
    p['w_in_b'] = jnp.concatenate([w_in[:, 3 * WIDTH:3 * WIDTH + Q_LORA + KV_LORA], zeros(HEAD_DIM),
                                   w_in[:, D_IN - QK_ROPE:], zeros(LANES - HEAD_DIM - QK_ROPE)], axis=1)
    return p


def _pad_weights(w):
    p = {}
    p['w_uq'] = _pad_cols(w['w_uq'], HEAD_DIM + QK_ROPE)
    kv = w['w_ukv'].reshape(KV_LORA, N_HEADS, 2 * HEAD_DIM)
    p['w_ukv'] = jnp.concatenate([_pad_cols(kv[:, :, :HEAD_DIM].reshape(KV_LORA, WIDTH), HEAD_DIM),
                                  _pad_cols(kv[:, :, HEAD_DIM:].reshape(KV_LORA, WIDTH), HEAD_DIM)], axis=1)
    p['w_o'] = jnp.concatenate(
        [_pad_cols(w['w_o'][i * WIDTH:(i + 1) * WIDTH].T, HEAD_DIM).T for i in range(2)], axis=0)
    p['g_a'] = _pad_cols(w['out_norm_a'], HEAD_DIM)
    p['g_b'] = _pad_cols(w['out_norm_b'], HEAD_DIM)
    return p


def _unpad_w_o(dwo):
    return jnp.concatenate([_unpad_cols(dwo[i * WIDTH_P:(i + 1) * WIDTH_P].T, HEAD_DIM).T for i in range(2)], axis=0)


def _unpad_grads(d):
    g = {}
    dwa = d['w_in_a'].reshape(D_MODEL, N_HEADS, 3, LANES)
    tail = d['w_in_b']
    g['w_in'] = jnp.concatenate(
        [dwa[:, :, i, :HEAD_DIM].reshape(D_MODEL, WIDTH) for i in range(3)]
        + [tail[:, :Q_LORA + KV_LORA], tail[:, Q_LORA + KV_LORA + HEAD_DIM:Q_LORA + KV_LORA + HEAD_DIM + QK_ROPE]],
        axis=1)
    g['w_uq'] = _unpad_cols(d['w_uq'], HEAD_DIM + QK_ROPE)
    dk = _unpad_cols(d['w_ukv'][:, :WIDTH_P], HEAD_DIM).reshape(KV_LORA, N_HEADS, HEAD_DIM)
    dv = _unpad_cols(d['w_ukv'][:, WIDTH_P:], HEAD_DIM).reshape(KV_LORA, N_HEADS, HEAD_DIM)
    g['w_ukv'] = jnp.concatenate([dk, dv], axis=2).reshape(KV_LORA, 2 * WIDTH)
    g['out_norm_a'] = _unpad_cols(d['g_a'], HEAD_DIM)
    g['out_norm_b'] = _unpad_cols(d['g_b'], HEAD_DIM)
    return g


LATE = ['w_uq', 'w_ukv', 'w_o', 'w_up', 'w_down']
EARLY_GRADS = ['w_up', 'w_down', 'w_o']


def _assemble_late(gathered):
    g = dict(zip(LATE, gathered))
    return {'w_uq': jnp.concatenate([g['w_uq'][i] for i in range(N_CHIPS)], axis=1),
            'w_ukv': jnp.concatenate([g['w_ukv'][i] for i in range(N_CHIPS)], axis=1),
            'w_o': g['w_o'].reshape(D_MODEL, D_MODEL),
            'w_down': g['w_down'].reshape(D_FF, D_MODEL),
            'w_up': g['w_up']}


def _local_step(x, target, w, late_shards=None):
    S = x.shape[0]
    w = dict(w)
    p = _pad_w_in(w['w_in'])
    tabs = _rope_tables(S)
    bias = _slab_bias(DIL_Q_FWD)

    h1 = _rms_fwd(x, w['norm_mix_pre'], "rms_mix_pre")
    proj_a = _mm(h1, p['w_in_a'], 'nn', F32, "mm_in_a")
    proj_b = _mm(h1, p['w_in_b'], 'nn', F32, "mm_in_b")
    gather = _Exchange([late_shards[n] for n in LATE], False) if late_shards else None
    oa, gathered = _dila_fwd(proj_a, bias, gather)
    if late_shards:
        w.update(_assemble_late(gathered))
    p.update(_pad_weights(w))
    cq_n, ckv_n, kr = _mla_prep(proj_b, w['q_lat_norm'], w['kv_lat_norm'], tabs)
    q_lin = _mm(cq_n, p['w_uq'], 'nn', F32, "mm_uq")
    kv_lin = _mm(ckv_n, p['w_ukv'], 'nn', F32, "mm_ukv")
    qb, kb, vb, kt, vt = _mla_qkv(q_lin, kv_lin, kr, tabs)
    ob, lse_b = _mla_fwd(qb, kb, vt)
    cat = _outnorm_fwd(oa, ob, p['g_a'], p['g_b'])
    y = _mm(cat, p['w_o'], 'nn', F32, "mm_o")
    x1, h2 = _post_mix(x, y, w['norm_mix_post'], w['norm_ffn_pre'])
    u0 = _mm(h2, w['w_up'], 'nn', F32, "mm_up", sharded=True)
    ug, uv, a = _conv_fwd(u0, w['conv_w'], w['conv_b'])
    y2 = _mm(a, w['w_down'], 'nn', F32, "mm_down")
    dx2, dy2, dg_ffn_post, _, loss = _final(x1, y2, w['norm_ffn_post'], target)

    g = {'norm_ffn_post': dg_ffn_post}
    da = _mm(dy2, w['w_down'], 'nt', F32, "mm_down_dx")
    g['w_down'] = _mm(a, dy2, 'tn', BF16, "mm_down_dw")
    du0, g['conv_w'], g['conv_b'] = _conv_bwd(u0, ug, uv, da, w['conv_w'])
    dh2 = _mm(du0, w['w_up'], 'nt', F32, "mm_up_dx", sharded=True)
    g['w_up'] = _mm(h2, du0, 'tn', BF16, "mm_up_dw", sharded=True)
    dx1, g['norm_ffn_pre'] = _rms_bwd(x1, w['norm_ffn_pre'], [dh2], dx2, F32, "rms_ffn_pre_bwd")
    dy, g['norm_mix_post'] = _rms_bwd(y, w['norm_mix_post'], [dx1], None, BF16, "rms_mix_post_bwd")
    dcat = _mm(dy, p['w_o'], 'nt', F32, "mm_o_dx")
    g['w_o'] = _unpad_w_o(_mm(cat, dy, 'tn', F32, "mm_o_dw"))
    dpad = {}
    dopack_a, do_b, dpad['g_a'], dpad['g_b'] = _outnorm_bwd(oa, ob, p['g_a'], p['g_b'], dcat)

    scatter = None
    if late_shards:
        row_slots = lambda a: a.astype(BF16).reshape(N_CHIPS, a.shape[0] // N_CHIPS, a.shape[1])
        scatter = _Exchange([g['w_up'], row_slots(g['w_down']), row_slots(g['w_o'])], True)
    dq_b, dk_b, dv_b, received = _mla_bwd(qb, kb, vb, kt, do_b, ob, lse_b, scatter)
    if late_shards:
        g.update(zip(EARLY_GRADS, received))
    dq_pre, dkv, dkr = _mla_bwd_prep(dq_b, dk_b, dv_b, tabs)
    dcq_n = _mm(dq_pre, p['w_uq'], 'nt', F32, "mm_uq_dx")
    dpad['w_uq'] = _mm(cq_n, dq_pre, 'tn', F32, "mm_uq_dw")
    dckv_n = _mm(dkv, p['w_ukv'], 'nt', F32, "mm_ukv_dx")
    dpad['w_ukv'] = _mm(ckv_n, dkv, 'tn', F32, "mm_ukv_dw")
    dproj_b, g['q_lat_norm'], g['kv_lat_norm'] = _mla_norm_bwd(proj_b, dcq_n, dckv_n, dkr,
                                                               w['q_lat_norm'], w['kv_lat_norm'])

    dproj_a = _dila_bwd(proj_a, dopack_a, _slab_bias(DIL_Q_BWD))
    dh1 = _mm(dproj_b, p['w_in_b'], 'nt', F32, "mm_in_b_dx")
    dh1 = _mm(dproj_a, p['w_in_a'], 'nt', F32, "mm_in_a_dx", add=dh1)
    dpad['w_in_a'] = _mm(h1, dproj_a, 'tn', F32, "mm_in_a_dw")
    dpad['w_in_b'] = _mm(h1, dproj_b, 'tn', F32, "mm_in_b_dw")
    grad_x, g['norm_mix_pre'] = _rms_bwd(x, w['norm_mix_pre'], [dh1], dx1, F32, "rms_mix_pre_bwd")
    g.update(_unpad_grads(dpad))
    return loss, grad_x, g


MESH = pl.DeviceIdType.MESH
ANY = pl.BlockSpec(memory_space=pl.ANY)


def _place():
    x, y, c = lax.axis_index("x"), lax.axis_index("y"), lax.axis_index("c")
    chips = [(1 - x, y), (x, 1 - y), (1 - x, 1 - y)]
    return x, y, c, chips


class _Exchange:
    def __init__(self, arrays, scatter):
        self.arrays, self.scatter, self.n = list(arrays), scatter, len(arrays)

    def out_shapes(self):
        return [jax.ShapeDtypeStruct(a.shape if self.scatter else (N_CHIPS,) + a.shape, a.dtype) for a in self.arrays]

    def scratch(self):
        return [pltpu.SemaphoreType.DMA((3 * self.n,)), pltpu.SemaphoreType.DMA((3 * self.n,)),
                pltpu.SemaphoreType.DMA((self.n,))]

    def _copies(self, in_refs, out_refs, sems, arrivals):
        send_sems, recv_sems, local_sems = sems
        x, y, c, chips = _place()
        me = 2 * x + y
        src = lambda b, chip: in_refs[b].at[chip] if self.scatter else in_refs[b]
        local = [pltpu.make_async_copy(src(b, me), out_refs[b].at[me], local_sems.at[b]) for b in range(self.n)]
        sends, recvs = [], []
        for j, (px, py) in enumerate(chips):
            for b in range(self.n):
                k = j * self.n + b
                common = dict(send_sem=send_sems.at[k], recv_sem=recv_sems.at[k], device_id=(px, py, c),
                              device_id_type=MESH)
                sends.append(pltpu.make_async_remote_copy(src_ref=src(b, 2 * px + py), dst_ref=out_refs[b].at[me],
                                                          **common))
                if arrivals:
                    recvs.append(pltpu.make_async_remote_copy(src_ref=src(b, me), dst_ref=out_refs[b].at[2 * px + py],
                                                              **common))
        return local, sends, recvs

    def start(self, in_refs, out_refs, sems):
        local, sends, _ = self._copies(in_refs, out_refs, sems, False)
        for cp in local + sends:
            cp.start()

    def finish(self, in_refs, out_refs, sems):
        local, sends, recvs = self._copies(in_refs, out_refs, sems, True)
        for cp in recvs:
            cp.wait_recv()
        for cp in sends:
            cp.wait_send()
        for cp in local:
            cp.wait()


def _exchange_call(arrays, scatter, name):
    ex = _Exchange(arrays, scatter)
    n = ex.n

    def body(*refs):
        in_refs, out_refs, sems = refs[:n], refs[n:2 * n], refs[2 * n:]
        ex.start(in_refs, out_refs, sems)
        ex.finish(in_refs, out_refs, sems)

    return pl.pallas_call(body, name=name, in_specs=[ANY] * n, out_specs=[ANY] * n, out_shape=ex.out_shapes(),
                          scratch_shapes=ex.scratch())(*ex.arrays)


def _all_gather(bufs, name="gather_weights"):
    return _exchange_call(bufs, False, name)


def _scatter_grads(slots, name="scatter_grads"):
    return _exchange_call(slots, True, name)


ELEMENTWISE_BLOCK = 256 * 1024


def _row_tile(rows, cols):
    best = None
    for t in range(16, min(rows, max(16, ELEMENTWISE_BLOCK // cols)) + 1, 16):
        if rows % t == 0:
            best = t
    return best if best is not None else rows


def _sum_slots(recv, name):
    _, R, C = recv.shape
    tr = _row_tile(R, C)

    def body(r_ref, o_ref):
        f = lambda i: r_ref[i].astype(F32)
        o_ref[...] = ((f(0) + f(1)) + f(2)) + f(3)

    return pl.pallas_call(
        body, name="sum_" + name, grid=(R // tr,),
        in_specs=[pl.BlockSpec((N_CHIPS, tr, C), lambda i: (0, i, 0))],
        out_specs=pl.BlockSpec((tr, C), lambda i: (i, 0)),
        out_shape=jax.ShapeDtypeStruct((R, C), F32),
        compiler_params=_cparams(("parallel",)),
    )(recv)


def _swap_sibling(parts):
    n = len(parts)

    def body(*refs):
        p_refs, o_refs, send_sems, recv_sems = refs[:n], refs[n:2 * n], refs[2 * n], refs[2 * n + 1]
        x, y, c, _ = _place()
        cps = [pltpu.make_async_remote_copy(src_ref=p_refs[b], dst_ref=o_refs[b], send_sem=send_sems.at[b],
                                            recv_sem=recv_sems.at[b], device_id=(x, y, 1 - c), device_id_type=MESH)
               for b in range(n)]
        for cp in cps:
            cp.start()
        for cp in cps:
            cp.wait()

    return pl.pallas_call(
        body, name="swap_sibling", in_specs=[ANY] * n, out_specs=[ANY] * n,
        out_shape=[jax.ShapeDtypeStruct(p.shape, p.dtype) for p in parts],
        scratch_shapes=[pltpu.SemaphoreType.DMA((n,)), pltpu.SemaphoreType.DMA((n,))],
    )(*parts)


def _adamw(g0, g1, w, m, v, name, offset=0):
    R, C = w.shape
    tr = _row_tile(R, C)
    packed = g0.shape != w.shape
    bc1 = 1.0 - ADAM_B1 ** ADAM_STEP
    bc2 = 1.0 - ADAM_B2 ** ADAM_STEP

    def body(g0_ref, g1_ref, w_ref, m_ref, v_ref, g_ref, d_ref, nm_ref, nv_ref):
        if packed:
            g = g0_ref[:, offset:offset + C] + g1_ref[:, offset:offset + C]
        else:
            g = g0_ref[...] + g1_ref[...]
        g_ref[...] = g
        nm = ADAM_B1 * m_ref[...] + (1.0 - ADAM_B1) * g
        nv = ADAM_B2 * v_ref[...] + (1.0 - ADAM_B2) * (g * g)
        nm_ref[...] = nm
        nv_ref[...] = nv
        d_ref[...] = -ADAM_LR * ((nm / bc1) / (jnp.sqrt(nv / bc2) + ADAM_EPS) + ADAM_WD * w_ref[...])

    spec = pl.BlockSpec((tr, C), lambda i: (i, 0))
    gspec = pl.BlockSpec(g0.shape, lambda i: (0, 0)) if packed else spec
    out = jax.ShapeDtypeStruct((R, C), F32)
    return pl.pallas_call(
        body, name="adamw_" + name, grid=(R // tr,), in_specs=[gspec, gspec, spec, spec, spec],
        out_specs=[spec] * 4, out_shape=[out] * 4, compiler_params=_cparams(("parallel",)),
    )(g0, g1, w, m, v)


def kernel(x, norm_mix_pre, w_in, q_lat_norm, w_uq, kv_lat_norm, w_ukv, out_norm_a, out_norm_b, w_o, norm_mix_post, norm_ffn_pre, w_up, conv_w, conv_b, w_down, norm_ffn_post, loss_target, m_norm_mix_pre, m_w_in, m_q_lat_norm, m_w_uq, m_kv_lat_norm, m_w_ukv, m_out_norm_a, m_out_norm_b, m_w_o, m_norm_mix_post, m_norm_ffn_pre, m_w_up, m_conv_w, m_conv_b, m_w_down, m_norm_ffn_post, v_norm_mix_pre, v_w_in, v_q_lat_norm, v_w_uq, v_kv_lat_norm, v_w_ukv, v_out_norm_a, v_out_norm_b, v_w_o, v_norm_mix_post, v_norm_ffn_pre, v_w_up, v_conv_w, v_conv_b, v_w_down, v_norm_ffn_post):
    args = dict(locals())
    strip = lambda a: a[0] if a.ndim == 3 else a
    wl = {n: strip(args[n]) for n in WEIGHTS}
    ml = {n: strip(args['m_' + n]) for n in WEIGHTS}
    vl = {n: strip(args['v_' + n]) for n in WEIGHTS}

    gathered = _all_gather([wl['w_in'].astype(BF16), wl['conv_w']])
    full = {n: wl[n] for n in SMALL}
    for n, a in zip(('w_in', 'conv_w'), gathered):
        full[n] = jnp.concatenate([a[i] for i in range(N_CHIPS)], axis=1)

    loss_b, grad_x, g = _local_step(x[0], loss_target[0], full, {n: wl[n].astype(BF16) for n in LATE})

    sharded = [n for n in WEIGHTS if SHARD_AXIS[n] is not None]
    late = [n for n in sharded if n not in EARLY_GRADS]

    def slots_of(n):
        a = g[n].astype(BF16)
        cols = a.shape[1] // N_CHIPS
        return jnp.stack([a[:, i * cols:(i + 1) * cols] for i in range(N_CHIPS)])

    small_pack = jnp.concatenate([g[n] for n in SMALL], axis=1)
    slots = [slots_of(n) for n in late] + [jnp.broadcast_to(small_pack[None], (N_CHIPS,) + small_pack.shape)]
    recv = dict(zip(late + ['small'], _scatter_grads(slots)))
    recv.update({n: g[n] for n in EARLY_GRADS})
    parts = [_sum_slots(recv[n], n) for n in sharded + ['small']]
    others = _swap_sibling(parts)

    outs = {}

    def record(n, results):
        for tag, a in zip(('grad', 'delta', 'new_m', 'new_v'), results):
            outs[tag + '_' + n] = a.reshape(args[n].shape)

    for n, p0, p1 in zip(sharded, parts, others):
        record(n, _adamw(p0, p1, wl[n], ml[n], vl[n], n))
    offset = 0
    for n in SMALL:
        record(n, _adamw(parts[-1], others[-1], wl[n], ml[n], vl[n], n, offset=offset))
        offset += wl[n].shape[1]

    loss = lax.psum(loss_b[0, 0], ("x", "y", "c"))
    return (loss, grad_x[None], *[outs['grad_' + n] for n in WEIGHTS], *[outs['delta_' + n] for n in WEIGHTS],
            *[outs['new_m_' + n] for n in WEIGHTS], *[outs['new_v_' + n] for n in WEIGHTS])
```

```python
import math

import jax
import jax.numpy as jnp
from jax import lax
from jax.experimental import pallas as pl
from jax.experimental.pallas import tpu as pltpu

F32 = jnp.float32
BF16 = jnp.bfloat16

LANES = 128
D_MODEL = 1024
N_HEADS = 8
HEAD_DIM = 64
QK_ROPE = 32
Q_LORA = 384
KV_LORA = 256
D_FF = 2816
WIDTH = N_HEADS * HEAD_DIM
WIDTH_P = N_HEADS * LANES
IN_SIZES = (WIDTH, WIDTH, WIDTH, Q_LORA, KV_LORA, QK_ROPE)
D_IN = sum(IN_SIZES)
TAIL_P = Q_LORA + KV_LORA + LANES
EPS = 1e-6
ROPE_BASE = 10000.0
MASKED = -2e30
M_INIT = -1e30
MLA_TQ = 4096
MLA_TK = 256
MLA_SCALE = (HEAD_DIM + QK_ROPE) ** -0.5
V_ROWS = HEAD_DIM + 16
DIL_SCALE = HEAD_DIM ** -0.5

ADAM_LR = 0.001
ADAM_B1 = 0.9
ADAM_B2 = 0.999
ADAM_EPS = 1e-08
ADAM_WD = 0.01
ADAM_STEP = 10

VMEM_LIMIT = 56 * 1024 * 1024

N_CHIPS = 4

WEIGHTS = ['norm_mix_pre', 'w_in', 'q_lat_norm', 'w_uq', 'kv_lat_norm', 'w_ukv', 'out_norm_a', 'out_norm_b',
           'w_o', 'norm_mix_post', 'norm_ffn_pre', 'w_up', 'conv_w', 'conv_b', 'w_down', 'norm_ffn_post']
SHARD_AXIS = {'norm_mix_pre': None, 'w_in': 1, 'q_lat_norm': None, 'w_uq': 1, 'kv_lat_norm': None, 'w_ukv': 1,
              'out_norm_a': None, 'out_norm_b': None, 'w_o': 0, 'norm_mix_post': None, 'norm_ffn_pre': None,
              'w_up': 1, 'conv_w': 1, 'conv_b': None, 'w_down': 0, 'norm_ffn_post': None}
SMALL = [n for n in WEIGHTS if SHARD_AXIS[n] is None]


def _tile(dim, target):
    best = None
    t = LANES
    while t <= min(dim, target):
        if dim % t == 0:
            best = t
        t += LANES
    return best if best is not None else dim


def _cparams(sem=None):
    return pltpu.CompilerParams(dimension_semantics=sem, vmem_limit_bytes=VMEM_LIMIT)


def _mm(a, b, mode, out_dtype, name, add=None, tm=1024, tn=1024, tk=1024, sharded=False):
    if mode == 'nn':
        (M, K), (K2, N) = a.shape, ((b.shape[1], N_CHIPS * b.shape[2]) if sharded else b.shape)
        dims = (((1,), (0,)), ((), ()))
    elif mode == 'nt':
        (M, K), (N, K2) = a.shape, ((b.shape[1], N_CHIPS * b.shape[2]) if sharded else b.shape)
        dims = (((1,), (1,)), ((), ()))
    else:
        (K, M), (K2, N) = a.shape, b.shape
        dims = (((0,), (0,)), ((), ()))
    assert K == K2, (a.shape, b.shape, mode)
    tm, tn, tk = _tile(M, tm), _tile(N, tn), _tile(K, tk)
    if K == D_FF:
        tk = K
    if N == D_FF:
        tn, tm = N, min(tm, 512)
    if M == D_FF:
        tm = M
    if sharded and mode == 'nt':
        tk = K // N_CHIPS
    elif sharded:
        tn = N // N_CHIPS
    nk = K // tk
    if mode == 'nn':
        a_spec = pl.BlockSpec((tm, tk), lambda i, j, k: (i, k))
        b_spec = (pl.BlockSpec((None, tk, tn), lambda i, j, k: (j, k, 0)) if sharded
                  else pl.BlockSpec((tk, tn), lambda i, j, k: (k, j)))
    elif mode == 'nt':
        a_spec = pl.BlockSpec((tm, tk), lambda i, j, k: (i, k))
        b_spec = (pl.BlockSpec((None, tn, tk), lambda i, j, k: (k, j, 0)) if sharded
                  else pl.BlockSpec((tn, tk), lambda i, j, k: (j, k)))
    else:
        a_spec = pl.BlockSpec((tk, tm), lambda i, j, k: (k, i))
        b_spec = pl.BlockSpec((tk, tn), lambda i, j, k: (k, j))
    o_spec = pl.BlockSpec((tm, tn), lambda i, j, k: (i, j))
    out_shape = jax.ShapeDtypeStruct((M, N), out_dtype)
    if sharded and mode == 'tn':
        o_spec = pl.BlockSpec((None, tm, tn), lambda i, j, k: (j, i, 0))
        out_shape = jax.ShapeDtypeStruct((N_CHIPS, M, tn), out_dtype)
    has_add = add is not None

    def body(*refs):
        if has_add:
            a_ref, b_ref, add_ref, o_ref, acc_ref = refs
        else:
            a_ref, b_ref, o_ref, acc_ref = refs
        k = pl.program_id(2)

        @pl.when(k == 0)
        def _():
            acc_ref[...] = jnp.zeros_like(acc_ref)

        acc_ref[...] += lax.dot_general(a_ref[...].astype(BF16), b_ref[...].astype(BF16), dims,
                                        preferred_element_type=F32)

        @pl.when(k == nk - 1)
        def _():
            r = acc_ref[...]
            if has_add:
                r = r + add_ref[...]
            o_ref[...] = r.astype(o_ref.dtype)

    ins = [a, b] + ([add] if has_add else [])
    in_specs = [a_spec, b_spec] + ([o_spec] if has_add else [])
    return pl.pallas_call(
        body, name=name, grid=(M // tm, N // tn, nk), in_specs=in_specs, out_specs=o_spec, out_shape=out_shape,
        scratch_shapes=[pltpu.VMEM((tm, tn), F32)],
        compiler_params=_cparams(("parallel", "parallel", "arbitrary")),
    )(*ins)


def _rows(body, name, S, ts, row_ins, full_ins, row_outs, acc_outs=(), chunk_outs=()):
    in_specs = [pl.BlockSpec((ts, a.shape[1]), lambda i: (i, 0)) for a in row_ins]
    in_specs += [pl.BlockSpec(a.shape, lambda i, nd=a.ndim: (0,) * nd) for a in full_ins]
    out_specs = [pl.BlockSpec((ts, w), lambda i: (i, 0)) for (w, _) in row_outs]
    out_specs += [pl.BlockSpec(shape, lambda i, nd=len(shape): (0,) * nd) for (shape, _) in acc_outs]
    out_specs += [pl.BlockSpec((lead, 1, LANES, ts), lambda i: (0, i, 0, 0)) for (lead, _) in chunk_outs]
    out_shape = [jax.ShapeDtypeStruct((S, w), dt) for (w, dt) in row_outs]
    out_shape += [jax.ShapeDtypeStruct(shape, dt) for (shape, dt) in acc_outs]
    out_shape += [jax.ShapeDtypeStruct((lead, S // ts, LANES, ts), dt) for (lead, dt) in chunk_outs]

    def kbody(*refs):
        body(pl.program_id(0), *refs)

    return pl.pallas_call(
        kbody, name=name, grid=(S // ts,), in_specs=in_specs, out_specs=out_specs, out_shape=out_shape,
        compiler_params=_cparams(("arbitrary",)),
    )(*row_ins, *full_ins)


def _acc_add(step, ref, val):
    @pl.when(step == 0)
    def _():
        ref[...] = val

    @pl.when(step != 0)
    def _():
        ref[...] += val


def _rms_fwd(x, g, name):
    S, W = x.shape

    def body(step, x_ref, g_ref, h_ref):
        xv = x_ref[...]
        r = lax.rsqrt(jnp.mean(xv * xv, axis=-1, keepdims=True) + EPS)
        h_ref[...] = (xv * r * g_ref[...]).astype(BF16)

    return _rows(body, name, S, 512, [x], [g], [(W, BF16)])[0]


def _rms_bwd_math(xv, g, dy, width):
    r = lax.rsqrt(jnp.sum(xv * xv, axis=-1, keepdims=True) * (1.0 / width) + EPS)
    xn = xv * r
    dyg = dy * g
    dx = r * (dyg - xn * (jnp.sum(dyg * xn, axis=-1, keepdims=True) * (1.0 / width)))
    return dx, dy * xn


def _rms_bwd(x, g, dys, resid, out_dtype, name):
    S, W = x.shape
    nd = len(dys)
    has_res = resid is not None

    def body(step, *refs):
        x_ref = refs[0]
        dy_refs = refs[1:1 + nd]
        pos = 1 + nd
        res_ref = refs[pos] if has_res else None
        pos += int(has_res)
        g_ref, dx_ref, dg_ref = refs[pos], refs[pos + 1], refs[pos + 2]
        dy = dy_refs[0][...].astype(F32)
        for r_ in dy_refs[1:]:
            dy = dy + r_[...].astype(F32)
        dx, dgr = _rms_bwd_math(x_ref[...], g_ref[...], dy, W)
        if has_res:
            dx = dx + res_ref[...]
        dx_ref[...] = dx.astype(dx_ref.dtype)
        _acc_add(step, dg_ref, jnp.sum(dgr, axis=0, keepdims=True))

    row_ins = [x] + list(dys) + ([resid] if has_res else [])
    dx, dg = _rows(body, name, S, 512, row_ins, [g], [(W, out_dtype)], [((1, W), F32)])
    return dx, dg


def _rope_apply(xv, c, sa, sb):
    return xv * c + pltpu.roll(xv, 16, 1) * sa + pltpu.roll(xv, LANES - 16, 1) * sb


def _rope_transpose(dy, c, sa, sb):
    return dy * c + pltpu.roll(dy * sa, LANES - 16, 1) + pltpu.roll(dy * sb, 16, 1)


def _rope_tables(S):
    pos = jnp.arange(S, dtype=F32)
    inv_freq = jnp.exp(-math.log(ROPE_BASE) * jnp.arange(0, QK_ROPE, 2, dtype=F32) / QK_ROPE)
    ang = pos[:, None] * inv_freq[None, :]
    cos, sin = jnp.cos(ang), jnp.sin(ang)
    ones, zeros = jnp.ones((S, HEAD_DIM), F32), jnp.zeros((S, HEAD_DIM), F32)
    z16, z32 = jnp.zeros((S, 16), F32), jnp.zeros((S, 32), F32)
    c = jnp.concatenate([ones, cos, cos, z32], axis=1)
    sa = jnp.concatenate([zeros, z16, sin, z32], axis=1)
    sb = jnp.concatenate([zeros, -sin, z16, z32], axis=1)
    return c, sa, sb


def _mla_prep(proj_b, g_q, g_kv, tabs):
    S = proj_b.shape[0]

    def body(step, p_ref, c_ref, sa_ref, sb_ref, gq_ref, gkv_ref, cq_ref, ckv_ref, kr_ref):
        cq = p_ref[:, 0:Q_LORA]
        ckv = p_ref[:, Q_LORA:Q_LORA + KV_LORA]
        kr = p_ref[:, Q_LORA + KV_LORA:TAIL_P]
        rq = lax.rsqrt(jnp.mean(cq * cq, axis=-1, keepdims=True) + EPS)
        cq_ref[...] = (cq * rq * gq_ref[...]).astype(BF16)
        rk = lax.rsqrt(jnp.mean(ckv * ckv, axis=-1, keepdims=True) + EPS)
        ckv_ref[...] = (ckv * rk * gkv_ref[...]).astype(BF16)
        kr_ref[...] = _rope_apply(kr, c_ref[...], sa_ref[...], sb_ref[...])

    return _rows(body, "mla_prep", S, 512, [proj_b, *tabs], [g_q, g_kv],
                 [(Q_LORA, BF16), (KV_LORA, BF16), (LANES, F32)])


def _mla_qkv(q, kv, kr, tabs):
    S = q.shape[0]

    def body(step, q_ref, kv_ref, kr_ref, c_ref, sa_ref, sb_ref, qb_ref, kb_ref, vb_ref, kt_ref, vt_ref):
        c, sa, sb = c_ref[...], sa_ref[...], sb_ref[...]
        krv = kr_ref[...]
        row = lax.broadcasted_iota(jnp.int32, (LANES, MLA_TK), 0)
        for h in range(N_HEADS):
            blk = slice(h * LANES, (h + 1) * LANES)
            qb_ref[:, blk] = (_rope_apply(q_ref[:, blk], c, sa, sb) * MLA_SCALE).astype(BF16)
            kh = kv_ref[:, blk] + krv
            kb_ref[:, blk] = kh.astype(BF16)
            kt_ref[h, 0] = kh.T.astype(BF16)
            vh = kv_ref[:, WIDTH_P + h * LANES:WIDTH_P + (h + 1) * LANES]
            vt_ref[h, 0] = jnp.where(row == HEAD_DIM, 1.0, vh.T).astype(BF16)
        vb_ref[...] = kv_ref[:, WIDTH_P:2 * WIDTH_P].astype(BF16)

    return _rows(body, "mla_qkv", S, MLA_TK, [q, kv, kr, *tabs], [],
                 [(WIDTH_P, BF16), (WIDTH_P, BF16), (WIDTH_P, BF16)],
                 chunk_outs=[(N_HEADS, BF16), (N_HEADS, BF16)])


def _outnorm_fwd(oa, ob, ga, gb):
    S = oa.shape[0]

    def body(step, oa_ref, ob_ref, ga_ref, gb_ref, cat_ref):
        live = lax.broadcasted_iota(jnp.int32, oa_ref.shape, 1) % LANES < HEAD_DIM
        for o_ref, g_ref, off in ((oa_ref, ga_ref, 0), (ob_ref, gb_ref, WIDTH_P)):
            o = jnp.where(live, o_ref[...], 0.0)
            r = lax.rsqrt(jnp.sum(o * o, axis=-1, keepdims=True) * (1.0 / WIDTH) + EPS)
            cat_ref[:, off:off + WIDTH_P] = (o * r * g_ref[...]).astype(BF16)

    return _rows(body, "outnorm_fwd", S, 512, [oa, ob], [ga, gb], [(2 * WIDTH_P, BF16)])[0]


def _outnorm_bwd(oa, ob, ga, gb, dcat):
    S = oa.shape[0]

    def body(step, oa_ref, ob_ref, dcat_ref, ga_ref, gb_ref, dpa_ref, dob_ref, dga_ref, dgb_ref):
        live = lax.broadcasted_iota(jnp.int32, oa_ref.shape, 1) % LANES < HEAD_DIM
        lane = lax.broadcasted_iota(jnp.int32, (oa_ref.shape[0], LANES), 1)
        packed = oa_ref[...]
        o = jnp.where(live, packed, 0.0)
        do, dgr = _rms_bwd_math(o, ga_ref[...], dcat_ref[:, 0:WIDTH_P], WIDTH)
        _acc_add(step, dga_ref, jnp.sum(dgr, axis=0, keepdims=True))
        prod = do.astype(BF16).astype(F32) * o
        for h in range(N_HEADS):
            blk = slice(h * LANES, (h + 1) * LANES)
            delta = jnp.sum(prod[:, blk], axis=-1, keepdims=True)
            lse = jnp.sum(jnp.where(lane == HEAD_DIM, packed[:, blk], 0.0), axis=-1, keepdims=True)
            out = do[:, blk]
            for k, piece in enumerate(_split3(-delta) + _split3(-lse)):
                out = jnp.where(lane == HEAD_DIM + k, piece, out)
            dpa_ref[:, blk] = out

        ov = ob_ref[...]
        do_b, dgr_b = _rms_bwd_math(ov, gb_ref[...], dcat_ref[:, WIDTH_P:2 * WIDTH_P], WIDTH)
        dob_ref[...] = do_b.astype(BF16)
        _acc_add(step, dgb_ref, jnp.sum(dgr_b, axis=0, keepdims=True))

    return _rows(body, "outnorm_bwd", S, 256, [oa, ob, dcat], [ga, gb],
                 [(WIDTH_P, F32), (WIDTH_P, BF16)], [((1, WIDTH_P), F32), ((1, WIDTH_P), F32)])


def _post_mix(x, y, g_post, g_pre):
    S, W = x.shape

    def body(step, x_ref, y_ref, gp_ref, gq_ref, x1_ref, h_ref):
        yv = y_ref[...]
        r = lax.rsqrt(jnp.mean(yv * yv, axis=-1, keepdims=True) + EPS)
        x1 = x_ref[...] + yv * r * gp_ref[...]
        x1_ref[...] = x1
        r1 = lax.rsqrt(jnp.mean(x1 * x1, axis=-1, keepdims=True) + EPS)
        h_ref[...] = (x1 * r1 * gq_ref[...]).astype(BF16)

    return _rows(body, "post_mix", S, 512, [x, y], [g_post, g_pre], [(W, F32), (W, BF16)])


def _final(x1, y2, g, target):
    S, W = x1.shape
    ts = 512
    nsteps = S // ts

    def body(step, x1_ref, y_ref, t_ref, g_ref, dx2_ref, dy_ref, dg_ref, sq_ref, loss_ref):
        yv = y_ref[...]
        gv = g_ref[...]
        r = lax.rsqrt(jnp.mean(yv * yv, axis=-1, keepdims=True) + EPS)
        yn = yv * r
        err = (x1_ref[...] + yn * gv) - t_ref[...]
        dx2 = err * (1.0 / W)
        dx2_ref[...] = dx2
        dyg = dx2 * gv
        dy = r * (dyg - yn * jnp.mean(dyg * yn, axis=-1, keepdims=True))
        dy_ref[...] = dy.astype(BF16)
        _acc_add(step, dg_ref, jnp.sum(dx2 * yn, axis=0, keepdims=True))
        _acc_add(step, sq_ref, jnp.sum(err * err, axis=0, keepdims=True))

        @pl.when(step == nsteps - 1)
        def _():
            tot = jnp.sum(sq_ref[...], axis=-1, keepdims=True) * (0.5 / W)
            loss_ref[...] = jnp.broadcast_to(tot, (1, LANES))

    return _rows(body, "final_loss", S, ts, [x1, y2, target], [g], [(W, F32), (W, BF16)],
                 [((1, W), F32), ((1, W), F32), ((1, LANES), F32)])


_GELU_C = math.sqrt(2.0 / math.pi)
_CONV_CHUNK = 128
_HALO = 8


def _gelu(g):
    t = jnp.tanh(_GELU_C * (g + 0.044715 * (g * g * g)))
    return g * (0.5 * (1.0 + t)), t


def _fill_padded(pad_ref, src_ref, S):
    zeros = jnp.zeros((_HALO, LANES), F32)
    pad_ref[0:_HALO, :] = zeros
    pad_ref[_HALO + S:2 * _HALO + S, :] = zeros
    for r0 in range(0, S, _CONV_CHUNK):
        pad_ref[_HALO + r0:_HALO + r0 + _CONV_CHUNK, :] = src_ref[r0:r0 + _CONV_CHUNK, :].astype(F32)


def _conv_fwd(u0, conv_w, conv_b):
    S, C2 = u0.shape
    nb = D_FF // LANES

    def body(u0g_ref, u0v_ref, wg_ref, wv_ref, bg_ref, bv_ref, ug_ref, uv_ref, a_ref, pg_ref, pv_ref):
        _fill_padded(pg_ref, u0g_ref, S)
        _fill_padded(pv_ref, u0v_ref, S)
        wg, wv = wg_ref[...], wv_ref[...]
        for r0 in range(0, S, _CONV_CHUNK):
            def conv(p_ref, w, b_ref):
                base = _HALO + r0
                return (p_ref[base - 1:base - 1 + _CONV_CHUNK, :] * w[0:1, :]
                        + p_ref[base:base + _CONV_CHUNK, :] * w[1:2, :]
                        + p_ref[base + 1:base + 1 + _CONV_CHUNK, :] * w[2:3, :] + b_ref[...])
            g = conv(pg_ref, wg, bg_ref)
            v = conv(pv_ref, wv, bv_ref)
            rows = slice(r0, r0 + _CONV_CHUNK)
            ug_ref[rows, :] = g
            uv_ref[rows, :] = v
            a_ref[rows, :] = (_gelu(g)[0] * v).astype(BF16)

    col = lambda off: pl.BlockSpec((S, LANES), lambda j: (0, j + off))
    wcol = lambda off: pl.BlockSpec((3, LANES), lambda j: (0, j + off))
    bcol = lambda off: pl.BlockSpec((1, LANES), lambda j: (0, j + off))
    ug, uv, a = pl.pallas_call(
        body, name="conv_gelu_fwd", grid=(nb,),
        in_specs=[col(0), col(nb), wcol(0), wcol(nb), bcol(0), bcol(nb)],
        out_specs=[col(0), col(0), col(0)],
        out_shape=[jax.ShapeDtypeStruct((S, D_FF), F32), jax.ShapeDtypeStruct((S, D_FF), F32),
                   jax.ShapeDtypeStruct((S, D_FF), BF16)],
        scratch_shapes=[pltpu.VMEM((S + 2 * _HALO, LANES), F32), pltpu.VMEM((S + 2 * _HALO, LANES), F32)],
        compiler_params=_cparams(("arbitrary",)),
    )(u0, u0, conv_w, conv_w, conv_b, conv_b)
    return ug, uv, a


def _conv_bwd(u0, ug, uv, da, conv_w):
    S = u0.shape[0]
    nb = D_FF // LANES

    def body(u0_ref, ug_ref, uv_ref, da_ref, w_ref, du0_ref, dw_ref, db_ref, pu_ref, pd_ref):
        is_g = pl.program_id(1) == 0
        _fill_padded(pu_ref, u0_ref, S)
        zeros = jnp.zeros((_HALO, LANES), F32)
        pd_ref[0:_HALO, :] = zeros
        pd_ref[_HALO + S:2 * _HALO + S, :] = zeros
        @pl.when(is_g)
        def _():
            for r0 in range(0, S, _CONV_CHUNK):
                rows = slice(r0, r0 + _CONV_CHUNK)
                g = ug_ref[rows, :]
                t = _gelu(g)[1]
                dgel = 0.5 * (1.0 + t) + (0.5 * g) * (1.0 - t * t) * (_GELU_C * (1.0 + 3.0 * 0.044715 * (g * g)))
                pd_ref[_HALO + r0:_HALO + r0 + _CONV_CHUNK, :] = da_ref[rows, :] * uv_ref[rows, :] * dgel

        @pl.when(jnp.logical_not(is_g))
        def _():
            for r0 in range(0, S, _CONV_CHUNK):
                rows = slice(r0, r0 + _CONV_CHUNK)
                pd_ref[_HALO + r0:_HALO + r0 + _CONV_CHUNK, :] = da_ref[rows, :] * _gelu(ug_ref[rows, :])[0]
        w = w_ref[...]
        acc_b = jnp.zeros((1, LANES), F32)
        acc_w = [jnp.zeros((1, LANES), F32) for _ in range(3)]
        for r0 in range(0, S, _CONV_CHUNK):
            base = _HALO + r0
            du_m = pd_ref[base - 1:base - 1 + _CONV_CHUNK, :]
            du_c = pd_ref[base:base + _CONV_CHUNK, :]
            du_p = pd_ref[base + 1:base + 1 + _CONV_CHUNK, :]
            du0_ref[r0:r0 + _CONV_CHUNK, :] = (du_p * w[0:1, :] + du_c * w[1:2, :] + du_m * w[2:3, :]).astype(BF16)
            acc_b = acc_b + jnp.sum(du_c, axis=0, keepdims=True)
            for k in range(3):
                acc_w[k] = acc_w[k] + jnp.sum(du_c * pu_ref[base + k - 1:base + k - 1 + _CONV_CHUNK, :],
                                              axis=0, keepdims=True)
        db_ref[...] = acc_b
        for k in range(3):
            dw_ref[k:k + 1, :] = acc_w[k]

    own = pl.BlockSpec((S, LANES), lambda j, half: (0, half * nb + j))
    shared = pl.BlockSpec((S, LANES), lambda j, half: (0, j))
    du0, dw, db = pl.pallas_call(
        body, name="conv_gelu_bwd", grid=(nb, 2),
        in_specs=[own, shared, shared, shared, pl.BlockSpec((3, LANES), lambda j, half: (0, half * nb + j))],
        out_specs=[own, pl.BlockSpec((3, LANES), lambda j, half: (0, half * nb + j)),
                   pl.BlockSpec((1, LANES), lambda j, half: (0, half * nb + j))],
        out_shape=[jax.ShapeDtypeStruct((S, 2 * D_FF), BF16), jax.ShapeDtypeStruct((3, 2 * D_FF), F32),
                   jax.ShapeDtypeStruct((1, 2 * D_FF), F32)],
        scratch_shapes=[pltpu.VMEM((S + 2 * _HALO, LANES), F32), pltpu.VMEM((S + 2 * _HALO, LANES), F32)],
        compiler_params=_cparams(("arbitrary", "arbitrary")),
    )(u0, ug, uv, da, conv_w)
    return du0, dw, db


DIL_HALF = 64
DIL_Q_FWD = 128
DIL_Q_BWD = 256
DILATIONS = (1, 4, 16)


def _lanes_hi_to_all(x):
    lane = lax.broadcasted_iota(jnp.int32, x.shape, 1)
    return jnp.where(lane < HEAD_DIM, pltpu.roll(x, HEAD_DIM, 1), x)


_NT = (((1,), (1,)), ((), ()))
_TN = (((0,), (0,)), ((), ()))
HEAD_COLS = 3 * LANES


def _slab_bias(q):
    row = jnp.arange(q, dtype=jnp.int32)[:, None]
    col = jnp.arange(q + 2 * DIL_HALF, dtype=jnp.int32)[None, :]
    slopes = jnp.exp2(-8.0 * jnp.arange(1, N_HEADS + 1, dtype=F32) / N_HEADS)
    out = []
    for r in DILATIONS:
        variants = []
        for shift in (DIL_HALF, 0, 2 * DIL_HALF):
            ad = jnp.abs(col - shift - row)
            variants.append(jnp.where(ad <= DIL_HALF, -slopes[:, None, None] * (ad * r).astype(F32)[None], MASKED))
        out.append(jnp.stack(variants, axis=1))
    return jnp.stack(out, axis=0)


DIL_CHUNK = 512


def _block_geometry(i, nblk, q):
    first = pl.multiple_of(i * q, q)
    slab0 = pl.multiple_of(jnp.clip(i * q - DIL_HALF, 0, (nblk - 1) * q - 2 * DIL_HALF), DIL_HALF)
    variant = jnp.where(i == 0, 1, jnp.where(i == nblk - 1, 2, 0))
    return first, slab0, variant


def _class_rows(c, r, r0, n):
    return pl.ds(c + r0 * r, n, stride=r) if r > 1 else pl.ds(r0, n)


def _dila_fwd(proj_a, bias, exchange=None):
    S = proj_a.shape[0]
    lmax = S // DILATIONS[1]
    DIL_Q, DIL_SLAB = DIL_Q_FWD, DIL_Q_FWD + 2 * DIL_HALF

    def compute(qh_ref, kh_ref, vh_ref, b_ref, o_ref, q_s, k_s, v_s, cm_s):
        lane = lax.broadcasted_iota(jnp.int32, (DIL_Q, LANES), 1)
        lane_s = lax.broadcasted_iota(jnp.int32, (DIL_SLAB, LANES), 1)
        lane_c = lax.broadcasted_iota(jnp.int32, (DIL_CHUNK, LANES), 1)

        def run(g, nblk, load_q, load_k, load_v, store):
            def block(i, carry):
                first, slab0, variant = _block_geometry(i, nblk, DIL_Q)
                qv, ks, vs = load_q(first), load_k(slab0), load_v(slab0)
                s = lax.dot_general(qv, ks, _NT, preferred_element_type=F32) + b_ref[g, 0, variant]
                m = jnp.max(s, axis=-1, keepdims=True)
                acc = jnp.dot(jnp.exp(s - m).astype(BF16), vs, preferred_element_type=F32)
                l = _lanes_hi_to_all(acc)
                store(first, jnp.where(lane < HEAD_DIM, acc / l, m + jnp.log(l)))
                return carry

            lax.fori_loop(0, nblk, block, 0, unroll=min(8, nblk))

        def direct_store(first, val):
            o_ref[pl.ds(first, DIL_Q), :] = val

        run(0, S // DIL_Q,
            lambda f: (qh_ref[pl.ds(f, DIL_Q), :] * DIL_SCALE).astype(BF16),
            lambda s0: kh_ref[pl.ds(s0, DIL_SLAB), :].astype(BF16),
            lambda s0: jnp.where(lane_s < HEAD_DIM, vh_ref[pl.ds(s0, DIL_SLAB), :], 1.0).astype(BF16),
            direct_store)

        def cm_store(first, val):
            cm_s[pl.ds(first, DIL_Q), :] = val

        for g, r in list(enumerate(DILATIONS))[1:]:
            L = S // r
            n = min(L, DIL_CHUNK)
            for c in range(r):
                for r0 in range(0, L, n):
                    src = _class_rows(c, r, r0, n)
                    q_s[r0:r0 + n, :] = (qh_ref[src, :] * DIL_SCALE).astype(BF16)
                    k_s[r0:r0 + n, :] = kh_ref[src, :].astype(BF16)
                    v_s[r0:r0 + n, :] = jnp.where(lane_c[:n] < HEAD_DIM, vh_ref[src, :], 1.0).astype(BF16)
                run(g, L // DIL_Q, lambda f: q_s[pl.ds(f, DIL_Q), :], lambda s0: k_s[pl.ds(s0, DIL_SLAB), :],
                    lambda s0: v_s[pl.ds(s0, DIL_SLAB), :], cm_store)
                for r0 in range(0, L, n):
                    dst = _class_rows(c, r, r0, n)
                    a, b = cm_s[r0:r0 + n, :], o_ref[dst, :]
                    la, lb = _lanes_hi_to_all(a), _lanes_hi_to_all(b)
                    m = jnp.maximum(la, lb)
                    wa, wb = jnp.exp(la - m), jnp.exp(lb - m)
                    tot = wa + wb
                    o_ref[dst, :] = jnp.where(lane_c[:n] < HEAD_DIM, (wa * a + wb * b) / tot, m + jnp.log(tot))

    ex = exchange
    n = ex.n if ex else 0

    def body(*refs):
        ins, ex_in, o_ref, ex_out = refs[:4], refs[4:4 + n], refs[4 + n], refs[5 + n:5 + 2 * n]
        scratch, sems = refs[5 + 2 * n:9 + 2 * n], refs[9 + 2 * n:]
        if ex:
            pl.when(pl.program_id(0) == 0)(lambda: ex.start(ex_in, ex_out, sems))
        compute(*ins, o_ref, *scratch)
        if ex:
            pl.when(pl.program_id(0) == N_HEADS - 1)(lambda: ex.finish(ex_in, ex_out, sems))

    out = pl.pallas_call(
        body, name="dil_fwd", grid=(N_HEADS,),
        in_specs=[pl.BlockSpec((S, LANES), lambda h: (0, 3 * h)), pl.BlockSpec((S, LANES), lambda h: (0, 3 * h + 1)),
                  pl.BlockSpec((S, LANES), lambda h: (0, 3 * h + 2)),
                  pl.BlockSpec((len(DILATIONS), 1, 3, DIL_Q, DIL_SLAB), lambda h: (0, h, 0, 0, 0))] + [ANY] * n,
        out_specs=[pl.BlockSpec((S, LANES), lambda h: (0, h))] + [ANY] * n,
        out_shape=[jax.ShapeDtypeStruct((S, WIDTH_P), F32)] + (ex.out_shapes() if ex else []),
        scratch_shapes=[pltpu.VMEM((lmax, LANES), BF16), pltpu.VMEM((lmax, LANES), BF16),
                        pltpu.VMEM((lmax, LANES), BF16), pltpu.VMEM((lmax, LANES), F32)] + (ex.scratch() if ex else []),
        compiler_params=_cparams(("arbitrary",)),
    )(proj_a, proj_a, proj_a, bias, *(ex.arrays if ex else []))
    return out[0], list(out[1:])


N_SPLIT = 3


def _split3(x):
    hi = x.astype(BF16).astype(F32)
    mid = (x - hi).astype(BF16).astype(F32)
    lo = (x - hi - mid).astype(BF16).astype(F32)
    return hi, mid, lo


def _dila_bwd(proj_a, dopack, bias):
    S = proj_a.shape[0]
    lmax = S // DILATIONS[1]
    DIL_Q, DIL_SLAB = DIL_Q_BWD, DIL_Q_BWD + 2 * DIL_HALF

    def body(qh_ref, kh_ref, vh_ref, d_ref, b_ref, out_ref, dq_ref, dk_ref, dv_ref, q_s, k_s, v_s, do_s,
             dq_c, dk_c, dv_c):
        def scalar_lanes(shape):
            lane = lax.broadcasted_iota(jnp.int32, shape, 1)
            return lane, (lane >= HEAD_DIM) & (lane < HEAD_DIM + N_SPLIT)

        def q_side(q, x):
            lane, ones = scalar_lanes(x.shape)
            lse_parts = pltpu.roll(x, LANES - N_SPLIT, 1)
            qv = jnp.where(lane < HEAD_DIM, q * DIL_SCALE, jnp.where(ones, lse_parts, 0.0)).astype(BF16)
            return qv, jnp.where(lane < HEAD_DIM + N_SPLIT, x, 0.0).astype(BF16)

        def kv_side(k, v):
            _, ones = scalar_lanes(k.shape)
            return jnp.where(ones, 1.0, k).astype(BF16), jnp.where(ones, 1.0, v).astype(BF16)

        def run(g, nblk, load_q, load_kv, dq_o, dk_o, dv_o):
            def block(i, carry):
                first, slab0, variant = _block_geometry(i, nblk, DIL_Q)
                rows, slab = pl.ds(first, DIL_Q), pl.ds(slab0, DIL_SLAB)
                (qv, dov), (ks, vs) = load_q(rows), load_kv(slab)
                p = jnp.exp(lax.dot_general(qv, ks, _NT, preferred_element_type=F32) + b_ref[g, 0, variant])
                ds = (p * lax.dot_general(dov, vs, _NT, preferred_element_type=F32)).astype(BF16)
                dq_o[rows, :] = jnp.dot(ds, ks, preferred_element_type=F32) * DIL_SCALE
                dk_o[slab, :] += lax.dot_general(ds, qv, _TN, preferred_element_type=F32)
                dv_o[slab, :] += lax.dot_general(p.astype(BF16), dov, _TN, preferred_element_type=F32)
                return carry

            lax.fori_loop(0, nblk, block, 0, unroll=min(8, nblk))

        dk_ref[...] = jnp.zeros_like(dk_ref)
        dv_ref[...] = jnp.zeros_like(dv_ref)
        run(0, S // DIL_Q,
            lambda rows: q_side(qh_ref[rows, :], d_ref[rows, :]),
            lambda slab: kv_side(kh_ref[slab, :], vh_ref[slab, :]),
            dq_ref, dk_ref, dv_ref)

        for g, r in list(enumerate(DILATIONS))[1:]:
            L = S // r
            n = min(L, DIL_CHUNK)
            for c in range(r):
                for r0 in range(0, L, n):
                    src = _class_rows(c, r, r0, n)
                    q_s[r0:r0 + n, :], do_s[r0:r0 + n, :] = q_side(qh_ref[src, :], d_ref[src, :])
                    k_s[r0:r0 + n, :], v_s[r0:r0 + n, :] = kv_side(kh_ref[src, :], vh_ref[src, :])
                    dk_c[r0:r0 + n, :] = jnp.zeros((n, LANES), F32)
                    dv_c[r0:r0 + n, :] = jnp.zeros((n, LANES), F32)
                run(g, L // DIL_Q, lambda rows: (q_s[rows, :], do_s[rows, :]),
                    lambda slab: (k_s[slab, :], v_s[slab, :]), dq_c, dk_c, dv_c)
                for r0 in range(0, L, n):
                    dst = _class_rows(c, r, r0, n)
                    for acc, cls in ((dq_ref, dq_c), (dk_ref, dk_c), (dv_ref, dv_c)):
                        acc[dst, :] += cls[r0:r0 + n, :]

        for r0 in range(0, S, DIL_CHUNK):
            for part, ref in enumerate((dq_ref, dk_ref, dv_ref)):
                out_ref[r0:r0 + DIL_CHUNK, part * LANES:(part + 1) * LANES] = ref[r0:r0 + DIL_CHUNK, :].astype(BF16)

    bf = lambda rows: pltpu.VMEM((rows, LANES), BF16)
    f32 = lambda rows: pltpu.VMEM((rows, LANES), F32)
    return pl.pallas_call(
        body, name="dil_bwd", grid=(N_HEADS,),
        in_specs=[pl.BlockSpec((S, LANES), lambda h: (0, 3 * h), pipeline_mode=pl.Buffered(1)),
                  pl.BlockSpec((S, LANES), lambda h: (0, 3 * h + 1), pipeline_mode=pl.Buffered(1)),
                  pl.BlockSpec((S, LANES), lambda h: (0, 3 * h + 2), pipeline_mode=pl.Buffered(1)),
                  pl.BlockSpec((S, LANES), lambda h: (0, h), pipeline_mode=pl.Buffered(1)),
                  pl.BlockSpec((len(DILATIONS), 1, 3, DIL_Q, DIL_SLAB), lambda h: (0, h, 0, 0, 0))],
        out_specs=pl.BlockSpec((S, HEAD_COLS), lambda h: (0, h)),
        out_shape=jax.ShapeDtypeStruct((S, N_HEADS * HEAD_COLS), BF16),
        scratch_shapes=[f32(S), f32(S), f32(S), bf(lmax), bf(lmax), bf(lmax), bf(lmax),
                        f32(lmax), f32(lmax), f32(lmax)],
        compiler_params=_cparams(("arbitrary",)),
    )(proj_a, proj_a, proj_a, dopack, bias)


def _mla_fwd(q, k, vt):
    S = q.shape[0]
    tq, tk = MLA_TQ, MLA_TK
    nq, nk = S // tq, S // tk

    def body(q_ref, k_ref, vt_ref, o_ref, lse_ref, acc_ref):
        qv = q_ref[...]
        acc_ref[...] = jnp.zeros_like(acc_ref)

        def pair(j, m):
            sts = []
            for c in (2 * j, 2 * j + 1):
                kc = k_ref[pl.ds(pl.multiple_of(c * tk, tk), tk), :]
                sts.append(lax.dot_general(kc, qv, _NT, preferred_element_type=F32))
            m_new = jnp.maximum(m, jnp.maximum(jnp.max(sts[0], axis=0, keepdims=True),
                                               jnp.max(sts[1], axis=0, keepdims=True)))
            upd = [jnp.dot(vt_ref[0, c, 0:V_ROWS, :], jnp.exp(st - m_new).astype(BF16), preferred_element_type=F32)
                   for c, st in zip((2 * j, 2 * j + 1), sts)]
            acc_ref[...] = jnp.exp(m - m_new) * acc_ref[...] + (upd[0] + upd[1])
            return m_new

        m = lax.fori_loop(0, nk // 2, pair, jnp.full((1, tq), M_INIT, F32))
        acc = acc_ref[...]
        l = acc[HEAD_DIM:HEAD_DIM + 1, :]
        ot = jnp.concatenate([acc[0:HEAD_DIM, :] / l, jnp.zeros((LANES - HEAD_DIM, tq), F32)], axis=0)
        o_ref[...] = ot.T
        lse_ref[0] = m + jnp.log(l)

    return pl.pallas_call(
        body, name="mla_fwd", grid=(N_HEADS, nq),
        in_specs=[pl.BlockSpec((tq, LANES), lambda h, i: (i, h)),
                  pl.BlockSpec((S, LANES), lambda h, i: (0, h)),
                  pl.BlockSpec((1, nk, LANES, tk), lambda h, i: (h, 0, 0, 0))],
        out_specs=[pl.BlockSpec((tq, LANES), lambda h, i: (i, h)),
                   pl.BlockSpec((1, 1, tq), lambda h, i: (h, 0, i))],
        out_shape=[jax.ShapeDtypeStruct((S, WIDTH_P), F32), jax.ShapeDtypeStruct((N_HEADS, 1, S), F32)],
        scratch_shapes=[pltpu.VMEM((V_ROWS, tq), F32)],
        compiler_params=_cparams(("parallel", "parallel")),
    )(q, k, vt)


def _mla_bwd(q, k, v, kt, do, o, lse, exchange=None):
    S = q.shape[0]
    tq, tk = MLA_TQ, MLA_TK
    nq, nk = S // tq, S // tk

    def compute(q_ref, do_ref, o_ref, lse_ref, k_ref, v_ref, kt_ref, dq_ref, dk_ref, dv_ref, dqt_ref):
        @pl.when(pl.program_id(1) == 0)
        def _():
            dk_ref[...] = jnp.zeros_like(dk_ref)
            dv_ref[...] = jnp.zeros_like(dv_ref)

        qv, dov = q_ref[...], do_ref[...]
        delta = jnp.sum((dov.astype(F32) * o_ref[...]).T, axis=0, keepdims=True)
        lse = lse_ref[0]
        dqt_ref[...] = jnp.zeros_like(dqt_ref)

        def chunk(c, carry):
            rows = pl.ds(pl.multiple_of(c * tk, tk), tk)
            kc, vc = k_ref[rows, :], v_ref[rows, :]
            pt = jnp.exp(lax.dot_general(kc, qv, _NT, preferred_element_type=F32) - lse)
            dv_ref[rows, :] += jnp.dot(pt.astype(BF16), dov, preferred_element_type=F32)
            dpt = lax.dot_general(vc, dov, _NT, preferred_element_type=F32)
            dst = (pt * (dpt - delta)).astype(BF16)
            dk_ref[rows, :] += jnp.dot(dst, qv, preferred_element_type=F32)
            dqt_ref[...] += jnp.dot(kt_ref[0, c], dst, preferred_element_type=F32)
            return carry

        lax.fori_loop(0, nk, chunk, 0, unroll=2)
        dq_ref[...] = (dqt_ref[...] * MLA_SCALE).T

    ex = exchange
    n = ex.n if ex else 0

    def body(*refs):
        ins, ex_in, outs, ex_out = refs[:7], refs[7:7 + n], refs[7 + n:10 + n], refs[10 + n:10 + 2 * n]
        dqt_ref, sems = refs[10 + 2 * n], refs[11 + 2 * n:]
        first = (pl.program_id(0) == 0) & (pl.program_id(1) == 0)
        last = (pl.program_id(0) == N_HEADS - 1) & (pl.program_id(1) == nq - 1)
        if ex:
            pl.when(first)(lambda: ex.start(ex_in, ex_out, sems))
        compute(*ins, *outs, dqt_ref)
        if ex:
            pl.when(last)(lambda: ex.finish(ex_in, ex_out, sems))

    qspec = pl.BlockSpec((tq, LANES), lambda h, i: (i, h))
    kspec = pl.BlockSpec((S, LANES), lambda h, i: (0, h))
    out = jax.ShapeDtypeStruct((S, WIDTH_P), F32)
    res = pl.pallas_call(
        body, name="mla_bwd", grid=(N_HEADS, nq),
        in_specs=[qspec, qspec, qspec, pl.BlockSpec((1, 1, tq), lambda h, i: (h, 0, i)), kspec, kspec,
                  pl.BlockSpec((1, nk, LANES, tk), lambda h, i: (h, 0, 0, 0))] + [ANY] * n,
        out_specs=[qspec, kspec, kspec] + [ANY] * n, out_shape=[out, out, out] + (ex.out_shapes() if ex else []),
        scratch_shapes=[pltpu.VMEM((LANES, tq), F32)] + (ex.scratch() if ex else []),
        compiler_params=_cparams(("arbitrary", "arbitrary")),
    )(q, do, o, lse, k, v, kt, *(ex.arrays if ex else []))
    return res[0], res[1], res[2], list(res[3:])


def _mla_bwd_prep(dq, dk, dv, tabs):
    S = dq.shape[0]

    def body(step, dq_ref, dk_ref, dv_ref, c_ref, sa_ref, sb_ref, dqp_ref, dkv_ref, dkr_ref):
        c, sa, sb = c_ref[...], sa_ref[...], sb_ref[...]
        dksum = jnp.zeros((dq_ref.shape[0], LANES), F32)
        for h in range(N_HEADS):
            blk = slice(h * LANES, (h + 1) * LANES)
            dqp_ref[:, blk] = _rope_transpose(dq_ref[:, blk], c, sa, sb).astype(BF16)
            dksum = dksum + dk_ref[:, blk]
        dkv_ref[:, 0:WIDTH_P] = dk_ref[...].astype(BF16)
        dkv_ref[:, WIDTH_P:2 * WIDTH_P] = dv_ref[...].astype(BF16)
        lane = lax.broadcasted_iota(jnp.int32, dksum.shape, 1)
        live = (lane >= HEAD_DIM) & (lane < HEAD_DIM + QK_ROPE)
        dkr_ref[...] = jnp.where(live, _rope_transpose(dksum, c, sa, sb), 0.0)

    return _rows(body, "mla_bwd_prep", S, 256, [dq, dk, dv, *tabs], [],
                 [(WIDTH_P, BF16), (2 * WIDTH_P, BF16), (LANES, F32)])


def _mla_norm_bwd(proj_b, dcq_n, dckv_n, dkr, g_q, g_kv):
    S = proj_b.shape[0]

    def body(step, p_ref, dcq_ref, dckv_ref, dkr_ref, gq_ref, gkv_ref, dp_ref, dgq_ref, dgkv_ref):
        dcq, dgq = _rms_bwd_math(p_ref[:, 0:Q_LORA], gq_ref[...], dcq_ref[...], Q_LORA)
        dckv, dgkv = _rms_bwd_math(p_ref[:, Q_LORA:Q_LORA + KV_LORA], gkv_ref[...], dckv_ref[...], KV_LORA)
        dp_ref[:, 0:Q_LORA] = dcq.astype(BF16)
        dp_ref[:, Q_LORA:Q_LORA + KV_LORA] = dckv.astype(BF16)
        dp_ref[:, Q_LORA + KV_LORA:TAIL_P] = dkr_ref[...].astype(BF16)
        _acc_add(step, dgq_ref, jnp.sum(dgq, axis=0, keepdims=True))
        _acc_add(step, dgkv_ref, jnp.sum(dgkv, axis=0, keepdims=True))

    return _rows(body, "mla_norm_bwd", S, 512, [proj_b, dcq_n, dckv_n, dkr], [g_q, g_kv], [(TAIL_P, BF16)],
                 [((1, Q_LORA), F32), ((1, KV_LORA), F32)])


def _pad_cols(w, d):
    lead = w.shape[:-1]
    w = w.reshape(lead + (N_HEADS, d))
    w = jnp.pad(w, [(0, 0)] * len(lead) + [(0, 0), (0, LANES - d)])
    return w.reshape(lead + (N_HEADS * LANES,))


def _unpad_cols(w, d):
    lead = w.shape[:-1]
    return w.reshape(lead + (N_HEADS, LANES))[..., :d].reshape(lead + (N_HEADS * d,))


def _pad_w_in(w_in):
    zeros = lambda n: jnp.zeros((D_MODEL, n), w_in.dtype)
    p = {}
    parts = [_pad_cols(w_in[:, i * WIDTH:(i + 1) * WIDTH], HEAD_DIM).reshape(D_MODEL, N_HEADS, 1, LANES)
             for i in range(3)]
    p['w_in_a'] = jnp.concatenate(parts, axis=2).reshape(D_MODEL, N_HEADS * HEAD_COLS)
    p['w_in_b'] = jnp.concatenate([w_in[:, 3 * WIDTH:3 * WIDTH + Q_LORA + KV_LORA], zeros(HEAD_DIM),
                                   w_in[:, D_IN - QK_ROPE:], zeros(LANES - HEAD_DIM - QK_ROPE)], axis=1)
    return p


def _pad_weights(w):
    p = {}
    p['w_uq'] = _pad_cols(w['w_uq'], HEAD_DIM + QK_ROPE)
    kv = w['w_ukv'].reshape(KV_LORA, N_HEADS, 2 * HEAD_DIM)
    p['w_ukv'] = jnp.concatenate([_pad_cols(kv[:, :, :HEAD_DIM].reshape(KV_LORA, WIDTH), HEAD_DIM),
                                  _pad_cols(kv[:, :, HEAD_DIM:].reshape(KV_LORA, WIDTH), HEAD_DIM)], axis=1)
    p['w_o'] = jnp.concatenate(
        [_pad_cols(w['w_o'][i * WIDTH:(i + 1) * WIDTH].T, HEAD_DIM).T for i in range(2)], axis=0)
    p['g_a'] = _pad_cols(w['out_norm_a'], HEAD_DIM)
    p['g_b'] = _pad_cols(w['out_norm_b'], HEAD_DIM)
    return p


def _unpad_w_o(dwo):
    return jnp.concatenate([_unpad_cols(dwo[i * WIDTH_P:(i + 1) * WIDTH_P].T, HEAD_DIM).T for i in range(2)], axis=0)


def _unpad_grads(d):
    g = {}
    dwa = d['w_in_a'].reshape(D_MODEL, N_HEADS, 3, LANES)
    tail = d['w_in_b']
    g['w_in'] = jnp.concatenate(
        [dwa[:, :, i, :HEAD_DIM].reshape(D_MODEL, WIDTH) for i in range(3)]
        + [tail[:, :Q_LORA + KV_LORA], tail[:, Q_LORA + KV_LORA + HEAD_DIM:Q_LORA + KV_LORA + HEAD_DIM + QK_ROPE]],
        axis=1)
    g['w_uq'] = _unpad_cols(d['w_uq'], HEAD_DIM + QK_ROPE)
    dk = _unpad_cols(d['w_ukv'][:, :WIDTH_P], HEAD_DIM).reshape(KV_LORA, N_HEADS, HEAD_DIM)
    dv = _unpad_cols(d['w_ukv'][:, WIDTH_P:], HEAD_DIM).reshape(KV_LORA, N_HEADS, HEAD_DIM)
    g['w_ukv'] = jnp.concatenate([dk, dv], axis=2).reshape(KV_LORA, 2 * WIDTH)
    g['out_norm_a'] = _unpad_cols(d['g_a'], HEAD_DIM)
    g['out_norm_b'] = _unpad_cols(d['g_b'], HEAD_DIM)
    return g


LATE = ['w_uq', 'w_ukv', 'w_o', 'w_up', 'w_down']
EARLY_GRADS = ['w_up', 'w_down', 'w_o']


def _assemble_late(gathered):
    g = dict(zip(LATE, gathered))
    return {'w_uq': jnp.concatenate([g['w_uq'][i] for i in range(N_CHIPS)], axis=1),
            'w_ukv': jnp.concatenate([g['w_ukv'][i] for i in range(N_CHIPS)], axis=1),
            'w_o': g['w_o'].reshape(D_MODEL, D_MODEL),
            'w_down': g['w_down'].reshape(D_FF, D_MODEL),
            'w_up': g['w_up']}


def _local_step(x, target, w, late_shards=None):
    S = x.shape[0]
    w = dict(w)
    p = _pad_w_in(w['w_in'])
    tabs = _rope_tables(S)
    bias = _slab_bias(DIL_Q_FWD)

    h1 = _rms_fwd(x, w['norm_mix_pre'], "rms_mix_pre")
    proj_a = _mm(h1, p['w_in_a'], 'nn', F32, "mm_in_a")
    proj_b = _mm(h1, p['w_in_b'], 'nn', F32, "mm_in_b")
    gather = _Exchange([late_shards[n] for n in LATE], False) if late_shards else None
    oa, gathered = _dila_fwd(proj_a, bias, gather)
    if late_shards:
        w.update(_assemble_late(gathered))
    p.update(_pad_weights(w))
    cq_n, ckv_n, kr = _mla_prep(proj_b, w['q_lat_norm'], w['kv_lat_norm'], tabs)
    q_lin = _mm(cq_n, p['w_uq'], 'nn', F32, "mm_uq")
    kv_lin = _mm(ckv_n, p['w_ukv'], 'nn', F32, "mm_ukv")
    qb, kb, vb, kt, vt = _mla_qkv(q_lin, kv_lin, kr, tabs)
    ob, lse_b = _mla_fwd(qb, kb, vt)
    cat = _outnorm_fwd(oa, ob, p['g_a'], p['g_b'])
    y = _mm(cat, p['w_o'], 'nn', F32, "mm_o")
    x1, h2 = _post_mix(x, y, w['norm_mix_post'], w['norm_ffn_pre'])
    u0 = _mm(h2, w['w_up'], 'nn', F32, "mm_up", sharded=True)
    ug, uv, a = _conv_fwd(u0, w['conv_w'], w['conv_b'])
    y2 = _mm(a, w['w_down'], 'nn', F32, "mm_down")
    dx2, dy2, dg_ffn_post, _, loss = _final(x1, y2, w['norm_ffn_post'], target)

    g = {'norm_ffn_post': dg_ffn_post}
    da = _mm(dy2, w['w_down'], 'nt', F32, "mm_down_dx")
    g['w_down'] = _mm(a, dy2, 'tn', BF16, "mm_down_dw")
    du0, g['conv_w'], g['conv_b'] = _conv_bwd(u0, ug, uv, da, w['conv_w'])
    dh2 = _mm(du0, w['w_up'], 'nt', F32, "mm_up_dx", sharded=True)
    g['w_up'] = _mm(h2, du0, 'tn', BF16, "mm_up_dw", sharded=True)
    dx1, g['norm_ffn_pre'] = _rms_bwd(x1, w['norm_ffn_pre'], [dh2], dx2, F32, "rms_ffn_pre_bwd")
    dy, g['norm_mix_post'] = _rms_bwd(y, w['norm_mix_post'], [dx1], None, BF16, "rms_mix_post_bwd")
    dcat = _mm(dy, p['w_o'], 'nt', F32, "mm_o_dx")
    g['w_o'] = _unpad_w_o(_mm(cat, dy, 'tn', F32, "mm_o_dw"))
    dpad = {}
    dopack_a, do_b, dpad['g_a'], dpad['g_b'] = _outnorm_bwd(oa, ob, p['g_a'], p['g_b'], dcat)

    scatter = None
    if late_shards:
        row_slots = lambda a: a.astype(BF16).reshape(N_CHIPS, a.shape[0] // N_CHIPS, a.shape[1])
        scatter = _Exchange([g['w_up'], row_slots(g['w_down']), row_slots(g['w_o'])], True)
    dq_b, dk_b, dv_b, received = _mla_bwd(qb, kb, vb, kt, do_b, ob, lse_b, scatter)
    if late_shards:
        g.update(zip(EARLY_GRADS, received))
    dq_pre, dkv, dkr = _mla_bwd_prep(dq_b, dk_b, dv_b, tabs)
    dcq_n = _mm(dq_pre, p['w_uq'], 'nt', F32, "mm_uq_dx")
    dpad['w_uq'] = _mm(cq_n, dq_pre, 'tn', F32, "mm_uq_dw")
    dckv_n = _mm(dkv, p['w_ukv'], 'nt', F32, "mm_ukv_dx")
    dpad['w_ukv'] = _mm(ckv_n, dkv, 'tn', F32, "mm_ukv_dw")
    dproj_b, g['q_lat_norm'], g['kv_lat_norm'] = _mla_norm_bwd(proj_b, dcq_n, dckv_n, dkr,
                                                               w['q_lat_norm'], w['kv_lat_norm'])

    dproj_a = _dila_bwd(proj_a, dopack_a, _slab_bias(DIL_Q_BWD))
    dh1 = _mm(dproj_b, p['w_in_b'], 'nt', F32, "mm_in_b_dx")
    dh1 = _mm(dproj_a, p['w_in_a'], 'nt', F32, "mm_in_a_dx", add=dh1)
    dpad['w_in_a'] = _mm(h1, dproj_a, 'tn', F32, "mm_in_a_dw")
    dpad['w_in_b'] = _mm(h1, dproj_b, 'tn', F32, "mm_in_b_dw")
    grad_x, g['norm_mix_pre'] = _rms_bwd(x, w['norm_mix_pre'], [dh1], dx1, F32, "rms_mix_pre_bwd")
    g.update(_unpad_grads(dpad))
    return loss, grad_x, g


MESH = pl.DeviceIdType.MESH
ANY = pl.BlockSpec(memory_space=pl.ANY)


def _place():
    x, y, c = lax.axis_index("x"), lax.axis_index("y"), lax.axis_index("c")
    chips = [(1 - x, y), (x, 1 - y), (1 - x, 1 - y)]
    return x, y, c, chips


class _Exchange:
    def __init__(self, arrays, scatter):
        self.arrays, self.scatter, self.n = list(arrays), scatter, len(arrays)

    def out_shapes(self):
        return [jax.ShapeDtypeStruct(a.shape if self.scatter else (N_CHIPS,) + a.shape, a.dtype) for a in self.arrays]

    def scratch(self):
        return [pltpu.SemaphoreType.DMA((3 * self.n,)), pltpu.SemaphoreType.DMA((3 * self.n,)),
                pltpu.SemaphoreType.DMA((self.n,))]

    def _copies(self, in_refs, out_refs, sems, arrivals):
        send_sems, recv_sems, local_sems = sems
        x, y, c, chips = _place()
        me = 2 * x + y
        src = lambda b, chip: in_refs[b].at[chip] if self.scatter else in_refs[b]
        local = [pltpu.make_async_copy(src(b, me), out_refs[b].at[me], local_sems.at[b]) for b in range(self.n)]
        sends, recvs = [], []
        for j, (px, py) in enumerate(chips):
            for b in range(self.n):
                k = j * self.n + b
                common = dict(send_sem=send_sems.at[k], recv_sem=recv_sems.at[k], device_id=(px, py, c),
                              device_id_type=MESH)
                sends.append(pltpu.make_async_remote_copy(src_ref=src(b, 2 * px + py), dst_ref=out_refs[b].at[me],
                                                          **common))
                if arrivals:
                    recvs.append(pltpu.make_async_remote_copy(src_ref=src(b, me), dst_ref=out_refs[b].at[2 * px + py],
                                                              **common))
        return local, sends, recvs

    def start(self, in_refs, out_refs, sems):
        local, sends, _ = self._copies(in_refs, out_refs, sems, False)
        for cp in local + sends:
            cp.start()

    def finish(self, in_refs, out_refs, sems):
        local, sends, recvs = self._copies(in_refs, out_refs, sems, True)
        for cp in recvs:
            cp.wait_recv()
        for cp in sends:
            cp.wait_send()
        for cp in local:
            cp.wait()


def _exchange_call(arrays, scatter, name):
    ex = _Exchange(arrays, scatter)
    n = ex.n

    def body(*refs):
        in_refs, out_refs, sems = refs[:n], refs[n:2 * n], refs[2 * n:]
        ex.start(in_refs, out_refs, sems)
        ex.finish(in_refs, out_refs, sems)

    return pl.pallas_call(body, name=name, in_specs=[ANY] * n, out_specs=[ANY] * n, out_shape=ex.out_shapes(),
                          scratch_shapes=ex.scratch())(*ex.arrays)


def _all_gather(bufs, name="gather_weights"):
    return _exchange_call(bufs, False, name)


def _scatter_grads(slots, name="scatter_grads"):
    return _exchange_call(slots, True, name)


ELEMENTWISE_BLOCK = 256 * 1024


def _row_tile(rows, cols):
    best = None
    for t in range(16, min(rows, max(16, ELEMENTWISE_BLOCK // cols)) + 1, 16):
        if rows % t == 0:
            best = t
    return best if best is not None else rows


def _sum_slots(recv, name):
    _, R, C = recv.shape
    tr = _row_tile(R, C)

    def body(r_ref, o_ref):
        f = lambda i: r_ref[i].astype(F32)
        o_ref[...] = ((f(0) + f(1)) + f(2)) + f(3)

    return pl.pallas_call(
        body, name="sum_" + name, grid=(R // tr,),
        in_specs=[pl.BlockSpec((N_CHIPS, tr, C), lambda i: (0, i, 0))],
        out_specs=pl.BlockSpec((tr, C), lambda i: (i, 0)),
        out_shape=jax.ShapeDtypeStruct((R, C), F32),
        compiler_params=_cparams(("parallel",)),
    )(recv)


def _swap_sibling(parts):
    n = len(parts)

    def body(*refs):
        p_refs, o_refs, send_sems, recv_sems = refs[:n], refs[n:2 * n], refs[2 * n], refs[2 * n + 1]
        x, y, c, _ = _place()
        cps = [pltpu.make_async_remote_copy(src_ref=p_refs[b], dst_ref=o_refs[b], send_sem=send_sems.at[b],
                                            recv_sem=recv_sems.at[b], device_id=(x, y, 1 - c), device_id_type=MESH)
               for b in range(n)]
        for cp in cps:
            cp.start()
        for cp in cps:
            cp.wait()

    return pl.pallas_call(
        body, name="swap_sibling", in_specs=[ANY] * n, out_specs=[ANY] * n,
        out_shape=[jax.ShapeDtypeStruct(p.shape, p.dtype) for p in parts],
        scratch_shapes=[pltpu.SemaphoreType.DMA((n,)), pltpu.SemaphoreType.DMA((n,))],
    )(*parts)


def _adamw(g0, g1, w, m, v, name, offset=0):
    R, C = w.shape
    tr = _row_tile(R, C)
    packed = g0.shape != w.shape
    bc1 = 1.0 - ADAM_B1 ** ADAM_STEP
    bc2 = 1.0 - ADAM_B2 ** ADAM_STEP

    def body(g0_ref, g1_ref, w_ref, m_ref, v_ref, g_ref, d_ref, nm_ref, nv_ref):
        if packed:
            g = g0_ref[:, offset:offset + C] + g1_ref[:, offset:offset + C]
        else:
            g = g0_ref[...] + g1_ref[...]
        g_ref[...] = g
        nm = ADAM_B1 * m_ref[...] + (1.0 - ADAM_B1) * g
        nv = ADAM_B2 * v_ref[...] + (1.0 - ADAM_B2) * (g * g)
        nm_ref[...] = nm
        nv_ref[...] = nv
        d_ref[...] = -ADAM_LR * ((nm / bc1) / (jnp.sqrt(nv / bc2) + ADAM_EPS) + ADAM_WD * w_ref[...])

    spec = pl.BlockSpec((tr, C), lambda i: (i, 0))
    gspec = pl.BlockSpec(g0.shape, lambda i: (0, 0)) if packed else spec
    out = jax.ShapeDtypeStruct((R, C), F32)
    return pl.pallas_call(
        body, name="adamw_" + name, grid=(R // tr,), in_specs=[gspec, gspec, spec, spec, spec],
        out_specs=[spec] * 4, out_shape=[out] * 4, compiler_params=_cparams(("parallel",)),
    )(g0, g1, w, m, v)


def kernel(x, norm_mix_pre, w_in, q_lat_norm, w_uq, kv_lat_norm, w_ukv, out_norm_a, out_norm_b, w_o, norm_mix_post, norm_ffn_pre, w_up, conv_w, conv_b, w_down, norm_ffn_post, loss_target, m_norm_mix_pre, m_w_in, m_q_lat_norm, m_w_uq, m_kv_lat_norm, m_w_ukv, m_out_norm_a, m_out_norm_b, m_w_o, m_norm_mix_post, m_norm_ffn_pre, m_w_up, m_conv_w, m_conv_b, m_w_down, m_norm_ffn_post, v_norm_mix_pre, v_w_in, v_q_lat_norm, v_w_uq, v_kv_lat_norm, v_w_ukv, v_out_norm_a, v_out_norm_b, v_w_o, v_norm_mix_post, v_norm_ffn_pre, v_w_up, v_conv_w, v_conv_b, v_w_down, v_norm_ffn_post):
    args = dict(locals())
    strip = lambda a: a[0] if a.ndim == 3 else a
    wl = {n: strip(args[n]) for n in WEIGHTS}
    ml = {n: strip(args['m_' + n]) for n in WEIGHTS}
    vl = {n: strip(args['v_' + n]) for n in WEIGHTS}

    gathered = _all_gather([wl['w_in'].astype(BF16), wl['conv_w']])
    full = {n: wl[n] for n in SMALL}
    for n, a in zip(('w_in', 'conv_w'), gathered):
        full[n] = jnp.concatenate([a[i] for i in range(N_CHIPS)], axis=1)

    loss_b, grad_x, g = _local_step(x[0], loss_target[0], full, {n: wl[n].astype(BF16) for n in LATE})

    sharded = [n for n in WEIGHTS if SHARD_AXIS[n] is not None]
    late = [n for n in sharded if n not in EARLY_GRADS]

    def slots_of(n):
        a = g[n].astype(BF16)
        cols = a.shape[1] // N_CHIPS
        return jnp.stack([a[:, i * cols:(i + 1) * cols] for i in range(N_CHIPS)])

    small_pack = jnp.concatenate([g[n] for n in SMALL], axis=1)
    slots = [slots_of(n) for n in late] + [jnp.broadcast_to(small_pack[None], (N_CHIPS,) + small_pack.shape)]
    recv = dict(zip(late + ['small'], _scatter_grads(slots)))
    recv.update({n: g[n] for n in EARLY_GRADS})
    parts = [_sum_slots(recv[n], n) for n in sharded + ['small']]
    others = _swap_sibling(parts)

    outs = {}

    def record(n, results):
        for tag, a in zip(('grad', 'delta', 'new_m', 'new_v'), results):
            outs[tag + '_' + n] = a.reshape(args[n].shape)

    for n, p0, p1 in zip(sharded, parts, others):
        record(n, _adamw(p0, p1, wl[n], ml[n], vl[n], n))
    offset = 0
    for n in SMALL:
        record(n, _adamw(parts[-1], others[-1], wl[n], ml[n], vl[n], n, offset=offset))
        offset += wl[n].shape[1]

    loss = lax.psum(loss_b[0, 0], ("x", "y", "c"))
    return (loss, grad_x[None], *[outs['grad_' + n] for n in WEIGHTS], *[outs['delta_' + n] for n in WEIGHTS],
            *[outs['new_m_' + n] for n in WEIGHTS], *[outs['new_v_' + n] for n in WEIGHTS])
```

```python
import math

import jax
import jax.numpy as jnp
from jax import lax
from jax.experimental import pallas as pl
from jax.experimental.pallas import tpu as pltpu

F32 = jnp.float32
BF16 = jnp.bfloat16

LANES = 128
D_MODEL = 1024
N_HEADS = 8
HEAD_DIM = 64
QK_ROPE = 32
Q_LORA = 384
KV_LORA = 256
D_FF = 2816
WIDTH = N_HEADS * HEAD_DIM
WIDTH_P = N_HEADS * LANES
IN_SIZES = (WIDTH, WIDTH, WIDTH, Q_LORA, KV_LORA, QK_ROPE)
D_IN = sum(IN_SIZES)
TAIL_P = Q_LORA + KV_LORA + LANES
EPS = 1e-6
ROPE_BASE = 10000.0
MASKED = -2e30
M_INIT = -1e30
MLA_TQ = 4096
MLA_TK = 256
MLA_SCALE = (HEAD_DIM + QK_ROPE) ** -0.5
V_ROWS = HEAD_DIM + 16
DIL_SCALE = HEAD_DIM ** -0.5

ADAM_LR = 0.001
ADAM_B1 = 0.9
ADAM_B2 = 0.999
ADAM_EPS = 1e-08
ADAM_WD = 0.01
ADAM_STEP = 10

VMEM_LIMIT = 56 * 1024 * 1024

N_CHIPS = 4

WEIGHTS = ['norm_mix_pre', 'w_in', 'q_lat_norm', 'w_uq', 'kv_lat_norm', 'w_ukv', 'out_norm_a', 'out_norm_b',
           'w_o', 'norm_mix_post', 'norm_ffn_pre', 'w_up', 'conv_w', 'conv_b', 'w_down', 'norm_ffn_post']
SHARD_AXIS = {'norm_mix_pre': None, 'w_in': 1, 'q_lat_norm': None, 'w_uq': 1, 'kv_lat_norm': None, 'w_ukv': 1,
              'out_norm_a': None, 'out_norm_b': None, 'w_o': 0, 'norm_mix_post': None, 'norm_ffn_pre': None,
              'w_up': 1, 'conv_w': 1, 'conv_b': None, 'w_down': 0, 'norm_ffn_post': None}
SMALL = [n for n in WEIGHTS if SHARD_AXIS[n] is None]


def _tile(dim, target):
    best = None
    t = LANES
    while t <= min(dim, target):
        if dim % t == 0:
            best = t
        t += LANES
    return best if best is not None else dim


def _cparams(sem=None):
    return pltpu.CompilerParams(dimension_semantics=sem, vmem_limit_bytes=VMEM_LIMIT)


def _mm(a, b, mode, out_dtype, name, add=None, tm=1024, tn=1024, tk=1024, sharded=False):
    if mode == 'nn':
        (M, K), (K2, N) = a.shape, ((b.shape[1], N_CHIPS * b.shape[2]) if sharded else b.shape)
        dims = (((1,), (0,)), ((), ()))
    elif mode == 'nt':
        (M, K), (N, K2) = a.shape, ((b.shape[1], N_CHIPS * b.shape[2]) if sharded else b.shape)
        dims = (((1,), (1,)), ((), ()))
    else:
        (K, M), (K2, N) = a.shape, b.shape
        dims = (((0,), (0,)), ((), ()))
    assert K == K2, (a.shape, b.shape, mode)
    tm, tn, tk = _tile(M, tm), _tile(N, tn), _tile(K, tk)
    if K == D_FF:
        tk = K
    if N == D_FF:
        tn, tm = N, min(tm, 512)
    if M == D_FF:
        tm = M
    if sharded and mode == 'nt':
        tk = K // N_CHIPS
    elif sharded:
        tn = N // N_CHIPS
    nk = K // tk
    if mode == 'nn':
        a_spec = pl.BlockSpec((tm, tk), lambda i, j, k: (i, k))
        b_spec = (pl.BlockSpec((None, tk, tn), lambda i, j, k: (j, k, 0)) if sharded
                  else pl.BlockSpec((tk, tn), lambda i, j, k: (k, j)))
    elif mode == 'nt':
        a_spec = pl.BlockSpec((tm, tk), lambda i, j, k: (i, k))
        b_spec = (pl.BlockSpec((None, tn, tk), lambda i, j, k: (k, j, 0)) if sharded
                  else pl.BlockSpec((tn, tk), lambda i, j, k: (j, k)))
    else:
        a_spec = pl.BlockSpec((tk, tm), lambda i, j, k: (k, i))
        b_spec = pl.BlockSpec((tk, tn), lambda i, j, k: (k, j))
    o_spec = pl.BlockSpec((tm, tn), lambda i, j, k: (i, j))
    out_shape = jax.ShapeDtypeStruct((M, N), out_dtype)
    if sharded and mode == 'tn':
        o_spec = pl.BlockSpec((None, tm, tn), lambda i, j, k: (j, i, 0))
        out_shape = jax.ShapeDtypeStruct((N_CHIPS, M, tn), out_dtype)
    has_add = add is not None

    def body(*refs):
        if has_add:
            a_ref, b_ref, add_ref, o_ref, acc_ref = refs
        else:
            a_ref, b_ref, o_ref, acc_ref = refs
        k = pl.program_id(2)

        @pl.when(k == 0)
        def _():
            acc_ref[...] = jnp.zeros_like(acc_ref)

        acc_ref[...] += lax.dot_general(a_ref[...].astype(BF16), b_ref[...].astype(BF16), dims,
                                        preferred_element_type=F32)

        @pl.when(k == nk - 1)
        def _():
            r = acc_ref[...]
            if has_add:
                r = r + add_ref[...]
            o_ref[...] = r.astype(o_ref.dtype)

    ins = [a, b] + ([add] if has_add else [])
    in_specs = [a_spec, b_spec] + ([o_spec] if has_add else [])
    return pl.pallas_call(
        body, name=name, grid=(M // tm, N // tn, nk), in_specs=in_specs, out_specs=o_spec, out_shape=out_shape,
        scratch_shapes=[pltpu.VMEM((tm, tn), F32)],
        compiler_params=_cparams(("parallel", "parallel", "arbitrary")),
    )(*ins)


def _rows(body, name, S, ts, row_ins, full_ins, row_outs, acc_outs=(), chunk_outs=()):
    in_specs = [pl.BlockSpec((ts, a.shape[1]), lambda i: (i, 0)) for a in row_ins]
    in_specs += [pl.BlockSpec(a.shape, lambda i, nd=a.ndim: (0,) * nd) for a in full_ins]
    out_specs = [pl.BlockSpec((ts, w), lambda i: (i, 0)) for (w, _) in row_outs]
    out_specs += [pl.BlockSpec(shape, lambda i, nd=len(shape): (0,) * nd) for (shape, _) in acc_outs]
    out_specs += [pl.BlockSpec((lead, 1, LANES, ts), lambda i: (0, i, 0, 0)) for (lead, _) in chunk_outs]
    out_shape = [jax.ShapeDtypeStruct((S, w), dt) for (w, dt) in row_outs]
    out_shape += [jax.ShapeDtypeStruct(shape, dt) for (shape, dt) in acc_outs]
    out_shape += [jax.ShapeDtypeStruct((lead, S // ts, LANES, ts), dt) for (lead, dt) in chunk_outs]

    def kbody(*refs):
        body(pl.program_id(0), *refs)

    return pl.pallas_call(
        kbody, name=name, grid=(S // ts,), in_specs=in_specs, out_specs=out_specs, out_shape=out_shape,
        compiler_params=_cparams(("arbitrary",)),
    )(*row_ins, *full_ins)


def _acc_add(step, ref, val):
    @pl.when(step == 0)
    def _():
        ref[...] = val

    @pl.when(step != 0)
    def _():
        ref[...] += val


def _rms_fwd(x, g, name):
    S, W = x.shape

    def body(step, x_ref, g_ref, h_ref):
        xv = x_ref[...]
        r = lax.rsqrt(jnp.mean(xv * xv, axis=-1, keepdims=True) + EPS)
        h_ref[...] = (xv * r * g_ref[...]).astype(BF16)

    return _rows(body, name, S, 512, [x], [g], [(W, BF16)])[0]


def _rms_bwd_math(xv, g, dy, width):
    r = lax.rsqrt(jnp.sum(xv * xv, axis=-1, keepdims=True) * (1.0 / width) + EPS)
    xn = xv * r
    dyg = dy * g
    dx = r * (dyg - xn * (jnp.sum(dyg * xn, axis=-1, keepdims=True) * (1.0 / width)))
    return dx, dy * xn


def _rms_bwd(x, g, dys, resid, out_dtype, name):
    S, W = x.shape
    nd = len(dys)
    has_res = resid is not None

    def body(step, *refs):
        x_ref = refs[0]
        dy_refs = refs[1:1 + nd]
        pos = 1 + nd
        res_ref = refs[pos] if has_res else None
        pos += int(has_res)
        g_ref, dx_ref, dg_ref = refs[pos], refs[pos + 1], refs[pos + 2]
        dy = dy_refs[0][...].astype(F32)
        for r_ in dy_refs[1:]:
            dy = dy + r_[...].astype(F32)
        dx, dgr = _rms_bwd_math(x_ref[...], g_ref[...], dy, W)
        if has_res:
            dx = dx + res_ref[...]
        dx_ref[...] = dx.astype(dx_ref.dtype)
        _acc_add(step, dg_ref, jnp.sum(dgr, axis=0, keepdims=True))

    row_ins = [x] + list(dys) + ([resid] if has_res else [])
    dx, dg = _rows(body, name, S, 512, row_ins, [g], [(W, out_dtype)], [((1, W), F32)])
    return dx, dg


def _rope_apply(xv, c, sa, sb):
    return xv * c + pltpu.roll(xv, 16, 1) * sa + pltpu.roll(xv, LANES - 16, 1) * sb


def _rope_transpose(dy, c, sa, sb):
    return dy * c + pltpu.roll(dy * sa, LANES - 16, 1) + pltpu.roll(dy * sb, 16, 1)


def _rope_tables(S):
    pos = jnp.arange(S, dtype=F32)
    inv_freq = jnp.exp(-math.log(ROPE_BASE) * jnp.arange(0, QK_ROPE, 2, dtype=F32) / QK_ROPE)
    ang = pos[:, None] * inv_freq[None, :]
    cos, sin = jnp.cos(ang), jnp.sin(ang)
    ones, zeros = jnp.ones((S, HEAD_DIM), F32), jnp.zeros((S, HEAD_DIM), F32)
    z16, z32 = jnp.zeros((S, 16), F32), jnp.zeros((S, 32), F32)
    c = jnp.concatenate([ones, cos, cos, z32], axis=1)
    sa = jnp.concatenate([zeros, z16, sin, z32], axis=1)
    sb = jnp.concatenate([zeros, -sin, z16, z32], axis=1)
    return c, sa, sb


def _mla_prep(proj_b, g_q, g_kv, tabs):
    S = proj_b.shape[0]

    def body(step, p_ref, c_ref, sa_ref, sb_ref, gq_ref, gkv_ref, cq_ref, ckv_ref, kr_ref):
        cq = p_ref[:, 0:Q_LORA]
        ckv = p_ref[:, Q_LORA:Q_LORA + KV_LORA]
        kr = p_ref[:, Q_LORA + KV_LORA:TAIL_P]
        rq = lax.rsqrt(jnp.mean(cq * cq, axis=-1, keepdims=True) + EPS)
        cq_ref[...] = (cq * rq * gq_ref[...]).astype(BF16)
        rk = lax.rsqrt(jnp.mean(ckv * ckv, axis=-1, keepdims=True) + EPS)
        ckv_ref[...] = (ckv * rk * gkv_ref[...]).astype(BF16)
        kr_ref[...] = _rope_apply(kr, c_ref[...], sa_ref[...], sb_ref[...])

    return _rows(body, "mla_prep", S, 512, [proj_b, *tabs], [g_q, g_kv],
                 [(Q_LORA, BF16), (KV_LORA, BF16), (LANES, F32)])


def _mla_qkv(q, kv, kr, tabs):
    S = q.shape[0]

    def body(step, q_ref, kv_ref, kr_ref, c_ref, sa_ref, sb_ref, qb_ref, kb_ref, vb_ref, kt_ref, vt_ref):
        c, sa, sb = c_ref[...], sa_ref[...], sb_ref[...]
        krv = kr_ref[...]
        row = lax.broadcasted_iota(jnp.int32, (LANES, MLA_TK), 0)
        for h in range(N_HEADS):
            blk = slice(h * LANES, (h + 1) * LANES)
            qb_ref[:, blk] = (_rope_apply(q_ref[:, blk], c, sa, sb) * MLA_SCALE).astype(BF16)
            kh = kv_ref[:, blk] + krv
            kb_ref[:, blk] = kh.astype(BF16)
            kt_ref[h, 0] = kh.T.astype(BF16)
            vh = kv_ref[:, WIDTH_P + h * LANES:WIDTH_P + (h + 1) * LANES]
            vt_ref[h, 0] = jnp.where(row == HEAD_DIM, 1.0, vh.T).astype(BF16)
        vb_ref[...] = kv_ref[:, WIDTH_P:2 * WIDTH_P].astype(BF16)

    return _rows(body, "mla_qkv", S, MLA_TK, [q, kv, kr, *tabs], [],
                 [(WIDTH_P, BF16), (WIDTH_P, BF16), (WIDTH_P, BF16)],
                 chunk_outs=[(N_HEADS, BF16), (N_HEADS, BF16)])


def _outnorm_fwd(oa, ob, ga, gb):
    S = oa.shape[0]

    def body(step, oa_ref, ob_ref, ga_ref, gb_ref, cat_ref):
        live = lax.broadcasted_iota(jnp.int32, oa_ref.shape, 1) % LANES < HEAD_DIM
        for o_ref, g_ref, off in ((oa_ref, ga_ref, 0), (ob_ref, gb_ref, WIDTH_P)):
            o = jnp.where(live, o_ref[...], 0.0)
            r = lax.rsqrt(jnp.sum(o * o, axis=-1, keepdims=True) * (1.0 / WIDTH) + EPS)
            cat_ref[:, off:off + WIDTH_P] = (o * r * g_ref[...]).astype(BF16)

    return _rows(body, "outnorm_fwd", S, 512, [oa, ob], [ga, gb], [(2 * WIDTH_P, BF16)])[0]


def _outnorm_bwd(oa, ob, ga, gb, dcat):
    S = oa.shape[0]

    def body(step, oa_ref, ob_ref, dcat_ref, ga_ref, gb_ref, dpa_ref, dob_ref, dga_ref, dgb_ref):
        live = lax.broadcasted_iota(jnp.int32, oa_ref.shape, 1) % LANES < HEAD_DIM
        lane = lax.broadcasted_iota(jnp.int32, (oa_ref.shape[0], LANES), 1)
        packed = oa_ref[...]
        o = jnp.where(live, packed, 0.0)
        do, dgr = _rms_bwd_math(o, ga_ref[...], dcat_ref[:, 0:WIDTH_P], WIDTH)
        _acc_add(step, dga_ref, jnp.sum(dgr, axis=0, keepdims=True))
        prod = do.astype(BF16).astype(F32) * o
        for h in range(N_HEADS):
            blk = slice(h * LANES, (h + 1) * LANES)
            delta = jnp.sum(prod[:, blk], axis=-1, keepdims=True)
            lse = jnp.sum(jnp.where(lane == HEAD_DIM, packed[:, blk], 0.0), axis=-1, keepdims=True)
            out = do[:, blk]
            for k, piece in enumerate(_split3(-delta) + _split3(-lse)):
                out = jnp.where(lane == HEAD_DIM + k, piece, out)
            dpa_ref[:, blk] = out

        ov = ob_ref[...]
        do_b, dgr_b = _rms_bwd_math(ov, gb_ref[...], dcat_ref[:, WIDTH_P:2 * WIDTH_P], WIDTH)
        dob_ref[...] = do_b.astype(BF16)
        _acc_add(step, dgb_ref, jnp.sum(dgr_b, axis=0, keepdims=True))

    return _rows(body, "outnorm_bwd", S, 512, [oa, ob, dcat], [ga, gb],
                 [(WIDTH_P, F32), (WIDTH_P, BF16)], [((1, WIDTH_P), F32), ((1, WIDTH_P), F32)])


def _post_mix(x, y, g_post, g_pre):
    S, W = x.shape

    def body(step, x_ref, y_ref, gp_ref, gq_ref, x1_ref, h_ref):
        yv = y_ref[...]
        r = lax.rsqrt(jnp.mean(yv * yv, axis=-1, keepdims=True) + EPS)
        x1 = x_ref[...] + yv * r * gp_ref[...]
        x1_ref[...] = x1
        r1 = lax.rsqrt(jnp.mean(x1 * x1, axis=-1, keepdims=True) + EPS)
        h_ref[...] = (x1 * r1 * gq_ref[...]).astype(BF16)

    return _rows(body, "post_mix", S, 512, [x, y], [g_post, g_pre], [(W, F32), (W, BF16)])


def _final(x1, y2, g, target):
    S, W = x1.shape
    ts = 512
    nsteps = S // ts

    def body(step, x1_ref, y_ref, t_ref, g_ref, dx2_ref, dy_ref, dg_ref, sq_ref, loss_ref):
        yv = y_ref[...]
        gv = g_ref[...]
        r = lax.rsqrt(jnp.mean(yv * yv, axis=-1, keepdims=True) + EPS)
        yn = yv * r
        err = (x1_ref[...] + yn * gv) - t_ref[...]
        dx2 = err * (1.0 / W)
        dx2_ref[...] = dx2
        dyg = dx2 * gv
        dy = r * (dyg - yn * jnp.mean(dyg * yn, axis=-1, keepdims=True))
        dy_ref[...] = dy.astype(BF16)
        _acc_add(step, dg_ref, jnp.sum(dx2 * yn, axis=0, keepdims=True))
        _acc_add(step, sq_ref, jnp.sum(err * err, axis=0, keepdims=True))

        @pl.when(step == nsteps - 1)
        def _():
            tot = jnp.sum(sq_ref[...], axis=-1, keepdims=True) * (0.5 / W)
            loss_ref[...] = jnp.broadcast_to(tot, (1, LANES))

    return _rows(body, "final_loss", S, ts, [x1, y2, target], [g], [(W, F32), (W, BF16)],
                 [((1, W), F32), ((1, W), F32), ((1, LANES), F32)])


_GELU_C = math.sqrt(2.0 / math.pi)
_CONV_CHUNK = 128
_HALO = 8


def _gelu(g):
    t = jnp.tanh(_GELU_C * (g + 0.044715 * (g * g * g)))
    return g * (0.5 * (1.0 + t)), t


def _fill_padded(pad_ref, src_ref, S):
    zeros = jnp.zeros((_HALO, LANES), F32)
    pad_ref[0:_HALO, :] = zeros
    pad_ref[_HALO + S:2 * _HALO + S, :] = zeros
    for r0 in range(0, S, _CONV_CHUNK):
        pad_ref[_HALO + r0:_HALO + r0 + _CONV_CHUNK, :] = src_ref[r0:r0 + _CONV_CHUNK, :].astype(F32)


def _conv_fwd(u0, conv_w, conv_b):
    S, C2 = u0.shape
    nb = D_FF // LANES

    def body(u0g_ref, u0v_ref, wg_ref, wv_ref, bg_ref, bv_ref, ug_ref, uv_ref, a_ref, pg_ref, pv_ref):
        _fill_padded(pg_ref, u0g_ref, S)
        _fill_padded(pv_ref, u0v_ref, S)
        wg, wv = wg_ref[...], wv_ref[...]
        for r0 in range(0, S, _CONV_CHUNK):
            def conv(p_ref, w, b_ref):
                base = _HALO + r0
                return (p_ref[base - 1:base - 1 + _CONV_CHUNK, :] * w[0:1, :]
                        + p_ref[base:base + _CONV_CHUNK, :] * w[1:2, :]
                        + p_ref[base + 1:base + 1 + _CONV_CHUNK, :] * w[2:3, :] + b_ref[...])
            g = conv(pg_ref, wg, bg_ref)
            v = conv(pv_ref, wv, bv_ref)
            rows = slice(r0, r0 + _CONV_CHUNK)
            ug_ref[rows, :] = g
            uv_ref[rows, :] = v
            a_ref[rows, :] = (_gelu(g)[0] * v).astype(BF16)

    col = lambda off: pl.BlockSpec((S, LANES), lambda j: (0, j + off))
    wcol = lambda off: pl.BlockSpec((3, LANES), lambda j: (0, j + off))
    bcol = lambda off: pl.BlockSpec((1, LANES), lambda j: (0, j + off))
    ug, uv, a = pl.pallas_call(
        body, name="conv_gelu_fwd", grid=(nb,),
        in_specs=[col(0), col(nb), wcol(0), wcol(nb), bcol(0), bcol(nb)],
        out_specs=[col(0), col(0), col(0)],
        out_shape=[jax.ShapeDtypeStruct((S, D_FF), F32), jax.ShapeDtypeStruct((S, D_FF), F32),
                   jax.ShapeDtypeStruct((S, D_FF), BF16)],
        scratch_shapes=[pltpu.VMEM((S + 2 * _HALO, LANES), F32), pltpu.VMEM((S + 2 * _HALO, LANES), F32)],
        compiler_params=_cparams(("arbitrary",)),
    )(u0, u0, conv_w, conv_w, conv_b, conv_b)
    return ug, uv, a


def _conv_bwd(u0, ug, uv, da, conv_w):
    S = u0.shape[0]
    nb = D_FF // LANES

    def body(u0_ref, ug_ref, uv_ref, da_ref, w_ref, du0_ref, dw_ref, db_ref, pu_ref, pd_ref):
        is_g = pl.program_id(1) == 0
        _fill_padded(pu_ref, u0_ref, S)
        zeros = jnp.zeros((_HALO, LANES), F32)
        pd_ref[0:_HALO, :] = zeros
        pd_ref[_HALO + S:2 * _HALO + S, :] = zeros
        @pl.when(is_g)
        def _():
            for r0 in range(0, S, _CONV_CHUNK):
                rows = slice(r0, r0 + _CONV_CHUNK)
                g = ug_ref[rows, :]
                t = _gelu(g)[1]
                dgel = 0.5 * (1.0 + t) + (0.5 * g) * (1.0 - t * t) * (_GELU_C * (1.0 + 3.0 * 0.044715 * (g * g)))
                pd_ref[_HALO + r0:_HALO + r0 + _CONV_CHUNK, :] = da_ref[rows, :] * uv_ref[rows, :] * dgel

        @pl.when(jnp.logical_not(is_g))
        def _():
            for r0 in range(0, S, _CONV_CHUNK):
                rows = slice(r0, r0 + _CONV_CHUNK)
                pd_ref[_HALO + r0:_HALO + r0 + _CONV_CHUNK, :] = da_ref[rows, :] * _gelu(ug_ref[rows, :])[0]
        w = w_ref[...]
        acc_b = jnp.zeros((1, LANES), F32)
        acc_w = [jnp.zeros((1, LANES), F32) for _ in range(3)]
        for r0 in range(0, S, _CONV_CHUNK):
            base = _HALO + r0
            du_m = pd_ref[base - 1:base - 1 + _CONV_CHUNK, :]
            du_c = pd_ref[base:base + _CONV_CHUNK, :]
            du_p = pd_ref[base + 1:base + 1 + _CONV_CHUNK, :]
            du0_ref[r0:r0 + _CONV_CHUNK, :] = (du_p * w[0:1, :] + du_c * w[1:2, :] + du_m * w[2:3, :]).astype(BF16)
            acc_b = acc_b + jnp.sum(du_c, axis=0, keepdims=True)
            for k in range(3):
                acc_w[k] = acc_w[k] + jnp.sum(du_c * pu_ref[base + k - 1:base + k - 1 + _CONV_CHUNK, :],
                                              axis=0, keepdims=True)
        db_ref[...] = acc_b
        for k in range(3):
            dw_ref[k:k + 1, :] = acc_w[k]

    own = pl.BlockSpec((S, LANES), lambda j, half: (0, half * nb + j))
    shared = pl.BlockSpec((S, LANES), lambda j, half: (0, j))
    du0, dw, db = pl.pallas_call(
        body, name="conv_gelu_bwd", grid=(nb, 2),
        in_specs=[own, shared, shared, shared, pl.BlockSpec((3, LANES), lambda j, half: (0, half * nb + j))],
        out_specs=[own, pl.BlockSpec((3, LANES), lambda j, half: (0, half * nb + j)),
                   pl.BlockSpec((1, LANES), lambda j, half: (0, half * nb + j))],
        out_shape=[jax.ShapeDtypeStruct((S, 2 * D_FF), BF16), jax.ShapeDtypeStruct((3, 2 * D_FF), F32),
                   jax.ShapeDtypeStruct((1, 2 * D_FF), F32)],
        scratch_shapes=[pltpu.VMEM((S + 2 * _HALO, LANES), F32), pltpu.VMEM((S + 2 * _HALO, LANES), F32)],
        compiler_params=_cparams(("arbitrary", "arbitrary")),
    )(u0, ug, uv, da, conv_w)
    return du0, dw, db


DIL_HALF = 64
DIL_Q_FWD = 128
DIL_Q_BWD = 256
DILATIONS = (1, 4, 16)


def _lanes_hi_to_all(x):
    lane = lax.broadcasted_iota(jnp.int32, x.shape, 1)
    return jnp.where(lane < HEAD_DIM, pltpu.roll(x, HEAD_DIM, 1), x)


_NT = (((1,), (1,)), ((), ()))
_TN = (((0,), (0,)), ((), ()))
HEAD_COLS = 3 * LANES


def _slab_bias(q):
    row = jnp.arange(q, dtype=jnp.int32)[:, None]
    col = jnp.arange(q + 2 * DIL_HALF, dtype=jnp.int32)[None, :]
    slopes = jnp.exp2(-8.0 * jnp.arange(1, N_HEADS + 1, dtype=F32) / N_HEADS)
    out = []
    for r in DILATIONS:
        variants = []
        for shift in (DIL_HALF, 0, 2 * DIL_HALF):
            ad = jnp.abs(col - shift - row)
            variants.append(jnp.where(ad <= DIL_HALF, -slopes[:, None, None] * (ad * r).astype(F32)[None], MASKED))
        out.append(jnp.stack(variants, axis=1))
    return jnp.stack(out, axis=0)


DIL_CHUNK = 512


def _block_geometry(i, nblk, q):
    first = pl.multiple_of(i * q, q)
    slab0 = pl.multiple_of(jnp.clip(i * q - DIL_HALF, 0, (nblk - 1) * q - 2 * DIL_HALF), DIL_HALF)
    variant = jnp.where(i == 0, 1, jnp.where(i == nblk - 1, 2, 0))
    return first, slab0, variant


def _class_rows(c, r, r0, n):
    return pl.ds(c + r0 * r, n, stride=r) if r > 1 else pl.ds(r0, n)


def _dila_fwd(proj_a, bias, exchange=None):
    S = proj_a.shape[0]
    lmax = S // DILATIONS[1]
    DIL_Q, DIL_SLAB = DIL_Q_FWD, DIL_Q_FWD + 2 * DIL_HALF

    def compute(qh_ref, kh_ref, vh_ref, b_ref, o_ref, q_s, k_s, v_s, cm_s):
        lane = lax.broadcasted_iota(jnp.int32, (DIL_Q, LANES), 1)
        lane_s = lax.broadcasted_iota(jnp.int32, (DIL_SLAB, LANES), 1)
        lane_c = lax.broadcasted_iota(jnp.int32, (DIL_CHUNK, LANES), 1)

        def run(g, nblk, load_q, load_k, load_v, store):
            def block(i, carry):
                first, slab0, variant = _block_geometry(i, nblk, DIL_Q)
                qv, ks, vs = load_q(first), load_k(slab0), load_v(slab0)
                s = lax.dot_general(qv, ks, _NT, preferred_element_type=F32) + b_ref[g, 0, variant]
                m = jnp.max(s, axis=-1, keepdims=True)
                acc = jnp.dot(jnp.exp(s - m).astype(BF16), vs, preferred_element_type=F32)
                l = _lanes_hi_to_all(acc)
                store(first, jnp.where(lane < HEAD_DIM, acc / l, m + jnp.log(l)))
                return carry

            lax.fori_loop(0, nblk, block, 0, unroll=min(8, nblk))

        def direct_store(first, val):
            o_ref[pl.ds(first, DIL_Q), :] = val

        run(0, S // DIL_Q,
            lambda f: (qh_ref[pl.ds(f, DIL_Q), :] * DIL_SCALE).astype(BF16),
            lambda s0: kh_ref[pl.ds(s0, DIL_SLAB), :].astype(BF16),
            lambda s0: jnp.where(lane_s < HEAD_DIM, vh_ref[pl.ds(s0, DIL_SLAB), :], 1.0).astype(BF16),
            direct_store)

        def cm_store(first, val):
            cm_s[pl.ds(first, DIL_Q), :] = val

        for g, r in list(enumerate(DILATIONS))[1:]:
            L = S // r
            n = min(L, DIL_CHUNK)
            for c in range(r):
                for r0 in range(0, L, n):
                    src = _class_rows(c, r, r0, n)
                    q_s[r0:r0 + n, :] = (qh_ref[src, :] * DIL_SCALE).astype(BF16)
                    k_s[r0:r0 + n, :] = kh_ref[src, :].astype(BF16)
                    v_s[r0:r0 + n, :] = jnp.where(lane_c[:n] < HEAD_DIM, vh_ref[src, :], 1.0).astype(BF16)
                run(g, L // DIL_Q, lambda f: q_s[pl.ds(f, DIL_Q), :], lambda s0: k_s[pl.ds(s0, DIL_SLAB), :],
                    lambda s0: v_s[pl.ds(s0, DIL_SLAB), :], cm_store)
                for r0 in range(0, L, n):
                    dst = _class_rows(c, r, r0, n)
                    a, b = cm_s[r0:r0 + n, :], o_ref[dst, :]
                    la, lb = _lanes_hi_to_all(a), _lanes_hi_to_all(b)
                    m = jnp.maximum(la, lb)
                    wa, wb = jnp.exp(la - m), jnp.exp(lb - m)
                    tot = wa + wb
                    o_ref[dst, :] = jnp.where(lane_c[:n] < HEAD_DIM, (wa * a + wb * b) / tot, m + jnp.log(tot))

    ex = exchange
    n = ex.n if ex else 0

    def body(*refs):
        ins, ex_in, o_ref, ex_out = refs[:4], refs[4:4 + n], refs[4 + n], refs[5 + n:5 + 2 * n]
        scratch, sems = refs[5 + 2 * n:9 + 2 * n], refs[9 + 2 * n:]
        if ex:
            pl.when(pl.program_id(0) == 0)(lambda: ex.start(ex_in, ex_out, sems))
        compute(*ins, o_ref, *scratch)
        if ex:
            pl.when(pl.program_id(0) == N_HEADS - 1)(lambda: ex.finish(ex_in, ex_out, sems))

    out = pl.pallas_call(
        body, name="dil_fwd", grid=(N_HEADS,),
        in_specs=[pl.BlockSpec((S, LANES), lambda h: (0, 3 * h)), pl.BlockSpec((S, LANES), lambda h: (0, 3 * h + 1)),
                  pl.BlockSpec((S, LANES), lambda h: (0, 3 * h + 2)),
                  pl.BlockSpec((len(DILATIONS), 1, 3, DIL_Q, DIL_SLAB), lambda h: (0, h, 0, 0, 0))] + [ANY] * n,
        out_specs=[pl.BlockSpec((S, LANES), lambda h: (0, h))] + [ANY] * n,
        out_shape=[jax.ShapeDtypeStruct((S, WIDTH_P), F32)] + (ex.out_shapes() if ex else []),
        scratch_shapes=[pltpu.VMEM((lmax, LANES), BF16), pltpu.VMEM((lmax, LANES), BF16),
                        pltpu.VMEM((lmax, LANES), BF16), pltpu.VMEM((lmax, LANES), F32)] + (ex.scratch() if ex else []),
        compiler_params=_cparams(("arbitrary",)),
    )(proj_a, proj_a, proj_a, bias, *(ex.arrays if ex else []))
    return out[0], list(out[1:])


N_SPLIT = 3


def _split3(x):
    hi = x.astype(BF16).astype(F32)
    mid = (x - hi).astype(BF16).astype(F32)
    lo = (x - hi - mid).astype(BF16).astype(F32)
    return hi, mid, lo


def _dila_bwd(proj_a, dopack, bias):
    S = proj_a.shape[0]
    lmax = S // DILATIONS[1]
    DIL_Q, DIL_SLAB = DIL_Q_BWD, DIL_Q_BWD + 2 * DIL_HALF

    def body(qh_ref, kh_ref, vh_ref, d_ref, b_ref, out_ref, dq_ref, dk_ref, dv_ref, q_s, k_s, v_s, do_s,
             dq_c, dk_c, dv_c):
        def scalar_lanes(shape):
            lane = lax.broadcasted_iota(jnp.int32, shape, 1)
            return lane, (lane >= HEAD_DIM) & (lane < HEAD_DIM + N_SPLIT)

        def q_side(q, x):
            lane, ones = scalar_lanes(x.shape)
            lse_parts = pltpu.roll(x, LANES - N_SPLIT, 1)
            qv = jnp.where(lane < HEAD_DIM, q * DIL_SCALE, jnp.where(ones, lse_parts, 0.0)).astype(BF16)
            return qv, jnp.where(lane < HEAD_DIM + N_SPLIT, x, 0.0).astype(BF16)

        def kv_side(k, v):
            _, ones = scalar_lanes(k.shape)
            return jnp.where(ones, 1.0, k).astype(BF16), jnp.where(ones, 1.0, v).astype(BF16)

        def run(g, nblk, load_q, load_kv, dq_o, dk_o, dv_o):
            def block(i, carry):
                first, slab0, variant = _block_geometry(i, nblk, DIL_Q)
                rows, slab = pl.ds(first, DIL_Q), pl.ds(slab0, DIL_SLAB)
                (qv, dov), (ks, vs) = load_q(rows), load_kv(slab)
                p = jnp.exp(lax.dot_general(qv, ks, _NT, preferred_element_type=F32) + b_ref[g, 0, variant])
                ds = (p * lax.dot_general(dov, vs, _NT, preferred_element_type=F32)).astype(BF16)
                dq_o[rows, :] = jnp.dot(ds, ks, preferred_element_type=F32) * DIL_SCALE
                dk_o[slab, :] += lax.dot_general(ds, qv, _TN, preferred_element_type=F32)
                dv_o[slab, :] += lax.dot_general(p.astype(BF16), dov, _TN, preferred_element_type=F32)
                return carry

            lax.fori_loop(0, nblk, block, 0, unroll=min(8, nblk))

        dk_ref[...] = jnp.zeros_like(dk_ref)
        dv_ref[...] = jnp.zeros_like(dv_ref)
        run(0, S // DIL_Q,
            lambda rows: q_side(qh_ref[rows, :], d_ref[rows, :]),
            lambda slab: kv_side(kh_ref[slab, :], vh_ref[slab, :]),
            dq_ref, dk_ref, dv_ref)

        for g, r in list(enumerate(DILATIONS))[1:]:
            L = S // r
            n = min(L, DIL_CHUNK)
            for c in range(r):
                for r0 in range(0, L, n):
                    src = _class_rows(c, r, r0, n)
                    q_s[r0:r0 + n, :], do_s[r0:r0 + n, :] = q_side(qh_ref[src, :], d_ref[src, :])
                    k_s[r0:r0 + n, :], v_s[r0:r0 + n, :] = kv_side(kh_ref[src, :], vh_ref[src, :])
                    dk_c[r0:r0 + n, :] = jnp.zeros((n, LANES), F32)
                    dv_c[r0:r0 + n, :] = jnp.zeros((n, LANES), F32)
                run(g, L // DIL_Q, lambda rows: (q_s[rows, :], do_s[rows, :]),
                    lambda slab: (k_s[slab, :], v_s[slab, :]), dq_c, dk_c, dv_c)
                for r0 in range(0, L, n):
                    dst = _class_rows(c, r, r0, n)
                    for acc, cls in ((dq_ref, dq_c), (dk_ref, dk_c), (dv_ref, dv_c)):
                        acc[dst, :] += cls[r0:r0 + n, :]

        for r0 in range(0, S, DIL_CHUNK):
            for part, ref in enumerate((dq_ref, dk_ref, dv_ref)):
                out_ref[r0:r0 + DIL_CHUNK, part * LANES:(part + 1) * LANES] = ref[r0:r0 + DIL_CHUNK, :].astype(BF16)

    bf = lambda rows: pltpu.VMEM((rows, LANES), BF16)
    f32 = lambda rows: pltpu.VMEM((rows, LANES), F32)
    return pl.pallas_call(
        body, name="dil_bwd", grid=(N_HEADS,),
        in_specs=[pl.BlockSpec((S, LANES), lambda h: (0, 3 * h), pipeline_mode=pl.Buffered(1)),
                  pl.BlockSpec((S, LANES), lambda h: (0, 3 * h + 1), pipeline_mode=pl.Buffered(1)),
                  pl.BlockSpec((S, LANES), lambda h: (0, 3 * h + 2), pipeline_mode=pl.Buffered(1)),
                  pl.BlockSpec((S, LANES), lambda h: (0, h), pipeline_mode=pl.Buffered(1)),
                  pl.BlockSpec((len(DILATIONS), 1, 3, DIL_Q, DIL_SLAB), lambda h: (0, h, 0, 0, 0))],
        out_specs=pl.BlockSpec((S, HEAD_COLS), lambda h: (0, h)),
        out_shape=jax.ShapeDtypeStruct((S, N_HEADS * HEAD_COLS), BF16),
        scratch_shapes=[f32(S), f32(S), f32(S), bf(lmax), bf(lmax), bf(lmax), bf(lmax),
                        f32(lmax), f32(lmax), f32(lmax)],
        compiler_params=_cparams(("arbitrary",)),
    )(proj_a, proj_a, proj_a, dopack, bias)


def _mla_fwd(q, k, vt):
    S = q.shape[0]
    tq, tk = MLA_TQ, MLA_TK
    nq, nk = S // tq, S // tk

    def body(q_ref, k_ref, vt_ref, o_ref, lse_ref, acc_ref):
        qv = q_ref[...]
        acc_ref[...] = jnp.zeros_like(acc_ref)

        def pair(j, m):
            sts = []
            for c in (2 * j, 2 * j + 1):
                kc = k_ref[pl.ds(pl.multiple_of(c * tk, tk), tk), :]
                sts.append(lax.dot_general(kc, qv, _NT, preferred_element_type=F32))
            m_new = jnp.maximum(m, jnp.maximum(jnp.max(sts[0], axis=0, keepdims=True),
                                               jnp.max(sts[1], axis=0, keepdims=True)))
            upd = [jnp.dot(vt_ref[0, c, 0:V_ROWS, :], jnp.exp(st - m_new).astype(BF16), preferred_element_type=F32)
                   for c, st in zip((2 * j, 2 * j + 1), sts)]
            acc_ref[...] = jnp.exp(m - m_new) * acc_ref[...] + (upd[0] + upd[1])
            return m_new

        m = lax.fori_loop(0, nk // 2, pair, jnp.full((1, tq), M_INIT, F32))
        acc = acc_ref[...]
        l = acc[HEAD_DIM:HEAD_DIM + 1, :]
        ot = jnp.concatenate([acc[0:HEAD_DIM, :] / l, jnp.zeros((LANES - HEAD_DIM, tq), F32)], axis=0)
        o_ref[...] = ot.T
        lse_ref[0] = m + jnp.log(l)

    return pl.pallas_call(
        body, name="mla_fwd", grid=(N_HEADS, nq),
        in_specs=[pl.BlockSpec((tq, LANES), lambda h, i: (i, h)),
                  pl.BlockSpec((S, LANES), lambda h, i: (0, h)),
                  pl.BlockSpec((1, nk, LANES, tk), lambda h, i: (h, 0, 0, 0))],
        out_specs=[pl.BlockSpec((tq, LANES), lambda h, i: (i, h)),
                   pl.BlockSpec((1, 1, tq), lambda h, i: (h, 0, i))],
        out_shape=[jax.ShapeDtypeStruct((S, WIDTH_P), F32), jax.ShapeDtypeStruct((N_HEADS, 1, S), F32)],
        scratch_shapes=[pltpu.VMEM((V_ROWS, tq), F32)],
        compiler_params=_cparams(("parallel", "parallel")),
    )(q, k, vt)


def _mla_bwd(q, k, v, kt, do, o, lse, exchange=None):
    S = q.shape[0]
    tq, tk = MLA_TQ, MLA_TK
    nq, nk = S // tq, S // tk

    def compute(q_ref, do_ref, o_ref, lse_ref, k_ref, v_ref, kt_ref, dq_ref, dk_ref, dv_ref, dqt_ref):
        @pl.when(pl.program_id(1) == 0)
        def _():
            dk_ref[...] = jnp.zeros_like(dk_ref)
            dv_ref[...] = jnp.zeros_like(dv_ref)

        qv, dov = q_ref[...], do_ref[...]
        delta = jnp.sum((dov.astype(F32) * o_ref[...]).T, axis=0, keepdims=True)
        lse = lse_ref[0]
        dqt_ref[...] = jnp.zeros_like(dqt_ref)

        def chunk(c, carry):
            rows = pl.ds(pl.multiple_of(c * tk, tk), tk)
            kc, vc = k_ref[rows, :], v_ref[rows, :]
            pt = jnp.exp(lax.dot_general(kc, qv, _NT, preferred_element_type=F32) - lse)
            dv_ref[rows, :] += jnp.dot(pt.astype(BF16), dov, preferred_element_type=F32)
            dpt = lax.dot_general(vc, dov, _NT, preferred_element_type=F32)
            dst = (pt * (dpt - delta)).astype(BF16)
            dk_ref[rows, :] += jnp.dot(dst, qv, preferred_element_type=F32)
            dqt_ref[...] += jnp.dot(kt_ref[0, c], dst, preferred_element_type=F32)
            return carry

        lax.fori_loop(0, nk, chunk, 0, unroll=2)
        dq_ref[...] = (dqt_ref[...] * MLA_SCALE).T

    ex = exchange
    n = ex.n if ex else 0

    def body(*refs):
        ins, ex_in, outs, ex_out = refs[:7], refs[7:7 + n], refs[7 + n:10 + n], refs[10 + n:10 + 2 * n]
        dqt_ref, sems = refs[10 + 2 * n], refs[11 + 2 * n:]
        first = (pl.program_id(0) == 0) & (pl.program_id(1) == 0)
        last = (pl.program_id(0) == N_HEADS - 1) & (pl.program_id(1) == nq - 1)
        if ex:
            pl.when(first)(lambda: ex.start(ex_in, ex_out, sems))
        compute(*ins, *outs, dqt_ref)
        if ex:
            pl.when(last)(lambda: ex.finish(ex_in, ex_out, sems))

    qspec = pl.BlockSpec((tq, LANES), lambda h, i: (i, h))
    kspec = pl.BlockSpec((S, LANES), lambda h, i: (0, h))
    out = jax.ShapeDtypeStruct((S, WIDTH_P), F32)
    res = pl.pallas_call(
        body, name="mla_bwd", grid=(N_HEADS, nq),
        in_specs=[qspec, qspec, qspec, pl.BlockSpec((1, 1, tq), lambda h, i: (h, 0, i)), kspec, kspec,
                  pl.BlockSpec((1, nk, LANES, tk), lambda h, i: (h, 0, 0, 0))] + [ANY] * n,
        out_specs=[qspec, kspec, kspec] + [ANY] * n, out_shape=[out, out, out] + (ex.out_shapes() if ex else []),
        scratch_shapes=[pltpu.VMEM((LANES, tq), F32)] + (ex.scratch() if ex else []),
        compiler_params=_cparams(("arbitrary", "arbitrary")),
    )(q, do, o, lse, k, v, kt, *(ex.arrays if ex else []))
    return res[0], res[1], res[2], list(res[3:])


def _mla_bwd_prep(dq, dk, dv, tabs):
    S = dq.shape[0]

    def body(step, dq_ref, dk_ref, dv_ref, c_ref, sa_ref, sb_ref, dqp_ref, dkv_ref, dkr_ref):
        c, sa, sb = c_ref[...], sa_ref[...], sb_ref[...]
        dksum = jnp.zeros((dq_ref.shape[0], LANES), F32)
        for h in range(N_HEADS):
            blk = slice(h * LANES, (h + 1) * LANES)
            dqp_ref[:, blk] = _rope_transpose(dq_ref[:, blk], c, sa, sb).astype(BF16)
            dksum = dksum + dk_ref[:, blk]
        dkv_ref[:, 0:WIDTH_P] = dk_ref[...].astype(BF16)
        dkv_ref[:, WIDTH_P:2 * WIDTH_P] = dv_ref[...].astype(BF16)
        lane = lax.broadcasted_iota(jnp.int32, dksum.shape, 1)
        live = (lane >= HEAD_DIM) & (lane < HEAD_DIM + QK_ROPE)
        dkr_ref[...] = jnp.where(live, _rope_transpose(dksum, c, sa, sb), 0.0)

    return _rows(body, "mla_bwd_prep", S, 512, [dq, dk, dv, *tabs], [],
                 [(WIDTH_P, BF16), (2 * WIDTH_P, BF16), (LANES, F32)])


def _mla_norm_bwd(proj_b, dcq_n, dckv_n, dkr, g_q, g_kv):
    S = proj_b.shape[0]

    def body(step, p_ref, dcq_ref, dckv_ref, dkr_ref, gq_ref, gkv_ref, dp_ref, dgq_ref, dgkv_ref):
        dcq, dgq = _rms_bwd_math(p_ref[:, 0:Q_LORA], gq_ref[...], dcq_ref[...], Q_LORA)
        dckv, dgkv = _rms_bwd_math(p_ref[:, Q_LORA:Q_LORA + KV_LORA], gkv_ref[...], dckv_ref[...], KV_LORA)
        dp_ref[:, 0:Q_LORA] = dcq.astype(BF16)
        dp_ref[:, Q_LORA:Q_LORA + KV_LORA] = dckv.astype(BF16)
        dp_ref[:, Q_LORA + KV_LORA:TAIL_P] = dkr_ref[...].astype(BF16)
        _acc_add(step, dgq_ref, jnp.sum(dgq, axis=0, keepdims=True))
        _acc_add(step, dgkv_ref, jnp.sum(dgkv, axis=0, keepdims=True))

    return _rows(body, "mla_norm_bwd", S, 512, [proj_b, dcq_n, dckv_n, dkr], [g_q, g_kv], [(TAIL_P, BF16)],
                 [((1, Q_LORA), F32), ((1, KV_LORA), F32)])


def _pad_cols(w, d):
    lead = w.shape[:-1]
    w = w.reshape(lead + (N_HEADS, d))
    w = jnp.pad(w, [(0, 0)] * len(lead) + [(0, 0), (0, LANES - d)])
    return w.reshape(lead + (N_HEADS * LANES,))


def _unpad_cols(w, d):
    lead = w.shape[:-1]
    return w.reshape(lead + (N_HEADS, LANES))[..., :d].reshape(lead + (N_HEADS * d,))


def _pad_w_in(w_in):
    zeros = lambda n: jnp.zeros((D_MODEL, n), w_in.dtype)
    p = {}
    parts = [_pad_cols(w_in[:, i * WIDTH:(i + 1) * WIDTH], HEAD_DIM).reshape(D_MODEL, N_HEADS, 1, LANES)
             for i in range(3)]
    p['w_in_a'] = jnp.concatenate(parts, axis=2).reshape(D_MODEL, N_HEADS * HEAD_COLS)
    p['w_in_b'] = jnp.concatenate([w_in[:, 3 * WIDTH:3 * WIDTH + Q_LORA + KV_LORA], zeros(HEAD_DIM),
                                   w_in[:, D_IN - QK_ROPE:], zeros(LANES - HEAD_DIM - QK_ROPE)], axis=1)
    return p


def _pad_weights(w):
    p = {}
    p['w_uq'] = _pad_cols(w['w_uq'], HEAD_DIM + QK_ROPE)
    kv = w['w_ukv'].reshape(KV_LORA, N_HEADS, 2 * HEAD_DIM)
    p['w_ukv'] = jnp.concatenate([_pad_cols(kv[:, :, :HEAD_DIM].reshape(KV_LORA, WIDTH), HEAD_DIM),
                                  _pad_cols(kv[:, :, HEAD_DIM:].reshape(KV_LORA, WIDTH), HEAD_DIM)], axis=1)
    p['w_o'] = jnp.concatenate(
        [_pad_cols(w['w_o'][i * WIDTH:(i + 1) * WIDTH].T, HEAD_DIM).T for i in range(2)], axis=0)
    p['g_a'] = _pad_cols(w['out_norm_a'], HEAD_DIM)
    p['g_b'] = _pad_cols(w['out_norm_b'], HEAD_DIM)
    return p


def _unpad_w_o(dwo):
    return jnp.concatenate([_unpad_cols(dwo[i * WIDTH_P:(i + 1) * WIDTH_P].T, HEAD_DIM).T for i in range(2)], axis=0)


def _unpad_grads(d):
    g = {}
    dwa = d['w_in_a'].reshape(D_MODEL, N_HEADS, 3, LANES)
    tail = d['w_in_b']
    g['w_in'] = jnp.concatenate(
        [dwa[:, :, i, :HEAD_DIM].reshape(D_MODEL, WIDTH) for i in range(3)]
        + [tail[:, :Q_LORA + KV_LORA], tail[:, Q_LORA + KV_LORA + HEAD_DIM:Q_LORA + KV_LORA + HEAD_DIM + QK_ROPE]],
        axis=1)
    g['w_uq'] = _unpad_cols(d['w_uq'], HEAD_DIM + QK_ROPE)
    dk = _unpad_cols(d['w_ukv'][:, :WIDTH_P], HEAD_DIM).reshape(KV_LORA, N_HEADS, HEAD_DIM)
    dv = _unpad_cols(d['w_ukv'][:, WIDTH_P:], HEAD_DIM).reshape(KV_LORA, N_HEADS, HEAD_DIM)
    g['w_ukv'] = jnp.concatenate([dk, dv], axis=2).reshape(KV_LORA, 2 * WIDTH)
    g['out_norm_a'] = _unpad_cols(d['g_a'], HEAD_DIM)
    g['out_norm_b'] = _unpad_cols(d['g_b'], HEAD_DIM)
    return g


LATE = ['w_uq', 'w_ukv', 'w_o', 'w_up', 'w_down']
EARLY_GRADS = ['w_up', 'w_down', 'w_o']


def _assemble_late(gathered):
    g = dict(zip(LATE, gathered))
    return {'w_uq': jnp.concatenate([g['w_uq'][i] for i in range(N_CHIPS)], axis=1),
            'w_ukv': jnp.concatenate([g['w_ukv'][i] for i in range(N_CHIPS)], axis=1),
            'w_o': g['w_o'].reshape(D_MODEL, D_MODEL),
            'w_down': g['w_down'].reshape(D_FF, D_MODEL),
            'w_up': g['w_up']}


def _local_step(x, target, w, late_shards=None):
    S = x.shape[0]
    w = dict(w)
    p = _pad_w_in(w['w_in'])
    tabs = _rope_tables(S)
    bias = _slab_bias(DIL_Q_FWD)

    h1 = _rms_fwd(x, w['norm_mix_pre'], "rms_mix_pre")
    proj_a = _mm(h1, p['w_in_a'], 'nn', F32, "mm_in_a")
    proj_b = _mm(h1, p['w_in_b'], 'nn', F32, "mm_in_b")
    gather = _Exchange([late_shards[n] for n in LATE], False) if late_shards else None
    oa, gathered = _dila_fwd(proj_a, bias, gather)
    if late_shards:
        w.update(_assemble_late(gathered))
    p.update(_pad_weights(w))
    cq_n, ckv_n, kr = _mla_prep(proj_b, w['q_lat_norm'], w['kv_lat_norm'], tabs)
    q_lin = _mm(cq_n, p['w_uq'], 'nn', F32, "mm_uq")
    kv_lin = _mm(ckv_n, p['w_ukv'], 'nn', F32, "mm_ukv")
    qb, kb, vb, kt, vt = _mla_qkv(q_lin, kv_lin, kr, tabs)
    ob, lse_b = _mla_fwd(qb, kb, vt)
    cat = _outnorm_fwd(oa, ob, p['g_a'], p['g_b'])
    y = _mm(cat, p['w_o'], 'nn', F32, "mm_o")
    x1, h2 = _post_mix(x, y, w['norm_mix_post'], w['norm_ffn_pre'])
    u0 = _mm(h2, w['w_up'], 'nn', F32, "mm_up", sharded=True)
    ug, uv, a = _conv_fwd(u0, w['conv_w'], w['conv_b'])
    y2 = _mm(a, w['w_down'], 'nn', F32, "mm_down")
    dx2, dy2, dg_ffn_post, _, loss = _final(x1, y2, w['norm_ffn_post'], target)

    g = {'norm_ffn_post': dg_ffn_post}
    da = _mm(dy2, w['w_down'], 'nt', F32, "mm_down_dx")
    g['w_down'] = _mm(a, dy2, 'tn', BF16, "mm_down_dw")
    du0, g['conv_w'], g['conv_b'] = _conv_bwd(u0, ug, uv, da, w['conv_w'])
    dh2 = _mm(du0, w['w_up'], 'nt', F32, "mm_up_dx", sharded=True)
    g['w_up'] = _mm(h2, du0, 'tn', BF16, "mm_up_dw", sharded=True)
    dx1, g['norm_ffn_pre'] = _rms_bwd(x1, w['norm_ffn_pre'], [dh2], dx2, F32, "rms_ffn_pre_bwd")
    dy, g['norm_mix_post'] = _rms_bwd(y, w['norm_mix_post'], [dx1], None, BF16, "rms_mix_post_bwd")
    dcat = _mm(dy, p['w_o'], 'nt', F32, "mm_o_dx")
    g['w_o'] = _unpad_w_o(_mm(cat, dy, 'tn', F32, "mm_o_dw"))
    dpad = {}
    dopack_a, do_b, dpad['g_a'], dpad['g_b'] = _outnorm_bwd(oa, ob, p['g_a'], p['g_b'], dcat)

    scatter = None
    if late_shards:
        row_slots = lambda a: a.astype(BF16).reshape(N_CHIPS, a.shape[0] // N_CHIPS, a.shape[1])
        scatter = _Exchange([g['w_up'], row_slots(g['w_down']), row_slots(g['w_o'])], True)
    dq_b, dk_b, dv_b, received = _mla_bwd(qb, kb, vb, kt, do_b, ob, lse_b, scatter)
    if late_shards:
        g.update(zip(EARLY_GRADS, received))
    dq_pre, dkv, dkr = _mla_bwd_prep(dq_b, dk_b, dv_b, tabs)
    dcq_n = _mm(dq_pre, p['w_uq'], 'nt', F32, "mm_uq_dx")
    dpad['w_uq'] = _mm(cq_n, dq_pre, 'tn', F32, "mm_uq_dw")
    dckv_n = _mm(dkv, p['w_ukv'], 'nt', F32, "mm_ukv_dx")
    dpad['w_ukv'] = _mm(ckv_n, dkv, 'tn', F32, "mm_ukv_dw")
    dproj_b, g['q_lat_norm'], g['kv_lat_norm'] = _mla_norm_bwd(proj_b, dcq_n, dckv_n, dkr,
                                                               w['q_lat_norm'], w['kv_lat_norm'])

    dproj_a = _dila_bwd(proj_a, dopack_a, _slab_bias(DIL_Q_BWD))
    dh1 = _mm(dproj_b, p['w_in_b'], 'nt', F32, "mm_in_b_dx")
    dh1 = _mm(dproj_a, p['w_in_a'], 'nt', F32, "mm_in_a_dx", add=dh1)
    dpad['w_in_a'] = _mm(h1, dproj_a, 'tn', F32, "mm_in_a_dw")
    dpad['w_in_b'] = _mm(h1, dproj_b, 'tn', F32, "mm_in_b_dw")
    grad_x, g['norm_mix_pre'] = _rms_bwd(x, w['norm_mix_pre'], [dh1], dx1, F32, "rms_mix_pre_bwd")
    g.update(_unpad_grads(dpad))
    return loss, grad_x, g


MESH = pl.DeviceIdType.MESH
ANY = pl.BlockSpec(memory_space=pl.ANY)


def _place():
    x, y, c = lax.axis_index("x"), lax.axis_index("y"), lax.axis_index("c")
    chips = [(1 - x, y), (x, 1 - y), (1 - x, 1 - y)]
    return x, y, c, chips


class _Exchange:
    def __init__(self, arrays, scatter):
        self.arrays, self.scatter, self.n = list(arrays), scatter, len(arrays)

    def out_shapes(self):
        return [jax.ShapeDtypeStruct(a.shape if self.scatter else (N_CHIPS,) + a.shape, a.dtype) for a in self.arrays]

    def scratch(self):
        return [pltpu.SemaphoreType.DMA((3 * self.n,)), pltpu.SemaphoreType.DMA((3 * self.n,)),
                pltpu.SemaphoreType.DMA((self.n,))]

    def _copies(self, in_refs, out_refs, sems, arrivals):
        send_sems, recv_sems, local_sems = sems
        x, y, c, chips = _place()
        me = 2 * x + y
        src = lambda b, chip: in_refs[b].at[chip] if self.scatter else in_refs[b]
        local = [pltpu.make_async_copy(src(b, me), out_refs[b].at[me], local_sems.at[b]) for b in range(self.n)]
        sends, recvs = [], []
        for j, (px, py) in enumerate(chips):
            for b in range(self.n):
                k = j * self.n + b
                common = dict(send_sem=send_sems.at[k], recv_sem=recv_sems.at[k], device_id=(px, py, c),
                              device_id_type=MESH)
                sends.append(pltpu.make_async_remote_copy(src_ref=src(b, 2 * px + py), dst_ref=out_refs[b].at[me],
                                                          **common))
                if arrivals:
                    recvs.append(pltpu.make_async_remote_copy(src_ref=src(b, me), dst_ref=out_refs[b].at[2 * px + py],
                                                              **common))
        return local, sends, recvs

    def start(self, in_refs, out_refs, sems):
        local, sends, _ = self._copies(in_refs, out_refs, sems, False)
        for cp in local + sends:
            cp.start()

    def finish(self, in_refs, out_refs, sems):
        local, sends, recvs = self._copies(in_refs, out_refs, sems, True)
        for cp in recvs:
            cp.wait_recv()
        for cp in sends:
            cp.wait_send()
        for cp in local:
            cp.wait()


def _exchange_call(arrays, scatter, name):
    ex = _Exchange(arrays, scatter)
    n = ex.n

    def body(*refs):
        in_refs, out_refs, sems = refs[:n], refs[n:2 * n], refs[2 * n:]
        ex.start(in_refs, out_refs, sems)
        ex.finish(in_refs, out_refs, sems)

    return pl.pallas_call(body, name=name, in_specs=[ANY] * n, out_specs=[ANY] * n, out_shape=ex.out_shapes(),
                          scratch_shapes=ex.scratch())(*ex.arrays)


def _all_gather(bufs, name="gather_weights"):
    return _exchange_call(bufs, False, name)


def _scatter_grads(slots, name="scatter_grads"):
    return _exchange_call(slots, True, name)


ELEMENTWISE_BLOCK = 256 * 1024


def _row_tile(rows, cols):
    best = None
    for t in range(16, min(rows, max(16, ELEMENTWISE_BLOCK // cols)) + 1, 16):
        if rows % t == 0:
            best = t
    return best if best is not None else rows


def _sum_slots(recv, name):
    _, R, C = recv.shape
    tr = _row_tile(R, C)

    def body(r_ref, o_ref):
        f = lambda i: r_ref[i].astype(F32)
        o_ref[...] = ((f(0) + f(1)) + f(2)) + f(3)

    return pl.pallas_call(
        body, name="sum_" + name, grid=(R // tr,),
        in_specs=[pl.BlockSpec((N_CHIPS, tr, C), lambda i: (0, i, 0))],
        out_specs=pl.BlockSpec((tr, C), lambda i: (i, 0)),
        out_shape=jax.ShapeDtypeStruct((R, C), F32),
        compiler_params=_cparams(("parallel",)),
    )(recv)


def _swap_sibling(parts):
    n = len(parts)

    def body(*refs):
        p_refs, o_refs, send_sems, recv_sems = refs[:n], refs[n:2 * n], refs[2 * n], refs[2 * n + 1]
        x, y, c, _ = _place()
        cps = [pltpu.make_async_remote_copy(src_ref=p_refs[b], dst_ref=o_refs[b], send_sem=send_sems.at[b],
                                            recv_sem=recv_sems.at[b], device_id=(x, y, 1 - c), device_id_type=MESH)
               for b in range(n)]
        for cp in cps:
            cp.start()
        for cp in cps:
            cp.wait()

    return pl.pallas_call(
        body, name="swap_sibling", in_specs=[ANY] * n, out_specs=[ANY] * n,
        out_shape=[jax.ShapeDtypeStruct(p.shape, p.dtype) for p in parts],
        scratch_shapes=[pltpu.SemaphoreType.DMA((n,)), pltpu.SemaphoreType.DMA((n,))],
    )(*parts)


def _adamw(g0, g1, w, m, v, name, offset=0):
    R, C = w.shape
    tr = _row_tile(R, C)
    packed = g0.shape != w.shape
    bc1 = 1.0 - ADAM_B1 ** ADAM_STEP
    bc2 = 1.0 - ADAM_B2 ** ADAM_STEP

    def body(g0_ref, g1_ref, w_ref, m_ref, v_ref, g_ref, d_ref, nm_ref, nv_ref):
        if packed:
            g = g0_ref[:, offset:offset + C] + g1_ref[:, offset:offset + C]
        else:
            g = g0_ref[...] + g1_ref[...]
        g_ref[...] = g
        nm = ADAM_B1 * m_ref[...] + (1.0 - ADAM_B1) * g
        nv = ADAM_B2 * v_ref[...] + (1.0 - ADAM_B2) * (g * g)
        nm_ref[...] = nm
        nv_ref[...] = nv
        d_ref[...] = -ADAM_LR * ((nm / bc1) / (jnp.sqrt(nv / bc2) + ADAM_EPS) + ADAM_WD * w_ref[...])

    spec = pl.BlockSpec((tr, C), lambda i: (i, 0))
    gspec = pl.BlockSpec(g0.shape, lambda i: (0, 0)) if packed else spec
    out = jax.ShapeDtypeStruct((R, C), F32)
    return pl.pallas_call(
        body, name="adamw_" + name, grid=(R // tr,), in_specs=[gspec, gspec, spec, spec, spec],
        out_specs=[spec] * 4, out_shape=[out] * 4, compiler_params=_cparams(("parallel",)),
    )(g0, g1, w, m, v)


def kernel(x, norm_mix_pre, w_in, q_lat_norm, w_uq, kv_lat_norm, w_ukv, out_norm_a, out_norm_b, w_o, norm_mix_post, norm_ffn_pre, w_up, conv_w, conv_b, w_down, norm_ffn_post, loss_target, m_norm_mix_pre, m_w_in, m_q_lat_norm, m_w_uq, m_kv_lat_norm, m_w_ukv, m_out_norm_a, m_out_norm_b, m_w_o, m_norm_mix_post, m_norm_ffn_pre, m_w_up, m_conv_w, m_conv_b, m_w_down, m_norm_ffn_post, v_norm_mix_pre, v_w_in, v_q_lat_norm, v_w_uq, v_kv_lat_norm, v_w_ukv, v_out_norm_a, v_out_norm_b, v_w_o, v_norm_mix_post, v_norm_ffn_pre, v_w_up, v_conv_w, v_conv_b, v_w_down, v_norm_ffn_post):
    args = dict(locals())
    strip = lambda a: a[0] if a.ndim == 3 else a
    wl = {n: strip(args[n]) for n in WEIGHTS}
    ml = {n: strip(args['m_' + n]) for n in WEIGHTS}
    vl = {n: strip(args['v_' + n]) for n in WEIGHTS}

    gathered = _all_gather([wl['w_in'].astype(BF16), wl['conv_w']])
    full = {n: wl[n] for n in SMALL}
    for n, a in zip(('w_in', 'conv_w'), gathered):
        full[n] = jnp.concatenate([a[i] for i in range(N_CHIPS)], axis=1)

    loss_b, grad_x, g = _local_step(x[0], loss_target[0], full, {n: wl[n].astype(BF16) for n in LATE})

    sharded = [n for n in WEIGHTS if SHARD_AXIS[n] is not None]
    late = [n for n in sharded if n not in EARLY_GRADS]

    def slots_of(n):
        a = g[n].astype(BF16)
        cols = a.shape[1] // N_CHIPS
        return jnp.stack([a[:, i * cols:(i + 1) * cols] for i in range(N_CHIPS)])

    small_pack = jnp.concatenate([g[n] for n in SMALL], axis=1)
    slots = [slots_of(n) for n in late] + [jnp.broadcast_to(small_pack[None], (N_CHIPS,) + small_pack.shape)]
    recv = dict(zip(late + ['small'], _scatter_grads(slots)))
    recv.update({n: g[n] for n in EARLY_GRADS})
    parts = [_sum_slots(recv[n], n) for n in sharded + ['small']]
    others = _swap_sibling(parts)

    outs = {}

    def record(n, results):
        for tag, a in zip(('grad', 'delta', 'new_m', 'new_v'), results):
            outs[tag + '_' + n] = a.reshape(args[n].shape)

    for n, p0, p1 in zip(sharded, parts, others):
        record(n, _adamw(p0, p1, wl[n], ml[n], vl[n], n))
    offset = 0
    for n in SMALL:
        record(n, _adamw(parts[-1], others[-1], wl[n], ml[n], vl[n], n, offset=offset))
        offset += wl[n].shape[1]

    loss = lax.psum(loss_b[0, 0], ("x", "y", "c"))
    return (loss, grad_x[None], *[outs['grad_' + n] for n in WEIGHTS], *[outs['delta_' + n] for n in WEIGHTS],
            *[outs['new_m_' + n] for n in WEIGHTS], *[outs['new_v_' + n] for n in WEIGHTS])
```
